```python
import jax, jax.numpy as jnp
from jax import lax
import numpy as np

D_MODEL = 1024
BATCH = 2
SEQ = 8192
DEPTH = 1

HEAD_DIM = 64
N_Q_HEADS = 8
N_KV_HEADS = 2
Q_REP = N_Q_HEADS // N_KV_HEADS
ATTN_WIDTH = N_Q_HEADS * HEAD_DIM
KV_WIDTH = N_KV_HEADS * HEAD_DIM
WINDOW = 128
ATTN_BLOCK = 128
N_GMLP_GROUPS = 8
GMLP_WIDTH = D_MODEL - ATTN_WIDTH
GMLP_GROUP_DIM = GMLP_WIDTH // N_GMLP_GROUPS
CHUNK = 128
MIX_WIDTH = ATTN_WIDTH + GMLP_WIDTH
IN_WIDTH = ATTN_WIDTH + 2 * KV_WIDTH + 2 * GMLP_WIDTH
N_EXPERTS = 32
TOP_K = 4
D_FF = D_MODEL
SWIGLU_LIMIT = 7.0
SWIGLU_ALPHA = 1.702
MOE_BLOCK = 128
LN_EPS = 1e-5
DEEPNORM_ALPHA = (2.0 * DEPTH) ** 0.25
DEEPNORM_BETA = (8.0 * DEPTH) ** -0.25
NEG_INF = -1e30

kernel_name = "hybrid_swa_sink_gmlp_moe_deepnorm"


def layer_norm(x, g, b):
    xf = x.astype(jnp.float32)
    mu = jnp.mean(xf, axis=-1, keepdims=True)
    xc = xf - mu
    var = jnp.mean(xc * xc, axis=-1, keepdims=True)
    return (xc * lax.rsqrt(var + LN_EPS)).astype(x.dtype) * g + b


def sliding_window_attention(q, k, v, sinks):
    B, S = q.shape[:2]
    nb = S // ATTN_BLOCK
    q = q.reshape(B, nb, ATTN_BLOCK, N_KV_HEADS, Q_REP, HEAD_DIM)
    k = k.reshape(B, nb, ATTN_BLOCK, N_KV_HEADS, HEAD_DIM)
    v = v.reshape(B, nb, ATTN_BLOCK, N_KV_HEADS, HEAD_DIM)

    def band(t):
        prev = jnp.pad(t[:, :-1], ((0, 0), (1, 0), (0, 0), (0, 0), (0, 0)))
        return jnp.concatenate([prev, t], axis=2)

    kb, vb = band(k), band(v)
    scores = jnp.einsum('bnqgrd,bnkgd->bngrqk', q, kb).astype(jnp.float32) * (HEAD_DIM ** -0.5)
    t_idx = jnp.arange(ATTN_BLOCK)[:, None]
    s_idx = jnp.arange(2 * ATTN_BLOCK)[None, :]
    diff = t_idx + ATTN_BLOCK - s_idx
    in_band = (diff >= 0) & (diff < WINDOW)
    blk = jnp.arange(nb)[:, None, None]
    valid = in_band[None] & ((blk * ATTN_BLOCK - ATTN_BLOCK + s_idx[None]) >= 0)
    scores = jnp.where(valid[None, :, None, None], scores, NEG_INF)
    sink = sinks.astype(jnp.float32).reshape(N_KV_HEADS, Q_REP)[None, None, :, :, None, None]
    sink = jnp.broadcast_to(sink, scores.shape[:-1] + (1,))
    probs = jax.nn.softmax(jnp.concatenate([scores, sink], axis=-1), axis=-1)[..., :-1]
    out = jnp.einsum('bngrqk,bnkgd->bnqgrd', probs.astype(v.dtype), vb)
    return out.reshape(B, S, ATTN_WIDTH)


def chunked_spatial_gating(u, g, ln_g, ln_b, w_s, b_s):
    B, S = u.shape[:2]
    nc = S // CHUNK
    u = jax.nn.gelu(u)
    g = jax.nn.gelu(g).reshape(B, S, N_GMLP_GROUPS, GMLP_GROUP_DIM)
    g = layer_norm(g, ln_g.reshape(N_GMLP_GROUPS, GMLP_GROUP_DIM), ln_b.reshape(N_GMLP_GROUPS, GMLP_GROUP_DIM))
    g = g.reshape(B, nc, CHUNK, N_GMLP_GROUPS, GMLP_GROUP_DIM)
    causal = jnp.tril(jnp.ones((CHUNK, CHUNK), dtype=bool))
    w = jnp.where(causal[None], w_s, jnp.zeros_like(w_s))
    mixed = jnp.einsum('gts,bcsgd->bctgd', w, g) + b_s.T[None, None, :, :, None]
    return u * mixed.reshape(B, S, GMLP_WIDTH)


def token_mixer(x, w_in, b_in, sinks, ln_v_g, ln_v_b, w_spatial, b_spatial, w_out, b_out):
    proj = x @ w_in + b_in
    o1 = ATTN_WIDTH
    o2 = o1 + KV_WIDTH
    o3 = o2 + KV_WIDTH
    o4 = o3 + GMLP_WIDTH
    q, k, v = proj[..., :o1], proj[..., o1:o2], proj[..., o2:o3]
    u, g = proj[..., o3:o4], proj[..., o4:]
    attn = sliding_window_attention(q, k, v, sinks)
    sgu = chunked_spatial_gating(u, g, ln_v_g, ln_v_b, w_spatial, b_spatial)
    return jnp.concatenate([attn, sgu], axis=-1) @ w_out + b_out


def routed_experts(x, w_router, b_router, w_gate, b_gate, w_up, b_up, w_down, b_down):
    B, S, D = x.shape
    T = B * S
    TK = T * TOP_K
    xt = x.reshape(T, D)
    logits = (xt @ w_router + b_router).astype(jnp.float32)
    top_vals, top_idx = lax.top_k(logits, TOP_K)
    gates = jax.nn.softmax(top_vals, axis=-1).astype(x.dtype)
    flat_e = top_idx.reshape(-1)
    flat_tok = jnp.arange(TK, dtype=jnp.int32) // TOP_K
    order = jnp.argsort(flat_e)
    sorted_e = flat_e[order]
    counts = jnp.bincount(flat_e, length=N_EXPERTS)
    start = jnp.cumsum(counts) - counts
    padded = (counts + MOE_BLOCK - 1) // MOE_BLOCK * MOE_BLOCK
    padded_end = jnp.cumsum(padded)
    padded_start = padded_end - padded
    dest = padded_start[sorted_e] + jnp.arange(TK, dtype=jnp.int32) - start[sorted_e]
    n_blocks = TK // MOE_BLOCK + N_EXPERTS
    n_rows = n_blocks * MOE_BLOCK
    row_tok = jnp.full((n_rows,), T, dtype=jnp.int32).at[dest].set(flat_tok[order])
    row_gate = jnp.zeros((n_rows,), x.dtype).at[dest].set(gates.reshape(-1)[order])
    block_e = jnp.minimum(
        jnp.searchsorted(padded_end, jnp.arange(n_blocks, dtype=padded_end.dtype) * MOE_BLOCK, side='right'),
        N_EXPERTS - 1)
    x_rows = jnp.concatenate([xt, jnp.zeros((1, D), xt.dtype)], axis=0)[row_tok]
    x_rows = x_rows.reshape(n_blocks, MOE_BLOCK, D)

    def expert_block(args):
        xb, e = args
        gt = jnp.minimum(xb @ w_gate[e] + b_gate[e], SWIGLU_LIMIT)
        up = jnp.clip(xb @ w_up[e] + b_up[e], -SWIGLU_LIMIT, SWIGLU_LIMIT)
        hid = gt * jax.nn.sigmoid(SWIGLU_ALPHA * gt) * (up + 1.0)
        return hid @ w_down[e] + b_down[e]

    y_rows = lax.map(expert_block, (x_rows, block_e)).reshape(n_rows, D)
    out = jnp.zeros((T + 1, D), x.dtype).at[row_tok].add(y_rows * row_gate[:, None])[:T]
    return out.reshape(B, S, D)


def setup_inputs(seed: int = 0) -> dict:
    key = jax.random.key(seed)
    ks = jax.random.split(key, 24)
    f32 = jnp.float32

    def nrm(k, shape, scale):
        return jax.random.normal(k, (DEPTH,) + shape, f32) * scale

    x = jax.random.normal(ks[0], (BATCH, SEQ, D_MODEL), f32)
    col_scale = jnp.concatenate([
        jnp.ones((ATTN_WIDTH + KV_WIDTH,), f32),
        jnp.full((KV_WIDTH,), DEEPNORM_BETA, f32),
        jnp.full((GMLP_WIDTH,), DEEPNORM_BETA, f32),
        jnp.ones((GMLP_WIDTH,), f32)])
    w_in = nrm(ks[1], (D_MODEL, IN_WIDTH), D_MODEL ** -0.5) * col_scale
    b_in = nrm(ks[2], (IN_WIDTH,), 0.02)
    sinks = nrm(ks[3], (N_Q_HEADS,), 0.5)
    ln_v_g = 1.0 + nrm(ks[4], (GMLP_WIDTH,), 0.05)
    ln_v_b = nrm(ks[5], (GMLP_WIDTH,), 0.02)
    w_spatial = nrm(ks[6], (N_GMLP_GROUPS, CHUNK, CHUNK), CHUNK ** -0.5)
    b_spatial = 1.0 + nrm(ks[7], (N_GMLP_GROUPS, CHUNK), 0.05)
    w_out = nrm(ks[8], (MIX_WIDTH, D_MODEL), MIX_WIDTH ** -0.5 * DEEPNORM_BETA)
    b_out = nrm(ks[9], (D_MODEL,), 0.02)
    ln1_g = 1.0 + nrm(ks[10], (D_MODEL,), 0.05)
    ln1_b = nrm(ks[11], (D_MODEL,), 0.02)
    w_router = nrm(ks[12], (D_MODEL, N_EXPERTS), D_MODEL ** -0.5)
    b_router = nrm(ks[13], (N_EXPERTS,), 0.01)
    w_gate = nrm(ks[14], (N_EXPERTS, D_MODEL, D_FF), D_MODEL ** -0.5)
    b_gate = nrm(ks[15], (N_EXPERTS, D_FF), 0.02)
    w_up = nrm(ks[16], (N_EXPERTS, D_MODEL, D_FF), D_MODEL ** -0.5 * DEEPNORM_BETA)
    b_up = nrm(ks[17], (N_EXPERTS, D_FF), 0.02)
    w_down = nrm(ks[18], (N_EXPERTS, D_FF, D_MODEL), D_FF ** -0.5 * DEEPNORM_BETA)
    b_down = nrm(ks[19], (N_EXPERTS, D_MODEL), 0.02)
    ln2_g = 1.0 + nrm(ks[20], (D_MODEL,), 0.05)
    ln2_b = nrm(ks[21], (D_MODEL,), 0.02)
    return {"x": x, "w_in": w_in, "b_in": b_in, "sinks": sinks, "ln_v_g": ln_v_g, "ln_v_b": ln_v_b,
            "w_spatial": w_spatial, "b_spatial": b_spatial, "w_out": w_out, "b_out": b_out,
            "ln1_g": ln1_g, "ln1_b": ln1_b, "w_router": w_router, "b_router": b_router,
            "w_gate": w_gate, "b_gate": b_gate, "w_up": w_up, "b_up": b_up,
            "w_down": w_down, "b_down": b_down, "ln2_g": ln2_g, "ln2_b": ln2_b}


def reference(x, w_in, b_in, sinks, ln_v_g, ln_v_b, w_spatial, b_spatial, w_out, b_out,
              ln1_g, ln1_b, w_router, b_router, w_gate, b_gate, w_up, b_up,
              w_down, b_down, ln2_g, ln2_b):
    for l in range(DEPTH):
        mix = token_mixer(x, w_in[l], b_in[l], sinks[l], ln_v_g[l], ln_v_b[l],
                          w_spatial[l], b_spatial[l], w_out[l], b_out[l])
        h = layer_norm(DEEPNORM_ALPHA * x + mix, ln1_g[l], ln1_b[l])
        ffn = routed_experts(h, w_router[l], b_router[l], w_gate[l], b_gate[l],
                             w_up[l], b_up[l], w_down[l], b_down[l])
        x = layer_norm(DEEPNORM_ALPHA * h + ffn, ln2_g[l], ln2_b[l])
    return x
```

```python
import functools

import jax
import jax.numpy as jnp
from jax import lax
from jax.experimental import pallas as pl
from jax.experimental.pallas import tpu as pltpu

D_MODEL = 1024
HEAD_DIM = 64
N_Q_HEADS = 8
N_KV_HEADS = 2
Q_REP = N_Q_HEADS // N_KV_HEADS
ATTN_WIDTH = N_Q_HEADS * HEAD_DIM
KV_WIDTH = N_KV_HEADS * HEAD_DIM
ATTN_BLOCK = 128
N_GMLP_GROUPS = 8
GMLP_WIDTH = D_MODEL - ATTN_WIDTH
GMLP_GROUP_DIM = GMLP_WIDTH // N_GMLP_GROUPS
IN_WIDTH = ATTN_WIDTH + 2 * KV_WIDTH + 2 * GMLP_WIDTH
N_EXPERTS = 32
TOP_K = 4
SWIGLU_LIMIT = 7.0
SWIGLU_ALPHA = 1.702
LN_EPS = 1e-5
DEPTH = 1
DEEPNORM_ALPHA = (2.0 * DEPTH) ** 0.25
NEG_INF = -1e30

MIXER_ROWS = 256
EXPERT_ROWS = 256
GATHER_ROWS = 2048
COMBINE_ROWS = 256
VMEM_LIMIT_BYTES = 56 * 1024 * 1024

_O_K = ATTN_WIDTH
_O_V = _O_K + KV_WIDTH
_O_U = _O_V + KV_WIDTH
_O_G = _O_U + GMLP_WIDTH


def _layer_norm(v, g, b):
    mu = jnp.mean(v, axis=-1, keepdims=True)
    vc = v - mu
    var = jnp.mean(vc * vc, axis=-1, keepdims=True)
    return vc * lax.rsqrt(var + LN_EPS) * g + b


def _attention_block(q, kb, vb, sinks_ref, first_block):
    nrow = Q_REP * ATTN_BLOCK
    t_idx = lax.broadcasted_iota(jnp.int32, (nrow, 2 * ATTN_BLOCK), 0) % ATTN_BLOCK
    s_idx = lax.broadcasted_iota(jnp.int32, (nrow, 2 * ATTN_BLOCK), 1)
    diff = t_idx + ATTN_BLOCK - s_idx
    valid = (diff >= 0) & (diff < ATTN_BLOCK) & (s_idx >= first_block * ATTN_BLOCK)
    head_of_row = lax.broadcasted_iota(jnp.int32, (nrow, 1), 0) // ATTN_BLOCK
    outs = []
    for g in range(N_KV_HEADS):
        kg = kb[:, g * HEAD_DIM:(g + 1) * HEAD_DIM]
        vg = vb[:, g * HEAD_DIM:(g + 1) * HEAD_DIM]
        qg = jnp.concatenate(
            [q[:, (g * Q_REP + r) * HEAD_DIM:(g * Q_REP + r + 1) * HEAD_DIM] for r in range(Q_REP)],
            axis=0).astype(jnp.bfloat16)
        s = lax.dot_general(qg, kg, (((1,), (1,)), ((), ())), preferred_element_type=jnp.float32)
        s = jnp.where(valid, s, NEG_INF)
        sink = jnp.zeros((nrow, 1), jnp.float32)
        for r in range(Q_REP):
            sink = jnp.where(head_of_row == r, sinks_ref[g * Q_REP + r], sink)
        m = jnp.maximum(jnp.max(s, axis=-1, keepdims=True), sink)
        p = jnp.exp(s - m)
        denom = jnp.sum(p, axis=-1, keepdims=True) + jnp.exp(sink - m)
        o = jnp.dot(p.astype(jnp.bfloat16), vg, preferred_element_type=jnp.float32) / denom
        outs.extend(o[r * ATTN_BLOCK:(r + 1) * ATTN_BLOCK] for r in range(Q_REP))
    return jnp.concatenate(outs, axis=-1)


def _mixer_kernel(sinks_ref, x_ref, w_in_ref, b_in_ref, lnv_g_ref, lnv_b_ref, grp_avg_ref,
                  w_sp_ref, b_sp_ref, w_out_ref, b_out_ref, ln1_g_ref, ln1_b_ref,
                  w_r_ref, b_r_ref, tri_ref,
                  h_ref, idx_ref, gate_ref, rank_ref, count_ref,
                  kv_prev_ref, *, steps_per_seq):
    i = pl.program_id(0)
    first_step = (i % steps_per_seq) == 0
    tm = x_ref.shape[0]
    n_sub = tm // ATTN_BLOCK

    @pl.when(i == 0)
    def _():
        count_ref[...] = jnp.zeros_like(count_ref)

    @pl.when(first_step)
    def _():
        kv_prev_ref[...] = jnp.zeros_like(kv_prev_ref)

    x = x_ref[...]
    proj = jnp.dot(x.astype(jnp.bfloat16), w_in_ref[...], preferred_element_type=jnp.float32) + b_in_ref[...]

    q_all = proj[:, :_O_K] * (HEAD_DIM ** -0.5)
    k_all = proj[:, _O_K:_O_V].astype(jnp.bfloat16)
    v_all = proj[:, _O_V:_O_U].astype(jnp.bfloat16)
    k_prev = kv_prev_ref[:, :KV_WIDTH]
    v_prev = kv_prev_ref[:, KV_WIDTH:]
    attn_blocks = []
    for sb in range(n_sub):
        rows = slice(sb * ATTN_BLOCK, (sb + 1) * ATTN_BLOCK)
        k_cur, v_cur = k_all[rows], v_all[rows]
        kb = jnp.concatenate([k_prev, k_cur], axis=0)
        vb = jnp.concatenate([v_prev, v_cur], axis=0)
        first_block = jnp.where(first_step, 1, 0) if sb == 0 else 0
        attn_blocks.append(_attention_block(q_all[rows], kb, vb, sinks_ref, first_block))
        k_prev, v_prev = k_cur, v_cur
    kv_prev_ref[:, :KV_WIDTH] = k_prev
    kv_prev_ref[:, KV_WIDTH:] = v_prev
    attn = jnp.concatenate(attn_blocks, axis=0)

    u = jax.nn.gelu(proj[:, _O_U:_O_G])
    gg = jax.nn.gelu(proj[:, _O_G:])
    avg = grp_avg_ref[...]
    mu = jnp.dot(gg.astype(jnp.bfloat16), avg, preferred_element_type=jnp.float32)
    gc = gg - mu
    var = jnp.dot((gc * gc).astype(jnp.bfloat16), avg, preferred_element_type=jnp.float32)
    gn = (gc * lax.rsqrt(var + LN_EPS) * lnv_g_ref[...] + lnv_b_ref[...]).astype(jnp.bfloat16)
    causal = (lax.broadcasted_iota(jnp.int32, (ATTN_BLOCK, ATTN_BLOCK), 0)
              >= lax.broadcasted_iota(jnp.int32, (ATTN_BLOCK, ATTN_BLOCK), 1))
    w_sp = [jnp.where(causal, w_sp_ref[g], 0.0).astype(jnp.bfloat16) for g in range(N_GMLP_GROUPS)]
    mixed_chunks = []
    for c in range(n_sub):
        rows = slice(c * ATTN_BLOCK, (c + 1) * ATTN_BLOCK)
        pieces = [
            jnp.dot(w_sp[g], gn[rows, g * GMLP_GROUP_DIM:(g + 1) * GMLP_GROUP_DIM],
                    preferred_element_type=jnp.float32)
            for g in range(N_GMLP_GROUPS)]
        mixed_chunks.append(jnp.concatenate(pieces, axis=-1) + b_sp_ref[...])
    sgu = u * jnp.concatenate(mixed_chunks, axis=0)

    mix = (jnp.dot(attn.astype(jnp.bfloat16), w_out_ref[:ATTN_WIDTH, :], preferred_element_type=jnp.float32)
           + jnp.dot(sgu.astype(jnp.bfloat16), w_out_ref[ATTN_WIDTH:, :], preferred_element_type=jnp.float32)
           + b_out_ref[...])
    h = _layer_norm(DEEPNORM_ALPHA * x + mix, ln1_g_ref[...], ln1_b_ref[...])
    h_ref[...] = h

    h_hi = h.astype(jnp.bfloat16)
    h_lo = (h - h_hi.astype(jnp.float32)).astype(jnp.bfloat16)
    part = jnp.dot(h_hi, w_r_ref[...], preferred_element_type=jnp.float32)
    logits = (part[:, :N_EXPERTS] + part[:, N_EXPERTS:]
              + jnp.dot(h_lo, w_r_ref[:, :N_EXPERTS], preferred_element_type=jnp.float32)
              + b_r_ref[...])
    lane = lax.broadcasted_iota(jnp.int32, (tm, N_EXPERTS), 1).astype(jnp.float32)
    work = logits
    vals, idxs = [], []
    onehot = jnp.zeros((tm, N_EXPERTS), jnp.float32)
    for _ in range(TOP_K):
        m = jnp.max(work, axis=-1, keepdims=True)
        idx = jnp.min(jnp.where(work == m, lane, float(N_EXPERTS)), axis=-1, keepdims=True)
        sel = lane == idx
        vals.append(m)
        idxs.append(idx)
        onehot = jnp.where(sel, 1.0, onehot)
        work = jnp.where(sel, -jnp.inf, work)
    exps = [jnp.exp(v - vals[0]) for v in vals]
    denom = exps[0] + exps[1] + exps[2] + exps[3]
    before = jnp.dot(tri_ref[...], onehot.astype(jnp.bfloat16), preferred_element_type=jnp.float32) + count_ref[...]
    for k in range(TOP_K):
        idx_ref[:, k:k + 1] = idxs[k].astype(jnp.int32)
        gate_ref[:, k:k + 1] = exps[k] / denom
        rank_ref[:, k:k + 1] = jnp.sum(jnp.where(lane == idxs[k], before, 0.0), axis=-1,
                                       keepdims=True).astype(jnp.int32)
    count_ref[...] += jnp.sum(onehot, axis=0, keepdims=True)


def _mixer(x2, sinks, w_in, b_in, lnv_g, lnv_b, w_sp, b_sp_full, w_out, b_out, ln1_g, ln1_b,
           w_r_cat, b_r, seq_len):
    t = x2.shape[0]
    tm = MIXER_ROWS
    grp = jnp.arange(GMLP_WIDTH) // GMLP_GROUP_DIM
    grp_avg = jnp.where(grp[:, None] == grp[None, :], 1.0 / GMLP_GROUP_DIM, 0.0).astype(jnp.bfloat16)
    tri = (jnp.arange(tm)[:, None] > jnp.arange(tm)[None, :]).astype(jnp.bfloat16)

    def full(shape):
        return pl.BlockSpec(shape, lambda i: (0,) * len(shape))

    return pl.pallas_call(
        functools.partial(_mixer_kernel, steps_per_seq=seq_len // tm),
        grid=(t // tm,),
        in_specs=[
            pl.BlockSpec(memory_space=pltpu.SMEM),
            pl.BlockSpec((tm, D_MODEL), lambda i: (i, 0)),
            full((D_MODEL, IN_WIDTH)), full((1, IN_WIDTH)),
            full((1, GMLP_WIDTH)), full((1, GMLP_WIDTH)), full((GMLP_WIDTH, GMLP_WIDTH)),
            full((N_GMLP_GROUPS, ATTN_BLOCK, ATTN_BLOCK)), full((ATTN_BLOCK, GMLP_WIDTH)),
            full((D_MODEL, D_MODEL)), full((1, D_MODEL)), full((1, D_MODEL)), full((1, D_MODEL)),
            full((D_MODEL, 2 * N_EXPERTS)), full((1, N_EXPERTS)), full((tm, tm)),
        ],
        out_specs=[
            pl.BlockSpec((tm, D_MODEL), lambda i: (i, 0)),
            pl.BlockSpec((tm, TOP_K), lambda i: (i, 0)),
            pl.BlockSpec((tm, TOP_K), lambda i: (i, 0)),
            pl.BlockSpec((tm, TOP_K), lambda i: (i, 0)),
            pl.BlockSpec((1, N_EXPERTS), lambda i: (0, 0)),
        ],
        out_shape=[
            jax.ShapeDtypeStruct((t, D_MODEL), jnp.float32),
            jax.ShapeDtypeStruct((t, TOP_K), jnp.int32),
            jax.ShapeDtypeStruct((t, TOP_K), jnp.float32),
            jax.ShapeDtypeStruct((t, TOP_K), jnp.int32),
            jax.ShapeDtypeStruct((1, N_EXPERTS), jnp.float32),
        ],
        scratch_shapes=[pltpu.VMEM((ATTN_BLOCK, 2 * KV_WIDTH), jnp.bfloat16)],
        compiler_params=pltpu.CompilerParams(
            dimension_semantics=("arbitrary",), vmem_limit_bytes=VMEM_LIMIT_BYTES),
        name="mixer",
    )(sinks, x2, w_in, b_in, lnv_g, lnv_b, grp_avg, w_sp, b_sp_full, w_out, b_out, ln1_g, ln1_b,
      w_r_cat, b_r, tri)


def _row_copy(src_ref, dst_ref, sem, src_row, dst_row):
    return pltpu.make_async_copy(src_ref.at[pl.ds(src_row, 1)], dst_ref.at[pl.ds(dst_row, 1)], sem)


def _gather_kernel(idx_ref, src_ref, dst_ref, sem):
    base = pl.program_id(0) * GATHER_ROWS

    def issue(r, carry):
        _row_copy(src_ref, dst_ref, sem, idx_ref[base + r], base + r).start()
        return carry

    lax.fori_loop(0, GATHER_ROWS, issue, 0, unroll=8)

    def drain(r, carry):
        _row_copy(src_ref, dst_ref, sem, 0, base + r).wait()
        return carry

    lax.fori_loop(0, GATHER_ROWS, drain, 0, unroll=8)


def _gather_rows(idx, src):
    n = idx.shape[0]
    return pl.pallas_call(
        _gather_kernel,
        grid_spec=pltpu.PrefetchScalarGridSpec(
            num_scalar_prefetch=1,
            grid=(n // GATHER_ROWS,),
            in_specs=[pl.BlockSpec(memory_space=pl.ANY)],
            out_specs=pl.BlockSpec(memory_space=pl.ANY),
            scratch_shapes=[pltpu.SemaphoreType.DMA],
        ),
        out_shape=jax.ShapeDtypeStruct((n, src.shape[1]), src.dtype),
        compiler_params=pltpu.CompilerParams(dimension_semantics=("arbitrary",)),
        name="gather_rows",
    )(idx, src)


def _expert_kernel(block_e_ref, n_used_ref, x_ref, wg_ref, bg_ref, wu_ref, bu_ref, wd_ref, bd_ref,
                   y_ref, wg_bf, wu_bf, wd_bf):
    i = pl.program_id(0)
    changed = jnp.logical_or(i == 0, block_e_ref[i] != block_e_ref[jnp.maximum(i - 1, 0)])

    @pl.when(changed)
    def _():
        wg_bf[...] = wg_ref[0].astype(jnp.bfloat16)
        wu_bf[...] = wu_ref[0].astype(jnp.bfloat16)
        wd_bf[...] = wd_ref[0].astype(jnp.bfloat16)

    @pl.when(i < n_used_ref[0])
    def _():
        xb = x_ref[...].astype(jnp.bfloat16)
        gt = jnp.minimum(jnp.dot(xb, wg_bf[...], preferred_element_type=jnp.float32) + bg_ref[0], SWIGLU_LIMIT)
        up = jnp.clip(jnp.dot(xb, wu_bf[...], preferred_element_type=jnp.float32) + bu_ref[0],
                      -SWIGLU_LIMIT, SWIGLU_LIMIT)
        hid = gt * jax.nn.sigmoid(SWIGLU_ALPHA * gt) * (up + 1.0)
        y_ref[...] = jnp.dot(hid.astype(jnp.bfloat16), wd_bf[...], preferred_element_type=jnp.float32) + bd_ref[0]

    @pl.when(i >= n_used_ref[0])
    def _():
        y_ref[...] = jnp.zeros_like(y_ref)


def _experts(block_e, n_used, x_rows, w_gate, b_gate, w_up, b_up, w_down, b_down):
    n_rows = x_rows.shape[0]
    bm = EXPERT_ROWS
    d_ff = w_gate.shape[2]

    def by_expert(shape):
        return pl.BlockSpec(shape, lambda i, be, nu: (be[i],) + (0,) * (len(shape) - 1))

    return pl.pallas_call(
        _expert_kernel,
        grid_spec=pltpu.PrefetchScalarGridSpec(
            num_scalar_prefetch=2,
            grid=(n_rows // bm,),
            in_specs=[
                pl.BlockSpec((bm, D_MODEL), lambda i, be, nu: (i, 0)),
                by_expert((1, D_MODEL, d_ff)), by_expert((1, 1, d_ff)),
                by_expert((1, D_MODEL, d_ff)), by_expert((1, 1, d_ff)),
                by_expert((1, d_ff, D_MODEL)), by_expert((1, 1, D_MODEL)),
            ],
            out_specs=pl.BlockSpec((bm, D_MODEL), lambda i, be, nu: (i, 0)),
            scratch_shapes=[
                pltpu.VMEM((D_MODEL, d_ff), jnp.bfloat16),
                pltpu.VMEM((D_MODEL, d_ff), jnp.bfloat16),
                pltpu.VMEM((d_ff, D_MODEL), jnp.bfloat16),
            ],
        ),
        out_shape=jax.ShapeDtypeStruct((n_rows, D_MODEL), jnp.float32),
        compiler_params=pltpu.CompilerParams(
            dimension_semantics=("arbitrary",), vmem_limit_bytes=VMEM_LIMIT_BYTES),
        name="experts",
    )(block_e, n_used, x_rows, w_gate, b_gate.reshape(N_EXPERTS, 1, d_ff), w_up,
      b_up.reshape(N_EXPERTS, 1, d_ff), w_down, b_down.reshape(N_EXPERTS, 1, D_MODEL))


def _combine_kernel(h_ref, y_ref, gate_ref, g_ref, b_ref, o_ref):
    ffn = gate_ref[:, 0:1] * y_ref[0]
    for k in range(1, TOP_K):
        ffn = ffn + gate_ref[:, k:k + 1] * y_ref[k]
    o_ref[...] = _layer_norm(DEEPNORM_ALPHA * h_ref[...] + ffn, g_ref[...], b_ref[...])


def _combine(h, y_tok, gates, ln2_g, ln2_b):
    t = h.shape[0]
    tm = COMBINE_ROWS
    return pl.pallas_call(
        _combine_kernel,
        grid=(t // tm,),
        in_specs=[
            pl.BlockSpec((tm, D_MODEL), lambda i: (i, 0)),
            pl.BlockSpec((TOP_K, tm, D_MODEL), lambda i: (0, i, 0)),
            pl.BlockSpec((tm, TOP_K), lambda i: (i, 0)),
            pl.BlockSpec((1, D_MODEL), lambda i: (0, 0)),
            pl.BlockSpec((1, D_MODEL), lambda i: (0, 0)),
        ],
        out_specs=pl.BlockSpec((tm, D_MODEL), lambda i: (i, 0)),
        out_shape=jax.ShapeDtypeStruct((t, D_MODEL), jnp.float32),
        compiler_params=pltpu.CompilerParams(
            dimension_semantics=("arbitrary",), vmem_limit_bytes=VMEM_LIMIT_BYTES),
        name="combine",
    )(h, y_tok, gates, ln2_g, ln2_b)


def _layer(x2, seq_len, w_in, b_in, sinks, ln_v_g, ln_v_b, w_spatial, b_spatial, w_out, b_out,
           ln1_g, ln1_b, w_router, b_router, w_gate, b_gate, w_up, b_up, w_down, b_down, ln2_g, ln2_b):
    t = x2.shape[0]
    tk = t * TOP_K
    bm = EXPERT_ROWS
    bf16 = jnp.bfloat16

    w_r_hi = w_router.astype(bf16)
    w_r_lo = (w_router - w_r_hi.astype(jnp.float32)).astype(bf16)
    w_r_cat = jnp.concatenate([w_r_hi, w_r_lo], axis=1)
    b_sp_full = jnp.repeat(b_spatial.T, GMLP_GROUP_DIM, axis=1)

    h, top_idx, gates, rank, counts = _mixer(
        x2, sinks, w_in.astype(bf16), b_in[None], ln_v_g[None], ln_v_b[None], w_spatial, b_sp_full,
        w_out.astype(bf16), b_out[None], ln1_g[None], ln1_b[None], w_r_cat, b_router[None], seq_len)

    counts = counts[0].astype(jnp.int32)
    padded = (counts + bm - 1) // bm * bm
    padded_end = jnp.cumsum(padded)
    padded_start = padded_end - padded
    n_blocks = tk // bm + N_EXPERTS
    n_rows = n_blocks * bm
    dest = padded_start[top_idx] + rank
    flat_tok = jnp.arange(tk, dtype=jnp.int32) // TOP_K
    row_tok = jnp.zeros((n_rows,), jnp.int32).at[dest.reshape(-1)].set(flat_tok)
    block_e = jnp.minimum(
        jnp.searchsorted(padded_end, jnp.arange(n_blocks, dtype=jnp.int32) * bm, side='right'),
        N_EXPERTS - 1).astype(jnp.int32)
    n_used = (padded_end[-1:] // bm).astype(jnp.int32)

    x_rows = _gather_rows(row_tok, h)
    y_rows = _experts(block_e, n_used, x_rows, w_gate, b_gate, w_up, b_up, w_down, b_down)
    y_tok = _gather_rows(dest.T.reshape(-1), y_rows).reshape(TOP_K, t, D_MODEL)
    return _combine(h, y_tok, gates, ln2_g[None], ln2_b[None])


def kernel(x, w_in, b_in, sinks, ln_v_g, ln_v_b, w_spatial, b_spatial, w_out, b_out, ln1_g, ln1_b,
           w_router, b_router, w_gate, b_gate, w_up, b_up, w_down, b_down, ln2_g, ln2_b):
    batch, seq_len, d = x.shape
    x2 = x.reshape(batch * seq_len, d)
    for l in range(DEPTH):
        x2 = _layer(x2, seq_len, w_in[l], b_in[l], sinks[l], ln_v_g[l], ln_v_b[l], w_spatial[l],
                    b_spatial[l], w_out[l], b_out[l], ln1_g[l], ln1_b[l], w_router[l], b_router[l],
                    w_gate[l], b_gate[l], w_up[l], b_up[l], w_down[l], b_down[l], ln2_g[l], ln2_b[l])
    return x2.reshape(batch, seq_len, d)
```

```python
import functools

import jax
import jax.numpy as jnp
from jax import lax
from jax.experimental import pallas as pl
from jax.experimental.pallas import tpu as pltpu
from jax.experimental.pallas import tpu_sc as plsc

D_MODEL = 1024
HEAD_DIM = 64
N_Q_HEADS = 8
N_KV_HEADS = 2
Q_REP = N_Q_HEADS // N_KV_HEADS
ATTN_WIDTH = N_Q_HEADS * HEAD_DIM
KV_WIDTH = N_KV_HEADS * HEAD_DIM
ATTN_BLOCK = 128
N_GMLP_GROUPS = 8
GMLP_WIDTH = D_MODEL - ATTN_WIDTH
GMLP_GROUP_DIM = GMLP_WIDTH // N_GMLP_GROUPS
IN_WIDTH = ATTN_WIDTH + 2 * KV_WIDTH + 2 * GMLP_WIDTH
N_EXPERTS = 32
TOP_K = 4
SWIGLU_LIMIT = 7.0
SWIGLU_ALPHA = 1.702
LN_EPS = 1e-5
DEPTH = 1
DEEPNORM_ALPHA = (2.0 * DEPTH) ** 0.25
NEG_INF = -1e30

MIXER_ROWS = 256
EXPERT_ROWS = 256
GATHER_WINDOW = 32
COMBINE_ROWS = 256
VMEM_LIMIT_BYTES = 56 * 1024 * 1024

_O_K = ATTN_WIDTH
_O_V = _O_K + KV_WIDTH
_O_U = _O_V + KV_WIDTH
_O_G = _O_U + GMLP_WIDTH


def _layer_norm(v, g, b):
    mu = jnp.mean(v, axis=-1, keepdims=True)
    vc = v - mu
    var = jnp.mean(vc * vc, axis=-1, keepdims=True)
    return vc * lax.rsqrt(var + LN_EPS) * g + b


def _attention_block(q, kb, vb, sinks_ref, first_block):
    nrow = Q_REP * ATTN_BLOCK
    t_idx = lax.broadcasted_iota(jnp.int32, (nrow, 2 * ATTN_BLOCK), 0) % ATTN_BLOCK
    s_idx = lax.broadcasted_iota(jnp.int32, (nrow, 2 * ATTN_BLOCK), 1)
    diff = t_idx + ATTN_BLOCK - s_idx
    valid = (diff >= 0) & (diff < ATTN_BLOCK) & (s_idx >= first_block * ATTN_BLOCK)
    head_of_row = lax.broadcasted_iota(jnp.int32, (nrow, 1), 0) // ATTN_BLOCK
    outs = []
    for g in range(N_KV_HEADS):
        kg = kb[:, g * HEAD_DIM:(g + 1) * HEAD_DIM]
        vg = vb[:, g * HEAD_DIM:(g + 1) * HEAD_DIM]
        qg = jnp.concatenate(
            [q[:, (g * Q_REP + r) * HEAD_DIM:(g * Q_REP + r + 1) * HEAD_DIM] for r in range(Q_REP)],
            axis=0).astype(jnp.bfloat16)
        s = lax.dot_general(qg, kg, (((1,), (1,)), ((), ())), preferred_element_type=jnp.float32)
        s = jnp.where(valid, s, NEG_INF)
        sink = jnp.zeros((nrow, 1), jnp.float32)
        for r in range(Q_REP):
            sink = jnp.where(head_of_row == r, sinks_ref[g * Q_REP + r], sink)
        m = jnp.maximum(jnp.max(s, axis=-1, keepdims=True), sink)
        p = jnp.exp(s - m)
        denom = jnp.sum(p, axis=-1, keepdims=True) + jnp.exp(sink - m)
        o = jnp.dot(p.astype(jnp.bfloat16), vg, preferred_element_type=jnp.float32) / denom
        outs.extend(o[r * ATTN_BLOCK:(r + 1) * ATTN_BLOCK] for r in range(Q_REP))
    return jnp.concatenate(outs, axis=-1)


def _mixer_kernel(sinks_ref, x_ref, w_in_ref, b_in_ref, lnv_g_ref, lnv_b_ref, grp_avg_ref,
                  w_sp_ref, b_sp_ref, w_out_ref, b_out_ref, ln1_g_ref, ln1_b_ref,
                  w_r_ref, b_r_ref, tri_ref,
                  h_ref, idx_ref, gate_ref, rank_ref, count_ref,
                  kv_prev_ref, *, steps_per_seq):
    i = pl.program_id(0)
    first_step = (i % steps_per_seq) == 0
    tm = x_ref.shape[0]
    n_sub = tm // ATTN_BLOCK

    @pl.when(i == 0)
    def _():
        count_ref[...] = jnp.zeros_like(count_ref)

    @pl.when(first_step)
    def _():
        kv_prev_ref[...] = jnp.zeros_like(kv_prev_ref)

    x = x_ref[...]
    proj = jnp.dot(x.astype(jnp.bfloat16), w_in_ref[...], preferred_element_type=jnp.float32) + b_in_ref[...]

    q_all = proj[:, :_O_K] * (HEAD_DIM ** -0.5)
    k_all = proj[:, _O_K:_O_V].astype(jnp.bfloat16)
    v_all = proj[:, _O_V:_O_U].astype(jnp.bfloat16)
    k_prev = kv_prev_ref[:, :KV_WIDTH]
    v_prev = kv_prev_ref[:, KV_WIDTH:]
    attn_blocks = []
    for sb in range(n_sub):
        rows = slice(sb * ATTN_BLOCK, (sb + 1) * ATTN_BLOCK)
        k_cur, v_cur = k_all[rows], v_all[rows]
        kb = jnp.concatenate([k_prev, k_cur], axis=0)
        vb = jnp.concatenate([v_prev, v_cur], axis=0)
        first_block = jnp.where(first_step, 1, 0) if sb == 0 else 0
        attn_blocks.append(_attention_block(q_all[rows], kb, vb, sinks_ref, first_block))
        k_prev, v_prev = k_cur, v_cur
    kv_prev_ref[:, :KV_WIDTH] = k_prev
    kv_prev_ref[:, KV_WIDTH:] = v_prev
    attn = jnp.concatenate(attn_blocks, axis=0)

    u = jax.nn.gelu(proj[:, _O_U:_O_G])
    gg = jax.nn.gelu(proj[:, _O_G:])
    avg = grp_avg_ref[...]
    mu = jnp.dot(gg.astype(jnp.bfloat16), avg, preferred_element_type=jnp.float32)
    gc = gg - mu
    var = jnp.dot((gc * gc).astype(jnp.bfloat16), avg, preferred_element_type=jnp.float32)
    gn = (gc * lax.rsqrt(var + LN_EPS) * lnv_g_ref[...] + lnv_b_ref[...]).astype(jnp.bfloat16)
    causal = (lax.broadcasted_iota(jnp.int32, (ATTN_BLOCK, ATTN_BLOCK), 0)
              >= lax.broadcasted_iota(jnp.int32, (ATTN_BLOCK, ATTN_BLOCK), 1))
    w_sp = [jnp.where(causal, w_sp_ref[g], 0.0).astype(jnp.bfloat16) for g in range(N_GMLP_GROUPS)]
    mixed_chunks = []
    for c in range(n_sub):
        rows = slice(c * ATTN_BLOCK, (c + 1) * ATTN_BLOCK)
        pieces = [
            jnp.dot(w_sp[g], gn[rows, g * GMLP_GROUP_DIM:(g + 1) * GMLP_GROUP_DIM],
                    preferred_element_type=jnp.float32)
            for g in range(N_GMLP_GROUPS)]
        mixed_chunks.append(jnp.concatenate(pieces, axis=-1) + b_sp_ref[...])
    sgu = u * jnp.concatenate(mixed_chunks, axis=0)

    mix = (jnp.dot(attn.astype(jnp.bfloat16), w_out_ref[:ATTN_WIDTH, :], preferred_element_type=jnp.float32)
           + jnp.dot(sgu.astype(jnp.bfloat16), w_out_ref[ATTN_WIDTH:, :], preferred_element_type=jnp.float32)
           + b_out_ref[...])
    h = _layer_norm(DEEPNORM_ALPHA * x + mix, ln1_g_ref[...], ln1_b_ref[...])
    h_ref[...] = h

    h_hi = h.astype(jnp.bfloat16)
    h_lo = (h - h_hi.astype(jnp.float32)).astype(jnp.bfloat16)
    part = jnp.dot(h_hi, w_r_ref[...], preferred_element_type=jnp.float32)
    logits = (part[:, :N_EXPERTS] + part[:, N_EXPERTS:]
              + jnp.dot(h_lo, w_r_ref[:, :N_EXPERTS], preferred_element_type=jnp.float32)
              + b_r_ref[...])
    lane = lax.broadcasted_iota(jnp.int32, (tm, N_EXPERTS), 1).astype(jnp.float32)
    work = logits
    vals, idxs = [], []
    onehot = jnp.zeros((tm, N_EXPERTS), jnp.float32)
    for _ in range(TOP_K):
        m = jnp.max(work, axis=-1, keepdims=True)
        idx = jnp.min(jnp.where(work == m, lane, float(N_EXPERTS)), axis=-1, keepdims=True)
        sel = lane == idx
        vals.append(m)
        idxs.append(idx)
        onehot = jnp.where(sel, 1.0, onehot)
        work = jnp.where(sel, -jnp.inf, work)
    exps = [jnp.exp(v - vals[0]) for v in vals]
    denom = exps[0] + exps[1] + exps[2] + exps[3]
    before = jnp.dot(tri_ref[...], onehot.astype(jnp.bfloat16), preferred_element_type=jnp.float32) + count_ref[...]
    for k in range(TOP_K):
        idx_ref[:, k:k + 1] = idxs[k].astype(jnp.int32)
        gate_ref[:, k:k + 1] = exps[k] / denom
        rank_ref[:, k:k + 1] = jnp.sum(jnp.where(lane == idxs[k], before, 0.0), axis=-1,
                                       keepdims=True).astype(jnp.int32)
    count_ref[...] += jnp.sum(onehot, axis=0, keepdims=True)


def _mixer(x2, sinks, w_in, b_in, lnv_g, lnv_b, w_sp, b_sp_full, w_out, b_out, ln1_g, ln1_b,
           w_r_cat, b_r, seq_len):
    t = x2.shape[0]
    tm = MIXER_ROWS
    grp = jnp.arange(GMLP_WIDTH) // GMLP_GROUP_DIM
    grp_avg = jnp.where(grp[:, None] == grp[None, :], 1.0 / GMLP_GROUP_DIM, 0.0).astype(jnp.bfloat16)
    tri = (jnp.arange(tm)[:, None] > jnp.arange(tm)[None, :]).astype(jnp.bfloat16)

    def full(shape):
        return pl.BlockSpec(shape, lambda i: (0,) * len(shape))

    return pl.pallas_call(
        functools.partial(_mixer_kernel, steps_per_seq=seq_len // tm),
        grid=(t // tm,),
        in_specs=[
            pl.BlockSpec(memory_space=pltpu.SMEM),
            pl.BlockSpec((tm, D_MODEL), lambda i: (i, 0)),
            full((D_MODEL, IN_WIDTH)), full((1, IN_WIDTH)),
            full((1, GMLP_WIDTH)), full((1, GMLP_WIDTH)), full((GMLP_WIDTH, GMLP_WIDTH)),
            full((N_GMLP_GROUPS, ATTN_BLOCK, ATTN_BLOCK)), full((ATTN_BLOCK, GMLP_WIDTH)),
            full((D_MODEL, D_MODEL)), full((1, D_MODEL)), full((1, D_MODEL)), full((1, D_MODEL)),
            full((D_MODEL, 2 * N_EXPERTS)), full((1, N_EXPERTS)), full((tm, tm)),
        ],
        out_specs=[
            pl.BlockSpec((tm, D_MODEL), lambda i: (i, 0)),
            pl.BlockSpec((tm, TOP_K), lambda i: (i, 0)),
            pl.BlockSpec((tm, TOP_K), lambda i: (i, 0)),
            pl.BlockSpec((tm, TOP_K), lambda i: (i, 0)),
            pl.BlockSpec((1, N_EXPERTS), lambda i: (0, 0)),
        ],
        out_shape=[
            jax.ShapeDtypeStruct((t, D_MODEL), jnp.float32),
            jax.ShapeDtypeStruct((t, TOP_K), jnp.int32),
            jax.ShapeDtypeStruct((t, TOP_K), jnp.float32),
            jax.ShapeDtypeStruct((t, TOP_K), jnp.int32),
            jax.ShapeDtypeStruct((1, N_EXPERTS), jnp.float32),
        ],
        scratch_shapes=[pltpu.VMEM((ATTN_BLOCK, 2 * KV_WIDTH), jnp.bfloat16)],
        compiler_params=pltpu.CompilerParams(
            dimension_semantics=("arbitrary",), vmem_limit_bytes=VMEM_LIMIT_BYTES),
        name="mixer",
    )(sinks, x2, w_in, b_in, lnv_g, lnv_b, grp_avg, w_sp, b_sp_full, w_out, b_out, ln1_g, ln1_b,
      w_r_cat, b_r, tri)


def _gather_rows(idx, src):
    n = idx.shape[0]
    width = src.shape[1]
    win = GATHER_WINDOW
    sc = plsc.get_sparse_core_info()
    n_workers = sc.num_cores * sc.num_subcores
    per_worker = n // n_workers
    n_pairs = per_worker // (2 * win)
    assert n_pairs * 2 * win * n_workers == n
    mesh = plsc.VectorSubcoreMesh(core_axis_name="core", subcore_axis_name="subcore")

    @functools.partial(
        pl.kernel, out_type=jax.ShapeDtypeStruct((n, width), src.dtype), mesh=mesh,
        scratch_types=[pltpu.VMEM((per_worker,), jnp.int32), pltpu.VMEM((2, win, width), src.dtype),
                       pltpu.SemaphoreType.DMA((2,)), pltpu.SemaphoreType.DMA((2,))],
        name="gather_rows")
    def gather(src_hbm, idx_hbm, out_hbm, idx_v, rows_v, fetch_sem, store_sem):
        worker = lax.axis_index("subcore") * sc.num_cores + lax.axis_index("core")
        base = worker * per_worker
        pltpu.sync_copy(idx_hbm.at[pl.ds(base, per_worker)], idx_v)

        def fetch(chunk, buf):
            return pltpu.make_async_copy(src_hbm.at[idx_v.at[pl.ds(chunk * win, win)]], rows_v.at[buf],
                                         fetch_sem.at[buf])

        def store(chunk, buf):
            return pltpu.make_async_copy(rows_v.at[buf], out_hbm.at[pl.ds(base + chunk * win, win)],
                                         store_sem.at[buf])

        @pl.loop(0, n_pairs)
        def _(p):
            for buf in range(2):
                @pl.when(p > 0)
                def _():
                    store(2 * p - 2 + buf, buf).wait()
                fetch(2 * p + buf, buf).start()
            for buf in range(2):
                fetch(2 * p + buf, buf).wait()
                store(2 * p + buf, buf).start()

        for buf in range(2):
            store(2 * n_pairs - 2 + buf, buf).wait()

    return gather(src, idx)


def _expert_kernel(block_e_ref, n_used_ref, x_ref, wg_ref, bg_ref, wu_ref, bu_ref, wd_ref, bd_ref,
                   y_ref, wg_bf, wu_bf, wd_bf):
    i = pl.program_id(0)
    changed = jnp.logical_or(i == 0, block_e_ref[i] != block_e_ref[jnp.maximum(i - 1, 0)])

    @pl.when(changed)
    def _():
        wg_bf[...] = wg_ref[0].astype(jnp.bfloat16)
        wu_bf[...] = wu_ref[0].astype(jnp.bfloat16)
        wd_bf[...] = wd_ref[0].astype(jnp.bfloat16)

    @pl.when(i < n_used_ref[0])
    def _():
        xb = x_ref[...].astype(jnp.bfloat16)
        gt = jnp.minimum(jnp.dot(xb, wg_bf[...], preferred_element_type=jnp.float32) + bg_ref[0], SWIGLU_LIMIT)
        up = jnp.clip(jnp.dot(xb, wu_bf[...], preferred_element_type=jnp.float32) + bu_ref[0],
                      -SWIGLU_LIMIT, SWIGLU_LIMIT)
        hid = gt * jax.nn.sigmoid(SWIGLU_ALPHA * gt) * (up + 1.0)
        y_ref[...] = jnp.dot(hid.astype(jnp.bfloat16), wd_bf[...], preferred_element_type=jnp.float32) + bd_ref[0]

    @pl.when(i >= n_used_ref[0])
    def _():
        y_ref[...] = jnp.zeros_like(y_ref)


def _experts(block_e, n_used, x_rows, w_gate, b_gate, w_up, b_up, w_down, b_down):
    n_rows = x_rows.shape[0]
    bm = EXPERT_ROWS
    d_ff = w_gate.shape[2]

    def by_expert(shape):
        return pl.BlockSpec(shape, lambda i, be, nu: (be[i],) + (0,) * (len(shape) - 1))

    return pl.pallas_call(
        _expert_kernel,
        grid_spec=pltpu.PrefetchScalarGridSpec(
            num_scalar_prefetch=2,
            grid=(n_rows // bm,),
            in_specs=[
                pl.BlockSpec((bm, D_MODEL), lambda i, be, nu: (i, 0)),
                by_expert((1, D_MODEL, d_ff)), by_expert((1, 1, d_ff)),
                by_expert((1, D_MODEL, d_ff)), by_expert((1, 1, d_ff)),
                by_expert((1, d_ff, D_MODEL)), by_expert((1, 1, D_MODEL)),
            ],
            out_specs=pl.BlockSpec((bm, D_MODEL), lambda i, be, nu: (i, 0)),
            scratch_shapes=[
                pltpu.VMEM((D_MODEL, d_ff), jnp.bfloat16),
                pltpu.VMEM((D_MODEL, d_ff), jnp.bfloat16),
                pltpu.VMEM((d_ff, D_MODEL), jnp.bfloat16),
            ],
        ),
        out_shape=jax.ShapeDtypeStruct((n_rows, D_MODEL), jnp.float32),
        compiler_params=pltpu.CompilerParams(
            dimension_semantics=("arbitrary",), vmem_limit_bytes=VMEM_LIMIT_BYTES),
        name="experts",
    )(block_e, n_used, x_rows, w_gate, b_gate.reshape(N_EXPERTS, 1, d_ff), w_up,
      b_up.reshape(N_EXPERTS, 1, d_ff), w_down, b_down.reshape(N_EXPERTS, 1, D_MODEL))


def _combine_kernel(h_ref, y_ref, gate_ref, g_ref, b_ref, o_ref):
    ffn = gate_ref[:, 0:1] * y_ref[0]
    for k in range(1, TOP_K):
        ffn = ffn + gate_ref[:, k:k + 1] * y_ref[k]
    o_ref[...] = _layer_norm(DEEPNORM_ALPHA * h_ref[...] + ffn, g_ref[...], b_ref[...])


def _combine(h, y_tok, gates, ln2_g, ln2_b):
    t = h.shape[0]
    tm = COMBINE_ROWS
    return pl.pallas_call(
        _combine_kernel,
        grid=(t // tm,),
        in_specs=[
            pl.BlockSpec((tm, D_MODEL), lambda i: (i, 0)),
            pl.BlockSpec((TOP_K, tm, D_MODEL), lambda i: (0, i, 0)),
            pl.BlockSpec((tm, TOP_K), lambda i: (i, 0)),
            pl.BlockSpec((1, D_MODEL), lambda i: (0, 0)),
            pl.BlockSpec((1, D_MODEL), lambda i: (0, 0)),
        ],
        out_specs=pl.BlockSpec((tm, D_MODEL), lambda i: (i, 0)),
        out_shape=jax.ShapeDtypeStruct((t, D_MODEL), jnp.float32),
        compiler_params=pltpu.CompilerParams(
            dimension_semantics=("arbitrary",), vmem_limit_bytes=VMEM_LIMIT_BYTES),
        name="combine",
    )(h, y_tok, gates, ln2_g, ln2_b)


def _layer(x2, seq_len, w_in, b_in, sinks, ln_v_g, ln_v_b, w_spatial, b_spatial, w_out, b_out,
           ln1_g, ln1_b, w_router, b_router, w_gate, b_gate, w_up, b_up, w_down, b_down, ln2_g, ln2_b):
    t = x2.shape[0]
    tk = t * TOP_K
    bm = EXPERT_ROWS
    bf16 = jnp.bfloat16

    w_r_hi = w_router.astype(bf16)
    w_r_lo = (w_router - w_r_hi.astype(jnp.float32)).astype(bf16)
    w_r_cat = jnp.concatenate([w_r_hi, w_r_lo], axis=1)
    b_sp_full = jnp.repeat(b_spatial.T, GMLP_GROUP_DIM, axis=1)

    h, top_idx, gates, rank, counts = _mixer(
        x2, sinks, w_in.astype(bf16), b_in[None], ln_v_g[None], ln_v_b[None], w_spatial, b_sp_full,
        w_out.astype(bf16), b_out[None], ln1_g[None], ln1_b[None], w_r_cat, b_router[None], seq_len)

    counts = counts[0].astype(jnp.int32)
    padded = (counts + bm - 1) // bm * bm
    padded_end = jnp.cumsum(padded)
    padded_start = padded_end - padded
    n_blocks = tk // bm + N_EXPERTS
    n_rows = n_blocks * bm
    dest = padded_start[top_idx] + rank
    flat_tok = jnp.arange(tk, dtype=jnp.int32) // TOP_K
    row_tok = jnp.zeros((n_rows,), jnp.int32).at[dest.reshape(-1)].set(flat_tok)
    block_start = jnp.arange(n_blocks, dtype=jnp.int32) * bm
    block_e = jnp.minimum(
        jnp.sum((padded_end[None, :] <= block_start[:, None]).astype(jnp.int32), axis=1), N_EXPERTS - 1)
    n_used = (padded_end[-1:] // bm).astype(jnp.int32)

    x_rows = _gather_rows(row_tok, h)
    y_rows = _experts(block_e, n_used, x_rows, w_gate, b_gate, w_up, b_up, w_down, b_down)
    y_tok = _gather_rows(dest.T.reshape(-1), y_rows).reshape(TOP_K, t, D_MODEL)
    return _combine(h, y_tok, gates, ln2_g[None], ln2_b[None])


def kernel(x, w_in, b_in, sinks, ln_v_g, ln_v_b, w_spatial, b_spatial, w_out, b_out, ln1_g, ln1_b,
           w_router, b_router, w_gate, b_gate, w_up, b_up, w_down, b_down, ln2_g, ln2_b):
    batch, seq_len, d = x.shape
    x2 = x.reshape(batch * seq_len, d)
    for l in range(DEPTH):
        x2 = _layer(x2, seq_len, w_in[l], b_in[l], sinks[l], ln_v_g[l], ln_v_b[l], w_spatial[l],
                    b_spatial[l], w_out[l], b_out[l], ln1_g[l], ln1_b[l], w_router[l], b_router[l],
                    w_gate[l], b_gate[l], w_up[l], b_up[l], w_down[l], b_down[l], ln2_g[l], ln2_b[l])
    return x2.reshape(batch, seq_len, d)
```

```python
import functools

import jax
import jax.numpy as jnp
from jax import lax
from jax.experimental import pallas as pl
from jax.experimental.pallas import tpu as pltpu
from jax.experimental.pallas import tpu_sc as plsc

D_MODEL = 1024
HEAD_DIM = 64
N_Q_HEADS = 8
N_KV_HEADS = 2
Q_REP = N_Q_HEADS // N_KV_HEADS
ATTN_WIDTH = N_Q_HEADS * HEAD_DIM
KV_WIDTH = N_KV_HEADS * HEAD_DIM
ATTN_BLOCK = 128
N_GMLP_GROUPS = 8
GMLP_WIDTH = D_MODEL - ATTN_WIDTH
GMLP_GROUP_DIM = GMLP_WIDTH // N_GMLP_GROUPS
IN_WIDTH = ATTN_WIDTH + 2 * KV_WIDTH + 2 * GMLP_WIDTH
N_EXPERTS = 32
TOP_K = 4
SWIGLU_LIMIT = 7.0
SWIGLU_ALPHA = 1.702
LN_EPS = 1e-5
DEPTH = 1
DEEPNORM_ALPHA = (2.0 * DEPTH) ** 0.25
NEG_INF = -1e30

MIXER_ROWS = 256
EXPERT_ROWS = 512
GATHER_WINDOW = 32
COMBINE_ROWS = 256
VMEM_LIMIT_BYTES = 56 * 1024 * 1024

_O_K = ATTN_WIDTH
_O_V = _O_K + KV_WIDTH
_O_U = _O_V + KV_WIDTH
_O_G = _O_U + GMLP_WIDTH


def _layer_norm(v, g, b):
    mu = jnp.mean(v, axis=-1, keepdims=True)
    vc = v - mu
    var = jnp.mean(vc * vc, axis=-1, keepdims=True)
    return vc * lax.rsqrt(var + LN_EPS) * g + b


def _attention_block(q, kb, vb, sinks_ref, first_block):
    nrow = Q_REP * ATTN_BLOCK
    t_idx = lax.broadcasted_iota(jnp.int32, (nrow, 2 * ATTN_BLOCK), 0) % ATTN_BLOCK
    s_idx = lax.broadcasted_iota(jnp.int32, (nrow, 2 * ATTN_BLOCK), 1)
    diff = t_idx + ATTN_BLOCK - s_idx
    valid = (diff >= 0) & (diff < ATTN_BLOCK) & (s_idx >= first_block * ATTN_BLOCK)
    head_of_row = lax.broadcasted_iota(jnp.int32, (nrow, 1), 0) // ATTN_BLOCK
    outs = []
    for g in range(N_KV_HEADS):
        kg = kb[:, g * HEAD_DIM:(g + 1) * HEAD_DIM]
        vg = vb[:, g * HEAD_DIM:(g + 1) * HEAD_DIM]
        qg = jnp.concatenate(
            [q[:, (g * Q_REP + r) * HEAD_DIM:(g * Q_REP + r + 1) * HEAD_DIM] for r in range(Q_REP)],
            axis=0).astype(jnp.bfloat16)
        s = lax.dot_general(qg, kg, (((1,), (1,)), ((), ())), preferred_element_type=jnp.float32)
        s = jnp.where(valid, s, NEG_INF)
        sink = jnp.zeros((nrow, 1), jnp.float32)
        for r in range(Q_REP):
            sink = jnp.where(head_of_row == r, sinks_ref[g * Q_REP + r], sink)
        m = jnp.maximum(jnp.max(s, axis=-1, keepdims=True), sink)
        p = jnp.exp(s - m)
        denom = jnp.sum(p, axis=-1, keepdims=True) + jnp.exp(sink - m)
        o = jnp.dot(p.astype(jnp.bfloat16), vg, preferred_element_type=jnp.float32) / denom
        outs.extend(o[r * ATTN_BLOCK:(r + 1) * ATTN_BLOCK] for r in range(Q_REP))
    return jnp.concatenate(outs, axis=-1)


def _mixer_kernel(sinks_ref, x_ref, w_in_ref, b_in_ref, lnv_g_ref, lnv_b_ref, grp_avg_ref,
                  w_sp_ref, b_sp_ref, w_out_ref, b_out_ref, ln1_g_ref, ln1_b_ref,
                  w_r_ref, b_r_ref, tri_ref,
                  h_ref, idx_ref, gate_ref, rank_ref, count_ref,
                  kv_prev_ref, *, steps_per_seq):
    i = pl.program_id(0)
    first_step = (i % steps_per_seq) == 0
    tm = x_ref.shape[0]
    n_sub = tm // ATTN_BLOCK

    @pl.when(i == 0)
    def _():
        count_ref[...] = jnp.zeros_like(count_ref)

    @pl.when(first_step)
    def _():
        kv_prev_ref[...] = jnp.zeros_like(kv_prev_ref)

    x = x_ref[...]
    proj = jnp.dot(x.astype(jnp.bfloat16), w_in_ref[...], preferred_element_type=jnp.float32) + b_in_ref[...]

    q_all = proj[:, :_O_K] * (HEAD_DIM ** -0.5)
    k_all = proj[:, _O_K:_O_V].astype(jnp.bfloat16)
    v_all = proj[:, _O_V:_O_U].astype(jnp.bfloat16)
    k_prev = kv_prev_ref[:, :KV_WIDTH]
    v_prev = kv_prev_ref[:, KV_WIDTH:]
    attn_blocks = []
    for sb in range(n_sub):
        rows = slice(sb * ATTN_BLOCK, (sb + 1) * ATTN_BLOCK)
        k_cur, v_cur = k_all[rows], v_all[rows]
        kb = jnp.concatenate([k_prev, k_cur], axis=0)
        vb = jnp.concatenate([v_prev, v_cur], axis=0)
        first_block = jnp.where(first_step, 1, 0) if sb == 0 else 0
        attn_blocks.append(_attention_block(q_all[rows], kb, vb, sinks_ref, first_block))
        k_prev, v_prev = k_cur, v_cur
    kv_prev_ref[:, :KV_WIDTH] = k_prev
    kv_prev_ref[:, KV_WIDTH:] = v_prev
    attn = jnp.concatenate(attn_blocks, axis=0)

    u = jax.nn.gelu(proj[:, _O_U:_O_G])
    gg = jax.nn.gelu(proj[:, _O_G:])
    avg = grp_avg_ref[...]
    mu = jnp.dot(gg.astype(jnp.bfloat16), avg, preferred_element_type=jnp.float32)
    gc = gg - mu
    var = jnp.dot((gc * gc).astype(jnp.bfloat16), avg, preferred_element_type=jnp.float32)
    gn = (gc * lax.rsqrt(var + LN_EPS) * lnv_g_ref[...] + lnv_b_ref[...]).astype(jnp.bfloat16)
    causal = (lax.broadcasted_iota(jnp.int32, (ATTN_BLOCK, ATTN_BLOCK), 0)
              >= lax.broadcasted_iota(jnp.int32, (ATTN_BLOCK, ATTN_BLOCK), 1))
    w_sp = [jnp.where(causal, w_sp_ref[g], 0.0).astype(jnp.bfloat16) for g in range(N_GMLP_GROUPS)]
    mixed_chunks = []
    for c in range(n_sub):
        rows = slice(c * ATTN_BLOCK, (c + 1) * ATTN_BLOCK)
        pieces = [
            jnp.dot(w_sp[g], gn[rows, g * GMLP_GROUP_DIM:(g + 1) * GMLP_GROUP_DIM],
                    preferred_element_type=jnp.float32)
            for g in range(N_GMLP_GROUPS)]
        mixed_chunks.append(jnp.concatenate(pieces, axis=-1) + b_sp_ref[...])
    sgu = u * jnp.concatenate(mixed_chunks, axis=0)

    mix = (jnp.dot(attn.astype(jnp.bfloat16), w_out_ref[:ATTN_WIDTH, :], preferred_element_type=jnp.float32)
           + jnp.dot(sgu.astype(jnp.bfloat16), w_out_ref[ATTN_WIDTH:, :], preferred_element_type=jnp.float32)
           + b_out_ref[...])
    h = _layer_norm(DEEPNORM_ALPHA * x + mix, ln1_g_ref[...], ln1_b_ref[...])
    h_ref[...] = h

    h_hi = h.astype(jnp.bfloat16)
    h_lo = (h - h_hi.astype(jnp.float32)).astype(jnp.bfloat16)
    part = jnp.dot(h_hi, w_r_ref[...], preferred_element_type=jnp.float32)
    logits = (part[:, :N_EXPERTS] + part[:, N_EXPERTS:]
              + jnp.dot(h_lo, w_r_ref[:, :N_EXPERTS], preferred_element_type=jnp.float32)
              + b_r_ref[...])
    lane = lax.broadcasted_iota(jnp.int32, (tm, N_EXPERTS), 1).astype(jnp.float32)
    work = logits
    vals, idxs = [], []
    onehot = jnp.zeros((tm, N_EXPERTS), jnp.float32)
    for _ in range(TOP_K):
        m = jnp.max(work, axis=-1, keepdims=True)
        idx = jnp.min(jnp.where(work == m, lane, float(N_EXPERTS)), axis=-1, keepdims=True)
        sel = lane == idx
        vals.append(m)
        idxs.append(idx)
        onehot = jnp.where(sel, 1.0, onehot)
        work = jnp.where(sel, -jnp.inf, work)
    exps = [jnp.exp(v - vals[0]) for v in vals]
    denom = exps[0] + exps[1] + exps[2] + exps[3]
    before = jnp.dot(tri_ref[...], onehot.astype(jnp.bfloat16), preferred_element_type=jnp.float32) + count_ref[...]
    for k in range(TOP_K):
        idx_ref[:, k:k + 1] = idxs[k].astype(jnp.int32)
        gate_ref[:, k:k + 1] = exps[k] / denom
        rank_ref[:, k:k + 1] = jnp.sum(jnp.where(lane == idxs[k], before, 0.0), axis=-1,
                                       keepdims=True).astype(jnp.int32)
    count_ref[...] += jnp.sum(onehot, axis=0, keepdims=True)


def _mixer(x2, sinks, w_in, b_in, lnv_g, lnv_b, w_sp, b_sp_full, w_out, b_out, ln1_g, ln1_b,
           w_r_cat, b_r, seq_len):
    t = x2.shape[0]
    tm = MIXER_ROWS
    grp = jnp.arange(GMLP_WIDTH) // GMLP_GROUP_DIM
    grp_avg = jnp.where(grp[:, None] == grp[None, :], 1.0 / GMLP_GROUP_DIM, 0.0).astype(jnp.bfloat16)
    tri = (jnp.arange(tm)[:, None] > jnp.arange(tm)[None, :]).astype(jnp.bfloat16)

    def full(shape):
        return pl.BlockSpec(shape, lambda i: (0,) * len(shape))

    return pl.pallas_call(
        functools.partial(_mixer_kernel, steps_per_seq=seq_len // tm),
        grid=(t // tm,),
        in_specs=[
            pl.BlockSpec(memory_space=pltpu.SMEM),
            pl.BlockSpec((tm, D_MODEL), lambda i: (i, 0)),
            full((D_MODEL, IN_WIDTH)), full((1, IN_WIDTH)),
            full((1, GMLP_WIDTH)), full((1, GMLP_WIDTH)), full((GMLP_WIDTH, GMLP_WIDTH)),
            full((N_GMLP_GROUPS, ATTN_BLOCK, ATTN_BLOCK)), full((ATTN_BLOCK, GMLP_WIDTH)),
            full((D_MODEL, D_MODEL)), full((1, D_MODEL)), full((1, D_MODEL)), full((1, D_MODEL)),
            full((D_MODEL, 2 * N_EXPERTS)), full((1, N_EXPERTS)), full((tm, tm)),
        ],
        out_specs=[
            pl.BlockSpec((tm, D_MODEL), lambda i: (i, 0)),
            pl.BlockSpec((tm, TOP_K), lambda i: (i, 0)),
            pl.BlockSpec((tm, TOP_K), lambda i: (i, 0)),
            pl.BlockSpec((tm, TOP_K), lambda i: (i, 0)),
            pl.BlockSpec((1, N_EXPERTS), lambda i: (0, 0)),
        ],
        out_shape=[
            jax.ShapeDtypeStruct((t, D_MODEL), jnp.float32),
            jax.ShapeDtypeStruct((t, TOP_K), jnp.int32),
            jax.ShapeDtypeStruct((t, TOP_K), jnp.float32),
            jax.ShapeDtypeStruct((t, TOP_K), jnp.int32),
            jax.ShapeDtypeStruct((1, N_EXPERTS), jnp.float32),
        ],
        scratch_shapes=[pltpu.VMEM((ATTN_BLOCK, 2 * KV_WIDTH), jnp.bfloat16)],
        compiler_params=pltpu.CompilerParams(
            dimension_semantics=("arbitrary",), vmem_limit_bytes=VMEM_LIMIT_BYTES),
        name="mixer",
    )(sinks, x2, w_in, b_in, lnv_g, lnv_b, grp_avg, w_sp, b_sp_full, w_out, b_out, ln1_g, ln1_b,
      w_r_cat, b_r, tri)


def _gather_rows(idx, src):
    n = idx.shape[0]
    width = src.shape[1]
    win = GATHER_WINDOW
    sc = plsc.get_sparse_core_info()
    n_workers = sc.num_cores * sc.num_subcores
    per_worker = n // n_workers
    n_pairs = per_worker // (2 * win)
    assert n_pairs * 2 * win * n_workers == n
    mesh = plsc.VectorSubcoreMesh(core_axis_name="core", subcore_axis_name="subcore")

    @functools.partial(
        pl.kernel, out_type=jax.ShapeDtypeStruct((n, width), src.dtype), mesh=mesh,
        scratch_types=[pltpu.VMEM((per_worker,), jnp.int32), pltpu.VMEM((2, win, width), src.dtype),
                       pltpu.SemaphoreType.DMA((2,)), pltpu.SemaphoreType.DMA((2,))],
        name="gather_rows")
    def gather(src_hbm, idx_hbm, out_hbm, idx_v, rows_v, fetch_sem, store_sem):
        worker = lax.axis_index("subcore") * sc.num_cores + lax.axis_index("core")
        base = worker * per_worker
        pltpu.sync_copy(idx_hbm.at[pl.ds(base, per_worker)], idx_v)

        def fetch(chunk, buf):
            return pltpu.make_async_copy(src_hbm.at[idx_v.at[pl.ds(chunk * win, win)]], rows_v.at[buf],
                                         fetch_sem.at[buf])

        def store(chunk, buf):
            return pltpu.make_async_copy(rows_v.at[buf], out_hbm.at[pl.ds(base + chunk * win, win)],
                                         store_sem.at[buf])

        @pl.loop(0, n_pairs)
        def _(p):
            for buf in range(2):
                @pl.when(p > 0)
                def _():
                    store(2 * p - 2 + buf, buf).wait()
                fetch(2 * p + buf, buf).start()
            for buf in range(2):
                fetch(2 * p + buf, buf).wait()
                store(2 * p + buf, buf).start()

        for buf in range(2):
            store(2 * n_pairs - 2 + buf, buf).wait()

    return gather(src, idx)


def _dispatch_rows(dest, src, n_rows):
    t, width = src.shape
    win = GATHER_WINDOW
    sc = plsc.get_sparse_core_info()
    n_workers = sc.num_cores * sc.num_subcores
    per_worker = t // n_workers
    n_chunks = per_worker // win
    n_pairs = n_chunks // 2
    assert n_pairs * 2 * win * n_workers == t
    idx = dest.T.reshape(TOP_K, n_workers, n_chunks, win).transpose(1, 0, 2, 3)
    idx = idx.reshape(n_workers, TOP_K * n_chunks, win)
    mesh = plsc.VectorSubcoreMesh(core_axis_name="core", subcore_axis_name="subcore")

    @functools.partial(
        pl.kernel, out_type=jax.ShapeDtypeStruct((n_rows, width), src.dtype), mesh=mesh,
        scratch_types=[pltpu.VMEM((TOP_K * n_chunks, win), jnp.int32), pltpu.VMEM((2, win, width), src.dtype),
                       pltpu.SemaphoreType.DMA((2,)), pltpu.SemaphoreType.DMA((2,))],
        name="dispatch_rows")
    def dispatch(src_hbm, idx_hbm, out_hbm, idx_v, rows_v, fetch_sem, store_sem):
        worker = lax.axis_index("subcore") * sc.num_cores + lax.axis_index("core")
        base = worker * per_worker
        pltpu.sync_copy(idx_hbm.at[worker], idx_v)

        def fetch(chunk, buf):
            return pltpu.make_async_copy(src_hbm.at[pl.ds(base + chunk * win, win)], rows_v.at[buf],
                                         fetch_sem.at[buf])

        def store(chunk, k, buf):
            return pltpu.make_async_copy(rows_v.at[buf], out_hbm.at[idx_v.at[k * n_chunks + chunk]],
                                         store_sem.at[buf])

        @pl.loop(0, n_pairs)
        def _(p):
            for buf in range(2):
                @pl.when(p > 0)
                def _():
                    for k in range(TOP_K):
                        store(2 * p - 2 + buf, k, buf).wait()
                fetch(2 * p + buf, buf).start()
            for buf in range(2):
                fetch(2 * p + buf, buf).wait()
                for k in range(TOP_K):
                    store(2 * p + buf, k, buf).start()

        for buf in range(2):
            for k in range(TOP_K):
                store(2 * n_pairs - 2 + buf, k, buf).wait()

    return dispatch(src, idx)


def _expert_kernel(block_e_ref, n_valid_ref, x_ref, wg_ref, bg_ref, wu_ref, bu_ref, wd_ref, bd_ref,
                   y_ref, wg_bf, wu_bf, wd_bf):
    i = pl.program_id(0)
    changed = jnp.logical_or(i == 0, block_e_ref[i] != block_e_ref[jnp.maximum(i - 1, 0)])
    n_valid = n_valid_ref[i]

    @pl.when(changed)
    def _():
        wg_bf[...] = wg_ref[0].astype(jnp.bfloat16)
        wu_bf[...] = wu_ref[0].astype(jnp.bfloat16)
        wd_bf[...] = wd_ref[0].astype(jnp.bfloat16)

    @pl.when(n_valid > 0)
    def _():
        row = lax.broadcasted_iota(jnp.int32, (x_ref.shape[0], 1), 0)
        xb = jnp.where(row < n_valid, x_ref[...], 0.0).astype(jnp.bfloat16)
        gt = jnp.minimum(jnp.dot(xb, wg_bf[...], preferred_element_type=jnp.float32) + bg_ref[0], SWIGLU_LIMIT)
        up = jnp.clip(jnp.dot(xb, wu_bf[...], preferred_element_type=jnp.float32) + bu_ref[0],
                      -SWIGLU_LIMIT, SWIGLU_LIMIT)
        hid = gt * jax.nn.sigmoid(SWIGLU_ALPHA * gt) * (up + 1.0)
        y_ref[...] = jnp.dot(hid.astype(jnp.bfloat16), wd_bf[...], preferred_element_type=jnp.float32) + bd_ref[0]

    @pl.when(n_valid == 0)
    def _():
        y_ref[...] = jnp.zeros_like(y_ref)


def _experts(block_e, n_valid, x_rows, w_gate, b_gate, w_up, b_up, w_down, b_down):
    n_rows = x_rows.shape[0]
    bm = EXPERT_ROWS
    d_ff = w_gate.shape[2]

    def by_expert(shape):
        return pl.BlockSpec(shape, lambda i, be, nu: (be[i],) + (0,) * (len(shape) - 1))

    return pl.pallas_call(
        _expert_kernel,
        grid_spec=pltpu.PrefetchScalarGridSpec(
            num_scalar_prefetch=2,
            grid=(n_rows // bm,),
            in_specs=[
                pl.BlockSpec((bm, D_MODEL), lambda i, be, nu: (i, 0)),
                by_expert((1, D_MODEL, d_ff)), by_expert((1, 1, d_ff)),
                by_expert((1, D_MODEL, d_ff)), by_expert((1, 1, d_ff)),
                by_expert((1, d_ff, D_MODEL)), by_expert((1, 1, D_MODEL)),
            ],
            out_specs=pl.BlockSpec((bm, D_MODEL), lambda i, be, nu: (i, 0)),
            scratch_shapes=[
                pltpu.VMEM((D_MODEL, d_ff), jnp.bfloat16),
                pltpu.VMEM((D_MODEL, d_ff), jnp.bfloat16),
                pltpu.VMEM((d_ff, D_MODEL), jnp.bfloat16),
            ],
        ),
        out_shape=jax.ShapeDtypeStruct((n_rows, D_MODEL), jnp.float32),
        compiler_params=pltpu.CompilerParams(
            dimension_semantics=("arbitrary",), vmem_limit_bytes=VMEM_LIMIT_BYTES),
        name="experts",
    )(block_e, n_valid, x_rows, w_gate, b_gate.reshape(N_EXPERTS, 1, d_ff), w_up,
      b_up.reshape(N_EXPERTS, 1, d_ff), w_down, b_down.reshape(N_EXPERTS, 1, D_MODEL))


def _combine_kernel(h_ref, y_ref, gate_ref, g_ref, b_ref, o_ref):
    ffn = gate_ref[:, 0:1] * y_ref[0]
    for k in range(1, TOP_K):
        ffn = ffn + gate_ref[:, k:k + 1] * y_ref[k]
    o_ref[...] = _layer_norm(DEEPNORM_ALPHA * h_ref[...] + ffn, g_ref[...], b_ref[...])


def _combine(h, y_tok, gates, ln2_g, ln2_b):
    t = h.shape[0]
    tm = COMBINE_ROWS
    return pl.pallas_call(
        _combine_kernel,
        grid=(t // tm,),
        in_specs=[
            pl.BlockSpec((tm, D_MODEL), lambda i: (i, 0)),
            pl.BlockSpec((TOP_K, tm, D_MODEL), lambda i: (0, i, 0)),
            pl.BlockSpec((tm, TOP_K), lambda i: (i, 0)),
            pl.BlockSpec((1, D_MODEL), lambda i: (0, 0)),
            pl.BlockSpec((1, D_MODEL), lambda i: (0, 0)),
        ],
        out_specs=pl.BlockSpec((tm, D_MODEL), lambda i: (i, 0)),
        out_shape=jax.ShapeDtypeStruct((t, D_MODEL), jnp.float32),
        compiler_params=pltpu.CompilerParams(
            dimension_semantics=("arbitrary",), vmem_limit_bytes=VMEM_LIMIT_BYTES),
        name="combine",
    )(h, y_tok, gates, ln2_g, ln2_b)


def _layer(x2, seq_len, w_in, b_in, sinks, ln_v_g, ln_v_b, w_spatial, b_spatial, w_out, b_out,
           ln1_g, ln1_b, w_router, b_router, w_gate, b_gate, w_up, b_up, w_down, b_down, ln2_g, ln2_b):
    t = x2.shape[0]
    tk = t * TOP_K
    bm = EXPERT_ROWS
    bf16 = jnp.bfloat16

    w_r_hi = w_router.astype(bf16)
    w_r_lo = (w_router - w_r_hi.astype(jnp.float32)).astype(bf16)
    w_r_cat = jnp.concatenate([w_r_hi, w_r_lo], axis=1)
    b_sp_full = jnp.repeat(b_spatial.T, GMLP_GROUP_DIM, axis=1)

    h, top_idx, gates, rank, counts = _mixer(
        x2, sinks, w_in.astype(bf16), b_in[None], ln_v_g[None], ln_v_b[None], w_spatial, b_sp_full,
        w_out.astype(bf16), b_out[None], ln1_g[None], ln1_b[None], w_r_cat, b_router[None], seq_len)

    counts = counts[0].astype(jnp.int32)
    padded = (counts + bm - 1) // bm * bm
    padded_end = jnp.cumsum(padded)
    padded_start = padded_end - padded
    n_blocks = tk // bm + N_EXPERTS
    n_rows = n_blocks * bm
    dest = padded_start[top_idx] + rank
    block_start = jnp.arange(n_blocks, dtype=jnp.int32) * bm
    block_e = jnp.minimum(
        jnp.sum((padded_end[None, :] <= block_start[:, None]).astype(jnp.int32), axis=1), N_EXPERTS - 1)
    n_valid = jnp.clip(padded_start[block_e] + counts[block_e] - block_start, 0, bm)

    x_rows = _dispatch_rows(dest, h, n_rows)
    y_rows = _experts(block_e, n_valid, x_rows, w_gate, b_gate, w_up, b_up, w_down, b_down)
    y_tok = _gather_rows(dest.T.reshape(-1), y_rows).reshape(TOP_K, t, D_MODEL)
    return _combine(h, y_tok, gates, ln2_g[None], ln2_b[None])


def kernel(x, w_in, b_in, sinks, ln_v_g, ln_v_b, w_spatial, b_spatial, w_out, b_out, ln1_g, ln1_b,
           w_router, b_router, w_gate, b_gate, w_up, b_up, w_down, b_down, ln2_g, ln2_b):
    batch, seq_len, d = x.shape
    x2 = x.reshape(batch * seq_len, d)
    for l in range(DEPTH):
        x2 = _layer(x2, seq_len, w_in[l], b_in[l], sinks[l], ln_v_g[l], ln_v_b[l], w_spatial[l],
                    b_spatial[l], w_out[l], b_out[l], ln1_g[l], ln1_b[l], w_router[l], b_router[l],
                    w_gate[l], b_gate[l], w_up[l], b_up[l], w_down[l], b_down[l], ln2_g[l], ln2_b[l])
    return x2.reshape(batch, seq_len, d)
```

```python
import functools

import jax
import jax.numpy as jnp
from jax import lax
from jax.experimental import pallas as pl
from jax.experimental.pallas import tpu as pltpu
from jax.experimental.pallas import tpu_sc as plsc

D_MODEL = 1024
HEAD_DIM = 64
N_Q_HEADS = 8
N_KV_HEADS = 2
Q_REP = N_Q_HEADS // N_KV_HEADS
ATTN_WIDTH = N_Q_HEADS * HEAD_DIM
KV_WIDTH = N_KV_HEADS * HEAD_DIM
ATTN_BLOCK = 128
N_GMLP_GROUPS = 8
GMLP_WIDTH = D_MODEL - ATTN_WIDTH
GMLP_GROUP_DIM = GMLP_WIDTH // N_GMLP_GROUPS
IN_WIDTH = ATTN_WIDTH + 2 * KV_WIDTH + 2 * GMLP_WIDTH
N_EXPERTS = 32
TOP_K = 4
SWIGLU_LIMIT = 7.0
SWIGLU_ALPHA = 1.702
LN_EPS = 1e-5
DEPTH = 1
DEEPNORM_ALPHA = (2.0 * DEPTH) ** 0.25
NEG_INF = -1e30

MIXER_ROWS = 256
EXPERT_ROWS = 512
GATHER_WINDOW = 64
META_ROWS = 16
COMBINE_ROWS = 256
VMEM_LIMIT_BYTES = 56 * 1024 * 1024

_O_K = ATTN_WIDTH
_O_V = _O_K + KV_WIDTH
_O_U = _O_V + KV_WIDTH
_O_G = _O_U + GMLP_WIDTH


def _pack_bf16_pairs(v):
    n = v.shape[1] // 2
    hi = lax.bitcast_convert_type(v[:, :n].astype(jnp.float32), jnp.int32)
    lo = lax.bitcast_convert_type(v[:, n:].astype(jnp.float32), jnp.int32)
    return hi | lax.shift_right_logical(lo, 16)


def _unpack_bf16_pairs(p):
    hi = lax.bitcast_convert_type(p & jnp.int32(-65536), jnp.float32)
    lo = lax.bitcast_convert_type(lax.shift_left(p, 16), jnp.float32)
    return hi, lo


def _layer_norm(v, g, b):
    mu = jnp.mean(v, axis=-1, keepdims=True)
    vc = v - mu
    var = jnp.mean(vc * vc, axis=-1, keepdims=True)
    return vc * lax.rsqrt(var + LN_EPS) * g + b


def _attention_block(q, kb, vb, sinks_ref, first_block):
    nrow = Q_REP * ATTN_BLOCK
    t_idx = lax.broadcasted_iota(jnp.int32, (nrow, 2 * ATTN_BLOCK), 0) % ATTN_BLOCK
    s_idx = lax.broadcasted_iota(jnp.int32, (nrow, 2 * ATTN_BLOCK), 1)
    diff = t_idx + ATTN_BLOCK - s_idx
    valid = (diff >= 0) & (diff < ATTN_BLOCK) & (s_idx >= first_block * ATTN_BLOCK)
    head_of_row = lax.broadcasted_iota(jnp.int32, (nrow, 1), 0) // ATTN_BLOCK
    outs = []
    for g in range(N_KV_HEADS):
        kg = kb[:, g * HEAD_DIM:(g + 1) * HEAD_DIM]
        vg = vb[:, g * HEAD_DIM:(g + 1) * HEAD_DIM]
        qg = jnp.concatenate(
            [q[:, (g * Q_REP + r) * HEAD_DIM:(g * Q_REP + r + 1) * HEAD_DIM] for r in range(Q_REP)],
            axis=0).astype(jnp.bfloat16)
        s = lax.dot_general(qg, kg, (((1,), (1,)), ((), ())), preferred_element_type=jnp.float32)
        s = jnp.where(valid, s, NEG_INF)
        sink = jnp.zeros((nrow, 1), jnp.float32)
        for r in range(Q_REP):
            sink = jnp.where(head_of_row == r, sinks_ref[g * Q_REP + r], sink)
        m = jnp.maximum(jnp.max(s, axis=-1, keepdims=True), sink)
        p = jnp.exp(s - m)
        denom = jnp.sum(p, axis=-1, keepdims=True) + jnp.exp(sink - m)
        o = jnp.dot(p.astype(jnp.bfloat16), vg, preferred_element_type=jnp.float32) / denom
        outs.extend(o[r * ATTN_BLOCK:(r + 1) * ATTN_BLOCK] for r in range(Q_REP))
    return jnp.concatenate(outs, axis=-1)


def _mixer_kernel(sinks_ref, x_ref, w_in_ref, b_in_ref, lnv_g_ref, lnv_b_ref, grp_avg_ref,
                  w_sp_ref, b_sp_ref, w_out_ref, b_out_ref, ln1_g_ref, ln1_b_ref,
                  w_r_ref, b_r_ref, tri_ref,
                  h_ref, hp_ref, meta_ref, gate_ref, count_ref,
                  kv_prev_ref, *, steps_per_seq):
    i = pl.program_id(0)
    first_step = (i % steps_per_seq) == 0
    tm = x_ref.shape[0]
    n_sub = tm // ATTN_BLOCK

    @pl.when(i == 0)
    def _():
        count_ref[...] = jnp.zeros_like(count_ref)

    @pl.when(first_step)
    def _():
        kv_prev_ref[...] = jnp.zeros_like(kv_prev_ref)

    x = x_ref[...]
    proj = jnp.dot(x.astype(jnp.bfloat16), w_in_ref[...], preferred_element_type=jnp.float32) + b_in_ref[...]

    q_all = proj[:, :_O_K] * (HEAD_DIM ** -0.5)
    k_all = proj[:, _O_K:_O_V].astype(jnp.bfloat16)
    v_all = proj[:, _O_V:_O_U].astype(jnp.bfloat16)
    k_prev = kv_prev_ref[:, :KV_WIDTH]
    v_prev = kv_prev_ref[:, KV_WIDTH:]
    attn_blocks = []
    for sb in range(n_sub):
        rows = slice(sb * ATTN_BLOCK, (sb + 1) * ATTN_BLOCK)
        k_cur, v_cur = k_all[rows], v_all[rows]
        kb = jnp.concatenate([k_prev, k_cur], axis=0)
        vb = jnp.concatenate([v_prev, v_cur], axis=0)
        first_block = jnp.where(first_step, 1, 0) if sb == 0 else 0
        attn_blocks.append(_attention_block(q_all[rows], kb, vb, sinks_ref, first_block))
        k_prev, v_prev = k_cur, v_cur
    kv_prev_ref[:, :KV_WIDTH] = k_prev
    kv_prev_ref[:, KV_WIDTH:] = v_prev
    attn = jnp.concatenate(attn_blocks, axis=0)

    u = jax.nn.gelu(proj[:, _O_U:_O_G])
    gg = jax.nn.gelu(proj[:, _O_G:])
    avg = grp_avg_ref[...]
    mu = jnp.dot(gg.astype(jnp.bfloat16), avg, preferred_element_type=jnp.float32)
    gc = gg - mu
    var = jnp.dot((gc * gc).astype(jnp.bfloat16), avg, preferred_element_type=jnp.float32)
    gn = (gc * lax.rsqrt(var + LN_EPS) * lnv_g_ref[...] + lnv_b_ref[...]).astype(jnp.bfloat16)
    causal = (lax.broadcasted_iota(jnp.int32, (ATTN_BLOCK, ATTN_BLOCK), 0)
              >= lax.broadcasted_iota(jnp.int32, (ATTN_BLOCK, ATTN_BLOCK), 1))
    w_sp = [jnp.where(causal, w_sp_ref[g], 0.0).astype(jnp.bfloat16) for g in range(N_GMLP_GROUPS)]
    mixed_chunks = []
    for c in range(n_sub):
        rows = slice(c * ATTN_BLOCK, (c + 1) * ATTN_BLOCK)
        pieces = [
            jnp.dot(w_sp[g], gn[rows, g * GMLP_GROUP_DIM:(g + 1) * GMLP_GROUP_DIM],
                    preferred_element_type=jnp.float32)
            for g in range(N_GMLP_GROUPS)]
        mixed_chunks.append(jnp.concatenate(pieces, axis=-1) + b_sp_ref[...])
    sgu = u * jnp.concatenate(mixed_chunks, axis=0)

    mix = (jnp.dot(attn.astype(jnp.bfloat16), w_out_ref[:ATTN_WIDTH, :], preferred_element_type=jnp.float32)
           + jnp.dot(sgu.astype(jnp.bfloat16), w_out_ref[ATTN_WIDTH:, :], preferred_element_type=jnp.float32)
           + b_out_ref[...])
    h = _layer_norm(DEEPNORM_ALPHA * x + mix, ln1_g_ref[...], ln1_b_ref[...])
    h_ref[...] = h

    h_hi = h.astype(jnp.bfloat16)
    hp_ref[...] = _pack_bf16_pairs(h_hi)
    h_lo = (h - h_hi.astype(jnp.float32)).astype(jnp.bfloat16)
    part = jnp.dot(h_hi, w_r_ref[...], preferred_element_type=jnp.float32)
    logits = (part[:, :N_EXPERTS] + part[:, N_EXPERTS:]
              + jnp.dot(h_lo, w_r_ref[:, :N_EXPERTS], preferred_element_type=jnp.float32)
              + b_r_ref[...])
    lane = lax.broadcasted_iota(jnp.int32, (tm, N_EXPERTS), 1).astype(jnp.float32)
    work = logits
    vals, idxs = [], []
    onehot = jnp.zeros((tm, N_EXPERTS), jnp.float32)
    for _ in range(TOP_K):
        m = jnp.max(work, axis=-1, keepdims=True)
        idx = jnp.min(jnp.where(work == m, lane, float(N_EXPERTS)), axis=-1, keepdims=True)
        sel = lane == idx
        vals.append(m)
        idxs.append(idx)
        onehot = jnp.where(sel, 1.0, onehot)
        work = jnp.where(sel, -jnp.inf, work)
    exps = [jnp.exp(v - vals[0]) for v in vals]
    denom = exps[0] + exps[1] + exps[2] + exps[3]
    before = jnp.dot(tri_ref[...], onehot.astype(jnp.bfloat16), preferred_element_type=jnp.float32) + count_ref[...]
    ranks = [jnp.sum(jnp.where(lane == idxs[k], before, 0.0), axis=-1, keepdims=True) for k in range(TOP_K)]
    gates = [e / denom for e in exps]
    for k in range(TOP_K):
        gate_ref[:, k:k + 1] = gates[k]
    meta_lane = lax.broadcasted_iota(jnp.int32, (tm, 128), 1)
    meta = jnp.zeros((tm, 128), jnp.float32)
    for c, col in enumerate(idxs + ranks + gates):
        meta = jnp.where(meta_lane == c, col, meta)
    meta_ref[...] = meta.T[:META_ROWS]
    count_ref[...] += jnp.sum(onehot, axis=0, keepdims=True)


def _mixer(x2, sinks, w_in, b_in, lnv_g, lnv_b, w_sp, b_sp_full, w_out, b_out, ln1_g, ln1_b,
           w_r_cat, b_r, seq_len):
    t = x2.shape[0]
    tm = MIXER_ROWS
    grp = jnp.arange(GMLP_WIDTH) // GMLP_GROUP_DIM
    grp_avg = jnp.where(grp[:, None] == grp[None, :], 1.0 / GMLP_GROUP_DIM, 0.0).astype(jnp.bfloat16)
    tri = (jnp.arange(tm)[:, None] > jnp.arange(tm)[None, :]).astype(jnp.bfloat16)

    def full(shape):
        return pl.BlockSpec(shape, lambda i: (0,) * len(shape))

    return pl.pallas_call(
        functools.partial(_mixer_kernel, steps_per_seq=seq_len // tm),
        grid=(t // tm,),
        in_specs=[
            pl.BlockSpec(memory_space=pltpu.SMEM),
            pl.BlockSpec((tm, D_MODEL), lambda i: (i, 0)),
            full((D_MODEL, IN_WIDTH)), full((1, IN_WIDTH)),
            full((1, GMLP_WIDTH)), full((1, GMLP_WIDTH)), full((GMLP_WIDTH, GMLP_WIDTH)),
            full((N_GMLP_GROUPS, ATTN_BLOCK, ATTN_BLOCK)), full((ATTN_BLOCK, GMLP_WIDTH)),
            full((D_MODEL, D_MODEL)), full((1, D_MODEL)), full((1, D_MODEL)), full((1, D_MODEL)),
            full((D_MODEL, 2 * N_EXPERTS)), full((1, N_EXPERTS)), full((tm, tm)),
        ],
        out_specs=[
            pl.BlockSpec((tm, D_MODEL), lambda i: (i, 0)),
            pl.BlockSpec((tm, D_MODEL // 2), lambda i: (i, 0)),
            pl.BlockSpec((META_ROWS, tm), lambda i: (0, i)),
            pl.BlockSpec((tm, TOP_K), lambda i: (i, 0)),
            pl.BlockSpec((1, N_EXPERTS), lambda i: (0, 0)),
        ],
        out_shape=[
            jax.ShapeDtypeStruct((t, D_MODEL), jnp.float32),
            jax.ShapeDtypeStruct((t, D_MODEL // 2), jnp.int32),
            jax.ShapeDtypeStruct((META_ROWS, t), jnp.float32),
            jax.ShapeDtypeStruct((t, TOP_K), jnp.float32),
            jax.ShapeDtypeStruct((1, N_EXPERTS), jnp.float32),
        ],
        scratch_shapes=[pltpu.VMEM((ATTN_BLOCK, 2 * KV_WIDTH), jnp.bfloat16)],
        compiler_params=pltpu.CompilerParams(
            dimension_semantics=("arbitrary",), vmem_limit_bytes=VMEM_LIMIT_BYTES),
        name="mixer",
    )(sinks, x2, w_in, b_in, lnv_g, lnv_b, grp_avg, w_sp, b_sp_full, w_out, b_out, ln1_g, ln1_b,
      w_r_cat, b_r, tri)


def _gather_rows(idx, src):
    n = idx.shape[0]
    width = src.shape[1]
    win = GATHER_WINDOW
    sc = plsc.get_sparse_core_info()
    n_workers = sc.num_cores * sc.num_subcores
    per_worker = n // n_workers
    n_pairs = per_worker // (2 * win)
    assert n_pairs * 2 * win * n_workers == n
    mesh = plsc.VectorSubcoreMesh(core_axis_name="core", subcore_axis_name="subcore")

    @functools.partial(
        pl.kernel, out_type=jax.ShapeDtypeStruct((n, width), src.dtype), mesh=mesh,
        scratch_types=[pltpu.VMEM((per_worker,), jnp.int32), pltpu.VMEM((2, win, width), src.dtype),
                       pltpu.SemaphoreType.DMA((2,)), pltpu.SemaphoreType.DMA((2,))],
        name="gather_rows")
    def gather(src_hbm, idx_hbm, out_hbm, idx_v, rows_v, fetch_sem, store_sem):
        worker = lax.axis_index("subcore") * sc.num_cores + lax.axis_index("core")
        base = worker * per_worker
        pltpu.sync_copy(idx_hbm.at[pl.ds(base, per_worker)], idx_v)

        def fetch(chunk, buf):
            return pltpu.make_async_copy(src_hbm.at[idx_v.at[pl.ds(chunk * win, win)]], rows_v.at[buf],
                                         fetch_sem.at[buf])

        def store(chunk, buf):
            return pltpu.make_async_copy(rows_v.at[buf], out_hbm.at[pl.ds(base + chunk * win, win)],
                                         store_sem.at[buf])

        @pl.loop(0, n_pairs)
        def _(p):
            for buf in range(2):
                @pl.when(p > 0)
                def _():
                    store(2 * p - 2 + buf, buf).wait()
                fetch(2 * p + buf, buf).start()
            for buf in range(2):
                fetch(2 * p + buf, buf).wait()
                store(2 * p + buf, buf).start()

        for buf in range(2):
            store(2 * n_pairs - 2 + buf, buf).wait()

    return gather(src, idx)


def _dispatch_rows(dest_t, src, n_rows):
    t, width = src.shape
    win = GATHER_WINDOW
    sc = plsc.get_sparse_core_info()
    n_workers = sc.num_cores * sc.num_subcores
    per_worker = t // n_workers
    n_chunks = per_worker // win
    n_pairs = n_chunks // 2
    assert n_pairs * 2 * win * n_workers == t
    idx = dest_t.reshape(TOP_K, n_workers, n_chunks, win).transpose(1, 0, 2, 3)
    idx = idx.reshape(n_workers, TOP_K * n_chunks, win)
    mesh = plsc.VectorSubcoreMesh(core_axis_name="core", subcore_axis_name="subcore")

    @functools.partial(
        pl.kernel, out_type=jax.ShapeDtypeStruct((n_rows, width), src.dtype), mesh=mesh,
        scratch_types=[pltpu.VMEM((TOP_K * n_chunks, win), jnp.int32), pltpu.VMEM((2, win, width), src.dtype),
                       pltpu.SemaphoreType.DMA((2,)), pltpu.SemaphoreType.DMA((2,))],
        name="dispatch_rows")
    def dispatch(src_hbm, idx_hbm, out_hbm, idx_v, rows_v, fetch_sem, store_sem):
        worker = lax.axis_index("subcore") * sc.num_cores + lax.axis_index("core")
        base = worker * per_worker
        pltpu.sync_copy(idx_hbm.at[worker], idx_v)

        def fetch(chunk, buf):
            return pltpu.make_async_copy(src_hbm.at[pl.ds(base + chunk * win, win)], rows_v.at[buf],
                                         fetch_sem.at[buf])

        def store(chunk, k, buf):
            return pltpu.make_async_copy(rows_v.at[buf], out_hbm.at[idx_v.at[k * n_chunks + chunk]],
                                         store_sem.at[buf])

        @pl.loop(0, n_pairs)
        def _(p):
            for buf in range(2):
                @pl.when(p > 0)
                def _():
                    for k in range(TOP_K):
                        store(2 * p - 2 + buf, k, buf).wait()
                fetch(2 * p + buf, buf).start()
            for buf in range(2):
                fetch(2 * p + buf, buf).wait()
                for k in range(TOP_K):
                    store(2 * p + buf, k, buf).start()

        for buf in range(2):
            for k in range(TOP_K):
                store(2 * n_pairs - 2 + buf, k, buf).wait()

    return dispatch(src, idx)


def _expert_kernel(block_e_ref, n_valid_ref, x_ref, wg_ref, bg_ref, wu_ref, bu_ref, wd_ref, bd_ref,
                   y_ref, wg_bf, wu_bf, wd_bf):
    i = pl.program_id(0)
    changed = jnp.logical_or(i == 0, block_e_ref[i] != block_e_ref[jnp.maximum(i - 1, 0)])
    n_valid = n_valid_ref[i]

    @pl.when(changed)
    def _():
        wg_bf[...] = wg_ref[0].astype(jnp.bfloat16)
        wu_bf[...] = wu_ref[0].astype(jnp.bfloat16)
        wd_bf[...] = wd_ref[0].astype(jnp.bfloat16)

    @pl.when(n_valid > 0)
    def _():
        row = lax.broadcasted_iota(jnp.int32, (x_ref.shape[0], 1), 0)
        x_hi, x_lo = _unpack_bf16_pairs(jnp.where(row < n_valid, x_ref[...], 0))
        xb = jnp.concatenate([x_hi, x_lo], axis=1).astype(jnp.bfloat16)
        gt = jnp.minimum(jnp.dot(xb, wg_bf[...], preferred_element_type=jnp.float32) + bg_ref[0], SWIGLU_LIMIT)
        up = jnp.clip(jnp.dot(xb, wu_bf[...], preferred_element_type=jnp.float32) + bu_ref[0],
                      -SWIGLU_LIMIT, SWIGLU_LIMIT)
        hid = gt * jax.nn.sigmoid(SWIGLU_ALPHA * gt) * (up + 1.0)
        y = jnp.dot(hid.astype(jnp.bfloat16), wd_bf[...], preferred_element_type=jnp.float32) + bd_ref[0]
        y_ref[...] = _pack_bf16_pairs(y.astype(jnp.bfloat16))

    @pl.when(n_valid == 0)
    def _():
        y_ref[...] = jnp.zeros_like(y_ref)


def _experts(block_e, n_valid, x_rows, w_gate, b_gate, w_up, b_up, w_down, b_down):
    n_rows = x_rows.shape[0]
    bm = EXPERT_ROWS
    d_ff = w_gate.shape[2]

    def by_expert(shape):
        return pl.BlockSpec(shape, lambda i, be, nu: (be[i],) + (0,) * (len(shape) - 1))

    return pl.pallas_call(
        _expert_kernel,
        grid_spec=pltpu.PrefetchScalarGridSpec(
            num_scalar_prefetch=2,
            grid=(n_rows // bm,),
            in_specs=[
                pl.BlockSpec((bm, D_MODEL // 2), lambda i, be, nu: (i, 0)),
                by_expert((1, D_MODEL, d_ff)), by_expert((1, 1, d_ff)),
                by_expert((1, D_MODEL, d_ff)), by_expert((1, 1, d_ff)),
                by_expert((1, d_ff, D_MODEL)), by_expert((1, 1, D_MODEL)),
            ],
            out_specs=pl.BlockSpec((bm, D_MODEL // 2), lambda i, be, nu: (i, 0)),
            scratch_shapes=[
                pltpu.VMEM((D_MODEL, d_ff), jnp.bfloat16),
                pltpu.VMEM((D_MODEL, d_ff), jnp.bfloat16),
                pltpu.VMEM((d_ff, D_MODEL), jnp.bfloat16),
            ],
        ),
        out_shape=jax.ShapeDtypeStruct((n_rows, D_MODEL // 2), jnp.int32),
        compiler_params=pltpu.CompilerParams(
            dimension_semantics=("arbitrary",), vmem_limit_bytes=VMEM_LIMIT_BYTES),
        name="experts",
    )(block_e, n_valid, x_rows, w_gate, b_gate.reshape(N_EXPERTS, 1, d_ff), w_up,
      b_up.reshape(N_EXPERTS, 1, d_ff), w_down, b_down.reshape(N_EXPERTS, 1, D_MODEL))


def _combine_kernel(h_ref, y_ref, gate_ref, g_ref, b_ref, o_ref):
    ffn_hi, ffn_lo = 0.0, 0.0
    for k in range(TOP_K):
        y_hi, y_lo = _unpack_bf16_pairs(y_ref[k])
        ffn_hi = ffn_hi + gate_ref[:, k:k + 1] * y_hi
        ffn_lo = ffn_lo + gate_ref[:, k:k + 1] * y_lo
    ffn = jnp.concatenate([ffn_hi, ffn_lo], axis=1)
    o_ref[...] = _layer_norm(DEEPNORM_ALPHA * h_ref[...] + ffn, g_ref[...], b_ref[...])


def _combine(h, y_tok, gates, ln2_g, ln2_b):
    t = h.shape[0]
    tm = COMBINE_ROWS
    return pl.pallas_call(
        _combine_kernel,
        grid=(t // tm,),
        in_specs=[
            pl.BlockSpec((tm, D_MODEL), lambda i: (i, 0)),
            pl.BlockSpec((TOP_K, tm, D_MODEL // 2), lambda i: (0, i, 0)),
            pl.BlockSpec((tm, TOP_K), lambda i: (i, 0)),
            pl.BlockSpec((1, D_MODEL), lambda i: (0, 0)),
            pl.BlockSpec((1, D_MODEL), lambda i: (0, 0)),
        ],
        out_specs=pl.BlockSpec((tm, D_MODEL), lambda i: (i, 0)),
        out_shape=jax.ShapeDtypeStruct((t, D_MODEL), jnp.float32),
        compiler_params=pltpu.CompilerParams(
            dimension_semantics=("arbitrary",), vmem_limit_bytes=VMEM_LIMIT_BYTES),
        name="combine",
    )(h, y_tok, gates, ln2_g, ln2_b)


def _layer(x2, seq_len, w_in, b_in, sinks, ln_v_g, ln_v_b, w_spatial, b_spatial, w_out, b_out,
           ln1_g, ln1_b, w_router, b_router, w_gate, b_gate, w_up, b_up, w_down, b_down, ln2_g, ln2_b):
    t = x2.shape[0]
    tk = t * TOP_K
    bm = EXPERT_ROWS
    bf16 = jnp.bfloat16

    w_r_hi = w_router.astype(bf16)
    w_r_lo = (w_router - w_r_hi.astype(jnp.float32)).astype(bf16)
    w_r_cat = jnp.concatenate([w_r_hi, w_r_lo], axis=1)
    b_sp_full = jnp.repeat(b_spatial.T, GMLP_GROUP_DIM, axis=1)

    h, h_packed, meta, gates, counts = _mixer(
        x2, sinks, w_in.astype(bf16), b_in[None], ln_v_g[None], ln_v_b[None], w_spatial, b_sp_full,
        w_out.astype(bf16), b_out[None], ln1_g[None], ln1_b[None], w_r_cat, b_router[None], seq_len)

    counts = counts[0].astype(jnp.int32)
    padded = (counts + bm - 1) // bm * bm
    padded_end = jnp.cumsum(padded)
    padded_start = padded_end - padded
    n_blocks = tk // bm + N_EXPERTS
    n_rows = n_blocks * bm
    top_idx_t = meta[:TOP_K].astype(jnp.int32)
    rank_t = meta[TOP_K:2 * TOP_K].astype(jnp.int32)
    dest_t = padded_start[top_idx_t] + rank_t
    block_start = jnp.arange(n_blocks, dtype=jnp.int32) * bm
    block_e = jnp.minimum(
        jnp.sum((padded_end[None, :] <= block_start[:, None]).astype(jnp.int32), axis=1), N_EXPERTS - 1)
    n_valid = jnp.clip(padded_start[block_e] + counts[block_e] - block_start, 0, bm)

    x_rows = _dispatch_rows(dest_t, h_packed, n_rows)
    y_rows = _experts(block_e, n_valid, x_rows, w_gate, b_gate, w_up, b_up, w_down, b_down)
    y_tok = _gather_rows(dest_t.reshape(-1), y_rows).reshape(TOP_K, t, D_MODEL // 2)
    return _combine(h, y_tok, gates, ln2_g[None], ln2_b[None])


def kernel(x, w_in, b_in, sinks, ln_v_g, ln_v_b, w_spatial, b_spatial, w_out, b_out, ln1_g, ln1_b,
           w_router, b_router, w_gate, b_gate, w_up, b_up, w_down, b_down, ln2_g, ln2_b):
    batch, seq_len, d = x.shape
    x2 = x.reshape(batch * seq_len, d)
    for l in range(DEPTH):
        x2 = _layer(x2, seq_len, w_in[l], b_in[l], sinks[l], ln_v_g[l], ln_v_b[l], w_spatial[l],
                    b_spatial[l], w_out[l], b_out[l], ln1_g[l], ln1_b[l], w_router[l], b_router[l],
                    w_gate[l], b_gate[l], w_up[l], b_up[l], w_down[l], b_down[l], ln2_g[l], ln2_b[l])
    return x2.reshape(batch, seq_len, d)
```

```python
import functools

import jax
import jax.numpy as jnp
from jax import lax
from jax.experimental import pallas as pl
from jax.experimental.pallas import tpu as pltpu
from jax.experimental.pallas import tpu_sc as plsc

D_MODEL = 1024
HEAD_DIM = 64
N_Q_HEADS = 8
N_KV_HEADS = 2
Q_REP = N_Q_HEADS // N_KV_HEADS
ATTN_WIDTH = N_Q_HEADS * HEAD_DIM
KV_WIDTH = N_KV_HEADS * HEAD_DIM
ATTN_BLOCK = 128
N_GMLP_GROUPS = 8
GMLP_WIDTH = D_MODEL - ATTN_WIDTH
GMLP_GROUP_DIM = GMLP_WIDTH // N_GMLP_GROUPS
IN_WIDTH = ATTN_WIDTH + 2 * KV_WIDTH + 2 * GMLP_WIDTH
N_EXPERTS = 32
TOP_K = 4
SWIGLU_LIMIT = 7.0
SWIGLU_ALPHA = 1.702
LN_EPS = 1e-5
DEPTH = 1
DEEPNORM_ALPHA = (2.0 * DEPTH) ** 0.25
NEG_INF = -1e30

MIXER_ROWS = 256
EXPERT_ROWS = 512
GATHER_WINDOW = 64
META_ROWS = 16
COMBINE_ROWS = 256
VMEM_LIMIT_BYTES = 56 * 1024 * 1024

_O_K = ATTN_WIDTH
_O_V = _O_K + KV_WIDTH
_O_U = _O_V + KV_WIDTH
_O_G = _O_U + GMLP_WIDTH


def _pack_bf16_pairs(v):
    n = v.shape[1] // 2
    hi = lax.bitcast_convert_type(v[:, :n].astype(jnp.float32), jnp.int32)
    lo = lax.bitcast_convert_type(v[:, n:].astype(jnp.float32), jnp.int32)
    return hi | lax.shift_right_logical(lo, 16)


def _unpack_bf16_pairs(p):
    hi = lax.bitcast_convert_type(p & jnp.int32(-65536), jnp.float32)
    lo = lax.bitcast_convert_type(lax.shift_left(p, 16), jnp.float32)
    return hi, lo


def _layer_norm(v, g, b):
    mu = jnp.mean(v, axis=-1, keepdims=True)
    vc = v - mu
    var = jnp.mean(vc * vc, axis=-1, keepdims=True)
    return vc * lax.rsqrt(var + LN_EPS) * g + b


def _attention_block(q, kb, vb, sinks_ref, first_block):
    nrow = Q_REP * ATTN_BLOCK
    t_idx = lax.broadcasted_iota(jnp.int32, (nrow, 2 * ATTN_BLOCK), 0) % ATTN_BLOCK
    s_idx = lax.broadcasted_iota(jnp.int32, (nrow, 2 * ATTN_BLOCK), 1)
    diff = t_idx + ATTN_BLOCK - s_idx
    valid = (diff >= 0) & (diff < ATTN_BLOCK) & (s_idx >= first_block * ATTN_BLOCK)
    head_of_row = lax.broadcasted_iota(jnp.int32, (nrow, 1), 0) // ATTN_BLOCK
    outs = []
    for g in range(N_KV_HEADS):
        kg = kb[:, g * HEAD_DIM:(g + 1) * HEAD_DIM]
        vg = vb[:, g * HEAD_DIM:(g + 1) * HEAD_DIM]
        qg = jnp.concatenate(
            [q[:, (g * Q_REP + r) * HEAD_DIM:(g * Q_REP + r + 1) * HEAD_DIM] for r in range(Q_REP)],
            axis=0).astype(jnp.bfloat16)
        s = lax.dot_general(qg, kg, (((1,), (1,)), ((), ())), preferred_element_type=jnp.float32)
        s = jnp.where(valid, s, NEG_INF)
        sink = jnp.zeros((nrow, 1), jnp.float32)
        for r in range(Q_REP):
            sink = jnp.where(head_of_row == r, sinks_ref[g * Q_REP + r], sink)
        m = jnp.maximum(jnp.max(s, axis=-1, keepdims=True), sink)
        p = jnp.exp(s - m)
        denom = jnp.sum(p, axis=-1, keepdims=True) + jnp.exp(sink - m)
        o = jnp.dot(p.astype(jnp.bfloat16), vg, preferred_element_type=jnp.float32) / denom
        outs.extend(o[r * ATTN_BLOCK:(r + 1) * ATTN_BLOCK] for r in range(Q_REP))
    return jnp.concatenate(outs, axis=-1)


def _mixer_kernel(sinks_ref, x_ref, w_in_ref, b_in_ref, lnv_g_ref, lnv_b_ref, grp_avg_ref,
                  w_sp_ref, b_sp_ref, w_out_ref, b_out_ref, ln1_g_ref, ln1_b_ref,
                  w_r_ref, b_r_ref, tri_ref,
                  h_ref, hp_ref, meta_ref, gate_ref, count_ref,
                  kv_prev_ref, *, steps_per_seq):
    i = pl.program_id(0)
    first_step = (i % steps_per_seq) == 0
    tm = x_ref.shape[0]
    n_sub = tm // ATTN_BLOCK

    @pl.when(i == 0)
    def _():
        count_ref[...] = jnp.zeros_like(count_ref)

    @pl.when(first_step)
    def _():
        kv_prev_ref[...] = jnp.zeros_like(kv_prev_ref)

    x = x_ref[...]
    proj = jnp.dot(x.astype(jnp.bfloat16), w_in_ref[...], preferred_element_type=jnp.float32) + b_in_ref[...]

    q_all = proj[:, :_O_K] * (HEAD_DIM ** -0.5)
    k_all = proj[:, _O_K:_O_V].astype(jnp.bfloat16)
    v_all = proj[:, _O_V:_O_U].astype(jnp.bfloat16)
    k_prev = kv_prev_ref[:, :KV_WIDTH]
    v_prev = kv_prev_ref[:, KV_WIDTH:]
    attn_blocks = []
    for sb in range(n_sub):
        rows = slice(sb * ATTN_BLOCK, (sb + 1) * ATTN_BLOCK)
        k_cur, v_cur = k_all[rows], v_all[rows]
        kb = jnp.concatenate([k_prev, k_cur], axis=0)
        vb = jnp.concatenate([v_prev, v_cur], axis=0)
        first_block = jnp.where(first_step, 1, 0) if sb == 0 else 0
        attn_blocks.append(_attention_block(q_all[rows], kb, vb, sinks_ref, first_block))
        k_prev, v_prev = k_cur, v_cur
    kv_prev_ref[:, :KV_WIDTH] = k_prev
    kv_prev_ref[:, KV_WIDTH:] = v_prev
    attn = jnp.concatenate(attn_blocks, axis=0)

    u = jax.nn.gelu(proj[:, _O_U:_O_G])
    gg = jax.nn.gelu(proj[:, _O_G:])
    avg = grp_avg_ref[...]
    mu = jnp.dot(gg.astype(jnp.bfloat16), avg, preferred_element_type=jnp.float32)
    gc = gg - mu
    var = jnp.dot((gc * gc).astype(jnp.bfloat16), avg, preferred_element_type=jnp.float32)
    gn = (gc * lax.rsqrt(var + LN_EPS) * lnv_g_ref[...] + lnv_b_ref[...]).astype(jnp.bfloat16)
    causal = (lax.broadcasted_iota(jnp.int32, (ATTN_BLOCK, ATTN_BLOCK), 0)
              >= lax.broadcasted_iota(jnp.int32, (ATTN_BLOCK, ATTN_BLOCK), 1))
    w_sp = [jnp.where(causal, w_sp_ref[g], 0.0).astype(jnp.bfloat16) for g in range(N_GMLP_GROUPS)]
    mixed_chunks = []
    for c in range(n_sub):
        rows = slice(c * ATTN_BLOCK, (c + 1) * ATTN_BLOCK)
        pieces = [
            jnp.dot(w_sp[g], gn[rows, g * GMLP_GROUP_DIM:(g + 1) * GMLP_GROUP_DIM],
                    preferred_element_type=jnp.float32)
            for g in range(N_GMLP_GROUPS)]
        mixed_chunks.append(jnp.concatenate(pieces, axis=-1) + b_sp_ref[...])
    sgu = u * jnp.concatenate(mixed_chunks, axis=0)

    mix = (jnp.dot(attn.astype(jnp.bfloat16), w_out_ref[:ATTN_WIDTH, :], preferred_element_type=jnp.float32)
           + jnp.dot(sgu.astype(jnp.bfloat16), w_out_ref[ATTN_WIDTH:, :], preferred_element_type=jnp.float32)
           + b_out_ref[...])
    h = _layer_norm(DEEPNORM_ALPHA * x + mix, ln1_g_ref[...], ln1_b_ref[...])
    h_ref[...] = h

    h_hi = h.astype(jnp.bfloat16)
    hp_ref[...] = _pack_bf16_pairs(h_hi)
    h_lo = (h - h_hi.astype(jnp.float32)).astype(jnp.bfloat16)
    part = jnp.dot(h_hi, w_r_ref[...], preferred_element_type=jnp.float32)
    logits = (part[:, :N_EXPERTS] + part[:, N_EXPERTS:]
              + jnp.dot(h_lo, w_r_ref[:, :N_EXPERTS], preferred_element_type=jnp.float32)
              + b_r_ref[...])
    lane = lax.broadcasted_iota(jnp.int32, (tm, N_EXPERTS), 1).astype(jnp.float32)
    work = logits
    vals, idxs = [], []
    onehot = jnp.zeros((tm, N_EXPERTS), jnp.float32)
    for _ in range(TOP_K):
        m = jnp.max(work, axis=-1, keepdims=True)
        idx = jnp.min(jnp.where(work == m, lane, float(N_EXPERTS)), axis=-1, keepdims=True)
        sel = lane == idx
        vals.append(m)
        idxs.append(idx)
        onehot = jnp.where(sel, 1.0, onehot)
        work = jnp.where(sel, -jnp.inf, work)
    exps = [jnp.exp(v - vals[0]) for v in vals]
    denom = exps[0] + exps[1] + exps[2] + exps[3]
    before = jnp.dot(tri_ref[...], onehot.astype(jnp.bfloat16), preferred_element_type=jnp.float32) + count_ref[...]
    ranks = [jnp.sum(jnp.where(lane == idxs[k], before, 0.0), axis=-1, keepdims=True) for k in range(TOP_K)]
    gates = [e / denom for e in exps]
    for k in range(TOP_K):
        gate_ref[:, k:k + 1] = gates[k]
    meta_lane = lax.broadcasted_iota(jnp.int32, (tm, 128), 1)
    meta = jnp.zeros((tm, 128), jnp.float32)
    for c, col in enumerate(idxs + ranks + gates):
        meta = jnp.where(meta_lane == c, col, meta)
    meta_ref[...] = meta.T[:META_ROWS]
    count_ref[...] += jnp.sum(onehot, axis=0, keepdims=True)


def _mixer(x2, sinks, w_in, b_in, lnv_g, lnv_b, w_sp, b_sp_full, w_out, b_out, ln1_g, ln1_b,
           w_r_cat, b_r, seq_len):
    t = x2.shape[0]
    tm = MIXER_ROWS
    grp = jnp.arange(GMLP_WIDTH) // GMLP_GROUP_DIM
    grp_avg = jnp.where(grp[:, None] == grp[None, :], 1.0 / GMLP_GROUP_DIM, 0.0).astype(jnp.bfloat16)
    tri = (jnp.arange(tm)[:, None] > jnp.arange(tm)[None, :]).astype(jnp.bfloat16)

    def full(shape):
        return pl.BlockSpec(shape, lambda i: (0,) * len(shape))

    return pl.pallas_call(
        functools.partial(_mixer_kernel, steps_per_seq=seq_len // tm),
        grid=(t // tm,),
        in_specs=[
            pl.BlockSpec(memory_space=pltpu.SMEM),
            pl.BlockSpec((tm, D_MODEL), lambda i: (i, 0)),
            full((D_MODEL, IN_WIDTH)), full((1, IN_WIDTH)),
            full((1, GMLP_WIDTH)), full((1, GMLP_WIDTH)), full((GMLP_WIDTH, GMLP_WIDTH)),
            full((N_GMLP_GROUPS, ATTN_BLOCK, ATTN_BLOCK)), full((ATTN_BLOCK, GMLP_WIDTH)),
            full((D_MODEL, D_MODEL)), full((1, D_MODEL)), full((1, D_MODEL)), full((1, D_MODEL)),
            full((D_MODEL, 2 * N_EXPERTS)), full((1, N_EXPERTS)), full((tm, tm)),
        ],
        out_specs=[
            pl.BlockSpec((tm, D_MODEL), lambda i: (i, 0)),
            pl.BlockSpec((tm, D_MODEL // 2), lambda i: (i, 0)),
            pl.BlockSpec((META_ROWS, tm), lambda i: (0, i)),
            pl.BlockSpec((tm, TOP_K), lambda i: (i, 0)),
            pl.BlockSpec((1, N_EXPERTS), lambda i: (0, 0)),
        ],
        out_shape=[
            jax.ShapeDtypeStruct((t, D_MODEL), jnp.float32),
            jax.ShapeDtypeStruct((t, D_MODEL // 2), jnp.int32),
            jax.ShapeDtypeStruct((META_ROWS, t), jnp.float32),
            jax.ShapeDtypeStruct((t, TOP_K), jnp.float32),
            jax.ShapeDtypeStruct((1, N_EXPERTS), jnp.float32),
        ],
        scratch_shapes=[pltpu.VMEM((ATTN_BLOCK, 2 * KV_WIDTH), jnp.bfloat16)],
        compiler_params=pltpu.CompilerParams(
            dimension_semantics=("arbitrary",), vmem_limit_bytes=VMEM_LIMIT_BYTES),
        name="mixer",
    )(sinks, x2, w_in, b_in, lnv_g, lnv_b, grp_avg, w_sp, b_sp_full, w_out, b_out, ln1_g, ln1_b,
      w_r_cat, b_r, tri)


def _gather_rows(idx, src):
    n = idx.shape[0]
    width = src.shape[1]
    win = GATHER_WINDOW
    sc = plsc.get_sparse_core_info()
    n_workers = sc.num_cores * sc.num_subcores
    per_worker = n // n_workers
    n_pairs = per_worker // (2 * win)
    assert n_pairs * 2 * win * n_workers == n
    mesh = plsc.VectorSubcoreMesh(core_axis_name="core", subcore_axis_name="subcore")

    @functools.partial(
        pl.kernel, out_type=jax.ShapeDtypeStruct((n, width), src.dtype), mesh=mesh,
        scratch_types=[pltpu.VMEM((per_worker,), jnp.int32), pltpu.VMEM((2, win, width), src.dtype),
                       pltpu.SemaphoreType.DMA((2,)), pltpu.SemaphoreType.DMA((2,))],
        name="gather_rows")
    def gather(src_hbm, idx_hbm, out_hbm, idx_v, rows_v, fetch_sem, store_sem):
        worker = lax.axis_index("subcore") * sc.num_cores + lax.axis_index("core")
        base = worker * per_worker
        pltpu.sync_copy(idx_hbm.at[pl.ds(base, per_worker)], idx_v)

        def fetch(chunk, buf):
            return pltpu.make_async_copy(src_hbm.at[idx_v.at[pl.ds(chunk * win, win)]], rows_v.at[buf],
                                         fetch_sem.at[buf])

        def store(chunk, buf):
            return pltpu.make_async_copy(rows_v.at[buf], out_hbm.at[pl.ds(base + chunk * win, win)],
                                         store_sem.at[buf])

        @pl.loop(0, n_pairs)
        def _(p):
            for buf in range(2):
                @pl.when(p > 0)
                def _():
                    store(2 * p - 2 + buf, buf).wait()
                fetch(2 * p + buf, buf).start()
            for buf in range(2):
                fetch(2 * p + buf, buf).wait()
                store(2 * p + buf, buf).start()

        for buf in range(2):
            store(2 * n_pairs - 2 + buf, buf).wait()

    return gather(src, idx)


def _dispatch_rows(dest_t, src, n_rows):
    t, width = src.shape
    win = GATHER_WINDOW
    sc = plsc.get_sparse_core_info()
    n_workers = sc.num_cores * sc.num_subcores
    per_worker = t // n_workers
    n_chunks = per_worker // win
    n_pairs = n_chunks // 2
    assert n_pairs * 2 * win * n_workers == t
    idx = dest_t.reshape(TOP_K, n_workers, n_chunks, win).transpose(1, 0, 2, 3)
    idx = idx.reshape(n_workers, TOP_K * n_chunks, win)
    mesh = plsc.VectorSubcoreMesh(core_axis_name="core", subcore_axis_name="subcore")

    @functools.partial(
        pl.kernel, out_type=jax.ShapeDtypeStruct((n_rows, width), src.dtype), mesh=mesh,
        scratch_types=[pltpu.VMEM((TOP_K * n_chunks, win), jnp.int32), pltpu.VMEM((2, win, width), src.dtype),
                       pltpu.SemaphoreType.DMA((2,)), pltpu.SemaphoreType.DMA((2,))],
        name="dispatch_rows")
    def dispatch(src_hbm, idx_hbm, out_hbm, idx_v, rows_v, fetch_sem, store_sem):
        worker = lax.axis_index("subcore") * sc.num_cores + lax.axis_index("core")
        base = worker * per_worker
        pltpu.sync_copy(idx_hbm.at[worker], idx_v)

        def fetch(chunk, buf):
            return pltpu.make_async_copy(src_hbm.at[pl.ds(base + chunk * win, win)], rows_v.at[buf],
                                         fetch_sem.at[buf])

        def store(chunk, k, buf):
            return pltpu.make_async_copy(rows_v.at[buf], out_hbm.at[idx_v.at[k * n_chunks + chunk]],
                                         store_sem.at[buf])

        @pl.loop(0, n_pairs)
        def _(p):
            for buf in range(2):
                @pl.when(p > 0)
                def _():
                    for k in range(TOP_K):
                        store(2 * p - 2 + buf, k, buf).wait()
                fetch(2 * p + buf, buf).start()
            for buf in range(2):
                fetch(2 * p + buf, buf).wait()
                for k in range(TOP_K):
                    store(2 * p + buf, k, buf).start()

        for buf in range(2):
            for k in range(TOP_K):
                store(2 * n_pairs - 2 + buf, k, buf).wait()

    return dispatch(src, idx)


def _expert_kernel(block_e_ref, n_valid_ref, x_ref, wg_ref, bg_ref, wu_ref, bu_ref, wd_ref, bd_ref,
                   y_ref, wg_bf, wu_bf, wd_bf):
    i = pl.program_id(0)
    changed = jnp.logical_or(i == 0, block_e_ref[i] != block_e_ref[jnp.maximum(i - 1, 0)])
    n_valid = n_valid_ref[i]

    @pl.when(changed)
    def _():
        wg_bf[...] = wg_ref[0].astype(jnp.bfloat16)
        wu_bf[...] = wu_ref[0].astype(jnp.bfloat16)
        wd_bf[...] = wd_ref[0].astype(jnp.bfloat16)

    @pl.when(n_valid > 0)
    def _():
        row = lax.broadcasted_iota(jnp.int32, (x_ref.shape[0], 1), 0)
        x_hi, x_lo = _unpack_bf16_pairs(jnp.where(row < n_valid, x_ref[...], 0))
        xb = jnp.concatenate([x_hi, x_lo], axis=1).astype(jnp.bfloat16)
        gt = jnp.minimum(jnp.dot(xb, wg_bf[...], preferred_element_type=jnp.float32) + bg_ref[0], SWIGLU_LIMIT)
        up = jnp.clip(jnp.dot(xb, wu_bf[...], preferred_element_type=jnp.float32) + bu_ref[0],
                      -SWIGLU_LIMIT, SWIGLU_LIMIT)
        hid = gt * jax.nn.sigmoid(SWIGLU_ALPHA * gt) * (up + 1.0)
        y = jnp.dot(hid.astype(jnp.bfloat16), wd_bf[...], preferred_element_type=jnp.float32) + bd_ref[0]
        y_ref[...] = _pack_bf16_pairs(y.astype(jnp.bfloat16))

    @pl.when(n_valid == 0)
    def _():
        y_ref[...] = jnp.zeros_like(y_ref)


def _experts(block_e, n_valid, x_rows, w_gate, b_gate, w_up, b_up, w_down, b_down):
    n_rows = x_rows.shape[0]
    bm = EXPERT_ROWS
    d_ff = w_gate.shape[2]

    def by_expert(shape):
        return pl.BlockSpec(shape, lambda i, be, nu: (be[i],) + (0,) * (len(shape) - 1))

    return pl.pallas_call(
        _expert_kernel,
        grid_spec=pltpu.PrefetchScalarGridSpec(
            num_scalar_prefetch=2,
            grid=(n_rows // bm,),
            in_specs=[
                pl.BlockSpec((bm, D_MODEL // 2), lambda i, be, nu: (i, 0)),
                by_expert((1, D_MODEL, d_ff)), by_expert((1, 1, d_ff)),
                by_expert((1, D_MODEL, d_ff)), by_expert((1, 1, d_ff)),
                by_expert((1, d_ff, D_MODEL)), by_expert((1, 1, D_MODEL)),
            ],
            out_specs=pl.BlockSpec((bm, D_MODEL // 2), lambda i, be, nu: (i, 0)),
            scratch_shapes=[
                pltpu.VMEM((D_MODEL, d_ff), jnp.bfloat16),
                pltpu.VMEM((D_MODEL, d_ff), jnp.bfloat16),
                pltpu.VMEM((d_ff, D_MODEL), jnp.bfloat16),
            ],
        ),
        out_shape=jax.ShapeDtypeStruct((n_rows, D_MODEL // 2), jnp.int32),
        compiler_params=pltpu.CompilerParams(
            dimension_semantics=("arbitrary",), vmem_limit_bytes=VMEM_LIMIT_BYTES),
        name="experts",
    )(block_e, n_valid, x_rows, w_gate, b_gate.reshape(N_EXPERTS, 1, d_ff), w_up,
      b_up.reshape(N_EXPERTS, 1, d_ff), w_down, b_down.reshape(N_EXPERTS, 1, D_MODEL))


def _combine_kernel(h_ref, y_ref, gate_ref, g_ref, b_ref, o_ref):
    ffn_hi, ffn_lo = 0.0, 0.0
    for k in range(TOP_K):
        y_hi, y_lo = _unpack_bf16_pairs(y_ref[k])
        ffn_hi = ffn_hi + gate_ref[:, k:k + 1] * y_hi
        ffn_lo = ffn_lo + gate_ref[:, k:k + 1] * y_lo
    ffn = jnp.concatenate([ffn_hi, ffn_lo], axis=1)
    o_ref[...] = _layer_norm(DEEPNORM_ALPHA * h_ref[...] + ffn, g_ref[...], b_ref[...])


def _combine(h, y_tok, gates, ln2_g, ln2_b):
    t = h.shape[0]
    tm = COMBINE_ROWS
    return pl.pallas_call(
        _combine_kernel,
        grid=(t // tm,),
        in_specs=[
            pl.BlockSpec((tm, D_MODEL), lambda i: (i, 0)),
            pl.BlockSpec((TOP_K, tm, D_MODEL // 2), lambda i: (0, i, 0)),
            pl.BlockSpec((tm, TOP_K), lambda i: (i, 0)),
            pl.BlockSpec((1, D_MODEL), lambda i: (0, 0)),
            pl.BlockSpec((1, D_MODEL), lambda i: (0, 0)),
        ],
        out_specs=pl.BlockSpec((tm, D_MODEL), lambda i: (i, 0)),
        out_shape=jax.ShapeDtypeStruct((t, D_MODEL), jnp.float32),
        compiler_params=pltpu.CompilerParams(
            dimension_semantics=("arbitrary",), vmem_limit_bytes=VMEM_LIMIT_BYTES),
        name="combine",
    )(h, y_tok, gates, ln2_g, ln2_b)


def _layer(x2, seq_len, w_in, b_in, sinks, ln_v_g, ln_v_b, w_spatial, b_spatial, w_out, b_out,
           ln1_g, ln1_b, w_router, b_router, w_gate, b_gate, w_up, b_up, w_down, b_down, ln2_g, ln2_b):
    t = x2.shape[0]
    tk = t * TOP_K
    bm = EXPERT_ROWS
    bf16 = jnp.bfloat16

    w_r_hi = w_router.astype(bf16)
    w_r_lo = (w_router - w_r_hi.astype(jnp.float32)).astype(bf16)
    w_r_cat = jnp.concatenate([w_r_hi, w_r_lo], axis=1)
    b_sp_full = jnp.repeat(b_spatial.T, GMLP_GROUP_DIM, axis=1)

    h, h_packed, meta, gates, counts = _mixer(
        x2, sinks, w_in.astype(bf16), b_in[None], ln_v_g[None], ln_v_b[None], w_spatial, b_sp_full,
        w_out.astype(bf16), b_out[None], ln1_g[None], ln1_b[None], w_r_cat, b_router[None], seq_len)

    counts = counts[0].astype(jnp.int32)
    experts = jnp.arange(N_EXPERTS, dtype=jnp.int32)
    padded = (counts + bm - 1) // bm * bm
    padded_end = jnp.sum(jnp.where(experts[None, :] <= experts[:, None], padded[None, :], 0), axis=1)
    padded_start = padded_end - padded
    n_blocks = tk // bm + N_EXPERTS
    n_rows = n_blocks * bm
    top_idx_t = meta[:TOP_K].astype(jnp.int32)
    rank_t = meta[TOP_K:2 * TOP_K].astype(jnp.int32)
    dest_t = rank_t + jnp.sum(
        jnp.where(top_idx_t[None] == experts[:, None, None], padded_start[:, None, None], 0), axis=0)
    block_start = jnp.arange(n_blocks, dtype=jnp.int32) * bm
    block_e = jnp.minimum(
        jnp.sum((padded_end[None, :] <= block_start[:, None]).astype(jnp.int32), axis=1), N_EXPERTS - 1)
    valid_end = jnp.sum(jnp.where(block_e[:, None] == experts[None, :], (padded_start + counts)[None, :], 0), axis=1)
    n_valid = jnp.clip(valid_end - block_start, 0, bm)

    x_rows = _dispatch_rows(dest_t, h_packed, n_rows)
    y_rows = _experts(block_e, n_valid, x_rows, w_gate, b_gate, w_up, b_up, w_down, b_down)
    y_tok = _gather_rows(dest_t.reshape(-1), y_rows).reshape(TOP_K, t, D_MODEL // 2)
    return _combine(h, y_tok, gates, ln2_g[None], ln2_b[None])


def kernel(x, w_in, b_in, sinks, ln_v_g, ln_v_b, w_spatial, b_spatial, w_out, b_out, ln1_g, ln1_b,
           w_router, b_router, w_gate, b_gate, w_up, b_up, w_down, b_down, ln2_g, ln2_b):
    batch, seq_len, d = x.shape
    x2 = x.reshape(batch * seq_len, d)
    for l in range(DEPTH):
        x2 = _layer(x2, seq_len, w_in[l], b_in[l], sinks[l], ln_v_g[l], ln_v_b[l], w_spatial[l],
                    b_spatial[l], w_out[l], b_out[l], ln1_g[l], ln1_b[l], w_router[l], b_router[l],
                    w_gate[l], b_gate[l], w_up[l], b_up[l], w_down[l], b_down[l], ln2_g[l], ln2_b[l])
    return x2.reshape(batch, seq_len, d)
```

```python
import functools

import jax
import jax.numpy as jnp
from jax import lax
from jax.experimental import pallas as pl
from jax.experimental.pallas import tpu as pltpu
from jax.experimental.pallas import tpu_sc as plsc

D_MODEL = 1024
HEAD_DIM = 64
N_Q_HEADS = 8
N_KV_HEADS = 2
Q_REP = N_Q_HEADS // N_KV_HEADS
ATTN_WIDTH = N_Q_HEADS * HEAD_DIM
KV_WIDTH = N_KV_HEADS * HEAD_DIM
ATTN_BLOCK = 128
N_GMLP_GROUPS = 8
GMLP_WIDTH = D_MODEL - ATTN_WIDTH
GMLP_GROUP_DIM = GMLP_WIDTH // N_GMLP_GROUPS
IN_WIDTH = ATTN_WIDTH + 2 * KV_WIDTH + 2 * GMLP_WIDTH
N_EXPERTS = 32
TOP_K = 4
SWIGLU_LIMIT = 7.0
SWIGLU_ALPHA = 1.702
LN_EPS = 1e-5
DEPTH = 1
DEEPNORM_ALPHA = (2.0 * DEPTH) ** 0.25
NEG_INF = -1e30

MIXER_ROWS = 256
EXPERT_ROWS = 512
GATHER_WINDOW = 64
META_ROWS = 16
COMBINE_ROWS = 256
VMEM_LIMIT_BYTES = 56 * 1024 * 1024

_O_K = ATTN_WIDTH
_O_V = _O_K + KV_WIDTH
_O_U = _O_V + KV_WIDTH
_O_G = _O_U + GMLP_WIDTH


def _pack_bf16_pairs(v):
    n = v.shape[1] // 2
    hi = lax.bitcast_convert_type(v[:, :n].astype(jnp.float32), jnp.int32)
    lo = lax.bitcast_convert_type(v[:, n:].astype(jnp.float32), jnp.int32)
    return hi | lax.shift_right_logical(lo, 16)


def _unpack_bf16_pairs(p):
    hi = lax.bitcast_convert_type(p & jnp.int32(-65536), jnp.float32)
    lo = lax.bitcast_convert_type(lax.shift_left(p, 16), jnp.float32)
    return hi, lo


def _layer_norm(v, g, b):
    mu = jnp.mean(v, axis=-1, keepdims=True)
    vc = v - mu
    var = jnp.mean(vc * vc, axis=-1, keepdims=True)
    return vc * lax.rsqrt(var + LN_EPS) * g + b


def _attention_block(q, kb, vb, sinks_ref, first_block):
    nrow = Q_REP * ATTN_BLOCK
    t_idx = lax.broadcasted_iota(jnp.int32, (nrow, 2 * ATTN_BLOCK), 0) % ATTN_BLOCK
    s_idx = lax.broadcasted_iota(jnp.int32, (nrow, 2 * ATTN_BLOCK), 1)
    diff = t_idx + ATTN_BLOCK - s_idx
    valid = (diff >= 0) & (diff < ATTN_BLOCK) & (s_idx >= first_block * ATTN_BLOCK)
    head_of_row = lax.broadcasted_iota(jnp.int32, (nrow, 1), 0) // ATTN_BLOCK
    outs = []
    for g in range(N_KV_HEADS):
        kg = kb[:, g * HEAD_DIM:(g + 1) * HEAD_DIM]
        vg = vb[:, g * HEAD_DIM:(g + 1) * HEAD_DIM]
        qg = jnp.concatenate(
            [q[:, (g * Q_REP + r) * HEAD_DIM:(g * Q_REP + r + 1) * HEAD_DIM] for r in range(Q_REP)],
            axis=0).astype(jnp.bfloat16)
        s = lax.dot_general(qg, kg, (((1,), (1,)), ((), ())), preferred_element_type=jnp.float32)
        s = jnp.where(valid, s, NEG_INF)
        sink = jnp.zeros((nrow, 1), jnp.float32)
        for r in range(Q_REP):
            sink = jnp.where(head_of_row == r, sinks_ref[g * Q_REP + r], sink)
        m = jnp.maximum(jnp.max(s, axis=-1, keepdims=True), sink)
        p = jnp.exp(s - m)
        denom = jnp.sum(p, axis=-1, keepdims=True) + jnp.exp(sink - m)
        o = jnp.dot(p.astype(jnp.bfloat16), vg, preferred_element_type=jnp.float32) / denom
        outs.extend(o[r * ATTN_BLOCK:(r + 1) * ATTN_BLOCK] for r in range(Q_REP))
    return jnp.concatenate(outs, axis=-1)


def _mixer_kernel(sinks_ref, x_ref, w_in_ref, b_in_ref, lnv_g_ref, lnv_b_ref, grp_avg_ref,
                  w_sp_ref, b_sp_ref, w_out_ref, b_out_ref, ln1_g_ref, ln1_b_ref,
                  w_r_ref, b_r_ref, tri_ref,
                  h_ref, hp_ref, meta_ref, gate_ref, count_ref,
                  kv_prev_ref, *, steps_per_seq):
    i = pl.program_id(0)
    first_step = (i % steps_per_seq) == 0
    tm = x_ref.shape[0]
    n_sub = tm // ATTN_BLOCK

    @pl.when(i == 0)
    def _():
        count_ref[...] = jnp.zeros_like(count_ref)

    @pl.when(first_step)
    def _():
        kv_prev_ref[...] = jnp.zeros_like(kv_prev_ref)

    x = x_ref[...]
    proj = jnp.dot(x.astype(jnp.bfloat16), w_in_ref[...], preferred_element_type=jnp.float32) + b_in_ref[...]

    q_all = proj[:, :_O_K] * (HEAD_DIM ** -0.5)
    k_all = proj[:, _O_K:_O_V].astype(jnp.bfloat16)
    v_all = proj[:, _O_V:_O_U].astype(jnp.bfloat16)
    k_prev = kv_prev_ref[:, :KV_WIDTH]
    v_prev = kv_prev_ref[:, KV_WIDTH:]
    attn_blocks = []
    for sb in range(n_sub):
        rows = slice(sb * ATTN_BLOCK, (sb + 1) * ATTN_BLOCK)
        k_cur, v_cur = k_all[rows], v_all[rows]
        kb = jnp.concatenate([k_prev, k_cur], axis=0)
        vb = jnp.concatenate([v_prev, v_cur], axis=0)
        first_block = jnp.where(first_step, 1, 0) if sb == 0 else 0
        attn_blocks.append(_attention_block(q_all[rows], kb, vb, sinks_ref, first_block))
        k_prev, v_prev = k_cur, v_cur
    kv_prev_ref[:, :KV_WIDTH] = k_prev
    kv_prev_ref[:, KV_WIDTH:] = v_prev
    attn = jnp.concatenate(attn_blocks, axis=0)

    u = jax.nn.gelu(proj[:, _O_U:_O_G])
    gg = jax.nn.gelu(proj[:, _O_G:])
    avg = grp_avg_ref[...]
    mu = jnp.dot(gg.astype(jnp.bfloat16), avg, preferred_element_type=jnp.float32)
    gc = gg - mu
    var = jnp.dot((gc * gc).astype(jnp.bfloat16), avg, preferred_element_type=jnp.float32)
    gn = (gc * lax.rsqrt(var + LN_EPS) * lnv_g_ref[...] + lnv_b_ref[...]).astype(jnp.bfloat16)
    causal = (lax.broadcasted_iota(jnp.int32, (ATTN_BLOCK, ATTN_BLOCK), 0)
              >= lax.broadcasted_iota(jnp.int32, (ATTN_BLOCK, ATTN_BLOCK), 1))
    w_sp = [jnp.where(causal, w_sp_ref[g], 0.0).astype(jnp.bfloat16) for g in range(N_GMLP_GROUPS)]
    mixed_chunks = []
    for c in range(n_sub):
        rows = slice(c * ATTN_BLOCK, (c + 1) * ATTN_BLOCK)
        pieces = [
            jnp.dot(w_sp[g], gn[rows, g * GMLP_GROUP_DIM:(g + 1) * GMLP_GROUP_DIM],
                    preferred_element_type=jnp.float32)
            for g in range(N_GMLP_GROUPS)]
        mixed_chunks.append(jnp.concatenate(pieces, axis=-1) + b_sp_ref[...])
    sgu = u * jnp.concatenate(mixed_chunks, axis=0)

    mix = (jnp.dot(attn.astype(jnp.bfloat16), w_out_ref[:ATTN_WIDTH, :], preferred_element_type=jnp.float32)
           + jnp.dot(sgu.astype(jnp.bfloat16), w_out_ref[ATTN_WIDTH:, :], preferred_element_type=jnp.float32)
           + b_out_ref[...])
    h = _layer_norm(DEEPNORM_ALPHA * x + mix, ln1_g_ref[...], ln1_b_ref[...])
    h_ref[...] = h

    h_hi = h.astype(jnp.bfloat16)
    hp_ref[...] = _pack_bf16_pairs(h_hi)
    h_lo = (h - h_hi.astype(jnp.float32)).astype(jnp.bfloat16)
    part = jnp.dot(h_hi, w_r_ref[...], preferred_element_type=jnp.float32)
    logits = (part[:, :N_EXPERTS] + part[:, N_EXPERTS:]
              + jnp.dot(h_lo, w_r_ref[:, :N_EXPERTS], preferred_element_type=jnp.float32)
              + b_r_ref[...])
    lane = lax.broadcasted_iota(jnp.int32, (tm, N_EXPERTS), 1).astype(jnp.float32)
    work = logits
    vals, idxs = [], []
    onehot = jnp.zeros((tm, N_EXPERTS), jnp.float32)
    for _ in range(TOP_K):
        m = jnp.max(work, axis=-1, keepdims=True)
        idx = jnp.min(jnp.where(work == m, lane, float(N_EXPERTS)), axis=-1, keepdims=True)
        sel = lane == idx
        vals.append(m)
        idxs.append(idx)
        onehot = jnp.where(sel, 1.0, onehot)
        work = jnp.where(sel, -jnp.inf, work)
    exps = [jnp.exp(v - vals[0]) for v in vals]
    denom = exps[0] + exps[1] + exps[2] + exps[3]
    before = jnp.dot(tri_ref[...], onehot.astype(jnp.bfloat16), preferred_element_type=jnp.float32) + count_ref[...]
    ranks = [jnp.sum(jnp.where(lane == idxs[k], before, 0.0), axis=-1, keepdims=True) for k in range(TOP_K)]
    gates = [e / denom for e in exps]
    for k in range(TOP_K):
        gate_ref[:, k:k + 1] = gates[k]
    meta_lane = lax.broadcasted_iota(jnp.int32, (tm, 128), 1)
    meta = jnp.zeros((tm, 128), jnp.float32)
    for c, col in enumerate(idxs + ranks + gates):
        meta = jnp.where(meta_lane == c, col, meta)
    meta_ref[...] = meta.T[:META_ROWS]
    count_ref[...] += jnp.sum(onehot, axis=0, keepdims=True)


def _mixer(x2, sinks, w_in, b_in, lnv_g, lnv_b, w_sp, b_sp_full, w_out, b_out, ln1_g, ln1_b,
           w_r_cat, b_r, seq_len):
    t = x2.shape[0]
    tm = MIXER_ROWS
    grp = jnp.arange(GMLP_WIDTH) // GMLP_GROUP_DIM
    grp_avg = jnp.where(grp[:, None] == grp[None, :], 1.0 / GMLP_GROUP_DIM, 0.0).astype(jnp.bfloat16)
    tri = (jnp.arange(tm)[:, None] > jnp.arange(tm)[None, :]).astype(jnp.bfloat16)

    def full(shape):
        return pl.BlockSpec(shape, lambda i: (0,) * len(shape))

    return pl.pallas_call(
        functools.partial(_mixer_kernel, steps_per_seq=seq_len // tm),
        grid=(t // tm,),
        in_specs=[
            pl.BlockSpec(memory_space=pltpu.SMEM),
            pl.BlockSpec((tm, D_MODEL), lambda i: (i, 0)),
            full((D_MODEL, IN_WIDTH)), full((1, IN_WIDTH)),
            full((1, GMLP_WIDTH)), full((1, GMLP_WIDTH)), full((GMLP_WIDTH, GMLP_WIDTH)),
            full((N_GMLP_GROUPS, ATTN_BLOCK, ATTN_BLOCK)), full((ATTN_BLOCK, GMLP_WIDTH)),
            full((D_MODEL, D_MODEL)), full((1, D_MODEL)), full((1, D_MODEL)), full((1, D_MODEL)),
            full((D_MODEL, 2 * N_EXPERTS)), full((1, N_EXPERTS)), full((tm, tm)),
        ],
        out_specs=[
            pl.BlockSpec((tm, D_MODEL), lambda i: (i, 0)),
            pl.BlockSpec((tm, D_MODEL // 2), lambda i: (i, 0)),
            pl.BlockSpec((META_ROWS, tm), lambda i: (0, i)),
            pl.BlockSpec((tm, TOP_K), lambda i: (i, 0)),
            pl.BlockSpec((1, N_EXPERTS), lambda i: (0, 0)),
        ],
        out_shape=[
            jax.ShapeDtypeStruct((t, D_MODEL), jnp.float32),
            jax.ShapeDtypeStruct((t, D_MODEL // 2), jnp.int32),
            jax.ShapeDtypeStruct((META_ROWS, t), jnp.float32),
            jax.ShapeDtypeStruct((t, TOP_K), jnp.float32),
            jax.ShapeDtypeStruct((1, N_EXPERTS), jnp.float32),
        ],
        scratch_shapes=[pltpu.VMEM((ATTN_BLOCK, 2 * KV_WIDTH), jnp.bfloat16)],
        compiler_params=pltpu.CompilerParams(
            dimension_semantics=("arbitrary",), vmem_limit_bytes=VMEM_LIMIT_BYTES),
        name="mixer",
    )(sinks, x2, w_in, b_in, lnv_g, lnv_b, grp_avg, w_sp, b_sp_full, w_out, b_out, ln1_g, ln1_b,
      w_r_cat, b_r, tri)


def _gather_rows(idx, src):
    n = idx.shape[0]
    width = src.shape[1]
    win = GATHER_WINDOW
    sc = plsc.get_sparse_core_info()
    n_workers = sc.num_cores * sc.num_subcores
    per_worker = n // n_workers
    n_pairs = per_worker // (2 * win)
    assert n_pairs * 2 * win * n_workers == n
    mesh = plsc.VectorSubcoreMesh(core_axis_name="core", subcore_axis_name="subcore")

    @functools.partial(
        pl.kernel, out_type=jax.ShapeDtypeStruct((n, width), src.dtype), mesh=mesh,
        scratch_types=[pltpu.VMEM((per_worker,), jnp.int32), pltpu.VMEM((2, win, width), src.dtype),
                       pltpu.SemaphoreType.DMA((2,)), pltpu.SemaphoreType.DMA((2,))],
        name="gather_rows")
    def gather(src_hbm, idx_hbm, out_hbm, idx_v, rows_v, fetch_sem, store_sem):
        worker = lax.axis_index("subcore") * sc.num_cores + lax.axis_index("core")
        base = worker * per_worker
        pltpu.sync_copy(idx_hbm.at[pl.ds(base, per_worker)], idx_v)

        def fetch(chunk, buf):
            return pltpu.make_async_copy(src_hbm.at[idx_v.at[pl.ds(chunk * win, win)]], rows_v.at[buf],
                                         fetch_sem.at[buf])

        def store(chunk, buf):
            return pltpu.make_async_copy(rows_v.at[buf], out_hbm.at[pl.ds(base + chunk * win, win)],
                                         store_sem.at[buf])

        @pl.loop(0, n_pairs)
        def _(p):
            for buf in range(2):
                @pl.when(p > 0)
                def _():
                    store(2 * p - 2 + buf, buf).wait()
                fetch(2 * p + buf, buf).start()
            for buf in range(2):
                fetch(2 * p + buf, buf).wait()
                store(2 * p + buf, buf).start()

        for buf in range(2):
            store(2 * n_pairs - 2 + buf, buf).wait()

    return gather(src, idx)


def _dispatch_rows(dest_t, src, n_rows):
    t, width = src.shape
    win = GATHER_WINDOW
    sc = plsc.get_sparse_core_info()
    n_workers = sc.num_cores * sc.num_subcores
    per_worker = t // n_workers
    n_chunks = per_worker // win
    n_pairs = n_chunks // 2
    assert n_pairs * 2 * win * n_workers == t
    idx = dest_t.reshape(TOP_K, n_workers, n_chunks, win).transpose(1, 0, 2, 3)
    idx = idx.reshape(n_workers, TOP_K * n_chunks, win)
    mesh = plsc.VectorSubcoreMesh(core_axis_name="core", subcore_axis_name="subcore")

    @functools.partial(
        pl.kernel, out_type=jax.ShapeDtypeStruct((n_rows, width), src.dtype), mesh=mesh,
        scratch_types=[pltpu.VMEM((TOP_K * n_chunks, win), jnp.int32), pltpu.VMEM((2, win, width), src.dtype),
                       pltpu.SemaphoreType.DMA((2,)), pltpu.SemaphoreType.DMA((2,))],
        name="dispatch_rows")
    def dispatch(src_hbm, idx_hbm, out_hbm, idx_v, rows_v, fetch_sem, store_sem):
        worker = lax.axis_index("subcore") * sc.num_cores + lax.axis_index("core")
        base = worker * per_worker
        pltpu.sync_copy(idx_hbm.at[worker], idx_v)

        def fetch(chunk, buf):
            return pltpu.make_async_copy(src_hbm.at[pl.ds(base + chunk * win, win)], rows_v.at[buf],
                                         fetch_sem.at[buf])

        def store(chunk, k, buf):
            return pltpu.make_async_copy(rows_v.at[buf], out_hbm.at[idx_v.at[k * n_chunks + chunk]],
                                         store_sem.at[buf])

        @pl.loop(0, n_pairs)
        def _(p):
            for buf in range(2):
                @pl.when(p > 0)
                def _():
                    for k in range(TOP_K):
                        store(2 * p - 2 + buf, k, buf).wait()
                fetch(2 * p + buf, buf).start()
            for buf in range(2):
                fetch(2 * p + buf, buf).wait()
                for k in range(TOP_K):
                    store(2 * p + buf, k, buf).start()

        for buf in range(2):
            for k in range(TOP_K):
                store(2 * n_pairs - 2 + buf, k, buf).wait()

    return dispatch(src, idx)


def _expert_kernel(block_e_ref, n_valid_ref, first_ref, next_e_ref, slot_ref,
                   x_ref, wg_hbm, bg_ref, wu_hbm, bu_ref, wd_hbm, bd_ref,
                   y_ref, wg_f32, wu_f32, wd_f32, wg_bf, wu_bf, wd_bf, sem):
    i = pl.program_id(0)
    n_valid = n_valid_ref[i]
    slot = slot_ref[i]
    staged = ((wg_hbm, wg_f32, wg_bf), (wu_hbm, wu_f32, wu_bf), (wd_hbm, wd_f32, wd_bf))

    def weight_copy(m, expert, s):
        return pltpu.make_async_copy(staged[m][0].at[expert], staged[m][1].at[s], sem.at[s, m])

    @pl.when(i == 0)
    def _():
        for m in range(3):
            weight_copy(m, block_e_ref[0], 0).start()

    @pl.when(first_ref[i] == 1)
    def _():
        @pl.when(next_e_ref[i] >= 0)
        def _():
            for m in range(3):
                weight_copy(m, next_e_ref[i], 1 - slot).start()

        for m in range(3):
            weight_copy(m, block_e_ref[i], slot).wait()
            staged[m][2][...] = staged[m][1][slot].astype(jnp.bfloat16)

    @pl.when(n_valid > 0)
    def _():
        row = lax.broadcasted_iota(jnp.int32, (x_ref.shape[0], 1), 0)
        x_hi, x_lo = _unpack_bf16_pairs(jnp.where(row < n_valid, x_ref[...], 0))
        xb = jnp.concatenate([x_hi, x_lo], axis=1).astype(jnp.bfloat16)
        gt = jnp.minimum(jnp.dot(xb, wg_bf[...], preferred_element_type=jnp.float32) + bg_ref[0], SWIGLU_LIMIT)
        up = jnp.clip(jnp.dot(xb, wu_bf[...], preferred_element_type=jnp.float32) + bu_ref[0],
                      -SWIGLU_LIMIT, SWIGLU_LIMIT)
        hid = gt * jax.nn.sigmoid(SWIGLU_ALPHA * gt) * (up + 1.0)
        y = jnp.dot(hid.astype(jnp.bfloat16), wd_bf[...], preferred_element_type=jnp.float32) + bd_ref[0]
        y_ref[...] = _pack_bf16_pairs(y.astype(jnp.bfloat16))

    @pl.when(n_valid == 0)
    def _():
        y_ref[...] = jnp.zeros_like(y_ref)


def _experts(block_e, n_valid, x_rows, w_gate, b_gate, w_up, b_up, w_down, b_down):
    n_rows = x_rows.shape[0]
    bm = EXPERT_ROWS
    n_blocks = n_rows // bm
    d_ff = w_gate.shape[2]

    blocks = jnp.arange(n_blocks, dtype=jnp.int32)
    is_first = jnp.concatenate([jnp.ones((1,), jnp.int32), (block_e[1:] != block_e[:-1]).astype(jnp.int32)])
    group = jnp.sum(jnp.where(blocks[None, :] <= blocks[:, None], is_first[None, :], 0), axis=1) - 1
    next_e = jnp.min(jnp.where(block_e[None, :] > block_e[:, None], block_e[None, :], N_EXPERTS), axis=1)
    next_e = jnp.where(next_e == N_EXPERTS, -1, next_e)

    def by_expert(shape):
        return pl.BlockSpec(shape, lambda i, be, *_: (be[i],) + (0,) * (len(shape) - 1))

    hbm = pl.BlockSpec(memory_space=pl.ANY)
    return pl.pallas_call(
        _expert_kernel,
        grid_spec=pltpu.PrefetchScalarGridSpec(
            num_scalar_prefetch=5,
            grid=(n_blocks,),
            in_specs=[
                pl.BlockSpec((bm, D_MODEL // 2), lambda i, *_: (i, 0)),
                hbm, by_expert((1, 1, d_ff)),
                hbm, by_expert((1, 1, d_ff)),
                hbm, by_expert((1, 1, D_MODEL)),
            ],
            out_specs=pl.BlockSpec((bm, D_MODEL // 2), lambda i, *_: (i, 0)),
            scratch_shapes=[
                pltpu.VMEM((2, D_MODEL, d_ff), jnp.float32),
                pltpu.VMEM((2, D_MODEL, d_ff), jnp.float32),
                pltpu.VMEM((2, d_ff, D_MODEL), jnp.float32),
                pltpu.VMEM((D_MODEL, d_ff), jnp.bfloat16),
                pltpu.VMEM((D_MODEL, d_ff), jnp.bfloat16),
                pltpu.VMEM((d_ff, D_MODEL), jnp.bfloat16),
                pltpu.SemaphoreType.DMA((2, 3)),
            ],
        ),
        out_shape=jax.ShapeDtypeStruct((n_rows, D_MODEL // 2), jnp.int32),
        compiler_params=pltpu.CompilerParams(
            dimension_semantics=("arbitrary",), vmem_limit_bytes=VMEM_LIMIT_BYTES),
        name="experts",
    )(block_e, n_valid, is_first, next_e, group % 2, x_rows, w_gate, b_gate.reshape(N_EXPERTS, 1, d_ff), w_up,
      b_up.reshape(N_EXPERTS, 1, d_ff), w_down, b_down.reshape(N_EXPERTS, 1, D_MODEL))


def _combine_kernel(h_ref, y_ref, gate_ref, g_ref, b_ref, o_ref):
    ffn_hi, ffn_lo = 0.0, 0.0
    for k in range(TOP_K):
        y_hi, y_lo = _unpack_bf16_pairs(y_ref[k])
        ffn_hi = ffn_hi + gate_ref[:, k:k + 1] * y_hi
        ffn_lo = ffn_lo + gate_ref[:, k:k + 1] * y_lo
    ffn = jnp.concatenate([ffn_hi, ffn_lo], axis=1)
    o_ref[...] = _layer_norm(DEEPNORM_ALPHA * h_ref[...] + ffn, g_ref[...], b_ref[...])


def _combine(h, y_tok, gates, ln2_g, ln2_b):
    t = h.shape[0]
    tm = COMBINE_ROWS
    return pl.pallas_call(
        _combine_kernel,
        grid=(t // tm,),
        in_specs=[
            pl.BlockSpec((tm, D_MODEL), lambda i: (i, 0)),
            pl.BlockSpec((TOP_K, tm, D_MODEL // 2), lambda i: (0, i, 0)),
            pl.BlockSpec((tm, TOP_K), lambda i: (i, 0)),
            pl.BlockSpec((1, D_MODEL), lambda i: (0, 0)),
            pl.BlockSpec((1, D_MODEL), lambda i: (0, 0)),
        ],
        out_specs=pl.BlockSpec((tm, D_MODEL), lambda i: (i, 0)),
        out_shape=jax.ShapeDtypeStruct((t, D_MODEL), jnp.float32),
        compiler_params=pltpu.CompilerParams(
            dimension_semantics=("arbitrary",), vmem_limit_bytes=VMEM_LIMIT_BYTES),
        name="combine",
    )(h, y_tok, gates, ln2_g, ln2_b)


def _layer(x2, seq_len, w_in, b_in, sinks, ln_v_g, ln_v_b, w_spatial, b_spatial, w_out, b_out,
           ln1_g, ln1_b, w_router, b_router, w_gate, b_gate, w_up, b_up, w_down, b_down, ln2_g, ln2_b):
    t = x2.shape[0]
    tk = t * TOP_K
    bm = EXPERT_ROWS
    bf16 = jnp.bfloat16

    w_r_hi = w_router.astype(bf16)
    w_r_lo = (w_router - w_r_hi.astype(jnp.float32)).astype(bf16)
    w_r_cat = jnp.concatenate([w_r_hi, w_r_lo], axis=1)
    b_sp_full = jnp.repeat(b_spatial.T, GMLP_GROUP_DIM, axis=1)

    h, h_packed, meta, gates, counts = _mixer(
        x2, sinks, w_in.astype(bf16), b_in[None], ln_v_g[None], ln_v_b[None], w_spatial, b_sp_full,
        w_out.astype(bf16), b_out[None], ln1_g[None], ln1_b[None], w_r_cat, b_router[None], seq_len)

    counts = counts[0].astype(jnp.int32)
    experts = jnp.arange(N_EXPERTS, dtype=jnp.int32)
    padded = (counts + bm - 1) // bm * bm
    padded_end = jnp.sum(jnp.where(experts[None, :] <= experts[:, None], padded[None, :], 0), axis=1)
    padded_start = padded_end - padded
    n_blocks = tk // bm + N_EXPERTS
    n_rows = n_blocks * bm
    top_idx_t = meta[:TOP_K].astype(jnp.int32)
    rank_t = meta[TOP_K:2 * TOP_K].astype(jnp.int32)
    dest_t = rank_t + jnp.sum(
        jnp.where(top_idx_t[None] == experts[:, None, None], padded_start[:, None, None], 0), axis=0)
    block_start = jnp.arange(n_blocks, dtype=jnp.int32) * bm
    block_e = jnp.minimum(
        jnp.sum((padded_end[None, :] <= block_start[:, None]).astype(jnp.int32), axis=1), N_EXPERTS - 1)
    valid_end = jnp.sum(jnp.where(block_e[:, None] == experts[None, :], (padded_start + counts)[None, :], 0), axis=1)
    n_valid = jnp.clip(valid_end - block_start, 0, bm)

    x_rows = _dispatch_rows(dest_t, h_packed, n_rows)
    y_rows = _experts(block_e, n_valid, x_rows, w_gate, b_gate, w_up, b_up, w_down, b_down)
    y_tok = _gather_rows(dest_t.reshape(-1), y_rows).reshape(TOP_K, t, D_MODEL // 2)
    return _combine(h, y_tok, gates, ln2_g[None], ln2_b[None])


def kernel(x, w_in, b_in, sinks, ln_v_g, ln_v_b, w_spatial, b_spatial, w_out, b_out, ln1_g, ln1_b,
           w_router, b_router, w_gate, b_gate, w_up, b_up, w_down, b_down, ln2_g, ln2_b):
    batch, seq_len, d = x.shape
    x2 = x.reshape(batch * seq_len, d)
    for l in range(DEPTH):
        x2 = _layer(x2, seq_len, w_in[l], b_in[l], sinks[l], ln_v_g[l], ln_v_b[l], w_spatial[l],
                    b_spatial[l], w_out[l], b_out[l], ln1_g[l], ln1_b[l], w_router[l], b_router[l],
                    w_gate[l], b_gate[l], w_up[l], b_up[l], w_down[l], b_down[l], ln2_g[l], ln2_b[l])
    return x2.reshape(batch, seq_len, d)
```

```python
import functools

import jax
import jax.numpy as jnp
from jax import lax
from jax.experimental import pallas as pl
from jax.experimental.pallas import tpu as pltpu
from jax.experimental.pallas import tpu_sc as plsc

D_MODEL = 1024
HEAD_DIM = 64
N_Q_HEADS = 8
N_KV_HEADS = 2
Q_REP = N_Q_HEADS // N_KV_HEADS
ATTN_WIDTH = N_Q_HEADS * HEAD_DIM
KV_WIDTH = N_KV_HEADS * HEAD_DIM
ATTN_BLOCK = 128
N_GMLP_GROUPS = 8
GMLP_WIDTH = D_MODEL - ATTN_WIDTH
GMLP_GROUP_DIM = GMLP_WIDTH // N_GMLP_GROUPS
IN_WIDTH = ATTN_WIDTH + 2 * KV_WIDTH + 2 * GMLP_WIDTH
N_EXPERTS = 32
TOP_K = 4
SWIGLU_LIMIT = 7.0
SWIGLU_ALPHA = 1.702
LN_EPS = 1e-5
DEPTH = 1
DEEPNORM_ALPHA = (2.0 * DEPTH) ** 0.25
NEG_INF = -1e30
LOG2_E = 1.4426950408889634

MIXER_ROWS = 256
EXPERT_ROWS = 512
GATHER_WINDOW = 64
META_ROWS = 16
ROUTER_LANES = 128
COMBINE_ROWS = 256
VMEM_LIMIT_BYTES = 56 * 1024 * 1024

_O_K = ATTN_WIDTH
_O_V = _O_K + KV_WIDTH
_O_U = _O_V + KV_WIDTH
_O_G = _O_U + GMLP_WIDTH


def _pack_bf16_pairs(v):
    n = v.shape[1] // 2
    hi = lax.bitcast_convert_type(v[:, :n].astype(jnp.float32), jnp.int32)
    lo = lax.bitcast_convert_type(v[:, n:].astype(jnp.float32), jnp.int32)
    return hi | lax.shift_right_logical(lo, 16)


def _unpack_bf16_pairs(p):
    hi = lax.bitcast_convert_type(p & jnp.int32(-65536), jnp.float32)
    lo = lax.bitcast_convert_type(lax.shift_left(p, 16), jnp.float32)
    return hi, lo


def _layer_norm(v, g, b):
    mu = jnp.mean(v, axis=-1, keepdims=True)
    vc = v - mu
    var = jnp.mean(vc * vc, axis=-1, keepdims=True)
    return vc * lax.rsqrt(var + LN_EPS) * g + b


def _attention_block(q, kb, vb, bias_ref, bias_sel):
    outs = []
    for g in range(N_KV_HEADS):
        kg = kb[:, g * HEAD_DIM:(g + 1) * HEAD_DIM]
        vg = vb[:, g * HEAD_DIM:(g + 1) * HEAD_DIM]
        qg = jnp.concatenate(
            [q[:, (g * Q_REP + r) * HEAD_DIM:(g * Q_REP + r + 1) * HEAD_DIM] for r in range(Q_REP)],
            axis=0).astype(jnp.bfloat16)
        s = (lax.dot_general(qg, kg, (((1,), (1,)), ((), ())), preferred_element_type=jnp.float32)
             + bias_ref[g, bias_sel])
        p = jnp.exp2(s - jnp.max(s, axis=-1, keepdims=True))
        denom = jnp.sum(p, axis=-1, keepdims=True)
        o = jnp.dot(p.astype(jnp.bfloat16), vg, preferred_element_type=jnp.float32) / denom
        outs.extend(o[r * ATTN_BLOCK:(r + 1) * ATTN_BLOCK] for r in range(Q_REP))
    return jnp.concatenate(outs, axis=-1)


def _mixer_kernel(x_ref, w_in_ref, b_in_ref, bias_ref, lnv_g_ref, lnv_b_ref, grp_avg_ref,
                  w_sp_ref, b_sp_ref, w_out_ref, b_out_ref, ln1_g_ref, ln1_b_ref,
                  w_r_ref, w_r_hi_ref, b_r_ref, tri_ref,
                  h_ref, hp_ref, meta_ref, count_ref,
                  kv_prev_ref, *, steps_per_seq):
    i = pl.program_id(0)
    first_step = (i % steps_per_seq) == 0
    tm = x_ref.shape[0]
    n_sub = tm // ATTN_BLOCK

    @pl.when(i == 0)
    def _():
        count_ref[...] = jnp.zeros_like(count_ref)

    @pl.when(first_step)
    def _():
        kv_prev_ref[...] = jnp.zeros_like(kv_prev_ref)

    x = x_ref[...]
    proj = jnp.dot(x.astype(jnp.bfloat16), w_in_ref[...], preferred_element_type=jnp.float32) + b_in_ref[...]

    q_all = proj[:, :_O_K] * (LOG2_E * HEAD_DIM ** -0.5)
    k_all = proj[:, _O_K:_O_V].astype(jnp.bfloat16)
    v_all = proj[:, _O_V:_O_U].astype(jnp.bfloat16)
    k_prev = kv_prev_ref[:, :KV_WIDTH]
    v_prev = kv_prev_ref[:, KV_WIDTH:]
    is_row0 = lax.broadcasted_iota(jnp.int32, (ATTN_BLOCK, KV_WIDTH), 0) == 0
    attn_blocks = []
    for sb in range(n_sub):
        rows = slice(sb * ATTN_BLOCK, (sb + 1) * ATTN_BLOCK)
        k_cur, v_cur = k_all[rows], v_all[rows]
        kb = jnp.concatenate([jnp.where(is_row0, 0, k_prev), k_cur], axis=0)
        vb = jnp.concatenate([jnp.where(is_row0, 0, v_prev), v_cur], axis=0)
        bias_sel = jnp.where(first_step, 1, 0) if sb == 0 else 0
        attn_blocks.append(_attention_block(q_all[rows], kb, vb, bias_ref, bias_sel))
        k_prev, v_prev = k_cur, v_cur
    kv_prev_ref[:, :KV_WIDTH] = k_prev
    kv_prev_ref[:, KV_WIDTH:] = v_prev
    attn = jnp.concatenate(attn_blocks, axis=0)

    u = jax.nn.gelu(proj[:, _O_U:_O_G])
    gg = jax.nn.gelu(proj[:, _O_G:])
    avg = grp_avg_ref[...]
    mu = jnp.dot(gg.astype(jnp.bfloat16), avg, preferred_element_type=jnp.float32)
    gc = gg - mu
    var = jnp.dot((gc * gc).astype(jnp.bfloat16), avg, preferred_element_type=jnp.float32)
    gn = (gc * lax.rsqrt(var + LN_EPS) * lnv_g_ref[...] + lnv_b_ref[...]).astype(jnp.bfloat16)
    causal = (lax.broadcasted_iota(jnp.int32, (ATTN_BLOCK, ATTN_BLOCK), 0)
              >= lax.broadcasted_iota(jnp.int32, (ATTN_BLOCK, ATTN_BLOCK), 1))
    w_sp = [jnp.where(causal, w_sp_ref[g], 0.0).astype(jnp.bfloat16) for g in range(N_GMLP_GROUPS)]
    mixed_chunks = []
    for c in range(n_sub):
        rows = slice(c * ATTN_BLOCK, (c + 1) * ATTN_BLOCK)
        pieces = [
            jnp.dot(w_sp[g], gn[rows, g * GMLP_GROUP_DIM:(g + 1) * GMLP_GROUP_DIM],
                    preferred_element_type=jnp.float32)
            for g in range(N_GMLP_GROUPS)]
        mixed_chunks.append(jnp.concatenate(pieces, axis=-1) + b_sp_ref[...])
    sgu = u * jnp.concatenate(mixed_chunks, axis=0)

    mix = (jnp.dot(attn.astype(jnp.bfloat16), w_out_ref[:ATTN_WIDTH, :], preferred_element_type=jnp.float32)
           + jnp.dot(sgu.astype(jnp.bfloat16), w_out_ref[ATTN_WIDTH:, :], preferred_element_type=jnp.float32)
           + b_out_ref[...])
    h = _layer_norm(DEEPNORM_ALPHA * x + mix, ln1_g_ref[...], ln1_b_ref[...])
    h_ref[...] = h

    h_hi = h.astype(jnp.bfloat16)
    hp_ref[...] = _pack_bf16_pairs(h_hi)
    h_lo = (h - h_hi.astype(jnp.float32)).astype(jnp.bfloat16)
    part = (jnp.dot(h_hi, w_r_ref[...], preferred_element_type=jnp.float32)
            + jnp.dot(h_lo, w_r_hi_ref[...], preferred_element_type=jnp.float32)).T
    logits = part[:N_EXPERTS] + part[N_EXPERTS:2 * N_EXPERTS] + b_r_ref[...]

    n_grp = N_EXPERTS // 8
    grp = [logits[8 * g:8 * (g + 1)] for g in range(n_grp)]
    sub = lax.broadcasted_iota(jnp.int32, (8, tm), 0)
    beaten = [jnp.zeros((8, tm), jnp.float32) for _ in range(n_grp)]
    for e2 in range(N_EXPERTS):
        g2, r2 = divmod(e2, 8)
        row = logits[e2:e2 + 1]
        for g in range(n_grp):
            if g > g2:
                wins = jnp.where(row >= grp[g], 1.0, 0.0)
            elif g < g2:
                wins = jnp.where(row > grp[g], 1.0, 0.0)
            else:
                wins = jnp.where(sub > r2, jnp.where(row >= grp[g], 1.0, 0.0), jnp.where(row > grp[g], 1.0, 0.0))
            beaten[g] = beaten[g] + wins
    place = jnp.concatenate(beaten, axis=0)
    expert_id = lax.broadcasted_iota(jnp.int32, (N_EXPERTS, tm), 0).astype(jnp.float32)
    onehot = jnp.where(place < TOP_K, 1.0, 0.0)
    before = jnp.dot(onehot.astype(jnp.bfloat16), tri_ref[...], preferred_element_type=jnp.float32) + count_ref[...]

    def pick(k, table):
        return jnp.sum(jnp.where(place == k, table, 0.0), axis=0, keepdims=True)

    vals = [pick(k, logits) for k in range(TOP_K)]
    exps = [jnp.exp(v - vals[0]) for v in vals]
    denom = exps[0] + exps[1] + exps[2] + exps[3]
    meta_ref[...] = jnp.zeros_like(meta_ref)
    for k in range(TOP_K):
        meta_ref[k:k + 1, :] = pick(k, expert_id)
        meta_ref[TOP_K + k:TOP_K + k + 1, :] = pick(k, before)
        meta_ref[2 * TOP_K + k:2 * TOP_K + k + 1, :] = exps[k] / denom
    count_ref[...] += jnp.sum(onehot, axis=1, keepdims=True)


def _score_bias(sinks):
    t_idx = jnp.arange(Q_REP * ATTN_BLOCK)[:, None] % ATTN_BLOCK
    s_idx = jnp.arange(2 * ATTN_BLOCK)[None, :]
    diff = t_idx + ATTN_BLOCK - s_idx
    band = (diff >= 0) & (diff < ATTN_BLOCK)
    masks = jnp.stack([band, band & (s_idx >= ATTN_BLOCK)])
    bias = jnp.where(masks, 0.0, NEG_INF).astype(jnp.float32)
    sink_rows = jnp.repeat(sinks.reshape(N_KV_HEADS, Q_REP) * LOG2_E, ATTN_BLOCK, axis=1)
    return jnp.where(s_idx == 0, sink_rows[:, None, :, None], bias[None])


def _mixer(x2, sinks, w_in, b_in, lnv_g, lnv_b, w_sp, b_sp_full, w_out, b_out, ln1_g, ln1_b,
           w_r, w_r_hi, b_r, seq_len):
    t = x2.shape[0]
    tm = MIXER_ROWS
    grp = jnp.arange(GMLP_WIDTH) // GMLP_GROUP_DIM
    grp_avg = jnp.where(grp[:, None] == grp[None, :], 1.0 / GMLP_GROUP_DIM, 0.0).astype(jnp.bfloat16)
    tri = (jnp.arange(tm)[:, None] < jnp.arange(tm)[None, :]).astype(jnp.bfloat16)

    def full(shape):
        return pl.BlockSpec(shape, lambda i: (0,) * len(shape))

    return pl.pallas_call(
        functools.partial(_mixer_kernel, steps_per_seq=seq_len // tm),
        grid=(t // tm,),
        in_specs=[
            pl.BlockSpec((tm, D_MODEL), lambda i: (i, 0)),
            full((D_MODEL, IN_WIDTH)), full((1, IN_WIDTH)),
            full((N_KV_HEADS, 2, Q_REP * ATTN_BLOCK, 2 * ATTN_BLOCK)),
            full((1, GMLP_WIDTH)), full((1, GMLP_WIDTH)), full((GMLP_WIDTH, GMLP_WIDTH)),
            full((N_GMLP_GROUPS, ATTN_BLOCK, ATTN_BLOCK)), full((ATTN_BLOCK, GMLP_WIDTH)),
            full((D_MODEL, D_MODEL)), full((1, D_MODEL)), full((1, D_MODEL)), full((1, D_MODEL)),
            full((D_MODEL, ROUTER_LANES)), full((D_MODEL, ROUTER_LANES)), full((N_EXPERTS, 1)), full((tm, tm)),
        ],
        out_specs=[
            pl.BlockSpec((tm, D_MODEL), lambda i: (i, 0)),
            pl.BlockSpec((tm, D_MODEL // 2), lambda i: (i, 0)),
            pl.BlockSpec((META_ROWS, tm), lambda i: (0, i)),
            pl.BlockSpec((N_EXPERTS, 1), lambda i: (0, 0)),
        ],
        out_shape=[
            jax.ShapeDtypeStruct((t, D_MODEL), jnp.float32),
            jax.ShapeDtypeStruct((t, D_MODEL // 2), jnp.int32),
            jax.ShapeDtypeStruct((META_ROWS, t), jnp.float32),
            jax.ShapeDtypeStruct((N_EXPERTS, 1), jnp.float32),
        ],
        scratch_shapes=[pltpu.VMEM((ATTN_BLOCK, 2 * KV_WIDTH), jnp.bfloat16)],
        compiler_params=pltpu.CompilerParams(
            dimension_semantics=("arbitrary",), vmem_limit_bytes=VMEM_LIMIT_BYTES),
        name="mixer",
    )(x2, w_in, b_in, _score_bias(sinks), lnv_g, lnv_b, grp_avg, w_sp, b_sp_full, w_out, b_out, ln1_g, ln1_b,
      w_r, w_r_hi, b_r, tri)


def _gather_rows(idx, src):
    n = idx.shape[0]
    width = src.shape[1]
    win = GATHER_WINDOW
    sc = plsc.get_sparse_core_info()
    n_workers = sc.num_cores * sc.num_subcores
    per_worker = n // n_workers
    n_pairs = per_worker // (2 * win)
    assert n_pairs * 2 * win * n_workers == n
    mesh = plsc.VectorSubcoreMesh(core_axis_name="core", subcore_axis_name="subcore")

    @functools.partial(
        pl.kernel, out_type=jax.ShapeDtypeStruct((n, width), src.dtype), mesh=mesh,
        scratch_types=[pltpu.VMEM((per_worker,), jnp.int32), pltpu.VMEM((2, win, width), src.dtype),
                       pltpu.SemaphoreType.DMA((2,)), pltpu.SemaphoreType.DMA((2,))],
        name="gather_rows")
    def gather(src_hbm, idx_hbm, out_hbm, idx_v, rows_v, fetch_sem, store_sem):
        worker = lax.axis_index("subcore") * sc.num_cores + lax.axis_index("core")
        base = worker * per_worker
        pltpu.sync_copy(idx_hbm.at[pl.ds(base, per_worker)], idx_v)

        def fetch(chunk, buf):
            return pltpu.make_async_copy(src_hbm.at[idx_v.at[pl.ds(chunk * win, win)]], rows_v.at[buf],
                                         fetch_sem.at[buf])

        def store(chunk, buf):
            return pltpu.make_async_copy(rows_v.at[buf], out_hbm.at[pl.ds(base + chunk * win, win)],
                                         store_sem.at[buf])

        @pl.loop(0, n_pairs)
        def _(p):
            for buf in range(2):
                @pl.when(p > 0)
                def _():
                    store(2 * p - 2 + buf, buf).wait()
                fetch(2 * p + buf, buf).start()
            for buf in range(2):
                fetch(2 * p + buf, buf).wait()
                store(2 * p + buf, buf).start()

        for buf in range(2):
            store(2 * n_pairs - 2 + buf, buf).wait()

    return gather(src, idx)


def _dispatch_rows(dest_t, src, n_rows):
    t, width = src.shape
    win = GATHER_WINDOW
    sc = plsc.get_sparse_core_info()
    n_workers = sc.num_cores * sc.num_subcores
    per_worker = t // n_workers
    n_chunks = per_worker // win
    n_pairs = n_chunks // 2
    assert n_pairs * 2 * win * n_workers == t
    idx = dest_t.reshape(TOP_K, n_workers, n_chunks, win).transpose(1, 0, 2, 3)
    idx = idx.reshape(n_workers, TOP_K * n_chunks, win)
    mesh = plsc.VectorSubcoreMesh(core_axis_name="core", subcore_axis_name="subcore")

    @functools.partial(
        pl.kernel, out_type=jax.ShapeDtypeStruct((n_rows, width), src.dtype), mesh=mesh,
        scratch_types=[pltpu.VMEM((TOP_K * n_chunks, win), jnp.int32), pltpu.VMEM((2, win, width), src.dtype),
                       pltpu.SemaphoreType.DMA((2,)), pltpu.SemaphoreType.DMA((2,))],
        name="dispatch_rows")
    def dispatch(src_hbm, idx_hbm, out_hbm, idx_v, rows_v, fetch_sem, store_sem):
        worker = lax.axis_index("subcore") * sc.num_cores + lax.axis_index("core")
        base = worker * per_worker
        pltpu.sync_copy(idx_hbm.at[worker], idx_v)

        def fetch(chunk, buf):
            return pltpu.make_async_copy(src_hbm.at[pl.ds(base + chunk * win, win)], rows_v.at[buf],
                                         fetch_sem.at[buf])

        def store(chunk, k, buf):
            return pltpu.make_async_copy(rows_v.at[buf], out_hbm.at[idx_v.at[k * n_chunks + chunk]],
                                         store_sem.at[buf])

        @pl.loop(0, n_pairs)
        def _(p):
            for buf in range(2):
                @pl.when(p > 0)
                def _():
                    for k in range(TOP_K):
                        store(2 * p - 2 + buf, k, buf).wait()
                fetch(2 * p + buf, buf).start()
            for buf in range(2):
                fetch(2 * p + buf, buf).wait()
                for k in range(TOP_K):
                    store(2 * p + buf, k, buf).start()

        for buf in range(2):
            for k in range(TOP_K):
                store(2 * n_pairs - 2 + buf, k, buf).wait()

    return dispatch(src, idx)


def _expert_kernel(block_e_ref, n_valid_ref, first_ref, next_e_ref, slot_ref,
                   x_ref, wg_hbm, bg_ref, wu_hbm, bu_ref, wd_hbm, bd_ref,
                   y_ref, wg_f32, wu_f32, wd_f32, wg_bf, wu_bf, wd_bf, sem):
    i = pl.program_id(0)
    n_valid = n_valid_ref[i]
    slot = slot_ref[i]
    staged = ((wg_hbm, wg_f32, wg_bf), (wu_hbm, wu_f32, wu_bf), (wd_hbm, wd_f32, wd_bf))

    def weight_copy(m, expert, s):
        return pltpu.make_async_copy(staged[m][0].at[expert], staged[m][1].at[s], sem.at[s, m])

    @pl.when(i == 0)
    def _():
        for m in range(3):
            weight_copy(m, block_e_ref[0], 0).start()

    @pl.when(first_ref[i] == 1)
    def _():
        @pl.when(next_e_ref[i] >= 0)
        def _():
            for m in range(3):
                weight_copy(m, next_e_ref[i], 1 - slot).start()

        for m in range(3):
            weight_copy(m, block_e_ref[i], slot).wait()
            staged[m][2][...] = staged[m][1][slot].astype(jnp.bfloat16)

    @pl.when(n_valid > 0)
    def _():
        row = lax.broadcasted_iota(jnp.int32, (x_ref.shape[0], 1), 0)
        x_hi, x_lo = _unpack_bf16_pairs(jnp.where(row < n_valid, x_ref[...], 0))
        xb = jnp.concatenate([x_hi, x_lo], axis=1).astype(jnp.bfloat16)
        gt = jnp.minimum(jnp.dot(xb, wg_bf[...], preferred_element_type=jnp.float32) + bg_ref[0], SWIGLU_LIMIT)
        up = jnp.clip(jnp.dot(xb, wu_bf[...], preferred_element_type=jnp.float32) + bu_ref[0],
                      -SWIGLU_LIMIT, SWIGLU_LIMIT)
        hid = gt * jax.nn.sigmoid(SWIGLU_ALPHA * gt) * (up + 1.0)
        y = jnp.dot(hid.astype(jnp.bfloat16), wd_bf[...], preferred_element_type=jnp.float32) + bd_ref[0]
        y_ref[...] = _pack_bf16_pairs(y.astype(jnp.bfloat16))

    @pl.when(n_valid == 0)
    def _():
        y_ref[...] = jnp.zeros_like(y_ref)


def _experts(block_e, n_valid, x_rows, w_gate, b_gate, w_up, b_up, w_down, b_down):
    n_rows = x_rows.shape[0]
    bm = EXPERT_ROWS
    n_blocks = n_rows // bm
    d_ff = w_gate.shape[2]

    blocks = jnp.arange(n_blocks, dtype=jnp.int32)
    is_first = jnp.concatenate([jnp.ones((1,), jnp.int32), (block_e[1:] != block_e[:-1]).astype(jnp.int32)])
    group = jnp.sum(jnp.where(blocks[None, :] <= blocks[:, None], is_first[None, :], 0), axis=1) - 1
    next_e = jnp.min(jnp.where(block_e[None, :] > block_e[:, None], block_e[None, :], N_EXPERTS), axis=1)
    next_e = jnp.where(next_e == N_EXPERTS, -1, next_e)

    def by_expert(shape):
        return pl.BlockSpec(shape, lambda i, be, *_: (be[i],) + (0,) * (len(shape) - 1))

    hbm = pl.BlockSpec(memory_space=pl.ANY)
    return pl.pallas_call(
        _expert_kernel,
        grid_spec=pltpu.PrefetchScalarGridSpec(
            num_scalar_prefetch=5,
            grid=(n_blocks,),
            in_specs=[
                pl.BlockSpec((bm, D_MODEL // 2), lambda i, *_: (i, 0)),
                hbm, by_expert((1, 1, d_ff)),
                hbm, by_expert((1, 1, d_ff)),
                hbm, by_expert((1, 1, D_MODEL)),
            ],
            out_specs=pl.BlockSpec((bm, D_MODEL // 2), lambda i, *_: (i, 0)),
            scratch_shapes=[
                pltpu.VMEM((2, D_MODEL, d_ff), jnp.float32),
                pltpu.VMEM((2, D_MODEL, d_ff), jnp.float32),
                pltpu.VMEM((2, d_ff, D_MODEL), jnp.float32),
                pltpu.VMEM((D_MODEL, d_ff), jnp.bfloat16),
                pltpu.VMEM((D_MODEL, d_ff), jnp.bfloat16),
                pltpu.VMEM((d_ff, D_MODEL), jnp.bfloat16),
                pltpu.SemaphoreType.DMA((2, 3)),
            ],
        ),
        out_shape=jax.ShapeDtypeStruct((n_rows, D_MODEL // 2), jnp.int32),
        compiler_params=pltpu.CompilerParams(
            dimension_semantics=("arbitrary",), vmem_limit_bytes=VMEM_LIMIT_BYTES),
        name="experts",
    )(block_e, n_valid, is_first, next_e, group % 2, x_rows, w_gate, b_gate.reshape(N_EXPERTS, 1, d_ff), w_up,
      b_up.reshape(N_EXPERTS, 1, d_ff), w_down, b_down.reshape(N_EXPERTS, 1, D_MODEL))


def _combine_kernel(h_ref, y_ref, meta_ref, g_ref, b_ref, o_ref):
    tm = h_ref.shape[0]
    meta_cols = jnp.concatenate([meta_ref[...], jnp.zeros((128 - META_ROWS, tm), jnp.float32)], axis=0).T
    ffn_hi, ffn_lo = 0.0, 0.0
    for k in range(TOP_K):
        gate = meta_cols[:, 2 * TOP_K + k:2 * TOP_K + k + 1]
        y_hi, y_lo = _unpack_bf16_pairs(y_ref[k])
        ffn_hi = ffn_hi + gate * y_hi
        ffn_lo = ffn_lo + gate * y_lo
    ffn = jnp.concatenate([ffn_hi, ffn_lo], axis=1)
    o_ref[...] = _layer_norm(DEEPNORM_ALPHA * h_ref[...] + ffn, g_ref[...], b_ref[...])


def _combine(h, y_tok, meta, ln2_g, ln2_b):
    t = h.shape[0]
    tm = COMBINE_ROWS
    return pl.pallas_call(
        _combine_kernel,
        grid=(t // tm,),
        in_specs=[
            pl.BlockSpec((tm, D_MODEL), lambda i: (i, 0)),
            pl.BlockSpec((TOP_K, tm, D_MODEL // 2), lambda i: (0, i, 0)),
            pl.BlockSpec((META_ROWS, tm), lambda i: (0, i)),
            pl.BlockSpec((1, D_MODEL), lambda i: (0, 0)),
            pl.BlockSpec((1, D_MODEL), lambda i: (0, 0)),
        ],
        out_specs=pl.BlockSpec((tm, D_MODEL), lambda i: (i, 0)),
        out_shape=jax.ShapeDtypeStruct((t, D_MODEL), jnp.float32),
        compiler_params=pltpu.CompilerParams(
            dimension_semantics=("arbitrary",), vmem_limit_bytes=VMEM_LIMIT_BYTES),
        name="combine",
    )(h, y_tok, meta, ln2_g, ln2_b)


def _layer(x2, seq_len, w_in, b_in, sinks, ln_v_g, ln_v_b, w_spatial, b_spatial, w_out, b_out,
           ln1_g, ln1_b, w_router, b_router, w_gate, b_gate, w_up, b_up, w_down, b_down, ln2_g, ln2_b):
    t = x2.shape[0]
    tk = t * TOP_K
    bm = EXPERT_ROWS
    bf16 = jnp.bfloat16

    w_r_hi = w_router.astype(bf16)
    w_r_lo = (w_router - w_r_hi.astype(jnp.float32)).astype(bf16)
    lane_pad = jnp.zeros((D_MODEL, ROUTER_LANES - 2 * N_EXPERTS), bf16)
    w_r = jnp.concatenate([w_r_hi, w_r_lo, lane_pad], axis=1)
    w_r_hi_only = jnp.concatenate([w_r_hi, jnp.zeros_like(w_r_lo), lane_pad], axis=1)
    b_sp_full = jnp.repeat(b_spatial.T, GMLP_GROUP_DIM, axis=1)

    h, h_packed, meta, counts = _mixer(
        x2, sinks, w_in.astype(bf16), b_in[None], ln_v_g[None], ln_v_b[None], w_spatial, b_sp_full,
        w_out.astype(bf16), b_out[None], ln1_g[None], ln1_b[None], w_r, w_r_hi_only, b_router[:, None], seq_len)

    counts = counts[:, 0].astype(jnp.int32)
    experts = jnp.arange(N_EXPERTS, dtype=jnp.int32)
    padded = (counts + bm - 1) // bm * bm
    padded_end = jnp.sum(jnp.where(experts[None, :] <= experts[:, None], padded[None, :], 0), axis=1)
    padded_start = padded_end - padded
    n_blocks = tk // bm + N_EXPERTS
    n_rows = n_blocks * bm
    top_idx_t = meta[:TOP_K].astype(jnp.int32)
    rank_t = meta[TOP_K:2 * TOP_K].astype(jnp.int32)
    dest_t = rank_t + jnp.sum(
        jnp.where(top_idx_t[None] == experts[:, None, None], padded_start[:, None, None], 0), axis=0)
    block_start = jnp.arange(n_blocks, dtype=jnp.int32) * bm
    block_e = jnp.minimum(
        jnp.sum((padded_end[None, :] <= block_start[:, None]).astype(jnp.int32), axis=1), N_EXPERTS - 1)
    valid_end = jnp.sum(jnp.where(block_e[:, None] == experts[None, :], (padded_start + counts)[None, :], 0), axis=1)
    n_valid = jnp.clip(valid_end - block_start, 0, bm)

    x_rows = _dispatch_rows(dest_t, h_packed, n_rows)
    y_rows = _experts(block_e, n_valid, x_rows, w_gate, b_gate, w_up, b_up, w_down, b_down)
    y_tok = _gather_rows(dest_t.reshape(-1), y_rows).reshape(TOP_K, t, D_MODEL // 2)
    return _combine(h, y_tok, meta, ln2_g[None], ln2_b[None])


def kernel(x, w_in, b_in, sinks, ln_v_g, ln_v_b, w_spatial, b_spatial, w_out, b_out, ln1_g, ln1_b,
           w_router, b_router, w_gate, b_gate, w_up, b_up, w_down, b_down, ln2_g, ln2_b):
    batch, seq_len, d = x.shape
    x2 = x.reshape(batch * seq_len, d)
    for l in range(DEPTH):
        x2 = _layer(x2, seq_len, w_in[l], b_in[l], sinks[l], ln_v_g[l], ln_v_b[l], w_spatial[l],
                    b_spatial[l], w_out[l], b_out[l], ln1_g[l], ln1_b[l], w_router[l], b_router[l],
                    w_gate[l], b_gate[l], w_up[l], b_up[l], w_down[l], b_down[l], ln2_g[l], ln2_b[l])
    return x2.reshape(batch, seq_len, d)
```

```python
import functools

import jax
import jax.numpy as jnp
from jax import lax
from jax.experimental import pallas as pl
from jax.experimental.pallas import tpu as pltpu
from jax.experimental.pallas import tpu_sc as plsc

D_MODEL = 1024
HEAD_DIM = 64
N_Q_HEADS = 8
N_KV_HEADS = 2
Q_REP = N_Q_HEADS // N_KV_HEADS
ATTN_WIDTH = N_Q_HEADS * HEAD_DIM
KV_WIDTH = N_KV_HEADS * HEAD_DIM
ATTN_BLOCK = 128
N_GMLP_GROUPS = 8
GMLP_WIDTH = D_MODEL - ATTN_WIDTH
GMLP_GROUP_DIM = GMLP_WIDTH // N_GMLP_GROUPS
IN_WIDTH = ATTN_WIDTH + 2 * KV_WIDTH + 2 * GMLP_WIDTH
N_EXPERTS = 32
TOP_K = 4
SWIGLU_LIMIT = 7.0
SWIGLU_ALPHA = 1.702
LN_EPS = 1e-5
DEPTH = 1
DEEPNORM_ALPHA = (2.0 * DEPTH) ** 0.25
NEG_INF = -1e30
LOG2_E = 1.4426950408889634

MIXER_ROWS = 256
EXPERT_ROWS = 512
GATHER_WINDOW = 64
META_ROWS = 16
ROUTER_LANES = 128
COMBINE_ROWS = 256
COMBINE_CHUNKS = 4
VMEM_LIMIT_BYTES = 56 * 1024 * 1024

_O_K = ATTN_WIDTH
_O_V = _O_K + KV_WIDTH
_O_U = _O_V + KV_WIDTH
_O_G = _O_U + GMLP_WIDTH


def _pack_bf16_pairs(v):
    n = v.shape[1] // 2
    hi = lax.bitcast_convert_type(v[:, :n].astype(jnp.float32), jnp.int32)
    lo = lax.bitcast_convert_type(v[:, n:].astype(jnp.float32), jnp.int32)
    return hi | lax.shift_right_logical(lo, 16)


def _unpack_bf16_pairs(p):
    hi = lax.bitcast_convert_type(p & jnp.int32(-65536), jnp.float32)
    lo = lax.bitcast_convert_type(lax.shift_left(p, 16), jnp.float32)
    return hi, lo


def _layer_norm(v, g, b):
    mu = jnp.mean(v, axis=-1, keepdims=True)
    vc = v - mu
    var = jnp.mean(vc * vc, axis=-1, keepdims=True)
    return vc * lax.rsqrt(var + LN_EPS) * g + b


def _attention_block(q, kb, vb, bias_ref, bias_sel):
    outs = []
    for g in range(N_KV_HEADS):
        kg = kb[:, g * HEAD_DIM:(g + 1) * HEAD_DIM]
        vg = vb[:, g * HEAD_DIM:(g + 1) * HEAD_DIM]
        qg = jnp.concatenate(
            [q[:, (g * Q_REP + r) * HEAD_DIM:(g * Q_REP + r + 1) * HEAD_DIM] for r in range(Q_REP)],
            axis=0).astype(jnp.bfloat16)
        s = (lax.dot_general(qg, kg, (((1,), (1,)), ((), ())), preferred_element_type=jnp.float32)
             + bias_ref[g, bias_sel])
        p = jnp.exp2(s - jnp.max(s, axis=-1, keepdims=True))
        denom = jnp.sum(p, axis=-1, keepdims=True)
        o = jnp.dot(p.astype(jnp.bfloat16), vg, preferred_element_type=jnp.float32) / denom
        outs.extend(o[r * ATTN_BLOCK:(r + 1) * ATTN_BLOCK] for r in range(Q_REP))
    return jnp.concatenate(outs, axis=-1)


def _mixer_kernel(x_ref, w_in_ref, b_in_ref, bias_ref, lnv_g_ref, lnv_b_ref, grp_avg_ref,
                  w_sp_ref, b_sp_ref, w_out_ref, b_out_ref, ln1_g_ref, ln1_b_ref,
                  w_r_ref, w_r_hi_ref, b_r_ref, tri_ref,
                  h_ref, hp_ref, meta_ref, count_ref,
                  kv_prev_ref, *, steps_per_seq):
    i = pl.program_id(0)
    first_step = (i % steps_per_seq) == 0
    tm = x_ref.shape[0]
    n_sub = tm // ATTN_BLOCK

    @pl.when(i == 0)
    def _():
        count_ref[...] = jnp.zeros_like(count_ref)

    @pl.when(first_step)
    def _():
        kv_prev_ref[...] = jnp.zeros_like(kv_prev_ref)

    x = x_ref[...]
    proj = jnp.dot(x.astype(jnp.bfloat16), w_in_ref[...], preferred_element_type=jnp.float32) + b_in_ref[...]

    q_all = proj[:, :_O_K] * (LOG2_E * HEAD_DIM ** -0.5)
    k_all = proj[:, _O_K:_O_V].astype(jnp.bfloat16)
    v_all = proj[:, _O_V:_O_U].astype(jnp.bfloat16)
    k_prev = kv_prev_ref[:, :KV_WIDTH]
    v_prev = kv_prev_ref[:, KV_WIDTH:]
    is_row0 = lax.broadcasted_iota(jnp.int32, (ATTN_BLOCK, KV_WIDTH), 0) == 0
    attn_blocks = []
    for sb in range(n_sub):
        rows = slice(sb * ATTN_BLOCK, (sb + 1) * ATTN_BLOCK)
        k_cur, v_cur = k_all[rows], v_all[rows]
        kb = jnp.concatenate([jnp.where(is_row0, 0, k_prev), k_cur], axis=0)
        vb = jnp.concatenate([jnp.where(is_row0, 0, v_prev), v_cur], axis=0)
        bias_sel = jnp.where(first_step, 1, 0) if sb == 0 else 0
        attn_blocks.append(_attention_block(q_all[rows], kb, vb, bias_ref, bias_sel))
        k_prev, v_prev = k_cur, v_cur
    kv_prev_ref[:, :KV_WIDTH] = k_prev
    kv_prev_ref[:, KV_WIDTH:] = v_prev
    attn = jnp.concatenate(attn_blocks, axis=0)

    u = jax.nn.gelu(proj[:, _O_U:_O_G])
    gg = jax.nn.gelu(proj[:, _O_G:])
    avg = grp_avg_ref[...]
    mu = jnp.dot(gg.astype(jnp.bfloat16), avg, preferred_element_type=jnp.float32)
    gc = gg - mu
    var = jnp.dot((gc * gc).astype(jnp.bfloat16), avg, preferred_element_type=jnp.float32)
    gn = (gc * lax.rsqrt(var + LN_EPS) * lnv_g_ref[...] + lnv_b_ref[...]).astype(jnp.bfloat16)
    causal = (lax.broadcasted_iota(jnp.int32, (ATTN_BLOCK, ATTN_BLOCK), 0)
              >= lax.broadcasted_iota(jnp.int32, (ATTN_BLOCK, ATTN_BLOCK), 1))
    w_sp = [jnp.where(causal, w_sp_ref[g], 0.0).astype(jnp.bfloat16) for g in range(N_GMLP_GROUPS)]
    mixed_chunks = []
    for c in range(n_sub):
        rows = slice(c * ATTN_BLOCK, (c + 1) * ATTN_BLOCK)
        pieces = [
            jnp.dot(w_sp[g], gn[rows, g * GMLP_GROUP_DIM:(g + 1) * GMLP_GROUP_DIM],
                    preferred_element_type=jnp.float32)
            for g in range(N_GMLP_GROUPS)]
        mixed_chunks.append(jnp.concatenate(pieces, axis=-1) + b_sp_ref[...])
    sgu = u * jnp.concatenate(mixed_chunks, axis=0)

    mix = (jnp.dot(attn.astype(jnp.bfloat16), w_out_ref[:ATTN_WIDTH, :], preferred_element_type=jnp.float32)
           + jnp.dot(sgu.astype(jnp.bfloat16), w_out_ref[ATTN_WIDTH:, :], preferred_element_type=jnp.float32)
           + b_out_ref[...])
    h = _layer_norm(DEEPNORM_ALPHA * x + mix, ln1_g_ref[...], ln1_b_ref[...])
    h_ref[...] = h

    h_hi = h.astype(jnp.bfloat16)
    hp_ref[...] = _pack_bf16_pairs(h_hi)
    h_lo = (h - h_hi.astype(jnp.float32)).astype(jnp.bfloat16)
    part = (jnp.dot(h_hi, w_r_ref[...], preferred_element_type=jnp.float32)
            + jnp.dot(h_lo, w_r_hi_ref[...], preferred_element_type=jnp.float32)).T
    logits = part[:N_EXPERTS] + part[N_EXPERTS:2 * N_EXPERTS] + b_r_ref[...]

    n_grp = N_EXPERTS // 8
    grp = [logits[8 * g:8 * (g + 1)] for g in range(n_grp)]
    sub = lax.broadcasted_iota(jnp.int32, (8, tm), 0)
    beaten = [jnp.zeros((8, tm), jnp.float32) for _ in range(n_grp)]
    for e2 in range(N_EXPERTS):
        g2, r2 = divmod(e2, 8)
        row = logits[e2:e2 + 1]
        for g in range(n_grp):
            if g > g2:
                wins = jnp.where(row >= grp[g], 1.0, 0.0)
            elif g < g2:
                wins = jnp.where(row > grp[g], 1.0, 0.0)
            else:
                wins = jnp.where(sub > r2, jnp.where(row >= grp[g], 1.0, 0.0), jnp.where(row > grp[g], 1.0, 0.0))
            beaten[g] = beaten[g] + wins
    place = jnp.concatenate(beaten, axis=0)
    expert_id = lax.broadcasted_iota(jnp.int32, (N_EXPERTS, tm), 0).astype(jnp.float32)
    onehot = jnp.where(place < TOP_K, 1.0, 0.0)
    before = jnp.dot(onehot.astype(jnp.bfloat16), tri_ref[...], preferred_element_type=jnp.float32) + count_ref[...]

    def pick(k, table):
        return jnp.sum(jnp.where(place == k, table, 0.0), axis=0, keepdims=True)

    vals = [pick(k, logits) for k in range(TOP_K)]
    exps = [jnp.exp(v - vals[0]) for v in vals]
    denom = exps[0] + exps[1] + exps[2] + exps[3]
    meta_ref[...] = jnp.zeros_like(meta_ref)
    for k in range(TOP_K):
        meta_ref[k:k + 1, :] = pick(k, expert_id)
        meta_ref[TOP_K + k:TOP_K + k + 1, :] = pick(k, before)
        meta_ref[2 * TOP_K + k:2 * TOP_K + k + 1, :] = exps[k] / denom
    count_ref[...] += jnp.sum(onehot, axis=1, keepdims=True)


def _score_bias(sinks):
    t_idx = jnp.arange(Q_REP * ATTN_BLOCK)[:, None] % ATTN_BLOCK
    s_idx = jnp.arange(2 * ATTN_BLOCK)[None, :]
    diff = t_idx + ATTN_BLOCK - s_idx
    band = (diff >= 0) & (diff < ATTN_BLOCK)
    masks = jnp.stack([band, band & (s_idx >= ATTN_BLOCK)])
    bias = jnp.where(masks, 0.0, NEG_INF).astype(jnp.float32)
    sink_rows = jnp.repeat(sinks.reshape(N_KV_HEADS, Q_REP) * LOG2_E, ATTN_BLOCK, axis=1)
    return jnp.where(s_idx == 0, sink_rows[:, None, :, None], bias[None])


def _mixer(x2, sinks, w_in, b_in, lnv_g, lnv_b, w_sp, b_sp_full, w_out, b_out, ln1_g, ln1_b,
           w_r, w_r_hi, b_r, seq_len):
    t = x2.shape[0]
    tm = MIXER_ROWS
    grp = jnp.arange(GMLP_WIDTH) // GMLP_GROUP_DIM
    grp_avg = jnp.where(grp[:, None] == grp[None, :], 1.0 / GMLP_GROUP_DIM, 0.0).astype(jnp.bfloat16)
    tri = (jnp.arange(tm)[:, None] < jnp.arange(tm)[None, :]).astype(jnp.bfloat16)

    def full(shape):
        return pl.BlockSpec(shape, lambda i: (0,) * len(shape))

    return pl.pallas_call(
        functools.partial(_mixer_kernel, steps_per_seq=seq_len // tm),
        grid=(t // tm,),
        in_specs=[
            pl.BlockSpec((tm, D_MODEL), lambda i: (i, 0)),
            full((D_MODEL, IN_WIDTH)), full((1, IN_WIDTH)),
            full((N_KV_HEADS, 2, Q_REP * ATTN_BLOCK, 2 * ATTN_BLOCK)),
            full((1, GMLP_WIDTH)), full((1, GMLP_WIDTH)), full((GMLP_WIDTH, GMLP_WIDTH)),
            full((N_GMLP_GROUPS, ATTN_BLOCK, ATTN_BLOCK)), full((ATTN_BLOCK, GMLP_WIDTH)),
            full((D_MODEL, D_MODEL)), full((1, D_MODEL)), full((1, D_MODEL)), full((1, D_MODEL)),
            full((D_MODEL, ROUTER_LANES)), full((D_MODEL, ROUTER_LANES)), full((N_EXPERTS, 1)), full((tm, tm)),
        ],
        out_specs=[
            pl.BlockSpec((tm, D_MODEL), lambda i: (i, 0)),
            pl.BlockSpec((tm, D_MODEL // 2), lambda i: (i, 0)),
            pl.BlockSpec((META_ROWS, tm), lambda i: (0, i)),
            pl.BlockSpec((N_EXPERTS, 1), lambda i: (0, 0)),
        ],
        out_shape=[
            jax.ShapeDtypeStruct((t, D_MODEL), jnp.float32),
            jax.ShapeDtypeStruct((t, D_MODEL // 2), jnp.int32),
            jax.ShapeDtypeStruct((META_ROWS, t), jnp.float32),
            jax.ShapeDtypeStruct((N_EXPERTS, 1), jnp.float32),
        ],
        scratch_shapes=[pltpu.VMEM((ATTN_BLOCK, 2 * KV_WIDTH), jnp.bfloat16)],
        compiler_params=pltpu.CompilerParams(
            dimension_semantics=("arbitrary",), vmem_limit_bytes=VMEM_LIMIT_BYTES),
        name="mixer",
    )(x2, w_in, b_in, _score_bias(sinks), lnv_g, lnv_b, grp_avg, w_sp, b_sp_full, w_out, b_out, ln1_g, ln1_b,
      w_r, w_r_hi, b_r, tri)


def _gather_rows(idx, src):
    n = idx.shape[0]
    width = src.shape[1]
    win = GATHER_WINDOW
    sc = plsc.get_sparse_core_info()
    n_workers = sc.num_cores * sc.num_subcores
    per_worker = n // n_workers
    n_pairs = per_worker // (2 * win)
    assert n_pairs * 2 * win * n_workers == n
    mesh = plsc.VectorSubcoreMesh(core_axis_name="core", subcore_axis_name="subcore")

    @functools.partial(
        pl.kernel, out_type=jax.ShapeDtypeStruct((n, width), src.dtype), mesh=mesh,
        scratch_types=[pltpu.VMEM((per_worker,), jnp.int32), pltpu.VMEM((2, win, width), src.dtype),
                       pltpu.SemaphoreType.DMA((2,)), pltpu.SemaphoreType.DMA((2,))],
        name="gather_rows")
    def gather(src_hbm, idx_hbm, out_hbm, idx_v, rows_v, fetch_sem, store_sem):
        worker = lax.axis_index("subcore") * sc.num_cores + lax.axis_index("core")
        base = worker * per_worker
        pltpu.sync_copy(idx_hbm.at[pl.ds(base, per_worker)], idx_v)

        def fetch(chunk, buf):
            return pltpu.make_async_copy(src_hbm.at[idx_v.at[pl.ds(chunk * win, win)]], rows_v.at[buf],
                                         fetch_sem.at[buf])

        def store(chunk, buf):
            return pltpu.make_async_copy(rows_v.at[buf], out_hbm.at[pl.ds(base + chunk * win, win)],
                                         store_sem.at[buf])

        @pl.loop(0, n_pairs)
        def _(p):
            for buf in range(2):
                @pl.when(p > 0)
                def _():
                    store(2 * p - 2 + buf, buf).wait()
                fetch(2 * p + buf, buf).start()
            for buf in range(2):
                fetch(2 * p + buf, buf).wait()
                store(2 * p + buf, buf).start()

        for buf in range(2):
            store(2 * n_pairs - 2 + buf, buf).wait()

    return gather(src, idx)


def _dispatch_rows(dest_t, src, n_rows):
    t, width = src.shape
    win = GATHER_WINDOW
    sc = plsc.get_sparse_core_info()
    n_workers = sc.num_cores * sc.num_subcores
    per_worker = t // n_workers
    n_chunks = per_worker // win
    n_pairs = n_chunks // 2
    assert n_pairs * 2 * win * n_workers == t
    idx = dest_t.reshape(TOP_K, n_workers, n_chunks, win).transpose(1, 0, 2, 3)
    idx = idx.reshape(n_workers, TOP_K * n_chunks, win)
    mesh = plsc.VectorSubcoreMesh(core_axis_name="core", subcore_axis_name="subcore")

    @functools.partial(
        pl.kernel, out_type=jax.ShapeDtypeStruct((n_rows, width), src.dtype), mesh=mesh,
        scratch_types=[pltpu.VMEM((TOP_K * n_chunks, win), jnp.int32), pltpu.VMEM((2, win, width), src.dtype),
                       pltpu.SemaphoreType.DMA((2,)), pltpu.SemaphoreType.DMA((2,))],
        name="dispatch_rows")
    def dispatch(src_hbm, idx_hbm, out_hbm, idx_v, rows_v, fetch_sem, store_sem):
        worker = lax.axis_index("subcore") * sc.num_cores + lax.axis_index("core")
        base = worker * per_worker
        pltpu.sync_copy(idx_hbm.at[worker], idx_v)

        def fetch(chunk, buf):
            return pltpu.make_async_copy(src_hbm.at[pl.ds(base + chunk * win, win)], rows_v.at[buf],
                                         fetch_sem.at[buf])

        def store(chunk, k, buf):
            return pltpu.make_async_copy(rows_v.at[buf], out_hbm.at[idx_v.at[k * n_chunks + chunk]],
                                         store_sem.at[buf])

        @pl.loop(0, n_pairs)
        def _(p):
            for buf in range(2):
                @pl.when(p > 0)
                def _():
                    for k in range(TOP_K):
                        store(2 * p - 2 + buf, k, buf).wait()
                fetch(2 * p + buf, buf).start()
            for buf in range(2):
                fetch(2 * p + buf, buf).wait()
                for k in range(TOP_K):
                    store(2 * p + buf, k, buf).start()

        for buf in range(2):
            for k in range(TOP_K):
                store(2 * n_pairs - 2 + buf, k, buf).wait()

    return dispatch(src, idx)


def _expert_kernel(block_e_ref, n_valid_ref, first_ref, next_e_ref, slot_ref,
                   x_ref, wg_hbm, bg_ref, wu_hbm, bu_ref, wd_hbm, bd_ref,
                   y_ref, wg_f32, wu_f32, wd_f32, wg_bf, wu_bf, wd_bf, sem):
    i = pl.program_id(0)
    n_valid = n_valid_ref[i]
    slot = slot_ref[i]
    staged = ((wg_hbm, wg_f32, wg_bf), (wu_hbm, wu_f32, wu_bf), (wd_hbm, wd_f32, wd_bf))

    def weight_copy(m, expert, s):
        return pltpu.make_async_copy(staged[m][0].at[expert], staged[m][1].at[s], sem.at[s, m])

    @pl.when(i == 0)
    def _():
        for m in range(3):
            weight_copy(m, block_e_ref[0], 0).start()

    @pl.when(first_ref[i] == 1)
    def _():
        @pl.when(next_e_ref[i] >= 0)
        def _():
            for m in range(3):
                weight_copy(m, next_e_ref[i], 1 - slot).start()

        for m in range(3):
            weight_copy(m, block_e_ref[i], slot).wait()
            staged[m][2][...] = staged[m][1][slot].astype(jnp.bfloat16)

    @pl.when(n_valid > 0)
    def _():
        row = lax.broadcasted_iota(jnp.int32, (x_ref.shape[0], 1), 0)
        x_hi, x_lo = _unpack_bf16_pairs(jnp.where(row < n_valid, x_ref[...], 0))
        xb = jnp.concatenate([x_hi, x_lo], axis=1).astype(jnp.bfloat16)
        gt = jnp.minimum(jnp.dot(xb, wg_bf[...], preferred_element_type=jnp.float32) + bg_ref[0], SWIGLU_LIMIT)
        up = jnp.clip(jnp.dot(xb, wu_bf[...], preferred_element_type=jnp.float32) + bu_ref[0],
                      -SWIGLU_LIMIT, SWIGLU_LIMIT)
        hid = gt * jax.nn.sigmoid(SWIGLU_ALPHA * gt) * (up + 1.0)
        y = jnp.dot(hid.astype(jnp.bfloat16), wd_bf[...], preferred_element_type=jnp.float32) + bd_ref[0]
        y_ref[...] = _pack_bf16_pairs(y.astype(jnp.bfloat16))

    @pl.when(n_valid == 0)
    def _():
        y_ref[...] = jnp.zeros_like(y_ref)


def _experts(block_e, n_valid, x_rows, w_gate, b_gate, w_up, b_up, w_down, b_down):
    n_rows = x_rows.shape[0]
    bm = EXPERT_ROWS
    n_blocks = n_rows // bm
    d_ff = w_gate.shape[2]

    blocks = jnp.arange(n_blocks, dtype=jnp.int32)
    is_first = jnp.concatenate([jnp.ones((1,), jnp.int32), (block_e[1:] != block_e[:-1]).astype(jnp.int32)])
    group = jnp.sum(jnp.where(blocks[None, :] <= blocks[:, None], is_first[None, :], 0), axis=1) - 1
    next_e = jnp.min(jnp.where(block_e[None, :] > block_e[:, None], block_e[None, :], N_EXPERTS), axis=1)
    next_e = jnp.where(next_e == N_EXPERTS, -1, next_e)

    def by_expert(shape):
        return pl.BlockSpec(shape, lambda i, be, *_: (be[i],) + (0,) * (len(shape) - 1))

    hbm = pl.BlockSpec(memory_space=pl.ANY)
    return pl.pallas_call(
        _expert_kernel,
        grid_spec=pltpu.PrefetchScalarGridSpec(
            num_scalar_prefetch=5,
            grid=(n_blocks,),
            in_specs=[
                pl.BlockSpec((bm, D_MODEL // 2), lambda i, *_: (i, 0)),
                hbm, by_expert((1, 1, d_ff)),
                hbm, by_expert((1, 1, d_ff)),
                hbm, by_expert((1, 1, D_MODEL)),
            ],
            out_specs=pl.BlockSpec((bm, D_MODEL // 2), lambda i, *_: (i, 0)),
            scratch_shapes=[
                pltpu.VMEM((2, D_MODEL, d_ff), jnp.float32),
                pltpu.VMEM((2, D_MODEL, d_ff), jnp.float32),
                pltpu.VMEM((2, d_ff, D_MODEL), jnp.float32),
                pltpu.VMEM((D_MODEL, d_ff), jnp.bfloat16),
                pltpu.VMEM((D_MODEL, d_ff), jnp.bfloat16),
                pltpu.VMEM((d_ff, D_MODEL), jnp.bfloat16),
                pltpu.SemaphoreType.DMA((2, 3)),
            ],
        ),
        out_shape=jax.ShapeDtypeStruct((n_rows, D_MODEL // 2), jnp.int32),
        compiler_params=pltpu.CompilerParams(
            dimension_semantics=("arbitrary",), vmem_limit_bytes=VMEM_LIMIT_BYTES),
        name="experts",
    )(block_e, n_valid, is_first, next_e, group % 2, x_rows, w_gate, b_gate.reshape(N_EXPERTS, 1, d_ff), w_up,
      b_up.reshape(N_EXPERTS, 1, d_ff), w_down, b_down.reshape(N_EXPERTS, 1, D_MODEL))


def _combine_kernel(h_ref, y_ref, meta_ref, g_ref, b_ref, *rest):
    o_ref = rest[-1]
    tm = h_ref.shape[0]
    meta_cols = jnp.concatenate([meta_ref[...], jnp.zeros((128 - META_ROWS, tm), jnp.float32)], axis=0).T
    ffn_hi, ffn_lo = 0.0, 0.0
    for k in range(TOP_K):
        gate = meta_cols[:, 2 * TOP_K + k:2 * TOP_K + k + 1]
        y_hi, y_lo = _unpack_bf16_pairs(y_ref[k])
        ffn_hi = ffn_hi + gate * y_hi
        ffn_lo = ffn_lo + gate * y_lo
    ffn = jnp.concatenate([ffn_hi, ffn_lo], axis=1)
    o_ref[...] = _layer_norm(DEEPNORM_ALPHA * h_ref[...] + ffn, g_ref[...], b_ref[...])


def _combine(h, y_tok, meta, ln2_g, ln2_b, chunk, out_prev):
    t = h.shape[0]
    tm = COMBINE_ROWS
    steps = y_tok.shape[1] // tm
    first = chunk * steps
    in_specs = [
        pl.BlockSpec((tm, D_MODEL), lambda i: (first + i, 0)),
        pl.BlockSpec((TOP_K, tm, D_MODEL // 2), lambda i: (0, i, 0)),
        pl.BlockSpec((META_ROWS, tm), lambda i: (0, first + i)),
        pl.BlockSpec((1, D_MODEL), lambda i: (0, 0)),
        pl.BlockSpec((1, D_MODEL), lambda i: (0, 0)),
    ]
    args = [h, y_tok, meta, ln2_g, ln2_b]
    aliases = {}
    if out_prev is not None:
        in_specs.append(pl.BlockSpec(memory_space=pl.ANY))
        args.append(out_prev)
        aliases = {len(args) - 1: 0}
    return pl.pallas_call(
        _combine_kernel,
        grid=(steps,),
        in_specs=in_specs,
        out_specs=pl.BlockSpec((tm, D_MODEL), lambda i: (first + i, 0)),
        out_shape=jax.ShapeDtypeStruct((t, D_MODEL), jnp.float32),
        input_output_aliases=aliases,
        compiler_params=pltpu.CompilerParams(
            dimension_semantics=("arbitrary",), vmem_limit_bytes=VMEM_LIMIT_BYTES),
        name="combine",
    )(*args)


def _layer(x2, seq_len, w_in, b_in, sinks, ln_v_g, ln_v_b, w_spatial, b_spatial, w_out, b_out,
           ln1_g, ln1_b, w_router, b_router, w_gate, b_gate, w_up, b_up, w_down, b_down, ln2_g, ln2_b):
    t = x2.shape[0]
    tk = t * TOP_K
    bm = EXPERT_ROWS
    bf16 = jnp.bfloat16

    w_r_hi = w_router.astype(bf16)
    w_r_lo = (w_router - w_r_hi.astype(jnp.float32)).astype(bf16)
    lane_pad = jnp.zeros((D_MODEL, ROUTER_LANES - 2 * N_EXPERTS), bf16)
    w_r = jnp.concatenate([w_r_hi, w_r_lo, lane_pad], axis=1)
    w_r_hi_only = jnp.concatenate([w_r_hi, jnp.zeros_like(w_r_lo), lane_pad], axis=1)
    b_sp_full = jnp.repeat(b_spatial.T, GMLP_GROUP_DIM, axis=1)

    h, h_packed, meta, counts = _mixer(
        x2, sinks, w_in.astype(bf16), b_in[None], ln_v_g[None], ln_v_b[None], w_spatial, b_sp_full,
        w_out.astype(bf16), b_out[None], ln1_g[None], ln1_b[None], w_r, w_r_hi_only, b_router[:, None], seq_len)

    counts = counts[:, 0].astype(jnp.int32)
    experts = jnp.arange(N_EXPERTS, dtype=jnp.int32)
    padded = (counts + bm - 1) // bm * bm
    padded_end = jnp.sum(jnp.where(experts[None, :] <= experts[:, None], padded[None, :], 0), axis=1)
    padded_start = padded_end - padded
    n_blocks = tk // bm + N_EXPERTS
    n_rows = n_blocks * bm
    top_idx_t = meta[:TOP_K].astype(jnp.int32)
    rank_t = meta[TOP_K:2 * TOP_K].astype(jnp.int32)
    dest_t = rank_t + jnp.sum(
        jnp.where(top_idx_t[None] == experts[:, None, None], padded_start[:, None, None], 0), axis=0)
    block_start = jnp.arange(n_blocks, dtype=jnp.int32) * bm
    block_e = jnp.minimum(
        jnp.sum((padded_end[None, :] <= block_start[:, None]).astype(jnp.int32), axis=1), N_EXPERTS - 1)
    valid_end = jnp.sum(jnp.where(block_e[:, None] == experts[None, :], (padded_start + counts)[None, :], 0), axis=1)
    n_valid = jnp.clip(valid_end - block_start, 0, bm)

    x_rows = _dispatch_rows(dest_t, h_packed, n_rows)
    y_rows = _experts(block_e, n_valid, x_rows, w_gate, b_gate, w_up, b_up, w_down, b_down)
    tc = t // COMBINE_CHUNKS
    out = None
    for c in range(COMBINE_CHUNKS):
        idx = dest_t[:, c * tc:(c + 1) * tc].reshape(-1)
        y_tok = _gather_rows(idx, y_rows).reshape(TOP_K, tc, D_MODEL // 2)
        out = _combine(h, y_tok, meta, ln2_g[None], ln2_b[None], c, out)
    return out


def kernel(x, w_in, b_in, sinks, ln_v_g, ln_v_b, w_spatial, b_spatial, w_out, b_out, ln1_g, ln1_b,
           w_router, b_router, w_gate, b_gate, w_up, b_up, w_down, b_down, ln2_g, ln2_b):
    batch, seq_len, d = x.shape
    x2 = x.reshape(batch * seq_len, d)
    for l in range(DEPTH):
        x2 = _layer(x2, seq_len, w_in[l], b_in[l], sinks[l], ln_v_g[l], ln_v_b[l], w_spatial[l],
                    b_spatial[l], w_out[l], b_out[l], ln1_g[l], ln1_b[l], w_router[l], b_router[l],
                    w_gate[l], b_gate[l], w_up[l], b_up[l], w_down[l], b_down[l], ln2_g[l], ln2_b[l])
    return x2.reshape(batch, seq_len, d)
```

```python
import functools

import jax
import jax.numpy as jnp
from jax import lax
from jax.experimental import pallas as pl
from jax.experimental.pallas import tpu as pltpu
from jax.experimental.pallas import tpu_sc as plsc

D_MODEL = 1024
HEAD_DIM = 64
N_Q_HEADS = 8
N_KV_HEADS = 2
Q_REP = N_Q_HEADS // N_KV_HEADS
ATTN_WIDTH = N_Q_HEADS * HEAD_DIM
KV_WIDTH = N_KV_HEADS * HEAD_DIM
ATTN_BLOCK = 128
N_GMLP_GROUPS = 8
GMLP_WIDTH = D_MODEL - ATTN_WIDTH
GMLP_GROUP_DIM = GMLP_WIDTH // N_GMLP_GROUPS
IN_WIDTH = ATTN_WIDTH + 2 * KV_WIDTH + 2 * GMLP_WIDTH
N_EXPERTS = 32
TOP_K = 4
SWIGLU_LIMIT = 7.0
SWIGLU_ALPHA = 1.702
LN_EPS = 1e-5
DEPTH = 1
DEEPNORM_ALPHA = (2.0 * DEPTH) ** 0.25
NEG_INF = -1e30
LOG2_E = 1.4426950408889634

MIXER_ROWS = 256
EXPERT_ROWS = 256
GATHER_WINDOW = 64
META_ROWS = 16
ROUTER_LANES = 128
COMBINE_ROWS = 256
VMEM_LIMIT_BYTES = 56 * 1024 * 1024

_O_K = ATTN_WIDTH
_O_V = _O_K + KV_WIDTH
_O_U = _O_V + KV_WIDTH
_O_G = _O_U + GMLP_WIDTH


def _pack_bf16_pairs(v):
    n = v.shape[1] // 2
    hi = lax.bitcast_convert_type(v[:, :n].astype(jnp.float32), jnp.int32)
    lo = lax.bitcast_convert_type(v[:, n:].astype(jnp.float32), jnp.int32)
    return hi | lax.shift_right_logical(lo, 16)


def _unpack_bf16_pairs(p):
    hi = lax.bitcast_convert_type(p & jnp.int32(-65536), jnp.float32)
    lo = lax.bitcast_convert_type(lax.shift_left(p, 16), jnp.float32)
    return hi, lo


def _layer_norm(v, g, b):
    mu = jnp.mean(v, axis=-1, keepdims=True)
    vc = v - mu
    var = jnp.mean(vc * vc, axis=-1, keepdims=True)
    return vc * lax.rsqrt(var + LN_EPS) * g + b


def _attention_block(q, kb, vb, bias_ref, bias_sel):
    outs = []
    for g in range(N_KV_HEADS):
        kg = kb[:, g * HEAD_DIM:(g + 1) * HEAD_DIM]
        vg = vb[:, g * HEAD_DIM:(g + 1) * HEAD_DIM]
        qg = jnp.concatenate(
            [q[:, (g * Q_REP + r) * HEAD_DIM:(g * Q_REP + r + 1) * HEAD_DIM] for r in range(Q_REP)],
            axis=0).astype(jnp.bfloat16)
        s = (lax.dot_general(qg, kg, (((1,), (1,)), ((), ())), preferred_element_type=jnp.float32)
             + bias_ref[g, bias_sel])
        p = jnp.exp2(s - jnp.max(s, axis=-1, keepdims=True))
        denom = jnp.sum(p, axis=-1, keepdims=True)
        o = jnp.dot(p.astype(jnp.bfloat16), vg, preferred_element_type=jnp.float32) / denom
        outs.extend(o[r * ATTN_BLOCK:(r + 1) * ATTN_BLOCK] for r in range(Q_REP))
    return jnp.concatenate(outs, axis=-1)


def _mixer_kernel(x_ref, w_in_ref, b_in_ref, bias_ref, lnv_g_ref, lnv_b_ref, grp_avg_ref,
                  w_sp_ref, b_sp_ref, w_out_ref, b_out_ref, ln1_g_ref, ln1_b_ref,
                  w_r_ref, w_r_hi_ref, b_r_ref, tri_ref,
                  h_ref, hp_ref, meta_ref, count_ref,
                  kv_prev_ref, *, steps_per_seq):
    i = pl.program_id(0)
    first_step = (i % steps_per_seq) == 0
    tm = x_ref.shape[0]
    n_sub = tm // ATTN_BLOCK

    @pl.when(i == 0)
    def _():
        count_ref[...] = jnp.zeros_like(count_ref)

    @pl.when(first_step)
    def _():
        kv_prev_ref[...] = jnp.zeros_like(kv_prev_ref)

    x = x_ref[...]
    proj = jnp.dot(x.astype(jnp.bfloat16), w_in_ref[...], preferred_element_type=jnp.float32) + b_in_ref[...]

    q_all = proj[:, :_O_K] * (LOG2_E * HEAD_DIM ** -0.5)
    k_all = proj[:, _O_K:_O_V].astype(jnp.bfloat16)
    v_all = proj[:, _O_V:_O_U].astype(jnp.bfloat16)
    k_prev = kv_prev_ref[:, :KV_WIDTH]
    v_prev = kv_prev_ref[:, KV_WIDTH:]
    is_row0 = lax.broadcasted_iota(jnp.int32, (ATTN_BLOCK, KV_WIDTH), 0) == 0
    attn_blocks = []
    for sb in range(n_sub):
        rows = slice(sb * ATTN_BLOCK, (sb + 1) * ATTN_BLOCK)
        k_cur, v_cur = k_all[rows], v_all[rows]
        kb = jnp.concatenate([jnp.where(is_row0, 0, k_prev), k_cur], axis=0)
        vb = jnp.concatenate([jnp.where(is_row0, 0, v_prev), v_cur], axis=0)
        bias_sel = jnp.where(first_step, 1, 0) if sb == 0 else 0
        attn_blocks.append(_attention_block(q_all[rows], kb, vb, bias_ref, bias_sel))
        k_prev, v_prev = k_cur, v_cur
    kv_prev_ref[:, :KV_WIDTH] = k_prev
    kv_prev_ref[:, KV_WIDTH:] = v_prev
    attn = jnp.concatenate(attn_blocks, axis=0)

    u = jax.nn.gelu(proj[:, _O_U:_O_G])
    gg = jax.nn.gelu(proj[:, _O_G:])
    avg = grp_avg_ref[...]
    mu = jnp.dot(gg.astype(jnp.bfloat16), avg, preferred_element_type=jnp.float32)
    gc = gg - mu
    var = jnp.dot((gc * gc).astype(jnp.bfloat16), avg, preferred_element_type=jnp.float32)
    gn = (gc * lax.rsqrt(var + LN_EPS) * lnv_g_ref[...] + lnv_b_ref[...]).astype(jnp.bfloat16)
    causal = (lax.broadcasted_iota(jnp.int32, (ATTN_BLOCK, ATTN_BLOCK), 0)
              >= lax.broadcasted_iota(jnp.int32, (ATTN_BLOCK, ATTN_BLOCK), 1))
    w_sp = [jnp.where(causal, w_sp_ref[g], 0.0).astype(jnp.bfloat16) for g in range(N_GMLP_GROUPS)]
    mixed_chunks = []
    for c in range(n_sub):
        rows = slice(c * ATTN_BLOCK, (c + 1) * ATTN_BLOCK)
        pieces = [
            jnp.dot(w_sp[g], gn[rows, g * GMLP_GROUP_DIM:(g + 1) * GMLP_GROUP_DIM],
                    preferred_element_type=jnp.float32)
            for g in range(N_GMLP_GROUPS)]
        mixed_chunks.append(jnp.concatenate(pieces, axis=-1) + b_sp_ref[...])
    sgu = u * jnp.concatenate(mixed_chunks, axis=0)

    mix = (jnp.dot(attn.astype(jnp.bfloat16), w_out_ref[:ATTN_WIDTH, :], preferred_element_type=jnp.float32)
           + jnp.dot(sgu.astype(jnp.bfloat16), w_out_ref[ATTN_WIDTH:, :], preferred_element_type=jnp.float32)
           + b_out_ref[...])
    h = _layer_norm(DEEPNORM_ALPHA * x + mix, ln1_g_ref[...], ln1_b_ref[...])
    h_ref[...] = h

    h_hi = h.astype(jnp.bfloat16)
    hp_ref[...] = _pack_bf16_pairs(h_hi)
    h_lo = (h - h_hi.astype(jnp.float32)).astype(jnp.bfloat16)
    part = (jnp.dot(h_hi, w_r_ref[...], preferred_element_type=jnp.float32)
            + jnp.dot(h_lo, w_r_hi_ref[...], preferred_element_type=jnp.float32)).T
    logits = part[:N_EXPERTS] + part[N_EXPERTS:2 * N_EXPERTS] + b_r_ref[...]

    n_grp = N_EXPERTS // 8
    grp = [logits[8 * g:8 * (g + 1)] for g in range(n_grp)]
    sub = lax.broadcasted_iota(jnp.int32, (8, tm), 0)
    beaten = [jnp.zeros((8, tm), jnp.float32) for _ in range(n_grp)]
    for e2 in range(N_EXPERTS):
        g2, r2 = divmod(e2, 8)
        row = logits[e2:e2 + 1]
        for g in range(n_grp):
            if g > g2:
                wins = jnp.where(row >= grp[g], 1.0, 0.0)
            elif g < g2:
                wins = jnp.where(row > grp[g], 1.0, 0.0)
            else:
                wins = jnp.where(sub > r2, jnp.where(row >= grp[g], 1.0, 0.0), jnp.where(row > grp[g], 1.0, 0.0))
            beaten[g] = beaten[g] + wins
    place = jnp.concatenate(beaten, axis=0)
    expert_id = lax.broadcasted_iota(jnp.int32, (N_EXPERTS, tm), 0).astype(jnp.float32)
    onehot = jnp.where(place < TOP_K, 1.0, 0.0)
    before = jnp.dot(onehot.astype(jnp.bfloat16), tri_ref[...], preferred_element_type=jnp.float32) + count_ref[...]

    def pick(k, table):
        return jnp.sum(jnp.where(place == k, table, 0.0), axis=0, keepdims=True)

    vals = [pick(k, logits) for k in range(TOP_K)]
    exps = [jnp.exp(v - vals[0]) for v in vals]
    denom = exps[0] + exps[1] + exps[2] + exps[3]
    meta_ref[...] = jnp.zeros_like(meta_ref)
    for k in range(TOP_K):
        meta_ref[k:k + 1, :] = pick(k, expert_id)
        meta_ref[TOP_K + k:TOP_K + k + 1, :] = pick(k, before)
        meta_ref[2 * TOP_K + k:2 * TOP_K + k + 1, :] = exps[k] / denom
    count_ref[...] += jnp.sum(onehot, axis=1, keepdims=True)


def _score_bias(sinks):
    t_idx = jnp.arange(Q_REP * ATTN_BLOCK)[:, None] % ATTN_BLOCK
    s_idx = jnp.arange(2 * ATTN_BLOCK)[None, :]
    diff = t_idx + ATTN_BLOCK - s_idx
    band = (diff >= 0) & (diff < ATTN_BLOCK)
    masks = jnp.stack([band, band & (s_idx >= ATTN_BLOCK)])
    bias = jnp.where(masks, 0.0, NEG_INF).astype(jnp.float32)
    sink_rows = jnp.repeat(sinks.reshape(N_KV_HEADS, Q_REP) * LOG2_E, ATTN_BLOCK, axis=1)
    return jnp.where(s_idx == 0, sink_rows[:, None, :, None], bias[None])


def _mixer(x2, sinks, w_in, b_in, lnv_g, lnv_b, w_sp, b_sp_full, w_out, b_out, ln1_g, ln1_b,
           w_r, w_r_hi, b_r, seq_len):
    t = x2.shape[0]
    tm = MIXER_ROWS
    grp = jnp.arange(GMLP_WIDTH) // GMLP_GROUP_DIM
    grp_avg = jnp.where(grp[:, None] == grp[None, :], 1.0 / GMLP_GROUP_DIM, 0.0).astype(jnp.bfloat16)
    tri = (jnp.arange(tm)[:, None] < jnp.arange(tm)[None, :]).astype(jnp.bfloat16)

    def full(shape):
        return pl.BlockSpec(shape, lambda i: (0,) * len(shape))

    return pl.pallas_call(
        functools.partial(_mixer_kernel, steps_per_seq=seq_len // tm),
        grid=(t // tm,),
        in_specs=[
            pl.BlockSpec((tm, D_MODEL), lambda i: (i, 0)),
            full((D_MODEL, IN_WIDTH)), full((1, IN_WIDTH)),
            full((N_KV_HEADS, 2, Q_REP * ATTN_BLOCK, 2 * ATTN_BLOCK)),
            full((1, GMLP_WIDTH)), full((1, GMLP_WIDTH)), full((GMLP_WIDTH, GMLP_WIDTH)),
            full((N_GMLP_GROUPS, ATTN_BLOCK, ATTN_BLOCK)), full((ATTN_BLOCK, GMLP_WIDTH)),
            full((D_MODEL, D_MODEL)), full((1, D_MODEL)), full((1, D_MODEL)), full((1, D_MODEL)),
            full((D_MODEL, ROUTER_LANES)), full((D_MODEL, ROUTER_LANES)), full((N_EXPERTS, 1)), full((tm, tm)),
        ],
        out_specs=[
            pl.BlockSpec((tm, D_MODEL), lambda i: (i, 0)),
            pl.BlockSpec((tm, D_MODEL // 2), lambda i: (i, 0)),
            pl.BlockSpec((META_ROWS, tm), lambda i: (0, i)),
            pl.BlockSpec((N_EXPERTS, 1), lambda i: (0, 0)),
        ],
        out_shape=[
            jax.ShapeDtypeStruct((t, D_MODEL), jnp.float32),
            jax.ShapeDtypeStruct((t, D_MODEL // 2), jnp.int32),
            jax.ShapeDtypeStruct((META_ROWS, t), jnp.float32),
            jax.ShapeDtypeStruct((N_EXPERTS, 1), jnp.float32),
        ],
        scratch_shapes=[pltpu.VMEM((ATTN_BLOCK, 2 * KV_WIDTH), jnp.bfloat16)],
        compiler_params=pltpu.CompilerParams(
            dimension_semantics=("arbitrary",), vmem_limit_bytes=VMEM_LIMIT_BYTES),
        name="mixer",
    )(x2, w_in, b_in, _score_bias(sinks), lnv_g, lnv_b, grp_avg, w_sp, b_sp_full, w_out, b_out, ln1_g, ln1_b,
      w_r, w_r_hi, b_r, tri)


def _gather_rows(idx, src):
    n = idx.shape[0]
    width = src.shape[1]
    win = GATHER_WINDOW
    sc = plsc.get_sparse_core_info()
    n_workers = sc.num_cores * sc.num_subcores
    per_worker = n // n_workers
    n_pairs = per_worker // (2 * win)
    assert n_pairs * 2 * win * n_workers == n
    mesh = plsc.VectorSubcoreMesh(core_axis_name="core", subcore_axis_name="subcore")

    @functools.partial(
        pl.kernel, out_type=jax.ShapeDtypeStruct((n, width), src.dtype), mesh=mesh,
        scratch_types=[pltpu.VMEM((per_worker,), jnp.int32), pltpu.VMEM((2, win, width), src.dtype),
                       pltpu.SemaphoreType.DMA((2,)), pltpu.SemaphoreType.DMA((2,))],
        name="gather_rows")
    def gather(src_hbm, idx_hbm, out_hbm, idx_v, rows_v, fetch_sem, store_sem):
        worker = lax.axis_index("subcore") * sc.num_cores + lax.axis_index("core")
        base = worker * per_worker
        pltpu.sync_copy(idx_hbm.at[pl.ds(base, per_worker)], idx_v)

        def fetch(chunk, buf):
            return pltpu.make_async_copy(src_hbm.at[idx_v.at[pl.ds(chunk * win, win)]], rows_v.at[buf],
                                         fetch_sem.at[buf])

        def store(chunk, buf):
            return pltpu.make_async_copy(rows_v.at[buf], out_hbm.at[pl.ds(base + chunk * win, win)],
                                         store_sem.at[buf])

        @pl.loop(0, n_pairs)
        def _(p):
            for buf in range(2):
                @pl.when(p > 0)
                def _():
                    store(2 * p - 2 + buf, buf).wait()
                fetch(2 * p + buf, buf).start()
            for buf in range(2):
                fetch(2 * p + buf, buf).wait()
                store(2 * p + buf, buf).start()

        for buf in range(2):
            store(2 * n_pairs - 2 + buf, buf).wait()

    return gather(src, idx)


def _dispatch_rows(dest_t, src, n_rows):
    t, width = src.shape
    win = GATHER_WINDOW
    sc = plsc.get_sparse_core_info()
    n_workers = sc.num_cores * sc.num_subcores
    per_worker = t // n_workers
    n_chunks = per_worker // win
    n_pairs = n_chunks // 2
    assert n_pairs * 2 * win * n_workers == t
    idx = dest_t.reshape(TOP_K, n_workers, n_chunks, win).transpose(1, 0, 2, 3)
    idx = idx.reshape(n_workers, TOP_K * n_chunks, win)
    mesh = plsc.VectorSubcoreMesh(core_axis_name="core", subcore_axis_name="subcore")

    @functools.partial(
        pl.kernel, out_type=jax.ShapeDtypeStruct((n_rows, width), src.dtype), mesh=mesh,
        scratch_types=[pltpu.VMEM((TOP_K * n_chunks, win), jnp.int32), pltpu.VMEM((2, win, width), src.dtype),
                       pltpu.SemaphoreType.DMA((2,)), pltpu.SemaphoreType.DMA((2,))],
        name="dispatch_rows")
    def dispatch(src_hbm, idx_hbm, out_hbm, idx_v, rows_v, fetch_sem, store_sem):
        worker = lax.axis_index("subcore") * sc.num_cores + lax.axis_index("core")
        base = worker * per_worker
        pltpu.sync_copy(idx_hbm.at[worker], idx_v)

        def fetch(chunk, buf):
            return pltpu.make_async_copy(src_hbm.at[pl.ds(base + chunk * win, win)], rows_v.at[buf],
                                         fetch_sem.at[buf])

        def store(chunk, k, buf):
            return pltpu.make_async_copy(rows_v.at[buf], out_hbm.at[idx_v.at[k * n_chunks + chunk]],
                                         store_sem.at[buf])

        @pl.loop(0, n_pairs)
        def _(p):
            for buf in range(2):
                @pl.when(p > 0)
                def _():
                    for k in range(TOP_K):
                        store(2 * p - 2 + buf, k, buf).wait()
                fetch(2 * p + buf, buf).start()
            for buf in range(2):
                fetch(2 * p + buf, buf).wait()
                for k in range(TOP_K):
                    store(2 * p + buf, k, buf).start()

        for buf in range(2):
            for k in range(TOP_K):
                store(2 * n_pairs - 2 + buf, k, buf).wait()

    return dispatch(src, idx)


def _expert_kernel(block_e_ref, n_valid_ref, first_ref, next_e_ref, slot_ref,
                   x_ref, wg_hbm, bg_ref, wu_hbm, bu_ref, wd_hbm, bd_ref,
                   y_ref, wg_f32, wu_f32, wd_f32, wg_bf, wu_bf, wd_bf, sem):
    i = pl.program_id(0)
    n_valid = n_valid_ref[i]
    slot = slot_ref[i]
    staged = ((wg_hbm, wg_f32, wg_bf), (wu_hbm, wu_f32, wu_bf), (wd_hbm, wd_f32, wd_bf))

    def weight_copy(m, expert, s):
        return pltpu.make_async_copy(staged[m][0].at[expert], staged[m][1].at[s], sem.at[s, m])

    @pl.when(i == 0)
    def _():
        for m in range(3):
            weight_copy(m, block_e_ref[0], 0).start()

    @pl.when(first_ref[i] == 1)
    def _():
        @pl.when(next_e_ref[i] >= 0)
        def _():
            for m in range(3):
                weight_copy(m, next_e_ref[i], 1 - slot).start()

        for m in range(3):
            weight_copy(m, block_e_ref[i], slot).wait()
            staged[m][2][...] = staged[m][1][slot].astype(jnp.bfloat16)

    @pl.when(n_valid > 0)
    def _():
        row = lax.broadcasted_iota(jnp.int32, (x_ref.shape[0], 1), 0)
        x_hi, x_lo = _unpack_bf16_pairs(jnp.where(row < n_valid, x_ref[...], 0))
        xb = jnp.concatenate([x_hi, x_lo], axis=1).astype(jnp.bfloat16)
        gt = jnp.minimum(jnp.dot(xb, wg_bf[...], preferred_element_type=jnp.float32) + bg_ref[0], SWIGLU_LIMIT)
        up = jnp.clip(jnp.dot(xb, wu_bf[...], preferred_element_type=jnp.float32) + bu_ref[0],
                      -SWIGLU_LIMIT, SWIGLU_LIMIT)
        hid = gt * jax.nn.sigmoid(SWIGLU_ALPHA * gt) * (up + 1.0)
        y = jnp.dot(hid.astype(jnp.bfloat16), wd_bf[...], preferred_element_type=jnp.float32) + bd_ref[0]
        y_ref[...] = _pack_bf16_pairs(y.astype(jnp.bfloat16))

    @pl.when(n_valid == 0)
    def _():
        y_ref[...] = jnp.zeros_like(y_ref)


def _experts(block_e, n_valid, x_rows, w_gate, b_gate, w_up, b_up, w_down, b_down):
    n_rows = x_rows.shape[0]
    bm = EXPERT_ROWS
    n_blocks = n_rows // bm
    d_ff = w_gate.shape[2]

    blocks = jnp.arange(n_blocks, dtype=jnp.int32)
    is_first = jnp.concatenate([jnp.ones((1,), jnp.int32), (block_e[1:] != block_e[:-1]).astype(jnp.int32)])
    group = jnp.sum(jnp.where(blocks[None, :] <= blocks[:, None], is_first[None, :], 0), axis=1) - 1
    next_e = jnp.min(jnp.where(block_e[None, :] > block_e[:, None], block_e[None, :], N_EXPERTS), axis=1)
    next_e = jnp.where(next_e == N_EXPERTS, -1, next_e)

    def by_expert(shape):
        return pl.BlockSpec(shape, lambda i, be, *_: (be[i],) + (0,) * (len(shape) - 1))

    hbm = pl.BlockSpec(memory_space=pl.ANY)
    return pl.pallas_call(
        _expert_kernel,
        grid_spec=pltpu.PrefetchScalarGridSpec(
            num_scalar_prefetch=5,
            grid=(n_blocks,),
            in_specs=[
                pl.BlockSpec((bm, D_MODEL // 2), lambda i, *_: (i, 0)),
                hbm, by_expert((1, 1, d_ff)),
                hbm, by_expert((1, 1, d_ff)),
                hbm, by_expert((1, 1, D_MODEL)),
            ],
            out_specs=pl.BlockSpec((bm, D_MODEL // 2), lambda i, *_: (i, 0)),
            scratch_shapes=[
                pltpu.VMEM((2, D_MODEL, d_ff), jnp.float32),
                pltpu.VMEM((2, D_MODEL, d_ff), jnp.float32),
                pltpu.VMEM((2, d_ff, D_MODEL), jnp.float32),
                pltpu.VMEM((D_MODEL, d_ff), jnp.bfloat16),
                pltpu.VMEM((D_MODEL, d_ff), jnp.bfloat16),
                pltpu.VMEM((d_ff, D_MODEL), jnp.bfloat16),
                pltpu.SemaphoreType.DMA((2, 3)),
            ],
        ),
        out_shape=jax.ShapeDtypeStruct((n_rows, D_MODEL // 2), jnp.int32),
        compiler_params=pltpu.CompilerParams(
            dimension_semantics=("arbitrary",), vmem_limit_bytes=VMEM_LIMIT_BYTES),
        name="experts",
    )(block_e, n_valid, is_first, next_e, group % 2, x_rows, w_gate, b_gate.reshape(N_EXPERTS, 1, d_ff), w_up,
      b_up.reshape(N_EXPERTS, 1, d_ff), w_down, b_down.reshape(N_EXPERTS, 1, D_MODEL))


def _combine_kernel(h_ref, y_ref, meta_ref, g_ref, b_ref, o_ref):
    tm = h_ref.shape[0]
    meta_cols = jnp.concatenate([meta_ref[...], jnp.zeros((128 - META_ROWS, tm), jnp.float32)], axis=0).T
    ffn_hi, ffn_lo = 0.0, 0.0
    for k in range(TOP_K):
        gate = meta_cols[:, 2 * TOP_K + k:2 * TOP_K + k + 1]
        y_hi, y_lo = _unpack_bf16_pairs(y_ref[k])
        ffn_hi = ffn_hi + gate * y_hi
        ffn_lo = ffn_lo + gate * y_lo
    ffn = jnp.concatenate([ffn_hi, ffn_lo], axis=1)
    o_ref[...] = _layer_norm(DEEPNORM_ALPHA * h_ref[...] + ffn, g_ref[...], b_ref[...])


def _combine(h, y_tok, meta, ln2_g, ln2_b):
    t = h.shape[0]
    tm = COMBINE_ROWS
    return pl.pallas_call(
        _combine_kernel,
        grid=(t // tm,),
        in_specs=[
            pl.BlockSpec((tm, D_MODEL), lambda i: (i, 0)),
            pl.BlockSpec((TOP_K, tm, D_MODEL // 2), lambda i: (0, i, 0)),
            pl.BlockSpec((META_ROWS, tm), lambda i: (0, i)),
            pl.BlockSpec((1, D_MODEL), lambda i: (0, 0)),
            pl.BlockSpec((1, D_MODEL), lambda i: (0, 0)),
        ],
        out_specs=pl.BlockSpec((tm, D_MODEL), lambda i: (i, 0)),
        out_shape=jax.ShapeDtypeStruct((t, D_MODEL), jnp.float32),
        compiler_params=pltpu.CompilerParams(
            dimension_semantics=("arbitrary",), vmem_limit_bytes=VMEM_LIMIT_BYTES),
        name="combine",
    )(h, y_tok, meta, ln2_g, ln2_b)


def _layer(x2, seq_len, w_in, b_in, sinks, ln_v_g, ln_v_b, w_spatial, b_spatial, w_out, b_out,
           ln1_g, ln1_b, w_router, b_router, w_gate, b_gate, w_up, b_up, w_down, b_down, ln2_g, ln2_b):
    t = x2.shape[0]
    tk = t * TOP_K
    bm = EXPERT_ROWS
    bf16 = jnp.bfloat16

    w_r_hi = w_router.astype(bf16)
    w_r_lo = (w_router - w_r_hi.astype(jnp.float32)).astype(bf16)
    lane_pad = jnp.zeros((D_MODEL, ROUTER_LANES - 2 * N_EXPERTS), bf16)
    w_r = jnp.concatenate([w_r_hi, w_r_lo, lane_pad], axis=1)
    w_r_hi_only = jnp.concatenate([w_r_hi, jnp.zeros_like(w_r_lo), lane_pad], axis=1)
    b_sp_full = jnp.repeat(b_spatial.T, GMLP_GROUP_DIM, axis=1)

    h, h_packed, meta, counts = _mixer(
        x2, sinks, w_in.astype(bf16), b_in[None], ln_v_g[None], ln_v_b[None], w_spatial, b_sp_full,
        w_out.astype(bf16), b_out[None], ln1_g[None], ln1_b[None], w_r, w_r_hi_only, b_router[:, None], seq_len)

    counts = counts[:, 0].astype(jnp.int32)
    experts = jnp.arange(N_EXPERTS, dtype=jnp.int32)
    padded = (counts + bm - 1) // bm * bm
    padded_end = jnp.sum(jnp.where(experts[None, :] <= experts[:, None], padded[None, :], 0), axis=1)
    padded_start = padded_end - padded
    n_blocks = tk // bm + N_EXPERTS
    n_rows = n_blocks * bm
    top_idx_t = meta[:TOP_K].astype(jnp.int32)
    rank_t = meta[TOP_K:2 * TOP_K].astype(jnp.int32)
    dest_t = rank_t + jnp.sum(
        jnp.where(top_idx_t[None] == experts[:, None, None], padded_start[:, None, None], 0), axis=0)
    block_start = jnp.arange(n_blocks, dtype=jnp.int32) * bm
    block_e = jnp.minimum(
        jnp.sum((padded_end[None, :] <= block_start[:, None]).astype(jnp.int32), axis=1), N_EXPERTS - 1)
    valid_end = jnp.sum(jnp.where(block_e[:, None] == experts[None, :], (padded_start + counts)[None, :], 0), axis=1)
    n_valid = jnp.clip(valid_end - block_start, 0, bm)

    x_rows = _dispatch_rows(dest_t, h_packed, n_rows)
    y_rows = _experts(block_e, n_valid, x_rows, w_gate, b_gate, w_up, b_up, w_down, b_down)
    y_tok = _gather_rows(dest_t.reshape(-1), y_rows).reshape(TOP_K, t, D_MODEL // 2)
    return _combine(h, y_tok, meta, ln2_g[None], ln2_b[None])


def kernel(x, w_in, b_in, sinks, ln_v_g, ln_v_b, w_spatial, b_spatial, w_out, b_out, ln1_g, ln1_b,
           w_router, b_router, w_gate, b_gate, w_up, b_up, w_down, b_down, ln2_g, ln2_b):
    batch, seq_len, d = x.shape
    x2 = x.reshape(batch * seq_len, d)
    for l in range(DEPTH):
        x2 = _layer(x2, seq_len, w_in[l], b_in[l], sinks[l], ln_v_g[l], ln_v_b[l], w_spatial[l],
                    b_spatial[l], w_out[l], b_out[l], ln1_g[l], ln1_b[l], w_router[l], b_router[l],
                    w_gate[l], b_gate[l], w_up[l], b_up[l], w_down[l], b_down[l], ln2_g[l], ln2_b[l])
    return x2.reshape(batch, seq_len, d)
```

```python
import functools

import jax
import jax.numpy as jnp
from jax import lax
from jax.experimental import pallas as pl
from jax.experimental.pallas import tpu as pltpu
from jax.experimental.pallas import tpu_sc as plsc

D_MODEL = 1024
HEAD_DIM = 64
N_Q_HEADS = 8
N_KV_HEADS = 2
Q_REP = N_Q_HEADS // N_KV_HEADS
ATTN_WIDTH = N_Q_HEADS * HEAD_DIM
KV_WIDTH = N_KV_HEADS * HEAD_DIM
ATTN_BLOCK = 128
N_GMLP_GROUPS = 8
GMLP_WIDTH = D_MODEL - ATTN_WIDTH
GMLP_GROUP_DIM = GMLP_WIDTH // N_GMLP_GROUPS
IN_WIDTH = ATTN_WIDTH + 2 * KV_WIDTH + 2 * GMLP_WIDTH
N_EXPERTS = 32
TOP_K = 4
SWIGLU_LIMIT = 7.0
SWIGLU_ALPHA = 1.702
LN_EPS = 1e-5
DEPTH = 1
DEEPNORM_ALPHA = (2.0 * DEPTH) ** 0.25
NEG_INF = -1e30
LOG2_E = 1.4426950408889634

MIXER_ROWS = 256
EXPERT_ROWS = 512
EXPERT_PARTIAL_ROWS = (128, 256)
GATHER_WINDOW = 64
META_ROWS = 16
ROUTER_LANES = 128
COMBINE_ROWS = 1024
VMEM_LIMIT_BYTES = 56 * 1024 * 1024

_O_K = ATTN_WIDTH
_O_V = _O_K + KV_WIDTH
_O_U = _O_V + KV_WIDTH
_O_G = _O_U + GMLP_WIDTH


def _pack_bf16_pairs(v):
    n = v.shape[1] // 2
    hi = lax.bitcast_convert_type(v[:, :n].astype(jnp.float32), jnp.int32)
    lo = lax.bitcast_convert_type(v[:, n:].astype(jnp.float32), jnp.int32)
    return hi | lax.shift_right_logical(lo, 16)


def _unpack_bf16_pairs(p):
    hi = lax.bitcast_convert_type(p & jnp.int32(-65536), jnp.float32)
    lo = lax.bitcast_convert_type(lax.shift_left(p, 16), jnp.float32)
    return hi, lo


def _layer_norm(v, g, b):
    mu = jnp.mean(v, axis=-1, keepdims=True)
    vc = v - mu
    var = jnp.mean(vc * vc, axis=-1, keepdims=True)
    return vc * lax.rsqrt(var + LN_EPS) * g + b


def _attention_block(q, kb, vb, bias_ref, bias_sel):
    outs = []
    for g in range(N_KV_HEADS):
        kg = kb[:, g * HEAD_DIM:(g + 1) * HEAD_DIM]
        vg = vb[:, g * HEAD_DIM:(g + 1) * HEAD_DIM]
        qg = jnp.concatenate(
            [q[:, (g * Q_REP + r) * HEAD_DIM:(g * Q_REP + r + 1) * HEAD_DIM] for r in range(Q_REP)],
            axis=0).astype(jnp.bfloat16)
        s = (lax.dot_general(qg, kg, (((1,), (1,)), ((), ())), preferred_element_type=jnp.float32)
             + bias_ref[g, bias_sel])
        p = jnp.exp2(s - jnp.max(s, axis=-1, keepdims=True))
        denom = jnp.sum(p, axis=-1, keepdims=True)
        o = jnp.dot(p.astype(jnp.bfloat16), vg, preferred_element_type=jnp.float32) / denom
        outs.extend(o[r * ATTN_BLOCK:(r + 1) * ATTN_BLOCK] for r in range(Q_REP))
    return jnp.concatenate(outs, axis=-1)


def _mixer_kernel(x_ref, w_in_ref, b_in_ref, bias_ref, lnv_g_ref, lnv_b_ref, grp_avg_ref,
                  w_sp_ref, b_sp_ref, w_out_ref, b_out_ref, ln1_g_ref, ln1_b_ref,
                  w_r_ref, w_r_hi_ref, b_r_ref, tri_ref,
                  h_ref, hp_ref, meta_ref, count_ref,
                  kv_prev_ref, *, steps_per_seq):
    i = pl.program_id(0)
    first_step = (i % steps_per_seq) == 0
    tm = x_ref.shape[0]
    n_sub = tm // ATTN_BLOCK

    @pl.when(i == 0)
    def _():
        count_ref[...] = jnp.zeros_like(count_ref)

    @pl.when(first_step)
    def _():
        kv_prev_ref[...] = jnp.zeros_like(kv_prev_ref)

    x = x_ref[...]
    proj = jnp.dot(x.astype(jnp.bfloat16), w_in_ref[...], preferred_element_type=jnp.float32) + b_in_ref[...]

    q_all = proj[:, :_O_K] * (LOG2_E * HEAD_DIM ** -0.5)
    k_all = proj[:, _O_K:_O_V].astype(jnp.bfloat16)
    v_all = proj[:, _O_V:_O_U].astype(jnp.bfloat16)
    k_prev = kv_prev_ref[:, :KV_WIDTH]
    v_prev = kv_prev_ref[:, KV_WIDTH:]
    is_row0 = lax.broadcasted_iota(jnp.int32, (ATTN_BLOCK, KV_WIDTH), 0) == 0
    attn_blocks = []
    for sb in range(n_sub):
        rows = slice(sb * ATTN_BLOCK, (sb + 1) * ATTN_BLOCK)
        k_cur, v_cur = k_all[rows], v_all[rows]
        kb = jnp.concatenate([jnp.where(is_row0, 0, k_prev), k_cur], axis=0)
        vb = jnp.concatenate([jnp.where(is_row0, 0, v_prev), v_cur], axis=0)
        bias_sel = jnp.where(first_step, 1, 0) if sb == 0 else 0
        attn_blocks.append(_attention_block(q_all[rows], kb, vb, bias_ref, bias_sel))
        k_prev, v_prev = k_cur, v_cur
    kv_prev_ref[:, :KV_WIDTH] = k_prev
    kv_prev_ref[:, KV_WIDTH:] = v_prev
    attn = jnp.concatenate(attn_blocks, axis=0)

    u = jax.nn.gelu(proj[:, _O_U:_O_G])
    gg = jax.nn.gelu(proj[:, _O_G:])
    avg = grp_avg_ref[...]
    mu = jnp.dot(gg.astype(jnp.bfloat16), avg, preferred_element_type=jnp.float32)
    gc = gg - mu
    var = jnp.dot((gc * gc).astype(jnp.bfloat16), avg, preferred_element_type=jnp.float32)
    gn = (gc * lax.rsqrt(var + LN_EPS) * lnv_g_ref[...] + lnv_b_ref[...]).astype(jnp.bfloat16)
    causal = (lax.broadcasted_iota(jnp.int32, (ATTN_BLOCK, ATTN_BLOCK), 0)
              >= lax.broadcasted_iota(jnp.int32, (ATTN_BLOCK, ATTN_BLOCK), 1))
    w_sp = [jnp.where(causal, w_sp_ref[g], 0.0).astype(jnp.bfloat16) for g in range(N_GMLP_GROUPS)]
    mixed_chunks = []
    for c in range(n_sub):
        rows = slice(c * ATTN_BLOCK, (c + 1) * ATTN_BLOCK)
        pieces = [
            jnp.dot(w_sp[g], gn[rows, g * GMLP_GROUP_DIM:(g + 1) * GMLP_GROUP_DIM],
                    preferred_element_type=jnp.float32)
            for g in range(N_GMLP_GROUPS)]
        mixed_chunks.append(jnp.concatenate(pieces, axis=-1) + b_sp_ref[...])
    sgu = u * jnp.concatenate(mixed_chunks, axis=0)

    mix = (jnp.dot(attn.astype(jnp.bfloat16), w_out_ref[:ATTN_WIDTH, :], preferred_element_type=jnp.float32)
           + jnp.dot(sgu.astype(jnp.bfloat16), w_out_ref[ATTN_WIDTH:, :], preferred_element_type=jnp.float32)
           + b_out_ref[...])
    h = _layer_norm(DEEPNORM_ALPHA * x + mix, ln1_g_ref[...], ln1_b_ref[...])
    h_ref[...] = h

    h_hi = h.astype(jnp.bfloat16)
    hp_ref[...] = _pack_bf16_pairs(h_hi)
    h_lo = (h - h_hi.astype(jnp.float32)).astype(jnp.bfloat16)
    part = (jnp.dot(h_hi, w_r_ref[...], preferred_element_type=jnp.float32)
            + jnp.dot(h_lo, w_r_hi_ref[...], preferred_element_type=jnp.float32)).T
    logits = part[:N_EXPERTS] + part[N_EXPERTS:2 * N_EXPERTS] + b_r_ref[...]

    n_grp = N_EXPERTS // 8
    grp = [logits[8 * g:8 * (g + 1)] for g in range(n_grp)]
    sub = lax.broadcasted_iota(jnp.int32, (8, tm), 0)
    beaten = [jnp.zeros((8, tm), jnp.float32) for _ in range(n_grp)]
    for e2 in range(N_EXPERTS):
        g2, r2 = divmod(e2, 8)
        row = logits[e2:e2 + 1]
        for g in range(n_grp):
            if g > g2:
                wins = jnp.where(row >= grp[g], 1.0, 0.0)
            elif g < g2:
                wins = jnp.where(row > grp[g], 1.0, 0.0)
            else:
                wins = jnp.where(sub > r2, jnp.where(row >= grp[g], 1.0, 0.0), jnp.where(row > grp[g], 1.0, 0.0))
            beaten[g] = beaten[g] + wins
    place = jnp.concatenate(beaten, axis=0)
    expert_id = lax.broadcasted_iota(jnp.int32, (N_EXPERTS, tm), 0).astype(jnp.float32)
    onehot = jnp.where(place < TOP_K, 1.0, 0.0)
    before = jnp.dot(onehot.astype(jnp.bfloat16), tri_ref[...], preferred_element_type=jnp.float32) + count_ref[...]

    def pick(k, table):
        return jnp.sum(jnp.where(place == k, table, 0.0), axis=0, keepdims=True)

    vals = [pick(k, logits) for k in range(TOP_K)]
    exps = [jnp.exp(v - vals[0]) for v in vals]
    denom = exps[0] + exps[1] + exps[2] + exps[3]
    meta_ref[...] = jnp.zeros_like(meta_ref)
    for k in range(TOP_K):
        meta_ref[k:k + 1, :] = pick(k, expert_id)
        meta_ref[TOP_K + k:TOP_K + k + 1, :] = pick(k, before)
        meta_ref[2 * TOP_K + k:2 * TOP_K + k + 1, :] = exps[k] / denom
    count_ref[...] += jnp.sum(onehot, axis=1, keepdims=True)


def _score_bias(sinks):
    t_idx = jnp.arange(Q_REP * ATTN_BLOCK)[:, None] % ATTN_BLOCK
    s_idx = jnp.arange(2 * ATTN_BLOCK)[None, :]
    diff = t_idx + ATTN_BLOCK - s_idx
    band = (diff >= 0) & (diff < ATTN_BLOCK)
    masks = jnp.stack([band, band & (s_idx >= ATTN_BLOCK)])
    bias = jnp.where(masks, 0.0, NEG_INF).astype(jnp.float32)
    sink_rows = jnp.repeat(sinks.reshape(N_KV_HEADS, Q_REP) * LOG2_E, ATTN_BLOCK, axis=1)
    return jnp.where(s_idx == 0, sink_rows[:, None, :, None], bias[None])


def _mixer(x2, sinks, w_in, b_in, lnv_g, lnv_b, w_sp, b_sp_full, w_out, b_out, ln1_g, ln1_b,
           w_r, w_r_hi, b_r, seq_len):
    t = x2.shape[0]
    tm = MIXER_ROWS
    grp = jnp.arange(GMLP_WIDTH) // GMLP_GROUP_DIM
    grp_avg = jnp.where(grp[:, None] == grp[None, :], 1.0 / GMLP_GROUP_DIM, 0.0).astype(jnp.bfloat16)
    tri = (jnp.arange(tm)[:, None] < jnp.arange(tm)[None, :]).astype(jnp.bfloat16)

    def full(shape):
        return pl.BlockSpec(shape, lambda i: (0,) * len(shape))

    return pl.pallas_call(
        functools.partial(_mixer_kernel, steps_per_seq=seq_len // tm),
        grid=(t // tm,),
        in_specs=[
            pl.BlockSpec((tm, D_MODEL), lambda i: (i, 0)),
            full((D_MODEL, IN_WIDTH)), full((1, IN_WIDTH)),
            full((N_KV_HEADS, 2, Q_REP * ATTN_BLOCK, 2 * ATTN_BLOCK)),
            full((1, GMLP_WIDTH)), full((1, GMLP_WIDTH)), full((GMLP_WIDTH, GMLP_WIDTH)),
            full((N_GMLP_GROUPS, ATTN_BLOCK, ATTN_BLOCK)), full((ATTN_BLOCK, GMLP_WIDTH)),
            full((D_MODEL, D_MODEL)), full((1, D_MODEL)), full((1, D_MODEL)), full((1, D_MODEL)),
            full((D_MODEL, ROUTER_LANES)), full((D_MODEL, ROUTER_LANES)), full((N_EXPERTS, 1)), full((tm, tm)),
        ],
        out_specs=[
            pl.BlockSpec((tm, D_MODEL), lambda i: (i, 0)),
            pl.BlockSpec((tm, D_MODEL // 2), lambda i: (i, 0)),
            pl.BlockSpec((META_ROWS, tm), lambda i: (0, i)),
            pl.BlockSpec((N_EXPERTS, 1), lambda i: (0, 0)),
        ],
        out_shape=[
            jax.ShapeDtypeStruct((t, D_MODEL), jnp.float32),
            jax.ShapeDtypeStruct((t, D_MODEL // 2), jnp.int32),
            jax.ShapeDtypeStruct((META_ROWS, t), jnp.float32),
            jax.ShapeDtypeStruct((N_EXPERTS, 1), jnp.float32),
        ],
        scratch_shapes=[pltpu.VMEM((ATTN_BLOCK, 2 * KV_WIDTH), jnp.bfloat16)],
        compiler_params=pltpu.CompilerParams(
            dimension_semantics=("arbitrary",), vmem_limit_bytes=VMEM_LIMIT_BYTES),
        name="mixer",
    )(x2, w_in, b_in, _score_bias(sinks), lnv_g, lnv_b, grp_avg, w_sp, b_sp_full, w_out, b_out, ln1_g, ln1_b,
      w_r, w_r_hi, b_r, tri)


def _gather_rows(idx, src):
    n = idx.shape[0]
    width = src.shape[1]
    win = GATHER_WINDOW
    sc = plsc.get_sparse_core_info()
    n_workers = sc.num_cores * sc.num_subcores
    per_worker = n // n_workers
    n_pairs = per_worker // (2 * win)
    assert n_pairs * 2 * win * n_workers == n
    mesh = plsc.VectorSubcoreMesh(core_axis_name="core", subcore_axis_name="subcore")

    @functools.partial(
        pl.kernel, out_type=jax.ShapeDtypeStruct((n, width), src.dtype), mesh=mesh,
        scratch_types=[pltpu.VMEM((per_worker,), jnp.int32), pltpu.VMEM((2, win, width), src.dtype),
                       pltpu.SemaphoreType.DMA((2,)), pltpu.SemaphoreType.DMA((2,))],
        name="gather_rows")
    def gather(src_hbm, idx_hbm, out_hbm, idx_v, rows_v, fetch_sem, store_sem):
        worker = lax.axis_index("subcore") * sc.num_cores + lax.axis_index("core")
        base = worker * per_worker
        pltpu.sync_copy(idx_hbm.at[pl.ds(base, per_worker)], idx_v)

        def fetch(chunk, buf):
            return pltpu.make_async_copy(src_hbm.at[idx_v.at[pl.ds(chunk * win, win)]], rows_v.at[buf],
                                         fetch_sem.at[buf])

        def store(chunk, buf):
            return pltpu.make_async_copy(rows_v.at[buf], out_hbm.at[pl.ds(base + chunk * win, win)],
                                         store_sem.at[buf])

        @pl.loop(0, n_pairs)
        def _(p):
            for buf in range(2):
                @pl.when(p > 0)
                def _():
                    store(2 * p - 2 + buf, buf).wait()
                fetch(2 * p + buf, buf).start()
            for buf in range(2):
                fetch(2 * p + buf, buf).wait()
                store(2 * p + buf, buf).start()

        for buf in range(2):
            store(2 * n_pairs - 2 + buf, buf).wait()

    return gather(src, idx)


def _dispatch_rows(dest_t, src, n_rows):
    t, width = src.shape
    win = GATHER_WINDOW
    sc = plsc.get_sparse_core_info()
    n_workers = sc.num_cores * sc.num_subcores
    per_worker = t // n_workers
    n_chunks = per_worker // win
    n_pairs = n_chunks // 2
    assert n_pairs * 2 * win * n_workers == t
    idx = dest_t.reshape(TOP_K, n_workers, n_chunks, win).transpose(1, 0, 2, 3)
    idx = idx.reshape(n_workers, TOP_K * n_chunks, win)
    mesh = plsc.VectorSubcoreMesh(core_axis_name="core", subcore_axis_name="subcore")

    @functools.partial(
        pl.kernel, out_type=jax.ShapeDtypeStruct((n_rows, width), src.dtype), mesh=mesh,
        scratch_types=[pltpu.VMEM((TOP_K * n_chunks, win), jnp.int32), pltpu.VMEM((2, win, width), src.dtype),
                       pltpu.SemaphoreType.DMA((2,)), pltpu.SemaphoreType.DMA((2,))],
        name="dispatch_rows")
    def dispatch(src_hbm, idx_hbm, out_hbm, idx_v, rows_v, fetch_sem, store_sem):
        worker = lax.axis_index("subcore") * sc.num_cores + lax.axis_index("core")
        base = worker * per_worker
        pltpu.sync_copy(idx_hbm.at[worker], idx_v)

        def fetch(chunk, buf):
            return pltpu.make_async_copy(src_hbm.at[pl.ds(base + chunk * win, win)], rows_v.at[buf],
                                         fetch_sem.at[buf])

        def store(chunk, k, buf):
            return pltpu.make_async_copy(rows_v.at[buf], out_hbm.at[idx_v.at[k * n_chunks + chunk]],
                                         store_sem.at[buf])

        @pl.loop(0, n_pairs)
        def _(p):
            for buf in range(2):
                @pl.when(p > 0)
                def _():
                    for k in range(TOP_K):
                        store(2 * p - 2 + buf, k, buf).wait()
                fetch(2 * p + buf, buf).start()
            for buf in range(2):
                fetch(2 * p + buf, buf).wait()
                for k in range(TOP_K):
                    store(2 * p + buf, k, buf).start()

        for buf in range(2):
            for k in range(TOP_K):
                store(2 * n_pairs - 2 + buf, k, buf).wait()

    return dispatch(src, idx)


def _expert_kernel(block_e_ref, n_valid_ref, first_ref, next_e_ref, slot_ref,
                   x_ref, wg_hbm, bg_ref, wu_hbm, bu_ref, wd_hbm, bd_ref,
                   y_ref, wg_f32, wu_f32, wd_f32, wg_bf, wu_bf, wd_bf, sem):
    i = pl.program_id(0)
    n_valid = n_valid_ref[i]
    slot = slot_ref[i]
    staged = ((wg_hbm, wg_f32, wg_bf), (wu_hbm, wu_f32, wu_bf), (wd_hbm, wd_f32, wd_bf))

    def weight_copy(m, expert, s):
        return pltpu.make_async_copy(staged[m][0].at[expert], staged[m][1].at[s], sem.at[s, m])

    @pl.when(i == 0)
    def _():
        for m in range(3):
            weight_copy(m, block_e_ref[0], 0).start()

    @pl.when(first_ref[i] == 1)
    def _():
        @pl.when(next_e_ref[i] >= 0)
        def _():
            for m in range(3):
                weight_copy(m, next_e_ref[i], 1 - slot).start()

        for m in range(3):
            weight_copy(m, block_e_ref[i], slot).wait()
            staged[m][2][...] = staged[m][1][slot].astype(jnp.bfloat16)

    def expert_mlp(rows):
        row = lax.broadcasted_iota(jnp.int32, (rows, 1), 0)
        x_hi, x_lo = _unpack_bf16_pairs(jnp.where(row < n_valid, x_ref[:rows, :], 0))
        xb = jnp.concatenate([x_hi, x_lo], axis=1).astype(jnp.bfloat16)
        gt = jnp.minimum(jnp.dot(xb, wg_bf[...], preferred_element_type=jnp.float32) + bg_ref[0], SWIGLU_LIMIT)
        up = jnp.clip(jnp.dot(xb, wu_bf[...], preferred_element_type=jnp.float32) + bu_ref[0],
                      -SWIGLU_LIMIT, SWIGLU_LIMIT)
        hid = gt * jax.nn.sigmoid(SWIGLU_ALPHA * gt) * (up + 1.0)
        y = jnp.dot(hid.astype(jnp.bfloat16), wd_bf[...], preferred_element_type=jnp.float32) + bd_ref[0]
        y_ref[:rows, :] = _pack_bf16_pairs(y.astype(jnp.bfloat16))
        if rows < y_ref.shape[0]:
            y_ref[rows:, :] = jnp.zeros((y_ref.shape[0] - rows, y_ref.shape[1]), y_ref.dtype)

    row_options = (0,) + EXPERT_PARTIAL_ROWS + (x_ref.shape[0],)
    for lo, hi in zip(row_options[:-1], row_options[1:]):
        pl.when(jnp.logical_and(n_valid > lo, n_valid <= hi))(functools.partial(expert_mlp, hi))

    @pl.when(n_valid == 0)
    def _():
        y_ref[...] = jnp.zeros_like(y_ref)


def _experts(block_e, n_valid, x_rows, w_gate, b_gate, w_up, b_up, w_down, b_down):
    n_rows = x_rows.shape[0]
    bm = EXPERT_ROWS
    n_blocks = n_rows // bm
    d_ff = w_gate.shape[2]

    blocks = jnp.arange(n_blocks, dtype=jnp.int32)
    is_first = jnp.concatenate([jnp.ones((1,), jnp.int32), (block_e[1:] != block_e[:-1]).astype(jnp.int32)])
    group = jnp.sum(jnp.where(blocks[None, :] <= blocks[:, None], is_first[None, :], 0), axis=1) - 1
    next_e = jnp.min(jnp.where(block_e[None, :] > block_e[:, None], block_e[None, :], N_EXPERTS), axis=1)
    next_e = jnp.where(next_e == N_EXPERTS, -1, next_e)

    def by_expert(shape):
        return pl.BlockSpec(shape, lambda i, be, *_: (be[i],) + (0,) * (len(shape) - 1))

    hbm = pl.BlockSpec(memory_space=pl.ANY)
    return pl.pallas_call(
        _expert_kernel,
        grid_spec=pltpu.PrefetchScalarGridSpec(
            num_scalar_prefetch=5,
            grid=(n_blocks,),
            in_specs=[
                pl.BlockSpec((bm, D_MODEL // 2), lambda i, *_: (i, 0)),
                hbm, by_expert((1, 1, d_ff)),
                hbm, by_expert((1, 1, d_ff)),
                hbm, by_expert((1, 1, D_MODEL)),
            ],
            out_specs=pl.BlockSpec((bm, D_MODEL // 2), lambda i, *_: (i, 0)),
            scratch_shapes=[
                pltpu.VMEM((2, D_MODEL, d_ff), jnp.float32),
                pltpu.VMEM((2, D_MODEL, d_ff), jnp.float32),
                pltpu.VMEM((2, d_ff, D_MODEL), jnp.float32),
                pltpu.VMEM((D_MODEL, d_ff), jnp.bfloat16),
                pltpu.VMEM((D_MODEL, d_ff), jnp.bfloat16),
                pltpu.VMEM((d_ff, D_MODEL), jnp.bfloat16),
                pltpu.SemaphoreType.DMA((2, 3)),
            ],
        ),
        out_shape=jax.ShapeDtypeStruct((n_rows, D_MODEL // 2), jnp.int32),
        compiler_params=pltpu.CompilerParams(
            dimension_semantics=("arbitrary",), vmem_limit_bytes=VMEM_LIMIT_BYTES),
        name="experts",
    )(block_e, n_valid, is_first, next_e, group % 2, x_rows, w_gate, b_gate.reshape(N_EXPERTS, 1, d_ff), w_up,
      b_up.reshape(N_EXPERTS, 1, d_ff), w_down, b_down.reshape(N_EXPERTS, 1, D_MODEL))


def _combine_kernel(h_ref, y_ref, meta_ref, g_ref, b_ref, o_ref):
    tm = h_ref.shape[0]
    meta_cols = jnp.concatenate([meta_ref[...], jnp.zeros((128 - META_ROWS, tm), jnp.float32)], axis=0).T
    ffn_hi, ffn_lo = 0.0, 0.0
    for k in range(TOP_K):
        gate = meta_cols[:, 2 * TOP_K + k:2 * TOP_K + k + 1]
        y_hi, y_lo = _unpack_bf16_pairs(y_ref[k])
        ffn_hi = ffn_hi + gate * y_hi
        ffn_lo = ffn_lo + gate * y_lo
    ffn = jnp.concatenate([ffn_hi, ffn_lo], axis=1)
    o_ref[...] = _layer_norm(DEEPNORM_ALPHA * h_ref[...] + ffn, g_ref[...], b_ref[...])


def _combine(h, y_tok, meta, ln2_g, ln2_b):
    t = h.shape[0]
    tm = COMBINE_ROWS
    return pl.pallas_call(
        _combine_kernel,
        grid=(t // tm,),
        in_specs=[
            pl.BlockSpec((tm, D_MODEL), lambda i: (i, 0)),
            pl.BlockSpec((TOP_K, tm, D_MODEL // 2), lambda i: (0, i, 0)),
            pl.BlockSpec((META_ROWS, tm), lambda i: (0, i)),
            pl.BlockSpec((1, D_MODEL), lambda i: (0, 0)),
            pl.BlockSpec((1, D_MODEL), lambda i: (0, 0)),
        ],
        out_specs=pl.BlockSpec((tm, D_MODEL), lambda i: (i, 0)),
        out_shape=jax.ShapeDtypeStruct((t, D_MODEL), jnp.float32),
        compiler_params=pltpu.CompilerParams(
            dimension_semantics=("arbitrary",), vmem_limit_bytes=VMEM_LIMIT_BYTES),
        name="combine",
    )(h, y_tok, meta, ln2_g, ln2_b)


def _layer(x2, seq_len, w_in, b_in, sinks, ln_v_g, ln_v_b, w_spatial, b_spatial, w_out, b_out,
           ln1_g, ln1_b, w_router, b_router, w_gate, b_gate, w_up, b_up, w_down, b_down, ln2_g, ln2_b):
    t = x2.shape[0]
    tk = t * TOP_K
    bm = EXPERT_ROWS
    bf16 = jnp.bfloat16

    w_r_hi = w_router.astype(bf16)
    w_r_lo = (w_router - w_r_hi.astype(jnp.float32)).astype(bf16)
    lane_pad = jnp.zeros((D_MODEL, ROUTER_LANES - 2 * N_EXPERTS), bf16)
    w_r = jnp.concatenate([w_r_hi, w_r_lo, lane_pad], axis=1)
    w_r_hi_only = jnp.concatenate([w_r_hi, jnp.zeros_like(w_r_lo), lane_pad], axis=1)
    b_sp_full = jnp.repeat(b_spatial.T, GMLP_GROUP_DIM, axis=1)

    h, h_packed, meta, counts = _mixer(
        x2, sinks, w_in.astype(bf16), b_in[None], ln_v_g[None], ln_v_b[None], w_spatial, b_sp_full,
        w_out.astype(bf16), b_out[None], ln1_g[None], ln1_b[None], w_r, w_r_hi_only, b_router[:, None], seq_len)

    counts = counts[:, 0].astype(jnp.int32)
    experts = jnp.arange(N_EXPERTS, dtype=jnp.int32)
    padded = (counts + bm - 1) // bm * bm
    padded_end = jnp.sum(jnp.where(experts[None, :] <= experts[:, None], padded[None, :], 0), axis=1)
    padded_start = padded_end - padded
    n_blocks = tk // bm + N_EXPERTS
    n_rows = n_blocks * bm
    top_idx_t = meta[:TOP_K].astype(jnp.int32)
    rank_t = meta[TOP_K:2 * TOP_K].astype(jnp.int32)
    dest_t = rank_t + jnp.sum(
        jnp.where(top_idx_t[None] == experts[:, None, None], padded_start[:, None, None], 0), axis=0)
    block_start = jnp.arange(n_blocks, dtype=jnp.int32) * bm
    block_e = jnp.minimum(
        jnp.sum((padded_end[None, :] <= block_start[:, None]).astype(jnp.int32), axis=1), N_EXPERTS - 1)
    valid_end = jnp.sum(jnp.where(block_e[:, None] == experts[None, :], (padded_start + counts)[None, :], 0), axis=1)
    n_valid = jnp.clip(valid_end - block_start, 0, bm)

    x_rows = _dispatch_rows(dest_t, h_packed, n_rows)
    y_rows = _experts(block_e, n_valid, x_rows, w_gate, b_gate, w_up, b_up, w_down, b_down)
    y_tok = _gather_rows(dest_t.reshape(-1), y_rows).reshape(TOP_K, t, D_MODEL // 2)
    return _combine(h, y_tok, meta, ln2_g[None], ln2_b[None])


def kernel(x, w_in, b_in, sinks, ln_v_g, ln_v_b, w_spatial, b_spatial, w_out, b_out, ln1_g, ln1_b,
           w_router, b_router, w_gate, b_gate, w_up, b_up, w_down, b_down, ln2_g, ln2_b):
    batch, seq_len, d = x.shape
    x2 = x.reshape(batch * seq_len, d)
    for l in range(DEPTH):
        x2 = _layer(x2, seq_len, w_in[l], b_in[l], sinks[l], ln_v_g[l], ln_v_b[l], w_spatial[l],
                    b_spatial[l], w_out[l], b_out[l], ln1_g[l], ln1_b[l], w_router[l], b_router[l],
                    w_gate[l], b_gate[l], w_up[l], b_up[l], w_down[l], b_down[l], ln2_g[l], ln2_b[l])
    return x2.reshape(batch, seq_len, d)
```

```python
import functools

import jax
import jax.numpy as jnp
from jax import lax
from jax.experimental import pallas as pl
from jax.experimental.pallas import tpu as pltpu
from jax.experimental.pallas import tpu_sc as plsc

D_MODEL = 1024
HEAD_DIM = 64
N_Q_HEADS = 8
N_KV_HEADS = 2
Q_REP = N_Q_HEADS // N_KV_HEADS
ATTN_WIDTH = N_Q_HEADS * HEAD_DIM
KV_WIDTH = N_KV_HEADS * HEAD_DIM
ATTN_BLOCK = 128
N_GMLP_GROUPS = 8
GMLP_WIDTH = D_MODEL - ATTN_WIDTH
GMLP_GROUP_DIM = GMLP_WIDTH // N_GMLP_GROUPS
IN_WIDTH = ATTN_WIDTH + 2 * KV_WIDTH + 2 * GMLP_WIDTH
N_EXPERTS = 32
TOP_K = 4
SWIGLU_LIMIT = 7.0
SWIGLU_ALPHA = 1.702
LN_EPS = 1e-5
DEPTH = 1
DEEPNORM_ALPHA = (2.0 * DEPTH) ** 0.25
NEG_INF = -1e30
LOG2_E = 1.4426950408889634

MIXER_ROWS = 256
EXPERT_ROWS = 512
EXPERT_PARTIAL_ROWS = (128, 256, 384)
GATHER_WINDOW = 64
META_ROWS = 16
ROUTER_LANES = 128
COMBINE_ROWS = 1024
VMEM_LIMIT_BYTES = 56 * 1024 * 1024

_O_K = ATTN_WIDTH
_O_V = _O_K + KV_WIDTH
_O_U = _O_V + KV_WIDTH
_O_G = _O_U + GMLP_WIDTH


def _pack_bf16_pairs(v):
    n = v.shape[1] // 2
    hi = lax.bitcast_convert_type(v[:, :n].astype(jnp.float32), jnp.int32)
    lo = lax.bitcast_convert_type(v[:, n:].astype(jnp.float32), jnp.int32)
    return hi | lax.shift_right_logical(lo, 16)


def _unpack_bf16_pairs(p):
    hi = lax.bitcast_convert_type(p & jnp.int32(-65536), jnp.float32)
    lo = lax.bitcast_convert_type(lax.shift_left(p, 16), jnp.float32)
    return hi, lo


def _layer_norm(v, g, b):
    mu = jnp.mean(v, axis=-1, keepdims=True)
    vc = v - mu
    var = jnp.mean(vc * vc, axis=-1, keepdims=True)
    return vc * lax.rsqrt(var + LN_EPS) * g + b


def _attention_block(q, kb, vb, bias_ref, bias_sel):
    outs = []
    for g in range(N_KV_HEADS):
        kg = kb[:, g * HEAD_DIM:(g + 1) * HEAD_DIM]
        vg = vb[:, g * HEAD_DIM:(g + 1) * HEAD_DIM]
        qg = jnp.concatenate(
            [q[:, (g * Q_REP + r) * HEAD_DIM:(g * Q_REP + r + 1) * HEAD_DIM] for r in range(Q_REP)],
            axis=0).astype(jnp.bfloat16)
        s = (lax.dot_general(qg, kg, (((1,), (1,)), ((), ())), preferred_element_type=jnp.float32)
             + bias_ref[g, bias_sel])
        p = jnp.exp2(s - jnp.max(s, axis=-1, keepdims=True))
        denom = jnp.sum(p, axis=-1, keepdims=True)
        o = jnp.dot(p.astype(jnp.bfloat16), vg, preferred_element_type=jnp.float32) / denom
        outs.extend(o[r * ATTN_BLOCK:(r + 1) * ATTN_BLOCK] for r in range(Q_REP))
    return jnp.concatenate(outs, axis=-1)


def _mixer_kernel(x_ref, w_in_ref, b_in_ref, bias_ref, lnv_g_ref, lnv_b_ref, grp_avg_ref,
                  w_sp_ref, b_sp_ref, w_out_ref, b_out_ref, ln1_g_ref, ln1_b_ref,
                  w_r_ref, w_r_hi_ref, b_r_ref, tri_ref,
                  h_ref, hp_ref, meta_ref, count_ref,
                  kv_prev_ref, *, steps_per_seq):
    i = pl.program_id(0)
    first_step = (i % steps_per_seq) == 0
    tm = x_ref.shape[0]
    n_sub = tm // ATTN_BLOCK

    @pl.when(i == 0)
    def _():
        count_ref[...] = jnp.zeros_like(count_ref)

    @pl.when(first_step)
    def _():
        kv_prev_ref[...] = jnp.zeros_like(kv_prev_ref)

    x = x_ref[...]
    proj = jnp.dot(x.astype(jnp.bfloat16), w_in_ref[...], preferred_element_type=jnp.float32) + b_in_ref[...]

    q_all = proj[:, :_O_K] * (LOG2_E * HEAD_DIM ** -0.5)
    k_all = proj[:, _O_K:_O_V].astype(jnp.bfloat16)
    v_all = proj[:, _O_V:_O_U].astype(jnp.bfloat16)
    k_prev = kv_prev_ref[:, :KV_WIDTH]
    v_prev = kv_prev_ref[:, KV_WIDTH:]
    is_row0 = lax.broadcasted_iota(jnp.int32, (ATTN_BLOCK, KV_WIDTH), 0) == 0
    attn_blocks = []
    for sb in range(n_sub):
        rows = slice(sb * ATTN_BLOCK, (sb + 1) * ATTN_BLOCK)
        k_cur, v_cur = k_all[rows], v_all[rows]
        kb = jnp.concatenate([jnp.where(is_row0, 0, k_prev), k_cur], axis=0)
        vb = jnp.concatenate([jnp.where(is_row0, 0, v_prev), v_cur], axis=0)
        bias_sel = jnp.where(first_step, 1, 0) if sb == 0 else 0
        attn_blocks.append(_attention_block(q_all[rows], kb, vb, bias_ref, bias_sel))
        k_prev, v_prev = k_cur, v_cur
    kv_prev_ref[:, :KV_WIDTH] = k_prev
    kv_prev_ref[:, KV_WIDTH:] = v_prev
    attn = jnp.concatenate(attn_blocks, axis=0)

    u = jax.nn.gelu(proj[:, _O_U:_O_G])
    gg = jax.nn.gelu(proj[:, _O_G:])
    avg = grp_avg_ref[...]
    mu = jnp.dot(gg.astype(jnp.bfloat16), avg, preferred_element_type=jnp.float32)
    gc = gg - mu
    var = jnp.dot((gc * gc).astype(jnp.bfloat16), avg, preferred_element_type=jnp.float32)
    gn = (gc * lax.rsqrt(var + LN_EPS) * lnv_g_ref[...] + lnv_b_ref[...]).astype(jnp.bfloat16)
    causal = (lax.broadcasted_iota(jnp.int32, (ATTN_BLOCK, ATTN_BLOCK), 0)
              >= lax.broadcasted_iota(jnp.int32, (ATTN_BLOCK, ATTN_BLOCK), 1))
    w_sp = [jnp.where(causal, w_sp_ref[g], 0.0).astype(jnp.bfloat16) for g in range(N_GMLP_GROUPS)]
    mixed_chunks = []
    for c in range(n_sub):
        rows = slice(c * ATTN_BLOCK, (c + 1) * ATTN_BLOCK)
        pieces = [
            jnp.dot(w_sp[g], gn[rows, g * GMLP_GROUP_DIM:(g + 1) * GMLP_GROUP_DIM],
                    preferred_element_type=jnp.float32)
            for g in range(N_GMLP_GROUPS)]
        mixed_chunks.append(jnp.concatenate(pieces, axis=-1) + b_sp_ref[...])
    sgu = u * jnp.concatenate(mixed_chunks, axis=0)

    mix = (jnp.dot(attn.astype(jnp.bfloat16), w_out_ref[:ATTN_WIDTH, :], preferred_element_type=jnp.float32)
           + jnp.dot(sgu.astype(jnp.bfloat16), w_out_ref[ATTN_WIDTH:, :], preferred_element_type=jnp.float32)
           + b_out_ref[...])
    h = _layer_norm(DEEPNORM_ALPHA * x + mix, ln1_g_ref[...], ln1_b_ref[...])
    h_ref[...] = h

    h_hi = h.astype(jnp.bfloat16)
    hp_ref[...] = _pack_bf16_pairs(h_hi)
    h_lo = (h - h_hi.astype(jnp.float32)).astype(jnp.bfloat16)
    part = (jnp.dot(h_hi, w_r_ref[...], preferred_element_type=jnp.float32)
            + jnp.dot(h_lo, w_r_hi_ref[...], preferred_element_type=jnp.float32)).T
    logits = part[:N_EXPERTS] + part[N_EXPERTS:2 * N_EXPERTS] + b_r_ref[...]

    n_grp = N_EXPERTS // 8
    grp = [logits[8 * g:8 * (g + 1)] for g in range(n_grp)]
    sub = lax.broadcasted_iota(jnp.int32, (8, tm), 0)
    beaten = [jnp.zeros((8, tm), jnp.float32) for _ in range(n_grp)]
    for e2 in range(N_EXPERTS):
        g2, r2 = divmod(e2, 8)
        row = logits[e2:e2 + 1]
        for g in range(n_grp):
            if g > g2:
                wins = jnp.where(row >= grp[g], 1.0, 0.0)
            elif g < g2:
                wins = jnp.where(row > grp[g], 1.0, 0.0)
            else:
                wins = jnp.where(sub > r2, jnp.where(row >= grp[g], 1.0, 0.0), jnp.where(row > grp[g], 1.0, 0.0))
            beaten[g] = beaten[g] + wins
    place = jnp.concatenate(beaten, axis=0)
    expert_id = lax.broadcasted_iota(jnp.int32, (N_EXPERTS, tm), 0).astype(jnp.float32)
    onehot = jnp.where(place < TOP_K, 1.0, 0.0)
    before = jnp.dot(onehot.astype(jnp.bfloat16), tri_ref[...], preferred_element_type=jnp.float32) + count_ref[...]

    def pick(k, table):
        return jnp.sum(jnp.where(place == k, table, 0.0), axis=0, keepdims=True)

    vals = [pick(k, logits) for k in range(TOP_K)]
    exps = [jnp.exp(v - vals[0]) for v in vals]
    denom = exps[0] + exps[1] + exps[2] + exps[3]
    meta_ref[...] = jnp.zeros_like(meta_ref)
    for k in range(TOP_K):
        meta_ref[k:k + 1, :] = pick(k, expert_id)
        meta_ref[TOP_K + k:TOP_K + k + 1, :] = pick(k, before)
        meta_ref[2 * TOP_K + k:2 * TOP_K + k + 1, :] = exps[k] / denom
    count_ref[...] += jnp.sum(onehot, axis=1, keepdims=True)


def _score_bias(sinks):
    t_idx = jnp.arange(Q_REP * ATTN_BLOCK)[:, None] % ATTN_BLOCK
    s_idx = jnp.arange(2 * ATTN_BLOCK)[None, :]
    diff = t_idx + ATTN_BLOCK - s_idx
    band = (diff >= 0) & (diff < ATTN_BLOCK)
    masks = jnp.stack([band, band & (s_idx >= ATTN_BLOCK)])
    bias = jnp.where(masks, 0.0, NEG_INF).astype(jnp.float32)
    sink_rows = jnp.repeat(sinks.reshape(N_KV_HEADS, Q_REP) * LOG2_E, ATTN_BLOCK, axis=1)
    return jnp.where(s_idx == 0, sink_rows[:, None, :, None], bias[None])


def _mixer(x2, sinks, w_in, b_in, lnv_g, lnv_b, w_sp, b_sp_full, w_out, b_out, ln1_g, ln1_b,
           w_r, w_r_hi, b_r, seq_len):
    t = x2.shape[0]
    tm = MIXER_ROWS
    grp = jnp.arange(GMLP_WIDTH) // GMLP_GROUP_DIM
    grp_avg = jnp.where(grp[:, None] == grp[None, :], 1.0 / GMLP_GROUP_DIM, 0.0).astype(jnp.bfloat16)
    tri = (jnp.arange(tm)[:, None] < jnp.arange(tm)[None, :]).astype(jnp.bfloat16)

    def full(shape):
        return pl.BlockSpec(shape, lambda i: (0,) * len(shape))

    return pl.pallas_call(
        functools.partial(_mixer_kernel, steps_per_seq=seq_len // tm),
        grid=(t // tm,),
        in_specs=[
            pl.BlockSpec((tm, D_MODEL), lambda i: (i, 0)),
            full((D_MODEL, IN_WIDTH)), full((1, IN_WIDTH)),
            full((N_KV_HEADS, 2, Q_REP * ATTN_BLOCK, 2 * ATTN_BLOCK)),
            full((1, GMLP_WIDTH)), full((1, GMLP_WIDTH)), full((GMLP_WIDTH, GMLP_WIDTH)),
            full((N_GMLP_GROUPS, ATTN_BLOCK, ATTN_BLOCK)), full((ATTN_BLOCK, GMLP_WIDTH)),
            full((D_MODEL, D_MODEL)), full((1, D_MODEL)), full((1, D_MODEL)), full((1, D_MODEL)),
            full((D_MODEL, ROUTER_LANES)), full((D_MODEL, ROUTER_LANES)), full((N_EXPERTS, 1)), full((tm, tm)),
        ],
        out_specs=[
            pl.BlockSpec((tm, D_MODEL), lambda i: (i, 0)),
            pl.BlockSpec((tm, D_MODEL // 2), lambda i: (i, 0)),
            pl.BlockSpec((META_ROWS, tm), lambda i: (0, i)),
            pl.BlockSpec((N_EXPERTS, 1), lambda i: (0, 0)),
        ],
        out_shape=[
            jax.ShapeDtypeStruct((t, D_MODEL), jnp.float32),
            jax.ShapeDtypeStruct((t, D_MODEL // 2), jnp.int32),
            jax.ShapeDtypeStruct((META_ROWS, t), jnp.float32),
            jax.ShapeDtypeStruct((N_EXPERTS, 1), jnp.float32),
        ],
        scratch_shapes=[pltpu.VMEM((ATTN_BLOCK, 2 * KV_WIDTH), jnp.bfloat16)],
        compiler_params=pltpu.CompilerParams(
            dimension_semantics=("arbitrary",), vmem_limit_bytes=VMEM_LIMIT_BYTES),
        name="mixer",
    )(x2, w_in, b_in, _score_bias(sinks), lnv_g, lnv_b, grp_avg, w_sp, b_sp_full, w_out, b_out, ln1_g, ln1_b,
      w_r, w_r_hi, b_r, tri)


def _gather_rows(idx, src):
    n = idx.shape[0]
    width = src.shape[1]
    win = GATHER_WINDOW
    sc = plsc.get_sparse_core_info()
    n_workers = sc.num_cores * sc.num_subcores
    per_worker = n // n_workers
    n_pairs = per_worker // (2 * win)
    assert n_pairs * 2 * win * n_workers == n
    mesh = plsc.VectorSubcoreMesh(core_axis_name="core", subcore_axis_name="subcore")

    @functools.partial(
        pl.kernel, out_type=jax.ShapeDtypeStruct((n, width), src.dtype), mesh=mesh,
        scratch_types=[pltpu.VMEM((per_worker,), jnp.int32), pltpu.VMEM((2, win, width), src.dtype),
                       pltpu.SemaphoreType.DMA((2,)), pltpu.SemaphoreType.DMA((2,))],
        name="gather_rows")
    def gather(src_hbm, idx_hbm, out_hbm, idx_v, rows_v, fetch_sem, store_sem):
        worker = lax.axis_index("subcore") * sc.num_cores + lax.axis_index("core")
        base = worker * per_worker
        pltpu.sync_copy(idx_hbm.at[pl.ds(base, per_worker)], idx_v)

        def fetch(chunk, buf):
            return pltpu.make_async_copy(src_hbm.at[idx_v.at[pl.ds(chunk * win, win)]], rows_v.at[buf],
                                         fetch_sem.at[buf])

        def store(chunk, buf):
            return pltpu.make_async_copy(rows_v.at[buf], out_hbm.at[pl.ds(base + chunk * win, win)],
                                         store_sem.at[buf])

        @pl.loop(0, n_pairs)
        def _(p):
            for buf in range(2):
                @pl.when(p > 0)
                def _():
                    store(2 * p - 2 + buf, buf).wait()
                fetch(2 * p + buf, buf).start()
            for buf in range(2):
                fetch(2 * p + buf, buf).wait()
                store(2 * p + buf, buf).start()

        for buf in range(2):
            store(2 * n_pairs - 2 + buf, buf).wait()

    return gather(src, idx)


def _dispatch_rows(dest_t, src, n_rows):
    t, width = src.shape
    win = GATHER_WINDOW
    sc = plsc.get_sparse_core_info()
    n_workers = sc.num_cores * sc.num_subcores
    per_worker = t // n_workers
    n_chunks = per_worker // win
    n_pairs = n_chunks // 2
    assert n_pairs * 2 * win * n_workers == t
    idx = dest_t.reshape(TOP_K, n_workers, n_chunks, win).transpose(1, 0, 2, 3)
    idx = idx.reshape(n_workers, TOP_K * n_chunks, win)
    mesh = plsc.VectorSubcoreMesh(core_axis_name="core", subcore_axis_name="subcore")

    @functools.partial(
        pl.kernel, out_type=jax.ShapeDtypeStruct((n_rows, width), src.dtype), mesh=mesh,
        scratch_types=[pltpu.VMEM((TOP_K * n_chunks, win), jnp.int32), pltpu.VMEM((2, win, width), src.dtype),
                       pltpu.SemaphoreType.DMA((2,)), pltpu.SemaphoreType.DMA((2,))],
        name="dispatch_rows")
    def dispatch(src_hbm, idx_hbm, out_hbm, idx_v, rows_v, fetch_sem, store_sem):
        worker = lax.axis_index("subcore") * sc.num_cores + lax.axis_index("core")
        base = worker * per_worker
        pltpu.sync_copy(idx_hbm.at[worker], idx_v)

        def fetch(chunk, buf):
            return pltpu.make_async_copy(src_hbm.at[pl.ds(base + chunk * win, win)], rows_v.at[buf],
                                         fetch_sem.at[buf])

        def store(chunk, k, buf):
            return pltpu.make_async_copy(rows_v.at[buf], out_hbm.at[idx_v.at[k * n_chunks + chunk]],
                                         store_sem.at[buf])

        @pl.loop(0, n_pairs)
        def _(p):
            for buf in range(2):
                @pl.when(p > 0)
                def _():
                    for k in range(TOP_K):
                        store(2 * p - 2 + buf, k, buf).wait()
                fetch(2 * p + buf, buf).start()
            for buf in range(2):
                fetch(2 * p + buf, buf).wait()
                for k in range(TOP_K):
                    store(2 * p + buf, k, buf).start()

        for buf in range(2):
            for k in range(TOP_K):
                store(2 * n_pairs - 2 + buf, k, buf).wait()

    return dispatch(src, idx)


def _expert_kernel(block_e_ref, n_valid_ref, first_ref, next_e_ref, slot_ref,
                   x_ref, wg_hbm, bg_ref, wu_hbm, bu_ref, wd_hbm, bd_ref,
                   y_ref, wg_f32, wu_f32, wd_f32, wg_bf, wu_bf, wd_bf, sem):
    i = pl.program_id(0)
    n_valid = n_valid_ref[i]
    slot = slot_ref[i]
    staged = ((wg_hbm, wg_f32, wg_bf), (wu_hbm, wu_f32, wu_bf), (wd_hbm, wd_f32, wd_bf))

    def weight_copy(m, expert, s):
        return pltpu.make_async_copy(staged[m][0].at[expert], staged[m][1].at[s], sem.at[s, m])

    @pl.when(i == 0)
    def _():
        for m in range(3):
            weight_copy(m, block_e_ref[0], 0).start()

    @pl.when(first_ref[i] == 1)
    def _():
        @pl.when(next_e_ref[i] >= 0)
        def _():
            for m in range(3):
                weight_copy(m, next_e_ref[i], 1 - slot).start()

        for m in range(3):
            weight_copy(m, block_e_ref[i], slot).wait()
            staged[m][2][...] = staged[m][1][slot].astype(jnp.bfloat16)

    def expert_mlp(rows):
        row = lax.broadcasted_iota(jnp.int32, (rows, 1), 0)
        x_hi, x_lo = _unpack_bf16_pairs(jnp.where(row < n_valid, x_ref[:rows, :], 0))
        xb = jnp.concatenate([x_hi, x_lo], axis=1).astype(jnp.bfloat16)
        gt = jnp.minimum(jnp.dot(xb, wg_bf[...], preferred_element_type=jnp.float32) + bg_ref[0], SWIGLU_LIMIT)
        up = jnp.clip(jnp.dot(xb, wu_bf[...], preferred_element_type=jnp.float32) + bu_ref[0],
                      -SWIGLU_LIMIT, SWIGLU_LIMIT)
        hid = gt * jax.nn.sigmoid(SWIGLU_ALPHA * gt) * (up + 1.0)
        y = jnp.dot(hid.astype(jnp.bfloat16), wd_bf[...], preferred_element_type=jnp.float32) + bd_ref[0]
        y_ref[:rows, :] = _pack_bf16_pairs(y.astype(jnp.bfloat16))
        if rows < y_ref.shape[0]:
            y_ref[rows:, :] = jnp.zeros((y_ref.shape[0] - rows, y_ref.shape[1]), y_ref.dtype)

    row_options = (0,) + EXPERT_PARTIAL_ROWS + (x_ref.shape[0],)
    for lo, hi in zip(row_options[:-1], row_options[1:]):
        pl.when(jnp.logical_and(n_valid > lo, n_valid <= hi))(functools.partial(expert_mlp, hi))

    @pl.when(n_valid == 0)
    def _():
        y_ref[...] = jnp.zeros_like(y_ref)


def _experts(block_e, n_valid, x_rows, w_gate, b_gate, w_up, b_up, w_down, b_down):
    n_rows = x_rows.shape[0]
    bm = EXPERT_ROWS
    n_blocks = n_rows // bm
    d_ff = w_gate.shape[2]

    blocks = jnp.arange(n_blocks, dtype=jnp.int32)
    is_first = jnp.concatenate([jnp.ones((1,), jnp.int32), (block_e[1:] != block_e[:-1]).astype(jnp.int32)])
    group = jnp.sum(jnp.where(blocks[None, :] <= blocks[:, None], is_first[None, :], 0), axis=1) - 1
    next_e = jnp.min(jnp.where(block_e[None, :] > block_e[:, None], block_e[None, :], N_EXPERTS), axis=1)
    next_e = jnp.where(next_e == N_EXPERTS, -1, next_e)

    def by_expert(shape):
        return pl.BlockSpec(shape, lambda i, be, *_: (be[i],) + (0,) * (len(shape) - 1))

    hbm = pl.BlockSpec(memory_space=pl.ANY)
    return pl.pallas_call(
        _expert_kernel,
        grid_spec=pltpu.PrefetchScalarGridSpec(
            num_scalar_prefetch=5,
            grid=(n_blocks,),
            in_specs=[
                pl.BlockSpec((bm, D_MODEL // 2), lambda i, *_: (i, 0)),
                hbm, by_expert((1, 1, d_ff)),
                hbm, by_expert((1, 1, d_ff)),
                hbm, by_expert((1, 1, D_MODEL)),
            ],
            out_specs=pl.BlockSpec((bm, D_MODEL // 2), lambda i, *_: (i, 0)),
            scratch_shapes=[
                pltpu.VMEM((2, D_MODEL, d_ff), jnp.float32),
                pltpu.VMEM((2, D_MODEL, d_ff), jnp.float32),
                pltpu.VMEM((2, d_ff, D_MODEL), jnp.float32),
                pltpu.VMEM((D_MODEL, d_ff), jnp.bfloat16),
                pltpu.VMEM((D_MODEL, d_ff), jnp.bfloat16),
                pltpu.VMEM((d_ff, D_MODEL), jnp.bfloat16),
                pltpu.SemaphoreType.DMA((2, 3)),
            ],
        ),
        out_shape=jax.ShapeDtypeStruct((n_rows, D_MODEL // 2), jnp.int32),
        compiler_params=pltpu.CompilerParams(
            dimension_semantics=("arbitrary",), vmem_limit_bytes=VMEM_LIMIT_BYTES),
        name="experts",
    )(block_e, n_valid, is_first, next_e, group % 2, x_rows, w_gate, b_gate.reshape(N_EXPERTS, 1, d_ff), w_up,
      b_up.reshape(N_EXPERTS, 1, d_ff), w_down, b_down.reshape(N_EXPERTS, 1, D_MODEL))


def _combine_kernel(h_ref, y_ref, meta_ref, g_ref, b_ref, o_ref):
    tm = h_ref.shape[0]
    meta_cols = jnp.concatenate([meta_ref[...], jnp.zeros((128 - META_ROWS, tm), jnp.float32)], axis=0).T
    ffn_hi, ffn_lo = 0.0, 0.0
    for k in range(TOP_K):
        gate = meta_cols[:, 2 * TOP_K + k:2 * TOP_K + k + 1]
        y_hi, y_lo = _unpack_bf16_pairs(y_ref[k])
        ffn_hi = ffn_hi + gate * y_hi
        ffn_lo = ffn_lo + gate * y_lo
    ffn = jnp.concatenate([ffn_hi, ffn_lo], axis=1)
    o_ref[...] = _layer_norm(DEEPNORM_ALPHA * h_ref[...] + ffn, g_ref[...], b_ref[...])


def _combine(h, y_tok, meta, ln2_g, ln2_b):
    t = h.shape[0]
    tm = COMBINE_ROWS
    return pl.pallas_call(
        _combine_kernel,
        grid=(t // tm,),
        in_specs=[
            pl.BlockSpec((tm, D_MODEL), lambda i: (i, 0)),
            pl.BlockSpec((TOP_K, tm, D_MODEL // 2), lambda i: (0, i, 0)),
            pl.BlockSpec((META_ROWS, tm), lambda i: (0, i)),
            pl.BlockSpec((1, D_MODEL), lambda i: (0, 0)),
            pl.BlockSpec((1, D_MODEL), lambda i: (0, 0)),
        ],
        out_specs=pl.BlockSpec((tm, D_MODEL), lambda i: (i, 0)),
        out_shape=jax.ShapeDtypeStruct((t, D_MODEL), jnp.float32),
        compiler_params=pltpu.CompilerParams(
            dimension_semantics=("arbitrary",), vmem_limit_bytes=VMEM_LIMIT_BYTES),
        name="combine",
    )(h, y_tok, meta, ln2_g, ln2_b)


def _layer(x2, seq_len, w_in, b_in, sinks, ln_v_g, ln_v_b, w_spatial, b_spatial, w_out, b_out,
           ln1_g, ln1_b, w_router, b_router, w_gate, b_gate, w_up, b_up, w_down, b_down, ln2_g, ln2_b):
    t = x2.shape[0]
    tk = t * TOP_K
    bm = EXPERT_ROWS
    bf16 = jnp.bfloat16

    w_r_hi = w_router.astype(bf16)
    w_r_lo = (w_router - w_r_hi.astype(jnp.float32)).astype(bf16)
    lane_pad = jnp.zeros((D_MODEL, ROUTER_LANES - 2 * N_EXPERTS), bf16)
    w_r = jnp.concatenate([w_r_hi, w_r_lo, lane_pad], axis=1)
    w_r_hi_only = jnp.concatenate([w_r_hi, jnp.zeros_like(w_r_lo), lane_pad], axis=1)
    b_sp_full = jnp.repeat(b_spatial.T, GMLP_GROUP_DIM, axis=1)

    h, h_packed, meta, counts = _mixer(
        x2, sinks, w_in.astype(bf16), b_in[None], ln_v_g[None], ln_v_b[None], w_spatial, b_sp_full,
        w_out.astype(bf16), b_out[None], ln1_g[None], ln1_b[None], w_r, w_r_hi_only, b_router[:, None], seq_len)

    counts = counts[:, 0].astype(jnp.int32)
    experts = jnp.arange(N_EXPERTS, dtype=jnp.int32)
    padded = (counts + bm - 1) // bm * bm
    padded_end = jnp.sum(jnp.where(experts[None, :] <= experts[:, None], padded[None, :], 0), axis=1)
    padded_start = padded_end - padded
    n_blocks = tk // bm + N_EXPERTS
    n_rows = n_blocks * bm
    top_idx_t = meta[:TOP_K].astype(jnp.int32)
    rank_t = meta[TOP_K:2 * TOP_K].astype(jnp.int32)
    dest_t = rank_t + jnp.sum(
        jnp.where(top_idx_t[None] == experts[:, None, None], padded_start[:, None, None], 0), axis=0)
    block_start = jnp.arange(n_blocks, dtype=jnp.int32) * bm
    block_e = jnp.minimum(
        jnp.sum((padded_end[None, :] <= block_start[:, None]).astype(jnp.int32), axis=1), N_EXPERTS - 1)
    valid_end = jnp.sum(jnp.where(block_e[:, None] == experts[None, :], (padded_start + counts)[None, :], 0), axis=1)
    n_valid = jnp.clip(valid_end - block_start, 0, bm)

    x_rows = _dispatch_rows(dest_t, h_packed, n_rows)
    y_rows = _experts(block_e, n_valid, x_rows, w_gate, b_gate, w_up, b_up, w_down, b_down)
    y_tok = _gather_rows(dest_t.reshape(-1), y_rows).reshape(TOP_K, t, D_MODEL // 2)
    return _combine(h, y_tok, meta, ln2_g[None], ln2_b[None])


def kernel(x, w_in, b_in, sinks, ln_v_g, ln_v_b, w_spatial, b_spatial, w_out, b_out, ln1_g, ln1_b,
           w_router, b_router, w_gate, b_gate, w_up, b_up, w_down, b_down, ln2_g, ln2_b):
    batch, seq_len, d = x.shape
    x2 = x.reshape(batch * seq_len, d)
    for l in range(DEPTH):
        x2 = _layer(x2, seq_len, w_in[l], b_in[l], sinks[l], ln_v_g[l], ln_v_b[l], w_spatial[l],
                    b_spatial[l], w_out[l], b_out[l], ln1_g[l], ln1_b[l], w_router[l], b_router[l],
                    w_gate[l], b_gate[l], w_up[l], b_up[l], w_down[l], b_down[l], ln2_g[l], ln2_b[l])
    return x2.reshape(batch, seq_len, d)
```

```python
import functools

import jax
import jax.numpy as jnp
from jax import lax
from jax.experimental import pallas as pl
from jax.experimental.pallas import tpu as pltpu
from jax.experimental.pallas import tpu_sc as plsc

D_MODEL = 1024
HEAD_DIM = 64
N_Q_HEADS = 8
N_KV_HEADS = 2
Q_REP = N_Q_HEADS // N_KV_HEADS
ATTN_WIDTH = N_Q_HEADS * HEAD_DIM
KV_WIDTH = N_KV_HEADS * HEAD_DIM
ATTN_BLOCK = 128
N_GMLP_GROUPS = 8
GMLP_WIDTH = D_MODEL - ATTN_WIDTH
GMLP_GROUP_DIM = GMLP_WIDTH // N_GMLP_GROUPS
IN_WIDTH = ATTN_WIDTH + 2 * KV_WIDTH + 2 * GMLP_WIDTH
N_EXPERTS = 32
TOP_K = 4
SWIGLU_LIMIT = 7.0
SWIGLU_ALPHA = 1.702
LN_EPS = 1e-5
DEPTH = 1
DEEPNORM_ALPHA = (2.0 * DEPTH) ** 0.25
NEG_INF = -1e30
LOG2_E = 1.4426950408889634

MIXER_ROWS = 512
EXPERT_ROWS = 512
EXPERT_PARTIAL_ROWS = (128, 256, 384)
GATHER_WINDOW = 64
META_ROWS = 16
ROUTER_LANES = 128
COMBINE_ROWS = 1024
VMEM_LIMIT_BYTES = 56 * 1024 * 1024

_O_K = ATTN_WIDTH
_O_V = _O_K + KV_WIDTH
_O_U = _O_V + KV_WIDTH
_O_G = _O_U + GMLP_WIDTH


def _pack_bf16_pairs(v):
    n = v.shape[1] // 2
    hi = lax.bitcast_convert_type(v[:, :n].astype(jnp.float32), jnp.int32)
    lo = lax.bitcast_convert_type(v[:, n:].astype(jnp.float32), jnp.int32)
    return hi | lax.shift_right_logical(lo, 16)


def _unpack_bf16_pairs(p):
    hi = lax.bitcast_convert_type(p & jnp.int32(-65536), jnp.float32)
    lo = lax.bitcast_convert_type(lax.shift_left(p, 16), jnp.float32)
    return hi, lo


def _layer_norm(v, g, b):
    mu = jnp.mean(v, axis=-1, keepdims=True)
    vc = v - mu
    var = jnp.mean(vc * vc, axis=-1, keepdims=True)
    return vc * lax.rsqrt(var + LN_EPS) * g + b


def _attention_block(q, kb, vb, bias_ref, bias_sel):
    outs = []
    for g in range(N_KV_HEADS):
        kg = kb[:, g * HEAD_DIM:(g + 1) * HEAD_DIM]
        vg = vb[:, g * HEAD_DIM:(g + 1) * HEAD_DIM]
        qg = jnp.concatenate(
            [q[:, (g * Q_REP + r) * HEAD_DIM:(g * Q_REP + r + 1) * HEAD_DIM] for r in range(Q_REP)],
            axis=0).astype(jnp.bfloat16)
        s = (lax.dot_general(qg, kg, (((1,), (1,)), ((), ())), preferred_element_type=jnp.float32)
             + bias_ref[g, bias_sel])
        p = jnp.exp2(s - jnp.max(s, axis=-1, keepdims=True))
        denom = jnp.sum(p, axis=-1, keepdims=True)
        o = jnp.dot(p.astype(jnp.bfloat16), vg, preferred_element_type=jnp.float32) / denom
        outs.extend(o[r * ATTN_BLOCK:(r + 1) * ATTN_BLOCK] for r in range(Q_REP))
    return jnp.concatenate(outs, axis=-1)


def _mixer_kernel(x_ref, w_in_ref, b_in_ref, bias_ref, lnv_g_ref, lnv_b_ref, grp_avg_ref,
                  w_sp_ref, b_sp_ref, w_out_ref, b_out_ref, ln1_g_ref, ln1_b_ref,
                  w_r_ref, w_r_hi_ref, b_r_ref, tri_ref,
                  h_ref, hp_ref, meta_ref, count_ref,
                  kv_prev_ref, *, steps_per_seq):
    i = pl.program_id(0)
    first_step = (i % steps_per_seq) == 0
    tm = x_ref.shape[0]
    n_sub = tm // ATTN_BLOCK

    @pl.when(i == 0)
    def _():
        count_ref[...] = jnp.zeros_like(count_ref)

    @pl.when(first_step)
    def _():
        kv_prev_ref[...] = jnp.zeros_like(kv_prev_ref)

    x = x_ref[...]
    proj = jnp.dot(x.astype(jnp.bfloat16), w_in_ref[...], preferred_element_type=jnp.float32) + b_in_ref[...]

    q_all = proj[:, :_O_K] * (LOG2_E * HEAD_DIM ** -0.5)
    k_all = proj[:, _O_K:_O_V].astype(jnp.bfloat16)
    v_all = proj[:, _O_V:_O_U].astype(jnp.bfloat16)
    k_prev = kv_prev_ref[:, :KV_WIDTH]
    v_prev = kv_prev_ref[:, KV_WIDTH:]
    is_row0 = lax.broadcasted_iota(jnp.int32, (ATTN_BLOCK, KV_WIDTH), 0) == 0
    attn_blocks = []
    for sb in range(n_sub):
        rows = slice(sb * ATTN_BLOCK, (sb + 1) * ATTN_BLOCK)
        k_cur, v_cur = k_all[rows], v_all[rows]
        kb = jnp.concatenate([jnp.where(is_row0, 0, k_prev), k_cur], axis=0)
        vb = jnp.concatenate([jnp.where(is_row0, 0, v_prev), v_cur], axis=0)
        bias_sel = jnp.where(first_step, 1, 0) if sb == 0 else 0
        attn_blocks.append(_attention_block(q_all[rows], kb, vb, bias_ref, bias_sel))
        k_prev, v_prev = k_cur, v_cur
    kv_prev_ref[:, :KV_WIDTH] = k_prev
    kv_prev_ref[:, KV_WIDTH:] = v_prev
    attn = jnp.concatenate(attn_blocks, axis=0)

    u = jax.nn.gelu(proj[:, _O_U:_O_G])
    gg = jax.nn.gelu(proj[:, _O_G:])
    avg = grp_avg_ref[...]
    mu = jnp.dot(gg.astype(jnp.bfloat16), avg, preferred_element_type=jnp.float32)
    gc = gg - mu
    var = jnp.dot((gc * gc).astype(jnp.bfloat16), avg, preferred_element_type=jnp.float32)
    gn = (gc * lax.rsqrt(var + LN_EPS) * lnv_g_ref[...] + lnv_b_ref[...]).astype(jnp.bfloat16)
    causal = (lax.broadcasted_iota(jnp.int32, (ATTN_BLOCK, ATTN_BLOCK), 0)
              >= lax.broadcasted_iota(jnp.int32, (ATTN_BLOCK, ATTN_BLOCK), 1))
    w_sp = [jnp.where(causal, w_sp_ref[g], 0.0).astype(jnp.bfloat16) for g in range(N_GMLP_GROUPS)]
    mixed_chunks = []
    for c in range(n_sub):
        rows = slice(c * ATTN_BLOCK, (c + 1) * ATTN_BLOCK)
        pieces = [
            jnp.dot(w_sp[g], gn[rows, g * GMLP_GROUP_DIM:(g + 1) * GMLP_GROUP_DIM],
                    preferred_element_type=jnp.float32)
            for g in range(N_GMLP_GROUPS)]
        mixed_chunks.append(jnp.concatenate(pieces, axis=-1) + b_sp_ref[...])
    sgu = u * jnp.concatenate(mixed_chunks, axis=0)

    mix = (jnp.dot(attn.astype(jnp.bfloat16), w_out_ref[:ATTN_WIDTH, :], preferred_element_type=jnp.float32)
           + jnp.dot(sgu.astype(jnp.bfloat16), w_out_ref[ATTN_WIDTH:, :], preferred_element_type=jnp.float32)
           + b_out_ref[...])
    h = _layer_norm(DEEPNORM_ALPHA * x + mix, ln1_g_ref[...], ln1_b_ref[...])
    h_ref[...] = h

    h_hi = h.astype(jnp.bfloat16)
    hp_ref[...] = _pack_bf16_pairs(h_hi)
    h_lo = (h - h_hi.astype(jnp.float32)).astype(jnp.bfloat16)
    part = (jnp.dot(h_hi, w_r_ref[...], preferred_element_type=jnp.float32)
            + jnp.dot(h_lo, w_r_hi_ref[...], preferred_element_type=jnp.float32)).T
    logits = part[:N_EXPERTS] + part[N_EXPERTS:2 * N_EXPERTS] + b_r_ref[...]

    n_grp = N_EXPERTS // 8
    grp = [logits[8 * g:8 * (g + 1)] for g in range(n_grp)]
    sub = lax.broadcasted_iota(jnp.int32, (8, tm), 0)
    beaten = [jnp.zeros((8, tm), jnp.float32) for _ in range(n_grp)]
    for e2 in range(N_EXPERTS):
        g2, r2 = divmod(e2, 8)
        row = logits[e2:e2 + 1]
        for g in range(n_grp):
            if g > g2:
                wins = jnp.where(row >= grp[g], 1.0, 0.0)
            elif g < g2:
                wins = jnp.where(row > grp[g], 1.0, 0.0)
            else:
                wins = jnp.where(sub > r2, jnp.where(row >= grp[g], 1.0, 0.0), jnp.where(row > grp[g], 1.0, 0.0))
            beaten[g] = beaten[g] + wins
    place = jnp.concatenate(beaten, axis=0)
    expert_id = lax.broadcasted_iota(jnp.int32, (N_EXPERTS, tm), 0).astype(jnp.float32)
    onehot = jnp.where(place < TOP_K, 1.0, 0.0)
    before = jnp.dot(onehot.astype(jnp.bfloat16), tri_ref[...], preferred_element_type=jnp.float32) + count_ref[...]

    def pick(k, table):
        return jnp.sum(jnp.where(place == k, table, 0.0), axis=0, keepdims=True)

    vals = [pick(k, logits) for k in range(TOP_K)]
    exps = [jnp.exp(v - vals[0]) for v in vals]
    denom = exps[0] + exps[1] + exps[2] + exps[3]
    meta_ref[...] = jnp.zeros_like(meta_ref)
    for k in range(TOP_K):
        meta_ref[k:k + 1, :] = pick(k, expert_id)
        meta_ref[TOP_K + k:TOP_K + k + 1, :] = pick(k, before)
        meta_ref[2 * TOP_K + k:2 * TOP_K + k + 1, :] = exps[k] / denom
    count_ref[...] += jnp.sum(onehot, axis=1, keepdims=True)


def _score_bias(sinks):
    t_idx = jnp.arange(Q_REP * ATTN_BLOCK)[:, None] % ATTN_BLOCK
    s_idx = jnp.arange(2 * ATTN_BLOCK)[None, :]
    diff = t_idx + ATTN_BLOCK - s_idx
    band = (diff >= 0) & (diff < ATTN_BLOCK)
    masks = jnp.stack([band, band & (s_idx >= ATTN_BLOCK)])
    bias = jnp.where(masks, 0.0, NEG_INF).astype(jnp.float32)
    sink_rows = jnp.repeat(sinks.reshape(N_KV_HEADS, Q_REP) * LOG2_E, ATTN_BLOCK, axis=1)
    return jnp.where(s_idx == 0, sink_rows[:, None, :, None], bias[None])


def _mixer(x2, sinks, w_in, b_in, lnv_g, lnv_b, w_sp, b_sp_full, w_out, b_out, ln1_g, ln1_b,
           w_r, w_r_hi, b_r, seq_len):
    t = x2.shape[0]
    tm = MIXER_ROWS
    grp = jnp.arange(GMLP_WIDTH) // GMLP_GROUP_DIM
    grp_avg = jnp.where(grp[:, None] == grp[None, :], 1.0 / GMLP_GROUP_DIM, 0.0).astype(jnp.bfloat16)
    tri = (jnp.arange(tm)[:, None] < jnp.arange(tm)[None, :]).astype(jnp.bfloat16)

    def full(shape):
        return pl.BlockSpec(shape, lambda i: (0,) * len(shape))

    return pl.pallas_call(
        functools.partial(_mixer_kernel, steps_per_seq=seq_len // tm),
        grid=(t // tm,),
        in_specs=[
            pl.BlockSpec((tm, D_MODEL), lambda i: (i, 0)),
            full((D_MODEL, IN_WIDTH)), full((1, IN_WIDTH)),
            full((N_KV_HEADS, 2, Q_REP * ATTN_BLOCK, 2 * ATTN_BLOCK)),
            full((1, GMLP_WIDTH)), full((1, GMLP_WIDTH)), full((GMLP_WIDTH, GMLP_WIDTH)),
            full((N_GMLP_GROUPS, ATTN_BLOCK, ATTN_BLOCK)), full((ATTN_BLOCK, GMLP_WIDTH)),
            full((D_MODEL, D_MODEL)), full((1, D_MODEL)), full((1, D_MODEL)), full((1, D_MODEL)),
            full((D_MODEL, ROUTER_LANES)), full((D_MODEL, ROUTER_LANES)), full((N_EXPERTS, 1)), full((tm, tm)),
        ],
        out_specs=[
            pl.BlockSpec((tm, D_MODEL), lambda i: (i, 0)),
            pl.BlockSpec((tm, D_MODEL // 2), lambda i: (i, 0)),
            pl.BlockSpec((META_ROWS, tm), lambda i: (0, i)),
            pl.BlockSpec((N_EXPERTS, 1), lambda i: (0, 0)),
        ],
        out_shape=[
            jax.ShapeDtypeStruct((t, D_MODEL), jnp.float32),
            jax.ShapeDtypeStruct((t, D_MODEL // 2), jnp.int32),
            jax.ShapeDtypeStruct((META_ROWS, t), jnp.float32),
            jax.ShapeDtypeStruct((N_EXPERTS, 1), jnp.float32),
        ],
        scratch_shapes=[pltpu.VMEM((ATTN_BLOCK, 2 * KV_WIDTH), jnp.bfloat16)],
        compiler_params=pltpu.CompilerParams(
            dimension_semantics=("arbitrary",), vmem_limit_bytes=VMEM_LIMIT_BYTES),
        name="mixer",
    )(x2, w_in, b_in, _score_bias(sinks), lnv_g, lnv_b, grp_avg, w_sp, b_sp_full, w_out, b_out, ln1_g, ln1_b,
      w_r, w_r_hi, b_r, tri)


def _gather_rows(idx, src):
    n = idx.shape[0]
    width = src.shape[1]
    win = GATHER_WINDOW
    sc = plsc.get_sparse_core_info()
    n_workers = sc.num_cores * sc.num_subcores
    per_worker = n // n_workers
    n_pairs = per_worker // (2 * win)
    assert n_pairs * 2 * win * n_workers == n
    mesh = plsc.VectorSubcoreMesh(core_axis_name="core", subcore_axis_name="subcore")

    @functools.partial(
        pl.kernel, out_type=jax.ShapeDtypeStruct((n, width), src.dtype), mesh=mesh,
        scratch_types=[pltpu.VMEM((per_worker,), jnp.int32), pltpu.VMEM((2, win, width), src.dtype),
                       pltpu.SemaphoreType.DMA((2,)), pltpu.SemaphoreType.DMA((2,))],
        name="gather_rows")
    def gather(src_hbm, idx_hbm, out_hbm, idx_v, rows_v, fetch_sem, store_sem):
        worker = lax.axis_index("subcore") * sc.num_cores + lax.axis_index("core")
        base = worker * per_worker
        pltpu.sync_copy(idx_hbm.at[pl.ds(base, per_worker)], idx_v)

        def fetch(chunk, buf):
            return pltpu.make_async_copy(src_hbm.at[idx_v.at[pl.ds(chunk * win, win)]], rows_v.at[buf],
                                         fetch_sem.at[buf])

        def store(chunk, buf):
            return pltpu.make_async_copy(rows_v.at[buf], out_hbm.at[pl.ds(base + chunk * win, win)],
                                         store_sem.at[buf])

        @pl.loop(0, n_pairs)
        def _(p):
            for buf in range(2):
                @pl.when(p > 0)
                def _():
                    store(2 * p - 2 + buf, buf).wait()
                fetch(2 * p + buf, buf).start()
            for buf in range(2):
                fetch(2 * p + buf, buf).wait()
                store(2 * p + buf, buf).start()

        for buf in range(2):
            store(2 * n_pairs - 2 + buf, buf).wait()

    return gather(src, idx)


def _dispatch_rows(dest_t, src, n_rows):
    t, width = src.shape
    win = GATHER_WINDOW
    sc = plsc.get_sparse_core_info()
    n_workers = sc.num_cores * sc.num_subcores
    per_worker = t // n_workers
    n_chunks = per_worker // win
    n_pairs = n_chunks // 2
    assert n_pairs * 2 * win * n_workers == t
    idx = dest_t.reshape(TOP_K, n_workers, n_chunks, win).transpose(1, 0, 2, 3)
    idx = idx.reshape(n_workers, TOP_K * n_chunks, win)
    mesh = plsc.VectorSubcoreMesh(core_axis_name="core", subcore_axis_name="subcore")

    @functools.partial(
        pl.kernel, out_type=jax.ShapeDtypeStruct((n_rows, width), src.dtype), mesh=mesh,
        scratch_types=[pltpu.VMEM((TOP_K * n_chunks, win), jnp.int32), pltpu.VMEM((2, win, width), src.dtype),
                       pltpu.SemaphoreType.DMA((2,)), pltpu.SemaphoreType.DMA((2,))],
        name="dispatch_rows")
    def dispatch(src_hbm, idx_hbm, out_hbm, idx_v, rows_v, fetch_sem, store_sem):
        worker = lax.axis_index("subcore") * sc.num_cores + lax.axis_index("core")
        base = worker * per_worker
        pltpu.sync_copy(idx_hbm.at[worker], idx_v)

        def fetch(chunk, buf):
            return pltpu.make_async_copy(src_hbm.at[pl.ds(base + chunk * win, win)], rows_v.at[buf],
                                         fetch_sem.at[buf])

        def store(chunk, k, buf):
            return pltpu.make_async_copy(rows_v.at[buf], out_hbm.at[idx_v.at[k * n_chunks + chunk]],
                                         store_sem.at[buf])

        @pl.loop(0, n_pairs)
        def _(p):
            for buf in range(2):
                @pl.when(p > 0)
                def _():
                    for k in range(TOP_K):
                        store(2 * p - 2 + buf, k, buf).wait()
                fetch(2 * p + buf, buf).start()
            for buf in range(2):
                fetch(2 * p + buf, buf).wait()
                for k in range(TOP_K):
                    store(2 * p + buf, k, buf).start()

        for buf in range(2):
            for k in range(TOP_K):
                store(2 * n_pairs - 2 + buf, k, buf).wait()

    return dispatch(src, idx)


def _expert_kernel(block_e_ref, n_valid_ref, first_ref, next_e_ref, slot_ref,
                   x_ref, wg_hbm, bg_ref, wu_hbm, bu_ref, wd_hbm, bd_ref,
                   y_ref, wg_f32, wu_f32, wd_f32, wg_bf, wu_bf, wd_bf, sem):
    i = pl.program_id(0)
    n_valid = n_valid_ref[i]
    slot = slot_ref[i]
    staged = ((wg_hbm, wg_f32, wg_bf), (wu_hbm, wu_f32, wu_bf), (wd_hbm, wd_f32, wd_bf))

    def weight_copy(m, expert, s):
        return pltpu.make_async_copy(staged[m][0].at[expert], staged[m][1].at[s], sem.at[s, m])

    @pl.when(i == 0)
    def _():
        for m in range(3):
            weight_copy(m, block_e_ref[0], 0).start()

    @pl.when(first_ref[i] == 1)
    def _():
        @pl.when(next_e_ref[i] >= 0)
        def _():
            for m in range(3):
                weight_copy(m, next_e_ref[i], 1 - slot).start()

        for m in range(3):
            weight_copy(m, block_e_ref[i], slot).wait()
            staged[m][2][...] = staged[m][1][slot].astype(jnp.bfloat16)

    def expert_mlp(rows):
        row = lax.broadcasted_iota(jnp.int32, (rows, 1), 0)
        x_hi, x_lo = _unpack_bf16_pairs(jnp.where(row < n_valid, x_ref[:rows, :], 0))
        xb = jnp.concatenate([x_hi, x_lo], axis=1).astype(jnp.bfloat16)
        gt = jnp.minimum(jnp.dot(xb, wg_bf[...], preferred_element_type=jnp.float32) + bg_ref[0], SWIGLU_LIMIT)
        up = jnp.clip(jnp.dot(xb, wu_bf[...], preferred_element_type=jnp.float32) + bu_ref[0],
                      -SWIGLU_LIMIT, SWIGLU_LIMIT)
        hid = gt * jax.nn.sigmoid(SWIGLU_ALPHA * gt) * (up + 1.0)
        y = jnp.dot(hid.astype(jnp.bfloat16), wd_bf[...], preferred_element_type=jnp.float32) + bd_ref[0]
        y_ref[:rows, :] = _pack_bf16_pairs(y.astype(jnp.bfloat16))
        if rows < y_ref.shape[0]:
            y_ref[rows:, :] = jnp.zeros((y_ref.shape[0] - rows, y_ref.shape[1]), y_ref.dtype)

    row_options = (0,) + EXPERT_PARTIAL_ROWS + (x_ref.shape[0],)
    for lo, hi in zip(row_options[:-1], row_options[1:]):
        pl.when(jnp.logical_and(n_valid > lo, n_valid <= hi))(functools.partial(expert_mlp, hi))

    @pl.when(n_valid == 0)
    def _():
        y_ref[...] = jnp.zeros_like(y_ref)


def _experts(block_e, n_valid, x_rows, w_gate, b_gate, w_up, b_up, w_down, b_down):
    n_rows = x_rows.shape[0]
    bm = EXPERT_ROWS
    n_blocks = n_rows // bm
    d_ff = w_gate.shape[2]

    blocks = jnp.arange(n_blocks, dtype=jnp.int32)
    is_first = jnp.concatenate([jnp.ones((1,), jnp.int32), (block_e[1:] != block_e[:-1]).astype(jnp.int32)])
    group = jnp.sum(jnp.where(blocks[None, :] <= blocks[:, None], is_first[None, :], 0), axis=1) - 1
    next_e = jnp.min(jnp.where(block_e[None, :] > block_e[:, None], block_e[None, :], N_EXPERTS), axis=1)
    next_e = jnp.where(next_e == N_EXPERTS, -1, next_e)

    def by_expert(shape):
        return pl.BlockSpec(shape, lambda i, be, *_: (be[i],) + (0,) * (len(shape) - 1))

    hbm = pl.BlockSpec(memory_space=pl.ANY)
    return pl.pallas_call(
        _expert_kernel,
        grid_spec=pltpu.PrefetchScalarGridSpec(
            num_scalar_prefetch=5,
            grid=(n_blocks,),
            in_specs=[
                pl.BlockSpec((bm, D_MODEL // 2), lambda i, *_: (i, 0)),
                hbm, by_expert((1, 1, d_ff)),
                hbm, by_expert((1, 1, d_ff)),
                hbm, by_expert((1, 1, D_MODEL)),
            ],
            out_specs=pl.BlockSpec((bm, D_MODEL // 2), lambda i, *_: (i, 0)),
            scratch_shapes=[
                pltpu.VMEM((2, D_MODEL, d_ff), jnp.float32),
                pltpu.VMEM((2, D_MODEL, d_ff), jnp.float32),
                pltpu.VMEM((2, d_ff, D_MODEL), jnp.float32),
                pltpu.VMEM((D_MODEL, d_ff), jnp.bfloat16),
                pltpu.VMEM((D_MODEL, d_ff), jnp.bfloat16),
                pltpu.VMEM((d_ff, D_MODEL), jnp.bfloat16),
                pltpu.SemaphoreType.DMA((2, 3)),
            ],
        ),
        out_shape=jax.ShapeDtypeStruct((n_rows, D_MODEL // 2), jnp.int32),
        compiler_params=pltpu.CompilerParams(
            dimension_semantics=("arbitrary",), vmem_limit_bytes=VMEM_LIMIT_BYTES),
        name="experts",
    )(block_e, n_valid, is_first, next_e, group % 2, x_rows, w_gate, b_gate.reshape(N_EXPERTS, 1, d_ff), w_up,
      b_up.reshape(N_EXPERTS, 1, d_ff), w_down, b_down.reshape(N_EXPERTS, 1, D_MODEL))


def _combine_kernel(h_ref, y_ref, meta_ref, g_ref, b_ref, o_ref):
    tm = h_ref.shape[0]
    meta_cols = jnp.concatenate([meta_ref[...], jnp.zeros((128 - META_ROWS, tm), jnp.float32)], axis=0).T
    ffn_hi, ffn_lo = 0.0, 0.0
    for k in range(TOP_K):
        gate = meta_cols[:, 2 * TOP_K + k:2 * TOP_K + k + 1]
        y_hi, y_lo = _unpack_bf16_pairs(y_ref[k])
        ffn_hi = ffn_hi + gate * y_hi
        ffn_lo = ffn_lo + gate * y_lo
    ffn = jnp.concatenate([ffn_hi, ffn_lo], axis=1)
    o_ref[...] = _layer_norm(DEEPNORM_ALPHA * h_ref[...] + ffn, g_ref[...], b_ref[...])


def _combine(h, y_tok, meta, ln2_g, ln2_b):
    t = h.shape[0]
    tm = COMBINE_ROWS
    return pl.pallas_call(
        _combine_kernel,
        grid=(t // tm,),
        in_specs=[
            pl.BlockSpec((tm, D_MODEL), lambda i: (i, 0)),
            pl.BlockSpec((TOP_K, tm, D_MODEL // 2), lambda i: (0, i, 0)),
            pl.BlockSpec((META_ROWS, tm), lambda i: (0, i)),
            pl.BlockSpec((1, D_MODEL), lambda i: (0, 0)),
            pl.BlockSpec((1, D_MODEL), lambda i: (0, 0)),
        ],
        out_specs=pl.BlockSpec((tm, D_MODEL), lambda i: (i, 0)),
        out_shape=jax.ShapeDtypeStruct((t, D_MODEL), jnp.float32),
        compiler_params=pltpu.CompilerParams(
            dimension_semantics=("arbitrary",), vmem_limit_bytes=VMEM_LIMIT_BYTES),
        name="combine",
    )(h, y_tok, meta, ln2_g, ln2_b)


def _layer(x2, seq_len, w_in, b_in, sinks, ln_v_g, ln_v_b, w_spatial, b_spatial, w_out, b_out,
           ln1_g, ln1_b, w_router, b_router, w_gate, b_gate, w_up, b_up, w_down, b_down, ln2_g, ln2_b):
    t = x2.shape[0]
    tk = t * TOP_K
    bm = EXPERT_ROWS
    bf16 = jnp.bfloat16

    w_r_hi = w_router.astype(bf16)
    w_r_lo = (w_router - w_r_hi.astype(jnp.float32)).astype(bf16)
    lane_pad = jnp.zeros((D_MODEL, ROUTER_LANES - 2 * N_EXPERTS), bf16)
    w_r = jnp.concatenate([w_r_hi, w_r_lo, lane_pad], axis=1)
    w_r_hi_only = jnp.concatenate([w_r_hi, jnp.zeros_like(w_r_lo), lane_pad], axis=1)
    b_sp_full = jnp.repeat(b_spatial.T, GMLP_GROUP_DIM, axis=1)

    h, h_packed, meta, counts = _mixer(
        x2, sinks, w_in.astype(bf16), b_in[None], ln_v_g[None], ln_v_b[None], w_spatial, b_sp_full,
        w_out.astype(bf16), b_out[None], ln1_g[None], ln1_b[None], w_r, w_r_hi_only, b_router[:, None], seq_len)

    counts = counts[:, 0].astype(jnp.int32)
    experts = jnp.arange(N_EXPERTS, dtype=jnp.int32)
    padded = (counts + bm - 1) // bm * bm
    padded_end = jnp.sum(jnp.where(experts[None, :] <= experts[:, None], padded[None, :], 0), axis=1)
    padded_start = padded_end - padded
    n_blocks = tk // bm + N_EXPERTS
    n_rows = n_blocks * bm
    top_idx_t = meta[:TOP_K].astype(jnp.int32)
    rank_t = meta[TOP_K:2 * TOP_K].astype(jnp.int32)
    dest_t = rank_t + jnp.sum(
        jnp.where(top_idx_t[None] == experts[:, None, None], padded_start[:, None, None], 0), axis=0)
    block_start = jnp.arange(n_blocks, dtype=jnp.int32) * bm
    block_e = jnp.minimum(
        jnp.sum((padded_end[None, :] <= block_start[:, None]).astype(jnp.int32), axis=1), N_EXPERTS - 1)
    valid_end = jnp.sum(jnp.where(block_e[:, None] == experts[None, :], (padded_start + counts)[None, :], 0), axis=1)
    n_valid = jnp.clip(valid_end - block_start, 0, bm)

    x_rows = _dispatch_rows(dest_t, h_packed, n_rows)
    y_rows = _experts(block_e, n_valid, x_rows, w_gate, b_gate, w_up, b_up, w_down, b_down)
    y_tok = _gather_rows(dest_t.reshape(-1), y_rows).reshape(TOP_K, t, D_MODEL // 2)
    return _combine(h, y_tok, meta, ln2_g[None], ln2_b[None])


def kernel(x, w_in, b_in, sinks, ln_v_g, ln_v_b, w_spatial, b_spatial, w_out, b_out, ln1_g, ln1_b,
           w_router, b_router, w_gate, b_gate, w_up, b_up, w_down, b_down, ln2_g, ln2_b):
    batch, seq_len, d = x.shape
    x2 = x.reshape(batch * seq_len, d)
    for l in range(DEPTH):
        x2 = _layer(x2, seq_len, w_in[l], b_in[l], sinks[l], ln_v_g[l], ln_v_b[l], w_spatial[l],
                    b_spatial[l], w_out[l], b_out[l], ln1_g[l], ln1_b[l], w_router[l], b_router[l],
                    w_gate[l], b_gate[l], w_up[l], b_up[l], w_down[l], b_down[l], ln2_g[l], ln2_b[l])
    return x2.reshape(batch, seq_len, d)
```

```python
import functools

import jax
import jax.numpy as jnp
from jax import lax
from jax.experimental import pallas as pl
from jax.experimental.pallas import tpu as pltpu
from jax.experimental.pallas import tpu_sc as plsc

D_MODEL = 1024
HEAD_DIM = 64
N_Q_HEADS = 8
N_KV_HEADS = 2
Q_REP = N_Q_HEADS // N_KV_HEADS
ATTN_WIDTH = N_Q_HEADS * HEAD_DIM
KV_WIDTH = N_KV_HEADS * HEAD_DIM
ATTN_BLOCK = 128
N_GMLP_GROUPS = 8
GMLP_WIDTH = D_MODEL - ATTN_WIDTH
GMLP_GROUP_DIM = GMLP_WIDTH // N_GMLP_GROUPS
IN_WIDTH = ATTN_WIDTH + 2 * KV_WIDTH + 2 * GMLP_WIDTH
N_EXPERTS = 32
TOP_K = 4
SWIGLU_LIMIT = 7.0
SWIGLU_ALPHA = 1.702
LN_EPS = 1e-5
DEPTH = 1
DEEPNORM_ALPHA = (2.0 * DEPTH) ** 0.25
NEG_INF = -1e30
LOG2_E = 1.4426950408889634

MIXER_ROWS = 512
EXPERT_ROWS = 512
EXPERT_PARTIAL_ROWS = (128, 256, 384)
EXPERT_FF_CHUNKS = 2
GATHER_WINDOW = 64
META_ROWS = 16
ROUTER_LANES = 128
COMBINE_ROWS = 1024
VMEM_LIMIT_BYTES = 56 * 1024 * 1024

_O_K = ATTN_WIDTH
_O_V = _O_K + KV_WIDTH
_O_U = _O_V + KV_WIDTH
_O_G = _O_U + GMLP_WIDTH


def _pack_bf16_pairs(v):
    n = v.shape[1] // 2
    hi = lax.bitcast_convert_type(v[:, :n].astype(jnp.float32), jnp.int32)
    lo = lax.bitcast_convert_type(v[:, n:].astype(jnp.float32), jnp.int32)
    return hi | lax.shift_right_logical(lo, 16)


def _unpack_bf16_pairs(p):
    hi = lax.bitcast_convert_type(p & jnp.int32(-65536), jnp.float32)
    lo = lax.bitcast_convert_type(lax.shift_left(p, 16), jnp.float32)
    return hi, lo


def _layer_norm(v, g, b):
    mu = jnp.mean(v, axis=-1, keepdims=True)
    vc = v - mu
    var = jnp.mean(vc * vc, axis=-1, keepdims=True)
    return vc * lax.rsqrt(var + LN_EPS) * g + b


def _attention_block(q, kb, vb, bias_ref, bias_sel):
    outs = []
    for g in range(N_KV_HEADS):
        kg = kb[:, g * HEAD_DIM:(g + 1) * HEAD_DIM]
        vg = vb[:, g * HEAD_DIM:(g + 1) * HEAD_DIM]
        qg = jnp.concatenate(
            [q[:, (g * Q_REP + r) * HEAD_DIM:(g * Q_REP + r + 1) * HEAD_DIM] for r in range(Q_REP)],
            axis=0).astype(jnp.bfloat16)
        s = (lax.dot_general(qg, kg, (((1,), (1,)), ((), ())), preferred_element_type=jnp.float32)
             + bias_ref[g, bias_sel])
        p = jnp.exp2(s - jnp.max(s, axis=-1, keepdims=True))
        denom = jnp.sum(p, axis=-1, keepdims=True)
        o = jnp.dot(p.astype(jnp.bfloat16), vg, preferred_element_type=jnp.float32) / denom
        outs.extend(o[r * ATTN_BLOCK:(r + 1) * ATTN_BLOCK] for r in range(Q_REP))
    return jnp.concatenate(outs, axis=-1)


def _mixer_kernel(x_ref, w_in_ref, b_in_ref, bias_ref, lnv_g_ref, lnv_b_ref, grp_avg_ref,
                  w_sp_ref, b_sp_ref, w_out_ref, b_out_ref, ln1_g_ref, ln1_b_ref,
                  w_r_ref, w_r_hi_ref, b_r_ref, tri_ref,
                  h_ref, hp_ref, meta_ref, count_ref,
                  kv_prev_ref, *, steps_per_seq):
    i = pl.program_id(0)
    first_step = (i % steps_per_seq) == 0
    tm = x_ref.shape[0]
    n_sub = tm // ATTN_BLOCK

    @pl.when(i == 0)
    def _():
        count_ref[...] = jnp.zeros_like(count_ref)

    @pl.when(first_step)
    def _():
        kv_prev_ref[...] = jnp.zeros_like(kv_prev_ref)

    x = x_ref[...]
    proj = jnp.dot(x.astype(jnp.bfloat16), w_in_ref[...], preferred_element_type=jnp.float32) + b_in_ref[...]

    q_all = proj[:, :_O_K] * (LOG2_E * HEAD_DIM ** -0.5)
    k_all = proj[:, _O_K:_O_V].astype(jnp.bfloat16)
    v_all = proj[:, _O_V:_O_U].astype(jnp.bfloat16)
    k_prev = kv_prev_ref[:, :KV_WIDTH]
    v_prev = kv_prev_ref[:, KV_WIDTH:]
    is_row0 = lax.broadcasted_iota(jnp.int32, (ATTN_BLOCK, KV_WIDTH), 0) == 0
    attn_blocks = []
    for sb in range(n_sub):
        rows = slice(sb * ATTN_BLOCK, (sb + 1) * ATTN_BLOCK)
        k_cur, v_cur = k_all[rows], v_all[rows]
        kb = jnp.concatenate([jnp.where(is_row0, 0, k_prev), k_cur], axis=0)
        vb = jnp.concatenate([jnp.where(is_row0, 0, v_prev), v_cur], axis=0)
        bias_sel = jnp.where(first_step, 1, 0) if sb == 0 else 0
        attn_blocks.append(_attention_block(q_all[rows], kb, vb, bias_ref, bias_sel))
        k_prev, v_prev = k_cur, v_cur
    kv_prev_ref[:, :KV_WIDTH] = k_prev
    kv_prev_ref[:, KV_WIDTH:] = v_prev
    attn = jnp.concatenate(attn_blocks, axis=0)

    u = jax.nn.gelu(proj[:, _O_U:_O_G])
    gg = jax.nn.gelu(proj[:, _O_G:])
    avg = grp_avg_ref[...]
    mu = jnp.dot(gg.astype(jnp.bfloat16), avg, preferred_element_type=jnp.float32)
    gc = gg - mu
    var = jnp.dot((gc * gc).astype(jnp.bfloat16), avg, preferred_element_type=jnp.float32)
    gn = (gc * lax.rsqrt(var + LN_EPS) * lnv_g_ref[...] + lnv_b_ref[...]).astype(jnp.bfloat16)
    causal = (lax.broadcasted_iota(jnp.int32, (ATTN_BLOCK, ATTN_BLOCK), 0)
              >= lax.broadcasted_iota(jnp.int32, (ATTN_BLOCK, ATTN_BLOCK), 1))
    w_sp = [jnp.where(causal, w_sp_ref[g], 0.0).astype(jnp.bfloat16) for g in range(N_GMLP_GROUPS)]
    mixed_chunks = []
    for c in range(n_sub):
        rows = slice(c * ATTN_BLOCK, (c + 1) * ATTN_BLOCK)
        pieces = [
            jnp.dot(w_sp[g], gn[rows, g * GMLP_GROUP_DIM:(g + 1) * GMLP_GROUP_DIM],
                    preferred_element_type=jnp.float32)
            for g in range(N_GMLP_GROUPS)]
        mixed_chunks.append(jnp.concatenate(pieces, axis=-1) + b_sp_ref[...])
    sgu = u * jnp.concatenate(mixed_chunks, axis=0)

    mix = (jnp.dot(attn.astype(jnp.bfloat16), w_out_ref[:ATTN_WIDTH, :], preferred_element_type=jnp.float32)
           + jnp.dot(sgu.astype(jnp.bfloat16), w_out_ref[ATTN_WIDTH:, :], preferred_element_type=jnp.float32)
           + b_out_ref[...])
    h = _layer_norm(DEEPNORM_ALPHA * x + mix, ln1_g_ref[...], ln1_b_ref[...])
    h_ref[...] = h

    h_hi = h.astype(jnp.bfloat16)
    hp_ref[...] = _pack_bf16_pairs(h_hi)
    h_lo = (h - h_hi.astype(jnp.float32)).astype(jnp.bfloat16)
    part = (jnp.dot(h_hi, w_r_ref[...], preferred_element_type=jnp.float32)
            + jnp.dot(h_lo, w_r_hi_ref[...], preferred_element_type=jnp.float32)).T
    logits = part[:N_EXPERTS] + part[N_EXPERTS:2 * N_EXPERTS] + b_r_ref[...]

    n_grp = N_EXPERTS // 8
    grp = [logits[8 * g:8 * (g + 1)] for g in range(n_grp)]
    sub = lax.broadcasted_iota(jnp.int32, (8, tm), 0)
    beaten = [jnp.zeros((8, tm), jnp.float32) for _ in range(n_grp)]
    for e2 in range(N_EXPERTS):
        g2, r2 = divmod(e2, 8)
        row = logits[e2:e2 + 1]
        for g in range(n_grp):
            if g > g2:
                wins = jnp.where(row >= grp[g], 1.0, 0.0)
            elif g < g2:
                wins = jnp.where(row > grp[g], 1.0, 0.0)
            else:
                wins = jnp.where(sub > r2, jnp.where(row >= grp[g], 1.0, 0.0), jnp.where(row > grp[g], 1.0, 0.0))
            beaten[g] = beaten[g] + wins
    place = jnp.concatenate(beaten, axis=0)
    expert_id = lax.broadcasted_iota(jnp.int32, (N_EXPERTS, tm), 0).astype(jnp.float32)
    onehot = jnp.where(place < TOP_K, 1.0, 0.0)
    before = jnp.dot(onehot.astype(jnp.bfloat16), tri_ref[...], preferred_element_type=jnp.float32) + count_ref[...]

    def pick(k, table):
        return jnp.sum(jnp.where(place == k, table, 0.0), axis=0, keepdims=True)

    vals = [pick(k, logits) for k in range(TOP_K)]
    exps = [jnp.exp(v - vals[0]) for v in vals]
    denom = exps[0] + exps[1] + exps[2] + exps[3]
    meta_ref[...] = jnp.zeros_like(meta_ref)
    for k in range(TOP_K):
        meta_ref[k:k + 1, :] = pick(k, expert_id)
        meta_ref[TOP_K + k:TOP_K + k + 1, :] = pick(k, before)
        meta_ref[2 * TOP_K + k:2 * TOP_K + k + 1, :] = exps[k] / denom
    count_ref[...] += jnp.sum(onehot, axis=1, keepdims=True)


def _score_bias(sinks):
    t_idx = jnp.arange(Q_REP * ATTN_BLOCK)[:, None] % ATTN_BLOCK
    s_idx = jnp.arange(2 * ATTN_BLOCK)[None, :]
    diff = t_idx + ATTN_BLOCK - s_idx
    band = (diff >= 0) & (diff < ATTN_BLOCK)
    masks = jnp.stack([band, band & (s_idx >= ATTN_BLOCK)])
    bias = jnp.where(masks, 0.0, NEG_INF).astype(jnp.float32)
    sink_rows = jnp.repeat(sinks.reshape(N_KV_HEADS, Q_REP) * LOG2_E, ATTN_BLOCK, axis=1)
    return jnp.where(s_idx == 0, sink_rows[:, None, :, None], bias[None])


def _mixer(x2, sinks, w_in, b_in, lnv_g, lnv_b, w_sp, b_sp_full, w_out, b_out, ln1_g, ln1_b,
           w_r, w_r_hi, b_r, seq_len):
    t = x2.shape[0]
    tm = MIXER_ROWS
    grp = jnp.arange(GMLP_WIDTH) // GMLP_GROUP_DIM
    grp_avg = jnp.where(grp[:, None] == grp[None, :], 1.0 / GMLP_GROUP_DIM, 0.0).astype(jnp.bfloat16)
    tri = (jnp.arange(tm)[:, None] < jnp.arange(tm)[None, :]).astype(jnp.bfloat16)

    def full(shape):
        return pl.BlockSpec(shape, lambda i: (0,) * len(shape))

    return pl.pallas_call(
        functools.partial(_mixer_kernel, steps_per_seq=seq_len // tm),
        grid=(t // tm,),
        in_specs=[
            pl.BlockSpec((tm, D_MODEL), lambda i: (i, 0)),
            full((D_MODEL, IN_WIDTH)), full((1, IN_WIDTH)),
            full((N_KV_HEADS, 2, Q_REP * ATTN_BLOCK, 2 * ATTN_BLOCK)),
            full((1, GMLP_WIDTH)), full((1, GMLP_WIDTH)), full((GMLP_WIDTH, GMLP_WIDTH)),
            full((N_GMLP_GROUPS, ATTN_BLOCK, ATTN_BLOCK)), full((ATTN_BLOCK, GMLP_WIDTH)),
            full((D_MODEL, D_MODEL)), full((1, D_MODEL)), full((1, D_MODEL)), full((1, D_MODEL)),
            full((D_MODEL, ROUTER_LANES)), full((D_MODEL, ROUTER_LANES)), full((N_EXPERTS, 1)), full((tm, tm)),
        ],
        out_specs=[
            pl.BlockSpec((tm, D_MODEL), lambda i: (i, 0)),
            pl.BlockSpec((tm, D_MODEL // 2), lambda i: (i, 0)),
            pl.BlockSpec((META_ROWS, tm), lambda i: (0, i)),
            pl.BlockSpec((N_EXPERTS, 1), lambda i: (0, 0)),
        ],
        out_shape=[
            jax.ShapeDtypeStruct((t, D_MODEL), jnp.float32),
            jax.ShapeDtypeStruct((t, D_MODEL // 2), jnp.int32),
            jax.ShapeDtypeStruct((META_ROWS, t), jnp.float32),
            jax.ShapeDtypeStruct((N_EXPERTS, 1), jnp.float32),
        ],
        scratch_shapes=[pltpu.VMEM((ATTN_BLOCK, 2 * KV_WIDTH), jnp.bfloat16)],
        compiler_params=pltpu.CompilerParams(
            dimension_semantics=("arbitrary",), vmem_limit_bytes=VMEM_LIMIT_BYTES),
        name="mixer",
    )(x2, w_in, b_in, _score_bias(sinks), lnv_g, lnv_b, grp_avg, w_sp, b_sp_full, w_out, b_out, ln1_g, ln1_b,
      w_r, w_r_hi, b_r, tri)


def _gather_rows(idx, src):
    n = idx.shape[0]
    width = src.shape[1]
    win = GATHER_WINDOW
    sc = plsc.get_sparse_core_info()
    n_workers = sc.num_cores * sc.num_subcores
    per_worker = n // n_workers
    n_pairs = per_worker // (2 * win)
    assert n_pairs * 2 * win * n_workers == n
    mesh = plsc.VectorSubcoreMesh(core_axis_name="core", subcore_axis_name="subcore")

    @functools.partial(
        pl.kernel, out_type=jax.ShapeDtypeStruct((n, width), src.dtype), mesh=mesh,
        scratch_types=[pltpu.VMEM((per_worker,), jnp.int32), pltpu.VMEM((2, win, width), src.dtype),
                       pltpu.SemaphoreType.DMA((2,)), pltpu.SemaphoreType.DMA((2,))],
        name="gather_rows")
    def gather(src_hbm, idx_hbm, out_hbm, idx_v, rows_v, fetch_sem, store_sem):
        worker = lax.axis_index("subcore") * sc.num_cores + lax.axis_index("core")
        base = worker * per_worker
        pltpu.sync_copy(idx_hbm.at[pl.ds(base, per_worker)], idx_v)

        def fetch(chunk, buf):
            return pltpu.make_async_copy(src_hbm.at[idx_v.at[pl.ds(chunk * win, win)]], rows_v.at[buf],
                                         fetch_sem.at[buf])

        def store(chunk, buf):
            return pltpu.make_async_copy(rows_v.at[buf], out_hbm.at[pl.ds(base + chunk * win, win)],
                                         store_sem.at[buf])

        @pl.loop(0, n_pairs)
        def _(p):
            for buf in range(2):
                @pl.when(p > 0)
                def _():
                    store(2 * p - 2 + buf, buf).wait()
                fetch(2 * p + buf, buf).start()
            for buf in range(2):
                fetch(2 * p + buf, buf).wait()
                store(2 * p + buf, buf).start()

        for buf in range(2):
            store(2 * n_pairs - 2 + buf, buf).wait()

    return gather(src, idx)


def _dispatch_rows(dest_t, src, n_rows):
    t, width = src.shape
    win = GATHER_WINDOW
    sc = plsc.get_sparse_core_info()
    n_workers = sc.num_cores * sc.num_subcores
    per_worker = t // n_workers
    n_chunks = per_worker // win
    n_pairs = n_chunks // 2
    assert n_pairs * 2 * win * n_workers == t
    idx = dest_t.reshape(TOP_K, n_workers, n_chunks, win).transpose(1, 0, 2, 3)
    idx = idx.reshape(n_workers, TOP_K * n_chunks, win)
    mesh = plsc.VectorSubcoreMesh(core_axis_name="core", subcore_axis_name="subcore")

    @functools.partial(
        pl.kernel, out_type=jax.ShapeDtypeStruct((n_rows, width), src.dtype), mesh=mesh,
        scratch_types=[pltpu.VMEM((TOP_K * n_chunks, win), jnp.int32), pltpu.VMEM((2, win, width), src.dtype),
                       pltpu.SemaphoreType.DMA((2,)), pltpu.SemaphoreType.DMA((2,))],
        name="dispatch_rows")
    def dispatch(src_hbm, idx_hbm, out_hbm, idx_v, rows_v, fetch_sem, store_sem):
        worker = lax.axis_index("subcore") * sc.num_cores + lax.axis_index("core")
        base = worker * per_worker
        pltpu.sync_copy(idx_hbm.at[worker], idx_v)

        def fetch(chunk, buf):
            return pltpu.make_async_copy(src_hbm.at[pl.ds(base + chunk * win, win)], rows_v.at[buf],
                                         fetch_sem.at[buf])

        def store(chunk, k, buf):
            return pltpu.make_async_copy(rows_v.at[buf], out_hbm.at[idx_v.at[k * n_chunks + chunk]],
                                         store_sem.at[buf])

        @pl.loop(0, n_pairs)
        def _(p):
            for buf in range(2):
                @pl.when(p > 0)
                def _():
                    for k in range(TOP_K):
                        store(2 * p - 2 + buf, k, buf).wait()
                fetch(2 * p + buf, buf).start()
            for buf in range(2):
                fetch(2 * p + buf, buf).wait()
                for k in range(TOP_K):
                    store(2 * p + buf, k, buf).start()

        for buf in range(2):
            for k in range(TOP_K):
                store(2 * n_pairs - 2 + buf, k, buf).wait()

    return dispatch(src, idx)


def _expert_kernel(block_e_ref, n_valid_ref, first_ref, next_e_ref, slot_ref,
                   x_ref, wg_hbm, bg_ref, wu_hbm, bu_ref, wd_hbm, bd_ref,
                   y_ref, wg_f32, wu_f32, wd_f32, wg_bf, wu_bf, wd_bf, sem):
    i = pl.program_id(0)
    n_valid = n_valid_ref[i]
    slot = slot_ref[i]
    staged = ((wg_hbm, wg_f32, wg_bf), (wu_hbm, wu_f32, wu_bf), (wd_hbm, wd_f32, wd_bf))

    def weight_copy(m, expert, s):
        return pltpu.make_async_copy(staged[m][0].at[expert], staged[m][1].at[s], sem.at[s, m])

    @pl.when(i == 0)
    def _():
        for m in range(3):
            weight_copy(m, block_e_ref[0], 0).start()

    @pl.when(first_ref[i] == 1)
    def _():
        @pl.when(next_e_ref[i] >= 0)
        def _():
            for m in range(3):
                weight_copy(m, next_e_ref[i], 1 - slot).start()

        for m in range(3):
            weight_copy(m, block_e_ref[i], slot).wait()
            staged[m][2][...] = staged[m][1][slot].astype(jnp.bfloat16)

    def expert_mlp(rows):
        row = lax.broadcasted_iota(jnp.int32, (rows, 1), 0)
        x_hi, x_lo = _unpack_bf16_pairs(jnp.where(row < n_valid, x_ref[:rows, :], 0))
        xb = jnp.concatenate([x_hi, x_lo], axis=1).astype(jnp.bfloat16)
        d_ff = wg_bf.shape[1]
        chunk = d_ff // EXPERT_FF_CHUNKS
        y = bd_ref[0]
        for c in range(EXPERT_FF_CHUNKS):
            cols = pl.ds(c * chunk, chunk)
            gt = jnp.minimum(jnp.dot(xb, wg_bf[:, cols], preferred_element_type=jnp.float32) + bg_ref[0, :, cols],
                             SWIGLU_LIMIT)
            up = jnp.clip(jnp.dot(xb, wu_bf[:, cols], preferred_element_type=jnp.float32) + bu_ref[0, :, cols],
                          -SWIGLU_LIMIT, SWIGLU_LIMIT)
            hid = gt * jax.nn.sigmoid(SWIGLU_ALPHA * gt) * (up + 1.0)
            y = y + jnp.dot(hid.astype(jnp.bfloat16), wd_bf[cols, :], preferred_element_type=jnp.float32)
        y_ref[:rows, :] = _pack_bf16_pairs(y.astype(jnp.bfloat16))
        if rows < y_ref.shape[0]:
            y_ref[rows:, :] = jnp.zeros((y_ref.shape[0] - rows, y_ref.shape[1]), y_ref.dtype)

    row_options = (0,) + EXPERT_PARTIAL_ROWS + (x_ref.shape[0],)
    for lo, hi in zip(row_options[:-1], row_options[1:]):
        pl.when(jnp.logical_and(n_valid > lo, n_valid <= hi))(functools.partial(expert_mlp, hi))

    @pl.when(n_valid == 0)
    def _():
        y_ref[...] = jnp.zeros_like(y_ref)


def _experts(block_e, n_valid, x_rows, w_gate, b_gate, w_up, b_up, w_down, b_down):
    n_rows = x_rows.shape[0]
    bm = EXPERT_ROWS
    n_blocks = n_rows // bm
    d_ff = w_gate.shape[2]

    blocks = jnp.arange(n_blocks, dtype=jnp.int32)
    is_first = jnp.concatenate([jnp.ones((1,), jnp.int32), (block_e[1:] != block_e[:-1]).astype(jnp.int32)])
    group = jnp.sum(jnp.where(blocks[None, :] <= blocks[:, None], is_first[None, :], 0), axis=1) - 1
    next_e = jnp.min(jnp.where(block_e[None, :] > block_e[:, None], block_e[None, :], N_EXPERTS), axis=1)
    next_e = jnp.where(next_e == N_EXPERTS, -1, next_e)

    def by_expert(shape):
        return pl.BlockSpec(shape, lambda i, be, *_: (be[i],) + (0,) * (len(shape) - 1))

    hbm = pl.BlockSpec(memory_space=pl.ANY)
    return pl.pallas_call(
        _expert_kernel,
        grid_spec=pltpu.PrefetchScalarGridSpec(
            num_scalar_prefetch=5,
            grid=(n_blocks,),
            in_specs=[
                pl.BlockSpec((bm, D_MODEL // 2), lambda i, *_: (i, 0)),
                hbm, by_expert((1, 1, d_ff)),
                hbm, by_expert((1, 1, d_ff)),
                hbm, by_expert((1, 1, D_MODEL)),
            ],
            out_specs=pl.BlockSpec((bm, D_MODEL // 2), lambda i, *_: (i, 0)),
            scratch_shapes=[
                pltpu.VMEM((2, D_MODEL, d_ff), jnp.float32),
                pltpu.VMEM((2, D_MODEL, d_ff), jnp.float32),
                pltpu.VMEM((2, d_ff, D_MODEL), jnp.float32),
                pltpu.VMEM((D_MODEL, d_ff), jnp.bfloat16),
                pltpu.VMEM((D_MODEL, d_ff), jnp.bfloat16),
                pltpu.VMEM((d_ff, D_MODEL), jnp.bfloat16),
                pltpu.SemaphoreType.DMA((2, 3)),
            ],
        ),
        out_shape=jax.ShapeDtypeStruct((n_rows, D_MODEL // 2), jnp.int32),
        compiler_params=pltpu.CompilerParams(
            dimension_semantics=("arbitrary",), vmem_limit_bytes=VMEM_LIMIT_BYTES),
        name="experts",
    )(block_e, n_valid, is_first, next_e, group % 2, x_rows, w_gate, b_gate.reshape(N_EXPERTS, 1, d_ff), w_up,
      b_up.reshape(N_EXPERTS, 1, d_ff), w_down, b_down.reshape(N_EXPERTS, 1, D_MODEL))


def _combine_kernel(h_ref, y_ref, meta_ref, g_ref, b_ref, o_ref):
    tm = h_ref.shape[0]
    meta_cols = jnp.concatenate([meta_ref[...], jnp.zeros((128 - META_ROWS, tm), jnp.float32)], axis=0).T
    ffn_hi, ffn_lo = 0.0, 0.0
    for k in range(TOP_K):
        gate = meta_cols[:, 2 * TOP_K + k:2 * TOP_K + k + 1]
        y_hi, y_lo = _unpack_bf16_pairs(y_ref[k])
        ffn_hi = ffn_hi + gate * y_hi
        ffn_lo = ffn_lo + gate * y_lo
    ffn = jnp.concatenate([ffn_hi, ffn_lo], axis=1)
    o_ref[...] = _layer_norm(DEEPNORM_ALPHA * h_ref[...] + ffn, g_ref[...], b_ref[...])


def _combine(h, y_tok, meta, ln2_g, ln2_b):
    t = h.shape[0]
    tm = COMBINE_ROWS
    return pl.pallas_call(
        _combine_kernel,
        grid=(t // tm,),
        in_specs=[
            pl.BlockSpec((tm, D_MODEL), lambda i: (i, 0)),
            pl.BlockSpec((TOP_K, tm, D_MODEL // 2), lambda i: (0, i, 0)),
            pl.BlockSpec((META_ROWS, tm), lambda i: (0, i)),
            pl.BlockSpec((1, D_MODEL), lambda i: (0, 0)),
            pl.BlockSpec((1, D_MODEL), lambda i: (0, 0)),
        ],
        out_specs=pl.BlockSpec((tm, D_MODEL), lambda i: (i, 0)),
        out_shape=jax.ShapeDtypeStruct((t, D_MODEL), jnp.float32),
        compiler_params=pltpu.CompilerParams(
            dimension_semantics=("arbitrary",), vmem_limit_bytes=VMEM_LIMIT_BYTES),
        name="combine",
    )(h, y_tok, meta, ln2_g, ln2_b)


def _layer(x2, seq_len, w_in, b_in, sinks, ln_v_g, ln_v_b, w_spatial, b_spatial, w_out, b_out,
           ln1_g, ln1_b, w_router, b_router, w_gate, b_gate, w_up, b_up, w_down, b_down, ln2_g, ln2_b):
    t = x2.shape[0]
    tk = t * TOP_K
    bm = EXPERT_ROWS
    bf16 = jnp.bfloat16

    w_r_hi = w_router.astype(bf16)
    w_r_lo = (w_router - w_r_hi.astype(jnp.float32)).astype(bf16)
    lane_pad = jnp.zeros((D_MODEL, ROUTER_LANES - 2 * N_EXPERTS), bf16)
    w_r = jnp.concatenate([w_r_hi, w_r_lo, lane_pad], axis=1)
    w_r_hi_only = jnp.concatenate([w_r_hi, jnp.zeros_like(w_r_lo), lane_pad], axis=1)
    b_sp_full = jnp.repeat(b_spatial.T, GMLP_GROUP_DIM, axis=1)

    h, h_packed, meta, counts = _mixer(
        x2, sinks, w_in.astype(bf16), b_in[None], ln_v_g[None], ln_v_b[None], w_spatial, b_sp_full,
        w_out.astype(bf16), b_out[None], ln1_g[None], ln1_b[None], w_r, w_r_hi_only, b_router[:, None], seq_len)

    counts = counts[:, 0].astype(jnp.int32)
    experts = jnp.arange(N_EXPERTS, dtype=jnp.int32)
    padded = (counts + bm - 1) // bm * bm
    padded_end = jnp.sum(jnp.where(experts[None, :] <= experts[:, None], padded[None, :], 0), axis=1)
    padded_start = padded_end - padded
    n_blocks = tk // bm + N_EXPERTS
    n_rows = n_blocks * bm
    top_idx_t = meta[:TOP_K].astype(jnp.int32)
    rank_t = meta[TOP_K:2 * TOP_K].astype(jnp.int32)
    dest_t = rank_t + jnp.sum(
        jnp.where(top_idx_t[None] == experts[:, None, None], padded_start[:, None, None], 0), axis=0)
    block_start = jnp.arange(n_blocks, dtype=jnp.int32) * bm
    block_e = jnp.minimum(
        jnp.sum((padded_end[None, :] <= block_start[:, None]).astype(jnp.int32), axis=1), N_EXPERTS - 1)
    valid_end = jnp.sum(jnp.where(block_e[:, None] == experts[None, :], (padded_start + counts)[None, :], 0), axis=1)
    n_valid = jnp.clip(valid_end - block_start, 0, bm)

    x_rows = _dispatch_rows(dest_t, h_packed, n_rows)
    y_rows = _experts(block_e, n_valid, x_rows, w_gate, b_gate, w_up, b_up, w_down, b_down)
    y_tok = _gather_rows(dest_t.reshape(-1), y_rows).reshape(TOP_K, t, D_MODEL // 2)
    return _combine(h, y_tok, meta, ln2_g[None], ln2_b[None])


def kernel(x, w_in, b_in, sinks, ln_v_g, ln_v_b, w_spatial, b_spatial, w_out, b_out, ln1_g, ln1_b,
           w_router, b_router, w_gate, b_gate, w_up, b_up, w_down, b_down, ln2_g, ln2_b):
    batch, seq_len, d = x.shape
    x2 = x.reshape(batch * seq_len, d)
    for l in range(DEPTH):
        x2 = _layer(x2, seq_len, w_in[l], b_in[l], sinks[l], ln_v_g[l], ln_v_b[l], w_spatial[l],
                    b_spatial[l], w_out[l], b_out[l], ln1_g[l], ln1_b[l], w_router[l], b_router[l],
                    w_gate[l], b_gate[l], w_up[l], b_up[l], w_down[l], b_down[l], ln2_g[l], ln2_b[l])
    return x2.reshape(batch, seq_len, d)
```

```python
import functools

import jax
import jax.numpy as jnp
from jax import lax
from jax.experimental import pallas as pl
from jax.experimental.pallas import tpu as pltpu
from jax.experimental.pallas import tpu_sc as plsc

D_MODEL = 1024
HEAD_DIM = 64
N_Q_HEADS = 8
N_KV_HEADS = 2
Q_REP = N_Q_HEADS // N_KV_HEADS
ATTN_WIDTH = N_Q_HEADS * HEAD_DIM
KV_WIDTH = N_KV_HEADS * HEAD_DIM
ATTN_BLOCK = 128
N_GMLP_GROUPS = 8
GMLP_WIDTH = D_MODEL - ATTN_WIDTH
GMLP_GROUP_DIM = GMLP_WIDTH // N_GMLP_GROUPS
IN_WIDTH = ATTN_WIDTH + 2 * KV_WIDTH + 2 * GMLP_WIDTH
N_EXPERTS = 32
TOP_K = 4
SWIGLU_LIMIT = 7.0
SWIGLU_ALPHA = 1.702
LN_EPS = 1e-5
DEPTH = 1
DEEPNORM_ALPHA = (2.0 * DEPTH) ** 0.25
NEG_INF = -1e30
LOG2_E = 1.4426950408889634

MIXER_ROWS = 512
EXPERT_ROWS = 512
EXPERT_PARTIAL_ROWS = (128, 256, 384)
DISPATCH_WINDOW = 64
GATHER_WINDOW = 32
GATHER_BUFFERS = 4
META_ROWS = 16
ROUTER_LANES = 128
COMBINE_ROWS = 1024
VMEM_LIMIT_BYTES = 56 * 1024 * 1024

_O_K = ATTN_WIDTH
_O_V = _O_K + KV_WIDTH
_O_U = _O_V + KV_WIDTH
_O_G = _O_U + GMLP_WIDTH


def _pack_bf16_pairs(v):
    n = v.shape[1] // 2
    hi = lax.bitcast_convert_type(v[:, :n].astype(jnp.float32), jnp.int32)
    lo = lax.bitcast_convert_type(v[:, n:].astype(jnp.float32), jnp.int32)
    return hi | lax.shift_right_logical(lo, 16)


def _unpack_bf16_pairs(p):
    hi = lax.bitcast_convert_type(p & jnp.int32(-65536), jnp.float32)
    lo = lax.bitcast_convert_type(lax.shift_left(p, 16), jnp.float32)
    return hi, lo


def _layer_norm(v, g, b):
    mu = jnp.mean(v, axis=-1, keepdims=True)
    vc = v - mu
    var = jnp.mean(vc * vc, axis=-1, keepdims=True)
    return vc * lax.rsqrt(var + LN_EPS) * g + b


def _attention_block(q, kb, vb, bias_ref, bias_sel):
    outs = []
    for g in range(N_KV_HEADS):
        kg = kb[:, g * HEAD_DIM:(g + 1) * HEAD_DIM]
        vg = vb[:, g * HEAD_DIM:(g + 1) * HEAD_DIM]
        qg = jnp.concatenate(
            [q[:, (g * Q_REP + r) * HEAD_DIM:(g * Q_REP + r + 1) * HEAD_DIM] for r in range(Q_REP)],
            axis=0).astype(jnp.bfloat16)
        s = (lax.dot_general(qg, kg, (((1,), (1,)), ((), ())), preferred_element_type=jnp.float32)
             + bias_ref[g, bias_sel])
        p = jnp.exp2(s - jnp.max(s, axis=-1, keepdims=True))
        denom = jnp.sum(p, axis=-1, keepdims=True)
        o = jnp.dot(p.astype(jnp.bfloat16), vg, preferred_element_type=jnp.float32) / denom
        outs.extend(o[r * ATTN_BLOCK:(r + 1) * ATTN_BLOCK] for r in range(Q_REP))
    return jnp.concatenate(outs, axis=-1)


def _mixer_kernel(x_ref, w_in_ref, b_in_ref, bias_ref, lnv_g_ref, lnv_b_ref, grp_avg_ref,
                  w_sp_ref, b_sp_ref, w_out_ref, b_out_ref, ln1_g_ref, ln1_b_ref,
                  w_r_ref, w_r_hi_ref, b_r_ref, tri_ref,
                  h_ref, hp_ref, meta_ref, count_ref,
                  kv_prev_ref, *, steps_per_seq):
    i = pl.program_id(0)
    first_step = (i % steps_per_seq) == 0
    tm = x_ref.shape[0]
    n_sub = tm // ATTN_BLOCK

    @pl.when(i == 0)
    def _():
        count_ref[...] = jnp.zeros_like(count_ref)

    @pl.when(first_step)
    def _():
        kv_prev_ref[...] = jnp.zeros_like(kv_prev_ref)

    x = x_ref[...]
    proj = jnp.dot(x.astype(jnp.bfloat16), w_in_ref[...], preferred_element_type=jnp.float32) + b_in_ref[...]

    q_all = proj[:, :_O_K] * (LOG2_E * HEAD_DIM ** -0.5)
    k_all = proj[:, _O_K:_O_V].astype(jnp.bfloat16)
    v_all = proj[:, _O_V:_O_U].astype(jnp.bfloat16)
    k_prev = kv_prev_ref[:, :KV_WIDTH]
    v_prev = kv_prev_ref[:, KV_WIDTH:]
    is_row0 = lax.broadcasted_iota(jnp.int32, (ATTN_BLOCK, KV_WIDTH), 0) == 0
    attn_blocks = []
    for sb in range(n_sub):
        rows = slice(sb * ATTN_BLOCK, (sb + 1) * ATTN_BLOCK)
        k_cur, v_cur = k_all[rows], v_all[rows]
        kb = jnp.concatenate([jnp.where(is_row0, 0, k_prev), k_cur], axis=0)
        vb = jnp.concatenate([jnp.where(is_row0, 0, v_prev), v_cur], axis=0)
        bias_sel = jnp.where(first_step, 1, 0) if sb == 0 else 0
        attn_blocks.append(_attention_block(q_all[rows], kb, vb, bias_ref, bias_sel))
        k_prev, v_prev = k_cur, v_cur
    kv_prev_ref[:, :KV_WIDTH] = k_prev
    kv_prev_ref[:, KV_WIDTH:] = v_prev
    attn = jnp.concatenate(attn_blocks, axis=0)

    u = jax.nn.gelu(proj[:, _O_U:_O_G])
    gg = jax.nn.gelu(proj[:, _O_G:])
    avg = grp_avg_ref[...]
    mu = jnp.dot(gg.astype(jnp.bfloat16), avg, preferred_element_type=jnp.float32)
    gc = gg - mu
    var = jnp.dot((gc * gc).astype(jnp.bfloat16), avg, preferred_element_type=jnp.float32)
    gn = (gc * lax.rsqrt(var + LN_EPS) * lnv_g_ref[...] + lnv_b_ref[...]).astype(jnp.bfloat16)
    causal = (lax.broadcasted_iota(jnp.int32, (ATTN_BLOCK, ATTN_BLOCK), 0)
              >= lax.broadcasted_iota(jnp.int32, (ATTN_BLOCK, ATTN_BLOCK), 1))
    w_sp = [jnp.where(causal, w_sp_ref[g], 0.0).astype(jnp.bfloat16) for g in range(N_GMLP_GROUPS)]
    mixed_chunks = []
    for c in range(n_sub):
        rows = slice(c * ATTN_BLOCK, (c + 1) * ATTN_BLOCK)
        pieces = [
            jnp.dot(w_sp[g], gn[rows, g * GMLP_GROUP_DIM:(g + 1) * GMLP_GROUP_DIM],
                    preferred_element_type=jnp.float32)
            for g in range(N_GMLP_GROUPS)]
        mixed_chunks.append(jnp.concatenate(pieces, axis=-1) + b_sp_ref[...])
    sgu = u * jnp.concatenate(mixed_chunks, axis=0)

    mix = (jnp.dot(attn.astype(jnp.bfloat16), w_out_ref[:ATTN_WIDTH, :], preferred_element_type=jnp.float32)
           + jnp.dot(sgu.astype(jnp.bfloat16), w_out_ref[ATTN_WIDTH:, :], preferred_element_type=jnp.float32)
           + b_out_ref[...])
    h = _layer_norm(DEEPNORM_ALPHA * x + mix, ln1_g_ref[...], ln1_b_ref[...])
    h_ref[...] = h

    h_hi = h.astype(jnp.bfloat16)
    hp_ref[...] = _pack_bf16_pairs(h_hi)
    h_lo = (h - h_hi.astype(jnp.float32)).astype(jnp.bfloat16)
    part = (jnp.dot(h_hi, w_r_ref[...], preferred_element_type=jnp.float32)
            + jnp.dot(h_lo, w_r_hi_ref[...], preferred_element_type=jnp.float32)).T
    logits = part[:N_EXPERTS] + part[N_EXPERTS:2 * N_EXPERTS] + b_r_ref[...]

    n_grp = N_EXPERTS // 8
    grp = [logits[8 * g:8 * (g + 1)] for g in range(n_grp)]
    sub = lax.broadcasted_iota(jnp.int32, (8, tm), 0)
    beaten = [jnp.zeros((8, tm), jnp.float32) for _ in range(n_grp)]
    for e2 in range(N_EXPERTS):
        g2, r2 = divmod(e2, 8)
        row = logits[e2:e2 + 1]
        for g in range(n_grp):
            if g > g2:
                wins = jnp.where(row >= grp[g], 1.0, 0.0)
            elif g < g2:
                wins = jnp.where(row > grp[g], 1.0, 0.0)
            else:
                wins = jnp.where(sub > r2, jnp.where(row >= grp[g], 1.0, 0.0), jnp.where(row > grp[g], 1.0, 0.0))
            beaten[g] = beaten[g] + wins
    place = jnp.concatenate(beaten, axis=0)
    expert_id = lax.broadcasted_iota(jnp.int32, (N_EXPERTS, tm), 0).astype(jnp.float32)
    onehot = jnp.where(place < TOP_K, 1.0, 0.0)
    before = jnp.dot(onehot.astype(jnp.bfloat16), tri_ref[...], preferred_element_type=jnp.float32) + count_ref[...]

    def pick(k, table):
        return jnp.sum(jnp.where(place == k, table, 0.0), axis=0, keepdims=True)

    vals = [pick(k, logits) for k in range(TOP_K)]
    exps = [jnp.exp(v - vals[0]) for v in vals]
    denom = exps[0] + exps[1] + exps[2] + exps[3]
    meta_ref[...] = jnp.zeros_like(meta_ref)
    for k in range(TOP_K):
        meta_ref[k:k + 1, :] = pick(k, expert_id)
        meta_ref[TOP_K + k:TOP_K + k + 1, :] = pick(k, before)
        meta_ref[2 * TOP_K + k:2 * TOP_K + k + 1, :] = exps[k] / denom
    count_ref[...] += jnp.sum(onehot, axis=1, keepdims=True)


def _score_bias(sinks):
    t_idx = jnp.arange(Q_REP * ATTN_BLOCK)[:, None] % ATTN_BLOCK
    s_idx = jnp.arange(2 * ATTN_BLOCK)[None, :]
    diff = t_idx + ATTN_BLOCK - s_idx
    band = (diff >= 0) & (diff < ATTN_BLOCK)
    masks = jnp.stack([band, band & (s_idx >= ATTN_BLOCK)])
    bias = jnp.where(masks, 0.0, NEG_INF).astype(jnp.float32)
    sink_rows = jnp.repeat(sinks.reshape(N_KV_HEADS, Q_REP) * LOG2_E, ATTN_BLOCK, axis=1)
    return jnp.where(s_idx == 0, sink_rows[:, None, :, None], bias[None])


def _mixer(x2, sinks, w_in, b_in, lnv_g, lnv_b, w_sp, b_sp_full, w_out, b_out, ln1_g, ln1_b,
           w_r, w_r_hi, b_r, seq_len):
    t = x2.shape[0]
    tm = MIXER_ROWS
    grp = jnp.arange(GMLP_WIDTH) // GMLP_GROUP_DIM
    grp_avg = jnp.where(grp[:, None] == grp[None, :], 1.0 / GMLP_GROUP_DIM, 0.0).astype(jnp.bfloat16)
    tri = (jnp.arange(tm)[:, None] < jnp.arange(tm)[None, :]).astype(jnp.bfloat16)

    def full(shape):
        return pl.BlockSpec(shape, lambda i: (0,) * len(shape))

    return pl.pallas_call(
        functools.partial(_mixer_kernel, steps_per_seq=seq_len // tm),
        grid=(t // tm,),
        in_specs=[
            pl.BlockSpec((tm, D_MODEL), lambda i: (i, 0)),
            full((D_MODEL, IN_WIDTH)), full((1, IN_WIDTH)),
            full((N_KV_HEADS, 2, Q_REP * ATTN_BLOCK, 2 * ATTN_BLOCK)),
            full((1, GMLP_WIDTH)), full((1, GMLP_WIDTH)), full((GMLP_WIDTH, GMLP_WIDTH)),
            full((N_GMLP_GROUPS, ATTN_BLOCK, ATTN_BLOCK)), full((ATTN_BLOCK, GMLP_WIDTH)),
            full((D_MODEL, D_MODEL)), full((1, D_MODEL)), full((1, D_MODEL)), full((1, D_MODEL)),
            full((D_MODEL, ROUTER_LANES)), full((D_MODEL, ROUTER_LANES)), full((N_EXPERTS, 1)), full((tm, tm)),
        ],
        out_specs=[
            pl.BlockSpec((tm, D_MODEL), lambda i: (i, 0)),
            pl.BlockSpec((tm, D_MODEL // 2), lambda i: (i, 0)),
            pl.BlockSpec((META_ROWS, tm), lambda i: (0, i)),
            pl.BlockSpec((N_EXPERTS, 1), lambda i: (0, 0)),
        ],
        out_shape=[
            jax.ShapeDtypeStruct((t, D_MODEL), jnp.float32),
            jax.ShapeDtypeStruct((t, D_MODEL // 2), jnp.int32),
            jax.ShapeDtypeStruct((META_ROWS, t), jnp.float32),
            jax.ShapeDtypeStruct((N_EXPERTS, 1), jnp.float32),
        ],
        scratch_shapes=[pltpu.VMEM((ATTN_BLOCK, 2 * KV_WIDTH), jnp.bfloat16)],
        compiler_params=pltpu.CompilerParams(
            dimension_semantics=("arbitrary",), vmem_limit_bytes=VMEM_LIMIT_BYTES),
        name="mixer",
    )(x2, w_in, b_in, _score_bias(sinks), lnv_g, lnv_b, grp_avg, w_sp, b_sp_full, w_out, b_out, ln1_g, ln1_b,
      w_r, w_r_hi, b_r, tri)


def _gather_rows(idx, src):
    n = idx.shape[0]
    width = src.shape[1]
    win, nbuf = GATHER_WINDOW, GATHER_BUFFERS
    sc = plsc.get_sparse_core_info()
    n_workers = sc.num_cores * sc.num_subcores
    per_worker = n // n_workers
    n_rounds = per_worker // (nbuf * win)
    assert n_rounds * nbuf * win * n_workers == n
    mesh = plsc.VectorSubcoreMesh(core_axis_name="core", subcore_axis_name="subcore")

    @functools.partial(
        pl.kernel, out_type=jax.ShapeDtypeStruct((n, width), src.dtype), mesh=mesh,
        scratch_types=[pltpu.VMEM((per_worker,), jnp.int32), pltpu.VMEM((nbuf, win, width), src.dtype),
                       pltpu.SemaphoreType.DMA((nbuf,)), pltpu.SemaphoreType.DMA((nbuf,))],
        name="gather_rows")
    def gather(src_hbm, idx_hbm, out_hbm, idx_v, rows_v, fetch_sem, store_sem):
        worker = lax.axis_index("subcore") * sc.num_cores + lax.axis_index("core")
        base = worker * per_worker
        pltpu.sync_copy(idx_hbm.at[pl.ds(base, per_worker)], idx_v)

        def fetch(chunk, buf):
            return pltpu.make_async_copy(src_hbm.at[idx_v.at[pl.ds(chunk * win, win)]], rows_v.at[buf],
                                         fetch_sem.at[buf])

        def store(chunk, buf):
            return pltpu.make_async_copy(rows_v.at[buf], out_hbm.at[pl.ds(base + chunk * win, win)],
                                         store_sem.at[buf])

        @pl.loop(0, n_rounds)
        def _(p):
            for buf in range(nbuf):
                @pl.when(p > 0)
                def _():
                    store(nbuf * (p - 1) + buf, buf).wait()
                fetch(nbuf * p + buf, buf).start()
            for buf in range(nbuf):
                fetch(nbuf * p + buf, buf).wait()
                store(nbuf * p + buf, buf).start()

        for buf in range(nbuf):
            store(nbuf * (n_rounds - 1) + buf, buf).wait()

    return gather(src, idx)


def _dispatch_rows(dest_t, src, n_rows):
    t, width = src.shape
    win = DISPATCH_WINDOW
    sc = plsc.get_sparse_core_info()
    n_workers = sc.num_cores * sc.num_subcores
    per_worker = t // n_workers
    n_chunks = per_worker // win
    n_pairs = n_chunks // 2
    assert n_pairs * 2 * win * n_workers == t
    idx = dest_t.reshape(TOP_K, n_workers, n_chunks, win).transpose(1, 0, 2, 3)
    idx = idx.reshape(n_workers, TOP_K * n_chunks, win)
    mesh = plsc.VectorSubcoreMesh(core_axis_name="core", subcore_axis_name="subcore")

    @functools.partial(
        pl.kernel, out_type=jax.ShapeDtypeStruct((n_rows, width), src.dtype), mesh=mesh,
        scratch_types=[pltpu.VMEM((TOP_K * n_chunks, win), jnp.int32), pltpu.VMEM((2, win, width), src.dtype),
                       pltpu.SemaphoreType.DMA((2,)), pltpu.SemaphoreType.DMA((2,))],
        name="dispatch_rows")
    def dispatch(src_hbm, idx_hbm, out_hbm, idx_v, rows_v, fetch_sem, store_sem):
        worker = lax.axis_index("subcore") * sc.num_cores + lax.axis_index("core")
        base = worker * per_worker
        pltpu.sync_copy(idx_hbm.at[worker], idx_v)

        def fetch(chunk, buf):
            return pltpu.make_async_copy(src_hbm.at[pl.ds(base + chunk * win, win)], rows_v.at[buf],
                                         fetch_sem.at[buf])

        def store(chunk, k, buf):
            return pltpu.make_async_copy(rows_v.at[buf], out_hbm.at[idx_v.at[k * n_chunks + chunk]],
                                         store_sem.at[buf])

        @pl.loop(0, n_pairs)
        def _(p):
            for buf in range(2):
                @pl.when(p > 0)
                def _():
                    for k in range(TOP_K):
                        store(2 * p - 2 + buf, k, buf).wait()
                fetch(2 * p + buf, buf).start()
            for buf in range(2):
                fetch(2 * p + buf, buf).wait()
                for k in range(TOP_K):
                    store(2 * p + buf, k, buf).start()

        for buf in range(2):
            for k in range(TOP_K):
                store(2 * n_pairs - 2 + buf, k, buf).wait()

    return dispatch(src, idx)


def _expert_kernel(block_e_ref, n_valid_ref, first_ref, next_e_ref, slot_ref,
                   x_ref, wg_hbm, bg_ref, wu_hbm, bu_ref, wd_hbm, bd_ref,
                   y_ref, wg_f32, wu_f32, wd_f32, wg_bf, wu_bf, wd_bf, sem):
    i = pl.program_id(0)
    n_valid = n_valid_ref[i]
    slot = slot_ref[i]
    staged = ((wg_hbm, wg_f32, wg_bf), (wu_hbm, wu_f32, wu_bf), (wd_hbm, wd_f32, wd_bf))

    def weight_copy(m, expert, s):
        return pltpu.make_async_copy(staged[m][0].at[expert], staged[m][1].at[s], sem.at[s, m])

    @pl.when(i == 0)
    def _():
        for m in range(3):
            weight_copy(m, block_e_ref[0], 0).start()

    @pl.when(first_ref[i] == 1)
    def _():
        @pl.when(next_e_ref[i] >= 0)
        def _():
            for m in range(3):
                weight_copy(m, next_e_ref[i], 1 - slot).start()

        for m in range(3):
            weight_copy(m, block_e_ref[i], slot).wait()
            staged[m][2][...] = staged[m][1][slot].astype(jnp.bfloat16)

    def expert_mlp(rows):
        row = lax.broadcasted_iota(jnp.int32, (rows, 1), 0)
        x_hi, x_lo = _unpack_bf16_pairs(jnp.where(row < n_valid, x_ref[:rows, :], 0))
        xb = jnp.concatenate([x_hi, x_lo], axis=1).astype(jnp.bfloat16)
        gt = jnp.minimum(jnp.dot(xb, wg_bf[...], preferred_element_type=jnp.float32) + bg_ref[0], SWIGLU_LIMIT)
        up = jnp.clip(jnp.dot(xb, wu_bf[...], preferred_element_type=jnp.float32) + bu_ref[0],
                      -SWIGLU_LIMIT, SWIGLU_LIMIT)
        hid = gt * jax.nn.sigmoid(SWIGLU_ALPHA * gt) * (up + 1.0)
        y = jnp.dot(hid.astype(jnp.bfloat16), wd_bf[...], preferred_element_type=jnp.float32) + bd_ref[0]
        y_ref[:rows, :] = _pack_bf16_pairs(y.astype(jnp.bfloat16))
        if rows < y_ref.shape[0]:
            y_ref[rows:, :] = jnp.zeros((y_ref.shape[0] - rows, y_ref.shape[1]), y_ref.dtype)

    row_options = (0,) + EXPERT_PARTIAL_ROWS + (x_ref.shape[0],)
    for lo, hi in zip(row_options[:-1], row_options[1:]):
        pl.when(jnp.logical_and(n_valid > lo, n_valid <= hi))(functools.partial(expert_mlp, hi))

    @pl.when(n_valid == 0)
    def _():
        y_ref[...] = jnp.zeros_like(y_ref)


def _experts(block_e, n_valid, x_rows, w_gate, b_gate, w_up, b_up, w_down, b_down):
    n_rows = x_rows.shape[0]
    bm = EXPERT_ROWS
    n_blocks = n_rows // bm
    d_ff = w_gate.shape[2]

    blocks = jnp.arange(n_blocks, dtype=jnp.int32)
    is_first = jnp.concatenate([jnp.ones((1,), jnp.int32), (block_e[1:] != block_e[:-1]).astype(jnp.int32)])
    group = jnp.sum(jnp.where(blocks[None, :] <= blocks[:, None], is_first[None, :], 0), axis=1) - 1
    next_e = jnp.min(jnp.where(block_e[None, :] > block_e[:, None], block_e[None, :], N_EXPERTS), axis=1)
    next_e = jnp.where(next_e == N_EXPERTS, -1, next_e)

    def by_expert(shape):
        return pl.BlockSpec(shape, lambda i, be, *_: (be[i],) + (0,) * (len(shape) - 1))

    hbm = pl.BlockSpec(memory_space=pl.ANY)
    return pl.pallas_call(
        _expert_kernel,
        grid_spec=pltpu.PrefetchScalarGridSpec(
            num_scalar_prefetch=5,
            grid=(n_blocks,),
            in_specs=[
                pl.BlockSpec((bm, D_MODEL // 2), lambda i, *_: (i, 0)),
                hbm, by_expert((1, 1, d_ff)),
                hbm, by_expert((1, 1, d_ff)),
                hbm, by_expert((1, 1, D_MODEL)),
            ],
            out_specs=pl.BlockSpec((bm, D_MODEL // 2), lambda i, *_: (i, 0)),
            scratch_shapes=[
                pltpu.VMEM((2, D_MODEL, d_ff), jnp.float32),
                pltpu.VMEM((2, D_MODEL, d_ff), jnp.float32),
                pltpu.VMEM((2, d_ff, D_MODEL), jnp.float32),
                pltpu.VMEM((D_MODEL, d_ff), jnp.bfloat16),
                pltpu.VMEM((D_MODEL, d_ff), jnp.bfloat16),
                pltpu.VMEM((d_ff, D_MODEL), jnp.bfloat16),
                pltpu.SemaphoreType.DMA((2, 3)),
            ],
        ),
        out_shape=jax.ShapeDtypeStruct((n_rows, D_MODEL // 2), jnp.int32),
        compiler_params=pltpu.CompilerParams(
            dimension_semantics=("arbitrary",), vmem_limit_bytes=VMEM_LIMIT_BYTES),
        name="experts",
    )(block_e, n_valid, is_first, next_e, group % 2, x_rows, w_gate, b_gate.reshape(N_EXPERTS, 1, d_ff), w_up,
      b_up.reshape(N_EXPERTS, 1, d_ff), w_down, b_down.reshape(N_EXPERTS, 1, D_MODEL))


def _combine_kernel(h_ref, y_ref, meta_ref, g_ref, b_ref, o_ref):
    tm = h_ref.shape[0]
    meta_cols = jnp.concatenate([meta_ref[...], jnp.zeros((128 - META_ROWS, tm), jnp.float32)], axis=0).T
    ffn_hi, ffn_lo = 0.0, 0.0
    for k in range(TOP_K):
        gate = meta_cols[:, 2 * TOP_K + k:2 * TOP_K + k + 1]
        y_hi, y_lo = _unpack_bf16_pairs(y_ref[k])
        ffn_hi = ffn_hi + gate * y_hi
        ffn_lo = ffn_lo + gate * y_lo
    ffn = jnp.concatenate([ffn_hi, ffn_lo], axis=1)
    o_ref[...] = _layer_norm(DEEPNORM_ALPHA * h_ref[...] + ffn, g_ref[...], b_ref[...])


def _combine(h, y_tok, meta, ln2_g, ln2_b):
    t = h.shape[0]
    tm = COMBINE_ROWS
    return pl.pallas_call(
        _combine_kernel,
        grid=(t // tm,),
        in_specs=[
            pl.BlockSpec((tm, D_MODEL), lambda i: (i, 0)),
            pl.BlockSpec((TOP_K, tm, D_MODEL // 2), lambda i: (0, i, 0)),
            pl.BlockSpec((META_ROWS, tm), lambda i: (0, i)),
            pl.BlockSpec((1, D_MODEL), lambda i: (0, 0)),
            pl.BlockSpec((1, D_MODEL), lambda i: (0, 0)),
        ],
        out_specs=pl.BlockSpec((tm, D_MODEL), lambda i: (i, 0)),
        out_shape=jax.ShapeDtypeStruct((t, D_MODEL), jnp.float32),
        compiler_params=pltpu.CompilerParams(
            dimension_semantics=("arbitrary",), vmem_limit_bytes=VMEM_LIMIT_BYTES),
        name="combine",
    )(h, y_tok, meta, ln2_g, ln2_b)


def _layer(x2, seq_len, w_in, b_in, sinks, ln_v_g, ln_v_b, w_spatial, b_spatial, w_out, b_out,
           ln1_g, ln1_b, w_router, b_router, w_gate, b_gate, w_up, b_up, w_down, b_down, ln2_g, ln2_b):
    t = x2.shape[0]
    tk = t * TOP_K
    bm = EXPERT_ROWS
    bf16 = jnp.bfloat16

    w_r_hi = w_router.astype(bf16)
    w_r_lo = (w_router - w_r_hi.astype(jnp.float32)).astype(bf16)
    lane_pad = jnp.zeros((D_MODEL, ROUTER_LANES - 2 * N_EXPERTS), bf16)
    w_r = jnp.concatenate([w_r_hi, w_r_lo, lane_pad], axis=1)
    w_r_hi_only = jnp.concatenate([w_r_hi, jnp.zeros_like(w_r_lo), lane_pad], axis=1)
    b_sp_full = jnp.repeat(b_spatial.T, GMLP_GROUP_DIM, axis=1)

    h, h_packed, meta, counts = _mixer(
        x2, sinks, w_in.astype(bf16), b_in[None], ln_v_g[None], ln_v_b[None], w_spatial, b_sp_full,
        w_out.astype(bf16), b_out[None], ln1_g[None], ln1_b[None], w_r, w_r_hi_only, b_router[:, None], seq_len)

    counts = counts[:, 0].astype(jnp.int32)
    experts = jnp.arange(N_EXPERTS, dtype=jnp.int32)
    padded = (counts + bm - 1) // bm * bm
    padded_end = jnp.sum(jnp.where(experts[None, :] <= experts[:, None], padded[None, :], 0), axis=1)
    padded_start = padded_end - padded
    n_blocks = tk // bm + N_EXPERTS
    n_rows = n_blocks * bm
    top_idx_t = meta[:TOP_K].astype(jnp.int32)
    rank_t = meta[TOP_K:2 * TOP_K].astype(jnp.int32)
    dest_t = rank_t + jnp.sum(
        jnp.where(top_idx_t[None] == experts[:, None, None], padded_start[:, None, None], 0), axis=0)
    block_start = jnp.arange(n_blocks, dtype=jnp.int32) * bm
    block_e = jnp.minimum(
        jnp.sum((padded_end[None, :] <= block_start[:, None]).astype(jnp.int32), axis=1), N_EXPERTS - 1)
    valid_end = jnp.sum(jnp.where(block_e[:, None] == experts[None, :], (padded_start + counts)[None, :], 0), axis=1)
    n_valid = jnp.clip(valid_end - block_start, 0, bm)

    x_rows = _dispatch_rows(dest_t, h_packed, n_rows)
    y_rows = _experts(block_e, n_valid, x_rows, w_gate, b_gate, w_up, b_up, w_down, b_down)
    y_tok = _gather_rows(dest_t.reshape(-1), y_rows).reshape(TOP_K, t, D_MODEL // 2)
    return _combine(h, y_tok, meta, ln2_g[None], ln2_b[None])


def kernel(x, w_in, b_in, sinks, ln_v_g, ln_v_b, w_spatial, b_spatial, w_out, b_out, ln1_g, ln1_b,
           w_router, b_router, w_gate, b_gate, w_up, b_up, w_down, b_down, ln2_g, ln2_b):
    batch, seq_len, d = x.shape
    x2 = x.reshape(batch * seq_len, d)
    for l in range(DEPTH):
        x2 = _layer(x2, seq_len, w_in[l], b_in[l], sinks[l], ln_v_g[l], ln_v_b[l], w_spatial[l],
                    b_spatial[l], w_out[l], b_out[l], ln1_g[l], ln1_b[l], w_router[l], b_router[l],
                    w_gate[l], b_gate[l], w_up[l], b_up[l], w_down[l], b_down[l], ln2_g[l], ln2_b[l])
    return x2.reshape(batch, seq_len, d)
```

```python
import functools

import jax
import jax.numpy as jnp
from jax import lax
from jax.experimental import pallas as pl
from jax.experimental.pallas import tpu as pltpu
from jax.experimental.pallas import tpu_sc as plsc

D_MODEL = 1024
HEAD_DIM = 64
N_Q_HEADS = 8
N_KV_HEADS = 2
Q_REP = N_Q_HEADS // N_KV_HEADS
ATTN_WIDTH = N_Q_HEADS * HEAD_DIM
KV_WIDTH = N_KV_HEADS * HEAD_DIM
ATTN_BLOCK = 128
N_GMLP_GROUPS = 8
GMLP_WIDTH = D_MODEL - ATTN_WIDTH
GMLP_GROUP_DIM = GMLP_WIDTH // N_GMLP_GROUPS
IN_WIDTH = ATTN_WIDTH + 2 * KV_WIDTH + 2 * GMLP_WIDTH
N_EXPERTS = 32
TOP_K = 4
SWIGLU_LIMIT = 7.0
SWIGLU_ALPHA = 1.702
LN_EPS = 1e-5
DEPTH = 1
DEEPNORM_ALPHA = (2.0 * DEPTH) ** 0.25
NEG_INF = -1e30
LOG2_E = 1.4426950408889634

LANES = 128
SUBLANES = 8

MIXER_ROWS = 512
EXPERT_ROWS = 512
EXPERT_PARTIAL_ROWS = (128, 256, 384)
GATHER_WINDOW = 64
META_ROWS = 16
ROUTER_LANES = LANES
COMBINE_ROWS = 1024
VMEM_LIMIT_BYTES = 56 * 1024 * 1024

_O_K = ATTN_WIDTH
_O_V = _O_K + KV_WIDTH
_O_U = _O_V + KV_WIDTH
_O_G = _O_U + GMLP_WIDTH


def _pack_bf16_pairs(v):
    n = v.shape[1] // 2
    hi = lax.bitcast_convert_type(v[:, :n].astype(jnp.float32), jnp.int32)
    lo = lax.bitcast_convert_type(v[:, n:].astype(jnp.float32), jnp.int32)
    return hi | lax.shift_right_logical(lo, 16)


def _unpack_bf16_pairs(p):
    hi = lax.bitcast_convert_type(p & jnp.int32(-65536), jnp.float32)
    lo = lax.bitcast_convert_type(lax.shift_left(p, 16), jnp.float32)
    return hi, lo


def _layer_norm(v, g, b):
    mu = jnp.mean(v, axis=-1, keepdims=True)
    vc = v - mu
    var = jnp.mean(vc * vc, axis=-1, keepdims=True)
    return vc * lax.rsqrt(var + LN_EPS) * g + b


def _attention_block(q, kb, vb, bias_ref, bias_sel):
    outs = []
    for g in range(N_KV_HEADS):
        kg = kb[:, g * HEAD_DIM:(g + 1) * HEAD_DIM]
        vg = vb[:, g * HEAD_DIM:(g + 1) * HEAD_DIM]
        qg = jnp.concatenate(
            [q[:, (g * Q_REP + r) * HEAD_DIM:(g * Q_REP + r + 1) * HEAD_DIM] for r in range(Q_REP)],
            axis=0).astype(jnp.bfloat16)
        s = (lax.dot_general(qg, kg, (((1,), (1,)), ((), ())), preferred_element_type=jnp.float32)
             + bias_ref[g, bias_sel])
        p = jnp.exp2(s - jnp.max(s, axis=-1, keepdims=True))
        denom = jnp.sum(p, axis=-1, keepdims=True)
        o = jnp.dot(p.astype(jnp.bfloat16), vg, preferred_element_type=jnp.float32) / denom
        outs.extend(o[r * ATTN_BLOCK:(r + 1) * ATTN_BLOCK] for r in range(Q_REP))
    return jnp.concatenate(outs, axis=-1)


def _mixer_kernel(x_ref, w_in_ref, b_in_ref, bias_ref, lnv_g_ref, lnv_b_ref, grp_avg_ref,
                  w_sp_ref, b_sp_ref, w_out_ref, b_out_ref, ln1_g_ref, ln1_b_ref,
                  w_r_ref, w_r_hi_ref, b_r_ref, tri_ref,
                  h_ref, hp_ref, meta_ref, count_ref,
                  kv_prev_ref, *, steps_per_seq):
    i = pl.program_id(0)
    first_step = (i % steps_per_seq) == 0
    tm = x_ref.shape[0]
    n_sub = tm // ATTN_BLOCK

    @pl.when(i == 0)
    def _():
        count_ref[...] = jnp.zeros_like(count_ref)

    @pl.when(first_step)
    def _():
        kv_prev_ref[...] = jnp.zeros_like(kv_prev_ref)

    x = x_ref[...]
    proj = jnp.dot(x.astype(jnp.bfloat16), w_in_ref[...], preferred_element_type=jnp.float32) + b_in_ref[...]

    q_all = proj[:, :_O_K] * (LOG2_E * HEAD_DIM ** -0.5)
    k_all = proj[:, _O_K:_O_V].astype(jnp.bfloat16)
    v_all = proj[:, _O_V:_O_U].astype(jnp.bfloat16)
    k_prev = kv_prev_ref[:, :KV_WIDTH]
    v_prev = kv_prev_ref[:, KV_WIDTH:]
    is_row0 = lax.broadcasted_iota(jnp.int32, (ATTN_BLOCK, KV_WIDTH), 0) == 0
    attn_blocks = []
    for sb in range(n_sub):
        rows = slice(sb * ATTN_BLOCK, (sb + 1) * ATTN_BLOCK)
        k_cur, v_cur = k_all[rows], v_all[rows]
        kb = jnp.concatenate([jnp.where(is_row0, 0, k_prev), k_cur], axis=0)
        vb = jnp.concatenate([jnp.where(is_row0, 0, v_prev), v_cur], axis=0)
        bias_sel = jnp.where(first_step, 1, 0) if sb == 0 else 0
        attn_blocks.append(_attention_block(q_all[rows], kb, vb, bias_ref, bias_sel))
        k_prev, v_prev = k_cur, v_cur
    kv_prev_ref[:, :KV_WIDTH] = k_prev
    kv_prev_ref[:, KV_WIDTH:] = v_prev
    attn = jnp.concatenate(attn_blocks, axis=0)

    u = jax.nn.gelu(proj[:, _O_U:_O_G])
    gg = jax.nn.gelu(proj[:, _O_G:])
    avg = grp_avg_ref[...]
    mu = jnp.dot(gg.astype(jnp.bfloat16), avg, preferred_element_type=jnp.float32)
    gc = gg - mu
    var = jnp.dot((gc * gc).astype(jnp.bfloat16), avg, preferred_element_type=jnp.float32)
    gn = (gc * lax.rsqrt(var + LN_EPS) * lnv_g_ref[...] + lnv_b_ref[...]).astype(jnp.bfloat16)
    causal = (lax.broadcasted_iota(jnp.int32, (ATTN_BLOCK, ATTN_BLOCK), 0)
              >= lax.broadcasted_iota(jnp.int32, (ATTN_BLOCK, ATTN_BLOCK), 1))
    w_sp = [jnp.where(causal, w_sp_ref[g], 0.0).astype(jnp.bfloat16) for g in range(N_GMLP_GROUPS)]
    mixed_chunks = []
    for c in range(n_sub):
        rows = slice(c * ATTN_BLOCK, (c + 1) * ATTN_BLOCK)
        pieces = [
            jnp.dot(w_sp[g], gn[rows, g * GMLP_GROUP_DIM:(g + 1) * GMLP_GROUP_DIM],
                    preferred_element_type=jnp.float32)
            for g in range(N_GMLP_GROUPS)]
        mixed_chunks.append(jnp.concatenate(pieces, axis=-1) + b_sp_ref[...])
    sgu = u * jnp.concatenate(mixed_chunks, axis=0)

    mix = (jnp.dot(attn.astype(jnp.bfloat16), w_out_ref[:ATTN_WIDTH, :], preferred_element_type=jnp.float32)
           + jnp.dot(sgu.astype(jnp.bfloat16), w_out_ref[ATTN_WIDTH:, :], preferred_element_type=jnp.float32)
           + b_out_ref[...])
    h = _layer_norm(DEEPNORM_ALPHA * x + mix, ln1_g_ref[...], ln1_b_ref[...])
    h_ref[...] = h

    h_hi = h.astype(jnp.bfloat16)
    hp_ref[...] = _pack_bf16_pairs(h_hi)
    h_lo = (h - h_hi.astype(jnp.float32)).astype(jnp.bfloat16)
    part = (jnp.dot(h_hi, w_r_ref[...], preferred_element_type=jnp.float32)
            + jnp.dot(h_lo, w_r_hi_ref[...], preferred_element_type=jnp.float32)).T
    logits = part[:N_EXPERTS] + part[N_EXPERTS:2 * N_EXPERTS] + b_r_ref[...]

    n_grp = N_EXPERTS // SUBLANES
    grp = [logits[SUBLANES * g:SUBLANES * (g + 1)] for g in range(n_grp)]
    sub = lax.broadcasted_iota(jnp.int32, (SUBLANES, tm), 0)
    beaten = [jnp.zeros((SUBLANES, tm), jnp.float32) for _ in range(n_grp)]
    for e2 in range(N_EXPERTS):
        g2, r2 = divmod(e2, SUBLANES)
        row = logits[e2:e2 + 1]
        for g in range(n_grp):
            if g > g2:
                wins = jnp.where(row >= grp[g], 1.0, 0.0)
            elif g < g2:
                wins = jnp.where(row > grp[g], 1.0, 0.0)
            else:
                wins = jnp.where(sub > r2, jnp.where(row >= grp[g], 1.0, 0.0), jnp.where(row > grp[g], 1.0, 0.0))
            beaten[g] = beaten[g] + wins
    place = jnp.concatenate(beaten, axis=0)
    expert_id = lax.broadcasted_iota(jnp.int32, (N_EXPERTS, tm), 0).astype(jnp.float32)
    onehot = jnp.where(place < TOP_K, 1.0, 0.0)
    before = jnp.dot(onehot.astype(jnp.bfloat16), tri_ref[...], preferred_element_type=jnp.float32) + count_ref[...]

    def pick(k, table):
        return jnp.sum(jnp.where(place == k, table, 0.0), axis=0, keepdims=True)

    vals = [pick(k, logits) for k in range(TOP_K)]
    exps = [jnp.exp(v - vals[0]) for v in vals]
    denom = exps[0] + exps[1] + exps[2] + exps[3]
    meta_ref[...] = jnp.zeros_like(meta_ref)
    for k in range(TOP_K):
        meta_ref[k:k + 1, :] = pick(k, expert_id)
        meta_ref[TOP_K + k:TOP_K + k + 1, :] = pick(k, before)
        meta_ref[2 * TOP_K + k:2 * TOP_K + k + 1, :] = exps[k] / denom
    count_ref[...] += jnp.sum(onehot, axis=1, keepdims=True)


def _score_bias(sinks):
    t_idx = jnp.arange(Q_REP * ATTN_BLOCK)[:, None] % ATTN_BLOCK
    s_idx = jnp.arange(2 * ATTN_BLOCK)[None, :]
    diff = t_idx + ATTN_BLOCK - s_idx
    band = (diff >= 0) & (diff < ATTN_BLOCK)
    masks = jnp.stack([band, band & (s_idx >= ATTN_BLOCK)])
    bias = jnp.where(masks, 0.0, NEG_INF).astype(jnp.float32)
    sink_rows = jnp.repeat(sinks.reshape(N_KV_HEADS, Q_REP) * LOG2_E, ATTN_BLOCK, axis=1)
    return jnp.where(s_idx == 0, sink_rows[:, None, :, None], bias[None])


def _mixer(x2, sinks, w_in, b_in, lnv_g, lnv_b, w_sp, b_sp_full, w_out, b_out, ln1_g, ln1_b,
           w_r, w_r_hi, b_r, seq_len):
    t = x2.shape[0]
    tm = MIXER_ROWS
    grp = jnp.arange(GMLP_WIDTH) // GMLP_GROUP_DIM
    grp_avg = jnp.where(grp[:, None] == grp[None, :], 1.0 / GMLP_GROUP_DIM, 0.0).astype(jnp.bfloat16)
    tri = (jnp.arange(tm)[:, None] < jnp.arange(tm)[None, :]).astype(jnp.bfloat16)

    def full(shape):
        return pl.BlockSpec(shape, lambda i: (0,) * len(shape))

    return pl.pallas_call(
        functools.partial(_mixer_kernel, steps_per_seq=seq_len // tm),
        grid=(t // tm,),
        in_specs=[
            pl.BlockSpec((tm, D_MODEL), lambda i: (i, 0)),
            full((D_MODEL, IN_WIDTH)), full((1, IN_WIDTH)),
            full((N_KV_HEADS, 2, Q_REP * ATTN_BLOCK, 2 * ATTN_BLOCK)),
            full((1, GMLP_WIDTH)), full((1, GMLP_WIDTH)), full((GMLP_WIDTH, GMLP_WIDTH)),
            full((N_GMLP_GROUPS, ATTN_BLOCK, ATTN_BLOCK)), full((ATTN_BLOCK, GMLP_WIDTH)),
            full((D_MODEL, D_MODEL)), full((1, D_MODEL)), full((1, D_MODEL)), full((1, D_MODEL)),
            full((D_MODEL, ROUTER_LANES)), full((D_MODEL, ROUTER_LANES)), full((N_EXPERTS, 1)), full((tm, tm)),
        ],
        out_specs=[
            pl.BlockSpec((tm, D_MODEL), lambda i: (i, 0)),
            pl.BlockSpec((tm, D_MODEL // 2), lambda i: (i, 0)),
            pl.BlockSpec((META_ROWS, tm), lambda i: (0, i)),
            pl.BlockSpec((N_EXPERTS, 1), lambda i: (0, 0)),
        ],
        out_shape=[
            jax.ShapeDtypeStruct((t, D_MODEL), jnp.float32),
            jax.ShapeDtypeStruct((t, D_MODEL // 2), jnp.int32),
            jax.ShapeDtypeStruct((META_ROWS, t), jnp.float32),
            jax.ShapeDtypeStruct((N_EXPERTS, 1), jnp.float32),
        ],
        scratch_shapes=[pltpu.VMEM((ATTN_BLOCK, 2 * KV_WIDTH), jnp.bfloat16)],
        compiler_params=pltpu.CompilerParams(
            dimension_semantics=("arbitrary",), vmem_limit_bytes=VMEM_LIMIT_BYTES),
        name="mixer",
    )(x2, w_in, b_in, _score_bias(sinks), lnv_g, lnv_b, grp_avg, w_sp, b_sp_full, w_out, b_out, ln1_g, ln1_b,
      w_r, w_r_hi, b_r, tri)


def _gather_rows(idx, src):
    n = idx.shape[0]
    width = src.shape[1]
    win = GATHER_WINDOW
    sc = plsc.get_sparse_core_info()
    n_workers = sc.num_cores * sc.num_subcores
    per_worker = n // n_workers
    n_pairs = per_worker // (2 * win)
    assert n_pairs * 2 * win * n_workers == n
    mesh = plsc.VectorSubcoreMesh(core_axis_name="core", subcore_axis_name="subcore")

    @functools.partial(
        pl.kernel, out_type=jax.ShapeDtypeStruct((n, width), src.dtype), mesh=mesh,
        scratch_types=[pltpu.VMEM((per_worker,), jnp.int32), pltpu.VMEM((2, win, width), src.dtype),
                       pltpu.SemaphoreType.DMA((2,)), pltpu.SemaphoreType.DMA((2,))],
        name="gather_rows")
    def gather(src_hbm, idx_hbm, out_hbm, idx_v, rows_v, fetch_sem, store_sem):
        worker = lax.axis_index("subcore") * sc.num_cores + lax.axis_index("core")
        base = worker * per_worker
        pltpu.sync_copy(idx_hbm.at[pl.ds(base, per_worker)], idx_v)

        def fetch(chunk, buf):
            return pltpu.make_async_copy(src_hbm.at[idx_v.at[pl.ds(chunk * win, win)]], rows_v.at[buf],
                                         fetch_sem.at[buf])

        def store(chunk, buf):
            return pltpu.make_async_copy(rows_v.at[buf], out_hbm.at[pl.ds(base + chunk * win, win)],
                                         store_sem.at[buf])

        @pl.loop(0, n_pairs)
        def _(p):
            for buf in range(2):
                @pl.when(p > 0)
                def _():
                    store(2 * p - 2 + buf, buf).wait()
                fetch(2 * p + buf, buf).start()
            for buf in range(2):
                fetch(2 * p + buf, buf).wait()
                store(2 * p + buf, buf).start()

        for buf in range(2):
            store(2 * n_pairs - 2 + buf, buf).wait()

    return gather(src, idx)


def _dispatch_rows(dest_t, src, n_rows):
    t, width = src.shape
    win = GATHER_WINDOW
    sc = plsc.get_sparse_core_info()
    n_workers = sc.num_cores * sc.num_subcores
    per_worker = t // n_workers
    n_chunks = per_worker // win
    n_pairs = n_chunks // 2
    assert n_pairs * 2 * win * n_workers == t
    idx = dest_t.reshape(TOP_K, n_workers, n_chunks, win).transpose(1, 0, 2, 3)
    idx = idx.reshape(n_workers, TOP_K * n_chunks, win)
    mesh = plsc.VectorSubcoreMesh(core_axis_name="core", subcore_axis_name="subcore")

    @functools.partial(
        pl.kernel, out_type=jax.ShapeDtypeStruct((n_rows, width), src.dtype), mesh=mesh,
        scratch_types=[pltpu.VMEM((TOP_K * n_chunks, win), jnp.int32), pltpu.VMEM((2, win, width), src.dtype),
                       pltpu.SemaphoreType.DMA((2,)), pltpu.SemaphoreType.DMA((2,))],
        name="dispatch_rows")
    def dispatch(src_hbm, idx_hbm, out_hbm, idx_v, rows_v, fetch_sem, store_sem):
        worker = lax.axis_index("subcore") * sc.num_cores + lax.axis_index("core")
        base = worker * per_worker
        pltpu.sync_copy(idx_hbm.at[worker], idx_v)

        def fetch(chunk, buf):
            return pltpu.make_async_copy(src_hbm.at[pl.ds(base + chunk * win, win)], rows_v.at[buf],
                                         fetch_sem.at[buf])

        def store(chunk, k, buf):
            return pltpu.make_async_copy(rows_v.at[buf], out_hbm.at[idx_v.at[k * n_chunks + chunk]],
                                         store_sem.at[buf])

        @pl.loop(0, n_pairs)
        def _(p):
            for buf in range(2):
                @pl.when(p > 0)
                def _():
                    for k in range(TOP_K):
                        store(2 * p - 2 + buf, k, buf).wait()
                fetch(2 * p + buf, buf).start()
            for buf in range(2):
                fetch(2 * p + buf, buf).wait()
                for k in range(TOP_K):
                    store(2 * p + buf, k, buf).start()

        for buf in range(2):
            for k in range(TOP_K):
                store(2 * n_pairs - 2 + buf, k, buf).wait()

    return dispatch(src, idx)


def _expert_kernel(block_e_ref, n_valid_ref, first_ref, next_e_ref, slot_ref,
                   x_ref, wg_hbm, bg_ref, wu_hbm, bu_ref, wd_hbm, bd_ref,
                   y_ref, wg_f32, wu_f32, wd_f32, wg_bf, wu_bf, wd_bf, sem):
    i = pl.program_id(0)
    n_valid = n_valid_ref[i]
    slot = slot_ref[i]
    staged = ((wg_hbm, wg_f32, wg_bf), (wu_hbm, wu_f32, wu_bf), (wd_hbm, wd_f32, wd_bf))

    def weight_copy(m, expert, s):
        return pltpu.make_async_copy(staged[m][0].at[expert], staged[m][1].at[s], sem.at[s, m])

    @pl.when(i == 0)
    def _():
        for m in range(3):
            weight_copy(m, block_e_ref[0], 0).start()

    @pl.when(first_ref[i] == 1)
    def _():
        @pl.when(next_e_ref[i] >= 0)
        def _():
            for m in range(3):
                weight_copy(m, next_e_ref[i], 1 - slot).start()

        for m in range(3):
            weight_copy(m, block_e_ref[i], slot).wait()
            staged[m][2][...] = staged[m][1][slot].astype(jnp.bfloat16)

    def expert_mlp(rows):
        row = lax.broadcasted_iota(jnp.int32, (rows, 1), 0)
        x_hi, x_lo = _unpack_bf16_pairs(jnp.where(row < n_valid, x_ref[:rows, :], 0))
        xb = jnp.concatenate([x_hi, x_lo], axis=1).astype(jnp.bfloat16)
        gt = jnp.minimum(jnp.dot(xb, wg_bf[...], preferred_element_type=jnp.float32) + bg_ref[0], SWIGLU_LIMIT)
        up = jnp.clip(jnp.dot(xb, wu_bf[...], preferred_element_type=jnp.float32) + bu_ref[0],
                      -SWIGLU_LIMIT, SWIGLU_LIMIT)
        hid = gt * jax.nn.sigmoid(SWIGLU_ALPHA * gt) * (up + 1.0)
        y = jnp.dot(hid.astype(jnp.bfloat16), wd_bf[...], preferred_element_type=jnp.float32) + bd_ref[0]
        y_ref[:rows, :] = _pack_bf16_pairs(y.astype(jnp.bfloat16))
        if rows < y_ref.shape[0]:
            y_ref[rows:, :] = jnp.zeros((y_ref.shape[0] - rows, y_ref.shape[1]), y_ref.dtype)

    row_options = (0,) + EXPERT_PARTIAL_ROWS + (x_ref.shape[0],)
    for lo, hi in zip(row_options[:-1], row_options[1:]):
        pl.when(jnp.logical_and(n_valid > lo, n_valid <= hi))(functools.partial(expert_mlp, hi))

    @pl.when(n_valid == 0)
    def _():
        y_ref[...] = jnp.zeros_like(y_ref)


def _experts(block_e, n_valid, x_rows, w_gate, b_gate, w_up, b_up, w_down, b_down):
    n_rows = x_rows.shape[0]
    bm = EXPERT_ROWS
    n_blocks = n_rows // bm
    d_ff = w_gate.shape[2]

    blocks = jnp.arange(n_blocks, dtype=jnp.int32)
    is_first = jnp.concatenate([jnp.ones((1,), jnp.int32), (block_e[1:] != block_e[:-1]).astype(jnp.int32)])
    group = jnp.sum(jnp.where(blocks[None, :] <= blocks[:, None], is_first[None, :], 0), axis=1) - 1
    next_e = jnp.min(jnp.where(block_e[None, :] > block_e[:, None], block_e[None, :], N_EXPERTS), axis=1)
    next_e = jnp.where(next_e == N_EXPERTS, -1, next_e)

    def by_expert(shape):
        return pl.BlockSpec(shape, lambda i, be, *_: (be[i],) + (0,) * (len(shape) - 1))

    hbm = pl.BlockSpec(memory_space=pl.ANY)
    return pl.pallas_call(
        _expert_kernel,
        grid_spec=pltpu.PrefetchScalarGridSpec(
            num_scalar_prefetch=5,
            grid=(n_blocks,),
            in_specs=[
                pl.BlockSpec((bm, D_MODEL // 2), lambda i, *_: (i, 0)),
                hbm, by_expert((1, 1, d_ff)),
                hbm, by_expert((1, 1, d_ff)),
                hbm, by_expert((1, 1, D_MODEL)),
            ],
            out_specs=pl.BlockSpec((bm, D_MODEL // 2), lambda i, *_: (i, 0)),
            scratch_shapes=[
                pltpu.VMEM((2, D_MODEL, d_ff), jnp.float32),
                pltpu.VMEM((2, D_MODEL, d_ff), jnp.float32),
                pltpu.VMEM((2, d_ff, D_MODEL), jnp.float32),
                pltpu.VMEM((D_MODEL, d_ff), jnp.bfloat16),
                pltpu.VMEM((D_MODEL, d_ff), jnp.bfloat16),
                pltpu.VMEM((d_ff, D_MODEL), jnp.bfloat16),
                pltpu.SemaphoreType.DMA((2, 3)),
            ],
        ),
        out_shape=jax.ShapeDtypeStruct((n_rows, D_MODEL // 2), jnp.int32),
        compiler_params=pltpu.CompilerParams(
            dimension_semantics=("arbitrary",), vmem_limit_bytes=VMEM_LIMIT_BYTES),
        name="experts",
    )(block_e, n_valid, is_first, next_e, group % 2, x_rows, w_gate, b_gate.reshape(N_EXPERTS, 1, d_ff), w_up,
      b_up.reshape(N_EXPERTS, 1, d_ff), w_down, b_down.reshape(N_EXPERTS, 1, D_MODEL))


def _combine_kernel(h_ref, y_ref, meta_ref, g_ref, b_ref, o_ref):
    tm = h_ref.shape[0]
    meta_cols = jnp.concatenate([meta_ref[...], jnp.zeros((LANES - META_ROWS, tm), jnp.float32)], axis=0).T
    ffn_hi, ffn_lo = 0.0, 0.0
    for k in range(TOP_K):
        gate = meta_cols[:, 2 * TOP_K + k:2 * TOP_K + k + 1]
        y_hi, y_lo = _unpack_bf16_pairs(y_ref[k])
        ffn_hi = ffn_hi + gate * y_hi
        ffn_lo = ffn_lo + gate * y_lo
    ffn = jnp.concatenate([ffn_hi, ffn_lo], axis=1)
    o_ref[...] = _layer_norm(DEEPNORM_ALPHA * h_ref[...] + ffn, g_ref[...], b_ref[...])


def _combine(h, y_tok, meta, ln2_g, ln2_b):
    t = h.shape[0]
    tm = COMBINE_ROWS
    return pl.pallas_call(
        _combine_kernel,
        grid=(t // tm,),
        in_specs=[
            pl.BlockSpec((tm, D_MODEL), lambda i: (i, 0)),
            pl.BlockSpec((TOP_K, tm, D_MODEL // 2), lambda i: (0, i, 0)),
            pl.BlockSpec((META_ROWS, tm), lambda i: (0, i)),
            pl.BlockSpec((1, D_MODEL), lambda i: (0, 0)),
            pl.BlockSpec((1, D_MODEL), lambda i: (0, 0)),
        ],
        out_specs=pl.BlockSpec((tm, D_MODEL), lambda i: (i, 0)),
        out_shape=jax.ShapeDtypeStruct((t, D_MODEL), jnp.float32),
        compiler_params=pltpu.CompilerParams(
            dimension_semantics=("arbitrary",), vmem_limit_bytes=VMEM_LIMIT_BYTES),
        name="combine",
    )(h, y_tok, meta, ln2_g, ln2_b)


def _layer(x2, seq_len, w_in, b_in, sinks, ln_v_g, ln_v_b, w_spatial, b_spatial, w_out, b_out,
           ln1_g, ln1_b, w_router, b_router, w_gate, b_gate, w_up, b_up, w_down, b_down, ln2_g, ln2_b):
    t = x2.shape[0]
    tk = t * TOP_K
    bm = EXPERT_ROWS
    bf16 = jnp.bfloat16

    w_r_hi = w_router.astype(bf16)
    w_r_lo = (w_router - w_r_hi.astype(jnp.float32)).astype(bf16)
    lane_pad = jnp.zeros((D_MODEL, ROUTER_LANES - 2 * N_EXPERTS), bf16)
    w_r = jnp.concatenate([w_r_hi, w_r_lo, lane_pad], axis=1)
    w_r_hi_only = jnp.concatenate([w_r_hi, jnp.zeros_like(w_r_lo), lane_pad], axis=1)
    b_sp_full = jnp.repeat(b_spatial.T, GMLP_GROUP_DIM, axis=1)

    h, h_packed, meta, counts = _mixer(
        x2, sinks, w_in.astype(bf16), b_in[None], ln_v_g[None], ln_v_b[None], w_spatial, b_sp_full,
        w_out.astype(bf16), b_out[None], ln1_g[None], ln1_b[None], w_r, w_r_hi_only, b_router[:, None], seq_len)

    counts = counts[:, 0].astype(jnp.int32)
    experts = jnp.arange(N_EXPERTS, dtype=jnp.int32)
    padded = (counts + bm - 1) // bm * bm
    padded_end = jnp.sum(jnp.where(experts[None, :] <= experts[:, None], padded[None, :], 0), axis=1)
    padded_start = padded_end - padded
    n_blocks = tk // bm + N_EXPERTS
    n_rows = n_blocks * bm
    top_idx_t = meta[:TOP_K].astype(jnp.int32)
    rank_t = meta[TOP_K:2 * TOP_K].astype(jnp.int32)
    dest_t = rank_t + jnp.sum(
        jnp.where(top_idx_t[None] == experts[:, None, None], padded_start[:, None, None], 0), axis=0)
    block_start = jnp.arange(n_blocks, dtype=jnp.int32) * bm
    block_e = jnp.minimum(
        jnp.sum((padded_end[None, :] <= block_start[:, None]).astype(jnp.int32), axis=1), N_EXPERTS - 1)
    valid_end = jnp.sum(jnp.where(block_e[:, None] == experts[None, :], (padded_start + counts)[None, :], 0), axis=1)
    n_valid = jnp.clip(valid_end - block_start, 0, bm)

    x_rows = _dispatch_rows(dest_t, h_packed, n_rows)
    y_rows = _experts(block_e, n_valid, x_rows, w_gate, b_gate, w_up, b_up, w_down, b_down)
    y_tok = _gather_rows(dest_t.reshape(-1), y_rows).reshape(TOP_K, t, D_MODEL // 2)
    return _combine(h, y_tok, meta, ln2_g[None], ln2_b[None])


def kernel(x, w_in, b_in, sinks, ln_v_g, ln_v_b, w_spatial, b_spatial, w_out, b_out, ln1_g, ln1_b,
           w_router, b_router, w_gate, b_gate, w_up, b_up, w_down, b_down, ln2_g, ln2_b):
    batch, seq_len, d = x.shape
    x2 = x.reshape(batch * seq_len, d)
    for l in range(DEPTH):
        x2 = _layer(x2, seq_len, w_in[l], b_in[l], sinks[l], ln_v_g[l], ln_v_b[l], w_spatial[l],
                    b_spatial[l], w_out[l], b_out[l], ln1_g[l], ln1_b[l], w_router[l], b_router[l],
                    w_gate[l], b_gate[l], w_up[l], b_up[l], w_down[l], b_down[l], ln2_g[l], ln2_b[l])
    return x2.reshape(batch, seq_len, d)
```

```python
import functools

import jax
import jax.numpy as jnp
from jax import lax
from jax.experimental import pallas as pl
from jax.experimental.pallas import tpu as pltpu
from jax.experimental.pallas import tpu_sc as plsc

D_MODEL = 1024
HEAD_DIM = 64
N_Q_HEADS = 8
N_KV_HEADS = 2
Q_REP = N_Q_HEADS // N_KV_HEADS
ATTN_WIDTH = N_Q_HEADS * HEAD_DIM
KV_WIDTH = N_KV_HEADS * HEAD_DIM
ATTN_BLOCK = 128
N_GMLP_GROUPS = 8
GMLP_WIDTH = D_MODEL - ATTN_WIDTH
GMLP_GROUP_DIM = GMLP_WIDTH // N_GMLP_GROUPS
IN_WIDTH = ATTN_WIDTH + 2 * KV_WIDTH + 2 * GMLP_WIDTH
N_EXPERTS = 32
TOP_K = 4
SWIGLU_LIMIT = 7.0
SWIGLU_ALPHA = 1.702
LN_EPS = 1e-5
DEPTH = 1
DEEPNORM_ALPHA = (2.0 * DEPTH) ** 0.25
NEG_INF = -1e30
LOG2_E = 1.4426950408889634

LANES = 128
SUBLANES = 8

MIXER_ROWS = 512
EXPERT_ROWS = 512
EXPERT_PARTIAL_ROWS = (128, 256, 384)
GATHER_WINDOW = 64
META_ROWS = 16
ROUTER_LANES = LANES
COMBINE_ROWS = 1024
VMEM_LIMIT_BYTES = 56 * 1024 * 1024

_O_K = ATTN_WIDTH
_O_V = _O_K + KV_WIDTH
_O_U = _O_V + KV_WIDTH
_O_G = _O_U + GMLP_WIDTH


def _pack_bf16_pairs(v):
    n = v.shape[1] // 2
    hi = lax.bitcast_convert_type(v[:, :n].astype(jnp.float32), jnp.int32)
    lo = lax.bitcast_convert_type(v[:, n:].astype(jnp.float32), jnp.int32)
    return hi | lax.shift_right_logical(lo, 16)


def _unpack_bf16_pairs(p):
    hi = lax.bitcast_convert_type(p & jnp.int32(-65536), jnp.float32)
    lo = lax.bitcast_convert_type(lax.shift_left(p, 16), jnp.float32)
    return hi, lo


def _layer_norm(v, g, b):
    mu = jnp.mean(v, axis=-1, keepdims=True)
    vc = v - mu
    var = jnp.mean(vc * vc, axis=-1, keepdims=True)
    return vc * lax.rsqrt(var + LN_EPS) * g + b


def _attention_block(q, kb, vb, bias_ref, bias_sel):
    outs = []
    for g in range(N_KV_HEADS):
        kg = kb[:, g * HEAD_DIM:(g + 1) * HEAD_DIM]
        vg = vb[:, g * HEAD_DIM:(g + 1) * HEAD_DIM]
        qg = jnp.concatenate(
            [q[:, (g * Q_REP + r) * HEAD_DIM:(g * Q_REP + r + 1) * HEAD_DIM] for r in range(Q_REP)],
            axis=0).astype(jnp.bfloat16)
        s = (lax.dot_general(qg, kg, (((1,), (1,)), ((), ())), preferred_element_type=jnp.float32)
             + bias_ref[g, bias_sel])
        p = jnp.exp2(s - jnp.max(s, axis=-1, keepdims=True))
        denom = jnp.sum(p, axis=-1, keepdims=True)
        o = jnp.dot(p.astype(jnp.bfloat16), vg, preferred_element_type=jnp.float32) / denom
        outs.extend(o[r * ATTN_BLOCK:(r + 1) * ATTN_BLOCK] for r in range(Q_REP))
    return jnp.concatenate(outs, axis=-1)


def _mixer_kernel(x_ref, w_in_ref, b_in_ref, bias_ref, lnv_g_ref, lnv_b_ref, grp_avg_ref,
                  w_sp_ref, b_sp_ref, w_out_ref, b_out_ref, ln1_g_ref, ln1_b_ref,
                  w_r_ref, w_r_hi_ref, b_r_ref, tri_ref,
                  h_ref, hp_ref, meta_ref, count_ref,
                  kv_prev_ref, *, steps_per_seq):
    i = pl.program_id(0)
    first_step = (i % steps_per_seq) == 0
    tm = x_ref.shape[0]
    n_sub = tm // ATTN_BLOCK

    @pl.when(i == 0)
    def _():
        count_ref[...] = jnp.zeros_like(count_ref)

    @pl.when(first_step)
    def _():
        kv_prev_ref[...] = jnp.zeros_like(kv_prev_ref)

    x = x_ref[...]
    proj = jnp.dot(x.astype(jnp.bfloat16), w_in_ref[...], preferred_element_type=jnp.float32) + b_in_ref[...]

    q_all = proj[:, :_O_K] * (LOG2_E * HEAD_DIM ** -0.5)
    k_all = proj[:, _O_K:_O_V].astype(jnp.bfloat16)
    v_all = proj[:, _O_V:_O_U].astype(jnp.bfloat16)
    k_prev = kv_prev_ref[:, :KV_WIDTH]
    v_prev = kv_prev_ref[:, KV_WIDTH:]
    is_row0 = lax.broadcasted_iota(jnp.int32, (ATTN_BLOCK, KV_WIDTH), 0) == 0
    attn_blocks = []
    for sb in range(n_sub):
        rows = slice(sb * ATTN_BLOCK, (sb + 1) * ATTN_BLOCK)
        k_cur, v_cur = k_all[rows], v_all[rows]
        kb = jnp.concatenate([jnp.where(is_row0, 0, k_prev), k_cur], axis=0)
        vb = jnp.concatenate([jnp.where(is_row0, 0, v_prev), v_cur], axis=0)
        bias_sel = jnp.where(first_step, 1, 0) if sb == 0 else 0
        attn_blocks.append(_attention_block(q_all[rows], kb, vb, bias_ref, bias_sel))
        k_prev, v_prev = k_cur, v_cur
    kv_prev_ref[:, :KV_WIDTH] = k_prev
    kv_prev_ref[:, KV_WIDTH:] = v_prev
    attn = jnp.concatenate(attn_blocks, axis=0)

    u = jax.nn.gelu(proj[:, _O_U:_O_G])
    gg = jax.nn.gelu(proj[:, _O_G:])
    avg = grp_avg_ref[...]
    mu = jnp.dot(gg.astype(jnp.bfloat16), avg, preferred_element_type=jnp.float32)
    gc = gg - mu
    var = jnp.dot((gc * gc).astype(jnp.bfloat16), avg, preferred_element_type=jnp.float32)
    gn = (gc * lax.rsqrt(var + LN_EPS) * lnv_g_ref[...] + lnv_b_ref[...]).astype(jnp.bfloat16)
    causal = (lax.broadcasted_iota(jnp.int32, (ATTN_BLOCK, ATTN_BLOCK), 0)
              >= lax.broadcasted_iota(jnp.int32, (ATTN_BLOCK, ATTN_BLOCK), 1))
    w_sp = [jnp.where(causal, w_sp_ref[g], 0.0).astype(jnp.bfloat16) for g in range(N_GMLP_GROUPS)]
    mixed_chunks = []
    for c in range(n_sub):
        rows = slice(c * ATTN_BLOCK, (c + 1) * ATTN_BLOCK)
        pieces = [
            jnp.dot(w_sp[g], gn[rows, g * GMLP_GROUP_DIM:(g + 1) * GMLP_GROUP_DIM],
                    preferred_element_type=jnp.float32)
            for g in range(N_GMLP_GROUPS)]
        mixed_chunks.append(jnp.concatenate(pieces, axis=-1) + b_sp_ref[...])
    sgu = u * jnp.concatenate(mixed_chunks, axis=0)

    mix = (jnp.dot(attn.astype(jnp.bfloat16), w_out_ref[:ATTN_WIDTH, :], preferred_element_type=jnp.float32)
           + jnp.dot(sgu.astype(jnp.bfloat16), w_out_ref[ATTN_WIDTH:, :], preferred_element_type=jnp.float32)
           + b_out_ref[...])
    h = _layer_norm(DEEPNORM_ALPHA * x + mix, ln1_g_ref[...], ln1_b_ref[...])
    h_ref[...] = h

    h_hi = h.astype(jnp.bfloat16)
    hp_ref[...] = _pack_bf16_pairs(h_hi)
    h_lo = (h - h_hi.astype(jnp.float32)).astype(jnp.bfloat16)
    part = (jnp.dot(h_hi, w_r_ref[...], preferred_element_type=jnp.float32)
            + jnp.dot(h_lo, w_r_hi_ref[...], preferred_element_type=jnp.float32)).T
    logits = part[:N_EXPERTS] + part[N_EXPERTS:2 * N_EXPERTS] + b_r_ref[...]

    n_grp = N_EXPERTS // SUBLANES
    grp = [logits[SUBLANES * g:SUBLANES * (g + 1)] for g in range(n_grp)]
    sub = lax.broadcasted_iota(jnp.int32, (SUBLANES, tm), 0)
    beaten = [jnp.zeros((SUBLANES, tm), jnp.float32) for _ in range(n_grp)]
    for e2 in range(N_EXPERTS):
        g2, r2 = divmod(e2, SUBLANES)
        row = logits[e2:e2 + 1]
        for g in range(n_grp):
            if g > g2:
                wins = jnp.where(row >= grp[g], 1.0, 0.0)
            elif g < g2:
                wins = jnp.where(row > grp[g], 1.0, 0.0)
            else:
                wins = jnp.where(sub > r2, jnp.where(row >= grp[g], 1.0, 0.0), jnp.where(row > grp[g], 1.0, 0.0))
            beaten[g] = beaten[g] + wins
    place = jnp.concatenate(beaten, axis=0)
    expert_id = lax.broadcasted_iota(jnp.int32, (N_EXPERTS, tm), 0).astype(jnp.float32)
    onehot = jnp.where(place < TOP_K, 1.0, 0.0)
    before = jnp.dot(onehot.astype(jnp.bfloat16), tri_ref[...], preferred_element_type=jnp.float32) + count_ref[...]

    def pick(k, table):
        return jnp.sum(jnp.where(place == k, table, 0.0), axis=0, keepdims=True)

    vals = [pick(k, logits) for k in range(TOP_K)]
    exps = [jnp.exp(v - vals[0]) for v in vals]
    denom = exps[0] + exps[1] + exps[2] + exps[3]
    meta_ref[...] = jnp.zeros_like(meta_ref)
    for k in range(TOP_K):
        meta_ref[k:k + 1, :] = pick(k, expert_id)
        meta_ref[TOP_K + k:TOP_K + k + 1, :] = pick(k, before)
        meta_ref[2 * TOP_K + k:2 * TOP_K + k + 1, :] = exps[k] / denom
    count_ref[...] += jnp.sum(onehot, axis=1, keepdims=True)


def _score_bias(sinks):
    t_idx = jnp.arange(Q_REP * ATTN_BLOCK)[:, None] % ATTN_BLOCK
    s_idx = jnp.arange(2 * ATTN_BLOCK)[None, :]
    diff = t_idx + ATTN_BLOCK - s_idx
    band = (diff >= 0) & (diff < ATTN_BLOCK)
    masks = jnp.stack([band, band & (s_idx >= ATTN_BLOCK)])
    bias = jnp.where(masks, 0.0, NEG_INF).astype(jnp.float32)
    sink_rows = jnp.repeat(sinks.reshape(N_KV_HEADS, Q_REP) * LOG2_E, ATTN_BLOCK, axis=1)
    return jnp.where(s_idx == 0, sink_rows[:, None, :, None], bias[None])


def _mixer(x2, sinks, w_in, b_in, lnv_g, lnv_b, w_sp, b_sp_full, w_out, b_out, ln1_g, ln1_b,
           w_r, w_r_hi, b_r, seq_len):
    t = x2.shape[0]
    tm = MIXER_ROWS
    grp = jnp.arange(GMLP_WIDTH) // GMLP_GROUP_DIM
    grp_avg = jnp.where(grp[:, None] == grp[None, :], 1.0 / GMLP_GROUP_DIM, 0.0).astype(jnp.bfloat16)
    tri = (jnp.arange(tm)[:, None] < jnp.arange(tm)[None, :]).astype(jnp.bfloat16)

    def full(shape):
        return pl.BlockSpec(shape, lambda i: (0,) * len(shape))

    return pl.pallas_call(
        functools.partial(_mixer_kernel, steps_per_seq=seq_len // tm),
        grid=(t // tm,),
        in_specs=[
            pl.BlockSpec((tm, D_MODEL), lambda i: (i, 0)),
            full((D_MODEL, IN_WIDTH)), full((1, IN_WIDTH)),
            full((N_KV_HEADS, 2, Q_REP * ATTN_BLOCK, 2 * ATTN_BLOCK)),
            full((1, GMLP_WIDTH)), full((1, GMLP_WIDTH)), full((GMLP_WIDTH, GMLP_WIDTH)),
            full((N_GMLP_GROUPS, ATTN_BLOCK, ATTN_BLOCK)), full((ATTN_BLOCK, GMLP_WIDTH)),
            full((D_MODEL, D_MODEL)), full((1, D_MODEL)), full((1, D_MODEL)), full((1, D_MODEL)),
            full((D_MODEL, ROUTER_LANES)), full((D_MODEL, ROUTER_LANES)), full((N_EXPERTS, 1)), full((tm, tm)),
        ],
        out_specs=[
            pl.BlockSpec((tm, D_MODEL), lambda i: (i, 0)),
            pl.BlockSpec((tm, D_MODEL // 2), lambda i: (i, 0)),
            pl.BlockSpec((META_ROWS, tm), lambda i: (0, i)),
            pl.BlockSpec((N_EXPERTS, 1), lambda i: (0, 0)),
        ],
        out_shape=[
            jax.ShapeDtypeStruct((t, D_MODEL), jnp.float32),
            jax.ShapeDtypeStruct((t, D_MODEL // 2), jnp.int32),
            jax.ShapeDtypeStruct((META_ROWS, t), jnp.float32),
            jax.ShapeDtypeStruct((N_EXPERTS, 1), jnp.float32),
        ],
        scratch_shapes=[pltpu.VMEM((ATTN_BLOCK, 2 * KV_WIDTH), jnp.bfloat16)],
        compiler_params=pltpu.CompilerParams(
            dimension_semantics=("arbitrary",), vmem_limit_bytes=VMEM_LIMIT_BYTES),
        name="mixer",
    )(x2, w_in, b_in, _score_bias(sinks), lnv_g, lnv_b, grp_avg, w_sp, b_sp_full, w_out, b_out, ln1_g, ln1_b,
      w_r, w_r_hi, b_r, tri)


def _gather_rows(idx, src):
    n = idx.shape[0]
    width = src.shape[1]
    win = GATHER_WINDOW
    sc = plsc.get_sparse_core_info()
    n_workers = sc.num_cores * sc.num_subcores
    per_worker = n // n_workers
    n_pairs = per_worker // (2 * win)
    assert n_pairs * 2 * win * n_workers == n
    mesh = plsc.VectorSubcoreMesh(core_axis_name="core", subcore_axis_name="subcore")

    @functools.partial(
        pl.kernel, out_type=jax.ShapeDtypeStruct((n, width), src.dtype), mesh=mesh,
        scratch_types=[pltpu.VMEM((per_worker,), jnp.int32), pltpu.VMEM((2, win, width), src.dtype),
                       pltpu.SemaphoreType.DMA((2,)), pltpu.SemaphoreType.DMA((2,))],
        name="gather_rows")
    def gather(src_hbm, idx_hbm, out_hbm, idx_v, rows_v, fetch_sem, store_sem):
        worker = lax.axis_index("subcore") * sc.num_cores + lax.axis_index("core")
        base = worker * per_worker
        pltpu.sync_copy(idx_hbm.at[pl.ds(base, per_worker)], idx_v)

        def fetch(chunk, buf):
            return pltpu.make_async_copy(src_hbm.at[idx_v.at[pl.ds(chunk * win, win)]], rows_v.at[buf],
                                         fetch_sem.at[buf])

        def store(chunk, buf):
            return pltpu.make_async_copy(rows_v.at[buf], out_hbm.at[pl.ds(base + chunk * win, win)],
                                         store_sem.at[buf])

        @pl.loop(0, n_pairs)
        def _(p):
            for buf in range(2):
                @pl.when(p > 0)
                def _():
                    store(2 * p - 2 + buf, buf).wait()
                fetch(2 * p + buf, buf).start()
            for buf in range(2):
                fetch(2 * p + buf, buf).wait()
                store(2 * p + buf, buf).start()

        for buf in range(2):
            store(2 * n_pairs - 2 + buf, buf).wait()

    return gather(src, idx)


def _dispatch_rows(dest_t, src, n_rows):
    t, width = src.shape
    win = GATHER_WINDOW
    sc = plsc.get_sparse_core_info()
    n_workers = sc.num_cores * sc.num_subcores
    per_worker = t // n_workers
    n_chunks = per_worker // win
    n_pairs = n_chunks // 2
    assert n_pairs * 2 * win * n_workers == t
    idx = dest_t.reshape(TOP_K, n_workers, n_chunks, win).transpose(1, 0, 2, 3)
    idx = idx.reshape(n_workers, TOP_K * n_chunks, win)
    mesh = plsc.VectorSubcoreMesh(core_axis_name="core", subcore_axis_name="subcore")

    @functools.partial(
        pl.kernel, out_type=jax.ShapeDtypeStruct((n_rows, width), src.dtype), mesh=mesh,
        scratch_types=[pltpu.VMEM((TOP_K * n_chunks, win), jnp.int32), pltpu.VMEM((2, win, width), src.dtype),
                       pltpu.SemaphoreType.DMA((2,)), pltpu.SemaphoreType.DMA((2,))],
        name="dispatch_rows")
    def dispatch(src_hbm, idx_hbm, out_hbm, idx_v, rows_v, fetch_sem, store_sem):
        worker = lax.axis_index("subcore") * sc.num_cores + lax.axis_index("core")
        base = worker * per_worker
        pltpu.sync_copy(idx_hbm.at[worker], idx_v)

        def fetch(chunk, buf):
            return pltpu.make_async_copy(src_hbm.at[pl.ds(base + chunk * win, win)], rows_v.at[buf],
                                         fetch_sem.at[buf])

        def store(chunk, k, buf):
            return pltpu.make_async_copy(rows_v.at[buf], out_hbm.at[idx_v.at[k * n_chunks + chunk]],
                                         store_sem.at[buf])

        @pl.loop(0, n_pairs)
        def _(p):
            for buf in range(2):
                @pl.when(p > 0)
                def _():
                    for k in range(TOP_K):
                        store(2 * p - 2 + buf, k, buf).wait()
                fetch(2 * p + buf, buf).start()
            for buf in range(2):
                fetch(2 * p + buf, buf).wait()
                for k in range(TOP_K):
                    store(2 * p + buf, k, buf).start()

        for buf in range(2):
            for k in range(TOP_K):
                store(2 * n_pairs - 2 + buf, k, buf).wait()

    return dispatch(src, idx)


def _expert_kernel(block_e_ref, n_valid_ref, first_ref, next_e_ref, slot_ref,
                   x_ref, wg_hbm, bg_ref, wu_hbm, bu_ref, wd_hbm, bd_ref,
                   y_ref, wg_f32, wu_f32, wd_f32, wg_bf, wu_bf, wd_bf, sem):
    i = pl.program_id(0)
    n_valid = n_valid_ref[i]
    slot = slot_ref[i]
    staged = ((wg_hbm, wg_f32, wg_bf), (wu_hbm, wu_f32, wu_bf), (wd_hbm, wd_f32, wd_bf))

    def weight_copy(m, expert, s):
        return pltpu.make_async_copy(staged[m][0].at[expert], staged[m][1].at[s], sem.at[s, m])

    @pl.when(i == 0)
    def _():
        for m in range(3):
            weight_copy(m, block_e_ref[0], 0).start()

    @pl.when(first_ref[i] == 1)
    def _():
        @pl.when(next_e_ref[i] >= 0)
        def _():
            for m in range(3):
                weight_copy(m, next_e_ref[i], 1 - slot).start()

        for m in range(3):
            weight_copy(m, block_e_ref[i], slot).wait()
            staged[m][2][...] = staged[m][1][slot].astype(jnp.bfloat16)

    def expert_mlp(rows):
        row = lax.broadcasted_iota(jnp.int32, (rows, 1), 0)
        x_hi, x_lo = _unpack_bf16_pairs(jnp.where(row < n_valid, x_ref[:rows, :], 0))
        xb = jnp.concatenate([x_hi, x_lo], axis=1).astype(jnp.bfloat16)
        gt = jnp.minimum(jnp.dot(xb, wg_bf[...], preferred_element_type=jnp.float32) + bg_ref[0], SWIGLU_LIMIT)
        up = jnp.clip(jnp.dot(xb, wu_bf[...], preferred_element_type=jnp.float32) + bu_ref[0],
                      -SWIGLU_LIMIT, SWIGLU_LIMIT)
        hid = gt * jax.nn.sigmoid(SWIGLU_ALPHA * gt) * (up + 1.0)
        y = jnp.dot(hid.astype(jnp.bfloat16), wd_bf[...], preferred_element_type=jnp.float32) + bd_ref[0]
        y_ref[:rows, :] = _pack_bf16_pairs(y.astype(jnp.bfloat16))
        if rows < y_ref.shape[0]:
            y_ref[rows:, :] = jnp.zeros((y_ref.shape[0] - rows, y_ref.shape[1]), y_ref.dtype)

    row_options = (0,) + EXPERT_PARTIAL_ROWS + (x_ref.shape[0],)
    for lo, hi in zip(row_options[:-1], row_options[1:]):
        pl.when(jnp.logical_and(n_valid > lo, n_valid <= hi))(functools.partial(expert_mlp, hi))

    @pl.when(n_valid == 0)
    def _():
        y_ref[...] = jnp.zeros_like(y_ref)


def _experts(block_e, n_valid, x_rows, w_gate, b_gate, w_up, b_up, w_down, b_down):
    n_rows = x_rows.shape[0]
    bm = EXPERT_ROWS
    n_blocks = n_rows // bm
    d_ff = w_gate.shape[2]

    blocks = jnp.arange(n_blocks, dtype=jnp.int32)
    is_first = jnp.concatenate([jnp.ones((1,), jnp.int32), (block_e[1:] != block_e[:-1]).astype(jnp.int32)])
    group = jnp.sum(jnp.where(blocks[None, :] <= blocks[:, None], is_first[None, :], 0), axis=1) - 1
    next_e = jnp.min(jnp.where(block_e[None, :] > block_e[:, None], block_e[None, :], N_EXPERTS), axis=1)
    next_e = jnp.where(next_e == N_EXPERTS, -1, next_e)

    def by_expert(shape):
        return pl.BlockSpec(shape, lambda i, be, *_: (be[i],) + (0,) * (len(shape) - 1))

    hbm = pl.BlockSpec(memory_space=pl.ANY)
    return pl.pallas_call(
        _expert_kernel,
        grid_spec=pltpu.PrefetchScalarGridSpec(
            num_scalar_prefetch=5,
            grid=(n_blocks,),
            in_specs=[
                pl.BlockSpec((bm, D_MODEL // 2), lambda i, *_: (i, 0)),
                hbm, by_expert((1, 1, d_ff)),
                hbm, by_expert((1, 1, d_ff)),
                hbm, by_expert((1, 1, D_MODEL)),
            ],
            out_specs=pl.BlockSpec((bm, D_MODEL // 2), lambda i, *_: (i, 0)),
            scratch_shapes=[
                pltpu.VMEM((2, D_MODEL, d_ff), jnp.float32),
                pltpu.VMEM((2, D_MODEL, d_ff), jnp.float32),
                pltpu.VMEM((2, d_ff, D_MODEL), jnp.float32),
                pltpu.VMEM((D_MODEL, d_ff), jnp.bfloat16),
                pltpu.VMEM((D_MODEL, d_ff), jnp.bfloat16),
                pltpu.VMEM((d_ff, D_MODEL), jnp.bfloat16),
                pltpu.SemaphoreType.DMA((2, 3)),
            ],
        ),
        out_shape=jax.ShapeDtypeStruct((n_rows, D_MODEL // 2), jnp.int32),
        compiler_params=pltpu.CompilerParams(
            dimension_semantics=("arbitrary",), vmem_limit_bytes=VMEM_LIMIT_BYTES),
        name="experts",
    )(block_e, n_valid, is_first, next_e, group % 2, x_rows, w_gate, b_gate.reshape(N_EXPERTS, 1, d_ff), w_up,
      b_up.reshape(N_EXPERTS, 1, d_ff), w_down, b_down.reshape(N_EXPERTS, 1, D_MODEL))


def _combine_kernel(h_ref, y_ref, meta_ref, g_ref, b_ref, *rest):
    o_ref = rest[-1]
    tm = h_ref.shape[0]
    meta_cols = jnp.concatenate([meta_ref[...], jnp.zeros((LANES - META_ROWS, tm), jnp.float32)], axis=0).T
    ffn_hi, ffn_lo = 0.0, 0.0
    for k in range(TOP_K):
        gate = meta_cols[:, 2 * TOP_K + k:2 * TOP_K + k + 1]
        y_hi, y_lo = _unpack_bf16_pairs(y_ref[k])
        ffn_hi = ffn_hi + gate * y_hi
        ffn_lo = ffn_lo + gate * y_lo
    ffn = jnp.concatenate([ffn_hi, ffn_lo], axis=1)
    o_ref[...] = _layer_norm(DEEPNORM_ALPHA * h_ref[...] + ffn, g_ref[...], b_ref[...])


def _combine(h, y_tok, meta, ln2_g, ln2_b, group, n_groups, out_prev):
    t = h.shape[0]
    tm = COMBINE_ROWS
    steps = t // tm
    in_specs = [
        pl.BlockSpec((tm, D_MODEL), lambda i: (i, 0)),
        pl.BlockSpec((TOP_K, tm, D_MODEL // 2), lambda i: (0, i, 0)),
        pl.BlockSpec((META_ROWS, tm), lambda i: (0, i)),
        pl.BlockSpec((1, D_MODEL), lambda i: (0, 0)),
        pl.BlockSpec((1, D_MODEL), lambda i: (0, 0)),
    ]
    args = [h, y_tok, meta, ln2_g, ln2_b]
    aliases = {}
    if out_prev is not None:
        in_specs.append(pl.BlockSpec(memory_space=pl.ANY))
        aliases = {len(args): 0}
        args.append(out_prev)
    return pl.pallas_call(
        _combine_kernel,
        grid=(steps,),
        in_specs=in_specs,
        out_specs=pl.BlockSpec((tm, D_MODEL), lambda i: (group * steps + i, 0)),
        out_shape=jax.ShapeDtypeStruct((n_groups * t, D_MODEL), jnp.float32),
        input_output_aliases=aliases,
        compiler_params=pltpu.CompilerParams(
            dimension_semantics=("arbitrary",), vmem_limit_bytes=VMEM_LIMIT_BYTES),
        name="combine",
    )(*args)


def _prepare_params(w_in, b_in, ln_v_g, ln_v_b, b_spatial, w_out, b_out, ln1_g, ln1_b, w_router, b_router):
    bf16 = jnp.bfloat16
    w_r_hi = w_router.astype(bf16)
    w_r_lo = (w_router - w_r_hi.astype(jnp.float32)).astype(bf16)
    lane_pad = jnp.zeros((D_MODEL, ROUTER_LANES - 2 * N_EXPERTS), bf16)
    w_r = jnp.concatenate([w_r_hi, w_r_lo, lane_pad], axis=1)
    w_r_hi_only = jnp.concatenate([w_r_hi, jnp.zeros_like(w_r_lo), lane_pad], axis=1)
    b_sp_full = jnp.repeat(b_spatial.T, GMLP_GROUP_DIM, axis=1)
    return dict(w_in=w_in.astype(bf16), b_in=b_in[None], lnv_g=ln_v_g[None], lnv_b=ln_v_b[None],
                b_sp_full=b_sp_full, w_out=w_out.astype(bf16), b_out=b_out[None], ln1_g=ln1_g[None],
                ln1_b=ln1_b[None], w_r=w_r, w_r_hi=w_r_hi_only, b_r=b_router[:, None])


def _token_group(x2, group, n_groups, out_prev, prm, sinks, w_spatial,
                 w_gate, b_gate, w_up, b_up, w_down, b_down, ln2_g, ln2_b):
    t = x2.shape[0]
    tk = t * TOP_K
    bm = EXPERT_ROWS

    h, h_packed, meta, counts = _mixer(
        x2, sinks, prm["w_in"], prm["b_in"], prm["lnv_g"], prm["lnv_b"], w_spatial, prm["b_sp_full"],
        prm["w_out"], prm["b_out"], prm["ln1_g"], prm["ln1_b"], prm["w_r"], prm["w_r_hi"], prm["b_r"], t)

    counts = counts[:, 0].astype(jnp.int32)
    experts = jnp.arange(N_EXPERTS, dtype=jnp.int32)
    padded = (counts + bm - 1) // bm * bm
    padded_end = jnp.sum(jnp.where(experts[None, :] <= experts[:, None], padded[None, :], 0), axis=1)
    padded_start = padded_end - padded
    n_blocks = tk // bm + N_EXPERTS
    n_rows = n_blocks * bm
    top_idx_t = meta[:TOP_K].astype(jnp.int32)
    rank_t = meta[TOP_K:2 * TOP_K].astype(jnp.int32)
    dest_t = rank_t + jnp.sum(
        jnp.where(top_idx_t[None] == experts[:, None, None], padded_start[:, None, None], 0), axis=0)
    block_start = jnp.arange(n_blocks, dtype=jnp.int32) * bm
    block_e = jnp.minimum(
        jnp.sum((padded_end[None, :] <= block_start[:, None]).astype(jnp.int32), axis=1), N_EXPERTS - 1)
    valid_end = jnp.sum(jnp.where(block_e[:, None] == experts[None, :], (padded_start + counts)[None, :], 0), axis=1)
    n_valid = jnp.clip(valid_end - block_start, 0, bm)

    x_rows = _dispatch_rows(dest_t, h_packed, n_rows)
    y_rows = _experts(block_e, n_valid, x_rows, w_gate, b_gate, w_up, b_up, w_down, b_down)
    y_tok = _gather_rows(dest_t.reshape(-1), y_rows).reshape(TOP_K, t, D_MODEL // 2)
    return _combine(h, y_tok, meta, ln2_g[None], ln2_b[None], group, n_groups, out_prev)


def kernel(x, w_in, b_in, sinks, ln_v_g, ln_v_b, w_spatial, b_spatial, w_out, b_out, ln1_g, ln1_b,
           w_router, b_router, w_gate, b_gate, w_up, b_up, w_down, b_down, ln2_g, ln2_b):
    batch, seq_len, d = x.shape
    for l in range(DEPTH):
        prm = _prepare_params(w_in[l], b_in[l], ln_v_g[l], ln_v_b[l], b_spatial[l], w_out[l], b_out[l],
                              ln1_g[l], ln1_b[l], w_router[l], b_router[l])
        out = None
        for b in range(batch):
            out = _token_group(x[b], b, batch, out, prm, sinks[l], w_spatial[l], w_gate[l], b_gate[l],
                               w_up[l], b_up[l], w_down[l], b_down[l], ln2_g[l], ln2_b[l])
        x = out.reshape(batch, seq_len, d)
    return x
```

```python
import functools

import jax
import jax.numpy as jnp
from jax import lax
from jax.experimental import pallas as pl
from jax.experimental.pallas import tpu as pltpu
from jax.experimental.pallas import tpu_sc as plsc

D_MODEL = 1024
HEAD_DIM = 64
N_Q_HEADS = 8
N_KV_HEADS = 2
Q_REP = N_Q_HEADS // N_KV_HEADS
ATTN_WIDTH = N_Q_HEADS * HEAD_DIM
KV_WIDTH = N_KV_HEADS * HEAD_DIM
ATTN_BLOCK = 128
N_GMLP_GROUPS = 8
GMLP_WIDTH = D_MODEL - ATTN_WIDTH
GMLP_GROUP_DIM = GMLP_WIDTH // N_GMLP_GROUPS
IN_WIDTH = ATTN_WIDTH + 2 * KV_WIDTH + 2 * GMLP_WIDTH
N_EXPERTS = 32
TOP_K = 4
SWIGLU_LIMIT = 7.0
SWIGLU_ALPHA = 1.702
LN_EPS = 1e-5
DEPTH = 1
DEEPNORM_ALPHA = (2.0 * DEPTH) ** 0.25
NEG_INF = -1e30
LOG2_E = 1.4426950408889634

LANES = 128
SUBLANES = 8

MIXER_ROWS = 512
EXPERT_ROWS = 1024
EXPERT_PARTIAL_ROWS = (128, 256, 384, 512, 768)
GATHER_WINDOW = 64
META_ROWS = 16
ROUTER_LANES = LANES
COMBINE_ROWS = 1024
VMEM_LIMIT_BYTES = 56 * 1024 * 1024

_O_K = ATTN_WIDTH
_O_V = _O_K + KV_WIDTH
_O_U = _O_V + KV_WIDTH
_O_G = _O_U + GMLP_WIDTH


def _pack_bf16_pairs(v):
    n = v.shape[1] // 2
    hi = lax.bitcast_convert_type(v[:, :n].astype(jnp.float32), jnp.int32)
    lo = lax.bitcast_convert_type(v[:, n:].astype(jnp.float32), jnp.int32)
    return hi | lax.shift_right_logical(lo, 16)


def _unpack_bf16_pairs(p):
    hi = lax.bitcast_convert_type(p & jnp.int32(-65536), jnp.float32)
    lo = lax.bitcast_convert_type(lax.shift_left(p, 16), jnp.float32)
    return hi, lo


def _layer_norm(v, g, b):
    mu = jnp.mean(v, axis=-1, keepdims=True)
    vc = v - mu
    var = jnp.mean(vc * vc, axis=-1, keepdims=True)
    return vc * lax.rsqrt(var + LN_EPS) * g + b


def _attention_block(q, kb, vb, bias_ref, bias_sel):
    outs = []
    for g in range(N_KV_HEADS):
        kg = kb[:, g * HEAD_DIM:(g + 1) * HEAD_DIM]
        vg = vb[:, g * HEAD_DIM:(g + 1) * HEAD_DIM]
        qg = jnp.concatenate(
            [q[:, (g * Q_REP + r) * HEAD_DIM:(g * Q_REP + r + 1) * HEAD_DIM] for r in range(Q_REP)],
            axis=0).astype(jnp.bfloat16)
        s = (lax.dot_general(qg, kg, (((1,), (1,)), ((), ())), preferred_element_type=jnp.float32)
             + bias_ref[g, bias_sel])
        p = jnp.exp2(s - jnp.max(s, axis=-1, keepdims=True))
        denom = jnp.sum(p, axis=-1, keepdims=True)
        o = jnp.dot(p.astype(jnp.bfloat16), vg, preferred_element_type=jnp.float32) / denom
        outs.extend(o[r * ATTN_BLOCK:(r + 1) * ATTN_BLOCK] for r in range(Q_REP))
    return jnp.concatenate(outs, axis=-1)


def _mixer_kernel(x_ref, w_in_ref, b_in_ref, bias_ref, lnv_g_ref, lnv_b_ref, grp_avg_ref,
                  w_sp_ref, b_sp_ref, w_out_ref, b_out_ref, ln1_g_ref, ln1_b_ref,
                  w_r_ref, w_r_hi_ref, b_r_ref, tri_ref,
                  h_ref, hp_ref, meta_ref, count_ref,
                  kv_prev_ref, *, steps_per_seq):
    i = pl.program_id(0)
    first_step = (i % steps_per_seq) == 0
    tm = x_ref.shape[0]
    n_sub = tm // ATTN_BLOCK

    @pl.when(i == 0)
    def _():
        count_ref[...] = jnp.zeros_like(count_ref)

    @pl.when(first_step)
    def _():
        kv_prev_ref[...] = jnp.zeros_like(kv_prev_ref)

    x = x_ref[...]
    proj = jnp.dot(x.astype(jnp.bfloat16), w_in_ref[...], preferred_element_type=jnp.float32) + b_in_ref[...]

    q_all = proj[:, :_O_K] * (LOG2_E * HEAD_DIM ** -0.5)
    k_all = proj[:, _O_K:_O_V].astype(jnp.bfloat16)
    v_all = proj[:, _O_V:_O_U].astype(jnp.bfloat16)
    k_prev = kv_prev_ref[:, :KV_WIDTH]
    v_prev = kv_prev_ref[:, KV_WIDTH:]
    is_row0 = lax.broadcasted_iota(jnp.int32, (ATTN_BLOCK, KV_WIDTH), 0) == 0
    attn_blocks = []
    for sb in range(n_sub):
        rows = slice(sb * ATTN_BLOCK, (sb + 1) * ATTN_BLOCK)
        k_cur, v_cur = k_all[rows], v_all[rows]
        kb = jnp.concatenate([jnp.where(is_row0, 0, k_prev), k_cur], axis=0)
        vb = jnp.concatenate([jnp.where(is_row0, 0, v_prev), v_cur], axis=0)
        bias_sel = jnp.where(first_step, 1, 0) if sb == 0 else 0
        attn_blocks.append(_attention_block(q_all[rows], kb, vb, bias_ref, bias_sel))
        k_prev, v_prev = k_cur, v_cur
    kv_prev_ref[:, :KV_WIDTH] = k_prev
    kv_prev_ref[:, KV_WIDTH:] = v_prev
    attn = jnp.concatenate(attn_blocks, axis=0)

    u = jax.nn.gelu(proj[:, _O_U:_O_G])
    gg = jax.nn.gelu(proj[:, _O_G:])
    avg = grp_avg_ref[...]
    mu = jnp.dot(gg.astype(jnp.bfloat16), avg, preferred_element_type=jnp.float32)
    gc = gg - mu
    var = jnp.dot((gc * gc).astype(jnp.bfloat16), avg, preferred_element_type=jnp.float32)
    gn = (gc * lax.rsqrt(var + LN_EPS) * lnv_g_ref[...] + lnv_b_ref[...]).astype(jnp.bfloat16)
    causal = (lax.broadcasted_iota(jnp.int32, (ATTN_BLOCK, ATTN_BLOCK), 0)
              >= lax.broadcasted_iota(jnp.int32, (ATTN_BLOCK, ATTN_BLOCK), 1))
    w_sp = [jnp.where(causal, w_sp_ref[g], 0.0).astype(jnp.bfloat16) for g in range(N_GMLP_GROUPS)]
    mixed_chunks = []
    for c in range(n_sub):
        rows = slice(c * ATTN_BLOCK, (c + 1) * ATTN_BLOCK)
        pieces = [
            jnp.dot(w_sp[g], gn[rows, g * GMLP_GROUP_DIM:(g + 1) * GMLP_GROUP_DIM],
                    preferred_element_type=jnp.float32)
            for g in range(N_GMLP_GROUPS)]
        mixed_chunks.append(jnp.concatenate(pieces, axis=-1) + b_sp_ref[...])
    sgu = u * jnp.concatenate(mixed_chunks, axis=0)

    mix = (jnp.dot(attn.astype(jnp.bfloat16), w_out_ref[:ATTN_WIDTH, :], preferred_element_type=jnp.float32)
           + jnp.dot(sgu.astype(jnp.bfloat16), w_out_ref[ATTN_WIDTH:, :], preferred_element_type=jnp.float32)
           + b_out_ref[...])
    h = _layer_norm(DEEPNORM_ALPHA * x + mix, ln1_g_ref[...], ln1_b_ref[...])
    h_ref[...] = h

    h_hi = h.astype(jnp.bfloat16)
    hp_ref[...] = _pack_bf16_pairs(h_hi)
    h_lo = (h - h_hi.astype(jnp.float32)).astype(jnp.bfloat16)
    part = (jnp.dot(h_hi, w_r_ref[...], preferred_element_type=jnp.float32)
            + jnp.dot(h_lo, w_r_hi_ref[...], preferred_element_type=jnp.float32)).T
    logits = part[:N_EXPERTS] + part[N_EXPERTS:2 * N_EXPERTS] + b_r_ref[...]

    n_grp = N_EXPERTS // SUBLANES
    grp = [logits[SUBLANES * g:SUBLANES * (g + 1)] for g in range(n_grp)]
    sub = lax.broadcasted_iota(jnp.int32, (SUBLANES, tm), 0)
    beaten = [jnp.zeros((SUBLANES, tm), jnp.float32) for _ in range(n_grp)]
    for e2 in range(N_EXPERTS):
        g2, r2 = divmod(e2, SUBLANES)
        row = logits[e2:e2 + 1]
        for g in range(n_grp):
            if g > g2:
                wins = jnp.where(row >= grp[g], 1.0, 0.0)
            elif g < g2:
                wins = jnp.where(row > grp[g], 1.0, 0.0)
            else:
                wins = jnp.where(sub > r2, jnp.where(row >= grp[g], 1.0, 0.0), jnp.where(row > grp[g], 1.0, 0.0))
            beaten[g] = beaten[g] + wins
    place = jnp.concatenate(beaten, axis=0)
    expert_id = lax.broadcasted_iota(jnp.int32, (N_EXPERTS, tm), 0).astype(jnp.float32)
    onehot = jnp.where(place < TOP_K, 1.0, 0.0)
    before = jnp.dot(onehot.astype(jnp.bfloat16), tri_ref[...], preferred_element_type=jnp.float32) + count_ref[...]

    def pick(k, table):
        return jnp.sum(jnp.where(place == k, table, 0.0), axis=0, keepdims=True)

    vals = [pick(k, logits) for k in range(TOP_K)]
    exps = [jnp.exp(v - vals[0]) for v in vals]
    denom = exps[0] + exps[1] + exps[2] + exps[3]
    meta_ref[...] = jnp.zeros_like(meta_ref)
    for k in range(TOP_K):
        meta_ref[k:k + 1, :] = pick(k, expert_id)
        meta_ref[TOP_K + k:TOP_K + k + 1, :] = pick(k, before)
        meta_ref[2 * TOP_K + k:2 * TOP_K + k + 1, :] = exps[k] / denom
    count_ref[...] += jnp.sum(onehot, axis=1, keepdims=True)


def _score_bias(sinks):
    t_idx = jnp.arange(Q_REP * ATTN_BLOCK)[:, None] % ATTN_BLOCK
    s_idx = jnp.arange(2 * ATTN_BLOCK)[None, :]
    diff = t_idx + ATTN_BLOCK - s_idx
    band = (diff >= 0) & (diff < ATTN_BLOCK)
    masks = jnp.stack([band, band & (s_idx >= ATTN_BLOCK)])
    bias = jnp.where(masks, 0.0, NEG_INF).astype(jnp.float32)
    sink_rows = jnp.repeat(sinks.reshape(N_KV_HEADS, Q_REP) * LOG2_E, ATTN_BLOCK, axis=1)
    return jnp.where(s_idx == 0, sink_rows[:, None, :, None], bias[None])


def _mixer(x2, sinks, w_in, b_in, lnv_g, lnv_b, w_sp, b_sp_full, w_out, b_out, ln1_g, ln1_b,
           w_r, w_r_hi, b_r, seq_len):
    t = x2.shape[0]
    tm = MIXER_ROWS
    grp = jnp.arange(GMLP_WIDTH) // GMLP_GROUP_DIM
    grp_avg = jnp.where(grp[:, None] == grp[None, :], 1.0 / GMLP_GROUP_DIM, 0.0).astype(jnp.bfloat16)
    tri = (jnp.arange(tm)[:, None] < jnp.arange(tm)[None, :]).astype(jnp.bfloat16)

    def full(shape):
        return pl.BlockSpec(shape, lambda i: (0,) * len(shape))

    return pl.pallas_call(
        functools.partial(_mixer_kernel, steps_per_seq=seq_len // tm),
        grid=(t // tm,),
        in_specs=[
            pl.BlockSpec((tm, D_MODEL), lambda i: (i, 0)),
            full((D_MODEL, IN_WIDTH)), full((1, IN_WIDTH)),
            full((N_KV_HEADS, 2, Q_REP * ATTN_BLOCK, 2 * ATTN_BLOCK)),
            full((1, GMLP_WIDTH)), full((1, GMLP_WIDTH)), full((GMLP_WIDTH, GMLP_WIDTH)),
            full((N_GMLP_GROUPS, ATTN_BLOCK, ATTN_BLOCK)), full((ATTN_BLOCK, GMLP_WIDTH)),
            full((D_MODEL, D_MODEL)), full((1, D_MODEL)), full((1, D_MODEL)), full((1, D_MODEL)),
            full((D_MODEL, ROUTER_LANES)), full((D_MODEL, ROUTER_LANES)), full((N_EXPERTS, 1)), full((tm, tm)),
        ],
        out_specs=[
            pl.BlockSpec((tm, D_MODEL), lambda i: (i, 0)),
            pl.BlockSpec((tm, D_MODEL // 2), lambda i: (i, 0)),
            pl.BlockSpec((META_ROWS, tm), lambda i: (0, i)),
            pl.BlockSpec((N_EXPERTS, 1), lambda i: (0, 0)),
        ],
        out_shape=[
            jax.ShapeDtypeStruct((t, D_MODEL), jnp.float32),
            jax.ShapeDtypeStruct((t, D_MODEL // 2), jnp.int32),
            jax.ShapeDtypeStruct((META_ROWS, t), jnp.float32),
            jax.ShapeDtypeStruct((N_EXPERTS, 1), jnp.float32),
        ],
        scratch_shapes=[pltpu.VMEM((ATTN_BLOCK, 2 * KV_WIDTH), jnp.bfloat16)],
        compiler_params=pltpu.CompilerParams(
            dimension_semantics=("arbitrary",), vmem_limit_bytes=VMEM_LIMIT_BYTES),
        name="mixer",
    )(x2, w_in, b_in, _score_bias(sinks), lnv_g, lnv_b, grp_avg, w_sp, b_sp_full, w_out, b_out, ln1_g, ln1_b,
      w_r, w_r_hi, b_r, tri)


def _gather_rows(idx, src):
    n = idx.shape[0]
    width = src.shape[1]
    win = GATHER_WINDOW
    sc = plsc.get_sparse_core_info()
    n_workers = sc.num_cores * sc.num_subcores
    per_worker = n // n_workers
    n_pairs = per_worker // (2 * win)
    assert n_pairs * 2 * win * n_workers == n
    mesh = plsc.VectorSubcoreMesh(core_axis_name="core", subcore_axis_name="subcore")

    @functools.partial(
        pl.kernel, out_type=jax.ShapeDtypeStruct((n, width), src.dtype), mesh=mesh,
        scratch_types=[pltpu.VMEM((per_worker,), jnp.int32), pltpu.VMEM((2, win, width), src.dtype),
                       pltpu.SemaphoreType.DMA((2,)), pltpu.SemaphoreType.DMA((2,))],
        name="gather_rows")
    def gather(src_hbm, idx_hbm, out_hbm, idx_v, rows_v, fetch_sem, store_sem):
        worker = lax.axis_index("subcore") * sc.num_cores + lax.axis_index("core")
        base = worker * per_worker
        pltpu.sync_copy(idx_hbm.at[pl.ds(base, per_worker)], idx_v)

        def fetch(chunk, buf):
            return pltpu.make_async_copy(src_hbm.at[idx_v.at[pl.ds(chunk * win, win)]], rows_v.at[buf],
                                         fetch_sem.at[buf])

        def store(chunk, buf):
            return pltpu.make_async_copy(rows_v.at[buf], out_hbm.at[pl.ds(base + chunk * win, win)],
                                         store_sem.at[buf])

        @pl.loop(0, n_pairs)
        def _(p):
            for buf in range(2):
                @pl.when(p > 0)
                def _():
                    store(2 * p - 2 + buf, buf).wait()
                fetch(2 * p + buf, buf).start()
            for buf in range(2):
                fetch(2 * p + buf, buf).wait()
                store(2 * p + buf, buf).start()

        for buf in range(2):
            store(2 * n_pairs - 2 + buf, buf).wait()

    return gather(src, idx)


def _dispatch_rows(dest_t, src, n_rows):
    t, width = src.shape
    win = GATHER_WINDOW
    sc = plsc.get_sparse_core_info()
    n_workers = sc.num_cores * sc.num_subcores
    per_worker = t // n_workers
    n_chunks = per_worker // win
    n_pairs = n_chunks // 2
    assert n_pairs * 2 * win * n_workers == t
    idx = dest_t.reshape(TOP_K, n_workers, n_chunks, win).transpose(1, 0, 2, 3)
    idx = idx.reshape(n_workers, TOP_K * n_chunks, win)
    mesh = plsc.VectorSubcoreMesh(core_axis_name="core", subcore_axis_name="subcore")

    @functools.partial(
        pl.kernel, out_type=jax.ShapeDtypeStruct((n_rows, width), src.dtype), mesh=mesh,
        scratch_types=[pltpu.VMEM((TOP_K * n_chunks, win), jnp.int32), pltpu.VMEM((2, win, width), src.dtype),
                       pltpu.SemaphoreType.DMA((2,)), pltpu.SemaphoreType.DMA((2,))],
        name="dispatch_rows")
    def dispatch(src_hbm, idx_hbm, out_hbm, idx_v, rows_v, fetch_sem, store_sem):
        worker = lax.axis_index("subcore") * sc.num_cores + lax.axis_index("core")
        base = worker * per_worker
        pltpu.sync_copy(idx_hbm.at[worker], idx_v)

        def fetch(chunk, buf):
            return pltpu.make_async_copy(src_hbm.at[pl.ds(base + chunk * win, win)], rows_v.at[buf],
                                         fetch_sem.at[buf])

        def store(chunk, k, buf):
            return pltpu.make_async_copy(rows_v.at[buf], out_hbm.at[idx_v.at[k * n_chunks + chunk]],
                                         store_sem.at[buf])

        @pl.loop(0, n_pairs)
        def _(p):
            for buf in range(2):
                @pl.when(p > 0)
                def _():
                    for k in range(TOP_K):
                        store(2 * p - 2 + buf, k, buf).wait()
                fetch(2 * p + buf, buf).start()
            for buf in range(2):
                fetch(2 * p + buf, buf).wait()
                for k in range(TOP_K):
                    store(2 * p + buf, k, buf).start()

        for buf in range(2):
            for k in range(TOP_K):
                store(2 * n_pairs - 2 + buf, k, buf).wait()

    return dispatch(src, idx)


def _expert_kernel(block_e_ref, n_valid_ref, first_ref, next_e_ref,
                   x_ref, wg_hbm, bg_ref, wu_hbm, bu_ref, wd_hbm, bd_ref,
                   y_ref, wg_f32, wu_f32, wd_f32, wg_bf, wu_bf, wd_bf, sem):
    i = pl.program_id(0)
    n_valid = n_valid_ref[i]
    staged = ((wg_hbm, wg_f32, wg_bf), (wu_hbm, wu_f32, wu_bf), (wd_hbm, wd_f32, wd_bf))

    def weight_copy(m, expert):
        return pltpu.make_async_copy(staged[m][0].at[expert], staged[m][1], sem.at[m])

    @pl.when(i == 0)
    def _():
        for m in range(3):
            weight_copy(m, block_e_ref[0]).start()

    @pl.when(first_ref[i] == 1)
    def _():
        for m in range(3):
            weight_copy(m, block_e_ref[i]).wait()
            staged[m][2][...] = staged[m][1][...].astype(jnp.bfloat16)

            @pl.when(next_e_ref[i] >= 0)
            def _():
                weight_copy(m, next_e_ref[i]).start()

    def expert_mlp(rows):
        row = lax.broadcasted_iota(jnp.int32, (rows, 1), 0)
        x_hi, x_lo = _unpack_bf16_pairs(jnp.where(row < n_valid, x_ref[:rows, :], 0))
        xb = jnp.concatenate([x_hi, x_lo], axis=1).astype(jnp.bfloat16)
        gt = jnp.minimum(jnp.dot(xb, wg_bf[...], preferred_element_type=jnp.float32) + bg_ref[0], SWIGLU_LIMIT)
        up = jnp.clip(jnp.dot(xb, wu_bf[...], preferred_element_type=jnp.float32) + bu_ref[0],
                      -SWIGLU_LIMIT, SWIGLU_LIMIT)
        hid = gt * jax.nn.sigmoid(SWIGLU_ALPHA * gt) * (up + 1.0)
        y = jnp.dot(hid.astype(jnp.bfloat16), wd_bf[...], preferred_element_type=jnp.float32) + bd_ref[0]
        y_ref[:rows, :] = _pack_bf16_pairs(y.astype(jnp.bfloat16))
        if rows < y_ref.shape[0]:
            y_ref[rows:, :] = jnp.zeros((y_ref.shape[0] - rows, y_ref.shape[1]), y_ref.dtype)

    row_options = (0,) + EXPERT_PARTIAL_ROWS + (x_ref.shape[0],)
    for lo, hi in zip(row_options[:-1], row_options[1:]):
        pl.when(jnp.logical_and(n_valid > lo, n_valid <= hi))(functools.partial(expert_mlp, hi))

    @pl.when(n_valid == 0)
    def _():
        y_ref[...] = jnp.zeros_like(y_ref)


def _experts(block_e, n_valid, x_rows, w_gate, b_gate, w_up, b_up, w_down, b_down):
    n_rows = x_rows.shape[0]
    bm = EXPERT_ROWS
    n_blocks = n_rows // bm
    d_ff = w_gate.shape[2]

    is_first = jnp.concatenate([jnp.ones((1,), jnp.int32), (block_e[1:] != block_e[:-1]).astype(jnp.int32)])
    next_e = jnp.min(jnp.where(block_e[None, :] > block_e[:, None], block_e[None, :], N_EXPERTS), axis=1)
    next_e = jnp.where(next_e == N_EXPERTS, -1, next_e)

    def by_expert(shape):
        return pl.BlockSpec(shape, lambda i, be, *_: (be[i],) + (0,) * (len(shape) - 1))

    hbm = pl.BlockSpec(memory_space=pl.ANY)
    return pl.pallas_call(
        _expert_kernel,
        grid_spec=pltpu.PrefetchScalarGridSpec(
            num_scalar_prefetch=4,
            grid=(n_blocks,),
            in_specs=[
                pl.BlockSpec((bm, D_MODEL // 2), lambda i, *_: (i, 0)),
                hbm, by_expert((1, 1, d_ff)),
                hbm, by_expert((1, 1, d_ff)),
                hbm, by_expert((1, 1, D_MODEL)),
            ],
            out_specs=pl.BlockSpec((bm, D_MODEL // 2), lambda i, *_: (i, 0)),
            scratch_shapes=[
                pltpu.VMEM((D_MODEL, d_ff), jnp.float32),
                pltpu.VMEM((D_MODEL, d_ff), jnp.float32),
                pltpu.VMEM((d_ff, D_MODEL), jnp.float32),
                pltpu.VMEM((D_MODEL, d_ff), jnp.bfloat16),
                pltpu.VMEM((D_MODEL, d_ff), jnp.bfloat16),
                pltpu.VMEM((d_ff, D_MODEL), jnp.bfloat16),
                pltpu.SemaphoreType.DMA((3,)),
            ],
        ),
        out_shape=jax.ShapeDtypeStruct((n_rows, D_MODEL // 2), jnp.int32),
        compiler_params=pltpu.CompilerParams(
            dimension_semantics=("arbitrary",), vmem_limit_bytes=VMEM_LIMIT_BYTES),
        name="experts",
    )(block_e, n_valid, is_first, next_e, x_rows, w_gate, b_gate.reshape(N_EXPERTS, 1, d_ff), w_up,
      b_up.reshape(N_EXPERTS, 1, d_ff), w_down, b_down.reshape(N_EXPERTS, 1, D_MODEL))


def _combine_kernel(h_ref, y_ref, meta_ref, g_ref, b_ref, o_ref):
    tm = h_ref.shape[0]
    meta_cols = jnp.concatenate([meta_ref[...], jnp.zeros((LANES - META_ROWS, tm), jnp.float32)], axis=0).T
    ffn_hi, ffn_lo = 0.0, 0.0
    for k in range(TOP_K):
        gate = meta_cols[:, 2 * TOP_K + k:2 * TOP_K + k + 1]
        y_hi, y_lo = _unpack_bf16_pairs(y_ref[k])
        ffn_hi = ffn_hi + gate * y_hi
        ffn_lo = ffn_lo + gate * y_lo
    ffn = jnp.concatenate([ffn_hi, ffn_lo], axis=1)
    o_ref[...] = _layer_norm(DEEPNORM_ALPHA * h_ref[...] + ffn, g_ref[...], b_ref[...])


def _combine(h, y_tok, meta, ln2_g, ln2_b):
    t = h.shape[0]
    tm = COMBINE_ROWS
    return pl.pallas_call(
        _combine_kernel,
        grid=(t // tm,),
        in_specs=[
            pl.BlockSpec((tm, D_MODEL), lambda i: (i, 0)),
            pl.BlockSpec((TOP_K, tm, D_MODEL // 2), lambda i: (0, i, 0)),
            pl.BlockSpec((META_ROWS, tm), lambda i: (0, i)),
            pl.BlockSpec((1, D_MODEL), lambda i: (0, 0)),
            pl.BlockSpec((1, D_MODEL), lambda i: (0, 0)),
        ],
        out_specs=pl.BlockSpec((tm, D_MODEL), lambda i: (i, 0)),
        out_shape=jax.ShapeDtypeStruct((t, D_MODEL), jnp.float32),
        compiler_params=pltpu.CompilerParams(
            dimension_semantics=("arbitrary",), vmem_limit_bytes=VMEM_LIMIT_BYTES),
        name="combine",
    )(h, y_tok, meta, ln2_g, ln2_b)


def _layer(x2, seq_len, w_in, b_in, sinks, ln_v_g, ln_v_b, w_spatial, b_spatial, w_out, b_out,
           ln1_g, ln1_b, w_router, b_router, w_gate, b_gate, w_up, b_up, w_down, b_down, ln2_g, ln2_b):
    t = x2.shape[0]
    tk = t * TOP_K
    bm = EXPERT_ROWS
    bf16 = jnp.bfloat16

    w_r_hi = w_router.astype(bf16)
    w_r_lo = (w_router - w_r_hi.astype(jnp.float32)).astype(bf16)
    lane_pad = jnp.zeros((D_MODEL, ROUTER_LANES - 2 * N_EXPERTS), bf16)
    w_r = jnp.concatenate([w_r_hi, w_r_lo, lane_pad], axis=1)
    w_r_hi_only = jnp.concatenate([w_r_hi, jnp.zeros_like(w_r_lo), lane_pad], axis=1)
    b_sp_full = jnp.repeat(b_spatial.T, GMLP_GROUP_DIM, axis=1)

    h, h_packed, meta, counts = _mixer(
        x2, sinks, w_in.astype(bf16), b_in[None], ln_v_g[None], ln_v_b[None], w_spatial, b_sp_full,
        w_out.astype(bf16), b_out[None], ln1_g[None], ln1_b[None], w_r, w_r_hi_only, b_router[:, None], seq_len)

    counts = counts[:, 0].astype(jnp.int32)
    experts = jnp.arange(N_EXPERTS, dtype=jnp.int32)
    padded = (counts + bm - 1) // bm * bm
    padded_end = jnp.sum(jnp.where(experts[None, :] <= experts[:, None], padded[None, :], 0), axis=1)
    padded_start = padded_end - padded
    n_blocks = tk // bm + N_EXPERTS
    n_rows = n_blocks * bm
    top_idx_t = meta[:TOP_K].astype(jnp.int32)
    rank_t = meta[TOP_K:2 * TOP_K].astype(jnp.int32)
    dest_t = rank_t + jnp.sum(
        jnp.where(top_idx_t[None] == experts[:, None, None], padded_start[:, None, None], 0), axis=0)
    block_start = jnp.arange(n_blocks, dtype=jnp.int32) * bm
    block_e = jnp.minimum(
        jnp.sum((padded_end[None, :] <= block_start[:, None]).astype(jnp.int32), axis=1), N_EXPERTS - 1)
    valid_end = jnp.sum(jnp.where(block_e[:, None] == experts[None, :], (padded_start + counts)[None, :], 0), axis=1)
    n_valid = jnp.clip(valid_end - block_start, 0, bm)

    x_rows = _dispatch_rows(dest_t, h_packed, n_rows)
    y_rows = _experts(block_e, n_valid, x_rows, w_gate, b_gate, w_up, b_up, w_down, b_down)
    y_tok = _gather_rows(dest_t.reshape(-1), y_rows).reshape(TOP_K, t, D_MODEL // 2)
    return _combine(h, y_tok, meta, ln2_g[None], ln2_b[None])


def kernel(x, w_in, b_in, sinks, ln_v_g, ln_v_b, w_spatial, b_spatial, w_out, b_out, ln1_g, ln1_b,
           w_router, b_router, w_gate, b_gate, w_up, b_up, w_down, b_down, ln2_g, ln2_b):
    batch, seq_len, d = x.shape
    x2 = x.reshape(batch * seq_len, d)
    for l in range(DEPTH):
        x2 = _layer(x2, seq_len, w_in[l], b_in[l], sinks[l], ln_v_g[l], ln_v_b[l], w_spatial[l],
                    b_spatial[l], w_out[l], b_out[l], ln1_g[l], ln1_b[l], w_router[l], b_router[l],
                    w_gate[l], b_gate[l], w_up[l], b_up[l], w_down[l], b_down[l], ln2_g[l], ln2_b[l])
    return x2.reshape(batch, seq_len, d)
```

```python
import functools

import jax
import jax.numpy as jnp
from jax import lax
from jax.experimental import pallas as pl
from jax.experimental.pallas import tpu as pltpu
from jax.experimental.pallas import tpu_sc as plsc

D_MODEL = 1024
HEAD_DIM = 64
N_Q_HEADS = 8
N_KV_HEADS = 2
Q_REP = N_Q_HEADS // N_KV_HEADS
ATTN_WIDTH = N_Q_HEADS * HEAD_DIM
KV_WIDTH = N_KV_HEADS * HEAD_DIM
ATTN_BLOCK = 128
N_GMLP_GROUPS = 8
GMLP_WIDTH = D_MODEL - ATTN_WIDTH
GMLP_GROUP_DIM = GMLP_WIDTH // N_GMLP_GROUPS
IN_WIDTH = ATTN_WIDTH + 2 * KV_WIDTH + 2 * GMLP_WIDTH
N_EXPERTS = 32
TOP_K = 4
SWIGLU_LIMIT = 7.0
SWIGLU_ALPHA = 1.702
LN_EPS = 1e-5
DEPTH = 1
DEEPNORM_ALPHA = (2.0 * DEPTH) ** 0.25
NEG_INF = -1e30
LOG2_E = 1.4426950408889634

LANES = 128
SUBLANES = 8

MIXER_ROWS = 1024
EXPERT_ROWS = 1024
EXPERT_PARTIAL_ROWS = (128, 256, 384, 512, 768)
GATHER_WINDOW = 64
META_ROWS = 16
ROUTER_LANES = LANES
COMBINE_ROWS = 1024
VMEM_LIMIT_BYTES = 56 * 1024 * 1024

_O_K = ATTN_WIDTH
_O_V = _O_K + KV_WIDTH
_O_U = _O_V + KV_WIDTH
_O_G = _O_U + GMLP_WIDTH


def _pack_bf16_pairs(v):
    n = v.shape[1] // 2
    hi = lax.bitcast_convert_type(v[:, :n].astype(jnp.float32), jnp.int32)
    lo = lax.bitcast_convert_type(v[:, n:].astype(jnp.float32), jnp.int32)
    return hi | lax.shift_right_logical(lo, 16)


def _unpack_bf16_pairs(p):
    hi = lax.bitcast_convert_type(p & jnp.int32(-65536), jnp.float32)
    lo = lax.bitcast_convert_type(lax.shift_left(p, 16), jnp.float32)
    return hi, lo


def _layer_norm(v, g, b):
    mu = jnp.mean(v, axis=-1, keepdims=True)
    vc = v - mu
    var = jnp.mean(vc * vc, axis=-1, keepdims=True)
    return vc * lax.rsqrt(var + LN_EPS) * g + b


def _attention_block(q, kb, vb, bias_ref, bias_sel):
    outs = []
    for g in range(N_KV_HEADS):
        kg = kb[:, g * HEAD_DIM:(g + 1) * HEAD_DIM]
        vg = vb[:, g * HEAD_DIM:(g + 1) * HEAD_DIM]
        qg = jnp.concatenate(
            [q[:, (g * Q_REP + r) * HEAD_DIM:(g * Q_REP + r + 1) * HEAD_DIM] for r in range(Q_REP)],
            axis=0).astype(jnp.bfloat16)
        s = (lax.dot_general(qg, kg, (((1,), (1,)), ((), ())), preferred_element_type=jnp.float32)
             + bias_ref[g, bias_sel])
        p = jnp.exp2(s - jnp.max(s, axis=-1, keepdims=True))
        denom = jnp.sum(p, axis=-1, keepdims=True)
        o = jnp.dot(p.astype(jnp.bfloat16), vg, preferred_element_type=jnp.float32) / denom
        outs.extend(o[r * ATTN_BLOCK:(r + 1) * ATTN_BLOCK] for r in range(Q_REP))
    return jnp.concatenate(outs, axis=-1)


def _mixer_kernel(x_ref, w_in_ref, b_in_ref, bias_ref, lnv_g_ref, lnv_b_ref, grp_avg_ref,
                  w_sp_ref, b_sp_ref, w_out_ref, b_out_ref, ln1_g_ref, ln1_b_ref,
                  w_r_ref, w_r_hi_ref, b_r_ref, tri_ref,
                  h_ref, hp_ref, meta_ref, count_ref,
                  kv_prev_ref, *, steps_per_seq):
    i = pl.program_id(0)
    first_step = (i % steps_per_seq) == 0
    tm = x_ref.shape[0]
    n_sub = tm // ATTN_BLOCK

    @pl.when(i == 0)
    def _():
        count_ref[...] = jnp.zeros_like(count_ref)

    @pl.when(first_step)
    def _():
        kv_prev_ref[...] = jnp.zeros_like(kv_prev_ref)

    x = x_ref[...]
    proj = jnp.dot(x.astype(jnp.bfloat16), w_in_ref[...], preferred_element_type=jnp.float32) + b_in_ref[...]

    q_all = proj[:, :_O_K] * (LOG2_E * HEAD_DIM ** -0.5)
    k_all = proj[:, _O_K:_O_V].astype(jnp.bfloat16)
    v_all = proj[:, _O_V:_O_U].astype(jnp.bfloat16)
    k_prev = kv_prev_ref[:, :KV_WIDTH]
    v_prev = kv_prev_ref[:, KV_WIDTH:]
    is_row0 = lax.broadcasted_iota(jnp.int32, (ATTN_BLOCK, KV_WIDTH), 0) == 0
    attn_blocks = []
    for sb in range(n_sub):
        rows = slice(sb * ATTN_BLOCK, (sb + 1) * ATTN_BLOCK)
        k_cur, v_cur = k_all[rows], v_all[rows]
        kb = jnp.concatenate([jnp.where(is_row0, 0, k_prev), k_cur], axis=0)
        vb = jnp.concatenate([jnp.where(is_row0, 0, v_prev), v_cur], axis=0)
        bias_sel = jnp.where(first_step, 1, 0) if sb == 0 else 0
        attn_blocks.append(_attention_block(q_all[rows], kb, vb, bias_ref, bias_sel))
        k_prev, v_prev = k_cur, v_cur
    kv_prev_ref[:, :KV_WIDTH] = k_prev
    kv_prev_ref[:, KV_WIDTH:] = v_prev
    attn = jnp.concatenate(attn_blocks, axis=0)

    u = jax.nn.gelu(proj[:, _O_U:_O_G])
    gg = jax.nn.gelu(proj[:, _O_G:])
    avg = grp_avg_ref[...]
    mu = jnp.dot(gg.astype(jnp.bfloat16), avg, preferred_element_type=jnp.float32)
    gc = gg - mu
    var = jnp.dot((gc * gc).astype(jnp.bfloat16), avg, preferred_element_type=jnp.float32)
    gn = (gc * lax.rsqrt(var + LN_EPS) * lnv_g_ref[...] + lnv_b_ref[...]).astype(jnp.bfloat16)
    causal = (lax.broadcasted_iota(jnp.int32, (ATTN_BLOCK, ATTN_BLOCK), 0)
              >= lax.broadcasted_iota(jnp.int32, (ATTN_BLOCK, ATTN_BLOCK), 1))
    w_sp = [jnp.where(causal, w_sp_ref[g], 0.0).astype(jnp.bfloat16) for g in range(N_GMLP_GROUPS)]
    mixed_chunks = []
    for c in range(n_sub):
        rows = slice(c * ATTN_BLOCK, (c + 1) * ATTN_BLOCK)
        pieces = [
            jnp.dot(w_sp[g], gn[rows, g * GMLP_GROUP_DIM:(g + 1) * GMLP_GROUP_DIM],
                    preferred_element_type=jnp.float32)
            for g in range(N_GMLP_GROUPS)]
        mixed_chunks.append(jnp.concatenate(pieces, axis=-1) + b_sp_ref[...])
    sgu = u * jnp.concatenate(mixed_chunks, axis=0)

    mix = (jnp.dot(attn.astype(jnp.bfloat16), w_out_ref[:ATTN_WIDTH, :], preferred_element_type=jnp.float32)
           + jnp.dot(sgu.astype(jnp.bfloat16), w_out_ref[ATTN_WIDTH:, :], preferred_element_type=jnp.float32)
           + b_out_ref[...])
    h = _layer_norm(DEEPNORM_ALPHA * x + mix, ln1_g_ref[...], ln1_b_ref[...])
    h_ref[...] = h

    h_hi = h.astype(jnp.bfloat16)
    hp_ref[...] = _pack_bf16_pairs(h_hi)
    h_lo = (h - h_hi.astype(jnp.float32)).astype(jnp.bfloat16)
    part = (jnp.dot(h_hi, w_r_ref[...], preferred_element_type=jnp.float32)
            + jnp.dot(h_lo, w_r_hi_ref[...], preferred_element_type=jnp.float32)).T
    logits = part[:N_EXPERTS] + part[N_EXPERTS:2 * N_EXPERTS] + b_r_ref[...]

    n_grp = N_EXPERTS // SUBLANES
    grp = [logits[SUBLANES * g:SUBLANES * (g + 1)] for g in range(n_grp)]
    sub = lax.broadcasted_iota(jnp.int32, (SUBLANES, tm), 0)
    beaten = [jnp.zeros((SUBLANES, tm), jnp.float32) for _ in range(n_grp)]
    for e2 in range(N_EXPERTS):
        g2, r2 = divmod(e2, SUBLANES)
        row = logits[e2:e2 + 1]
        for g in range(n_grp):
            if g > g2:
                wins = jnp.where(row >= grp[g], 1.0, 0.0)
            elif g < g2:
                wins = jnp.where(row > grp[g], 1.0, 0.0)
            else:
                wins = jnp.where(sub > r2, jnp.where(row >= grp[g], 1.0, 0.0), jnp.where(row > grp[g], 1.0, 0.0))
            beaten[g] = beaten[g] + wins
    place = jnp.concatenate(beaten, axis=0)
    expert_id = lax.broadcasted_iota(jnp.int32, (N_EXPERTS, tm), 0).astype(jnp.float32)
    onehot = jnp.where(place < TOP_K, 1.0, 0.0)
    before = jnp.dot(onehot.astype(jnp.bfloat16), tri_ref[...], preferred_element_type=jnp.float32) + count_ref[...]

    def pick(k, table):
        return jnp.sum(jnp.where(place == k, table, 0.0), axis=0, keepdims=True)

    vals = [pick(k, logits) for k in range(TOP_K)]
    exps = [jnp.exp(v - vals[0]) for v in vals]
    denom = exps[0] + exps[1] + exps[2] + exps[3]
    meta_ref[...] = jnp.zeros_like(meta_ref)
    for k in range(TOP_K):
        meta_ref[k:k + 1, :] = pick(k, expert_id)
        meta_ref[TOP_K + k:TOP_K + k + 1, :] = pick(k, before)
        meta_ref[2 * TOP_K + k:2 * TOP_K + k + 1, :] = exps[k] / denom
    count_ref[...] += jnp.sum(onehot, axis=1, keepdims=True)


def _score_bias(sinks):
    t_idx = jnp.arange(Q_REP * ATTN_BLOCK)[:, None] % ATTN_BLOCK
    s_idx = jnp.arange(2 * ATTN_BLOCK)[None, :]
    diff = t_idx + ATTN_BLOCK - s_idx
    band = (diff >= 0) & (diff < ATTN_BLOCK)
    masks = jnp.stack([band, band & (s_idx >= ATTN_BLOCK)])
    bias = jnp.where(masks, 0.0, NEG_INF).astype(jnp.float32)
    sink_rows = jnp.repeat(sinks.reshape(N_KV_HEADS, Q_REP) * LOG2_E, ATTN_BLOCK, axis=1)
    return jnp.where(s_idx == 0, sink_rows[:, None, :, None], bias[None])


def _mixer(x2, sinks, w_in, b_in, lnv_g, lnv_b, w_sp, b_sp_full, w_out, b_out, ln1_g, ln1_b,
           w_r, w_r_hi, b_r, seq_len):
    t = x2.shape[0]
    tm = MIXER_ROWS
    grp = jnp.arange(GMLP_WIDTH) // GMLP_GROUP_DIM
    grp_avg = jnp.where(grp[:, None] == grp[None, :], 1.0 / GMLP_GROUP_DIM, 0.0).astype(jnp.bfloat16)
    tri = (jnp.arange(tm)[:, None] < jnp.arange(tm)[None, :]).astype(jnp.bfloat16)

    def full(shape):
        return pl.BlockSpec(shape, lambda i: (0,) * len(shape))

    return pl.pallas_call(
        functools.partial(_mixer_kernel, steps_per_seq=seq_len // tm),
        grid=(t // tm,),
        in_specs=[
            pl.BlockSpec((tm, D_MODEL), lambda i: (i, 0)),
            full((D_MODEL, IN_WIDTH)), full((1, IN_WIDTH)),
            full((N_KV_HEADS, 2, Q_REP * ATTN_BLOCK, 2 * ATTN_BLOCK)),
            full((1, GMLP_WIDTH)), full((1, GMLP_WIDTH)), full((GMLP_WIDTH, GMLP_WIDTH)),
            full((N_GMLP_GROUPS, ATTN_BLOCK, ATTN_BLOCK)), full((ATTN_BLOCK, GMLP_WIDTH)),
            full((D_MODEL, D_MODEL)), full((1, D_MODEL)), full((1, D_MODEL)), full((1, D_MODEL)),
            full((D_MODEL, ROUTER_LANES)), full((D_MODEL, ROUTER_LANES)), full((N_EXPERTS, 1)), full((tm, tm)),
        ],
        out_specs=[
            pl.BlockSpec((tm, D_MODEL), lambda i: (i, 0)),
            pl.BlockSpec((tm, D_MODEL // 2), lambda i: (i, 0)),
            pl.BlockSpec((META_ROWS, tm), lambda i: (0, i)),
            pl.BlockSpec((N_EXPERTS, 1), lambda i: (0, 0)),
        ],
        out_shape=[
            jax.ShapeDtypeStruct((t, D_MODEL), jnp.float32),
            jax.ShapeDtypeStruct((t, D_MODEL // 2), jnp.int32),
            jax.ShapeDtypeStruct((META_ROWS, t), jnp.float32),
            jax.ShapeDtypeStruct((N_EXPERTS, 1), jnp.float32),
        ],
        scratch_shapes=[pltpu.VMEM((ATTN_BLOCK, 2 * KV_WIDTH), jnp.bfloat16)],
        compiler_params=pltpu.CompilerParams(
            dimension_semantics=("arbitrary",), vmem_limit_bytes=VMEM_LIMIT_BYTES),
        name="mixer",
    )(x2, w_in, b_in, _score_bias(sinks), lnv_g, lnv_b, grp_avg, w_sp, b_sp_full, w_out, b_out, ln1_g, ln1_b,
      w_r, w_r_hi, b_r, tri)


def _gather_rows(idx, src):
    n = idx.shape[0]
    width = src.shape[1]
    win = GATHER_WINDOW
    sc = plsc.get_sparse_core_info()
    n_workers = sc.num_cores * sc.num_subcores
    per_worker = n // n_workers
    n_pairs = per_worker // (2 * win)
    assert n_pairs * 2 * win * n_workers == n
    mesh = plsc.VectorSubcoreMesh(core_axis_name="core", subcore_axis_name="subcore")

    @functools.partial(
        pl.kernel, out_type=jax.ShapeDtypeStruct((n, width), src.dtype), mesh=mesh,
        scratch_types=[pltpu.VMEM((per_worker,), jnp.int32), pltpu.VMEM((2, win, width), src.dtype),
                       pltpu.SemaphoreType.DMA((2,)), pltpu.SemaphoreType.DMA((2,))],
        name="gather_rows")
    def gather(src_hbm, idx_hbm, out_hbm, idx_v, rows_v, fetch_sem, store_sem):
        worker = lax.axis_index("subcore") * sc.num_cores + lax.axis_index("core")
        base = worker * per_worker
        pltpu.sync_copy(idx_hbm.at[pl.ds(base, per_worker)], idx_v)

        def fetch(chunk, buf):
            return pltpu.make_async_copy(src_hbm.at[idx_v.at[pl.ds(chunk * win, win)]], rows_v.at[buf],
                                         fetch_sem.at[buf])

        def store(chunk, buf):
            return pltpu.make_async_copy(rows_v.at[buf], out_hbm.at[pl.ds(base + chunk * win, win)],
                                         store_sem.at[buf])

        @pl.loop(0, n_pairs)
        def _(p):
            for buf in range(2):
                @pl.when(p > 0)
                def _():
                    store(2 * p - 2 + buf, buf).wait()
                fetch(2 * p + buf, buf).start()
            for buf in range(2):
                fetch(2 * p + buf, buf).wait()
                store(2 * p + buf, buf).start()

        for buf in range(2):
            store(2 * n_pairs - 2 + buf, buf).wait()

    return gather(src, idx)


def _dispatch_rows(dest_t, src, n_rows):
    t, width = src.shape
    win = GATHER_WINDOW
    sc = plsc.get_sparse_core_info()
    n_workers = sc.num_cores * sc.num_subcores
    per_worker = t // n_workers
    n_chunks = per_worker // win
    n_pairs = n_chunks // 2
    assert n_pairs * 2 * win * n_workers == t
    idx = dest_t.reshape(TOP_K, n_workers, n_chunks, win).transpose(1, 0, 2, 3)
    idx = idx.reshape(n_workers, TOP_K * n_chunks, win)
    mesh = plsc.VectorSubcoreMesh(core_axis_name="core", subcore_axis_name="subcore")

    @functools.partial(
        pl.kernel, out_type=jax.ShapeDtypeStruct((n_rows, width), src.dtype), mesh=mesh,
        scratch_types=[pltpu.VMEM((TOP_K * n_chunks, win), jnp.int32), pltpu.VMEM((2, win, width), src.dtype),
                       pltpu.SemaphoreType.DMA((2,)), pltpu.SemaphoreType.DMA((2,))],
        name="dispatch_rows")
    def dispatch(src_hbm, idx_hbm, out_hbm, idx_v, rows_v, fetch_sem, store_sem):
        worker = lax.axis_index("subcore") * sc.num_cores + lax.axis_index("core")
        base = worker * per_worker
        pltpu.sync_copy(idx_hbm.at[worker], idx_v)

        def fetch(chunk, buf):
            return pltpu.make_async_copy(src_hbm.at[pl.ds(base + chunk * win, win)], rows_v.at[buf],
                                         fetch_sem.at[buf])

        def store(chunk, k, buf):
            return pltpu.make_async_copy(rows_v.at[buf], out_hbm.at[idx_v.at[k * n_chunks + chunk]],
                                         store_sem.at[buf])

        @pl.loop(0, n_pairs)
        def _(p):
            for buf in range(2):
                @pl.when(p > 0)
                def _():
                    for k in range(TOP_K):
                        store(2 * p - 2 + buf, k, buf).wait()
                fetch(2 * p + buf, buf).start()
            for buf in range(2):
                fetch(2 * p + buf, buf).wait()
                for k in range(TOP_K):
                    store(2 * p + buf, k, buf).start()

        for buf in range(2):
            for k in range(TOP_K):
                store(2 * n_pairs - 2 + buf, k, buf).wait()

    return dispatch(src, idx)


def _expert_kernel(block_e_ref, n_valid_ref, first_ref, next_e_ref,
                   x_ref, wg_hbm, bg_ref, wu_hbm, bu_ref, wd_hbm, bd_ref,
                   y_ref, wg_f32, wu_f32, wd_f32, wg_bf, wu_bf, wd_bf, sem):
    i = pl.program_id(0)
    n_valid = n_valid_ref[i]
    staged = ((wg_hbm, wg_f32, wg_bf), (wu_hbm, wu_f32, wu_bf), (wd_hbm, wd_f32, wd_bf))

    def weight_copy(m, expert):
        return pltpu.make_async_copy(staged[m][0].at[expert], staged[m][1], sem.at[m])

    @pl.when(i == 0)
    def _():
        for m in range(3):
            weight_copy(m, block_e_ref[0]).start()

    @pl.when(first_ref[i] == 1)
    def _():
        for m in range(3):
            weight_copy(m, block_e_ref[i]).wait()
            staged[m][2][...] = staged[m][1][...].astype(jnp.bfloat16)

            @pl.when(next_e_ref[i] >= 0)
            def _():
                weight_copy(m, next_e_ref[i]).start()

    def expert_mlp(rows):
        row = lax.broadcasted_iota(jnp.int32, (rows, 1), 0)
        x_hi, x_lo = _unpack_bf16_pairs(jnp.where(row < n_valid, x_ref[:rows, :], 0))
        xb = jnp.concatenate([x_hi, x_lo], axis=1).astype(jnp.bfloat16)
        gt = jnp.minimum(jnp.dot(xb, wg_bf[...], preferred_element_type=jnp.float32) + bg_ref[0], SWIGLU_LIMIT)
        up = jnp.clip(jnp.dot(xb, wu_bf[...], preferred_element_type=jnp.float32) + bu_ref[0],
                      -SWIGLU_LIMIT, SWIGLU_LIMIT)
        hid = gt * jax.nn.sigmoid(SWIGLU_ALPHA * gt) * (up + 1.0)
        y = jnp.dot(hid.astype(jnp.bfloat16), wd_bf[...], preferred_element_type=jnp.float32) + bd_ref[0]
        y_ref[:rows, :] = _pack_bf16_pairs(y.astype(jnp.bfloat16))
        if rows < y_ref.shape[0]:
            y_ref[rows:, :] = jnp.zeros((y_ref.shape[0] - rows, y_ref.shape[1]), y_ref.dtype)

    row_options = (0,) + EXPERT_PARTIAL_ROWS + (x_ref.shape[0],)
    for lo, hi in zip(row_options[:-1], row_options[1:]):
        pl.when(jnp.logical_and(n_valid > lo, n_valid <= hi))(functools.partial(expert_mlp, hi))

    @pl.when(n_valid == 0)
    def _():
        y_ref[...] = jnp.zeros_like(y_ref)


def _experts(block_e, n_valid, x_rows, w_gate, b_gate, w_up, b_up, w_down, b_down):
    n_rows = x_rows.shape[0]
    bm = EXPERT_ROWS
    n_blocks = n_rows // bm
    d_ff = w_gate.shape[2]

    is_first = jnp.concatenate([jnp.ones((1,), jnp.int32), (block_e[1:] != block_e[:-1]).astype(jnp.int32)])
    next_e = jnp.min(jnp.where(block_e[None, :] > block_e[:, None], block_e[None, :], N_EXPERTS), axis=1)
    next_e = jnp.where(next_e == N_EXPERTS, -1, next_e)

    def by_expert(shape):
        return pl.BlockSpec(shape, lambda i, be, *_: (be[i],) + (0,) * (len(shape) - 1))

    hbm = pl.BlockSpec(memory_space=pl.ANY)
    return pl.pallas_call(
        _expert_kernel,
        grid_spec=pltpu.PrefetchScalarGridSpec(
            num_scalar_prefetch=4,
            grid=(n_blocks,),
            in_specs=[
                pl.BlockSpec((bm, D_MODEL // 2), lambda i, *_: (i, 0)),
                hbm, by_expert((1, 1, d_ff)),
                hbm, by_expert((1, 1, d_ff)),
                hbm, by_expert((1, 1, D_MODEL)),
            ],
            out_specs=pl.BlockSpec((bm, D_MODEL // 2), lambda i, *_: (i, 0)),
            scratch_shapes=[
                pltpu.VMEM((D_MODEL, d_ff), jnp.float32),
                pltpu.VMEM((D_MODEL, d_ff), jnp.float32),
                pltpu.VMEM((d_ff, D_MODEL), jnp.float32),
                pltpu.VMEM((D_MODEL, d_ff), jnp.bfloat16),
                pltpu.VMEM((D_MODEL, d_ff), jnp.bfloat16),
                pltpu.VMEM((d_ff, D_MODEL), jnp.bfloat16),
                pltpu.SemaphoreType.DMA((3,)),
            ],
        ),
        out_shape=jax.ShapeDtypeStruct((n_rows, D_MODEL // 2), jnp.int32),
        compiler_params=pltpu.CompilerParams(
            dimension_semantics=("arbitrary",), vmem_limit_bytes=VMEM_LIMIT_BYTES),
        name="experts",
    )(block_e, n_valid, is_first, next_e, x_rows, w_gate, b_gate.reshape(N_EXPERTS, 1, d_ff), w_up,
      b_up.reshape(N_EXPERTS, 1, d_ff), w_down, b_down.reshape(N_EXPERTS, 1, D_MODEL))


def _combine_kernel(h_ref, y_ref, meta_ref, g_ref, b_ref, o_ref):
    tm = h_ref.shape[0]
    meta_cols = jnp.concatenate([meta_ref[...], jnp.zeros((LANES - META_ROWS, tm), jnp.float32)], axis=0).T
    ffn_hi, ffn_lo = 0.0, 0.0
    for k in range(TOP_K):
        gate = meta_cols[:, 2 * TOP_K + k:2 * TOP_K + k + 1]
        y_hi, y_lo = _unpack_bf16_pairs(y_ref[k])
        ffn_hi = ffn_hi + gate * y_hi
        ffn_lo = ffn_lo + gate * y_lo
    ffn = jnp.concatenate([ffn_hi, ffn_lo], axis=1)
    o_ref[...] = _layer_norm(DEEPNORM_ALPHA * h_ref[...] + ffn, g_ref[...], b_ref[...])


def _combine(h, y_tok, meta, ln2_g, ln2_b):
    t = h.shape[0]
    tm = COMBINE_ROWS
    return pl.pallas_call(
        _combine_kernel,
        grid=(t // tm,),
        in_specs=[
            pl.BlockSpec((tm, D_MODEL), lambda i: (i, 0)),
            pl.BlockSpec((TOP_K, tm, D_MODEL // 2), lambda i: (0, i, 0)),
            pl.BlockSpec((META_ROWS, tm), lambda i: (0, i)),
            pl.BlockSpec((1, D_MODEL), lambda i: (0, 0)),
            pl.BlockSpec((1, D_MODEL), lambda i: (0, 0)),
        ],
        out_specs=pl.BlockSpec((tm, D_MODEL), lambda i: (i, 0)),
        out_shape=jax.ShapeDtypeStruct((t, D_MODEL), jnp.float32),
        compiler_params=pltpu.CompilerParams(
            dimension_semantics=("arbitrary",), vmem_limit_bytes=VMEM_LIMIT_BYTES),
        name="combine",
    )(h, y_tok, meta, ln2_g, ln2_b)


def _layer(x2, seq_len, w_in, b_in, sinks, ln_v_g, ln_v_b, w_spatial, b_spatial, w_out, b_out,
           ln1_g, ln1_b, w_router, b_router, w_gate, b_gate, w_up, b_up, w_down, b_down, ln2_g, ln2_b):
    t = x2.shape[0]
    tk = t * TOP_K
    bm = EXPERT_ROWS
    bf16 = jnp.bfloat16

    w_r_hi = w_router.astype(bf16)
    w_r_lo = (w_router - w_r_hi.astype(jnp.float32)).astype(bf16)
    lane_pad = jnp.zeros((D_MODEL, ROUTER_LANES - 2 * N_EXPERTS), bf16)
    w_r = jnp.concatenate([w_r_hi, w_r_lo, lane_pad], axis=1)
    w_r_hi_only = jnp.concatenate([w_r_hi, jnp.zeros_like(w_r_lo), lane_pad], axis=1)
    b_sp_full = jnp.repeat(b_spatial.T, GMLP_GROUP_DIM, axis=1)

    h, h_packed, meta, counts = _mixer(
        x2, sinks, w_in.astype(bf16), b_in[None], ln_v_g[None], ln_v_b[None], w_spatial, b_sp_full,
        w_out.astype(bf16), b_out[None], ln1_g[None], ln1_b[None], w_r, w_r_hi_only, b_router[:, None], seq_len)

    counts = counts[:, 0].astype(jnp.int32)
    experts = jnp.arange(N_EXPERTS, dtype=jnp.int32)
    padded = (counts + bm - 1) // bm * bm
    padded_end = jnp.sum(jnp.where(experts[None, :] <= experts[:, None], padded[None, :], 0), axis=1)
    padded_start = padded_end - padded
    n_blocks = tk // bm + N_EXPERTS
    n_rows = n_blocks * bm
    top_idx_t = meta[:TOP_K].astype(jnp.int32)
    rank_t = meta[TOP_K:2 * TOP_K].astype(jnp.int32)
    dest_t = rank_t + jnp.sum(
        jnp.where(top_idx_t[None] == experts[:, None, None], padded_start[:, None, None], 0), axis=0)
    block_start = jnp.arange(n_blocks, dtype=jnp.int32) * bm
    block_e = jnp.minimum(
        jnp.sum((padded_end[None, :] <= block_start[:, None]).astype(jnp.int32), axis=1), N_EXPERTS - 1)
    valid_end = jnp.sum(jnp.where(block_e[:, None] == experts[None, :], (padded_start + counts)[None, :], 0), axis=1)
    n_valid = jnp.clip(valid_end - block_start, 0, bm)

    x_rows = _dispatch_rows(dest_t, h_packed, n_rows)
    y_rows = _experts(block_e, n_valid, x_rows, w_gate, b_gate, w_up, b_up, w_down, b_down)
    y_tok = _gather_rows(dest_t.reshape(-1), y_rows).reshape(TOP_K, t, D_MODEL // 2)
    return _combine(h, y_tok, meta, ln2_g[None], ln2_b[None])


def kernel(x, w_in, b_in, sinks, ln_v_g, ln_v_b, w_spatial, b_spatial, w_out, b_out, ln1_g, ln1_b,
           w_router, b_router, w_gate, b_gate, w_up, b_up, w_down, b_down, ln2_g, ln2_b):
    batch, seq_len, d = x.shape
    x2 = x.reshape(batch * seq_len, d)
    for l in range(DEPTH):
        x2 = _layer(x2, seq_len, w_in[l], b_in[l], sinks[l], ln_v_g[l], ln_v_b[l], w_spatial[l],
                    b_spatial[l], w_out[l], b_out[l], ln1_g[l], ln1_b[l], w_router[l], b_router[l],
                    w_gate[l], b_gate[l], w_up[l], b_up[l], w_down[l], b_down[l], ln2_g[l], ln2_b[l])
    return x2.reshape(batch, seq_len, d)
```

```python
import functools

import jax
import jax.numpy as jnp
from jax import lax
from jax.experimental import pallas as pl
from jax.experimental.pallas import tpu as pltpu
from jax.experimental.pallas import tpu_sc as plsc

D_MODEL = 1024
HEAD_DIM = 64
N_Q_HEADS = 8
N_KV_HEADS = 2
Q_REP = N_Q_HEADS // N_KV_HEADS
ATTN_WIDTH = N_Q_HEADS * HEAD_DIM
KV_WIDTH = N_KV_HEADS * HEAD_DIM
ATTN_BLOCK = 128
N_GMLP_GROUPS = 8
GMLP_WIDTH = D_MODEL - ATTN_WIDTH
GMLP_GROUP_DIM = GMLP_WIDTH // N_GMLP_GROUPS
IN_WIDTH = ATTN_WIDTH + 2 * KV_WIDTH + 2 * GMLP_WIDTH
N_EXPERTS = 32
TOP_K = 4
SWIGLU_LIMIT = 7.0
SWIGLU_ALPHA = 1.702
LN_EPS = 1e-5
DEPTH = 1
DEEPNORM_ALPHA = (2.0 * DEPTH) ** 0.25
NEG_INF = -1e30
LOG2_E = 1.4426950408889634

LANES = 128
SUBLANES = 8

MIXER_ROWS = 512
PROJ_CHUNK = 256
EXPERT_ROWS = 1024
EXPERT_PARTIAL_ROWS = (128, 256, 384, 512, 768)
GATHER_WINDOW = 64
META_ROWS = 16
ROUTER_LANES = LANES
COMBINE_ROWS = 1024
VMEM_LIMIT_BYTES = 56 * 1024 * 1024

_O_K = ATTN_WIDTH
_O_V = _O_K + KV_WIDTH
_O_U = _O_V + KV_WIDTH
_O_G = _O_U + GMLP_WIDTH


def _pack_bf16_pairs(v):
    n = v.shape[1] // 2
    hi = lax.bitcast_convert_type(v[:, :n].astype(jnp.float32), jnp.int32)
    lo = lax.bitcast_convert_type(v[:, n:].astype(jnp.float32), jnp.int32)
    return hi | lax.shift_right_logical(lo, 16)


def _unpack_bf16_pairs(p):
    hi = lax.bitcast_convert_type(p & jnp.int32(-65536), jnp.float32)
    lo = lax.bitcast_convert_type(lax.shift_left(p, 16), jnp.float32)
    return hi, lo


def _layer_norm(v, g, b):
    mu = jnp.mean(v, axis=-1, keepdims=True)
    vc = v - mu
    var = jnp.mean(vc * vc, axis=-1, keepdims=True)
    return vc * lax.rsqrt(var + LN_EPS) * g + b


def _attention_block(q, kb, vb, bias_ref, bias_sel):
    outs = []
    for g in range(N_KV_HEADS):
        kg = kb[:, g * HEAD_DIM:(g + 1) * HEAD_DIM]
        vg = vb[:, g * HEAD_DIM:(g + 1) * HEAD_DIM]
        qg = jnp.concatenate(
            [q[:, (g * Q_REP + r) * HEAD_DIM:(g * Q_REP + r + 1) * HEAD_DIM] for r in range(Q_REP)],
            axis=0).astype(jnp.bfloat16)
        s = (lax.dot_general(qg, kg, (((1,), (1,)), ((), ())), preferred_element_type=jnp.float32)
             + bias_ref[g, bias_sel])
        p = jnp.exp2(s - jnp.max(s, axis=-1, keepdims=True))
        denom = jnp.sum(p, axis=-1, keepdims=True)
        o = jnp.dot(p.astype(jnp.bfloat16), vg, preferred_element_type=jnp.float32) / denom
        outs.extend(o[r * ATTN_BLOCK:(r + 1) * ATTN_BLOCK] for r in range(Q_REP))
    return jnp.concatenate(outs, axis=-1)


def _mixer_kernel(x_ref, w_in_ref, b_in_ref, bias_ref, lnv_g_ref, lnv_b_ref, grp_avg_ref,
                  w_sp_ref, b_sp_ref, w_out_ref, b_out_ref, ln1_g_ref, ln1_b_ref,
                  w_r_ref, w_r_hi_ref, b_r_ref, tri_ref,
                  h_ref, hp_ref, meta_ref, count_ref,
                  kv_prev_ref, hres_ref, *, steps_per_seq, n_steps):
    i = pl.program_id(0)
    first_step = (i % steps_per_seq) == 0
    tm = x_ref.shape[0]
    n_sub = tm // ATTN_BLOCK

    @pl.when(i == 0)
    def _():
        count_ref[...] = jnp.zeros_like(count_ref)

    @pl.when(first_step)
    def _():
        kv_prev_ref[...] = jnp.zeros_like(kv_prev_ref)

    def route_stages():
        h = _layer_norm(hres_ref[...], ln1_g_ref[...], ln1_b_ref[...])
        h_ref[...] = h
        yield
        h_hi = h.astype(jnp.bfloat16)
        hp_ref[...] = _pack_bf16_pairs(h_hi)
        h_lo = (h - h_hi.astype(jnp.float32)).astype(jnp.bfloat16)
        yield
        part = (jnp.dot(h_hi, w_r_ref[...], preferred_element_type=jnp.float32)
                + jnp.dot(h_lo, w_r_hi_ref[...], preferred_element_type=jnp.float32)).T
        logits = part[:N_EXPERTS] + part[N_EXPERTS:2 * N_EXPERTS] + b_r_ref[...]
        yield
        n_grp = N_EXPERTS // SUBLANES
        grp = [logits[SUBLANES * g:SUBLANES * (g + 1)] for g in range(n_grp)]
        sub = lax.broadcasted_iota(jnp.int32, (SUBLANES, tm), 0)
        beaten = [jnp.zeros((SUBLANES, tm), jnp.float32) for _ in range(n_grp)]
        for e2 in range(N_EXPERTS):
            g2, r2 = divmod(e2, SUBLANES)
            row = logits[e2:e2 + 1]
            for g in range(n_grp):
                if g > g2:
                    wins = jnp.where(row >= grp[g], 1.0, 0.0)
                elif g < g2:
                    wins = jnp.where(row > grp[g], 1.0, 0.0)
                else:
                    wins = jnp.where(sub > r2, jnp.where(row >= grp[g], 1.0, 0.0),
                                     jnp.where(row > grp[g], 1.0, 0.0))
                beaten[g] = beaten[g] + wins
            if e2 == N_EXPERTS // 2 - 1:
                yield
        place = jnp.concatenate(beaten, axis=0)
        expert_id = lax.broadcasted_iota(jnp.int32, (N_EXPERTS, tm), 0).astype(jnp.float32)
        onehot = jnp.where(place < TOP_K, 1.0, 0.0)
        yield
        before = (jnp.dot(onehot.astype(jnp.bfloat16), tri_ref[...], preferred_element_type=jnp.float32)
                  + count_ref[...])

        def pick(k, table):
            return jnp.sum(jnp.where(place == k, table, 0.0), axis=0, keepdims=True)

        vals = [pick(k, logits) for k in range(TOP_K)]
        exps = [jnp.exp(v - vals[0]) for v in vals]
        denom = exps[0] + exps[1] + exps[2] + exps[3]
        yield
        meta_ref[...] = jnp.zeros_like(meta_ref)
        for k in range(TOP_K):
            meta_ref[k:k + 1, :] = pick(k, expert_id)
            meta_ref[TOP_K + k:TOP_K + k + 1, :] = pick(k, before)
            meta_ref[2 * TOP_K + k:2 * TOP_K + k + 1, :] = exps[k] / denom
        count_ref[...] += jnp.sum(onehot, axis=1, keepdims=True)

    def mix_block(stages):
        x = x_ref[...]
        x_bf = x.astype(jnp.bfloat16)
        proj = []
        for c in range(IN_WIDTH // PROJ_CHUNK):
            cols = pl.ds(c * PROJ_CHUNK, PROJ_CHUNK)
            proj.append(jnp.dot(x_bf, w_in_ref[:, cols], preferred_element_type=jnp.float32) + b_in_ref[:, cols])
            next(stages, None)
        for _ in stages:
            pass

        def proj_cols(lo, hi):
            return jnp.concatenate(proj[lo // PROJ_CHUNK:hi // PROJ_CHUNK], axis=-1)

        q_all = proj_cols(0, _O_K) * (LOG2_E * HEAD_DIM ** -0.5)
        kv_all = proj_cols(_O_K, _O_U).astype(jnp.bfloat16)
        k_all, v_all = kv_all[:, :KV_WIDTH], kv_all[:, KV_WIDTH:]
        k_prev = kv_prev_ref[:, :KV_WIDTH]
        v_prev = kv_prev_ref[:, KV_WIDTH:]
        is_row0 = lax.broadcasted_iota(jnp.int32, (ATTN_BLOCK, KV_WIDTH), 0) == 0
        attn_blocks = []
        for sb in range(n_sub):
            rows = slice(sb * ATTN_BLOCK, (sb + 1) * ATTN_BLOCK)
            k_cur, v_cur = k_all[rows], v_all[rows]
            kb = jnp.concatenate([jnp.where(is_row0, 0, k_prev), k_cur], axis=0)
            vb = jnp.concatenate([jnp.where(is_row0, 0, v_prev), v_cur], axis=0)
            bias_sel = jnp.where(first_step, 1, 0) if sb == 0 else 0
            attn_blocks.append(_attention_block(q_all[rows], kb, vb, bias_ref, bias_sel))
            k_prev, v_prev = k_cur, v_cur
        kv_prev_ref[:, :KV_WIDTH] = k_prev
        kv_prev_ref[:, KV_WIDTH:] = v_prev
        attn = jnp.concatenate(attn_blocks, axis=0)

        u = jax.nn.gelu(proj_cols(_O_U, _O_G))
        gg = jax.nn.gelu(proj_cols(_O_G, IN_WIDTH))
        avg = grp_avg_ref[...]
        mu = jnp.dot(gg.astype(jnp.bfloat16), avg, preferred_element_type=jnp.float32)
        gc = gg - mu
        var = jnp.dot((gc * gc).astype(jnp.bfloat16), avg, preferred_element_type=jnp.float32)
        gn = (gc * lax.rsqrt(var + LN_EPS) * lnv_g_ref[...] + lnv_b_ref[...]).astype(jnp.bfloat16)
        causal = (lax.broadcasted_iota(jnp.int32, (ATTN_BLOCK, ATTN_BLOCK), 0)
                  >= lax.broadcasted_iota(jnp.int32, (ATTN_BLOCK, ATTN_BLOCK), 1))
        w_sp = [jnp.where(causal, w_sp_ref[g], 0.0).astype(jnp.bfloat16) for g in range(N_GMLP_GROUPS)]
        mixed_chunks = []
        for c in range(n_sub):
            rows = slice(c * ATTN_BLOCK, (c + 1) * ATTN_BLOCK)
            pieces = [
                jnp.dot(w_sp[g], gn[rows, g * GMLP_GROUP_DIM:(g + 1) * GMLP_GROUP_DIM],
                        preferred_element_type=jnp.float32)
                for g in range(N_GMLP_GROUPS)]
            mixed_chunks.append(jnp.concatenate(pieces, axis=-1) + b_sp_ref[...])
        sgu = u * jnp.concatenate(mixed_chunks, axis=0)

        mix = (jnp.dot(attn.astype(jnp.bfloat16), w_out_ref[:ATTN_WIDTH, :], preferred_element_type=jnp.float32)
               + jnp.dot(sgu.astype(jnp.bfloat16), w_out_ref[ATTN_WIDTH:, :], preferred_element_type=jnp.float32)
               + b_out_ref[...])
        hres_ref[...] = DEEPNORM_ALPHA * x + mix

    @pl.when(i == 0)
    def _():
        mix_block(iter(()))

    @pl.when(jnp.logical_and(i > 0, i < n_steps))
    def _():
        mix_block(route_stages())

    @pl.when(i == n_steps)
    def _():
        for _ in route_stages():
            pass


def _score_bias(sinks):
    t_idx = jnp.arange(Q_REP * ATTN_BLOCK)[:, None] % ATTN_BLOCK
    s_idx = jnp.arange(2 * ATTN_BLOCK)[None, :]
    diff = t_idx + ATTN_BLOCK - s_idx
    band = (diff >= 0) & (diff < ATTN_BLOCK)
    masks = jnp.stack([band, band & (s_idx >= ATTN_BLOCK)])
    bias = jnp.where(masks, 0.0, NEG_INF).astype(jnp.float32)
    sink_rows = jnp.repeat(sinks.reshape(N_KV_HEADS, Q_REP) * LOG2_E, ATTN_BLOCK, axis=1)
    return jnp.where(s_idx == 0, sink_rows[:, None, :, None], bias[None])


def _mixer(x2, sinks, w_in, b_in, lnv_g, lnv_b, w_sp, b_sp_full, w_out, b_out, ln1_g, ln1_b,
           w_r, w_r_hi, b_r, seq_len):
    t = x2.shape[0]
    tm = MIXER_ROWS
    n_steps = t // tm
    grp = jnp.arange(GMLP_WIDTH) // GMLP_GROUP_DIM
    grp_avg = jnp.where(grp[:, None] == grp[None, :], 1.0 / GMLP_GROUP_DIM, 0.0).astype(jnp.bfloat16)
    tri = (jnp.arange(tm)[:, None] < jnp.arange(tm)[None, :]).astype(jnp.bfloat16)

    def full(shape):
        return pl.BlockSpec(shape, lambda i: (0,) * len(shape))

    return pl.pallas_call(
        functools.partial(_mixer_kernel, steps_per_seq=seq_len // tm, n_steps=n_steps),
        grid=(n_steps + 1,),
        in_specs=[
            pl.BlockSpec((tm, D_MODEL), lambda i: (jnp.minimum(i, n_steps - 1), 0)),
            full((D_MODEL, IN_WIDTH)), full((1, IN_WIDTH)),
            full((N_KV_HEADS, 2, Q_REP * ATTN_BLOCK, 2 * ATTN_BLOCK)),
            full((1, GMLP_WIDTH)), full((1, GMLP_WIDTH)), full((GMLP_WIDTH, GMLP_WIDTH)),
            full((N_GMLP_GROUPS, ATTN_BLOCK, ATTN_BLOCK)), full((ATTN_BLOCK, GMLP_WIDTH)),
            full((D_MODEL, D_MODEL)), full((1, D_MODEL)), full((1, D_MODEL)), full((1, D_MODEL)),
            full((D_MODEL, ROUTER_LANES)), full((D_MODEL, ROUTER_LANES)), full((N_EXPERTS, 1)), full((tm, tm)),
        ],
        out_specs=[
            pl.BlockSpec((tm, D_MODEL), lambda i: (jnp.maximum(i - 1, 0), 0)),
            pl.BlockSpec((tm, D_MODEL // 2), lambda i: (jnp.maximum(i - 1, 0), 0)),
            pl.BlockSpec((META_ROWS, tm), lambda i: (0, jnp.maximum(i - 1, 0))),
            pl.BlockSpec((N_EXPERTS, 1), lambda i: (0, 0)),
        ],
        out_shape=[
            jax.ShapeDtypeStruct((t, D_MODEL), jnp.float32),
            jax.ShapeDtypeStruct((t, D_MODEL // 2), jnp.int32),
            jax.ShapeDtypeStruct((META_ROWS, t), jnp.float32),
            jax.ShapeDtypeStruct((N_EXPERTS, 1), jnp.float32),
        ],
        scratch_shapes=[pltpu.VMEM((ATTN_BLOCK, 2 * KV_WIDTH), jnp.bfloat16),
                        pltpu.VMEM((tm, D_MODEL), jnp.float32)],
        compiler_params=pltpu.CompilerParams(
            dimension_semantics=("arbitrary",), vmem_limit_bytes=VMEM_LIMIT_BYTES),
        name="mixer",
    )(x2, w_in, b_in, _score_bias(sinks), lnv_g, lnv_b, grp_avg, w_sp, b_sp_full, w_out, b_out, ln1_g, ln1_b,
      w_r, w_r_hi, b_r, tri)


def _gather_rows(idx, src):
    n = idx.shape[0]
    width = src.shape[1]
    win = GATHER_WINDOW
    sc = plsc.get_sparse_core_info()
    n_workers = sc.num_cores * sc.num_subcores
    per_worker = n // n_workers
    n_pairs = per_worker // (2 * win)
    assert n_pairs * 2 * win * n_workers == n
    mesh = plsc.VectorSubcoreMesh(core_axis_name="core", subcore_axis_name="subcore")

    @functools.partial(
        pl.kernel, out_type=jax.ShapeDtypeStruct((n, width), src.dtype), mesh=mesh,
        scratch_types=[pltpu.VMEM((per_worker,), jnp.int32), pltpu.VMEM((2, win, width), src.dtype),
                       pltpu.SemaphoreType.DMA((2,)), pltpu.SemaphoreType.DMA((2,))],
        name="gather_rows")
    def gather(src_hbm, idx_hbm, out_hbm, idx_v, rows_v, fetch_sem, store_sem):
        worker = lax.axis_index("subcore") * sc.num_cores + lax.axis_index("core")
        base = worker * per_worker
        pltpu.sync_copy(idx_hbm.at[pl.ds(base, per_worker)], idx_v)

        def fetch(chunk, buf):
            return pltpu.make_async_copy(src_hbm.at[idx_v.at[pl.ds(chunk * win, win)]], rows_v.at[buf],
                                         fetch_sem.at[buf])

        def store(chunk, buf):
            return pltpu.make_async_copy(rows_v.at[buf], out_hbm.at[pl.ds(base + chunk * win, win)],
                                         store_sem.at[buf])

        @pl.loop(0, n_pairs)
        def _(p):
            for buf in range(2):
                @pl.when(p > 0)
                def _():
                    store(2 * p - 2 + buf, buf).wait()
                fetch(2 * p + buf, buf).start()
            for buf in range(2):
                fetch(2 * p + buf, buf).wait()
                store(2 * p + buf, buf).start()

        for buf in range(2):
            store(2 * n_pairs - 2 + buf, buf).wait()

    return gather(src, idx)


def _dispatch_rows(dest_t, src, n_rows):
    t, width = src.shape
    win = GATHER_WINDOW
    sc = plsc.get_sparse_core_info()
    n_workers = sc.num_cores * sc.num_subcores
    per_worker = t // n_workers
    n_chunks = per_worker // win
    n_pairs = n_chunks // 2
    assert n_pairs * 2 * win * n_workers == t
    idx = dest_t.reshape(TOP_K, n_workers, n_chunks, win).transpose(1, 0, 2, 3)
    idx = idx.reshape(n_workers, TOP_K * n_chunks, win)
    mesh = plsc.VectorSubcoreMesh(core_axis_name="core", subcore_axis_name="subcore")

    @functools.partial(
        pl.kernel, out_type=jax.ShapeDtypeStruct((n_rows, width), src.dtype), mesh=mesh,
        scratch_types=[pltpu.VMEM((TOP_K * n_chunks, win), jnp.int32), pltpu.VMEM((2, win, width), src.dtype),
                       pltpu.SemaphoreType.DMA((2,)), pltpu.SemaphoreType.DMA((2,))],
        name="dispatch_rows")
    def dispatch(src_hbm, idx_hbm, out_hbm, idx_v, rows_v, fetch_sem, store_sem):
        worker = lax.axis_index("subcore") * sc.num_cores + lax.axis_index("core")
        base = worker * per_worker
        pltpu.sync_copy(idx_hbm.at[worker], idx_v)

        def fetch(chunk, buf):
            return pltpu.make_async_copy(src_hbm.at[pl.ds(base + chunk * win, win)], rows_v.at[buf],
                                         fetch_sem.at[buf])

        def store(chunk, k, buf):
            return pltpu.make_async_copy(rows_v.at[buf], out_hbm.at[idx_v.at[k * n_chunks + chunk]],
                                         store_sem.at[buf])

        @pl.loop(0, n_pairs)
        def _(p):
            for buf in range(2):
                @pl.when(p > 0)
                def _():
                    for k in range(TOP_K):
                        store(2 * p - 2 + buf, k, buf).wait()
                fetch(2 * p + buf, buf).start()
            for buf in range(2):
                fetch(2 * p + buf, buf).wait()
                for k in range(TOP_K):
                    store(2 * p + buf, k, buf).start()

        for buf in range(2):
            for k in range(TOP_K):
                store(2 * n_pairs - 2 + buf, k, buf).wait()

    return dispatch(src, idx)


def _expert_kernel(block_e_ref, n_valid_ref, first_ref, next_e_ref,
                   x_ref, wg_hbm, bg_ref, wu_hbm, bu_ref, wd_hbm, bd_ref,
                   y_ref, wg_f32, wu_f32, wd_f32, wg_bf, wu_bf, wd_bf, sem):
    i = pl.program_id(0)
    n_valid = n_valid_ref[i]
    staged = ((wg_hbm, wg_f32, wg_bf), (wu_hbm, wu_f32, wu_bf), (wd_hbm, wd_f32, wd_bf))

    def weight_copy(m, expert):
        return pltpu.make_async_copy(staged[m][0].at[expert], staged[m][1], sem.at[m])

    @pl.when(i == 0)
    def _():
        for m in range(3):
            weight_copy(m, block_e_ref[0]).start()

    @pl.when(first_ref[i] == 1)
    def _():
        for m in range(3):
            weight_copy(m, block_e_ref[i]).wait()
            staged[m][2][...] = staged[m][1][...].astype(jnp.bfloat16)

            @pl.when(next_e_ref[i] >= 0)
            def _():
                weight_copy(m, next_e_ref[i]).start()

    def expert_mlp(rows):
        row = lax.broadcasted_iota(jnp.int32, (rows, 1), 0)
        x_hi, x_lo = _unpack_bf16_pairs(jnp.where(row < n_valid, x_ref[:rows, :], 0))
        xb = jnp.concatenate([x_hi, x_lo], axis=1).astype(jnp.bfloat16)
        gt = jnp.minimum(jnp.dot(xb, wg_bf[...], preferred_element_type=jnp.float32) + bg_ref[0], SWIGLU_LIMIT)
        up = jnp.clip(jnp.dot(xb, wu_bf[...], preferred_element_type=jnp.float32) + bu_ref[0],
                      -SWIGLU_LIMIT, SWIGLU_LIMIT)
        hid = gt * jax.nn.sigmoid(SWIGLU_ALPHA * gt) * (up + 1.0)
        y = jnp.dot(hid.astype(jnp.bfloat16), wd_bf[...], preferred_element_type=jnp.float32) + bd_ref[0]
        y_ref[:rows, :] = _pack_bf16_pairs(y.astype(jnp.bfloat16))
        if rows < y_ref.shape[0]:
            y_ref[rows:, :] = jnp.zeros((y_ref.shape[0] - rows, y_ref.shape[1]), y_ref.dtype)

    row_options = (0,) + EXPERT_PARTIAL_ROWS + (x_ref.shape[0],)
    for lo, hi in zip(row_options[:-1], row_options[1:]):
        pl.when(jnp.logical_and(n_valid > lo, n_valid <= hi))(functools.partial(expert_mlp, hi))

    @pl.when(n_valid == 0)
    def _():
        y_ref[...] = jnp.zeros_like(y_ref)


def _experts(block_e, n_valid, x_rows, w_gate, b_gate, w_up, b_up, w_down, b_down):
    n_rows = x_rows.shape[0]
    bm = EXPERT_ROWS
    n_blocks = n_rows // bm
    d_ff = w_gate.shape[2]

    is_first = jnp.concatenate([jnp.ones((1,), jnp.int32), (block_e[1:] != block_e[:-1]).astype(jnp.int32)])
    next_e = jnp.min(jnp.where(block_e[None, :] > block_e[:, None], block_e[None, :], N_EXPERTS), axis=1)
    next_e = jnp.where(next_e == N_EXPERTS, -1, next_e)

    def by_expert(shape):
        return pl.BlockSpec(shape, lambda i, be, *_: (be[i],) + (0,) * (len(shape) - 1))

    hbm = pl.BlockSpec(memory_space=pl.ANY)
    return pl.pallas_call(
        _expert_kernel,
        grid_spec=pltpu.PrefetchScalarGridSpec(
            num_scalar_prefetch=4,
            grid=(n_blocks,),
            in_specs=[
                pl.BlockSpec((bm, D_MODEL // 2), lambda i, *_: (i, 0)),
                hbm, by_expert((1, 1, d_ff)),
                hbm, by_expert((1, 1, d_ff)),
                hbm, by_expert((1, 1, D_MODEL)),
            ],
            out_specs=pl.BlockSpec((bm, D_MODEL // 2), lambda i, *_: (i, 0)),
            scratch_shapes=[
                pltpu.VMEM((D_MODEL, d_ff), jnp.float32),
                pltpu.VMEM((D_MODEL, d_ff), jnp.float32),
                pltpu.VMEM((d_ff, D_MODEL), jnp.float32),
                pltpu.VMEM((D_MODEL, d_ff), jnp.bfloat16),
                pltpu.VMEM((D_MODEL, d_ff), jnp.bfloat16),
                pltpu.VMEM((d_ff, D_MODEL), jnp.bfloat16),
                pltpu.SemaphoreType.DMA((3,)),
            ],
        ),
        out_shape=jax.ShapeDtypeStruct((n_rows, D_MODEL // 2), jnp.int32),
        compiler_params=pltpu.CompilerParams(
            dimension_semantics=("arbitrary",), vmem_limit_bytes=VMEM_LIMIT_BYTES),
        name="experts",
    )(block_e, n_valid, is_first, next_e, x_rows, w_gate, b_gate.reshape(N_EXPERTS, 1, d_ff), w_up,
      b_up.reshape(N_EXPERTS, 1, d_ff), w_down, b_down.reshape(N_EXPERTS, 1, D_MODEL))


def _combine_kernel(h_ref, y_ref, meta_ref, g_ref, b_ref, o_ref):
    tm = h_ref.shape[0]
    meta_cols = jnp.concatenate([meta_ref[...], jnp.zeros((LANES - META_ROWS, tm), jnp.float32)], axis=0).T
    ffn_hi, ffn_lo = 0.0, 0.0
    for k in range(TOP_K):
        gate = meta_cols[:, 2 * TOP_K + k:2 * TOP_K + k + 1]
        y_hi, y_lo = _unpack_bf16_pairs(y_ref[k])
        ffn_hi = ffn_hi + gate * y_hi
        ffn_lo = ffn_lo + gate * y_lo
    ffn = jnp.concatenate([ffn_hi, ffn_lo], axis=1)
    o_ref[...] = _layer_norm(DEEPNORM_ALPHA * h_ref[...] + ffn, g_ref[...], b_ref[...])


def _combine(h, y_tok, meta, ln2_g, ln2_b):
    t = h.shape[0]
    tm = COMBINE_ROWS
    return pl.pallas_call(
        _combine_kernel,
        grid=(t // tm,),
        in_specs=[
            pl.BlockSpec((tm, D_MODEL), lambda i: (i, 0)),
            pl.BlockSpec((TOP_K, tm, D_MODEL // 2), lambda i: (0, i, 0)),
            pl.BlockSpec((META_ROWS, tm), lambda i: (0, i)),
            pl.BlockSpec((1, D_MODEL), lambda i: (0, 0)),
            pl.BlockSpec((1, D_MODEL), lambda i: (0, 0)),
        ],
        out_specs=pl.BlockSpec((tm, D_MODEL), lambda i: (i, 0)),
        out_shape=jax.ShapeDtypeStruct((t, D_MODEL), jnp.float32),
        compiler_params=pltpu.CompilerParams(
            dimension_semantics=("arbitrary",), vmem_limit_bytes=VMEM_LIMIT_BYTES),
        name="combine",
    )(h, y_tok, meta, ln2_g, ln2_b)


def _layer(x2, seq_len, w_in, b_in, sinks, ln_v_g, ln_v_b, w_spatial, b_spatial, w_out, b_out,
           ln1_g, ln1_b, w_router, b_router, w_gate, b_gate, w_up, b_up, w_down, b_down, ln2_g, ln2_b):
    t = x2.shape[0]
    tk = t * TOP_K
    bm = EXPERT_ROWS
    bf16 = jnp.bfloat16

    w_r_hi = w_router.astype(bf16)
    w_r_lo = (w_router - w_r_hi.astype(jnp.float32)).astype(bf16)
    lane_pad = jnp.zeros((D_MODEL, ROUTER_LANES - 2 * N_EXPERTS), bf16)
    w_r = jnp.concatenate([w_r_hi, w_r_lo, lane_pad], axis=1)
    w_r_hi_only = jnp.concatenate([w_r_hi, jnp.zeros_like(w_r_lo), lane_pad], axis=1)
    b_sp_full = jnp.repeat(b_spatial.T, GMLP_GROUP_DIM, axis=1)

    h, h_packed, meta, counts = _mixer(
        x2, sinks, w_in.astype(bf16), b_in[None], ln_v_g[None], ln_v_b[None], w_spatial, b_sp_full,
        w_out.astype(bf16), b_out[None], ln1_g[None], ln1_b[None], w_r, w_r_hi_only, b_router[:, None], seq_len)

    counts = counts[:, 0].astype(jnp.int32)
    experts = jnp.arange(N_EXPERTS, dtype=jnp.int32)
    padded = (counts + bm - 1) // bm * bm
    padded_end = jnp.sum(jnp.where(experts[None, :] <= experts[:, None], padded[None, :], 0), axis=1)
    padded_start = padded_end - padded
    n_blocks = tk // bm + N_EXPERTS
    n_rows = n_blocks * bm
    top_idx_t = meta[:TOP_K].astype(jnp.int32)
    rank_t = meta[TOP_K:2 * TOP_K].astype(jnp.int32)
    dest_t = rank_t + jnp.sum(
        jnp.where(top_idx_t[None] == experts[:, None, None], padded_start[:, None, None], 0), axis=0)
    block_start = jnp.arange(n_blocks, dtype=jnp.int32) * bm
    block_e = jnp.minimum(
        jnp.sum((padded_end[None, :] <= block_start[:, None]).astype(jnp.int32), axis=1), N_EXPERTS - 1)
    valid_end = jnp.sum(jnp.where(block_e[:, None] == experts[None, :], (padded_start + counts)[None, :], 0), axis=1)
    n_valid = jnp.clip(valid_end - block_start, 0, bm)

    x_rows = _dispatch_rows(dest_t, h_packed, n_rows)
    y_rows = _experts(block_e, n_valid, x_rows, w_gate, b_gate, w_up, b_up, w_down, b_down)
    y_tok = _gather_rows(dest_t.reshape(-1), y_rows).reshape(TOP_K, t, D_MODEL // 2)
    return _combine(h, y_tok, meta, ln2_g[None], ln2_b[None])


def kernel(x, w_in, b_in, sinks, ln_v_g, ln_v_b, w_spatial, b_spatial, w_out, b_out, ln1_g, ln1_b,
           w_router, b_router, w_gate, b_gate, w_up, b_up, w_down, b_down, ln2_g, ln2_b):
    batch, seq_len, d = x.shape
    x2 = x.reshape(batch * seq_len, d)
    for l in range(DEPTH):
        x2 = _layer(x2, seq_len, w_in[l], b_in[l], sinks[l], ln_v_g[l], ln_v_b[l], w_spatial[l],
                    b_spatial[l], w_out[l], b_out[l], ln1_g[l], ln1_b[l], w_router[l], b_router[l],
                    w_gate[l], b_gate[l], w_up[l], b_up[l], w_down[l], b_down[l], ln2_g[l], ln2_b[l])
    return x2.reshape(batch, seq_len, d)
```

```python
import functools

import jax
import jax.numpy as jnp
from jax import lax
from jax.experimental import pallas as pl
from jax.experimental.pallas import tpu as pltpu
from jax.experimental.pallas import tpu_sc as plsc

D_MODEL = 1024
HEAD_DIM = 64
N_Q_HEADS = 8
N_KV_HEADS = 2
Q_REP = N_Q_HEADS // N_KV_HEADS
ATTN_WIDTH = N_Q_HEADS * HEAD_DIM
KV_WIDTH = N_KV_HEADS * HEAD_DIM
ATTN_BLOCK = 128
N_GMLP_GROUPS = 8
GMLP_WIDTH = D_MODEL - ATTN_WIDTH
GMLP_GROUP_DIM = GMLP_WIDTH // N_GMLP_GROUPS
IN_WIDTH = ATTN_WIDTH + 2 * KV_WIDTH + 2 * GMLP_WIDTH
N_EXPERTS = 32
TOP_K = 4
SWIGLU_LIMIT = 7.0
SWIGLU_ALPHA = 1.702
LN_EPS = 1e-5
DEPTH = 1
DEEPNORM_ALPHA = (2.0 * DEPTH) ** 0.25
NEG_INF = -1e30
LOG2_E = 1.4426950408889634

LANES = 128
SUBLANES = 8

MIXER_ROWS = 512
PROJ_CHUNK = 256
EXPERT_ROWS = 1024
EXPERT_PARTIAL_ROWS = (128, 256, 384, 512, 768)
GATHER_WINDOW = 64
META_ROWS = 16
ROUTER_LANES = LANES
COMBINE_ROWS = 1024
VMEM_LIMIT_BYTES = 56 * 1024 * 1024

_O_K = ATTN_WIDTH
_O_V = _O_K + KV_WIDTH
_O_U = _O_V + KV_WIDTH
_O_G = _O_U + GMLP_WIDTH


def _pack_bf16_pairs(v):
    n = v.shape[1] // 2
    hi = lax.bitcast_convert_type(v[:, :n].astype(jnp.float32), jnp.int32)
    lo = lax.bitcast_convert_type(v[:, n:].astype(jnp.float32), jnp.int32)
    return hi | lax.shift_right_logical(lo, 16)


def _unpack_bf16_pairs(p):
    hi = lax.bitcast_convert_type(p & jnp.int32(-65536), jnp.float32)
    lo = lax.bitcast_convert_type(lax.shift_left(p, 16), jnp.float32)
    return hi, lo


def _layer_norm(v, g, b):
    mu = jnp.mean(v, axis=-1, keepdims=True)
    vc = v - mu
    var = jnp.mean(vc * vc, axis=-1, keepdims=True)
    return vc * lax.rsqrt(var + LN_EPS) * g + b


def _attention_block(q, kb, vb, bias_ref, bias_sel):
    outs = []
    for g in range(N_KV_HEADS):
        kg = kb[:, g * HEAD_DIM:(g + 1) * HEAD_DIM]
        vg = vb[:, g * HEAD_DIM:(g + 1) * HEAD_DIM]
        qg = jnp.concatenate(
            [q[:, (g * Q_REP + r) * HEAD_DIM:(g * Q_REP + r + 1) * HEAD_DIM] for r in range(Q_REP)],
            axis=0).astype(jnp.bfloat16)
        s = (lax.dot_general(qg, kg, (((1,), (1,)), ((), ())), preferred_element_type=jnp.float32)
             + bias_ref[g, bias_sel])
        p = jnp.exp2(s - jnp.max(s, axis=-1, keepdims=True))
        denom = jnp.sum(p, axis=-1, keepdims=True)
        o = jnp.dot(p.astype(jnp.bfloat16), vg, preferred_element_type=jnp.float32) / denom
        outs.extend(o[r * ATTN_BLOCK:(r + 1) * ATTN_BLOCK] for r in range(Q_REP))
    return jnp.concatenate(outs, axis=-1)


def _mixer_kernel(x_ref, w_in_ref, b_in_ref, bias_ref, lnv_g_ref, lnv_b_ref, grp_avg_ref,
                  w_sp_ref, b_sp_ref, w_out_ref, b_out_ref, ln1_g_ref, ln1_b_ref,
                  w_r_ref, w_r_hi_ref, b_r_ref, tri_ref,
                  h_ref, hp_ref, meta_ref, count_ref,
                  kv_prev_ref, hres_ref, *, steps_per_seq, n_steps):
    i = pl.program_id(0)
    first_step = (i % steps_per_seq) == 0
    tm = x_ref.shape[0]
    n_sub = tm // ATTN_BLOCK

    @pl.when(i == 0)
    def _():
        count_ref[...] = jnp.zeros_like(count_ref)

    @pl.when(first_step)
    def _():
        kv_prev_ref[...] = jnp.zeros_like(kv_prev_ref)

    def route_stages():
        h = _layer_norm(hres_ref[...], ln1_g_ref[...], ln1_b_ref[...])
        h_ref[...] = h
        yield
        h_hi = h.astype(jnp.bfloat16)
        hp_ref[...] = _pack_bf16_pairs(h_hi)
        h_lo = (h - h_hi.astype(jnp.float32)).astype(jnp.bfloat16)
        yield
        part = (jnp.dot(h_hi, w_r_ref[...], preferred_element_type=jnp.float32)
                + jnp.dot(h_lo, w_r_hi_ref[...], preferred_element_type=jnp.float32)).T
        logits = part[:N_EXPERTS] + part[N_EXPERTS:2 * N_EXPERTS] + b_r_ref[...]
        yield
        n_grp = N_EXPERTS // SUBLANES
        grp = [logits[SUBLANES * g:SUBLANES * (g + 1)] for g in range(n_grp)]
        sub = lax.broadcasted_iota(jnp.int32, (SUBLANES, tm), 0)
        beaten = [jnp.zeros((SUBLANES, tm), jnp.float32) for _ in range(n_grp)]
        for e2 in range(N_EXPERTS):
            g2, r2 = divmod(e2, SUBLANES)
            row = logits[e2:e2 + 1]
            for g in range(n_grp):
                if g > g2:
                    wins = jnp.where(row >= grp[g], 1.0, 0.0)
                elif g < g2:
                    wins = jnp.where(row > grp[g], 1.0, 0.0)
                else:
                    wins = jnp.where(sub > r2, jnp.where(row >= grp[g], 1.0, 0.0),
                                     jnp.where(row > grp[g], 1.0, 0.0))
                beaten[g] = beaten[g] + wins
            if e2 == N_EXPERTS // 2 - 1:
                yield
        place = jnp.concatenate(beaten, axis=0)
        expert_id = lax.broadcasted_iota(jnp.int32, (N_EXPERTS, tm), 0).astype(jnp.float32)
        onehot = jnp.where(place < TOP_K, 1.0, 0.0)
        yield
        before = (jnp.dot(onehot.astype(jnp.bfloat16), tri_ref[...], preferred_element_type=jnp.float32)
                  + count_ref[...])

        def pick(k, table):
            return jnp.sum(jnp.where(place == k, table, 0.0), axis=0, keepdims=True)

        vals = [pick(k, logits) for k in range(TOP_K)]
        exps = [jnp.exp(v - vals[0]) for v in vals]
        denom = exps[0] + exps[1] + exps[2] + exps[3]
        yield
        meta_ref[...] = jnp.zeros_like(meta_ref)
        for k in range(TOP_K):
            meta_ref[k:k + 1, :] = pick(k, expert_id)
            meta_ref[TOP_K + k:TOP_K + k + 1, :] = pick(k, before)
            meta_ref[2 * TOP_K + k:2 * TOP_K + k + 1, :] = exps[k] / denom
        count_ref[...] += jnp.sum(onehot, axis=1, keepdims=True)

    def mix_block(stages):
        x = x_ref[...]
        x_bf = x.astype(jnp.bfloat16)
        proj = []
        for c in range(IN_WIDTH // PROJ_CHUNK):
            cols = pl.ds(c * PROJ_CHUNK, PROJ_CHUNK)
            proj.append(jnp.dot(x_bf, w_in_ref[:, cols], preferred_element_type=jnp.float32) + b_in_ref[:, cols])
            next(stages, None)
        for _ in stages:
            pass

        def proj_cols(lo, hi):
            return jnp.concatenate(proj[lo // PROJ_CHUNK:hi // PROJ_CHUNK], axis=-1)

        q_all = proj_cols(0, _O_K) * (LOG2_E * HEAD_DIM ** -0.5)
        kv_all = proj_cols(_O_K, _O_U).astype(jnp.bfloat16)
        k_all, v_all = kv_all[:, :KV_WIDTH], kv_all[:, KV_WIDTH:]
        k_prev = kv_prev_ref[:, :KV_WIDTH]
        v_prev = kv_prev_ref[:, KV_WIDTH:]
        is_row0 = lax.broadcasted_iota(jnp.int32, (ATTN_BLOCK, KV_WIDTH), 0) == 0
        attn_blocks = []
        for sb in range(n_sub):
            rows = slice(sb * ATTN_BLOCK, (sb + 1) * ATTN_BLOCK)
            k_cur, v_cur = k_all[rows], v_all[rows]
            kb = jnp.concatenate([jnp.where(is_row0, 0, k_prev), k_cur], axis=0)
            vb = jnp.concatenate([jnp.where(is_row0, 0, v_prev), v_cur], axis=0)
            bias_sel = jnp.where(first_step, 1, 0) if sb == 0 else 0
            attn_blocks.append(_attention_block(q_all[rows], kb, vb, bias_ref, bias_sel))
            k_prev, v_prev = k_cur, v_cur
        kv_prev_ref[:, :KV_WIDTH] = k_prev
        kv_prev_ref[:, KV_WIDTH:] = v_prev
        attn = jnp.concatenate(attn_blocks, axis=0)

        u = jax.nn.gelu(proj_cols(_O_U, _O_G))
        gg = jax.nn.gelu(proj_cols(_O_G, IN_WIDTH))
        avg = grp_avg_ref[...]
        mu = jnp.dot(gg.astype(jnp.bfloat16), avg, preferred_element_type=jnp.float32)
        gc = gg - mu
        var = jnp.dot((gc * gc).astype(jnp.bfloat16), avg, preferred_element_type=jnp.float32)
        gn = (gc * lax.rsqrt(var + LN_EPS) * lnv_g_ref[...] + lnv_b_ref[...]).astype(jnp.bfloat16)
        causal = (lax.broadcasted_iota(jnp.int32, (ATTN_BLOCK, ATTN_BLOCK), 0)
                  >= lax.broadcasted_iota(jnp.int32, (ATTN_BLOCK, ATTN_BLOCK), 1))
        w_sp = [jnp.where(causal, w_sp_ref[g], 0.0).astype(jnp.bfloat16) for g in range(N_GMLP_GROUPS)]
        mixed_chunks = []
        for c in range(n_sub):
            rows = slice(c * ATTN_BLOCK, (c + 1) * ATTN_BLOCK)
            pieces = [
                jnp.dot(w_sp[g], gn[rows, g * GMLP_GROUP_DIM:(g + 1) * GMLP_GROUP_DIM],
                        preferred_element_type=jnp.float32)
                for g in range(N_GMLP_GROUPS)]
            mixed_chunks.append(jnp.concatenate(pieces, axis=-1) + b_sp_ref[...])
        sgu = u * jnp.concatenate(mixed_chunks, axis=0)

        mix = (jnp.dot(attn.astype(jnp.bfloat16), w_out_ref[:ATTN_WIDTH, :], preferred_element_type=jnp.float32)
               + jnp.dot(sgu.astype(jnp.bfloat16), w_out_ref[ATTN_WIDTH:, :], preferred_element_type=jnp.float32)
               + b_out_ref[...])
        hres_ref[...] = DEEPNORM_ALPHA * x + mix

    @pl.when(i == 0)
    def _():
        mix_block(iter(()))

    @pl.when(jnp.logical_and(i > 0, i < n_steps))
    def _():
        mix_block(route_stages())

    @pl.when(i == n_steps)
    def _():
        for _ in route_stages():
            pass


def _score_bias(sinks):
    t_idx = jnp.arange(Q_REP * ATTN_BLOCK)[:, None] % ATTN_BLOCK
    s_idx = jnp.arange(2 * ATTN_BLOCK)[None, :]
    diff = t_idx + ATTN_BLOCK - s_idx
    band = (diff >= 0) & (diff < ATTN_BLOCK)
    masks = jnp.stack([band, band & (s_idx >= ATTN_BLOCK)])
    bias = jnp.where(masks, 0.0, NEG_INF).astype(jnp.float32)
    sink_rows = jnp.repeat(sinks.reshape(N_KV_HEADS, Q_REP) * LOG2_E, ATTN_BLOCK, axis=1)
    return jnp.where(s_idx == 0, sink_rows[:, None, :, None], bias[None])


def _mixer(x2, sinks, w_in, b_in, lnv_g, lnv_b, w_sp, b_sp_full, w_out, b_out, ln1_g, ln1_b,
           w_r, w_r_hi, b_r, seq_len):
    t = x2.shape[0]
    tm = MIXER_ROWS
    n_steps = t // tm
    grp = jnp.arange(GMLP_WIDTH) // GMLP_GROUP_DIM
    grp_avg = jnp.where(grp[:, None] == grp[None, :], 1.0 / GMLP_GROUP_DIM, 0.0).astype(jnp.bfloat16)
    tri = (jnp.arange(tm)[:, None] < jnp.arange(tm)[None, :]).astype(jnp.bfloat16)

    def full(shape):
        return pl.BlockSpec(shape, lambda i: (0,) * len(shape))

    return pl.pallas_call(
        functools.partial(_mixer_kernel, steps_per_seq=seq_len // tm, n_steps=n_steps),
        grid=(n_steps + 1,),
        in_specs=[
            pl.BlockSpec((tm, D_MODEL), lambda i: (jnp.minimum(i, n_steps - 1), 0)),
            full((D_MODEL, IN_WIDTH)), full((1, IN_WIDTH)),
            full((N_KV_HEADS, 2, Q_REP * ATTN_BLOCK, 2 * ATTN_BLOCK)),
            full((1, GMLP_WIDTH)), full((1, GMLP_WIDTH)), full((GMLP_WIDTH, GMLP_WIDTH)),
            full((N_GMLP_GROUPS, ATTN_BLOCK, ATTN_BLOCK)), full((ATTN_BLOCK, GMLP_WIDTH)),
            full((D_MODEL, D_MODEL)), full((1, D_MODEL)), full((1, D_MODEL)), full((1, D_MODEL)),
            full((D_MODEL, ROUTER_LANES)), full((D_MODEL, ROUTER_LANES)), full((N_EXPERTS, 1)), full((tm, tm)),
        ],
        out_specs=[
            pl.BlockSpec((tm, D_MODEL), lambda i: (jnp.maximum(i - 1, 0), 0)),
            pl.BlockSpec((tm, D_MODEL // 2), lambda i: (jnp.maximum(i - 1, 0), 0)),
            pl.BlockSpec((META_ROWS, tm), lambda i: (0, jnp.maximum(i - 1, 0))),
            pl.BlockSpec((N_EXPERTS, 1), lambda i: (0, 0)),
        ],
        out_shape=[
            jax.ShapeDtypeStruct((t, D_MODEL), jnp.float32),
            jax.ShapeDtypeStruct((t, D_MODEL // 2), jnp.int32),
            jax.ShapeDtypeStruct((META_ROWS, t), jnp.float32),
            jax.ShapeDtypeStruct((N_EXPERTS, 1), jnp.float32),
        ],
        scratch_shapes=[pltpu.VMEM((ATTN_BLOCK, 2 * KV_WIDTH), jnp.bfloat16),
                        pltpu.VMEM((tm, D_MODEL), jnp.float32)],
        compiler_params=pltpu.CompilerParams(
            dimension_semantics=("arbitrary",), vmem_limit_bytes=VMEM_LIMIT_BYTES),
        name="mixer",
    )(x2, w_in, b_in, _score_bias(sinks), lnv_g, lnv_b, grp_avg, w_sp, b_sp_full, w_out, b_out, ln1_g, ln1_b,
      w_r, w_r_hi, b_r, tri)


def _gather_rows(idx, src):
    n = idx.shape[0]
    width = src.shape[1]
    win = GATHER_WINDOW
    sc = plsc.get_sparse_core_info()
    n_workers = sc.num_cores * sc.num_subcores
    per_worker = n // n_workers
    n_pairs = per_worker // (2 * win)
    assert n_pairs * 2 * win * n_workers == n
    mesh = plsc.VectorSubcoreMesh(core_axis_name="core", subcore_axis_name="subcore")

    @functools.partial(
        pl.kernel, out_type=jax.ShapeDtypeStruct((n, width), src.dtype), mesh=mesh,
        scratch_types=[pltpu.VMEM((per_worker,), jnp.int32), pltpu.VMEM((2, win, width), src.dtype),
                       pltpu.SemaphoreType.DMA((2,)), pltpu.SemaphoreType.DMA((2,))],
        name="gather_rows")
    def gather(src_hbm, idx_hbm, out_hbm, idx_v, rows_v, fetch_sem, store_sem):
        worker = lax.axis_index("subcore") * sc.num_cores + lax.axis_index("core")
        base = worker * per_worker
        pltpu.sync_copy(idx_hbm.at[pl.ds(base, per_worker)], idx_v)

        def fetch(chunk, buf):
            return pltpu.make_async_copy(src_hbm.at[idx_v.at[pl.ds(chunk * win, win)]], rows_v.at[buf],
                                         fetch_sem.at[buf])

        def store(chunk, buf):
            return pltpu.make_async_copy(rows_v.at[buf], out_hbm.at[pl.ds(base + chunk * win, win)],
                                         store_sem.at[buf])

        @pl.loop(0, n_pairs)
        def _(p):
            for buf in range(2):
                @pl.when(p > 0)
                def _():
                    store(2 * p - 2 + buf, buf).wait()
                fetch(2 * p + buf, buf).start()
            for buf in range(2):
                fetch(2 * p + buf, buf).wait()
                store(2 * p + buf, buf).start()

        for buf in range(2):
            store(2 * n_pairs - 2 + buf, buf).wait()

    return gather(src, idx)


def _dispatch_rows(dest_t, src, n_rows):
    t, width = src.shape
    win = GATHER_WINDOW
    sc = plsc.get_sparse_core_info()
    n_workers = sc.num_cores * sc.num_subcores
    per_worker = t // n_workers
    n_chunks = per_worker // win
    n_pairs = n_chunks // 2
    assert n_pairs * 2 * win * n_workers == t
    idx = dest_t.reshape(TOP_K, n_workers, n_chunks, win).transpose(1, 0, 2, 3)
    idx = idx.reshape(n_workers, TOP_K * n_chunks, win)
    mesh = plsc.VectorSubcoreMesh(core_axis_name="core", subcore_axis_name="subcore")

    @functools.partial(
        pl.kernel, out_type=jax.ShapeDtypeStruct((n_rows, width), src.dtype), mesh=mesh,
        scratch_types=[pltpu.VMEM((TOP_K * n_chunks, win), jnp.int32), pltpu.VMEM((2, win, width), src.dtype),
                       pltpu.SemaphoreType.DMA((2,)), pltpu.SemaphoreType.DMA((2,))],
        name="dispatch_rows")
    def dispatch(src_hbm, idx_hbm, out_hbm, idx_v, rows_v, fetch_sem, store_sem):
        worker = lax.axis_index("subcore") * sc.num_cores + lax.axis_index("core")
        base = worker * per_worker
        pltpu.sync_copy(idx_hbm.at[worker], idx_v)

        def fetch(chunk, buf):
            return pltpu.make_async_copy(src_hbm.at[pl.ds(base + chunk * win, win)], rows_v.at[buf],
                                         fetch_sem.at[buf])

        def store(chunk, k, buf):
            return pltpu.make_async_copy(rows_v.at[buf], out_hbm.at[idx_v.at[k * n_chunks + chunk]],
                                         store_sem.at[buf])

        @pl.loop(0, n_pairs)
        def _(p):
            for buf in range(2):
                @pl.when(p > 0)
                def _():
                    for k in range(TOP_K):
                        store(2 * p - 2 + buf, k, buf).wait()
                fetch(2 * p + buf, buf).start()
            for buf in range(2):
                fetch(2 * p + buf, buf).wait()
                for k in range(TOP_K):
                    store(2 * p + buf, k, buf).start()

        for buf in range(2):
            for k in range(TOP_K):
                store(2 * n_pairs - 2 + buf, k, buf).wait()

    return dispatch(src, idx)


def _expert_kernel(block_e_ref, n_valid_ref, first_ref, next_e_ref, row_block_ref,
                   x_ref, wg_hbm, bg_ref, wu_hbm, bu_ref, wd_hbm, bd_ref,
                   y_ref, wg_f32, wu_f32, wd_f32, wg_bf, wu_bf, wd_bf, sem):
    i = pl.program_id(0)
    n_valid = n_valid_ref[i]
    staged = ((wg_hbm, wg_f32, wg_bf), (wu_hbm, wu_f32, wu_bf), (wd_hbm, wd_f32, wd_bf))

    def weight_copy(m, expert):
        return pltpu.make_async_copy(staged[m][0].at[expert], staged[m][1], sem.at[m])

    @pl.when(i == 0)
    def _():
        for m in range(3):
            weight_copy(m, block_e_ref[0]).start()

    @pl.when(first_ref[i] == 1)
    def _():
        for m in range(3):
            weight_copy(m, block_e_ref[i]).wait()
            staged[m][2][...] = staged[m][1][...].astype(jnp.bfloat16)

            @pl.when(next_e_ref[i] >= 0)
            def _():
                weight_copy(m, next_e_ref[i]).start()

    def expert_mlp(rows):
        row = lax.broadcasted_iota(jnp.int32, (rows, 1), 0)
        x_hi, x_lo = _unpack_bf16_pairs(jnp.where(row < n_valid, x_ref[:rows, :], 0))
        xb = jnp.concatenate([x_hi, x_lo], axis=1).astype(jnp.bfloat16)
        gt = jnp.minimum(jnp.dot(xb, wg_bf[...], preferred_element_type=jnp.float32) + bg_ref[0], SWIGLU_LIMIT)
        up = jnp.clip(jnp.dot(xb, wu_bf[...], preferred_element_type=jnp.float32) + bu_ref[0],
                      -SWIGLU_LIMIT, SWIGLU_LIMIT)
        hid = gt * jax.nn.sigmoid(SWIGLU_ALPHA * gt) * (up + 1.0)
        y = jnp.dot(hid.astype(jnp.bfloat16), wd_bf[...], preferred_element_type=jnp.float32) + bd_ref[0]
        y_ref[:rows, :] = _pack_bf16_pairs(y.astype(jnp.bfloat16))
        if rows < y_ref.shape[0]:
            y_ref[rows:, :] = jnp.zeros((y_ref.shape[0] - rows, y_ref.shape[1]), y_ref.dtype)

    row_options = (0,) + EXPERT_PARTIAL_ROWS + (x_ref.shape[0],)
    for lo, hi in zip(row_options[:-1], row_options[1:]):
        pl.when(jnp.logical_and(n_valid > lo, n_valid <= hi))(functools.partial(expert_mlp, hi))


def _experts(block_e, n_valid, x_rows, w_gate, b_gate, w_up, b_up, w_down, b_down):
    n_rows = x_rows.shape[0]
    bm = EXPERT_ROWS
    n_blocks = n_rows // bm
    d_ff = w_gate.shape[2]

    blocks = jnp.arange(n_blocks, dtype=jnp.int32)
    last_used = jnp.sum((n_valid > 0).astype(jnp.int32)) - 1
    row_block = jnp.minimum(blocks, last_used)
    block_e = jnp.sum(jnp.where(blocks[None, :] == row_block[:, None], block_e[None, :], 0), axis=1)

    is_first = jnp.concatenate([jnp.ones((1,), jnp.int32), (block_e[1:] != block_e[:-1]).astype(jnp.int32)])
    next_e = jnp.min(jnp.where(block_e[None, :] > block_e[:, None], block_e[None, :], N_EXPERTS), axis=1)
    next_e = jnp.where(next_e == N_EXPERTS, -1, next_e)

    def by_expert(shape):
        return pl.BlockSpec(shape, lambda i, be, *_: (be[i],) + (0,) * (len(shape) - 1))

    hbm = pl.BlockSpec(memory_space=pl.ANY)
    return pl.pallas_call(
        _expert_kernel,
        grid_spec=pltpu.PrefetchScalarGridSpec(
            num_scalar_prefetch=5,
            grid=(n_blocks,),
            in_specs=[
                pl.BlockSpec((bm, D_MODEL // 2), lambda i, be, nv, fi, ne, rb: (rb[i], 0)),
                hbm, by_expert((1, 1, d_ff)),
                hbm, by_expert((1, 1, d_ff)),
                hbm, by_expert((1, 1, D_MODEL)),
            ],
            out_specs=pl.BlockSpec((bm, D_MODEL // 2), lambda i, be, nv, fi, ne, rb: (rb[i], 0)),
            scratch_shapes=[
                pltpu.VMEM((D_MODEL, d_ff), jnp.float32),
                pltpu.VMEM((D_MODEL, d_ff), jnp.float32),
                pltpu.VMEM((d_ff, D_MODEL), jnp.float32),
                pltpu.VMEM((D_MODEL, d_ff), jnp.bfloat16),
                pltpu.VMEM((D_MODEL, d_ff), jnp.bfloat16),
                pltpu.VMEM((d_ff, D_MODEL), jnp.bfloat16),
                pltpu.SemaphoreType.DMA((3,)),
            ],
        ),
        out_shape=jax.ShapeDtypeStruct((n_rows, D_MODEL // 2), jnp.int32),
        compiler_params=pltpu.CompilerParams(
            dimension_semantics=("arbitrary",), vmem_limit_bytes=VMEM_LIMIT_BYTES),
        name="experts",
    )(block_e, n_valid, is_first, next_e, row_block, x_rows, w_gate, b_gate.reshape(N_EXPERTS, 1, d_ff), w_up,
      b_up.reshape(N_EXPERTS, 1, d_ff), w_down, b_down.reshape(N_EXPERTS, 1, D_MODEL))


def _combine_kernel(h_ref, y_ref, meta_ref, g_ref, b_ref, o_ref):
    tm = h_ref.shape[0]
    meta_cols = jnp.concatenate([meta_ref[...], jnp.zeros((LANES - META_ROWS, tm), jnp.float32)], axis=0).T
    ffn_hi, ffn_lo = 0.0, 0.0
    for k in range(TOP_K):
        gate = meta_cols[:, 2 * TOP_K + k:2 * TOP_K + k + 1]
        y_hi, y_lo = _unpack_bf16_pairs(y_ref[k])
        ffn_hi = ffn_hi + gate * y_hi
        ffn_lo = ffn_lo + gate * y_lo
    ffn = jnp.concatenate([ffn_hi, ffn_lo], axis=1)
    o_ref[...] = _layer_norm(DEEPNORM_ALPHA * h_ref[...] + ffn, g_ref[...], b_ref[...])


def _combine(h, y_tok, meta, ln2_g, ln2_b):
    t = h.shape[0]
    tm = COMBINE_ROWS
    return pl.pallas_call(
        _combine_kernel,
        grid=(t // tm,),
        in_specs=[
            pl.BlockSpec((tm, D_MODEL), lambda i: (i, 0)),
            pl.BlockSpec((TOP_K, tm, D_MODEL // 2), lambda i: (0, i, 0)),
            pl.BlockSpec((META_ROWS, tm), lambda i: (0, i)),
            pl.BlockSpec((1, D_MODEL), lambda i: (0, 0)),
            pl.BlockSpec((1, D_MODEL), lambda i: (0, 0)),
        ],
        out_specs=pl.BlockSpec((tm, D_MODEL), lambda i: (i, 0)),
        out_shape=jax.ShapeDtypeStruct((t, D_MODEL), jnp.float32),
        compiler_params=pltpu.CompilerParams(
            dimension_semantics=("arbitrary",), vmem_limit_bytes=VMEM_LIMIT_BYTES),
        name="combine",
    )(h, y_tok, meta, ln2_g, ln2_b)


def _layer(x2, seq_len, w_in, b_in, sinks, ln_v_g, ln_v_b, w_spatial, b_spatial, w_out, b_out,
           ln1_g, ln1_b, w_router, b_router, w_gate, b_gate, w_up, b_up, w_down, b_down, ln2_g, ln2_b):
    t = x2.shape[0]
    tk = t * TOP_K
    bm = EXPERT_ROWS
    bf16 = jnp.bfloat16

    w_r_hi = w_router.astype(bf16)
    w_r_lo = (w_router - w_r_hi.astype(jnp.float32)).astype(bf16)
    lane_pad = jnp.zeros((D_MODEL, ROUTER_LANES - 2 * N_EXPERTS), bf16)
    w_r = jnp.concatenate([w_r_hi, w_r_lo, lane_pad], axis=1)
    w_r_hi_only = jnp.concatenate([w_r_hi, jnp.zeros_like(w_r_lo), lane_pad], axis=1)
    b_sp_full = jnp.repeat(b_spatial.T, GMLP_GROUP_DIM, axis=1)

    h, h_packed, meta, counts = _mixer(
        x2, sinks, w_in.astype(bf16), b_in[None], ln_v_g[None], ln_v_b[None], w_spatial, b_sp_full,
        w_out.astype(bf16), b_out[None], ln1_g[None], ln1_b[None], w_r, w_r_hi_only, b_router[:, None], seq_len)

    counts = counts[:, 0].astype(jnp.int32)
    experts = jnp.arange(N_EXPERTS, dtype=jnp.int32)
    padded = (counts + bm - 1) // bm * bm
    padded_end = jnp.sum(jnp.where(experts[None, :] <= experts[:, None], padded[None, :], 0), axis=1)
    padded_start = padded_end - padded
    n_blocks = tk // bm + N_EXPERTS
    n_rows = n_blocks * bm
    top_idx_t = meta[:TOP_K].astype(jnp.int32)
    rank_t = meta[TOP_K:2 * TOP_K].astype(jnp.int32)
    dest_t = rank_t + jnp.sum(
        jnp.where(top_idx_t[None] == experts[:, None, None], padded_start[:, None, None], 0), axis=0)
    block_start = jnp.arange(n_blocks, dtype=jnp.int32) * bm
    block_e = jnp.minimum(
        jnp.sum((padded_end[None, :] <= block_start[:, None]).astype(jnp.int32), axis=1), N_EXPERTS - 1)
    valid_end = jnp.sum(jnp.where(block_e[:, None] == experts[None, :], (padded_start + counts)[None, :], 0), axis=1)
    n_valid = jnp.clip(valid_end - block_start, 0, bm)

    x_rows = _dispatch_rows(dest_t, h_packed, n_rows)
    y_rows = _experts(block_e, n_valid, x_rows, w_gate, b_gate, w_up, b_up, w_down, b_down)
    y_tok = _gather_rows(dest_t.reshape(-1), y_rows).reshape(TOP_K, t, D_MODEL // 2)
    return _combine(h, y_tok, meta, ln2_g[None], ln2_b[None])


def kernel(x, w_in, b_in, sinks, ln_v_g, ln_v_b, w_spatial, b_spatial, w_out, b_out, ln1_g, ln1_b,
           w_router, b_router, w_gate, b_gate, w_up, b_up, w_down, b_down, ln2_g, ln2_b):
    batch, seq_len, d = x.shape
    x2 = x.reshape(batch * seq_len, d)
    for l in range(DEPTH):
        x2 = _layer(x2, seq_len, w_in[l], b_in[l], sinks[l], ln_v_g[l], ln_v_b[l], w_spatial[l],
                    b_spatial[l], w_out[l], b_out[l], ln1_g[l], ln1_b[l], w_router[l], b_router[l],
                    w_gate[l], b_gate[l], w_up[l], b_up[l], w_down[l], b_down[l], ln2_g[l], ln2_b[l])
    return x2.reshape(batch, seq_len, d)
```

```python
import functools

import jax
import jax.numpy as jnp
from jax import lax
from jax.experimental import pallas as pl
from jax.experimental.pallas import tpu as pltpu
from jax.experimental.pallas import tpu_sc as plsc

D_MODEL = 1024
HEAD_DIM = 64
N_Q_HEADS = 8
N_KV_HEADS = 2
Q_REP = N_Q_HEADS // N_KV_HEADS
ATTN_WIDTH = N_Q_HEADS * HEAD_DIM
KV_WIDTH = N_KV_HEADS * HEAD_DIM
ATTN_BLOCK = 128
N_GMLP_GROUPS = 8
GMLP_WIDTH = D_MODEL - ATTN_WIDTH
GMLP_GROUP_DIM = GMLP_WIDTH // N_GMLP_GROUPS
IN_WIDTH = ATTN_WIDTH + 2 * KV_WIDTH + 2 * GMLP_WIDTH
N_EXPERTS = 32
TOP_K = 4
SWIGLU_LIMIT = 7.0
SWIGLU_ALPHA = 1.702
LN_EPS = 1e-5
DEPTH = 1
DEEPNORM_ALPHA = (2.0 * DEPTH) ** 0.25
NEG_INF = -1e30
LOG2_E = 1.4426950408889634

LANES = 128
SUBLANES = 8

MIXER_ROWS = 512
PROJ_CHUNK = 256
EXPERT_ROWS = 1024
EXPERT_PARTIAL_ROWS = (128, 256, 384, 512, 768)
GATHER_WINDOW = 64
META_ROWS = 16
ROUTER_LANES = LANES
COMBINE_ROWS = 1024
VMEM_LIMIT_BYTES = 56 * 1024 * 1024

_O_K = ATTN_WIDTH
_O_V = _O_K + KV_WIDTH
_O_U = _O_V + KV_WIDTH
_O_G = _O_U + GMLP_WIDTH


def _pack_bf16_pairs(v):
    n = v.shape[1] // 2
    hi = lax.bitcast_convert_type(v[:, :n].astype(jnp.float32), jnp.int32)
    lo = lax.bitcast_convert_type(v[:, n:].astype(jnp.float32), jnp.int32)
    return hi | lax.shift_right_logical(lo, 16)


def _unpack_bf16_pairs(p):
    hi = lax.bitcast_convert_type(p & jnp.int32(-65536), jnp.float32)
    lo = lax.bitcast_convert_type(lax.shift_left(p, 16), jnp.float32)
    return hi, lo


def _layer_norm(v, g, b):
    mu = jnp.mean(v, axis=-1, keepdims=True)
    vc = v - mu
    var = jnp.mean(vc * vc, axis=-1, keepdims=True)
    return vc * lax.rsqrt(var + LN_EPS) * g + b


def _attention_block(q, kb, vb, bias_ref, bias_sel):
    outs = []
    for g in range(N_KV_HEADS):
        kg = kb[:, g * HEAD_DIM:(g + 1) * HEAD_DIM]
        vg = vb[:, g * HEAD_DIM:(g + 1) * HEAD_DIM]
        qg = jnp.concatenate(
            [q[:, (g * Q_REP + r) * HEAD_DIM:(g * Q_REP + r + 1) * HEAD_DIM] for r in range(Q_REP)],
            axis=0).astype(jnp.bfloat16)
        s = (lax.dot_general(qg, kg, (((1,), (1,)), ((), ())), preferred_element_type=jnp.float32)
             + bias_ref[g, bias_sel])
        p = jnp.exp2(s - jnp.max(s, axis=-1, keepdims=True))
        denom = jnp.sum(p, axis=-1, keepdims=True)
        o = jnp.dot(p.astype(jnp.bfloat16), vg, preferred_element_type=jnp.float32) / denom
        outs.extend(o[r * ATTN_BLOCK:(r + 1) * ATTN_BLOCK] for r in range(Q_REP))
    return jnp.concatenate(outs, axis=-1)


def _mixer_kernel(x_ref, w_in_ref, b_in_ref, bias_ref, lnv_g_ref, lnv_b_ref, grp_avg_ref,
                  w_sp_ref, b_sp_ref, w_out_ref, b_out_ref, ln1_g_ref, ln1_b_ref,
                  w_r_ref, w_r_hi_ref, b_r_ref, tri_ref,
                  h_ref, hp_ref, meta_ref, count_ref,
                  kv_prev_ref, hres_ref, *, steps_per_seq, n_steps):
    i = pl.program_id(0)
    first_step = (i % steps_per_seq) == 0
    tm = x_ref.shape[0]
    n_sub = tm // ATTN_BLOCK

    @pl.when(i == 0)
    def _():
        count_ref[...] = jnp.zeros_like(count_ref)

    @pl.when(first_step)
    def _():
        kv_prev_ref[...] = jnp.zeros_like(kv_prev_ref)

    def route_stages():
        h = _layer_norm(hres_ref[...], ln1_g_ref[...], ln1_b_ref[...])
        h_ref[...] = h
        yield
        h_hi = h.astype(jnp.bfloat16)
        hp_ref[...] = _pack_bf16_pairs(h_hi)
        h_lo = (h - h_hi.astype(jnp.float32)).astype(jnp.bfloat16)
        yield
        part = (jnp.dot(h_hi, w_r_ref[...], preferred_element_type=jnp.float32)
                + jnp.dot(h_lo, w_r_hi_ref[...], preferred_element_type=jnp.float32)).T
        logits = part[:N_EXPERTS] + part[N_EXPERTS:2 * N_EXPERTS] + b_r_ref[...]
        yield
        n_grp = N_EXPERTS // SUBLANES
        grp = [logits[SUBLANES * g:SUBLANES * (g + 1)] for g in range(n_grp)]
        sub = lax.broadcasted_iota(jnp.int32, (SUBLANES, tm), 0)
        beaten = [jnp.zeros((SUBLANES, tm), jnp.float32) for _ in range(n_grp)]
        for e2 in range(N_EXPERTS):
            g2, r2 = divmod(e2, SUBLANES)
            row = logits[e2:e2 + 1]
            for g in range(n_grp):
                if g > g2:
                    wins = jnp.where(row >= grp[g], 1.0, 0.0)
                elif g < g2:
                    wins = jnp.where(row > grp[g], 1.0, 0.0)
                else:
                    wins = jnp.where(sub > r2, jnp.where(row >= grp[g], 1.0, 0.0),
                                     jnp.where(row > grp[g], 1.0, 0.0))
                beaten[g] = beaten[g] + wins
            if e2 == N_EXPERTS // 2 - 1:
                yield
        place = jnp.concatenate(beaten, axis=0)
        expert_id = lax.broadcasted_iota(jnp.int32, (N_EXPERTS, tm), 0).astype(jnp.float32)
        onehot = jnp.where(place < TOP_K, 1.0, 0.0)
        yield
        before = (jnp.dot(onehot.astype(jnp.bfloat16), tri_ref[...], preferred_element_type=jnp.float32)
                  + count_ref[...])

        def pick(k, table):
            return jnp.sum(jnp.where(place == k, table, 0.0), axis=0, keepdims=True)

        vals = [pick(k, logits) for k in range(TOP_K)]
        exps = [jnp.exp(v - vals[0]) for v in vals]
        denom = exps[0] + exps[1] + exps[2] + exps[3]
        yield
        meta_ref[...] = jnp.zeros_like(meta_ref)
        for k in range(TOP_K):
            meta_ref[k:k + 1, :] = pick(k, expert_id)
            meta_ref[TOP_K + k:TOP_K + k + 1, :] = pick(k, before)
            meta_ref[2 * TOP_K + k:2 * TOP_K + k + 1, :] = exps[k] / denom
        count_ref[...] += jnp.sum(onehot, axis=1, keepdims=True)

    def mix_block(stages):
        x = x_ref[...]
        x_bf = x.astype(jnp.bfloat16)
        proj = []
        for c in range(IN_WIDTH // PROJ_CHUNK):
            cols = pl.ds(c * PROJ_CHUNK, PROJ_CHUNK)
            proj.append(jnp.dot(x_bf, w_in_ref[:, cols], preferred_element_type=jnp.float32) + b_in_ref[:, cols])
            next(stages, None)
        for _ in stages:
            pass

        def proj_cols(lo, hi):
            return jnp.concatenate(proj[lo // PROJ_CHUNK:hi // PROJ_CHUNK], axis=-1)

        q_all = proj_cols(0, _O_K) * (LOG2_E * HEAD_DIM ** -0.5)
        kv_all = proj_cols(_O_K, _O_U).astype(jnp.bfloat16)
        k_all, v_all = kv_all[:, :KV_WIDTH], kv_all[:, KV_WIDTH:]
        k_prev = kv_prev_ref[:, :KV_WIDTH]
        v_prev = kv_prev_ref[:, KV_WIDTH:]
        is_row0 = lax.broadcasted_iota(jnp.int32, (ATTN_BLOCK, KV_WIDTH), 0) == 0
        attn_blocks = []
        for sb in range(n_sub):
            rows = slice(sb * ATTN_BLOCK, (sb + 1) * ATTN_BLOCK)
            k_cur, v_cur = k_all[rows], v_all[rows]
            kb = jnp.concatenate([jnp.where(is_row0, 0, k_prev), k_cur], axis=0)
            vb = jnp.concatenate([jnp.where(is_row0, 0, v_prev), v_cur], axis=0)
            bias_sel = jnp.where(first_step, 1, 0) if sb == 0 else 0
            attn_blocks.append(_attention_block(q_all[rows], kb, vb, bias_ref, bias_sel))
            k_prev, v_prev = k_cur, v_cur
        kv_prev_ref[:, :KV_WIDTH] = k_prev
        kv_prev_ref[:, KV_WIDTH:] = v_prev
        attn = jnp.concatenate(attn_blocks, axis=0)

        u = jax.nn.gelu(proj_cols(_O_U, _O_G))
        gg = jax.nn.gelu(proj_cols(_O_G, IN_WIDTH))
        avg = grp_avg_ref[...]
        mu = jnp.dot(gg.astype(jnp.bfloat16), avg, preferred_element_type=jnp.float32)
        gc = gg - mu
        var = jnp.dot((gc * gc).astype(jnp.bfloat16), avg, preferred_element_type=jnp.float32)
        gn = (gc * lax.rsqrt(var + LN_EPS) * lnv_g_ref[...] + lnv_b_ref[...]).astype(jnp.bfloat16)
        causal = (lax.broadcasted_iota(jnp.int32, (ATTN_BLOCK, ATTN_BLOCK), 0)
                  >= lax.broadcasted_iota(jnp.int32, (ATTN_BLOCK, ATTN_BLOCK), 1))
        w_sp = [jnp.where(causal, w_sp_ref[g], 0.0).astype(jnp.bfloat16) for g in range(N_GMLP_GROUPS)]
        mixed_chunks = []
        for c in range(n_sub):
            rows = slice(c * ATTN_BLOCK, (c + 1) * ATTN_BLOCK)
            pieces = [
                jnp.dot(w_sp[g], gn[rows, g * GMLP_GROUP_DIM:(g + 1) * GMLP_GROUP_DIM],
                        preferred_element_type=jnp.float32)
                for g in range(N_GMLP_GROUPS)]
            mixed_chunks.append(jnp.concatenate(pieces, axis=-1) + b_sp_ref[...])
        sgu = u * jnp.concatenate(mixed_chunks, axis=0)

        mix = (jnp.dot(attn.astype(jnp.bfloat16), w_out_ref[:ATTN_WIDTH, :], preferred_element_type=jnp.float32)
               + jnp.dot(sgu.astype(jnp.bfloat16), w_out_ref[ATTN_WIDTH:, :], preferred_element_type=jnp.float32)
               + b_out_ref[...])
        hres_ref[...] = DEEPNORM_ALPHA * x + mix

    @pl.when(i == 0)
    def _():
        mix_block(iter(()))

    @pl.when(jnp.logical_and(i > 0, i < n_steps))
    def _():
        mix_block(route_stages())

    @pl.when(i == n_steps)
    def _():
        for _ in route_stages():
            pass


def _score_bias(sinks):
    t_idx = jnp.arange(Q_REP * ATTN_BLOCK)[:, None] % ATTN_BLOCK
    s_idx = jnp.arange(2 * ATTN_BLOCK)[None, :]
    diff = t_idx + ATTN_BLOCK - s_idx
    band = (diff >= 0) & (diff < ATTN_BLOCK)
    masks = jnp.stack([band, band & (s_idx >= ATTN_BLOCK)])
    bias = jnp.where(masks, 0.0, NEG_INF).astype(jnp.float32)
    sink_rows = jnp.repeat(sinks.reshape(N_KV_HEADS, Q_REP) * LOG2_E, ATTN_BLOCK, axis=1)
    return jnp.where(s_idx == 0, sink_rows[:, None, :, None], bias[None])


def _mixer(x2, sinks, w_in, b_in, lnv_g, lnv_b, w_sp, b_sp_full, w_out, b_out, ln1_g, ln1_b,
           w_r, w_r_hi, b_r, seq_len):
    t = x2.shape[0]
    tm = MIXER_ROWS
    n_steps = t // tm
    grp = jnp.arange(GMLP_WIDTH) // GMLP_GROUP_DIM
    grp_avg = jnp.where(grp[:, None] == grp[None, :], 1.0 / GMLP_GROUP_DIM, 0.0).astype(jnp.bfloat16)
    tri = (jnp.arange(tm)[:, None] < jnp.arange(tm)[None, :]).astype(jnp.bfloat16)

    def full(shape):
        return pl.BlockSpec(shape, lambda i: (0,) * len(shape))

    return pl.pallas_call(
        functools.partial(_mixer_kernel, steps_per_seq=seq_len // tm, n_steps=n_steps),
        grid=(n_steps + 1,),
        in_specs=[
            pl.BlockSpec((tm, D_MODEL), lambda i: (jnp.minimum(i, n_steps - 1), 0)),
            full((D_MODEL, IN_WIDTH)), full((1, IN_WIDTH)),
            full((N_KV_HEADS, 2, Q_REP * ATTN_BLOCK, 2 * ATTN_BLOCK)),
            full((1, GMLP_WIDTH)), full((1, GMLP_WIDTH)), full((GMLP_WIDTH, GMLP_WIDTH)),
            full((N_GMLP_GROUPS, ATTN_BLOCK, ATTN_BLOCK)), full((ATTN_BLOCK, GMLP_WIDTH)),
            full((D_MODEL, D_MODEL)), full((1, D_MODEL)), full((1, D_MODEL)), full((1, D_MODEL)),
            full((D_MODEL, ROUTER_LANES)), full((D_MODEL, ROUTER_LANES)), full((N_EXPERTS, 1)), full((tm, tm)),
        ],
        out_specs=[
            pl.BlockSpec((tm, D_MODEL), lambda i: (jnp.maximum(i - 1, 0), 0)),
            pl.BlockSpec((tm, D_MODEL // 2), lambda i: (jnp.maximum(i - 1, 0), 0)),
            pl.BlockSpec((META_ROWS, tm), lambda i: (0, jnp.maximum(i - 1, 0))),
            pl.BlockSpec((N_EXPERTS, 1), lambda i: (0, 0)),
        ],
        out_shape=[
            jax.ShapeDtypeStruct((t, D_MODEL), jnp.float32),
            jax.ShapeDtypeStruct((t, D_MODEL // 2), jnp.int32),
            jax.ShapeDtypeStruct((META_ROWS, t), jnp.float32),
            jax.ShapeDtypeStruct((N_EXPERTS, 1), jnp.float32),
        ],
        scratch_shapes=[pltpu.VMEM((ATTN_BLOCK, 2 * KV_WIDTH), jnp.bfloat16),
                        pltpu.VMEM((tm, D_MODEL), jnp.float32)],
        compiler_params=pltpu.CompilerParams(
            dimension_semantics=("arbitrary",), vmem_limit_bytes=VMEM_LIMIT_BYTES,
            allow_input_fusion=[k in (1, 9) for k in range(17)]),
        name="mixer",
    )(x2, w_in, b_in, _score_bias(sinks), lnv_g, lnv_b, grp_avg, w_sp, b_sp_full, w_out, b_out, ln1_g, ln1_b,
      w_r, w_r_hi, b_r, tri)


def _gather_rows(idx, src):
    n = idx.shape[0]
    width = src.shape[1]
    win = GATHER_WINDOW
    sc = plsc.get_sparse_core_info()
    n_workers = sc.num_cores * sc.num_subcores
    per_worker = n // n_workers
    n_pairs = per_worker // (2 * win)
    assert n_pairs * 2 * win * n_workers == n
    mesh = plsc.VectorSubcoreMesh(core_axis_name="core", subcore_axis_name="subcore")

    @functools.partial(
        pl.kernel, out_type=jax.ShapeDtypeStruct((n, width), src.dtype), mesh=mesh,
        scratch_types=[pltpu.VMEM((per_worker,), jnp.int32), pltpu.VMEM((2, win, width), src.dtype),
                       pltpu.SemaphoreType.DMA((2,)), pltpu.SemaphoreType.DMA((2,))],
        name="gather_rows")
    def gather(src_hbm, idx_hbm, out_hbm, idx_v, rows_v, fetch_sem, store_sem):
        worker = lax.axis_index("subcore") * sc.num_cores + lax.axis_index("core")
        base = worker * per_worker
        pltpu.sync_copy(idx_hbm.at[pl.ds(base, per_worker)], idx_v)

        def fetch(chunk, buf):
            return pltpu.make_async_copy(src_hbm.at[idx_v.at[pl.ds(chunk * win, win)]], rows_v.at[buf],
                                         fetch_sem.at[buf])

        def store(chunk, buf):
            return pltpu.make_async_copy(rows_v.at[buf], out_hbm.at[pl.ds(base + chunk * win, win)],
                                         store_sem.at[buf])

        @pl.loop(0, n_pairs)
        def _(p):
            for buf in range(2):
                @pl.when(p > 0)
                def _():
                    store(2 * p - 2 + buf, buf).wait()
                fetch(2 * p + buf, buf).start()
            for buf in range(2):
                fetch(2 * p + buf, buf).wait()
                store(2 * p + buf, buf).start()

        for buf in range(2):
            store(2 * n_pairs - 2 + buf, buf).wait()

    return gather(src, idx)


def _dispatch_rows(dest_t, src, n_rows):
    t, width = src.shape
    win = GATHER_WINDOW
    sc = plsc.get_sparse_core_info()
    n_workers = sc.num_cores * sc.num_subcores
    per_worker = t // n_workers
    n_chunks = per_worker // win
    n_pairs = n_chunks // 2
    assert n_pairs * 2 * win * n_workers == t
    idx = dest_t.reshape(TOP_K, n_workers, n_chunks, win).transpose(1, 0, 2, 3)
    idx = idx.reshape(n_workers, TOP_K * n_chunks, win)
    mesh = plsc.VectorSubcoreMesh(core_axis_name="core", subcore_axis_name="subcore")

    @functools.partial(
        pl.kernel, out_type=jax.ShapeDtypeStruct((n_rows, width), src.dtype), mesh=mesh,
        scratch_types=[pltpu.VMEM((TOP_K * n_chunks, win), jnp.int32), pltpu.VMEM((2, win, width), src.dtype),
                       pltpu.SemaphoreType.DMA((2,)), pltpu.SemaphoreType.DMA((2,))],
        name="dispatch_rows")
    def dispatch(src_hbm, idx_hbm, out_hbm, idx_v, rows_v, fetch_sem, store_sem):
        worker = lax.axis_index("subcore") * sc.num_cores + lax.axis_index("core")
        base = worker * per_worker
        pltpu.sync_copy(idx_hbm.at[worker], idx_v)

        def fetch(chunk, buf):
            return pltpu.make_async_copy(src_hbm.at[pl.ds(base + chunk * win, win)], rows_v.at[buf],
                                         fetch_sem.at[buf])

        def store(chunk, k, buf):
            return pltpu.make_async_copy(rows_v.at[buf], out_hbm.at[idx_v.at[k * n_chunks + chunk]],
                                         store_sem.at[buf])

        @pl.loop(0, n_pairs)
        def _(p):
            for buf in range(2):
                @pl.when(p > 0)
                def _():
                    for k in range(TOP_K):
                        store(2 * p - 2 + buf, k, buf).wait()
                fetch(2 * p + buf, buf).start()
            for buf in range(2):
                fetch(2 * p + buf, buf).wait()
                for k in range(TOP_K):
                    store(2 * p + buf, k, buf).start()

        for buf in range(2):
            for k in range(TOP_K):
                store(2 * n_pairs - 2 + buf, k, buf).wait()

    return dispatch(src, idx)


def _expert_kernel(block_e_ref, n_valid_ref, first_ref, next_e_ref, row_block_ref,
                   x_ref, wg_hbm, bg_ref, wu_hbm, bu_ref, wd_hbm, bd_ref,
                   y_ref, wg_f32, wu_f32, wd_f32, wg_bf, wu_bf, wd_bf, sem):
    i = pl.program_id(0)
    n_valid = n_valid_ref[i]
    staged = ((wg_hbm, wg_f32, wg_bf), (wu_hbm, wu_f32, wu_bf), (wd_hbm, wd_f32, wd_bf))

    def weight_copy(m, expert):
        return pltpu.make_async_copy(staged[m][0].at[expert], staged[m][1], sem.at[m])

    @pl.when(i == 0)
    def _():
        for m in range(3):
            weight_copy(m, block_e_ref[0]).start()

    @pl.when(first_ref[i] == 1)
    def _():
        for m in range(3):
            weight_copy(m, block_e_ref[i]).wait()
            staged[m][2][...] = staged[m][1][...].astype(jnp.bfloat16)

            @pl.when(next_e_ref[i] >= 0)
            def _():
                weight_copy(m, next_e_ref[i]).start()

    def expert_mlp(rows):
        row = lax.broadcasted_iota(jnp.int32, (rows, 1), 0)
        x_hi, x_lo = _unpack_bf16_pairs(jnp.where(row < n_valid, x_ref[:rows, :], 0))
        xb = jnp.concatenate([x_hi, x_lo], axis=1).astype(jnp.bfloat16)
        gt = jnp.minimum(jnp.dot(xb, wg_bf[...], preferred_element_type=jnp.float32) + bg_ref[0], SWIGLU_LIMIT)
        up = jnp.clip(jnp.dot(xb, wu_bf[...], preferred_element_type=jnp.float32) + bu_ref[0],
                      -SWIGLU_LIMIT, SWIGLU_LIMIT)
        hid = gt * jax.nn.sigmoid(SWIGLU_ALPHA * gt) * (up + 1.0)
        y = jnp.dot(hid.astype(jnp.bfloat16), wd_bf[...], preferred_element_type=jnp.float32) + bd_ref[0]
        y_ref[:rows, :] = _pack_bf16_pairs(y.astype(jnp.bfloat16))
        if rows < y_ref.shape[0]:
            y_ref[rows:, :] = jnp.zeros((y_ref.shape[0] - rows, y_ref.shape[1]), y_ref.dtype)

    row_options = (0,) + EXPERT_PARTIAL_ROWS + (x_ref.shape[0],)
    for lo, hi in zip(row_options[:-1], row_options[1:]):
        pl.when(jnp.logical_and(n_valid > lo, n_valid <= hi))(functools.partial(expert_mlp, hi))


def _experts(block_e, n_valid, x_rows, w_gate, b_gate, w_up, b_up, w_down, b_down):
    n_rows = x_rows.shape[0]
    bm = EXPERT_ROWS
    n_blocks = n_rows // bm
    d_ff = w_gate.shape[2]

    blocks = jnp.arange(n_blocks, dtype=jnp.int32)
    last_used = jnp.sum((n_valid > 0).astype(jnp.int32)) - 1
    row_block = jnp.minimum(blocks, last_used)
    block_e = jnp.sum(jnp.where(blocks[None, :] == row_block[:, None], block_e[None, :], 0), axis=1)

    is_first = jnp.concatenate([jnp.ones((1,), jnp.int32), (block_e[1:] != block_e[:-1]).astype(jnp.int32)])
    next_e = jnp.min(jnp.where(block_e[None, :] > block_e[:, None], block_e[None, :], N_EXPERTS), axis=1)
    next_e = jnp.where(next_e == N_EXPERTS, -1, next_e)

    def by_expert(shape):
        return pl.BlockSpec(shape, lambda i, be, *_: (be[i],) + (0,) * (len(shape) - 1))

    hbm = pl.BlockSpec(memory_space=pl.ANY)
    return pl.pallas_call(
        _expert_kernel,
        grid_spec=pltpu.PrefetchScalarGridSpec(
            num_scalar_prefetch=5,
            grid=(n_blocks,),
            in_specs=[
                pl.BlockSpec((bm, D_MODEL // 2), lambda i, be, nv, fi, ne, rb: (rb[i], 0)),
                hbm, by_expert((1, 1, d_ff)),
                hbm, by_expert((1, 1, d_ff)),
                hbm, by_expert((1, 1, D_MODEL)),
            ],
            out_specs=pl.BlockSpec((bm, D_MODEL // 2), lambda i, be, nv, fi, ne, rb: (rb[i], 0)),
            scratch_shapes=[
                pltpu.VMEM((D_MODEL, d_ff), jnp.float32),
                pltpu.VMEM((D_MODEL, d_ff), jnp.float32),
                pltpu.VMEM((d_ff, D_MODEL), jnp.float32),
                pltpu.VMEM((D_MODEL, d_ff), jnp.bfloat16),
                pltpu.VMEM((D_MODEL, d_ff), jnp.bfloat16),
                pltpu.VMEM((d_ff, D_MODEL), jnp.bfloat16),
                pltpu.SemaphoreType.DMA((3,)),
            ],
        ),
        out_shape=jax.ShapeDtypeStruct((n_rows, D_MODEL // 2), jnp.int32),
        compiler_params=pltpu.CompilerParams(
            dimension_semantics=("arbitrary",), vmem_limit_bytes=VMEM_LIMIT_BYTES),
        name="experts",
    )(block_e, n_valid, is_first, next_e, row_block, x_rows, w_gate, b_gate.reshape(N_EXPERTS, 1, d_ff), w_up,
      b_up.reshape(N_EXPERTS, 1, d_ff), w_down, b_down.reshape(N_EXPERTS, 1, D_MODEL))


def _combine_kernel(h_ref, y_ref, meta_ref, g_ref, b_ref, o_ref):
    tm = h_ref.shape[0]
    meta_cols = jnp.concatenate([meta_ref[...], jnp.zeros((LANES - META_ROWS, tm), jnp.float32)], axis=0).T
    ffn_hi, ffn_lo = 0.0, 0.0
    for k in range(TOP_K):
        gate = meta_cols[:, 2 * TOP_K + k:2 * TOP_K + k + 1]
        y_hi, y_lo = _unpack_bf16_pairs(y_ref[k])
        ffn_hi = ffn_hi + gate * y_hi
        ffn_lo = ffn_lo + gate * y_lo
    ffn = jnp.concatenate([ffn_hi, ffn_lo], axis=1)
    o_ref[...] = _layer_norm(DEEPNORM_ALPHA * h_ref[...] + ffn, g_ref[...], b_ref[...])


def _combine(h, y_tok, meta, ln2_g, ln2_b):
    t = h.shape[0]
    tm = COMBINE_ROWS
    return pl.pallas_call(
        _combine_kernel,
        grid=(t // tm,),
        in_specs=[
            pl.BlockSpec((tm, D_MODEL), lambda i: (i, 0)),
            pl.BlockSpec((TOP_K, tm, D_MODEL // 2), lambda i: (0, i, 0)),
            pl.BlockSpec((META_ROWS, tm), lambda i: (0, i)),
            pl.BlockSpec((1, D_MODEL), lambda i: (0, 0)),
            pl.BlockSpec((1, D_MODEL), lambda i: (0, 0)),
        ],
        out_specs=pl.BlockSpec((tm, D_MODEL), lambda i: (i, 0)),
        out_shape=jax.ShapeDtypeStruct((t, D_MODEL), jnp.float32),
        compiler_params=pltpu.CompilerParams(
            dimension_semantics=("arbitrary",), vmem_limit_bytes=VMEM_LIMIT_BYTES),
        name="combine",
    )(h, y_tok, meta, ln2_g, ln2_b)


def _layer(x2, seq_len, w_in, b_in, sinks, ln_v_g, ln_v_b, w_spatial, b_spatial, w_out, b_out,
           ln1_g, ln1_b, w_router, b_router, w_gate, b_gate, w_up, b_up, w_down, b_down, ln2_g, ln2_b):
    t = x2.shape[0]
    tk = t * TOP_K
    bm = EXPERT_ROWS
    bf16 = jnp.bfloat16

    w_r_hi = w_router.astype(bf16)
    w_r_lo = (w_router - w_r_hi.astype(jnp.float32)).astype(bf16)
    lane_pad = jnp.zeros((D_MODEL, ROUTER_LANES - 2 * N_EXPERTS), bf16)
    w_r = jnp.concatenate([w_r_hi, w_r_lo, lane_pad], axis=1)
    w_r_hi_only = jnp.concatenate([w_r_hi, jnp.zeros_like(w_r_lo), lane_pad], axis=1)
    b_sp_full = jnp.repeat(b_spatial.T, GMLP_GROUP_DIM, axis=1)

    h, h_packed, meta, counts = _mixer(
        x2, sinks, w_in.astype(bf16), b_in[None], ln_v_g[None], ln_v_b[None], w_spatial, b_sp_full,
        w_out.astype(bf16), b_out[None], ln1_g[None], ln1_b[None], w_r, w_r_hi_only, b_router[:, None], seq_len)

    counts = counts[:, 0].astype(jnp.int32)
    experts = jnp.arange(N_EXPERTS, dtype=jnp.int32)
    padded = (counts + bm - 1) // bm * bm
    padded_end = jnp.sum(jnp.where(experts[None, :] <= experts[:, None], padded[None, :], 0), axis=1)
    padded_start = padded_end - padded
    n_blocks = tk // bm + N_EXPERTS
    n_rows = n_blocks * bm
    top_idx_t = meta[:TOP_K].astype(jnp.int32)
    rank_t = meta[TOP_K:2 * TOP_K].astype(jnp.int32)
    dest_t = rank_t + jnp.sum(
        jnp.where(top_idx_t[None] == experts[:, None, None], padded_start[:, None, None], 0), axis=0)
    block_start = jnp.arange(n_blocks, dtype=jnp.int32) * bm
    block_e = jnp.minimum(
        jnp.sum((padded_end[None, :] <= block_start[:, None]).astype(jnp.int32), axis=1), N_EXPERTS - 1)
    valid_end = jnp.sum(jnp.where(block_e[:, None] == experts[None, :], (padded_start + counts)[None, :], 0), axis=1)
    n_valid = jnp.clip(valid_end - block_start, 0, bm)

    x_rows = _dispatch_rows(dest_t, h_packed, n_rows)
    y_rows = _experts(block_e, n_valid, x_rows, w_gate, b_gate, w_up, b_up, w_down, b_down)
    y_tok = _gather_rows(dest_t.reshape(-1), y_rows).reshape(TOP_K, t, D_MODEL // 2)
    return _combine(h, y_tok, meta, ln2_g[None], ln2_b[None])


def kernel(x, w_in, b_in, sinks, ln_v_g, ln_v_b, w_spatial, b_spatial, w_out, b_out, ln1_g, ln1_b,
           w_router, b_router, w_gate, b_gate, w_up, b_up, w_down, b_down, ln2_g, ln2_b):
    batch, seq_len, d = x.shape
    x2 = x.reshape(batch * seq_len, d)
    for l in range(DEPTH):
        x2 = _layer(x2, seq_len, w_in[l], b_in[l], sinks[l], ln_v_g[l], ln_v_b[l], w_spatial[l],
                    b_spatial[l], w_out[l], b_out[l], ln1_g[l], ln1_b[l], w_router[l], b_router[l],
                    w_gate[l], b_gate[l], w_up[l], b_up[l], w_down[l], b_down[l], ln2_g[l], ln2_b[l])
    return x2.reshape(batch, seq_len, d)
```

```python
import functools

import jax
import jax.numpy as jnp
from jax import lax
from jax.experimental import pallas as pl
from jax.experimental.pallas import tpu as pltpu
from jax.experimental.pallas import tpu_sc as plsc

D_MODEL = 1024
HEAD_DIM = 64
N_Q_HEADS = 8
N_KV_HEADS = 2
Q_REP = N_Q_HEADS // N_KV_HEADS
ATTN_WIDTH = N_Q_HEADS * HEAD_DIM
KV_WIDTH = N_KV_HEADS * HEAD_DIM
ATTN_BLOCK = 128
N_GMLP_GROUPS = 8
GMLP_WIDTH = D_MODEL - ATTN_WIDTH
GMLP_GROUP_DIM = GMLP_WIDTH // N_GMLP_GROUPS
IN_WIDTH = ATTN_WIDTH + 2 * KV_WIDTH + 2 * GMLP_WIDTH
N_EXPERTS = 32
TOP_K = 4
SWIGLU_LIMIT = 7.0
SWIGLU_ALPHA = 1.702
LN_EPS = 1e-5
DEPTH = 1
DEEPNORM_ALPHA = (2.0 * DEPTH) ** 0.25
NEG_INF = -1e30
LOG2_E = 1.4426950408889634

LANES = 128
SUBLANES = 8

MIXER_ROWS = 512
PROJ_CHUNK = 256
EXPERT_ROWS = 1024
EXPERT_PARTIAL_ROWS = (128, 256, 384, 512, 768)
GATHER_WINDOW = 64
META_ROWS = 16
ROUTER_LANES = LANES
COMBINE_ROWS = 1024
VMEM_LIMIT_BYTES = 56 * 1024 * 1024

_O_K = ATTN_WIDTH
_O_V = _O_K + KV_WIDTH
_O_U = _O_V + KV_WIDTH
_O_G = _O_U + GMLP_WIDTH


def _pack_bf16_pairs(v):
    n = v.shape[1] // 2
    hi = lax.bitcast_convert_type(v[:, :n].astype(jnp.float32), jnp.int32)
    lo = lax.bitcast_convert_type(v[:, n:].astype(jnp.float32), jnp.int32)
    return hi | lax.shift_right_logical(lo, 16)


def _unpack_bf16_pairs(p):
    hi = lax.bitcast_convert_type(p & jnp.int32(-65536), jnp.float32)
    lo = lax.bitcast_convert_type(lax.shift_left(p, 16), jnp.float32)
    return hi, lo


def _layer_norm(v, g, b):
    mu = jnp.mean(v, axis=-1, keepdims=True)
    vc = v - mu
    var = jnp.mean(vc * vc, axis=-1, keepdims=True)
    return vc * lax.rsqrt(var + LN_EPS) * g + b


def _attention_block(q, kb, vb, bias_ref, bias_sel):
    outs = []
    for g in range(N_KV_HEADS):
        kg = kb[:, g * HEAD_DIM:(g + 1) * HEAD_DIM]
        vg = vb[:, g * HEAD_DIM:(g + 1) * HEAD_DIM]
        qg = jnp.concatenate(
            [q[:, (g * Q_REP + r) * HEAD_DIM:(g * Q_REP + r + 1) * HEAD_DIM] for r in range(Q_REP)],
            axis=0).astype(jnp.bfloat16)
        s = (lax.dot_general(qg, kg, (((1,), (1,)), ((), ())), preferred_element_type=jnp.float32)
             + bias_ref[g, bias_sel])
        p = jnp.exp2(s - jnp.max(s, axis=-1, keepdims=True))
        denom = jnp.sum(p, axis=-1, keepdims=True)
        o = jnp.dot(p.astype(jnp.bfloat16), vg, preferred_element_type=jnp.float32) / denom
        outs.extend(o[r * ATTN_BLOCK:(r + 1) * ATTN_BLOCK] for r in range(Q_REP))
    return jnp.concatenate(outs, axis=-1)


def _mixer_kernel(x_ref, w_in_ref, b_in_ref, bias_ref, lnv_g_ref, lnv_b_ref, grp_avg_ref,
                  w_sp_ref, b_sp_ref, w_out_ref, b_out_ref, ln1_g_ref, ln1_b_ref,
                  w_r_ref, w_r_hi_ref, b_r_ref, tri_ref,
                  h_ref, hp_ref, meta_ref, count_ref,
                  kv_prev_ref, hres_ref, *, steps_per_seq, n_steps):
    i = pl.program_id(0)
    first_step = (i % steps_per_seq) == 0
    tm = x_ref.shape[0]
    n_sub = tm // ATTN_BLOCK

    @pl.when(i == 0)
    def _():
        count_ref[...] = jnp.zeros_like(count_ref)

    @pl.when(first_step)
    def _():
        kv_prev_ref[...] = jnp.zeros_like(kv_prev_ref)

    def route_stages():
        h = _layer_norm(hres_ref[...], ln1_g_ref[...], ln1_b_ref[...])
        h_ref[...] = h
        yield
        h_hi = h.astype(jnp.bfloat16)
        hp_ref[...] = _pack_bf16_pairs(h_hi)
        h_lo = (h - h_hi.astype(jnp.float32)).astype(jnp.bfloat16)
        yield
        part = (jnp.dot(h_hi, w_r_ref[...], preferred_element_type=jnp.float32)
                + jnp.dot(h_lo, w_r_hi_ref[...], preferred_element_type=jnp.float32)).T
        logits = part[:N_EXPERTS] + part[N_EXPERTS:2 * N_EXPERTS] + b_r_ref[...]
        yield
        n_grp = N_EXPERTS // SUBLANES
        grp = [logits[SUBLANES * g:SUBLANES * (g + 1)] for g in range(n_grp)]
        sub = lax.broadcasted_iota(jnp.int32, (SUBLANES, tm), 0)
        beaten = [jnp.zeros((SUBLANES, tm), jnp.float32) for _ in range(n_grp)]
        for e2 in range(N_EXPERTS):
            g2, r2 = divmod(e2, SUBLANES)
            row = logits[e2:e2 + 1]
            for g in range(n_grp):
                if g > g2:
                    wins = jnp.where(row >= grp[g], 1.0, 0.0)
                elif g < g2:
                    wins = jnp.where(row > grp[g], 1.0, 0.0)
                else:
                    wins = jnp.where(sub > r2, jnp.where(row >= grp[g], 1.0, 0.0),
                                     jnp.where(row > grp[g], 1.0, 0.0))
                beaten[g] = beaten[g] + wins
            if e2 == N_EXPERTS // 2 - 1:
                yield
        place = jnp.concatenate(beaten, axis=0)
        expert_id = lax.broadcasted_iota(jnp.int32, (N_EXPERTS, tm), 0).astype(jnp.float32)
        onehot = jnp.where(place < TOP_K, 1.0, 0.0)
        yield
        before = (jnp.dot(onehot.astype(jnp.bfloat16), tri_ref[...], preferred_element_type=jnp.float32)
                  + count_ref[...])

        def pick(k, table):
            return jnp.sum(jnp.where(place == k, table, 0.0), axis=0, keepdims=True)

        vals = [pick(k, logits) for k in range(TOP_K)]
        exps = [jnp.exp(v - vals[0]) for v in vals]
        denom = exps[0] + exps[1] + exps[2] + exps[3]
        yield
        meta_ref[...] = jnp.zeros_like(meta_ref)
        for k in range(TOP_K):
            meta_ref[k:k + 1, :] = pick(k, expert_id)
            meta_ref[TOP_K + k:TOP_K + k + 1, :] = pick(k, before)
            meta_ref[2 * TOP_K + k:2 * TOP_K + k + 1, :] = exps[k] / denom
        count_ref[...] += jnp.sum(onehot, axis=1, keepdims=True)

    def mix_block(stages):
        x = x_ref[...]
        x_bf = x.astype(jnp.bfloat16)
        proj = []
        for c in range(IN_WIDTH // PROJ_CHUNK):
            cols = pl.ds(c * PROJ_CHUNK, PROJ_CHUNK)
            proj.append(jnp.dot(x_bf, w_in_ref[:, cols], preferred_element_type=jnp.float32) + b_in_ref[:, cols])
            next(stages, None)
        for _ in stages:
            pass

        def proj_cols(lo, hi):
            return jnp.concatenate(proj[lo // PROJ_CHUNK:hi // PROJ_CHUNK], axis=-1)

        q_all = proj_cols(0, _O_K) * (LOG2_E * HEAD_DIM ** -0.5)
        kv_all = proj_cols(_O_K, _O_U).astype(jnp.bfloat16)
        k_all, v_all = kv_all[:, :KV_WIDTH], kv_all[:, KV_WIDTH:]
        k_prev = kv_prev_ref[:, :KV_WIDTH]
        v_prev = kv_prev_ref[:, KV_WIDTH:]
        is_row0 = lax.broadcasted_iota(jnp.int32, (ATTN_BLOCK, KV_WIDTH), 0) == 0
        attn_blocks = []
        for sb in range(n_sub):
            rows = slice(sb * ATTN_BLOCK, (sb + 1) * ATTN_BLOCK)
            k_cur, v_cur = k_all[rows], v_all[rows]
            kb = jnp.concatenate([jnp.where(is_row0, 0, k_prev), k_cur], axis=0)
            vb = jnp.concatenate([jnp.where(is_row0, 0, v_prev), v_cur], axis=0)
            bias_sel = jnp.where(first_step, 1, 0) if sb == 0 else 0
            attn_blocks.append(_attention_block(q_all[rows], kb, vb, bias_ref, bias_sel))
            k_prev, v_prev = k_cur, v_cur
        kv_prev_ref[:, :KV_WIDTH] = k_prev
        kv_prev_ref[:, KV_WIDTH:] = v_prev
        attn = jnp.concatenate(attn_blocks, axis=0)

        u = jax.nn.gelu(proj_cols(_O_U, _O_G))
        gg = jax.nn.gelu(proj_cols(_O_G, IN_WIDTH))
        avg = grp_avg_ref[...]
        mu = jnp.dot(gg.astype(jnp.bfloat16), avg, preferred_element_type=jnp.float32)
        gc = gg - mu
        var = jnp.dot((gc * gc).astype(jnp.bfloat16), avg, preferred_element_type=jnp.float32)
        gn = (gc * lax.rsqrt(var + LN_EPS) * lnv_g_ref[...] + lnv_b_ref[...]).astype(jnp.bfloat16)
        causal = (lax.broadcasted_iota(jnp.int32, (ATTN_BLOCK, ATTN_BLOCK), 0)
                  >= lax.broadcasted_iota(jnp.int32, (ATTN_BLOCK, ATTN_BLOCK), 1))
        w_sp = [jnp.where(causal, w_sp_ref[g], 0.0).astype(jnp.bfloat16) for g in range(N_GMLP_GROUPS)]
        mixed_chunks = []
        for c in range(n_sub):
            rows = slice(c * ATTN_BLOCK, (c + 1) * ATTN_BLOCK)
            pieces = [
                jnp.dot(w_sp[g], gn[rows, g * GMLP_GROUP_DIM:(g + 1) * GMLP_GROUP_DIM],
                        preferred_element_type=jnp.float32)
                for g in range(N_GMLP_GROUPS)]
            mixed_chunks.append(jnp.concatenate(pieces, axis=-1) + b_sp_ref[...])
        sgu = u * jnp.concatenate(mixed_chunks, axis=0)

        mix = (jnp.dot(attn.astype(jnp.bfloat16), w_out_ref[:ATTN_WIDTH, :], preferred_element_type=jnp.float32)
               + jnp.dot(sgu.astype(jnp.bfloat16), w_out_ref[ATTN_WIDTH:, :], preferred_element_type=jnp.float32)
               + b_out_ref[...])
        hres_ref[...] = DEEPNORM_ALPHA * x + mix

    @pl.when(i == 0)
    def _():
        mix_block(iter(()))

    @pl.when(jnp.logical_and(i > 0, i < n_steps))
    def _():
        mix_block(route_stages())

    @pl.when(i == n_steps)
    def _():
        for _ in route_stages():
            pass


def _score_bias(sinks):
    t_idx = jnp.arange(Q_REP * ATTN_BLOCK)[:, None] % ATTN_BLOCK
    s_idx = jnp.arange(2 * ATTN_BLOCK)[None, :]
    diff = t_idx + ATTN_BLOCK - s_idx
    band = (diff >= 0) & (diff < ATTN_BLOCK)
    masks = jnp.stack([band, band & (s_idx >= ATTN_BLOCK)])
    bias = jnp.where(masks, 0.0, NEG_INF).astype(jnp.float32)
    sink_rows = jnp.repeat(sinks.reshape(N_KV_HEADS, Q_REP) * LOG2_E, ATTN_BLOCK, axis=1)
    return jnp.where(s_idx == 0, sink_rows[:, None, :, None], bias[None])


def _mixer(x2, sinks, w_in, b_in, lnv_g, lnv_b, w_sp, b_sp_full, w_out, b_out, ln1_g, ln1_b,
           w_r, w_r_hi, b_r, seq_len):
    t = x2.shape[0]
    tm = MIXER_ROWS
    n_steps = t // tm
    grp = jnp.arange(GMLP_WIDTH) // GMLP_GROUP_DIM
    grp_avg = jnp.where(grp[:, None] == grp[None, :], 1.0 / GMLP_GROUP_DIM, 0.0).astype(jnp.bfloat16)
    tri = (jnp.arange(tm)[:, None] < jnp.arange(tm)[None, :]).astype(jnp.bfloat16)

    def full(shape):
        return pl.BlockSpec(shape, lambda i: (0,) * len(shape))

    return pl.pallas_call(
        functools.partial(_mixer_kernel, steps_per_seq=seq_len // tm, n_steps=n_steps),
        grid=(n_steps + 1,),
        in_specs=[
            pl.BlockSpec((tm, D_MODEL), lambda i: (jnp.minimum(i, n_steps - 1), 0)),
            full((D_MODEL, IN_WIDTH)), full((1, IN_WIDTH)),
            full((N_KV_HEADS, 2, Q_REP * ATTN_BLOCK, 2 * ATTN_BLOCK)),
            full((1, GMLP_WIDTH)), full((1, GMLP_WIDTH)), full((GMLP_WIDTH, GMLP_WIDTH)),
            full((N_GMLP_GROUPS, ATTN_BLOCK, ATTN_BLOCK)), full((ATTN_BLOCK, GMLP_WIDTH)),
            full((D_MODEL, D_MODEL)), full((1, D_MODEL)), full((1, D_MODEL)), full((1, D_MODEL)),
            full((D_MODEL, ROUTER_LANES)), full((D_MODEL, ROUTER_LANES)), full((N_EXPERTS, 1)), full((tm, tm)),
        ],
        out_specs=[
            pl.BlockSpec((tm, D_MODEL), lambda i: (jnp.maximum(i - 1, 0), 0)),
            pl.BlockSpec((tm, D_MODEL // 2), lambda i: (jnp.maximum(i - 1, 0), 0)),
            pl.BlockSpec((META_ROWS, tm), lambda i: (0, jnp.maximum(i - 1, 0))),
            pl.BlockSpec((N_EXPERTS, 1), lambda i: (0, 0)),
        ],
        out_shape=[
            jax.ShapeDtypeStruct((t, D_MODEL), jnp.float32),
            jax.ShapeDtypeStruct((t, D_MODEL // 2), jnp.int32),
            jax.ShapeDtypeStruct((META_ROWS, t), jnp.float32),
            jax.ShapeDtypeStruct((N_EXPERTS, 1), jnp.float32),
        ],
        scratch_shapes=[pltpu.VMEM((ATTN_BLOCK, 2 * KV_WIDTH), jnp.bfloat16),
                        pltpu.VMEM((tm, D_MODEL), jnp.float32)],
        compiler_params=pltpu.CompilerParams(
            dimension_semantics=("arbitrary",), vmem_limit_bytes=VMEM_LIMIT_BYTES,
            allow_input_fusion=[k in (1, 9) for k in range(17)]),
        name="mixer",
    )(x2, w_in, b_in, _score_bias(sinks), lnv_g, lnv_b, grp_avg, w_sp, b_sp_full, w_out, b_out, ln1_g, ln1_b,
      w_r, w_r_hi, b_r, tri)


def _gather_rows(idx, src):
    n = idx.shape[0]
    width = src.shape[1]
    win = GATHER_WINDOW
    sc = plsc.get_sparse_core_info()
    n_workers = sc.num_cores * sc.num_subcores
    per_worker = n // n_workers
    n_pairs = per_worker // (2 * win)
    assert n_pairs * 2 * win * n_workers == n
    mesh = plsc.VectorSubcoreMesh(core_axis_name="core", subcore_axis_name="subcore")

    @functools.partial(
        pl.kernel, out_type=jax.ShapeDtypeStruct((n, width), src.dtype), mesh=mesh,
        scratch_types=[pltpu.VMEM((per_worker,), jnp.int32), pltpu.VMEM((2, win, width), src.dtype),
                       pltpu.SemaphoreType.DMA((2,)), pltpu.SemaphoreType.DMA((2,))],
        name="gather_rows")
    def gather(src_hbm, idx_hbm, out_hbm, idx_v, rows_v, fetch_sem, store_sem):
        worker = lax.axis_index("subcore") * sc.num_cores + lax.axis_index("core")
        base = worker * per_worker
        pltpu.sync_copy(idx_hbm.at[pl.ds(base, per_worker)], idx_v)

        def fetch(chunk, buf):
            return pltpu.make_async_copy(src_hbm.at[idx_v.at[pl.ds(chunk * win, win)]], rows_v.at[buf],
                                         fetch_sem.at[buf])

        def store(chunk, buf):
            return pltpu.make_async_copy(rows_v.at[buf], out_hbm.at[pl.ds(base + chunk * win, win)],
                                         store_sem.at[buf])

        @pl.loop(0, n_pairs)
        def _(p):
            for buf in range(2):
                @pl.when(p > 0)
                def _():
                    store(2 * p - 2 + buf, buf).wait()
                fetch(2 * p + buf, buf).start()
            for buf in range(2):
                fetch(2 * p + buf, buf).wait()
                store(2 * p + buf, buf).start()

        for buf in range(2):
            store(2 * n_pairs - 2 + buf, buf).wait()

    return gather(src, idx)


def _dispatch_rows(dest_t, src, n_rows):
    t, width = src.shape
    win = GATHER_WINDOW
    sc = plsc.get_sparse_core_info()
    n_workers = sc.num_cores * sc.num_subcores
    per_worker = t // n_workers
    n_chunks = per_worker // win
    n_pairs = n_chunks // 2
    assert n_pairs * 2 * win * n_workers == t
    idx = dest_t.reshape(TOP_K, n_workers, n_chunks, win).transpose(1, 0, 2, 3)
    idx = idx.reshape(n_workers, TOP_K * n_chunks, win)
    mesh = plsc.VectorSubcoreMesh(core_axis_name="core", subcore_axis_name="subcore")

    @functools.partial(
        pl.kernel, out_type=jax.ShapeDtypeStruct((n_rows, width), src.dtype), mesh=mesh,
        scratch_types=[pltpu.VMEM((TOP_K * n_chunks, win), jnp.int32), pltpu.VMEM((2, win, width), src.dtype),
                       pltpu.SemaphoreType.DMA((2,)), pltpu.SemaphoreType.DMA((2,))],
        name="dispatch_rows")
    def dispatch(src_hbm, idx_hbm, out_hbm, idx_v, rows_v, fetch_sem, store_sem):
        worker = lax.axis_index("subcore") * sc.num_cores + lax.axis_index("core")
        base = worker * per_worker
        pltpu.sync_copy(idx_hbm.at[worker], idx_v)

        def fetch(chunk, buf):
            return pltpu.make_async_copy(src_hbm.at[pl.ds(base + chunk * win, win)], rows_v.at[buf],
                                         fetch_sem.at[buf])

        def store(chunk, k, buf):
            return pltpu.make_async_copy(rows_v.at[buf], out_hbm.at[idx_v.at[k * n_chunks + chunk]],
                                         store_sem.at[buf])

        @pl.loop(0, n_pairs)
        def _(p):
            for buf in range(2):
                @pl.when(p > 0)
                def _():
                    for k in range(TOP_K):
                        store(2 * p - 2 + buf, k, buf).wait()
                fetch(2 * p + buf, buf).start()
            for buf in range(2):
                fetch(2 * p + buf, buf).wait()
                for k in range(TOP_K):
                    store(2 * p + buf, k, buf).start()

        for buf in range(2):
            for k in range(TOP_K):
                store(2 * n_pairs - 2 + buf, k, buf).wait()

    return dispatch(src, idx)


def _expert_kernel(block_e_ref, n_valid_ref, first_ref, next_e_ref, row_block_ref,
                   x_ref, wg_hbm, bg_ref, wu_hbm, bu_ref, wd_hbm, bd_ref,
                   y_ref, wg_f32, wu_f32, wd_f32, wg_bf, wu_bf, wd_bf, sem):
    i = pl.program_id(0)
    n_valid = n_valid_ref[i]
    staged = ((wg_hbm, wg_f32, wg_bf), (wu_hbm, wu_f32, wu_bf), (wd_hbm, wd_f32, wd_bf))

    def weight_copy(m, expert):
        return pltpu.make_async_copy(staged[m][0].at[expert], staged[m][1], sem.at[m])

    @pl.when(i == 0)
    def _():
        for m in range(3):
            weight_copy(m, block_e_ref[0]).start()

    @pl.when(first_ref[i] == 1)
    def _():
        for m in range(3):
            weight_copy(m, block_e_ref[i]).wait()
            staged[m][2][...] = staged[m][1][...].astype(jnp.bfloat16)

            @pl.when(next_e_ref[i] >= 0)
            def _():
                weight_copy(m, next_e_ref[i]).start()

    def expert_mlp(rows):
        row = lax.broadcasted_iota(jnp.int32, (rows, 1), 0)
        x_hi, x_lo = _unpack_bf16_pairs(jnp.where(row < n_valid, x_ref[:rows, :], 0))
        xb = jnp.concatenate([x_hi, x_lo], axis=1).astype(jnp.bfloat16)
        gt = jnp.minimum(jnp.dot(xb, wg_bf[...], preferred_element_type=jnp.float32) + bg_ref[0], SWIGLU_LIMIT)
        up = jnp.clip(jnp.dot(xb, wu_bf[...], preferred_element_type=jnp.float32) + bu_ref[0],
                      -SWIGLU_LIMIT, SWIGLU_LIMIT)
        hid = gt * jax.nn.sigmoid(SWIGLU_ALPHA * gt) * (up + 1.0)
        y = jnp.dot(hid.astype(jnp.bfloat16), wd_bf[...], preferred_element_type=jnp.float32) + bd_ref[0]
        y_ref[:rows, :] = _pack_bf16_pairs(y.astype(jnp.bfloat16))
        if rows < y_ref.shape[0]:
            y_ref[rows:, :] = jnp.zeros((y_ref.shape[0] - rows, y_ref.shape[1]), y_ref.dtype)

    row_options = (0,) + EXPERT_PARTIAL_ROWS + (x_ref.shape[0],)
    for lo, hi in zip(row_options[:-1], row_options[1:]):
        pl.when(jnp.logical_and(n_valid > lo, n_valid <= hi))(functools.partial(expert_mlp, hi))


def _experts(block_e, n_valid, x_rows, w_gate, b_gate, w_up, b_up, w_down, b_down):
    n_rows = x_rows.shape[0]
    bm = EXPERT_ROWS
    n_blocks = n_rows // bm
    d_ff = w_gate.shape[2]

    blocks = jnp.arange(n_blocks, dtype=jnp.int32)
    last_used = jnp.sum((n_valid > 0).astype(jnp.int32)) - 1
    row_block = jnp.minimum(blocks, last_used)
    block_e = jnp.sum(jnp.where(blocks[None, :] == row_block[:, None], block_e[None, :], 0), axis=1)

    is_first = jnp.concatenate([jnp.ones((1,), jnp.int32), (block_e[1:] != block_e[:-1]).astype(jnp.int32)])
    next_e = jnp.min(jnp.where(block_e[None, :] > block_e[:, None], block_e[None, :], N_EXPERTS), axis=1)
    next_e = jnp.where(next_e == N_EXPERTS, -1, next_e)

    def by_expert(shape):
        return pl.BlockSpec(shape, lambda i, be, *_: (be[i],) + (0,) * (len(shape) - 1))

    hbm = pl.BlockSpec(memory_space=pl.ANY)
    return pl.pallas_call(
        _expert_kernel,
        grid_spec=pltpu.PrefetchScalarGridSpec(
            num_scalar_prefetch=5,
            grid=(n_blocks,),
            in_specs=[
                pl.BlockSpec((bm, D_MODEL // 2), lambda i, be, nv, fi, ne, rb: (rb[i], 0)),
                hbm, by_expert((1, 1, d_ff)),
                hbm, by_expert((1, 1, d_ff)),
                hbm, by_expert((1, 1, D_MODEL)),
            ],
            out_specs=pl.BlockSpec((bm, D_MODEL // 2), lambda i, be, nv, fi, ne, rb: (rb[i], 0)),
            scratch_shapes=[
                pltpu.VMEM((D_MODEL, d_ff), jnp.float32),
                pltpu.VMEM((D_MODEL, d_ff), jnp.float32),
                pltpu.VMEM((d_ff, D_MODEL), jnp.float32),
                pltpu.VMEM((D_MODEL, d_ff), jnp.bfloat16),
                pltpu.VMEM((D_MODEL, d_ff), jnp.bfloat16),
                pltpu.VMEM((d_ff, D_MODEL), jnp.bfloat16),
                pltpu.SemaphoreType.DMA((3,)),
            ],
        ),
        out_shape=jax.ShapeDtypeStruct((n_rows, D_MODEL // 2), jnp.int32),
        compiler_params=pltpu.CompilerParams(
            dimension_semantics=("arbitrary",), vmem_limit_bytes=VMEM_LIMIT_BYTES,
            allow_input_fusion=[k in (7, 9, 11) for k in range(12)]),
        name="experts",
    )(block_e, n_valid, is_first, next_e, row_block, x_rows, w_gate, b_gate.reshape(N_EXPERTS, 1, d_ff), w_up,
      b_up.reshape(N_EXPERTS, 1, d_ff), w_down, b_down.reshape(N_EXPERTS, 1, D_MODEL))


def _combine_kernel(h_ref, y_ref, meta_ref, g_ref, b_ref, o_ref):
    tm = h_ref.shape[0]
    meta_cols = jnp.concatenate([meta_ref[...], jnp.zeros((LANES - META_ROWS, tm), jnp.float32)], axis=0).T
    ffn_hi, ffn_lo = 0.0, 0.0
    for k in range(TOP_K):
        gate = meta_cols[:, 2 * TOP_K + k:2 * TOP_K + k + 1]
        y_hi, y_lo = _unpack_bf16_pairs(y_ref[k])
        ffn_hi = ffn_hi + gate * y_hi
        ffn_lo = ffn_lo + gate * y_lo
    ffn = jnp.concatenate([ffn_hi, ffn_lo], axis=1)
    o_ref[...] = _layer_norm(DEEPNORM_ALPHA * h_ref[...] + ffn, g_ref[...], b_ref[...])


def _combine(h, y_tok, meta, ln2_g, ln2_b):
    t = h.shape[0]
    tm = COMBINE_ROWS
    return pl.pallas_call(
        _combine_kernel,
        grid=(t // tm,),
        in_specs=[
            pl.BlockSpec((tm, D_MODEL), lambda i: (i, 0)),
            pl.BlockSpec((TOP_K, tm, D_MODEL // 2), lambda i: (0, i, 0)),
            pl.BlockSpec((META_ROWS, tm), lambda i: (0, i)),
            pl.BlockSpec((1, D_MODEL), lambda i: (0, 0)),
            pl.BlockSpec((1, D_MODEL), lambda i: (0, 0)),
        ],
        out_specs=pl.BlockSpec((tm, D_MODEL), lambda i: (i, 0)),
        out_shape=jax.ShapeDtypeStruct((t, D_MODEL), jnp.float32),
        compiler_params=pltpu.CompilerParams(
            dimension_semantics=("arbitrary",), vmem_limit_bytes=VMEM_LIMIT_BYTES),
        name="combine",
    )(h, y_tok, meta, ln2_g, ln2_b)


def _layer(x2, seq_len, w_in, b_in, sinks, ln_v_g, ln_v_b, w_spatial, b_spatial, w_out, b_out,
           ln1_g, ln1_b, w_router, b_router, w_gate, b_gate, w_up, b_up, w_down, b_down, ln2_g, ln2_b):
    t = x2.shape[0]
    tk = t * TOP_K
    bm = EXPERT_ROWS
    bf16 = jnp.bfloat16

    w_r_hi = w_router.astype(bf16)
    w_r_lo = (w_router - w_r_hi.astype(jnp.float32)).astype(bf16)
    lane_pad = jnp.zeros((D_MODEL, ROUTER_LANES - 2 * N_EXPERTS), bf16)
    w_r = jnp.concatenate([w_r_hi, w_r_lo, lane_pad], axis=1)
    w_r_hi_only = jnp.concatenate([w_r_hi, jnp.zeros_like(w_r_lo), lane_pad], axis=1)
    b_sp_full = jnp.repeat(b_spatial.T, GMLP_GROUP_DIM, axis=1)

    h, h_packed, meta, counts = _mixer(
        x2, sinks, w_in.astype(bf16), b_in[None], ln_v_g[None], ln_v_b[None], w_spatial, b_sp_full,
        w_out.astype(bf16), b_out[None], ln1_g[None], ln1_b[None], w_r, w_r_hi_only, b_router[:, None], seq_len)

    counts = counts[:, 0].astype(jnp.int32)
    experts = jnp.arange(N_EXPERTS, dtype=jnp.int32)
    padded = (counts + bm - 1) // bm * bm
    padded_end = jnp.sum(jnp.where(experts[None, :] <= experts[:, None], padded[None, :], 0), axis=1)
    padded_start = padded_end - padded
    n_blocks = tk // bm + N_EXPERTS
    n_rows = n_blocks * bm
    top_idx_t = meta[:TOP_K].astype(jnp.int32)
    rank_t = meta[TOP_K:2 * TOP_K].astype(jnp.int32)
    dest_t = rank_t + jnp.sum(
        jnp.where(top_idx_t[None] == experts[:, None, None], padded_start[:, None, None], 0), axis=0)
    block_start = jnp.arange(n_blocks, dtype=jnp.int32) * bm
    block_e = jnp.minimum(
        jnp.sum((padded_end[None, :] <= block_start[:, None]).astype(jnp.int32), axis=1), N_EXPERTS - 1)
    valid_end = jnp.sum(jnp.where(block_e[:, None] == experts[None, :], (padded_start + counts)[None, :], 0), axis=1)
    n_valid = jnp.clip(valid_end - block_start, 0, bm)

    x_rows = _dispatch_rows(dest_t, h_packed, n_rows)
    y_rows = _experts(block_e, n_valid, x_rows, w_gate, b_gate, w_up, b_up, w_down, b_down)
    y_tok = _gather_rows(dest_t.reshape(-1), y_rows).reshape(TOP_K, t, D_MODEL // 2)
    return _combine(h, y_tok, meta, ln2_g[None], ln2_b[None])


def kernel(x, w_in, b_in, sinks, ln_v_g, ln_v_b, w_spatial, b_spatial, w_out, b_out, ln1_g, ln1_b,
           w_router, b_router, w_gate, b_gate, w_up, b_up, w_down, b_down, ln2_g, ln2_b):
    batch, seq_len, d = x.shape
    x2 = x.reshape(batch * seq_len, d)
    for l in range(DEPTH):
        x2 = _layer(x2, seq_len, w_in[l], b_in[l], sinks[l], ln_v_g[l], ln_v_b[l], w_spatial[l],
                    b_spatial[l], w_out[l], b_out[l], ln1_g[l], ln1_b[l], w_router[l], b_router[l],
                    w_gate[l], b_gate[l], w_up[l], b_up[l], w_down[l], b_down[l], ln2_g[l], ln2_b[l])
    return x2.reshape(batch, seq_len, d)
```

```python
import functools

import jax
import jax.numpy as jnp
from jax import lax
from jax.experimental import pallas as pl
from jax.experimental.pallas import tpu as pltpu
from jax.experimental.pallas import tpu_sc as plsc

D_MODEL = 1024
HEAD_DIM = 64
N_Q_HEADS = 8
N_KV_HEADS = 2
Q_REP = N_Q_HEADS // N_KV_HEADS
ATTN_WIDTH = N_Q_HEADS * HEAD_DIM
KV_WIDTH = N_KV_HEADS * HEAD_DIM
ATTN_BLOCK = 128
N_GMLP_GROUPS = 8
GMLP_WIDTH = D_MODEL - ATTN_WIDTH
GMLP_GROUP_DIM = GMLP_WIDTH // N_GMLP_GROUPS
IN_WIDTH = ATTN_WIDTH + 2 * KV_WIDTH + 2 * GMLP_WIDTH
N_EXPERTS = 32
TOP_K = 4
SWIGLU_LIMIT = 7.0
SWIGLU_ALPHA = 1.702
LN_EPS = 1e-5
DEPTH = 1
DEEPNORM_ALPHA = (2.0 * DEPTH) ** 0.25
NEG_INF = -1e30
LOG2_E = 1.4426950408889634

LANES = 128
SUBLANES = 8

MIXER_ROWS = 512
PROJ_CHUNK = 256
EXPERT_ROWS = 1024
EXPERT_PARTIAL_ROWS = (128, 256, 384, 512, 768)
GATHER_WINDOW = 64
META_ROWS = 16
ROUTER_LANES = LANES
COMBINE_ROWS = 1024
VMEM_LIMIT_BYTES = 56 * 1024 * 1024

_O_K = ATTN_WIDTH
_O_V = _O_K + KV_WIDTH
_O_U = _O_V + KV_WIDTH
_O_G = _O_U + GMLP_WIDTH


def _pack_bf16_pairs(v):
    n = v.shape[1] // 2
    hi = lax.bitcast_convert_type(v[:, :n].astype(jnp.float32), jnp.int32)
    lo = lax.bitcast_convert_type(v[:, n:].astype(jnp.float32), jnp.int32)
    return hi | lax.shift_right_logical(lo, 16)


def _unpack_bf16_pairs(p):
    hi = lax.bitcast_convert_type(p & jnp.int32(-65536), jnp.float32)
    lo = lax.bitcast_convert_type(lax.shift_left(p, 16), jnp.float32)
    return hi, lo


def _layer_norm(v, g, b):
    mu = jnp.mean(v, axis=-1, keepdims=True)
    vc = v - mu
    var = jnp.mean(vc * vc, axis=-1, keepdims=True)
    return vc * lax.rsqrt(var + LN_EPS) * g + b


def _attention_block(q, kb, vb, bias_ref, bias_sel):
    outs = []
    for g in range(N_KV_HEADS):
        kg = kb[:, g * HEAD_DIM:(g + 1) * HEAD_DIM]
        vg = vb[:, g * HEAD_DIM:(g + 1) * HEAD_DIM]
        qg = jnp.concatenate(
            [q[:, (g * Q_REP + r) * HEAD_DIM:(g * Q_REP + r + 1) * HEAD_DIM] for r in range(Q_REP)],
            axis=0).astype(jnp.bfloat16)
        s = (lax.dot_general(qg, kg, (((1,), (1,)), ((), ())), preferred_element_type=jnp.float32)
             + bias_ref[g, bias_sel])
        p = jnp.exp2(s - jnp.max(s, axis=-1, keepdims=True))
        denom = jnp.sum(p, axis=-1, keepdims=True)
        o = jnp.dot(p.astype(jnp.bfloat16), vg, preferred_element_type=jnp.float32) / denom
        outs.extend(o[r * ATTN_BLOCK:(r + 1) * ATTN_BLOCK] for r in range(Q_REP))
    return jnp.concatenate(outs, axis=-1)


def _mixer_kernel(x_ref, w_in_ref, b_in_ref, bias_ref, lnv_g_ref, lnv_b_ref, grp_avg_ref,
                  w_sp_ref, b_sp_ref, w_out_ref, b_out_ref, ln1_g_ref, ln1_b_ref,
                  w_r_ref, w_r_hi_ref, b_r_ref, tri_ref,
                  h_ref, hp_ref, meta_ref, count_ref,
                  kv_prev_ref, hres_ref, *, steps_per_seq, n_steps):
    i = pl.program_id(0)
    first_step = (i % steps_per_seq) == 0
    tm = x_ref.shape[0]
    n_sub = tm // ATTN_BLOCK

    @pl.when(i == 0)
    def _():
        count_ref[...] = jnp.zeros_like(count_ref)

    @pl.when(first_step)
    def _():
        kv_prev_ref[...] = jnp.zeros_like(kv_prev_ref)

    def route_stages():
        h = _layer_norm(hres_ref[...], ln1_g_ref[...], ln1_b_ref[...])
        h_ref[...] = h
        yield
        h_hi = h.astype(jnp.bfloat16)
        hp_ref[...] = _pack_bf16_pairs(h_hi)
        h_lo = (h - h_hi.astype(jnp.float32)).astype(jnp.bfloat16)
        yield
        part = (jnp.dot(h_hi, w_r_ref[...], preferred_element_type=jnp.float32)
                + jnp.dot(h_lo, w_r_hi_ref[...], preferred_element_type=jnp.float32)).T
        logits = part[:N_EXPERTS] + part[N_EXPERTS:2 * N_EXPERTS] + b_r_ref[...]
        yield
        n_grp = N_EXPERTS // SUBLANES
        grp = [logits[SUBLANES * g:SUBLANES * (g + 1)] for g in range(n_grp)]
        sub = lax.broadcasted_iota(jnp.int32, (SUBLANES, tm), 0)
        beaten = [jnp.zeros((SUBLANES, tm), jnp.float32) for _ in range(n_grp)]
        for e2 in range(N_EXPERTS):
            g2, r2 = divmod(e2, SUBLANES)
            row = logits[e2:e2 + 1]
            for g in range(n_grp):
                if g > g2:
                    wins = jnp.where(row >= grp[g], 1.0, 0.0)
                elif g < g2:
                    wins = jnp.where(row > grp[g], 1.0, 0.0)
                else:
                    wins = jnp.where(sub > r2, jnp.where(row >= grp[g], 1.0, 0.0),
                                     jnp.where(row > grp[g], 1.0, 0.0))
                beaten[g] = beaten[g] + wins
            if e2 == N_EXPERTS // 2 - 1:
                yield
        place = jnp.concatenate(beaten, axis=0)
        expert_id = lax.broadcasted_iota(jnp.int32, (N_EXPERTS, tm), 0).astype(jnp.float32)
        onehot = jnp.where(place < TOP_K, 1.0, 0.0)
        yield
        before = (jnp.dot(onehot.astype(jnp.bfloat16), tri_ref[...], preferred_element_type=jnp.float32)
                  + count_ref[...])

        def pick(k, table):
            return jnp.sum(jnp.where(place == k, table, 0.0), axis=0, keepdims=True)

        vals = [pick(k, logits) for k in range(TOP_K)]
        exps = [jnp.exp(v - vals[0]) for v in vals]
        denom = exps[0] + exps[1] + exps[2] + exps[3]
        yield
        meta_ref[...] = jnp.zeros_like(meta_ref)
        for k in range(TOP_K):
            meta_ref[k:k + 1, :] = pick(k, expert_id)
            meta_ref[TOP_K + k:TOP_K + k + 1, :] = pick(k, before)
            meta_ref[2 * TOP_K + k:2 * TOP_K + k + 1, :] = exps[k] / denom
        count_ref[...] += jnp.sum(onehot, axis=1, keepdims=True)

    def mix_block(stages):
        x = x_ref[...]
        x_bf = x.astype(jnp.bfloat16)
        proj = []
        for c in range(IN_WIDTH // PROJ_CHUNK):
            cols = pl.ds(c * PROJ_CHUNK, PROJ_CHUNK)
            proj.append(jnp.dot(x_bf, w_in_ref[:, cols], preferred_element_type=jnp.float32) + b_in_ref[:, cols])
            next(stages, None)
        for _ in stages:
            pass

        def proj_cols(lo, hi):
            return jnp.concatenate(proj[lo // PROJ_CHUNK:hi // PROJ_CHUNK], axis=-1)

        q_all = proj_cols(0, _O_K) * (LOG2_E * HEAD_DIM ** -0.5)
        kv_all = proj_cols(_O_K, _O_U).astype(jnp.bfloat16)
        k_all, v_all = kv_all[:, :KV_WIDTH], kv_all[:, KV_WIDTH:]
        k_prev = kv_prev_ref[:, :KV_WIDTH]
        v_prev = kv_prev_ref[:, KV_WIDTH:]
        is_row0 = lax.broadcasted_iota(jnp.int32, (ATTN_BLOCK, KV_WIDTH), 0) == 0
        attn_blocks = []
        for sb in range(n_sub):
            rows = slice(sb * ATTN_BLOCK, (sb + 1) * ATTN_BLOCK)
            k_cur, v_cur = k_all[rows], v_all[rows]
            kb = jnp.concatenate([jnp.where(is_row0, 0, k_prev), k_cur], axis=0)
            vb = jnp.concatenate([jnp.where(is_row0, 0, v_prev), v_cur], axis=0)
            bias_sel = jnp.where(first_step, 1, 0) if sb == 0 else 0
            attn_blocks.append(_attention_block(q_all[rows], kb, vb, bias_ref, bias_sel))
            k_prev, v_prev = k_cur, v_cur
        kv_prev_ref[:, :KV_WIDTH] = k_prev
        kv_prev_ref[:, KV_WIDTH:] = v_prev
        attn = jnp.concatenate(attn_blocks, axis=0)

        u = jax.nn.gelu(proj_cols(_O_U, _O_G))
        gg = jax.nn.gelu(proj_cols(_O_G, IN_WIDTH))
        avg = grp_avg_ref[...]
        mu = jnp.dot(gg.astype(jnp.bfloat16), avg, preferred_element_type=jnp.float32)
        gc = gg - mu
        var = jnp.dot((gc * gc).astype(jnp.bfloat16), avg, preferred_element_type=jnp.float32)
        gn = (gc * lax.rsqrt(var + LN_EPS) * lnv_g_ref[...] + lnv_b_ref[...]).astype(jnp.bfloat16)
        causal = (lax.broadcasted_iota(jnp.int32, (ATTN_BLOCK, ATTN_BLOCK), 0)
                  >= lax.broadcasted_iota(jnp.int32, (ATTN_BLOCK, ATTN_BLOCK), 1))
        w_sp = [jnp.where(causal, w_sp_ref[g], 0.0).astype(jnp.bfloat16) for g in range(N_GMLP_GROUPS)]
        mixed_chunks = []
        for c in range(n_sub):
            rows = slice(c * ATTN_BLOCK, (c + 1) * ATTN_BLOCK)
            pieces = [
                jnp.dot(w_sp[g], gn[rows, g * GMLP_GROUP_DIM:(g + 1) * GMLP_GROUP_DIM],
                        preferred_element_type=jnp.float32)
                for g in range(N_GMLP_GROUPS)]
            mixed_chunks.append(jnp.concatenate(pieces, axis=-1) + b_sp_ref[...])
        sgu = u * jnp.concatenate(mixed_chunks, axis=0)

        mix = (jnp.dot(attn.astype(jnp.bfloat16), w_out_ref[:ATTN_WIDTH, :], preferred_element_type=jnp.float32)
               + jnp.dot(sgu.astype(jnp.bfloat16), w_out_ref[ATTN_WIDTH:, :], preferred_element_type=jnp.float32)
               + b_out_ref[...])
        hres_ref[...] = DEEPNORM_ALPHA * x + mix

    @pl.when(i == 0)
    def _():
        mix_block(iter(()))

    @pl.when(jnp.logical_and(i > 0, i < n_steps))
    def _():
        mix_block(route_stages())

    @pl.when(i == n_steps)
    def _():
        for _ in route_stages():
            pass


def _score_bias(sinks):
    t_idx = jnp.arange(Q_REP * ATTN_BLOCK)[:, None] % ATTN_BLOCK
    s_idx = jnp.arange(2 * ATTN_BLOCK)[None, :]
    diff = t_idx + ATTN_BLOCK - s_idx
    band = (diff >= 0) & (diff < ATTN_BLOCK)
    masks = jnp.stack([band, band & (s_idx >= ATTN_BLOCK)])
    bias = jnp.where(masks, 0.0, NEG_INF).astype(jnp.float32)
    sink_rows = jnp.repeat(sinks.reshape(N_KV_HEADS, Q_REP) * LOG2_E, ATTN_BLOCK, axis=1)
    return jnp.where(s_idx == 0, sink_rows[:, None, :, None], bias[None])


def _mixer(x2, sinks, w_in, b_in, lnv_g, lnv_b, w_sp, b_sp_full, w_out, b_out, ln1_g, ln1_b,
           w_r, w_r_hi, b_r, seq_len):
    t = x2.shape[0]
    tm = MIXER_ROWS
    n_steps = t // tm
    grp = jnp.arange(GMLP_WIDTH) // GMLP_GROUP_DIM
    grp_avg = jnp.where(grp[:, None] == grp[None, :], 1.0 / GMLP_GROUP_DIM, 0.0).astype(jnp.bfloat16)
    tri = (jnp.arange(tm)[:, None] < jnp.arange(tm)[None, :]).astype(jnp.bfloat16)

    def full(shape):
        return pl.BlockSpec(shape, lambda i: (0,) * len(shape))

    return pl.pallas_call(
        functools.partial(_mixer_kernel, steps_per_seq=seq_len // tm, n_steps=n_steps),
        grid=(n_steps + 1,),
        in_specs=[
            pl.BlockSpec((tm, D_MODEL), lambda i: (jnp.minimum(i, n_steps - 1), 0)),
            full((D_MODEL, IN_WIDTH)), full((1, IN_WIDTH)),
            full((N_KV_HEADS, 2, Q_REP * ATTN_BLOCK, 2 * ATTN_BLOCK)),
            full((1, GMLP_WIDTH)), full((1, GMLP_WIDTH)), full((GMLP_WIDTH, GMLP_WIDTH)),
            full((N_GMLP_GROUPS, ATTN_BLOCK, ATTN_BLOCK)), full((ATTN_BLOCK, GMLP_WIDTH)),
            full((D_MODEL, D_MODEL)), full((1, D_MODEL)), full((1, D_MODEL)), full((1, D_MODEL)),
            full((D_MODEL, ROUTER_LANES)), full((D_MODEL, ROUTER_LANES)), full((N_EXPERTS, 1)), full((tm, tm)),
        ],
        out_specs=[
            pl.BlockSpec((tm, D_MODEL), lambda i: (jnp.maximum(i - 1, 0), 0)),
            pl.BlockSpec((tm, D_MODEL // 2), lambda i: (jnp.maximum(i - 1, 0), 0)),
            pl.BlockSpec((META_ROWS, tm), lambda i: (0, jnp.maximum(i - 1, 0))),
            pl.BlockSpec((N_EXPERTS, 1), lambda i: (0, 0)),
        ],
        out_shape=[
            jax.ShapeDtypeStruct((t, D_MODEL), jnp.float32),
            jax.ShapeDtypeStruct((t, D_MODEL // 2), jnp.int32),
            jax.ShapeDtypeStruct((META_ROWS, t), jnp.float32),
            jax.ShapeDtypeStruct((N_EXPERTS, 1), jnp.float32),
        ],
        scratch_shapes=[pltpu.VMEM((ATTN_BLOCK, 2 * KV_WIDTH), jnp.bfloat16),
                        pltpu.VMEM((tm, D_MODEL), jnp.float32)],
        compiler_params=pltpu.CompilerParams(
            dimension_semantics=("arbitrary",), vmem_limit_bytes=VMEM_LIMIT_BYTES,
            allow_input_fusion=[k in (1, 9) for k in range(17)]),
        name="mixer",
    )(x2, w_in, b_in, _score_bias(sinks), lnv_g, lnv_b, grp_avg, w_sp, b_sp_full, w_out, b_out, ln1_g, ln1_b,
      w_r, w_r_hi, b_r, tri)


def _gather_rows(idx, src):
    n = idx.shape[0]
    width = src.shape[1]
    win = GATHER_WINDOW
    sc = plsc.get_sparse_core_info()
    n_workers = sc.num_cores * sc.num_subcores
    per_worker = n // n_workers
    n_pairs = per_worker // (2 * win)
    assert n_pairs * 2 * win * n_workers == n
    mesh = plsc.VectorSubcoreMesh(core_axis_name="core", subcore_axis_name="subcore")

    @functools.partial(
        pl.kernel, out_type=jax.ShapeDtypeStruct((n, width), src.dtype), mesh=mesh,
        scratch_types=[pltpu.VMEM((per_worker,), jnp.int32), pltpu.VMEM((2, win, width), src.dtype),
                       pltpu.SemaphoreType.DMA((2,)), pltpu.SemaphoreType.DMA((2,))],
        name="gather_rows")
    def gather(src_hbm, idx_hbm, out_hbm, idx_v, rows_v, fetch_sem, store_sem):
        worker = lax.axis_index("subcore") * sc.num_cores + lax.axis_index("core")
        base = worker * per_worker
        pltpu.sync_copy(idx_hbm.at[pl.ds(base, per_worker)], idx_v)

        def fetch(chunk, buf):
            return pltpu.make_async_copy(src_hbm.at[idx_v.at[pl.ds(chunk * win, win)]], rows_v.at[buf],
                                         fetch_sem.at[buf])

        def store(chunk, buf):
            return pltpu.make_async_copy(rows_v.at[buf], out_hbm.at[pl.ds(base + chunk * win, win)],
                                         store_sem.at[buf])

        @pl.loop(0, n_pairs)
        def _(p):
            for buf in range(2):
                @pl.when(p > 0)
                def _():
                    store(2 * p - 2 + buf, buf).wait()
                fetch(2 * p + buf, buf).start()
            for buf in range(2):
                fetch(2 * p + buf, buf).wait()
                store(2 * p + buf, buf).start()

        for buf in range(2):
            store(2 * n_pairs - 2 + buf, buf).wait()

    return gather(src, idx)


def _dispatch_rows(dest_t, src, n_rows):
    t, width = src.shape
    win = GATHER_WINDOW
    sc = plsc.get_sparse_core_info()
    n_workers = sc.num_cores * sc.num_subcores
    per_worker = t // n_workers
    n_chunks = per_worker // win
    n_pairs = n_chunks // 2
    assert n_pairs * 2 * win * n_workers == t
    idx = dest_t.reshape(TOP_K, n_workers, n_chunks, win).transpose(1, 0, 2, 3)
    idx = idx.reshape(n_workers, TOP_K * n_chunks, win)
    mesh = plsc.VectorSubcoreMesh(core_axis_name="core", subcore_axis_name="subcore")

    @functools.partial(
        pl.kernel, out_type=jax.ShapeDtypeStruct((n_rows, width), src.dtype), mesh=mesh,
        scratch_types=[pltpu.VMEM((TOP_K * n_chunks, win), jnp.int32), pltpu.VMEM((2, win, width), src.dtype),
                       pltpu.SemaphoreType.DMA((2,)), pltpu.SemaphoreType.DMA((2,))],
        name="dispatch_rows")
    def dispatch(src_hbm, idx_hbm, out_hbm, idx_v, rows_v, fetch_sem, store_sem):
        worker = lax.axis_index("subcore") * sc.num_cores + lax.axis_index("core")
        base = worker * per_worker
        pltpu.sync_copy(idx_hbm.at[worker], idx_v)

        def fetch(chunk, buf):
            return pltpu.make_async_copy(src_hbm.at[pl.ds(base + chunk * win, win)], rows_v.at[buf],
                                         fetch_sem.at[buf])

        def store(chunk, k, buf):
            return pltpu.make_async_copy(rows_v.at[buf], out_hbm.at[idx_v.at[k * n_chunks + chunk]],
                                         store_sem.at[buf])

        @pl.loop(0, n_pairs)
        def _(p):
            for buf in range(2):
                @pl.when(p > 0)
                def _():
                    for k in range(TOP_K):
                        store(2 * p - 2 + buf, k, buf).wait()
                fetch(2 * p + buf, buf).start()
            for buf in range(2):
                fetch(2 * p + buf, buf).wait()
                for k in range(TOP_K):
                    store(2 * p + buf, k, buf).start()

        for buf in range(2):
            for k in range(TOP_K):
                store(2 * n_pairs - 2 + buf, k, buf).wait()

    return dispatch(src, idx)


def _expert_kernel(block_e_ref, n_valid_ref, first_ref, next_e_ref, row_block_ref,
                   x_ref, wg_hbm, wu_hbm, wd_hbm, bias_ref,
                   y_ref, wg_f32, wu_f32, wd_f32, wg_bf, wu_bf, wd_bf, sem):
    i = pl.program_id(0)
    n_valid = n_valid_ref[i]
    staged = ((wg_hbm, wg_f32, wg_bf), (wu_hbm, wu_f32, wu_bf), (wd_hbm, wd_f32, wd_bf))

    def weight_copy(m, expert):
        return pltpu.make_async_copy(staged[m][0].at[expert], staged[m][1], sem.at[m])

    @pl.when(i == 0)
    def _():
        for m in range(3):
            weight_copy(m, block_e_ref[0]).start()

    @pl.when(first_ref[i] == 1)
    def _():
        for m in range(3):
            weight_copy(m, block_e_ref[i]).wait()
            staged[m][2][...] = staged[m][1][...].astype(jnp.bfloat16)

            @pl.when(next_e_ref[i] >= 0)
            def _():
                weight_copy(m, next_e_ref[i]).start()

    def expert_mlp(rows):
        row = lax.broadcasted_iota(jnp.int32, (rows, 1), 0)
        x_hi, x_lo = _unpack_bf16_pairs(jnp.where(row < n_valid, x_ref[:rows, :], 0))
        xb = jnp.concatenate([x_hi, x_lo], axis=1).astype(jnp.bfloat16)
        gt = jnp.minimum(jnp.dot(xb, wg_bf[...], preferred_element_type=jnp.float32) + bias_ref[0, 0:1], SWIGLU_LIMIT)
        up = jnp.clip(jnp.dot(xb, wu_bf[...], preferred_element_type=jnp.float32) + bias_ref[0, 1:2],
                      -SWIGLU_LIMIT, SWIGLU_LIMIT)
        hid = gt * jax.nn.sigmoid(SWIGLU_ALPHA * gt) * (up + 1.0)
        y = jnp.dot(hid.astype(jnp.bfloat16), wd_bf[...], preferred_element_type=jnp.float32) + bias_ref[0, 2:3]
        y_ref[:rows, :] = _pack_bf16_pairs(y.astype(jnp.bfloat16))
        if rows < y_ref.shape[0]:
            y_ref[rows:, :] = jnp.zeros((y_ref.shape[0] - rows, y_ref.shape[1]), y_ref.dtype)

    row_options = (0,) + EXPERT_PARTIAL_ROWS + (x_ref.shape[0],)
    for lo, hi in zip(row_options[:-1], row_options[1:]):
        pl.when(jnp.logical_and(n_valid > lo, n_valid <= hi))(functools.partial(expert_mlp, hi))


def _experts(block_e, n_valid, x_rows, w_gate, b_gate, w_up, b_up, w_down, b_down):
    n_rows = x_rows.shape[0]
    bm = EXPERT_ROWS
    n_blocks = n_rows // bm
    d_ff = w_gate.shape[2]

    blocks = jnp.arange(n_blocks, dtype=jnp.int32)
    last_used = jnp.sum((n_valid > 0).astype(jnp.int32)) - 1
    row_block = jnp.minimum(blocks, last_used)
    block_e = jnp.sum(jnp.where(blocks[None, :] == row_block[:, None], block_e[None, :], 0), axis=1)

    is_first = jnp.concatenate([jnp.ones((1,), jnp.int32), (block_e[1:] != block_e[:-1]).astype(jnp.int32)])
    next_e = jnp.min(jnp.where(block_e[None, :] > block_e[:, None], block_e[None, :], N_EXPERTS), axis=1)
    next_e = jnp.where(next_e == N_EXPERTS, -1, next_e)

    def by_expert(shape):
        return pl.BlockSpec(shape, lambda i, be, *_: (be[i],) + (0,) * (len(shape) - 1))

    hbm = pl.BlockSpec(memory_space=pl.ANY)
    assert d_ff == D_MODEL
    biases = jnp.zeros((N_EXPERTS, SUBLANES, D_MODEL), jnp.float32)
    biases = biases.at[:, 0].set(b_gate).at[:, 1].set(b_up).at[:, 2].set(b_down)
    return pl.pallas_call(
        _expert_kernel,
        grid_spec=pltpu.PrefetchScalarGridSpec(
            num_scalar_prefetch=5,
            grid=(n_blocks,),
            in_specs=[
                pl.BlockSpec((bm, D_MODEL // 2), lambda i, be, nv, fi, ne, rb: (rb[i], 0)),
                hbm, hbm, hbm, by_expert((1, SUBLANES, D_MODEL)),
            ],
            out_specs=pl.BlockSpec((bm, D_MODEL // 2), lambda i, be, nv, fi, ne, rb: (rb[i], 0)),
            scratch_shapes=[
                pltpu.VMEM((D_MODEL, d_ff), jnp.float32),
                pltpu.VMEM((D_MODEL, d_ff), jnp.float32),
                pltpu.VMEM((d_ff, D_MODEL), jnp.float32),
                pltpu.VMEM((D_MODEL, d_ff), jnp.bfloat16),
                pltpu.VMEM((D_MODEL, d_ff), jnp.bfloat16),
                pltpu.VMEM((d_ff, D_MODEL), jnp.bfloat16),
                pltpu.SemaphoreType.DMA((3,)),
            ],
        ),
        out_shape=jax.ShapeDtypeStruct((n_rows, D_MODEL // 2), jnp.int32),
        compiler_params=pltpu.CompilerParams(
            dimension_semantics=("arbitrary",), vmem_limit_bytes=VMEM_LIMIT_BYTES),
        name="experts",
    )(block_e, n_valid, is_first, next_e, row_block, x_rows, w_gate, w_up, w_down, biases)


def _combine_kernel(h_ref, y_ref, meta_ref, g_ref, b_ref, o_ref):
    tm = h_ref.shape[0]
    meta_cols = jnp.concatenate([meta_ref[...], jnp.zeros((LANES - META_ROWS, tm), jnp.float32)], axis=0).T
    ffn_hi, ffn_lo = 0.0, 0.0
    for k in range(TOP_K):
        gate = meta_cols[:, 2 * TOP_K + k:2 * TOP_K + k + 1]
        y_hi, y_lo = _unpack_bf16_pairs(y_ref[k])
        ffn_hi = ffn_hi + gate * y_hi
        ffn_lo = ffn_lo + gate * y_lo
    ffn = jnp.concatenate([ffn_hi, ffn_lo], axis=1)
    o_ref[...] = _layer_norm(DEEPNORM_ALPHA * h_ref[...] + ffn, g_ref[...], b_ref[...])


def _combine(h, y_tok, meta, ln2_g, ln2_b):
    t = h.shape[0]
    tm = COMBINE_ROWS
    return pl.pallas_call(
        _combine_kernel,
        grid=(t // tm,),
        in_specs=[
            pl.BlockSpec((tm, D_MODEL), lambda i: (i, 0)),
            pl.BlockSpec((TOP_K, tm, D_MODEL // 2), lambda i: (0, i, 0)),
            pl.BlockSpec((META_ROWS, tm), lambda i: (0, i)),
            pl.BlockSpec((1, D_MODEL), lambda i: (0, 0)),
            pl.BlockSpec((1, D_MODEL), lambda i: (0, 0)),
        ],
        out_specs=pl.BlockSpec((tm, D_MODEL), lambda i: (i, 0)),
        out_shape=jax.ShapeDtypeStruct((t, D_MODEL), jnp.float32),
        compiler_params=pltpu.CompilerParams(
            dimension_semantics=("arbitrary",), vmem_limit_bytes=VMEM_LIMIT_BYTES),
        name="combine",
    )(h, y_tok, meta, ln2_g, ln2_b)


def _layer(x2, seq_len, w_in, b_in, sinks, ln_v_g, ln_v_b, w_spatial, b_spatial, w_out, b_out,
           ln1_g, ln1_b, w_router, b_router, w_gate, b_gate, w_up, b_up, w_down, b_down, ln2_g, ln2_b):
    t = x2.shape[0]
    tk = t * TOP_K
    bm = EXPERT_ROWS
    bf16 = jnp.bfloat16

    w_r_hi = w_router.astype(bf16)
    w_r_lo = (w_router - w_r_hi.astype(jnp.float32)).astype(bf16)
    lane_pad = jnp.zeros((D_MODEL, ROUTER_LANES - 2 * N_EXPERTS), bf16)
    w_r = jnp.concatenate([w_r_hi, w_r_lo, lane_pad], axis=1)
    w_r_hi_only = jnp.concatenate([w_r_hi, jnp.zeros_like(w_r_lo), lane_pad], axis=1)
    b_sp_full = jnp.repeat(b_spatial.T, GMLP_GROUP_DIM, axis=1)

    h, h_packed, meta, counts = _mixer(
        x2, sinks, w_in.astype(bf16), b_in[None], ln_v_g[None], ln_v_b[None], w_spatial, b_sp_full,
        w_out.astype(bf16), b_out[None], ln1_g[None], ln1_b[None], w_r, w_r_hi_only, b_router[:, None], seq_len)

    counts = counts[:, 0].astype(jnp.int32)
    experts = jnp.arange(N_EXPERTS, dtype=jnp.int32)
    padded = (counts + bm - 1) // bm * bm
    padded_end = jnp.sum(jnp.where(experts[None, :] <= experts[:, None], padded[None, :], 0), axis=1)
    padded_start = padded_end - padded
    n_blocks = tk // bm + N_EXPERTS
    n_rows = n_blocks * bm
    top_idx_t = meta[:TOP_K].astype(jnp.int32)
    rank_t = meta[TOP_K:2 * TOP_K].astype(jnp.int32)
    dest_t = rank_t + jnp.sum(
        jnp.where(top_idx_t[None] == experts[:, None, None], padded_start[:, None, None], 0), axis=0)
    block_start = jnp.arange(n_blocks, dtype=jnp.int32) * bm
    block_e = jnp.minimum(
        jnp.sum((padded_end[None, :] <= block_start[:, None]).astype(jnp.int32), axis=1), N_EXPERTS - 1)
    valid_end = jnp.sum(jnp.where(block_e[:, None] == experts[None, :], (padded_start + counts)[None, :], 0), axis=1)
    n_valid = jnp.clip(valid_end - block_start, 0, bm)

    x_rows = _dispatch_rows(dest_t, h_packed, n_rows)
    y_rows = _experts(block_e, n_valid, x_rows, w_gate, b_gate, w_up, b_up, w_down, b_down)
    y_tok = _gather_rows(dest_t.reshape(-1), y_rows).reshape(TOP_K, t, D_MODEL // 2)
    return _combine(h, y_tok, meta, ln2_g[None], ln2_b[None])


def kernel(x, w_in, b_in, sinks, ln_v_g, ln_v_b, w_spatial, b_spatial, w_out, b_out, ln1_g, ln1_b,
           w_router, b_router, w_gate, b_gate, w_up, b_up, w_down, b_down, ln2_g, ln2_b):
    batch, seq_len, d = x.shape
    x2 = x.reshape(batch * seq_len, d)
    for l in range(DEPTH):
        x2 = _layer(x2, seq_len, w_in[l], b_in[l], sinks[l], ln_v_g[l], ln_v_b[l], w_spatial[l],
                    b_spatial[l], w_out[l], b_out[l], ln1_g[l], ln1_b[l], w_router[l], b_router[l],
                    w_gate[l], b_gate[l], w_up[l], b_up[l], w_down[l], b_down[l], ln2_g[l], ln2_b[l])
    return x2.reshape(batch, seq_len, d)
```

```python
import functools

import jax
import jax.numpy as jnp
from jax import lax
from jax.experimental import pallas as pl
from jax.experimental.pallas import tpu as pltpu
from jax.experimental.pallas import tpu_sc as plsc

D_MODEL = 1024
HEAD_DIM = 64
N_Q_HEADS = 8
N_KV_HEADS = 2
Q_REP = N_Q_HEADS // N_KV_HEADS
ATTN_WIDTH = N_Q_HEADS * HEAD_DIM
KV_WIDTH = N_KV_HEADS * HEAD_DIM
ATTN_BLOCK = 128
N_GMLP_GROUPS = 8
GMLP_WIDTH = D_MODEL - ATTN_WIDTH
GMLP_GROUP_DIM = GMLP_WIDTH // N_GMLP_GROUPS
IN_WIDTH = ATTN_WIDTH + 2 * KV_WIDTH + 2 * GMLP_WIDTH
N_EXPERTS = 32
TOP_K = 4
SWIGLU_LIMIT = 7.0
SWIGLU_ALPHA = 1.702
LN_EPS = 1e-5
DEPTH = 1
DEEPNORM_ALPHA = (2.0 * DEPTH) ** 0.25
NEG_INF = -1e30
LOG2_E = 1.4426950408889634

LANES = 128
SUBLANES = 8

MIXER_ROWS = 512
PROJ_CHUNK = 256
EXPERT_ROWS = 1024
EXPERT_PARTIAL_ROWS = (128, 256, 384, 512, 768)
GATHER_WINDOW = 64
META_ROWS = 16
ROUTER_LANES = LANES
COMBINE_ROWS = 1024
VMEM_LIMIT_BYTES = 56 * 1024 * 1024

_O_K = ATTN_WIDTH
_O_V = _O_K + KV_WIDTH
_O_U = _O_V + KV_WIDTH
_O_G = _O_U + GMLP_WIDTH


def _pack_bf16_pairs(v):
    n = v.shape[1] // 2
    hi = lax.bitcast_convert_type(v[:, :n].astype(jnp.float32), jnp.int32)
    lo = lax.bitcast_convert_type(v[:, n:].astype(jnp.float32), jnp.int32)
    return hi | lax.shift_right_logical(lo, 16)


def _unpack_bf16_pairs(p):
    hi = lax.bitcast_convert_type(p & jnp.int32(-65536), jnp.float32)
    lo = lax.bitcast_convert_type(lax.shift_left(p, 16), jnp.float32)
    return hi, lo


def _layer_norm(v, g, b):
    mu = jnp.mean(v, axis=-1, keepdims=True)
    vc = v - mu
    var = jnp.mean(vc * vc, axis=-1, keepdims=True)
    return vc * lax.rsqrt(var + LN_EPS) * g + b


def _attention_block(q, kb, vb, bias_ref, bias_sel):
    outs = []
    for g in range(N_KV_HEADS):
        kg = kb[:, g * HEAD_DIM:(g + 1) * HEAD_DIM]
        vg = vb[:, g * HEAD_DIM:(g + 1) * HEAD_DIM]
        qg = jnp.concatenate(
            [q[:, (g * Q_REP + r) * HEAD_DIM:(g * Q_REP + r + 1) * HEAD_DIM] for r in range(Q_REP)],
            axis=0).astype(jnp.bfloat16)
        s = (lax.dot_general(qg, kg, (((1,), (1,)), ((), ())), preferred_element_type=jnp.float32)
             + bias_ref[g, bias_sel])
        p = jnp.exp2(s - jnp.max(s, axis=-1, keepdims=True))
        denom = jnp.sum(p, axis=-1, keepdims=True)
        o = jnp.dot(p.astype(jnp.bfloat16), vg, preferred_element_type=jnp.float32) / denom
        outs.extend(o[r * ATTN_BLOCK:(r + 1) * ATTN_BLOCK] for r in range(Q_REP))
    return jnp.concatenate(outs, axis=-1)


def _mixer_kernel(x_ref, w_in_ref, b_in_ref, bias_ref, lnv_g_ref, lnv_b_ref, grp_avg_ref,
                  w_sp_ref, b_sp_ref, w_out_ref, b_out_ref, ln1_g_ref, ln1_b_ref,
                  w_r_ref, w_r_hi_ref, b_r_ref, tri_ref,
                  h_ref, hp_ref, meta_ref, count_ref,
                  kv_prev_ref, hres_ref, *, steps_per_seq, n_steps):
    i = pl.program_id(0)
    first_step = (i % steps_per_seq) == 0
    tm = x_ref.shape[0]
    n_sub = tm // ATTN_BLOCK

    @pl.when(i == 0)
    def _():
        count_ref[...] = jnp.zeros_like(count_ref)

    @pl.when(first_step)
    def _():
        kv_prev_ref[...] = jnp.zeros_like(kv_prev_ref)

    def route_stages():
        h = _layer_norm(hres_ref[...], ln1_g_ref[...], ln1_b_ref[...])
        h_ref[...] = h
        yield
        h_hi = h.astype(jnp.bfloat16)
        hp_ref[...] = _pack_bf16_pairs(h_hi)
        h_lo = (h - h_hi.astype(jnp.float32)).astype(jnp.bfloat16)
        yield
        part = (jnp.dot(h_hi, w_r_ref[...], preferred_element_type=jnp.float32)
                + jnp.dot(h_lo, w_r_hi_ref[...], preferred_element_type=jnp.float32)).T
        logits = part[:N_EXPERTS] + part[N_EXPERTS:2 * N_EXPERTS] + b_r_ref[...]
        yield
        n_grp = N_EXPERTS // SUBLANES
        grp = [logits[SUBLANES * g:SUBLANES * (g + 1)] for g in range(n_grp)]
        sub = lax.broadcasted_iota(jnp.int32, (SUBLANES, tm), 0)
        beaten = [jnp.zeros((SUBLANES, tm), jnp.float32) for _ in range(n_grp)]
        for e2 in range(N_EXPERTS):
            g2, r2 = divmod(e2, SUBLANES)
            row = logits[e2:e2 + 1]
            for g in range(n_grp):
                if g > g2:
                    wins = jnp.where(row >= grp[g], 1.0, 0.0)
                elif g < g2:
                    wins = jnp.where(row > grp[g], 1.0, 0.0)
                else:
                    wins = jnp.where(sub > r2, jnp.where(row >= grp[g], 1.0, 0.0),
                                     jnp.where(row > grp[g], 1.0, 0.0))
                beaten[g] = beaten[g] + wins
            if e2 == N_EXPERTS // 2 - 1:
                yield
        place = jnp.concatenate(beaten, axis=0)
        expert_id = lax.broadcasted_iota(jnp.int32, (N_EXPERTS, tm), 0).astype(jnp.float32)
        onehot = jnp.where(place < TOP_K, 1.0, 0.0)
        yield
        before = (jnp.dot(onehot.astype(jnp.bfloat16), tri_ref[...], preferred_element_type=jnp.float32)
                  + count_ref[...])

        def pick(k, table):
            return jnp.sum(jnp.where(place == k, table, 0.0), axis=0, keepdims=True)

        vals = [pick(k, logits) for k in range(TOP_K)]
        exps = [jnp.exp(v - vals[0]) for v in vals]
        denom = exps[0] + exps[1] + exps[2] + exps[3]
        yield
        meta_ref[...] = jnp.zeros_like(meta_ref)
        for k in range(TOP_K):
            meta_ref[k:k + 1, :] = pick(k, expert_id)
            meta_ref[TOP_K + k:TOP_K + k + 1, :] = pick(k, before)
            meta_ref[2 * TOP_K + k:2 * TOP_K + k + 1, :] = exps[k] / denom
        count_ref[...] += jnp.sum(onehot, axis=1, keepdims=True)

    def mix_block(stages):
        x = x_ref[...]
        x_bf = x.astype(jnp.bfloat16)
        proj = []
        for c in range(IN_WIDTH // PROJ_CHUNK):
            cols = pl.ds(c * PROJ_CHUNK, PROJ_CHUNK)
            proj.append(jnp.dot(x_bf, w_in_ref[:, cols], preferred_element_type=jnp.float32) + b_in_ref[:, cols])
            next(stages, None)
        for _ in stages:
            pass

        def proj_cols(lo, hi):
            return jnp.concatenate(proj[lo // PROJ_CHUNK:hi // PROJ_CHUNK], axis=-1)

        q_all = proj_cols(0, _O_K) * (LOG2_E * HEAD_DIM ** -0.5)
        kv_all = proj_cols(_O_K, _O_U).astype(jnp.bfloat16)
        k_all, v_all = kv_all[:, :KV_WIDTH], kv_all[:, KV_WIDTH:]
        k_prev = kv_prev_ref[:, :KV_WIDTH]
        v_prev = kv_prev_ref[:, KV_WIDTH:]
        is_row0 = lax.broadcasted_iota(jnp.int32, (ATTN_BLOCK, KV_WIDTH), 0) == 0
        attn_blocks = []
        for sb in range(n_sub):
            rows = slice(sb * ATTN_BLOCK, (sb + 1) * ATTN_BLOCK)
            k_cur, v_cur = k_all[rows], v_all[rows]
            kb = jnp.concatenate([jnp.where(is_row0, 0, k_prev), k_cur], axis=0)
            vb = jnp.concatenate([jnp.where(is_row0, 0, v_prev), v_cur], axis=0)
            bias_sel = jnp.where(first_step, 1, 0) if sb == 0 else 0
            attn_blocks.append(_attention_block(q_all[rows], kb, vb, bias_ref, bias_sel))
            k_prev, v_prev = k_cur, v_cur
        kv_prev_ref[:, :KV_WIDTH] = k_prev
        kv_prev_ref[:, KV_WIDTH:] = v_prev
        attn = jnp.concatenate(attn_blocks, axis=0)

        u = jax.nn.gelu(proj_cols(_O_U, _O_G))
        gg = jax.nn.gelu(proj_cols(_O_G, IN_WIDTH))
        avg = grp_avg_ref[...]
        mu = jnp.dot(gg.astype(jnp.bfloat16), avg, preferred_element_type=jnp.float32)
        gc = gg - mu
        var = jnp.dot((gc * gc).astype(jnp.bfloat16), avg, preferred_element_type=jnp.float32)
        gn = (gc * lax.rsqrt(var + LN_EPS) * lnv_g_ref[...] + lnv_b_ref[...]).astype(jnp.bfloat16)
        causal = (lax.broadcasted_iota(jnp.int32, (ATTN_BLOCK, ATTN_BLOCK), 0)
                  >= lax.broadcasted_iota(jnp.int32, (ATTN_BLOCK, ATTN_BLOCK), 1))
        w_sp = [jnp.where(causal, w_sp_ref[g], 0.0).astype(jnp.bfloat16) for g in range(N_GMLP_GROUPS)]
        mixed_chunks = []
        for c in range(n_sub):
            rows = slice(c * ATTN_BLOCK, (c + 1) * ATTN_BLOCK)
            pieces = [
                jnp.dot(w_sp[g], gn[rows, g * GMLP_GROUP_DIM:(g + 1) * GMLP_GROUP_DIM],
                        preferred_element_type=jnp.float32)
                for g in range(N_GMLP_GROUPS)]
            mixed_chunks.append(jnp.concatenate(pieces, axis=-1) + b_sp_ref[...])
        sgu = u * jnp.concatenate(mixed_chunks, axis=0)

        mix = (jnp.dot(attn.astype(jnp.bfloat16), w_out_ref[:ATTN_WIDTH, :], preferred_element_type=jnp.float32)
               + jnp.dot(sgu.astype(jnp.bfloat16), w_out_ref[ATTN_WIDTH:, :], preferred_element_type=jnp.float32)
               + b_out_ref[...])
        hres_ref[...] = DEEPNORM_ALPHA * x + mix

    @pl.when(i == 0)
    def _():
        mix_block(iter(()))

    @pl.when(jnp.logical_and(i > 0, i < n_steps))
    def _():
        mix_block(route_stages())

    @pl.when(i == n_steps)
    def _():
        for _ in route_stages():
            pass


def _score_bias(sinks):
    t_idx = jnp.arange(Q_REP * ATTN_BLOCK)[:, None] % ATTN_BLOCK
    s_idx = jnp.arange(2 * ATTN_BLOCK)[None, :]
    diff = t_idx + ATTN_BLOCK - s_idx
    band = (diff >= 0) & (diff < ATTN_BLOCK)
    masks = jnp.stack([band, band & (s_idx >= ATTN_BLOCK)])
    bias = jnp.where(masks, 0.0, NEG_INF).astype(jnp.float32)
    sink_rows = jnp.repeat(sinks.reshape(N_KV_HEADS, Q_REP) * LOG2_E, ATTN_BLOCK, axis=1)
    return jnp.where(s_idx == 0, sink_rows[:, None, :, None], bias[None])


def _mixer(x2, sinks, w_in, b_in, lnv_g, lnv_b, w_sp, b_sp_full, w_out, b_out, ln1_g, ln1_b,
           w_r, w_r_hi, b_r, seq_len):
    t = x2.shape[0]
    tm = MIXER_ROWS
    n_steps = t // tm
    grp = jnp.arange(GMLP_WIDTH) // GMLP_GROUP_DIM
    grp_avg = jnp.where(grp[:, None] == grp[None, :], 1.0 / GMLP_GROUP_DIM, 0.0).astype(jnp.bfloat16)
    tri = (jnp.arange(tm)[:, None] < jnp.arange(tm)[None, :]).astype(jnp.bfloat16)

    def full(shape):
        return pl.BlockSpec(shape, lambda i: (0,) * len(shape))

    return pl.pallas_call(
        functools.partial(_mixer_kernel, steps_per_seq=seq_len // tm, n_steps=n_steps),
        grid=(n_steps + 1,),
        in_specs=[
            pl.BlockSpec((tm, D_MODEL), lambda i: (jnp.minimum(i, n_steps - 1), 0)),
            full((D_MODEL, IN_WIDTH)), full((1, IN_WIDTH)),
            full((N_KV_HEADS, 2, Q_REP * ATTN_BLOCK, 2 * ATTN_BLOCK)),
            full((1, GMLP_WIDTH)), full((1, GMLP_WIDTH)), full((GMLP_WIDTH, GMLP_WIDTH)),
            full((N_GMLP_GROUPS, ATTN_BLOCK, ATTN_BLOCK)), full((ATTN_BLOCK, GMLP_WIDTH)),
            full((D_MODEL, D_MODEL)), full((1, D_MODEL)), full((1, D_MODEL)), full((1, D_MODEL)),
            full((D_MODEL, ROUTER_LANES)), full((D_MODEL, ROUTER_LANES)), full((N_EXPERTS, 1)), full((tm, tm)),
        ],
        out_specs=[
            pl.BlockSpec((tm, D_MODEL), lambda i: (jnp.maximum(i - 1, 0), 0)),
            pl.BlockSpec((tm, D_MODEL // 2), lambda i: (jnp.maximum(i - 1, 0), 0)),
            pl.BlockSpec((META_ROWS, tm), lambda i: (0, jnp.maximum(i - 1, 0))),
            pl.BlockSpec((N_EXPERTS, 1), lambda i: (0, 0)),
        ],
        out_shape=[
            jax.ShapeDtypeStruct((t, D_MODEL), jnp.float32),
            jax.ShapeDtypeStruct((t, D_MODEL // 2), jnp.int32),
            jax.ShapeDtypeStruct((META_ROWS, t), jnp.float32),
            jax.ShapeDtypeStruct((N_EXPERTS, 1), jnp.float32),
        ],
        scratch_shapes=[pltpu.VMEM((ATTN_BLOCK, 2 * KV_WIDTH), jnp.bfloat16),
                        pltpu.VMEM((tm, D_MODEL), jnp.float32)],
        compiler_params=pltpu.CompilerParams(
            dimension_semantics=("arbitrary",), vmem_limit_bytes=VMEM_LIMIT_BYTES),
        name="mixer",
    )(x2, w_in, b_in, _score_bias(sinks), lnv_g, lnv_b, grp_avg, w_sp, b_sp_full, w_out, b_out, ln1_g, ln1_b,
      w_r, w_r_hi, b_r, tri)


def _gather_rows(idx, src):
    n = idx.shape[0]
    width = src.shape[1]
    win = GATHER_WINDOW
    sc = plsc.get_sparse_core_info()
    n_workers = sc.num_cores * sc.num_subcores
    per_worker = n // n_workers
    n_pairs = per_worker // (2 * win)
    assert n_pairs * 2 * win * n_workers == n
    mesh = plsc.VectorSubcoreMesh(core_axis_name="core", subcore_axis_name="subcore")

    @functools.partial(
        pl.kernel, out_type=jax.ShapeDtypeStruct((n, width), src.dtype), mesh=mesh,
        scratch_types=[pltpu.VMEM((per_worker,), jnp.int32), pltpu.VMEM((2, win, width), src.dtype),
                       pltpu.SemaphoreType.DMA((2,)), pltpu.SemaphoreType.DMA((2,))],
        name="gather_rows")
    def gather(src_hbm, idx_hbm, out_hbm, idx_v, rows_v, fetch_sem, store_sem):
        worker = lax.axis_index("subcore") * sc.num_cores + lax.axis_index("core")
        base = worker * per_worker
        pltpu.sync_copy(idx_hbm.at[pl.ds(base, per_worker)], idx_v)

        def fetch(chunk, buf):
            return pltpu.make_async_copy(src_hbm.at[idx_v.at[pl.ds(chunk * win, win)]], rows_v.at[buf],
                                         fetch_sem.at[buf])

        def store(chunk, buf):
            return pltpu.make_async_copy(rows_v.at[buf], out_hbm.at[pl.ds(base + chunk * win, win)],
                                         store_sem.at[buf])

        @pl.loop(0, n_pairs)
        def _(p):
            for buf in range(2):
                @pl.when(p > 0)
                def _():
                    store(2 * p - 2 + buf, buf).wait()
                fetch(2 * p + buf, buf).start()
            for buf in range(2):
                fetch(2 * p + buf, buf).wait()
                store(2 * p + buf, buf).start()

        for buf in range(2):
            store(2 * n_pairs - 2 + buf, buf).wait()

    return gather(src, idx)


def _dispatch_rows(dest_t, src, n_rows):
    t, width = src.shape
    win = GATHER_WINDOW
    sc = plsc.get_sparse_core_info()
    n_workers = sc.num_cores * sc.num_subcores
    per_worker = t // n_workers
    n_chunks = per_worker // win
    n_pairs = n_chunks // 2
    assert n_pairs * 2 * win * n_workers == t
    idx = dest_t.reshape(TOP_K, n_workers, n_chunks, win).transpose(1, 0, 2, 3)
    idx = idx.reshape(n_workers, TOP_K * n_chunks, win)
    mesh = plsc.VectorSubcoreMesh(core_axis_name="core", subcore_axis_name="subcore")

    @functools.partial(
        pl.kernel, out_type=jax.ShapeDtypeStruct((n_rows, width), src.dtype), mesh=mesh,
        scratch_types=[pltpu.VMEM((TOP_K * n_chunks, win), jnp.int32), pltpu.VMEM((2, win, width), src.dtype),
                       pltpu.SemaphoreType.DMA((2,)), pltpu.SemaphoreType.DMA((2,))],
        name="dispatch_rows")
    def dispatch(src_hbm, idx_hbm, out_hbm, idx_v, rows_v, fetch_sem, store_sem):
        worker = lax.axis_index("subcore") * sc.num_cores + lax.axis_index("core")
        base = worker * per_worker
        pltpu.sync_copy(idx_hbm.at[worker], idx_v)

        def fetch(chunk, buf):
            return pltpu.make_async_copy(src_hbm.at[pl.ds(base + chunk * win, win)], rows_v.at[buf],
                                         fetch_sem.at[buf])

        def store(chunk, k, buf):
            return pltpu.make_async_copy(rows_v.at[buf], out_hbm.at[idx_v.at[k * n_chunks + chunk]],
                                         store_sem.at[buf])

        @pl.loop(0, n_pairs)
        def _(p):
            for buf in range(2):
                @pl.when(p > 0)
                def _():
                    for k in range(TOP_K):
                        store(2 * p - 2 + buf, k, buf).wait()
                fetch(2 * p + buf, buf).start()
            for buf in range(2):
                fetch(2 * p + buf, buf).wait()
                for k in range(TOP_K):
                    store(2 * p + buf, k, buf).start()

        for buf in range(2):
            for k in range(TOP_K):
                store(2 * n_pairs - 2 + buf, k, buf).wait()

    return dispatch(src, idx)


def _expert_kernel(block_e_ref, n_valid_ref, first_ref, next_e_ref, row_block_ref,
                   x_ref, wg_hbm, wu_hbm, wd_hbm, bias_ref,
                   y_ref, wg_f32, wu_f32, wd_f32, wg_bf, wu_bf, wd_bf, sem):
    i = pl.program_id(0)
    n_valid = n_valid_ref[i]
    staged = ((wg_hbm, wg_f32, wg_bf), (wu_hbm, wu_f32, wu_bf), (wd_hbm, wd_f32, wd_bf))

    def weight_copy(m, expert):
        return pltpu.make_async_copy(staged[m][0].at[expert], staged[m][1], sem.at[m])

    @pl.when(i == 0)
    def _():
        for m in range(3):
            weight_copy(m, block_e_ref[0]).start()

    @pl.when(first_ref[i] == 1)
    def _():
        for m in range(3):
            weight_copy(m, block_e_ref[i]).wait()
            staged[m][2][...] = staged[m][1][...].astype(jnp.bfloat16)

            @pl.when(next_e_ref[i] >= 0)
            def _():
                weight_copy(m, next_e_ref[i]).start()

    def expert_mlp(rows):
        row = lax.broadcasted_iota(jnp.int32, (rows, 1), 0)
        x_hi, x_lo = _unpack_bf16_pairs(jnp.where(row < n_valid, x_ref[:rows, :], 0))
        xb = jnp.concatenate([x_hi, x_lo], axis=1).astype(jnp.bfloat16)
        gt = jnp.minimum(jnp.dot(xb, wg_bf[...], preferred_element_type=jnp.float32) + bias_ref[0, 0:1], SWIGLU_LIMIT)
        up = jnp.clip(jnp.dot(xb, wu_bf[...], preferred_element_type=jnp.float32) + bias_ref[0, 1:2],
                      -SWIGLU_LIMIT, SWIGLU_LIMIT)
        hid = gt * jax.nn.sigmoid(SWIGLU_ALPHA * gt) * (up + 1.0)
        y = jnp.dot(hid.astype(jnp.bfloat16), wd_bf[...], preferred_element_type=jnp.float32) + bias_ref[0, 2:3]
        y_ref[:rows, :] = _pack_bf16_pairs(y.astype(jnp.bfloat16))
        if rows < y_ref.shape[0]:
            y_ref[rows:, :] = jnp.zeros((y_ref.shape[0] - rows, y_ref.shape[1]), y_ref.dtype)

    row_options = (0,) + EXPERT_PARTIAL_ROWS + (x_ref.shape[0],)
    for lo, hi in zip(row_options[:-1], row_options[1:]):
        pl.when(jnp.logical_and(n_valid > lo, n_valid <= hi))(functools.partial(expert_mlp, hi))


def _experts(block_e, n_valid, x_rows, w_gate, b_gate, w_up, b_up, w_down, b_down):
    n_rows = x_rows.shape[0]
    bm = EXPERT_ROWS
    n_blocks = n_rows // bm
    d_ff = w_gate.shape[2]

    blocks = jnp.arange(n_blocks, dtype=jnp.int32)
    last_used = jnp.sum((n_valid > 0).astype(jnp.int32)) - 1
    row_block = jnp.minimum(blocks, last_used)
    block_e = jnp.sum(jnp.where(blocks[None, :] == row_block[:, None], block_e[None, :], 0), axis=1)

    is_first = jnp.concatenate([jnp.ones((1,), jnp.int32), (block_e[1:] != block_e[:-1]).astype(jnp.int32)])
    next_e = jnp.min(jnp.where(block_e[None, :] > block_e[:, None], block_e[None, :], N_EXPERTS), axis=1)
    next_e = jnp.where(next_e == N_EXPERTS, -1, next_e)

    def by_expert(shape):
        return pl.BlockSpec(shape, lambda i, be, *_: (be[i],) + (0,) * (len(shape) - 1))

    hbm = pl.BlockSpec(memory_space=pl.ANY)
    assert d_ff == D_MODEL
    biases = jnp.zeros((N_EXPERTS, SUBLANES, D_MODEL), jnp.float32)
    biases = biases.at[:, 0].set(b_gate).at[:, 1].set(b_up).at[:, 2].set(b_down)
    return pl.pallas_call(
        _expert_kernel,
        grid_spec=pltpu.PrefetchScalarGridSpec(
            num_scalar_prefetch=5,
            grid=(n_blocks,),
            in_specs=[
                pl.BlockSpec((bm, D_MODEL // 2), lambda i, be, nv, fi, ne, rb: (rb[i], 0)),
                hbm, hbm, hbm, by_expert((1, SUBLANES, D_MODEL)),
            ],
            out_specs=pl.BlockSpec((bm, D_MODEL // 2), lambda i, be, nv, fi, ne, rb: (rb[i], 0)),
            scratch_shapes=[
                pltpu.VMEM((D_MODEL, d_ff), jnp.float32),
                pltpu.VMEM((D_MODEL, d_ff), jnp.float32),
                pltpu.VMEM((d_ff, D_MODEL), jnp.float32),
                pltpu.VMEM((D_MODEL, d_ff), jnp.bfloat16),
                pltpu.VMEM((D_MODEL, d_ff), jnp.bfloat16),
                pltpu.VMEM((d_ff, D_MODEL), jnp.bfloat16),
                pltpu.SemaphoreType.DMA((3,)),
            ],
        ),
        out_shape=jax.ShapeDtypeStruct((n_rows, D_MODEL // 2), jnp.int32),
        compiler_params=pltpu.CompilerParams(
            dimension_semantics=("arbitrary",), vmem_limit_bytes=VMEM_LIMIT_BYTES),
        name="experts",
    )(block_e, n_valid, is_first, next_e, row_block, x_rows, w_gate, w_up, w_down, biases)


def _combine_kernel(h_ref, y_ref, meta_ref, g_ref, b_ref, o_ref):
    tm = h_ref.shape[0]
    meta_cols = jnp.concatenate([meta_ref[...], jnp.zeros((LANES - META_ROWS, tm), jnp.float32)], axis=0).T
    ffn_hi, ffn_lo = 0.0, 0.0
    for k in range(TOP_K):
        gate = meta_cols[:, 2 * TOP_K + k:2 * TOP_K + k + 1]
        y_hi, y_lo = _unpack_bf16_pairs(y_ref[k])
        ffn_hi = ffn_hi + gate * y_hi
        ffn_lo = ffn_lo + gate * y_lo
    ffn = jnp.concatenate([ffn_hi, ffn_lo], axis=1)
    o_ref[...] = _layer_norm(DEEPNORM_ALPHA * h_ref[...] + ffn, g_ref[...], b_ref[...])


def _combine(h, y_tok, meta, ln2_g, ln2_b):
    t = h.shape[0]
    tm = COMBINE_ROWS
    return pl.pallas_call(
        _combine_kernel,
        grid=(t // tm,),
        in_specs=[
            pl.BlockSpec((tm, D_MODEL), lambda i: (i, 0)),
            pl.BlockSpec((TOP_K, tm, D_MODEL // 2), lambda i: (0, i, 0)),
            pl.BlockSpec((META_ROWS, tm), lambda i: (0, i)),
            pl.BlockSpec((1, D_MODEL), lambda i: (0, 0)),
            pl.BlockSpec((1, D_MODEL), lambda i: (0, 0)),
        ],
        out_specs=pl.BlockSpec((tm, D_MODEL), lambda i: (i, 0)),
        out_shape=jax.ShapeDtypeStruct((t, D_MODEL), jnp.float32),
        compiler_params=pltpu.CompilerParams(
            dimension_semantics=("arbitrary",), vmem_limit_bytes=VMEM_LIMIT_BYTES),
        name="combine",
    )(h, y_tok, meta, ln2_g, ln2_b)


def _layer(x2, seq_len, w_in, b_in, sinks, ln_v_g, ln_v_b, w_spatial, b_spatial, w_out, b_out,
           ln1_g, ln1_b, w_router, b_router, w_gate, b_gate, w_up, b_up, w_down, b_down, ln2_g, ln2_b):
    t = x2.shape[0]
    tk = t * TOP_K
    bm = EXPERT_ROWS
    bf16 = jnp.bfloat16

    w_r_hi = w_router.astype(bf16)
    w_r_lo = (w_router - w_r_hi.astype(jnp.float32)).astype(bf16)
    lane_pad = jnp.zeros((D_MODEL, ROUTER_LANES - 2 * N_EXPERTS), bf16)
    w_r = jnp.concatenate([w_r_hi, w_r_lo, lane_pad], axis=1)
    w_r_hi_only = jnp.concatenate([w_r_hi, jnp.zeros_like(w_r_lo), lane_pad], axis=1)
    b_sp_full = jnp.repeat(b_spatial.T, GMLP_GROUP_DIM, axis=1)

    h, h_packed, meta, counts = _mixer(
        x2, sinks, w_in.astype(bf16), b_in[None], ln_v_g[None], ln_v_b[None], w_spatial, b_sp_full,
        w_out.astype(bf16), b_out[None], ln1_g[None], ln1_b[None], w_r, w_r_hi_only, b_router[:, None], seq_len)

    counts = counts[:, 0].astype(jnp.int32)
    experts = jnp.arange(N_EXPERTS, dtype=jnp.int32)
    padded = (counts + bm - 1) // bm * bm
    padded_end = jnp.sum(jnp.where(experts[None, :] <= experts[:, None], padded[None, :], 0), axis=1)
    padded_start = padded_end - padded
    n_blocks = tk // bm + N_EXPERTS
    n_rows = n_blocks * bm
    top_idx_t = meta[:TOP_K].astype(jnp.int32)
    rank_t = meta[TOP_K:2 * TOP_K].astype(jnp.int32)
    dest_t = rank_t + jnp.sum(
        jnp.where(top_idx_t[None] == experts[:, None, None], padded_start[:, None, None], 0), axis=0)
    block_start = jnp.arange(n_blocks, dtype=jnp.int32) * bm
    block_e = jnp.minimum(
        jnp.sum((padded_end[None, :] <= block_start[:, None]).astype(jnp.int32), axis=1), N_EXPERTS - 1)
    valid_end = jnp.sum(jnp.where(block_e[:, None] == experts[None, :], (padded_start + counts)[None, :], 0), axis=1)
    n_valid = jnp.clip(valid_end - block_start, 0, bm)

    x_rows = _dispatch_rows(dest_t, h_packed, n_rows)
    y_rows = _experts(block_e, n_valid, x_rows, w_gate, b_gate, w_up, b_up, w_down, b_down)
    y_tok = _gather_rows(dest_t.reshape(-1), y_rows).reshape(TOP_K, t, D_MODEL // 2)
    return _combine(h, y_tok, meta, ln2_g[None], ln2_b[None])


def kernel(x, w_in, b_in, sinks, ln_v_g, ln_v_b, w_spatial, b_spatial, w_out, b_out, ln1_g, ln1_b,
           w_router, b_router, w_gate, b_gate, w_up, b_up, w_down, b_down, ln2_g, ln2_b):
    batch, seq_len, d = x.shape
    x2 = x.reshape(batch * seq_len, d)
    for l in range(DEPTH):
        x2 = _layer(x2, seq_len, w_in[l], b_in[l], sinks[l], ln_v_g[l], ln_v_b[l], w_spatial[l],
                    b_spatial[l], w_out[l], b_out[l], ln1_g[l], ln1_b[l], w_router[l], b_router[l],
                    w_gate[l], b_gate[l], w_up[l], b_up[l], w_down[l], b_down[l], ln2_g[l], ln2_b[l])
    return x2.reshape(batch, seq_len, d)
```

```python
import functools

import jax
import jax.numpy as jnp
from jax import lax
from jax.experimental import pallas as pl
from jax.experimental.pallas import tpu as pltpu
from jax.experimental.pallas import tpu_sc as plsc

D_MODEL = 1024
HEAD_DIM = 64
N_Q_HEADS = 8
N_KV_HEADS = 2
Q_REP = N_Q_HEADS // N_KV_HEADS
ATTN_WIDTH = N_Q_HEADS * HEAD_DIM
KV_WIDTH = N_KV_HEADS * HEAD_DIM
ATTN_BLOCK = 128
N_GMLP_GROUPS = 8
GMLP_WIDTH = D_MODEL - ATTN_WIDTH
GMLP_GROUP_DIM = GMLP_WIDTH // N_GMLP_GROUPS
IN_WIDTH = ATTN_WIDTH + 2 * KV_WIDTH + 2 * GMLP_WIDTH
N_EXPERTS = 32
TOP_K = 4
SWIGLU_LIMIT = 7.0
SWIGLU_ALPHA = 1.702
LN_EPS = 1e-5
DEPTH = 1
DEEPNORM_ALPHA = (2.0 * DEPTH) ** 0.25
NEG_INF = -1e30
LOG2_E = 1.4426950408889634

LANES = 128
SUBLANES = 8

MIXER_ROWS = 512
PROJ_CHUNK = 256
EXPERT_ROWS = 1024
EXPERT_PARTIAL_ROWS = (128, 256, 384, 512, 768)
GATHER_WINDOW = 64
META_ROWS = 16
ROUTER_LANES = LANES
COMBINE_ROWS = 1024
VMEM_LIMIT_BYTES = 56 * 1024 * 1024

_O_K = ATTN_WIDTH
_O_V = _O_K + KV_WIDTH
_O_U = _O_V + KV_WIDTH
_O_G = _O_U + GMLP_WIDTH


def _pack_bf16_pairs(v):
    n = v.shape[1] // 2
    hi = lax.bitcast_convert_type(v[:, :n].astype(jnp.float32), jnp.int32)
    lo = lax.bitcast_convert_type(v[:, n:].astype(jnp.float32), jnp.int32)
    return hi | lax.shift_right_logical(lo, 16)


def _unpack_bf16_pairs(p):
    hi = lax.bitcast_convert_type(p & jnp.int32(-65536), jnp.float32)
    lo = lax.bitcast_convert_type(lax.shift_left(p, 16), jnp.float32)
    return hi, lo


def _layer_norm(v, g, b):
    mu = jnp.mean(v, axis=-1, keepdims=True)
    vc = v - mu
    var = jnp.mean(vc * vc, axis=-1, keepdims=True)
    return vc * lax.rsqrt(var + LN_EPS) * g + b


def _attention_block(q, kb, vb, bias_ref, bias_sel):
    outs = []
    for g in range(N_KV_HEADS):
        kg = kb[:, g * HEAD_DIM:(g + 1) * HEAD_DIM]
        vg = vb[:, g * HEAD_DIM:(g + 1) * HEAD_DIM]
        qg = jnp.concatenate(
            [q[:, (g * Q_REP + r) * HEAD_DIM:(g * Q_REP + r + 1) * HEAD_DIM] for r in range(Q_REP)],
            axis=0).astype(jnp.bfloat16)
        s = (lax.dot_general(qg, kg, (((1,), (1,)), ((), ())), preferred_element_type=jnp.float32)
             + bias_ref[g, bias_sel])
        p = jnp.exp2(s - jnp.max(s, axis=-1, keepdims=True))
        denom = jnp.sum(p, axis=-1, keepdims=True)
        o = jnp.dot(p.astype(jnp.bfloat16), vg, preferred_element_type=jnp.float32) / denom
        outs.extend(o[r * ATTN_BLOCK:(r + 1) * ATTN_BLOCK] for r in range(Q_REP))
    return jnp.concatenate(outs, axis=-1)


def _mixer_kernel(x_ref, w_in_ref, b_in_ref, bias_ref, lnv_g_ref, lnv_b_ref, grp_avg_ref,
                  w_sp_ref, b_sp_ref, w_out_ref, b_out_ref, ln1_g_ref, ln1_b_ref,
                  w_r_ref, w_r_hi_ref, b_r_ref, tri_ref,
                  h_ref, hp_ref, meta_ref, count_ref,
                  kv_prev_ref, hres_ref, *, steps_per_seq, n_steps):
    i = pl.program_id(0)
    first_step = (i % steps_per_seq) == 0
    tm = x_ref.shape[0]
    n_sub = tm // ATTN_BLOCK

    @pl.when(i == 0)
    def _():
        count_ref[...] = jnp.zeros_like(count_ref)

    @pl.when(first_step)
    def _():
        kv_prev_ref[...] = jnp.zeros_like(kv_prev_ref)

    def route_stages():
        h = _layer_norm(hres_ref[...], ln1_g_ref[...], ln1_b_ref[...])
        h_ref[...] = h
        yield
        h_hi = h.astype(jnp.bfloat16)
        hp_ref[...] = _pack_bf16_pairs(h_hi)
        h_lo = (h - h_hi.astype(jnp.float32)).astype(jnp.bfloat16)
        yield
        part = (jnp.dot(h_hi, w_r_ref[...], preferred_element_type=jnp.float32)
                + jnp.dot(h_lo, w_r_hi_ref[...], preferred_element_type=jnp.float32)).T
        logits = part[:N_EXPERTS] + part[N_EXPERTS:2 * N_EXPERTS] + b_r_ref[...]
        yield
        n_grp = N_EXPERTS // SUBLANES
        grp = [logits[SUBLANES * g:SUBLANES * (g + 1)] for g in range(n_grp)]
        sub = lax.broadcasted_iota(jnp.int32, (SUBLANES, tm), 0)
        beaten = [jnp.zeros((SUBLANES, tm), jnp.float32) for _ in range(n_grp)]
        for e2 in range(N_EXPERTS):
            g2, r2 = divmod(e2, SUBLANES)
            row = logits[e2:e2 + 1]
            for g in range(n_grp):
                if g > g2:
                    wins = jnp.where(row >= grp[g], 1.0, 0.0)
                elif g < g2:
                    wins = jnp.where(row > grp[g], 1.0, 0.0)
                else:
                    wins = jnp.where(sub > r2, jnp.where(row >= grp[g], 1.0, 0.0),
                                     jnp.where(row > grp[g], 1.0, 0.0))
                beaten[g] = beaten[g] + wins
            if e2 == N_EXPERTS // 2 - 1:
                yield
        place = jnp.concatenate(beaten, axis=0)
        expert_id = lax.broadcasted_iota(jnp.int32, (N_EXPERTS, tm), 0).astype(jnp.float32)
        onehot = jnp.where(place < TOP_K, 1.0, 0.0)
        yield
        before = (jnp.dot(onehot.astype(jnp.bfloat16), tri_ref[...], preferred_element_type=jnp.float32)
                  + count_ref[...])

        def pick(k, table):
            return jnp.sum(jnp.where(place == k, table, 0.0), axis=0, keepdims=True)

        vals = [pick(k, logits) for k in range(TOP_K)]
        exps = [jnp.exp(v - vals[0]) for v in vals]
        denom = exps[0] + exps[1] + exps[2] + exps[3]
        yield
        meta_ref[...] = jnp.zeros_like(meta_ref)
        for k in range(TOP_K):
            meta_ref[k:k + 1, :] = pick(k, expert_id)
            meta_ref[TOP_K + k:TOP_K + k + 1, :] = pick(k, before)
            meta_ref[2 * TOP_K + k:2 * TOP_K + k + 1, :] = exps[k] / denom
        count_ref[...] += jnp.sum(onehot, axis=1, keepdims=True)

    def mix_block(stages):
        x = x_ref[...]
        x_bf = x.astype(jnp.bfloat16)
        proj = []
        for c in range(IN_WIDTH // PROJ_CHUNK):
            cols = pl.ds(c * PROJ_CHUNK, PROJ_CHUNK)
            proj.append(jnp.dot(x_bf, w_in_ref[:, cols], preferred_element_type=jnp.float32) + b_in_ref[:, cols])
            next(stages, None)
        for _ in stages:
            pass

        def proj_cols(lo, hi):
            return jnp.concatenate(proj[lo // PROJ_CHUNK:hi // PROJ_CHUNK], axis=-1)

        q_all = proj_cols(0, _O_K) * (LOG2_E * HEAD_DIM ** -0.5)
        kv_all = proj_cols(_O_K, _O_U).astype(jnp.bfloat16)
        k_all, v_all = kv_all[:, :KV_WIDTH], kv_all[:, KV_WIDTH:]
        k_prev = kv_prev_ref[:, :KV_WIDTH]
        v_prev = kv_prev_ref[:, KV_WIDTH:]
        is_row0 = lax.broadcasted_iota(jnp.int32, (ATTN_BLOCK, KV_WIDTH), 0) == 0
        attn_blocks = []
        for sb in range(n_sub):
            rows = slice(sb * ATTN_BLOCK, (sb + 1) * ATTN_BLOCK)
            k_cur, v_cur = k_all[rows], v_all[rows]
            kb = jnp.concatenate([jnp.where(is_row0, 0, k_prev), k_cur], axis=0)
            vb = jnp.concatenate([jnp.where(is_row0, 0, v_prev), v_cur], axis=0)
            bias_sel = jnp.where(first_step, 1, 0) if sb == 0 else 0
            attn_blocks.append(_attention_block(q_all[rows], kb, vb, bias_ref, bias_sel))
            k_prev, v_prev = k_cur, v_cur
        kv_prev_ref[:, :KV_WIDTH] = k_prev
        kv_prev_ref[:, KV_WIDTH:] = v_prev
        attn = jnp.concatenate(attn_blocks, axis=0)

        u = jax.nn.gelu(proj_cols(_O_U, _O_G))
        gg = jax.nn.gelu(proj_cols(_O_G, IN_WIDTH))
        avg = grp_avg_ref[...]
        mu = jnp.dot(gg.astype(jnp.bfloat16), avg, preferred_element_type=jnp.float32)
        gc = gg - mu
        var = jnp.dot((gc * gc).astype(jnp.bfloat16), avg, preferred_element_type=jnp.float32)
        gn = (gc * lax.rsqrt(var + LN_EPS) * lnv_g_ref[...] + lnv_b_ref[...]).astype(jnp.bfloat16)
        causal = (lax.broadcasted_iota(jnp.int32, (ATTN_BLOCK, ATTN_BLOCK), 0)
                  >= lax.broadcasted_iota(jnp.int32, (ATTN_BLOCK, ATTN_BLOCK), 1))
        w_sp = [jnp.where(causal, w_sp_ref[g], 0.0).astype(jnp.bfloat16) for g in range(N_GMLP_GROUPS)]
        mixed_chunks = []
        for c in range(n_sub):
            rows = slice(c * ATTN_BLOCK, (c + 1) * ATTN_BLOCK)
            pieces = [
                jnp.dot(w_sp[g], gn[rows, g * GMLP_GROUP_DIM:(g + 1) * GMLP_GROUP_DIM],
                        preferred_element_type=jnp.float32)
                for g in range(N_GMLP_GROUPS)]
            mixed_chunks.append(jnp.concatenate(pieces, axis=-1) + b_sp_ref[...])
        sgu = u * jnp.concatenate(mixed_chunks, axis=0)

        mix = (jnp.dot(attn.astype(jnp.bfloat16), w_out_ref[:ATTN_WIDTH, :], preferred_element_type=jnp.float32)
               + jnp.dot(sgu.astype(jnp.bfloat16), w_out_ref[ATTN_WIDTH:, :], preferred_element_type=jnp.float32)
               + b_out_ref[...])
        hres_ref[...] = DEEPNORM_ALPHA * x + mix

    @pl.when(i == 0)
    def _():
        mix_block(iter(()))

    @pl.when(jnp.logical_and(i > 0, i < n_steps))
    def _():
        mix_block(route_stages())

    @pl.when(i == n_steps)
    def _():
        for _ in route_stages():
            pass


def _score_bias(sinks):
    t_idx = jnp.arange(Q_REP * ATTN_BLOCK)[:, None] % ATTN_BLOCK
    s_idx = jnp.arange(2 * ATTN_BLOCK)[None, :]
    diff = t_idx + ATTN_BLOCK - s_idx
    band = (diff >= 0) & (diff < ATTN_BLOCK)
    masks = jnp.stack([band, band & (s_idx >= ATTN_BLOCK)])
    bias = jnp.where(masks, 0.0, NEG_INF).astype(jnp.float32)
    sink_rows = jnp.repeat(sinks.reshape(N_KV_HEADS, Q_REP) * LOG2_E, ATTN_BLOCK, axis=1)
    return jnp.where(s_idx == 0, sink_rows[:, None, :, None], bias[None])


def _mixer(x2, sinks, w_in, b_in, lnv_g, lnv_b, w_sp, b_sp_full, w_out, b_out, ln1_g, ln1_b,
           w_r, w_r_hi, b_r, seq_len):
    t = x2.shape[0]
    tm = MIXER_ROWS
    n_steps = t // tm
    grp = jnp.arange(GMLP_WIDTH) // GMLP_GROUP_DIM
    grp_avg = jnp.where(grp[:, None] == grp[None, :], 1.0 / GMLP_GROUP_DIM, 0.0).astype(jnp.bfloat16)
    tri = (jnp.arange(tm)[:, None] < jnp.arange(tm)[None, :]).astype(jnp.bfloat16)

    def full(shape):
        return pl.BlockSpec(shape, lambda i: (0,) * len(shape))

    return pl.pallas_call(
        functools.partial(_mixer_kernel, steps_per_seq=seq_len // tm, n_steps=n_steps),
        grid=(n_steps + 1,),
        in_specs=[
            pl.BlockSpec((tm, D_MODEL), lambda i: (jnp.minimum(i, n_steps - 1), 0)),
            full((D_MODEL, IN_WIDTH)), full((1, IN_WIDTH)),
            full((N_KV_HEADS, 2, Q_REP * ATTN_BLOCK, 2 * ATTN_BLOCK)),
            full((1, GMLP_WIDTH)), full((1, GMLP_WIDTH)), full((GMLP_WIDTH, GMLP_WIDTH)),
            full((N_GMLP_GROUPS, ATTN_BLOCK, ATTN_BLOCK)), full((ATTN_BLOCK, GMLP_WIDTH)),
            full((D_MODEL, D_MODEL)), full((1, D_MODEL)), full((1, D_MODEL)), full((1, D_MODEL)),
            full((D_MODEL, ROUTER_LANES)), full((D_MODEL, ROUTER_LANES)), full((N_EXPERTS, 1)), full((tm, tm)),
        ],
        out_specs=[
            pl.BlockSpec((tm, D_MODEL), lambda i: (jnp.maximum(i - 1, 0), 0)),
            pl.BlockSpec((tm, D_MODEL // 2), lambda i: (jnp.maximum(i - 1, 0), 0)),
            pl.BlockSpec((META_ROWS, tm), lambda i: (0, jnp.maximum(i - 1, 0))),
            pl.BlockSpec((N_EXPERTS, 1), lambda i: (0, 0)),
        ],
        out_shape=[
            jax.ShapeDtypeStruct((t, D_MODEL), jnp.float32),
            jax.ShapeDtypeStruct((t, D_MODEL // 2), jnp.int32),
            jax.ShapeDtypeStruct((META_ROWS, t), jnp.float32),
            jax.ShapeDtypeStruct((N_EXPERTS, 1), jnp.float32),
        ],
        scratch_shapes=[pltpu.VMEM((ATTN_BLOCK, 2 * KV_WIDTH), jnp.bfloat16),
                        pltpu.VMEM((tm, D_MODEL), jnp.float32)],
        compiler_params=pltpu.CompilerParams(
            dimension_semantics=("arbitrary",), vmem_limit_bytes=VMEM_LIMIT_BYTES,
            allow_input_fusion=[k in (1, 9, 13, 14) for k in range(17)]),
        name="mixer",
    )(x2, w_in, b_in, _score_bias(sinks), lnv_g, lnv_b, grp_avg, w_sp, b_sp_full, w_out, b_out, ln1_g, ln1_b,
      w_r, w_r_hi, b_r, tri)


def _gather_rows(idx, src):
    n = idx.shape[0]
    width = src.shape[1]
    win = GATHER_WINDOW
    sc = plsc.get_sparse_core_info()
    n_workers = sc.num_cores * sc.num_subcores
    per_worker = n // n_workers
    n_pairs = per_worker // (2 * win)
    assert n_pairs * 2 * win * n_workers == n
    mesh = plsc.VectorSubcoreMesh(core_axis_name="core", subcore_axis_name="subcore")

    @functools.partial(
        pl.kernel, out_type=jax.ShapeDtypeStruct((n, width), src.dtype), mesh=mesh,
        scratch_types=[pltpu.VMEM((per_worker,), jnp.int32), pltpu.VMEM((2, win, width), src.dtype),
                       pltpu.SemaphoreType.DMA((2,)), pltpu.SemaphoreType.DMA((2,))],
        name="gather_rows")
    def gather(src_hbm, idx_hbm, out_hbm, idx_v, rows_v, fetch_sem, store_sem):
        worker = lax.axis_index("subcore") * sc.num_cores + lax.axis_index("core")
        base = worker * per_worker
        pltpu.sync_copy(idx_hbm.at[pl.ds(base, per_worker)], idx_v)

        def fetch(chunk, buf):
            return pltpu.make_async_copy(src_hbm.at[idx_v.at[pl.ds(chunk * win, win)]], rows_v.at[buf],
                                         fetch_sem.at[buf])

        def store(chunk, buf):
            return pltpu.make_async_copy(rows_v.at[buf], out_hbm.at[pl.ds(base + chunk * win, win)],
                                         store_sem.at[buf])

        @pl.loop(0, n_pairs)
        def _(p):
            for buf in range(2):
                @pl.when(p > 0)
                def _():
                    store(2 * p - 2 + buf, buf).wait()
                fetch(2 * p + buf, buf).start()
            for buf in range(2):
                fetch(2 * p + buf, buf).wait()
                store(2 * p + buf, buf).start()

        for buf in range(2):
            store(2 * n_pairs - 2 + buf, buf).wait()

    return gather(src, idx)


def _dispatch_rows(dest_t, src, n_rows):
    t, width = src.shape
    win = GATHER_WINDOW
    sc = plsc.get_sparse_core_info()
    n_workers = sc.num_cores * sc.num_subcores
    per_worker = t // n_workers
    n_chunks = per_worker // win
    n_pairs = n_chunks // 2
    assert n_pairs * 2 * win * n_workers == t
    idx = dest_t.reshape(TOP_K, n_workers, n_chunks, win).transpose(1, 0, 2, 3)
    idx = idx.reshape(n_workers, TOP_K * n_chunks, win)
    mesh = plsc.VectorSubcoreMesh(core_axis_name="core", subcore_axis_name="subcore")

    @functools.partial(
        pl.kernel, out_type=jax.ShapeDtypeStruct((n_rows, width), src.dtype), mesh=mesh,
        scratch_types=[pltpu.VMEM((TOP_K * n_chunks, win), jnp.int32), pltpu.VMEM((2, win, width), src.dtype),
                       pltpu.SemaphoreType.DMA((2,)), pltpu.SemaphoreType.DMA((2,))],
        name="dispatch_rows")
    def dispatch(src_hbm, idx_hbm, out_hbm, idx_v, rows_v, fetch_sem, store_sem):
        worker = lax.axis_index("subcore") * sc.num_cores + lax.axis_index("core")
        base = worker * per_worker
        pltpu.sync_copy(idx_hbm.at[worker], idx_v)

        def fetch(chunk, buf):
            return pltpu.make_async_copy(src_hbm.at[pl.ds(base + chunk * win, win)], rows_v.at[buf],
                                         fetch_sem.at[buf])

        def store(chunk, k, buf):
            return pltpu.make_async_copy(rows_v.at[buf], out_hbm.at[idx_v.at[k * n_chunks + chunk]],
                                         store_sem.at[buf])

        @pl.loop(0, n_pairs)
        def _(p):
            for buf in range(2):
                @pl.when(p > 0)
                def _():
                    for k in range(TOP_K):
                        store(2 * p - 2 + buf, k, buf).wait()
                fetch(2 * p + buf, buf).start()
            for buf in range(2):
                fetch(2 * p + buf, buf).wait()
                for k in range(TOP_K):
                    store(2 * p + buf, k, buf).start()

        for buf in range(2):
            for k in range(TOP_K):
                store(2 * n_pairs - 2 + buf, k, buf).wait()

    return dispatch(src, idx)


def _expert_kernel(block_e_ref, n_valid_ref, first_ref, next_e_ref, row_block_ref,
                   x_ref, wg_hbm, wu_hbm, wd_hbm, bias_ref,
                   y_ref, wg_f32, wu_f32, wd_f32, wg_bf, wu_bf, wd_bf, sem):
    i = pl.program_id(0)
    n_valid = n_valid_ref[i]
    staged = ((wg_hbm, wg_f32, wg_bf), (wu_hbm, wu_f32, wu_bf), (wd_hbm, wd_f32, wd_bf))

    def weight_copy(m, expert):
        return pltpu.make_async_copy(staged[m][0].at[expert], staged[m][1], sem.at[m])

    @pl.when(i == 0)
    def _():
        for m in range(3):
            weight_copy(m, block_e_ref[0]).start()

    @pl.when(first_ref[i] == 1)
    def _():
        for m in range(3):
            weight_copy(m, block_e_ref[i]).wait()
            staged[m][2][...] = staged[m][1][...].astype(jnp.bfloat16)

            @pl.when(next_e_ref[i] >= 0)
            def _():
                weight_copy(m, next_e_ref[i]).start()

    def expert_mlp(rows):
        row = lax.broadcasted_iota(jnp.int32, (rows, 1), 0)
        x_hi, x_lo = _unpack_bf16_pairs(jnp.where(row < n_valid, x_ref[:rows, :], 0))
        xb = jnp.concatenate([x_hi, x_lo], axis=1).astype(jnp.bfloat16)
        gt = jnp.minimum(jnp.dot(xb, wg_bf[...], preferred_element_type=jnp.float32) + bias_ref[0, 0:1], SWIGLU_LIMIT)
        up = jnp.clip(jnp.dot(xb, wu_bf[...], preferred_element_type=jnp.float32) + bias_ref[0, 1:2],
                      -SWIGLU_LIMIT, SWIGLU_LIMIT)
        hid = gt * jax.nn.sigmoid(SWIGLU_ALPHA * gt) * (up + 1.0)
        y = jnp.dot(hid.astype(jnp.bfloat16), wd_bf[...], preferred_element_type=jnp.float32) + bias_ref[0, 2:3]
        y_ref[:rows, :] = _pack_bf16_pairs(y.astype(jnp.bfloat16))
        if rows < y_ref.shape[0]:
            y_ref[rows:, :] = jnp.zeros((y_ref.shape[0] - rows, y_ref.shape[1]), y_ref.dtype)

    row_options = (0,) + EXPERT_PARTIAL_ROWS + (x_ref.shape[0],)
    for lo, hi in zip(row_options[:-1], row_options[1:]):
        pl.when(jnp.logical_and(n_valid > lo, n_valid <= hi))(functools.partial(expert_mlp, hi))


def _experts(block_e, n_valid, x_rows, w_gate, b_gate, w_up, b_up, w_down, b_down):
    n_rows = x_rows.shape[0]
    bm = EXPERT_ROWS
    n_blocks = n_rows // bm
    d_ff = w_gate.shape[2]

    blocks = jnp.arange(n_blocks, dtype=jnp.int32)
    last_used = jnp.sum((n_valid > 0).astype(jnp.int32)) - 1
    row_block = jnp.minimum(blocks, last_used)
    block_e = jnp.sum(jnp.where(blocks[None, :] == row_block[:, None], block_e[None, :], 0), axis=1)

    is_first = jnp.concatenate([jnp.ones((1,), jnp.int32), (block_e[1:] != block_e[:-1]).astype(jnp.int32)])
    next_e = jnp.min(jnp.where(block_e[None, :] > block_e[:, None], block_e[None, :], N_EXPERTS), axis=1)
    next_e = jnp.where(next_e == N_EXPERTS, -1, next_e)

    def by_expert(shape):
        return pl.BlockSpec(shape, lambda i, be, *_: (be[i],) + (0,) * (len(shape) - 1))

    hbm = pl.BlockSpec(memory_space=pl.ANY)
    assert d_ff == D_MODEL
    biases = jnp.zeros((N_EXPERTS, SUBLANES, D_MODEL), jnp.float32)
    biases = biases.at[:, 0].set(b_gate).at[:, 1].set(b_up).at[:, 2].set(b_down)
    return pl.pallas_call(
        _expert_kernel,
        grid_spec=pltpu.PrefetchScalarGridSpec(
            num_scalar_prefetch=5,
            grid=(n_blocks,),
            in_specs=[
                pl.BlockSpec((bm, D_MODEL // 2), lambda i, be, nv, fi, ne, rb: (rb[i], 0)),
                hbm, hbm, hbm, by_expert((1, SUBLANES, D_MODEL)),
            ],
            out_specs=pl.BlockSpec((bm, D_MODEL // 2), lambda i, be, nv, fi, ne, rb: (rb[i], 0)),
            scratch_shapes=[
                pltpu.VMEM((D_MODEL, d_ff), jnp.float32),
                pltpu.VMEM((D_MODEL, d_ff), jnp.float32),
                pltpu.VMEM((d_ff, D_MODEL), jnp.float32),
                pltpu.VMEM((D_MODEL, d_ff), jnp.bfloat16),
                pltpu.VMEM((D_MODEL, d_ff), jnp.bfloat16),
                pltpu.VMEM((d_ff, D_MODEL), jnp.bfloat16),
                pltpu.SemaphoreType.DMA((3,)),
            ],
        ),
        out_shape=jax.ShapeDtypeStruct((n_rows, D_MODEL // 2), jnp.int32),
        compiler_params=pltpu.CompilerParams(
            dimension_semantics=("arbitrary",), vmem_limit_bytes=VMEM_LIMIT_BYTES),
        name="experts",
    )(block_e, n_valid, is_first, next_e, row_block, x_rows, w_gate, w_up, w_down, biases)


def _combine_kernel(h_ref, y_ref, meta_ref, g_ref, b_ref, o_ref):
    tm = h_ref.shape[0]
    meta_cols = jnp.concatenate([meta_ref[...], jnp.zeros((LANES - META_ROWS, tm), jnp.float32)], axis=0).T
    ffn_hi, ffn_lo = 0.0, 0.0
    for k in range(TOP_K):
        gate = meta_cols[:, 2 * TOP_K + k:2 * TOP_K + k + 1]
        y_hi, y_lo = _unpack_bf16_pairs(y_ref[k])
        ffn_hi = ffn_hi + gate * y_hi
        ffn_lo = ffn_lo + gate * y_lo
    ffn = jnp.concatenate([ffn_hi, ffn_lo], axis=1)
    o_ref[...] = _layer_norm(DEEPNORM_ALPHA * h_ref[...] + ffn, g_ref[...], b_ref[...])


def _combine(h, y_tok, meta, ln2_g, ln2_b):
    t = h.shape[0]
    tm = COMBINE_ROWS
    return pl.pallas_call(
        _combine_kernel,
        grid=(t // tm,),
        in_specs=[
            pl.BlockSpec((tm, D_MODEL), lambda i: (i, 0)),
            pl.BlockSpec((TOP_K, tm, D_MODEL // 2), lambda i: (0, i, 0)),
            pl.BlockSpec((META_ROWS, tm), lambda i: (0, i)),
            pl.BlockSpec((1, D_MODEL), lambda i: (0, 0)),
            pl.BlockSpec((1, D_MODEL), lambda i: (0, 0)),
        ],
        out_specs=pl.BlockSpec((tm, D_MODEL), lambda i: (i, 0)),
        out_shape=jax.ShapeDtypeStruct((t, D_MODEL), jnp.float32),
        compiler_params=pltpu.CompilerParams(
            dimension_semantics=("arbitrary",), vmem_limit_bytes=VMEM_LIMIT_BYTES),
        name="combine",
    )(h, y_tok, meta, ln2_g, ln2_b)


def _layer(x2, seq_len, w_in, b_in, sinks, ln_v_g, ln_v_b, w_spatial, b_spatial, w_out, b_out,
           ln1_g, ln1_b, w_router, b_router, w_gate, b_gate, w_up, b_up, w_down, b_down, ln2_g, ln2_b):
    t = x2.shape[0]
    tk = t * TOP_K
    bm = EXPERT_ROWS
    bf16 = jnp.bfloat16

    w_r_hi = w_router.astype(bf16)
    w_r_lo = (w_router - w_r_hi.astype(jnp.float32)).astype(bf16)
    lane_pad = jnp.zeros((D_MODEL, ROUTER_LANES - 2 * N_EXPERTS), bf16)
    w_r = jnp.concatenate([w_r_hi, w_r_lo, lane_pad], axis=1)
    w_r_hi_only = jnp.concatenate([w_r_hi, jnp.zeros_like(w_r_lo), lane_pad], axis=1)
    b_sp_full = jnp.repeat(b_spatial.T, GMLP_GROUP_DIM, axis=1)

    h, h_packed, meta, counts = _mixer(
        x2, sinks, w_in.astype(bf16), b_in[None], ln_v_g[None], ln_v_b[None], w_spatial, b_sp_full,
        w_out.astype(bf16), b_out[None], ln1_g[None], ln1_b[None], w_r, w_r_hi_only, b_router[:, None], seq_len)

    counts = counts[:, 0].astype(jnp.int32)
    experts = jnp.arange(N_EXPERTS, dtype=jnp.int32)
    padded = (counts + bm - 1) // bm * bm
    padded_end = jnp.sum(jnp.where(experts[None, :] <= experts[:, None], padded[None, :], 0), axis=1)
    padded_start = padded_end - padded
    n_blocks = tk // bm + N_EXPERTS
    n_rows = n_blocks * bm
    top_idx_t = meta[:TOP_K].astype(jnp.int32)
    rank_t = meta[TOP_K:2 * TOP_K].astype(jnp.int32)
    dest_t = rank_t + jnp.sum(
        jnp.where(top_idx_t[None] == experts[:, None, None], padded_start[:, None, None], 0), axis=0)
    block_start = jnp.arange(n_blocks, dtype=jnp.int32) * bm
    block_e = jnp.minimum(
        jnp.sum((padded_end[None, :] <= block_start[:, None]).astype(jnp.int32), axis=1), N_EXPERTS - 1)
    valid_end = jnp.sum(jnp.where(block_e[:, None] == experts[None, :], (padded_start + counts)[None, :], 0), axis=1)
    n_valid = jnp.clip(valid_end - block_start, 0, bm)

    x_rows = _dispatch_rows(dest_t, h_packed, n_rows)
    y_rows = _experts(block_e, n_valid, x_rows, w_gate, b_gate, w_up, b_up, w_down, b_down)
    y_tok = _gather_rows(dest_t.reshape(-1), y_rows).reshape(TOP_K, t, D_MODEL // 2)
    return _combine(h, y_tok, meta, ln2_g[None], ln2_b[None])


def kernel(x, w_in, b_in, sinks, ln_v_g, ln_v_b, w_spatial, b_spatial, w_out, b_out, ln1_g, ln1_b,
           w_router, b_router, w_gate, b_gate, w_up, b_up, w_down, b_down, ln2_g, ln2_b):
    batch, seq_len, d = x.shape
    x2 = x.reshape(batch * seq_len, d)
    for l in range(DEPTH):
        x2 = _layer(x2, seq_len, w_in[l], b_in[l], sinks[l], ln_v_g[l], ln_v_b[l], w_spatial[l],
                    b_spatial[l], w_out[l], b_out[l], ln1_g[l], ln1_b[l], w_router[l], b_router[l],
                    w_gate[l], b_gate[l], w_up[l], b_up[l], w_down[l], b_down[l], ln2_g[l], ln2_b[l])
    return x2.reshape(batch, seq_len, d)
```

```python
import functools

import jax
import jax.numpy as jnp
from jax import lax
from jax.experimental import pallas as pl
from jax.experimental.pallas import tpu as pltpu
from jax.experimental.pallas import tpu_sc as plsc

D_MODEL = 1024
HEAD_DIM = 64
N_Q_HEADS = 8
N_KV_HEADS = 2
Q_REP = N_Q_HEADS // N_KV_HEADS
ATTN_WIDTH = N_Q_HEADS * HEAD_DIM
KV_WIDTH = N_KV_HEADS * HEAD_DIM
ATTN_BLOCK = 128
N_GMLP_GROUPS = 8
GMLP_WIDTH = D_MODEL - ATTN_WIDTH
GMLP_GROUP_DIM = GMLP_WIDTH // N_GMLP_GROUPS
IN_WIDTH = ATTN_WIDTH + 2 * KV_WIDTH + 2 * GMLP_WIDTH
N_EXPERTS = 32
TOP_K = 4
SWIGLU_LIMIT = 7.0
SWIGLU_ALPHA = 1.702
LN_EPS = 1e-5
DEPTH = 1
DEEPNORM_ALPHA = (2.0 * DEPTH) ** 0.25
NEG_INF = -1e30
LOG2_E = 1.4426950408889634

LANES = 128
SUBLANES = 8

MIXER_ROWS = 512
PROJ_CHUNK = 256
EXPERT_ROWS = 1024
EXPERT_PARTIAL_ROWS = (128, 256, 384, 512, 768)
GATHER_WINDOW = 64
META_ROWS = 16
ROUTER_LANES = LANES
COMBINE_ROWS = 1024
VMEM_LIMIT_BYTES = 56 * 1024 * 1024

_O_K = ATTN_WIDTH
_O_V = _O_K + KV_WIDTH
_O_U = _O_V + KV_WIDTH
_O_G = _O_U + GMLP_WIDTH


def _pack_bf16_pairs(v):
    n = v.shape[1] // 2
    hi = lax.bitcast_convert_type(v[:, :n].astype(jnp.float32), jnp.int32)
    lo = lax.bitcast_convert_type(v[:, n:].astype(jnp.float32), jnp.int32)
    return hi | lax.shift_right_logical(lo, 16)


def _unpack_bf16_pairs(p):
    hi = lax.bitcast_convert_type(p & jnp.int32(-65536), jnp.float32)
    lo = lax.bitcast_convert_type(lax.shift_left(p, 16), jnp.float32)
    return hi, lo


def _layer_norm(v, g, b):
    mu = jnp.mean(v, axis=-1, keepdims=True)
    vc = v - mu
    var = jnp.mean(vc * vc, axis=-1, keepdims=True)
    return vc * lax.rsqrt(var + LN_EPS) * g + b


def _attention_block(q, kb, vb, bias_ref, bias_sel):
    outs = []
    for g in range(N_KV_HEADS):
        kg = kb[:, g * HEAD_DIM:(g + 1) * HEAD_DIM]
        vg = vb[:, g * HEAD_DIM:(g + 1) * HEAD_DIM]
        qg = jnp.concatenate(
            [q[:, (g * Q_REP + r) * HEAD_DIM:(g * Q_REP + r + 1) * HEAD_DIM] for r in range(Q_REP)],
            axis=0).astype(jnp.bfloat16)
        s = (lax.dot_general(qg, kg, (((1,), (1,)), ((), ())), preferred_element_type=jnp.float32)
             + bias_ref[g, bias_sel])
        p = jnp.exp2(s - jnp.max(s, axis=-1, keepdims=True))
        denom = jnp.sum(p, axis=-1, keepdims=True)
        o = jnp.dot(p.astype(jnp.bfloat16), vg, preferred_element_type=jnp.float32) / denom
        outs.extend(o[r * ATTN_BLOCK:(r + 1) * ATTN_BLOCK] for r in range(Q_REP))
    return jnp.concatenate(outs, axis=-1)


def _mixer_kernel(x_ref, w_in_ref, b_in_ref, bias_ref, lnv_g_ref, lnv_b_ref, grp_avg_ref,
                  w_sp_ref, b_sp_ref, w_out_ref, b_out_ref, ln1_g_ref, ln1_b_ref,
                  w_r_ref, w_r_hi_ref, b_r_ref, tri_ref,
                  h_ref, hp_ref, meta_ref, count_ref,
                  kv_prev_ref, hres_ref, *, steps_per_seq, n_steps):
    i = pl.program_id(0)
    first_step = (i % steps_per_seq) == 0
    tm = x_ref.shape[0]
    n_sub = tm // ATTN_BLOCK

    @pl.when(i == 0)
    def _():
        count_ref[...] = jnp.zeros_like(count_ref)

    @pl.when(first_step)
    def _():
        kv_prev_ref[...] = jnp.zeros_like(kv_prev_ref)

    def route_stages():
        h = _layer_norm(hres_ref[...], ln1_g_ref[...], ln1_b_ref[...])
        h_ref[...] = h
        yield
        h_hi = h.astype(jnp.bfloat16)
        hp_ref[...] = _pack_bf16_pairs(h_hi)
        h_lo = (h - h_hi.astype(jnp.float32)).astype(jnp.bfloat16)
        yield
        part = (jnp.dot(h_hi, w_r_ref[...], preferred_element_type=jnp.float32)
                + jnp.dot(h_lo, w_r_hi_ref[...], preferred_element_type=jnp.float32)).T
        logits = part[:N_EXPERTS] + part[N_EXPERTS:2 * N_EXPERTS] + b_r_ref[...]
        yield
        n_grp = N_EXPERTS // SUBLANES
        grp = [logits[SUBLANES * g:SUBLANES * (g + 1)] for g in range(n_grp)]
        sub = lax.broadcasted_iota(jnp.int32, (SUBLANES, tm), 0)
        beaten = [jnp.zeros((SUBLANES, tm), jnp.float32) for _ in range(n_grp)]
        for e2 in range(N_EXPERTS):
            g2, r2 = divmod(e2, SUBLANES)
            row = logits[e2:e2 + 1]
            for g in range(n_grp):
                if g > g2:
                    wins = jnp.where(row >= grp[g], 1.0, 0.0)
                elif g < g2:
                    wins = jnp.where(row > grp[g], 1.0, 0.0)
                else:
                    wins = jnp.where(sub > r2, jnp.where(row >= grp[g], 1.0, 0.0),
                                     jnp.where(row > grp[g], 1.0, 0.0))
                beaten[g] = beaten[g] + wins
            if e2 == N_EXPERTS // 2 - 1:
                yield
        place = jnp.concatenate(beaten, axis=0)
        expert_id = lax.broadcasted_iota(jnp.int32, (N_EXPERTS, tm), 0).astype(jnp.float32)
        onehot = jnp.where(place < TOP_K, 1.0, 0.0)
        yield
        before = (jnp.dot(onehot.astype(jnp.bfloat16), tri_ref[...], preferred_element_type=jnp.float32)
                  + count_ref[...])

        def pick(k, table):
            return jnp.sum(jnp.where(place == k, table, 0.0), axis=0, keepdims=True)

        vals = [pick(k, logits) for k in range(TOP_K)]
        exps = [jnp.exp(v - vals[0]) for v in vals]
        denom = exps[0] + exps[1] + exps[2] + exps[3]
        yield
        meta_ref[...] = jnp.zeros_like(meta_ref)
        for k in range(TOP_K):
            meta_ref[k:k + 1, :] = pick(k, expert_id)
            meta_ref[TOP_K + k:TOP_K + k + 1, :] = pick(k, before)
            meta_ref[2 * TOP_K + k:2 * TOP_K + k + 1, :] = exps[k] / denom
        count_ref[...] += jnp.sum(onehot, axis=1, keepdims=True)

    def mix_block(stages):
        x = x_ref[...]
        x_bf = x.astype(jnp.bfloat16)
        proj = []
        for c in range(IN_WIDTH // PROJ_CHUNK):
            cols = pl.ds(c * PROJ_CHUNK, PROJ_CHUNK)
            proj.append(jnp.dot(x_bf, w_in_ref[:, cols], preferred_element_type=jnp.float32) + b_in_ref[:, cols])
            next(stages, None)
        for _ in stages:
            pass

        def proj_cols(lo, hi):
            return jnp.concatenate(proj[lo // PROJ_CHUNK:hi // PROJ_CHUNK], axis=-1)

        q_all = proj_cols(0, _O_K) * (LOG2_E * HEAD_DIM ** -0.5)
        kv_all = proj_cols(_O_K, _O_U).astype(jnp.bfloat16)
        k_all, v_all = kv_all[:, :KV_WIDTH], kv_all[:, KV_WIDTH:]
        k_prev = kv_prev_ref[:, :KV_WIDTH]
        v_prev = kv_prev_ref[:, KV_WIDTH:]
        is_row0 = lax.broadcasted_iota(jnp.int32, (ATTN_BLOCK, KV_WIDTH), 0) == 0
        attn_blocks = []
        for sb in range(n_sub):
            rows = slice(sb * ATTN_BLOCK, (sb + 1) * ATTN_BLOCK)
            k_cur, v_cur = k_all[rows], v_all[rows]
            kb = jnp.concatenate([jnp.where(is_row0, 0, k_prev), k_cur], axis=0)
            vb = jnp.concatenate([jnp.where(is_row0, 0, v_prev), v_cur], axis=0)
            bias_sel = jnp.where(first_step, 1, 0) if sb == 0 else 0
            attn_blocks.append(_attention_block(q_all[rows], kb, vb, bias_ref, bias_sel))
            k_prev, v_prev = k_cur, v_cur
        kv_prev_ref[:, :KV_WIDTH] = k_prev
        kv_prev_ref[:, KV_WIDTH:] = v_prev
        attn = jnp.concatenate(attn_blocks, axis=0)

        u = jax.nn.gelu(proj_cols(_O_U, _O_G))
        gg = jax.nn.gelu(proj_cols(_O_G, IN_WIDTH))
        avg = grp_avg_ref[...]
        mu = jnp.dot(gg.astype(jnp.bfloat16), avg, preferred_element_type=jnp.float32)
        gc = gg - mu
        var = jnp.dot((gc * gc).astype(jnp.bfloat16), avg, preferred_element_type=jnp.float32)
        gn = (gc * lax.rsqrt(var + LN_EPS) * lnv_g_ref[...] + lnv_b_ref[...]).astype(jnp.bfloat16)
        causal = (lax.broadcasted_iota(jnp.int32, (ATTN_BLOCK, ATTN_BLOCK), 0)
                  >= lax.broadcasted_iota(jnp.int32, (ATTN_BLOCK, ATTN_BLOCK), 1))
        w_sp = [jnp.where(causal, w_sp_ref[g], 0.0).astype(jnp.bfloat16) for g in range(N_GMLP_GROUPS)]
        mixed_chunks = []
        for c in range(n_sub):
            rows = slice(c * ATTN_BLOCK, (c + 1) * ATTN_BLOCK)
            pieces = [
                jnp.dot(w_sp[g], gn[rows, g * GMLP_GROUP_DIM:(g + 1) * GMLP_GROUP_DIM],
                        preferred_element_type=jnp.float32)
                for g in range(N_GMLP_GROUPS)]
            mixed_chunks.append(jnp.concatenate(pieces, axis=-1) + b_sp_ref[...])
        sgu = u * jnp.concatenate(mixed_chunks, axis=0)

        mix = (jnp.dot(attn.astype(jnp.bfloat16), w_out_ref[:ATTN_WIDTH, :], preferred_element_type=jnp.float32)
               + jnp.dot(sgu.astype(jnp.bfloat16), w_out_ref[ATTN_WIDTH:, :], preferred_element_type=jnp.float32)
               + b_out_ref[...])
        hres_ref[...] = DEEPNORM_ALPHA * x + mix

    @pl.when(i == 0)
    def _():
        mix_block(iter(()))

    @pl.when(jnp.logical_and(i > 0, i < n_steps))
    def _():
        mix_block(route_stages())

    @pl.when(i == n_steps)
    def _():
        for _ in route_stages():
            pass


def _score_bias(sinks):
    t_idx = jnp.arange(Q_REP * ATTN_BLOCK)[:, None] % ATTN_BLOCK
    s_idx = jnp.arange(2 * ATTN_BLOCK)[None, :]
    diff = t_idx + ATTN_BLOCK - s_idx
    band = (diff >= 0) & (diff < ATTN_BLOCK)
    masks = jnp.stack([band, band & (s_idx >= ATTN_BLOCK)])
    bias = jnp.where(masks, 0.0, NEG_INF).astype(jnp.float32)
    sink_rows = jnp.repeat(sinks.reshape(N_KV_HEADS, Q_REP) * LOG2_E, ATTN_BLOCK, axis=1)
    return jnp.where(s_idx == 0, sink_rows[:, None, :, None], bias[None])


def _mixer(x2, sinks, w_in, b_in, lnv_g, lnv_b, w_sp, b_sp_full, w_out, b_out, ln1_g, ln1_b,
           w_r, w_r_hi, b_r, seq_len):
    t = x2.shape[0]
    tm = MIXER_ROWS
    n_steps = t // tm
    grp = jnp.arange(GMLP_WIDTH) // GMLP_GROUP_DIM
    grp_avg = jnp.where(grp[:, None] == grp[None, :], 1.0 / GMLP_GROUP_DIM, 0.0).astype(jnp.bfloat16)
    tri = (jnp.arange(tm)[:, None] < jnp.arange(tm)[None, :]).astype(jnp.bfloat16)

    def full(shape):
        return pl.BlockSpec(shape, lambda i: (0,) * len(shape))

    return pl.pallas_call(
        functools.partial(_mixer_kernel, steps_per_seq=seq_len // tm, n_steps=n_steps),
        grid=(n_steps + 1,),
        in_specs=[
            pl.BlockSpec((tm, D_MODEL), lambda i: (jnp.minimum(i, n_steps - 1), 0)),
            full((D_MODEL, IN_WIDTH)), full((1, IN_WIDTH)),
            full((N_KV_HEADS, 2, Q_REP * ATTN_BLOCK, 2 * ATTN_BLOCK)),
            full((1, GMLP_WIDTH)), full((1, GMLP_WIDTH)), full((GMLP_WIDTH, GMLP_WIDTH)),
            full((N_GMLP_GROUPS, ATTN_BLOCK, ATTN_BLOCK)), full((ATTN_BLOCK, GMLP_WIDTH)),
            full((D_MODEL, D_MODEL)), full((1, D_MODEL)), full((1, D_MODEL)), full((1, D_MODEL)),
            full((D_MODEL, ROUTER_LANES)), full((D_MODEL, ROUTER_LANES)), full((N_EXPERTS, 1)), full((tm, tm)),
        ],
        out_specs=[
            pl.BlockSpec((tm, D_MODEL), lambda i: (jnp.maximum(i - 1, 0), 0)),
            pl.BlockSpec((tm, D_MODEL // 2), lambda i: (jnp.maximum(i - 1, 0), 0)),
            pl.BlockSpec((META_ROWS, tm), lambda i: (0, jnp.maximum(i - 1, 0))),
            pl.BlockSpec((N_EXPERTS, 1), lambda i: (0, 0)),
        ],
        out_shape=[
            jax.ShapeDtypeStruct((t, D_MODEL), jnp.float32),
            jax.ShapeDtypeStruct((t, D_MODEL // 2), jnp.int32),
            jax.ShapeDtypeStruct((META_ROWS, t), jnp.float32),
            jax.ShapeDtypeStruct((N_EXPERTS, 1), jnp.float32),
        ],
        scratch_shapes=[pltpu.VMEM((ATTN_BLOCK, 2 * KV_WIDTH), jnp.bfloat16),
                        pltpu.VMEM((tm, D_MODEL), jnp.float32)],
        compiler_params=pltpu.CompilerParams(
            dimension_semantics=("arbitrary",), vmem_limit_bytes=VMEM_LIMIT_BYTES,
            allow_input_fusion=[k in (1, 9) for k in range(17)]),
        name="mixer",
    )(x2, w_in, b_in, _score_bias(sinks), lnv_g, lnv_b, grp_avg, w_sp, b_sp_full, w_out, b_out, ln1_g, ln1_b,
      w_r, w_r_hi, b_r, tri)


def _gather_rows(idx, src):
    n = idx.shape[0]
    width = src.shape[1]
    win = GATHER_WINDOW
    sc = plsc.get_sparse_core_info()
    n_workers = sc.num_cores * sc.num_subcores
    per_worker = n // n_workers
    n_pairs = per_worker // (2 * win)
    assert n_pairs * 2 * win * n_workers == n
    mesh = plsc.VectorSubcoreMesh(core_axis_name="core", subcore_axis_name="subcore")

    @functools.partial(
        pl.kernel, out_type=jax.ShapeDtypeStruct((n, width), src.dtype), mesh=mesh,
        scratch_types=[pltpu.VMEM((per_worker,), jnp.int32), pltpu.VMEM((2, win, width), src.dtype),
                       pltpu.SemaphoreType.DMA((2,)), pltpu.SemaphoreType.DMA((2,))],
        name="gather_rows")
    def gather(src_hbm, idx_hbm, out_hbm, idx_v, rows_v, fetch_sem, store_sem):
        worker = lax.axis_index("subcore") * sc.num_cores + lax.axis_index("core")
        base = worker * per_worker
        pltpu.sync_copy(idx_hbm.at[pl.ds(base, per_worker)], idx_v)

        def fetch(chunk, buf):
            return pltpu.make_async_copy(src_hbm.at[idx_v.at[pl.ds(chunk * win, win)]], rows_v.at[buf],
                                         fetch_sem.at[buf])

        def store(chunk, buf):
            return pltpu.make_async_copy(rows_v.at[buf], out_hbm.at[pl.ds(base + chunk * win, win)],
                                         store_sem.at[buf])

        @pl.loop(0, n_pairs)
        def _(p):
            for buf in range(2):
                @pl.when(p > 0)
                def _():
                    store(2 * p - 2 + buf, buf).wait()
                fetch(2 * p + buf, buf).start()
            for buf in range(2):
                fetch(2 * p + buf, buf).wait()
                store(2 * p + buf, buf).start()

        for buf in range(2):
            store(2 * n_pairs - 2 + buf, buf).wait()

    return gather(src, idx)


def _dispatch_rows(dest_t, src, n_rows):
    t, width = src.shape
    win = GATHER_WINDOW
    sc = plsc.get_sparse_core_info()
    n_workers = sc.num_cores * sc.num_subcores
    per_worker = t // n_workers
    n_chunks = per_worker // win
    n_pairs = n_chunks // 2
    assert n_pairs * 2 * win * n_workers == t
    idx = dest_t.reshape(TOP_K, n_workers, n_chunks, win).transpose(1, 0, 2, 3)
    idx = idx.reshape(n_workers, TOP_K * n_chunks, win)
    mesh = plsc.VectorSubcoreMesh(core_axis_name="core", subcore_axis_name="subcore")

    @functools.partial(
        pl.kernel, out_type=jax.ShapeDtypeStruct((n_rows, width), src.dtype), mesh=mesh,
        scratch_types=[pltpu.VMEM((TOP_K * n_chunks, win), jnp.int32), pltpu.VMEM((2, win, width), src.dtype),
                       pltpu.SemaphoreType.DMA((2,)), pltpu.SemaphoreType.DMA((2,))],
        name="dispatch_rows")
    def dispatch(src_hbm, idx_hbm, out_hbm, idx_v, rows_v, fetch_sem, store_sem):
        worker = lax.axis_index("subcore") * sc.num_cores + lax.axis_index("core")
        base = worker * per_worker
        pltpu.sync_copy(idx_hbm.at[worker], idx_v)

        def fetch(chunk, buf):
            return pltpu.make_async_copy(src_hbm.at[pl.ds(base + chunk * win, win)], rows_v.at[buf],
                                         fetch_sem.at[buf])

        def store(chunk, k, buf):
            return pltpu.make_async_copy(rows_v.at[buf], out_hbm.at[idx_v.at[k * n_chunks + chunk]],
                                         store_sem.at[buf])

        @pl.loop(0, n_pairs)
        def _(p):
            for buf in range(2):
                @pl.when(p > 0)
                def _():
                    for k in range(TOP_K):
                        store(2 * p - 2 + buf, k, buf).wait()
                fetch(2 * p + buf, buf).start()
            for buf in range(2):
                fetch(2 * p + buf, buf).wait()
                for k in range(TOP_K):
                    store(2 * p + buf, k, buf).start()

        for buf in range(2):
            for k in range(TOP_K):
                store(2 * n_pairs - 2 + buf, k, buf).wait()

    return dispatch(src, idx)


def _expert_kernel(block_e_ref, n_valid_ref, first_ref, next_e_ref, row_block_ref,
                   x_ref, wg_hbm, wu_hbm, wd_hbm, bias_ref,
                   y_ref, wg_f32, wu_f32, wd_f32, wg_bf, wu_bf, wd_bf, sem):
    i = pl.program_id(0)
    n_valid = n_valid_ref[i]
    staged = ((wg_hbm, wg_f32, wg_bf), (wu_hbm, wu_f32, wu_bf), (wd_hbm, wd_f32, wd_bf))

    def weight_copy(m, expert):
        return pltpu.make_async_copy(staged[m][0].at[expert], staged[m][1], sem.at[m])

    @pl.when(i == 0)
    def _():
        for m in range(3):
            weight_copy(m, block_e_ref[0]).start()

    @pl.when(first_ref[i] == 1)
    def _():
        for m in range(3):
            weight_copy(m, block_e_ref[i]).wait()
            staged[m][2][...] = staged[m][1][...].astype(jnp.bfloat16)

            @pl.when(next_e_ref[i] >= 0)
            def _():
                weight_copy(m, next_e_ref[i]).start()

    def expert_mlp(rows):
        row = lax.broadcasted_iota(jnp.int32, (rows, 1), 0)
        x_hi, x_lo = _unpack_bf16_pairs(jnp.where(row < n_valid, x_ref[:rows, :], 0))
        xb = jnp.concatenate([x_hi, x_lo], axis=1).astype(jnp.bfloat16)
        gt = jnp.minimum(jnp.dot(xb, wg_bf[...], preferred_element_type=jnp.float32) + bias_ref[0, 0:1], SWIGLU_LIMIT)
        up = jnp.clip(jnp.dot(xb, wu_bf[...], preferred_element_type=jnp.float32) + bias_ref[0, 1:2],
                      -SWIGLU_LIMIT, SWIGLU_LIMIT)
        hid = gt * jax.nn.sigmoid(SWIGLU_ALPHA * gt) * (up + 1.0)
        y = jnp.dot(hid.astype(jnp.bfloat16), wd_bf[...], preferred_element_type=jnp.float32) + bias_ref[0, 2:3]
        y_ref[:rows, :] = _pack_bf16_pairs(y.astype(jnp.bfloat16))
        if rows < y_ref.shape[0]:
            y_ref[rows:, :] = jnp.zeros((y_ref.shape[0] - rows, y_ref.shape[1]), y_ref.dtype)

    row_options = (0,) + EXPERT_PARTIAL_ROWS + (x_ref.shape[0],)
    for lo, hi in zip(row_options[:-1], row_options[1:]):
        pl.when(jnp.logical_and(n_valid > lo, n_valid <= hi))(functools.partial(expert_mlp, hi))


def _experts(block_e, n_valid, x_rows, w_gate, b_gate, w_up, b_up, w_down, b_down):
    n_rows = x_rows.shape[0]
    bm = EXPERT_ROWS
    n_blocks = n_rows // bm
    d_ff = w_gate.shape[2]

    blocks = jnp.arange(n_blocks, dtype=jnp.int32)
    last_used = jnp.sum((n_valid > 0).astype(jnp.int32)) - 1
    row_block = jnp.minimum(blocks, last_used)
    block_e = jnp.sum(jnp.where(blocks[None, :] == row_block[:, None], block_e[None, :], 0), axis=1)

    is_first = jnp.concatenate([jnp.ones((1,), jnp.int32), (block_e[1:] != block_e[:-1]).astype(jnp.int32)])
    next_e = jnp.min(jnp.where(block_e[None, :] > block_e[:, None], block_e[None, :], N_EXPERTS), axis=1)
    next_e = jnp.where(next_e == N_EXPERTS, -1, next_e)

    def by_expert(shape):
        return pl.BlockSpec(shape, lambda i, be, *_: (be[i],) + (0,) * (len(shape) - 1))

    hbm = pl.BlockSpec(memory_space=pl.ANY)
    assert d_ff == D_MODEL
    biases = jnp.concatenate(
        [b_gate[:, None], b_up[:, None], b_down[:, None], jnp.zeros((N_EXPERTS, SUBLANES - 3, D_MODEL), jnp.float32)],
        axis=1)
    return pl.pallas_call(
        _expert_kernel,
        grid_spec=pltpu.PrefetchScalarGridSpec(
            num_scalar_prefetch=5,
            grid=(n_blocks,),
            in_specs=[
                pl.BlockSpec((bm, D_MODEL // 2), lambda i, be, nv, fi, ne, rb: (rb[i], 0)),
                hbm, hbm, hbm, by_expert((1, SUBLANES, D_MODEL)),
            ],
            out_specs=pl.BlockSpec((bm, D_MODEL // 2), lambda i, be, nv, fi, ne, rb: (rb[i], 0)),
            scratch_shapes=[
                pltpu.VMEM((D_MODEL, d_ff), jnp.float32),
                pltpu.VMEM((D_MODEL, d_ff), jnp.float32),
                pltpu.VMEM((d_ff, D_MODEL), jnp.float32),
                pltpu.VMEM((D_MODEL, d_ff), jnp.bfloat16),
                pltpu.VMEM((D_MODEL, d_ff), jnp.bfloat16),
                pltpu.VMEM((d_ff, D_MODEL), jnp.bfloat16),
                pltpu.SemaphoreType.DMA((3,)),
            ],
        ),
        out_shape=jax.ShapeDtypeStruct((n_rows, D_MODEL // 2), jnp.int32),
        compiler_params=pltpu.CompilerParams(
            dimension_semantics=("arbitrary",), vmem_limit_bytes=VMEM_LIMIT_BYTES),
        name="experts",
    )(block_e, n_valid, is_first, next_e, row_block, x_rows, w_gate, w_up, w_down, biases)


def _combine_kernel(h_ref, y_ref, meta_ref, g_ref, b_ref, o_ref):
    tm = h_ref.shape[0]
    meta_cols = jnp.concatenate([meta_ref[...], jnp.zeros((LANES - META_ROWS, tm), jnp.float32)], axis=0).T
    ffn_hi, ffn_lo = 0.0, 0.0
    for k in range(TOP_K):
        gate = meta_cols[:, 2 * TOP_K + k:2 * TOP_K + k + 1]
        y_hi, y_lo = _unpack_bf16_pairs(y_ref[k])
        ffn_hi = ffn_hi + gate * y_hi
        ffn_lo = ffn_lo + gate * y_lo
    ffn = jnp.concatenate([ffn_hi, ffn_lo], axis=1)
    o_ref[...] = _layer_norm(DEEPNORM_ALPHA * h_ref[...] + ffn, g_ref[...], b_ref[...])


def _combine(h, y_tok, meta, ln2_g, ln2_b):
    t = h.shape[0]
    tm = COMBINE_ROWS
    return pl.pallas_call(
        _combine_kernel,
        grid=(t // tm,),
        in_specs=[
            pl.BlockSpec((tm, D_MODEL), lambda i: (i, 0)),
            pl.BlockSpec((TOP_K, tm, D_MODEL // 2), lambda i: (0, i, 0)),
            pl.BlockSpec((META_ROWS, tm), lambda i: (0, i)),
            pl.BlockSpec((1, D_MODEL), lambda i: (0, 0)),
            pl.BlockSpec((1, D_MODEL), lambda i: (0, 0)),
        ],
        out_specs=pl.BlockSpec((tm, D_MODEL), lambda i: (i, 0)),
        out_shape=jax.ShapeDtypeStruct((t, D_MODEL), jnp.float32),
        compiler_params=pltpu.CompilerParams(
            dimension_semantics=("arbitrary",), vmem_limit_bytes=VMEM_LIMIT_BYTES),
        name="combine",
    )(h, y_tok, meta, ln2_g, ln2_b)


def _layer(x2, seq_len, w_in, b_in, sinks, ln_v_g, ln_v_b, w_spatial, b_spatial, w_out, b_out,
           ln1_g, ln1_b, w_router, b_router, w_gate, b_gate, w_up, b_up, w_down, b_down, ln2_g, ln2_b):
    t = x2.shape[0]
    tk = t * TOP_K
    bm = EXPERT_ROWS
    bf16 = jnp.bfloat16

    w_r_hi = w_router.astype(bf16)
    w_r_lo = (w_router - w_r_hi.astype(jnp.float32)).astype(bf16)
    lane_pad = jnp.zeros((D_MODEL, ROUTER_LANES - 2 * N_EXPERTS), bf16)
    w_r = jnp.concatenate([w_r_hi, w_r_lo, lane_pad], axis=1)
    w_r_hi_only = jnp.concatenate([w_r_hi, jnp.zeros_like(w_r_lo), lane_pad], axis=1)
    b_sp_full = jnp.repeat(b_spatial.T, GMLP_GROUP_DIM, axis=1)

    h, h_packed, meta, counts = _mixer(
        x2, sinks, w_in.astype(bf16), b_in[None], ln_v_g[None], ln_v_b[None], w_spatial, b_sp_full,
        w_out.astype(bf16), b_out[None], ln1_g[None], ln1_b[None], w_r, w_r_hi_only, b_router[:, None], seq_len)

    counts = counts[:, 0].astype(jnp.int32)
    experts = jnp.arange(N_EXPERTS, dtype=jnp.int32)
    padded = (counts + bm - 1) // bm * bm
    padded_end = jnp.sum(jnp.where(experts[None, :] <= experts[:, None], padded[None, :], 0), axis=1)
    padded_start = padded_end - padded
    n_blocks = tk // bm + N_EXPERTS
    n_rows = n_blocks * bm
    top_idx_t = meta[:TOP_K].astype(jnp.int32)
    rank_t = meta[TOP_K:2 * TOP_K].astype(jnp.int32)
    dest_t = rank_t + jnp.sum(
        jnp.where(top_idx_t[None] == experts[:, None, None], padded_start[:, None, None], 0), axis=0)
    block_start = jnp.arange(n_blocks, dtype=jnp.int32) * bm
    block_e = jnp.minimum(
        jnp.sum((padded_end[None, :] <= block_start[:, None]).astype(jnp.int32), axis=1), N_EXPERTS - 1)
    valid_end = jnp.sum(jnp.where(block_e[:, None] == experts[None, :], (padded_start + counts)[None, :], 0), axis=1)
    n_valid = jnp.clip(valid_end - block_start, 0, bm)

    x_rows = _dispatch_rows(dest_t, h_packed, n_rows)
    y_rows = _experts(block_e, n_valid, x_rows, w_gate, b_gate, w_up, b_up, w_down, b_down)
    y_tok = _gather_rows(dest_t.reshape(-1), y_rows).reshape(TOP_K, t, D_MODEL // 2)
    return _combine(h, y_tok, meta, ln2_g[None], ln2_b[None])


def kernel(x, w_in, b_in, sinks, ln_v_g, ln_v_b, w_spatial, b_spatial, w_out, b_out, ln1_g, ln1_b,
           w_router, b_router, w_gate, b_gate, w_up, b_up, w_down, b_down, ln2_g, ln2_b):
    batch, seq_len, d = x.shape
    x2 = x.reshape(batch * seq_len, d)
    for l in range(DEPTH):
        x2 = _layer(x2, seq_len, w_in[l], b_in[l], sinks[l], ln_v_g[l], ln_v_b[l], w_spatial[l],
                    b_spatial[l], w_out[l], b_out[l], ln1_g[l], ln1_b[l], w_router[l], b_router[l],
                    w_gate[l], b_gate[l], w_up[l], b_up[l], w_down[l], b_down[l], ln2_g[l], ln2_b[l])
    return x2.reshape(batch, seq_len, d)
```

```python
import functools

import jax
import jax.numpy as jnp
from jax import lax
from jax.experimental import pallas as pl
from jax.experimental.pallas import tpu as pltpu
from jax.experimental.pallas import tpu_sc as plsc

D_MODEL = 1024
HEAD_DIM = 64
N_Q_HEADS = 8
N_KV_HEADS = 2
Q_REP = N_Q_HEADS // N_KV_HEADS
ATTN_WIDTH = N_Q_HEADS * HEAD_DIM
KV_WIDTH = N_KV_HEADS * HEAD_DIM
ATTN_BLOCK = 128
N_GMLP_GROUPS = 8
GMLP_WIDTH = D_MODEL - ATTN_WIDTH
GMLP_GROUP_DIM = GMLP_WIDTH // N_GMLP_GROUPS
IN_WIDTH = ATTN_WIDTH + 2 * KV_WIDTH + 2 * GMLP_WIDTH
N_EXPERTS = 32
TOP_K = 4
SWIGLU_LIMIT = 7.0
SWIGLU_ALPHA = 1.702
LN_EPS = 1e-5
DEPTH = 1
DEEPNORM_ALPHA = (2.0 * DEPTH) ** 0.25
NEG_INF = -1e30
LOG2_E = 1.4426950408889634

LANES = 128
SUBLANES = 8

MIXER_ROWS = 512
PROJ_CHUNK = 256
EXPERT_ROWS = 1024
EXPERT_PARTIAL_ROWS = (128, 256, 384, 512, 768)
GATHER_WINDOW = 64
META_ROWS = 16
ROUTER_LANES = LANES
COMBINE_ROWS = 1024
VMEM_LIMIT_BYTES = 56 * 1024 * 1024

_O_K = ATTN_WIDTH
_O_V = _O_K + KV_WIDTH
_O_U = _O_V + KV_WIDTH
_O_G = _O_U + GMLP_WIDTH


def _pack_bf16_pairs(v):
    n = v.shape[1] // 2
    hi = lax.bitcast_convert_type(v[:, :n].astype(jnp.float32), jnp.int32)
    lo = lax.bitcast_convert_type(v[:, n:].astype(jnp.float32), jnp.int32)
    return hi | lax.shift_right_logical(lo, 16)


def _unpack_bf16_pairs(p):
    hi = lax.bitcast_convert_type(p & jnp.int32(-65536), jnp.float32)
    lo = lax.bitcast_convert_type(lax.shift_left(p, 16), jnp.float32)
    return hi, lo


def _layer_norm(v, g, b):
    mu = jnp.mean(v, axis=-1, keepdims=True)
    vc = v - mu
    var = jnp.mean(vc * vc, axis=-1, keepdims=True)
    return vc * lax.rsqrt(var + LN_EPS) * g + b


def _attention_block(q, kb, vb, bias_ref, bias_sel):
    outs = []
    for g in range(N_KV_HEADS):
        kg = kb[:, g * HEAD_DIM:(g + 1) * HEAD_DIM]
        vg = vb[:, g * HEAD_DIM:(g + 1) * HEAD_DIM]
        qg = jnp.concatenate(
            [q[:, (g * Q_REP + r) * HEAD_DIM:(g * Q_REP + r + 1) * HEAD_DIM] for r in range(Q_REP)],
            axis=0).astype(jnp.bfloat16)
        s = (lax.dot_general(qg, kg, (((1,), (1,)), ((), ())), preferred_element_type=jnp.float32)
             + bias_ref[g, bias_sel])
        p = jnp.exp2(s - jnp.max(s, axis=-1, keepdims=True))
        denom = jnp.sum(p, axis=-1, keepdims=True)
        o = jnp.dot(p.astype(jnp.bfloat16), vg, preferred_element_type=jnp.float32) / denom
        outs.extend(o[r * ATTN_BLOCK:(r + 1) * ATTN_BLOCK] for r in range(Q_REP))
    return jnp.concatenate(outs, axis=-1)


def _mixer_kernel(x_ref, w_in_ref, b_in_ref, bias_ref, lnv_g_ref, lnv_b_ref, grp_avg_ref,
                  w_sp_ref, b_sp_ref, w_out_ref, b_out_ref, ln1_g_ref, ln1_b_ref,
                  w_r_ref, w_r_hi_ref, b_r_ref, tri_ref,
                  h_ref, hp_ref, meta_ref, count_ref,
                  kv_prev_ref, hres_ref, *, steps_per_seq, n_steps):
    i = pl.program_id(0)
    first_step = (i % steps_per_seq) == 0
    tm = x_ref.shape[0]
    n_sub = tm // ATTN_BLOCK

    @pl.when(i == 0)
    def _():
        count_ref[...] = jnp.zeros_like(count_ref)

    @pl.when(first_step)
    def _():
        kv_prev_ref[...] = jnp.zeros_like(kv_prev_ref)

    def route_stages():
        h = _layer_norm(hres_ref[...], ln1_g_ref[...], ln1_b_ref[...])
        h_ref[...] = h
        yield
        h_hi = h.astype(jnp.bfloat16)
        hp_ref[...] = _pack_bf16_pairs(h_hi)
        h_lo = (h - h_hi.astype(jnp.float32)).astype(jnp.bfloat16)
        yield
        part = (jnp.dot(h_hi, w_r_ref[...], preferred_element_type=jnp.float32)
                + jnp.dot(h_lo, w_r_hi_ref[...], preferred_element_type=jnp.float32)).T
        logits = part[:N_EXPERTS] + part[N_EXPERTS:2 * N_EXPERTS] + b_r_ref[...]
        yield
        n_grp = N_EXPERTS // SUBLANES
        grp = [logits[SUBLANES * g:SUBLANES * (g + 1)] for g in range(n_grp)]
        sub = lax.broadcasted_iota(jnp.int32, (SUBLANES, tm), 0)
        beaten = [jnp.zeros((SUBLANES, tm), jnp.float32) for _ in range(n_grp)]
        for e2 in range(N_EXPERTS):
            g2, r2 = divmod(e2, SUBLANES)
            row = logits[e2:e2 + 1]
            for g in range(n_grp):
                if g > g2:
                    wins = jnp.where(row >= grp[g], 1.0, 0.0)
                elif g < g2:
                    wins = jnp.where(row > grp[g], 1.0, 0.0)
                else:
                    wins = jnp.where(sub > r2, jnp.where(row >= grp[g], 1.0, 0.0),
                                     jnp.where(row > grp[g], 1.0, 0.0))
                beaten[g] = beaten[g] + wins
            if e2 == N_EXPERTS // 2 - 1:
                yield
        place = jnp.concatenate(beaten, axis=0)
        expert_id = lax.broadcasted_iota(jnp.int32, (N_EXPERTS, tm), 0).astype(jnp.float32)
        onehot = jnp.where(place < TOP_K, 1.0, 0.0)
        yield
        before = (jnp.dot(onehot.astype(jnp.bfloat16), tri_ref[...], preferred_element_type=jnp.float32)
                  + count_ref[...])

        def pick(k, table):
            return jnp.sum(jnp.where(place == k, table, 0.0), axis=0, keepdims=True)

        vals = [pick(k, logits) for k in range(TOP_K)]
        exps = [jnp.exp(v - vals[0]) for v in vals]
        denom = exps[0] + exps[1] + exps[2] + exps[3]
        yield
        meta_ref[...] = jnp.zeros_like(meta_ref)
        for k in range(TOP_K):
            meta_ref[k:k + 1, :] = pick(k, expert_id)
            meta_ref[TOP_K + k:TOP_K + k + 1, :] = pick(k, before)
            meta_ref[2 * TOP_K + k:2 * TOP_K + k + 1, :] = exps[k] / denom
        count_ref[...] += jnp.sum(onehot, axis=1, keepdims=True)

    def mix_block(stages):
        x = x_ref[...]
        x_bf = x.astype(jnp.bfloat16)
        proj = []
        for c in range(IN_WIDTH // PROJ_CHUNK):
            cols = pl.ds(c * PROJ_CHUNK, PROJ_CHUNK)
            proj.append(jnp.dot(x_bf, w_in_ref[:, cols], preferred_element_type=jnp.float32) + b_in_ref[:, cols])
            next(stages, None)
        for _ in stages:
            pass

        def proj_cols(lo, hi):
            return jnp.concatenate(proj[lo // PROJ_CHUNK:hi // PROJ_CHUNK], axis=-1)

        q_all = proj_cols(0, _O_K) * (LOG2_E * HEAD_DIM ** -0.5)
        kv_all = proj_cols(_O_K, _O_U).astype(jnp.bfloat16)
        k_all, v_all = kv_all[:, :KV_WIDTH], kv_all[:, KV_WIDTH:]
        k_prev = kv_prev_ref[:, :KV_WIDTH]
        v_prev = kv_prev_ref[:, KV_WIDTH:]
        is_row0 = lax.broadcasted_iota(jnp.int32, (ATTN_BLOCK, KV_WIDTH), 0) == 0
        attn_blocks = []
        for sb in range(n_sub):
            rows = slice(sb * ATTN_BLOCK, (sb + 1) * ATTN_BLOCK)
            k_cur, v_cur = k_all[rows], v_all[rows]
            kb = jnp.concatenate([jnp.where(is_row0, 0, k_prev), k_cur], axis=0)
            vb = jnp.concatenate([jnp.where(is_row0, 0, v_prev), v_cur], axis=0)
            bias_sel = jnp.where(first_step, 1, 0) if sb == 0 else 0
            attn_blocks.append(_attention_block(q_all[rows], kb, vb, bias_ref, bias_sel))
            k_prev, v_prev = k_cur, v_cur
        kv_prev_ref[:, :KV_WIDTH] = k_prev
        kv_prev_ref[:, KV_WIDTH:] = v_prev
        attn = jnp.concatenate(attn_blocks, axis=0)

        u = jax.nn.gelu(proj_cols(_O_U, _O_G))
        gg = jax.nn.gelu(proj_cols(_O_G, IN_WIDTH))
        avg = grp_avg_ref[...]
        mu = jnp.dot(gg.astype(jnp.bfloat16), avg, preferred_element_type=jnp.float32)
        gc = gg - mu
        var = jnp.dot((gc * gc).astype(jnp.bfloat16), avg, preferred_element_type=jnp.float32)
        gn = (gc * lax.rsqrt(var + LN_EPS) * lnv_g_ref[...] + lnv_b_ref[...]).astype(jnp.bfloat16)
        causal = (lax.broadcasted_iota(jnp.int32, (ATTN_BLOCK, ATTN_BLOCK), 0)
                  >= lax.broadcasted_iota(jnp.int32, (ATTN_BLOCK, ATTN_BLOCK), 1))
        w_sp = [jnp.where(causal, w_sp_ref[g], 0.0).astype(jnp.bfloat16) for g in range(N_GMLP_GROUPS)]
        mixed_chunks = []
        for c in range(n_sub):
            rows = slice(c * ATTN_BLOCK, (c + 1) * ATTN_BLOCK)
            pieces = [
                jnp.dot(w_sp[g], gn[rows, g * GMLP_GROUP_DIM:(g + 1) * GMLP_GROUP_DIM],
                        preferred_element_type=jnp.float32)
                for g in range(N_GMLP_GROUPS)]
            mixed_chunks.append(jnp.concatenate(pieces, axis=-1) + b_sp_ref[...])
        sgu = u * jnp.concatenate(mixed_chunks, axis=0)

        mix = (jnp.dot(attn.astype(jnp.bfloat16), w_out_ref[:ATTN_WIDTH, :], preferred_element_type=jnp.float32)
               + jnp.dot(sgu.astype(jnp.bfloat16), w_out_ref[ATTN_WIDTH:, :], preferred_element_type=jnp.float32)
               + b_out_ref[...])
        hres_ref[...] = DEEPNORM_ALPHA * x + mix

    @pl.when(i == 0)
    def _():
        mix_block(iter(()))

    @pl.when(jnp.logical_and(i > 0, i < n_steps))
    def _():
        mix_block(route_stages())

    @pl.when(i == n_steps)
    def _():
        for _ in route_stages():
            pass


def _score_bias(sinks):
    t_idx = jnp.arange(Q_REP * ATTN_BLOCK)[:, None] % ATTN_BLOCK
    s_idx = jnp.arange(2 * ATTN_BLOCK)[None, :]
    diff = t_idx + ATTN_BLOCK - s_idx
    band = (diff >= 0) & (diff < ATTN_BLOCK)
    masks = jnp.stack([band, band & (s_idx >= ATTN_BLOCK)])
    bias = jnp.where(masks, 0.0, NEG_INF).astype(jnp.float32)
    sink_rows = jnp.repeat(sinks.reshape(N_KV_HEADS, Q_REP) * LOG2_E, ATTN_BLOCK, axis=1)
    return jnp.where(s_idx == 0, sink_rows[:, None, :, None], bias[None])


def _mixer(x2, sinks, w_in, b_in, lnv_g, lnv_b, w_sp, b_sp_full, w_out, b_out, ln1_g, ln1_b,
           w_r, w_r_hi, b_r, seq_len):
    t = x2.shape[0]
    tm = MIXER_ROWS
    n_steps = t // tm
    grp = jnp.arange(GMLP_WIDTH) // GMLP_GROUP_DIM
    grp_avg = jnp.where(grp[:, None] == grp[None, :], 1.0 / GMLP_GROUP_DIM, 0.0).astype(jnp.bfloat16)
    tri = (jnp.arange(tm)[:, None] < jnp.arange(tm)[None, :]).astype(jnp.bfloat16)

    def full(shape):
        return pl.BlockSpec(shape, lambda i: (0,) * len(shape))

    return pl.pallas_call(
        functools.partial(_mixer_kernel, steps_per_seq=seq_len // tm, n_steps=n_steps),
        grid=(n_steps + 1,),
        in_specs=[
            pl.BlockSpec((tm, D_MODEL), lambda i: (jnp.minimum(i, n_steps - 1), 0)),
            full((D_MODEL, IN_WIDTH)), full((1, IN_WIDTH)),
            full((N_KV_HEADS, 2, Q_REP * ATTN_BLOCK, 2 * ATTN_BLOCK)),
            full((1, GMLP_WIDTH)), full((1, GMLP_WIDTH)), full((GMLP_WIDTH, GMLP_WIDTH)),
            full((N_GMLP_GROUPS, ATTN_BLOCK, ATTN_BLOCK)), full((ATTN_BLOCK, GMLP_WIDTH)),
            full((D_MODEL, D_MODEL)), full((1, D_MODEL)), full((1, D_MODEL)), full((1, D_MODEL)),
            full((D_MODEL, ROUTER_LANES)), full((D_MODEL, ROUTER_LANES)), full((N_EXPERTS, 1)), full((tm, tm)),
        ],
        out_specs=[
            pl.BlockSpec((tm, D_MODEL), lambda i: (jnp.maximum(i - 1, 0), 0)),
            pl.BlockSpec((tm, D_MODEL // 2), lambda i: (jnp.maximum(i - 1, 0), 0)),
            pl.BlockSpec((META_ROWS, tm), lambda i: (0, jnp.maximum(i - 1, 0))),
            pl.BlockSpec((N_EXPERTS, 1), lambda i: (0, 0)),
        ],
        out_shape=[
            jax.ShapeDtypeStruct((t, D_MODEL), jnp.float32),
            jax.ShapeDtypeStruct((t, D_MODEL // 2), jnp.int32),
            jax.ShapeDtypeStruct((META_ROWS, t), jnp.float32),
            jax.ShapeDtypeStruct((N_EXPERTS, 1), jnp.float32),
        ],
        scratch_shapes=[pltpu.VMEM((ATTN_BLOCK, 2 * KV_WIDTH), jnp.bfloat16),
                        pltpu.VMEM((tm, D_MODEL), jnp.float32)],
        compiler_params=pltpu.CompilerParams(
            dimension_semantics=("arbitrary",), vmem_limit_bytes=VMEM_LIMIT_BYTES,
            allow_input_fusion=[k in (1, 9) for k in range(17)]),
        name="mixer",
    )(x2, w_in, b_in, _score_bias(sinks), lnv_g, lnv_b, grp_avg, w_sp, b_sp_full, w_out, b_out, ln1_g, ln1_b,
      w_r, w_r_hi, b_r, tri)


def _gather_rows(idx, src):
    n = idx.shape[0]
    width = src.shape[1]
    win = GATHER_WINDOW
    sc = plsc.get_sparse_core_info()
    n_workers = sc.num_cores * sc.num_subcores
    per_worker = n // n_workers
    n_pairs = per_worker // (2 * win)
    assert n_pairs * 2 * win * n_workers == n
    mesh = plsc.VectorSubcoreMesh(core_axis_name="core", subcore_axis_name="subcore")

    @functools.partial(
        pl.kernel, out_type=jax.ShapeDtypeStruct((n, width), src.dtype), mesh=mesh,
        scratch_types=[pltpu.VMEM((per_worker,), jnp.int32), pltpu.VMEM((2, win, width), src.dtype),
                       pltpu.SemaphoreType.DMA((2,)), pltpu.SemaphoreType.DMA((2,))],
        name="gather_rows")
    def gather(src_hbm, idx_hbm, out_hbm, idx_v, rows_v, fetch_sem, store_sem):
        worker = lax.axis_index("subcore") * sc.num_cores + lax.axis_index("core")
        base = worker * per_worker
        pltpu.sync_copy(idx_hbm.at[pl.ds(base, per_worker)], idx_v)

        def fetch(chunk, buf):
            return pltpu.make_async_copy(src_hbm.at[idx_v.at[pl.ds(chunk * win, win)]], rows_v.at[buf],
                                         fetch_sem.at[buf])

        def store(chunk, buf):
            return pltpu.make_async_copy(rows_v.at[buf], out_hbm.at[pl.ds(base + chunk * win, win)],
                                         store_sem.at[buf])

        @pl.loop(0, n_pairs)
        def _(p):
            for buf in range(2):
                @pl.when(p > 0)
                def _():
                    store(2 * p - 2 + buf, buf).wait()
                fetch(2 * p + buf, buf).start()
            for buf in range(2):
                fetch(2 * p + buf, buf).wait()
                store(2 * p + buf, buf).start()

        for buf in range(2):
            store(2 * n_pairs - 2 + buf, buf).wait()

    return gather(src, idx)


def _dispatch_rows(dest_t, src, n_rows):
    t, width = src.shape
    win = GATHER_WINDOW
    sc = plsc.get_sparse_core_info()
    n_workers = sc.num_cores * sc.num_subcores
    per_worker = t // n_workers
    n_chunks = per_worker // win
    n_pairs = n_chunks // 2
    assert n_pairs * 2 * win * n_workers == t
    idx = dest_t.reshape(TOP_K, n_workers, n_chunks, win).transpose(1, 0, 2, 3)
    idx = idx.reshape(n_workers, TOP_K * n_chunks, win)
    mesh = plsc.VectorSubcoreMesh(core_axis_name="core", subcore_axis_name="subcore")

    @functools.partial(
        pl.kernel, out_type=jax.ShapeDtypeStruct((n_rows, width), src.dtype), mesh=mesh,
        scratch_types=[pltpu.VMEM((TOP_K * n_chunks, win), jnp.int32), pltpu.VMEM((2, win, width), src.dtype),
                       pltpu.SemaphoreType.DMA((2,)), pltpu.SemaphoreType.DMA((2,))],
        name="dispatch_rows")
    def dispatch(src_hbm, idx_hbm, out_hbm, idx_v, rows_v, fetch_sem, store_sem):
        worker = lax.axis_index("subcore") * sc.num_cores + lax.axis_index("core")
        base = worker * per_worker
        pltpu.sync_copy(idx_hbm.at[worker], idx_v)

        def fetch(chunk, buf):
            return pltpu.make_async_copy(src_hbm.at[pl.ds(base + chunk * win, win)], rows_v.at[buf],
                                         fetch_sem.at[buf])

        def store(chunk, k, buf):
            return pltpu.make_async_copy(rows_v.at[buf], out_hbm.at[idx_v.at[k * n_chunks + chunk]],
                                         store_sem.at[buf])

        @pl.loop(0, n_pairs)
        def _(p):
            for buf in range(2):
                @pl.when(p > 0)
                def _():
                    for k in range(TOP_K):
                        store(2 * p - 2 + buf, k, buf).wait()
                fetch(2 * p + buf, buf).start()
            for buf in range(2):
                fetch(2 * p + buf, buf).wait()
                for k in range(TOP_K):
                    store(2 * p + buf, k, buf).start()

        for buf in range(2):
            for k in range(TOP_K):
                store(2 * n_pairs - 2 + buf, k, buf).wait()

    return dispatch(src, idx)


def _expert_kernel(block_e_ref, n_valid_ref, first_ref, next_e_ref, row_block_ref,
                   x_ref, wg_hbm, wu_hbm, wd_hbm, bias_ref,
                   y_ref, wg_f32, wu_f32, wd_f32, wg_bf, wu_bf, wd_bf, sem):
    i = pl.program_id(0)
    n_valid = n_valid_ref[i]
    staged = ((wg_hbm, wg_f32, wg_bf), (wu_hbm, wu_f32, wu_bf), (wd_hbm, wd_f32, wd_bf))

    def weight_copy(m, expert):
        return pltpu.make_async_copy(staged[m][0].at[expert], staged[m][1], sem.at[m])

    @pl.when(i == 0)
    def _():
        for m in range(3):
            weight_copy(m, block_e_ref[0]).start()

    @pl.when(first_ref[i] == 1)
    def _():
        for m in range(3):
            weight_copy(m, block_e_ref[i]).wait()
            staged[m][2][...] = staged[m][1][...].astype(jnp.bfloat16)

            @pl.when(next_e_ref[i] >= 0)
            def _():
                weight_copy(m, next_e_ref[i]).start()

    def expert_mlp(rows):
        row = lax.broadcasted_iota(jnp.int32, (rows, 1), 0)
        x_hi, x_lo = _unpack_bf16_pairs(jnp.where(row < n_valid, x_ref[:rows, :], 0))
        xb = jnp.concatenate([x_hi, x_lo], axis=1).astype(jnp.bfloat16)
        gt = jnp.minimum(jnp.dot(xb, wg_bf[...], preferred_element_type=jnp.float32) + bias_ref[0, 0:1], SWIGLU_LIMIT)
        up = jnp.clip(jnp.dot(xb, wu_bf[...], preferred_element_type=jnp.float32) + bias_ref[0, 1:2],
                      -SWIGLU_LIMIT, SWIGLU_LIMIT)
        hid = gt * jax.nn.sigmoid(SWIGLU_ALPHA * gt) * (up + 1.0)
        y = jnp.dot(hid.astype(jnp.bfloat16), wd_bf[...], preferred_element_type=jnp.float32) + bias_ref[0, 2:3]
        y_ref[:rows, :] = _pack_bf16_pairs(y.astype(jnp.bfloat16))
        if rows < y_ref.shape[0]:
            y_ref[rows:, :] = jnp.zeros((y_ref.shape[0] - rows, y_ref.shape[1]), y_ref.dtype)

    row_options = (0,) + EXPERT_PARTIAL_ROWS + (x_ref.shape[0],)
    for lo, hi in zip(row_options[:-1], row_options[1:]):
        pl.when(jnp.logical_and(n_valid > lo, n_valid <= hi))(functools.partial(expert_mlp, hi))


def _experts(block_e, n_valid, x_rows, w_gate, b_gate, w_up, b_up, w_down, b_down):
    n_rows = x_rows.shape[0]
    bm = EXPERT_ROWS
    n_blocks = n_rows // bm
    d_ff = w_gate.shape[2]

    blocks = jnp.arange(n_blocks, dtype=jnp.int32)
    last_used = jnp.sum((n_valid > 0).astype(jnp.int32)) - 1
    row_block = jnp.minimum(blocks, last_used)
    block_e = jnp.sum(jnp.where(blocks[None, :] == row_block[:, None], block_e[None, :], 0), axis=1)

    is_first = jnp.concatenate([jnp.ones((1,), jnp.int32), (block_e[1:] != block_e[:-1]).astype(jnp.int32)])
    next_e = jnp.min(jnp.where(block_e[None, :] > block_e[:, None], block_e[None, :], N_EXPERTS), axis=1)
    next_e = jnp.where(next_e == N_EXPERTS, -1, next_e)

    def by_expert(shape):
        return pl.BlockSpec(shape, lambda i, be, *_: (be[i],) + (0,) * (len(shape) - 1))

    hbm = pl.BlockSpec(memory_space=pl.ANY)
    assert d_ff == D_MODEL
    biases = jnp.zeros((N_EXPERTS, SUBLANES, D_MODEL), jnp.float32)
    biases = biases.at[:, 0].set(b_gate).at[:, 1].set(b_up).at[:, 2].set(b_down)
    return pl.pallas_call(
        _expert_kernel,
        grid_spec=pltpu.PrefetchScalarGridSpec(
            num_scalar_prefetch=5,
            grid=(n_blocks,),
            in_specs=[
                pl.BlockSpec((bm, D_MODEL // 2), lambda i, be, nv, fi, ne, rb: (rb[i], 0)),
                hbm, hbm, hbm, by_expert((1, SUBLANES, D_MODEL)),
            ],
            out_specs=pl.BlockSpec((bm, D_MODEL // 2), lambda i, be, nv, fi, ne, rb: (rb[i], 0)),
            scratch_shapes=[
                pltpu.VMEM((D_MODEL, d_ff), jnp.float32),
                pltpu.VMEM((D_MODEL, d_ff), jnp.float32),
                pltpu.VMEM((d_ff, D_MODEL), jnp.float32),
                pltpu.VMEM((D_MODEL, d_ff), jnp.bfloat16),
                pltpu.VMEM((D_MODEL, d_ff), jnp.bfloat16),
                pltpu.VMEM((d_ff, D_MODEL), jnp.bfloat16),
                pltpu.SemaphoreType.DMA((3,)),
            ],
        ),
        out_shape=jax.ShapeDtypeStruct((n_rows, D_MODEL // 2), jnp.int32),
        input_output_aliases={5: 0},
        compiler_params=pltpu.CompilerParams(
            dimension_semantics=("arbitrary",), vmem_limit_bytes=VMEM_LIMIT_BYTES),
        name="experts",
    )(block_e, n_valid, is_first, next_e, row_block, x_rows, w_gate, w_up, w_down, biases)


def _combine_kernel(h_ref, y_ref, meta_ref, g_ref, b_ref, o_ref):
    tm = h_ref.shape[0]
    meta_cols = jnp.concatenate([meta_ref[...], jnp.zeros((LANES - META_ROWS, tm), jnp.float32)], axis=0).T
    ffn_hi, ffn_lo = 0.0, 0.0
    for k in range(TOP_K):
        gate = meta_cols[:, 2 * TOP_K + k:2 * TOP_K + k + 1]
        y_hi, y_lo = _unpack_bf16_pairs(y_ref[k])
        ffn_hi = ffn_hi + gate * y_hi
        ffn_lo = ffn_lo + gate * y_lo
    ffn = jnp.concatenate([ffn_hi, ffn_lo], axis=1)
    o_ref[...] = _layer_norm(DEEPNORM_ALPHA * h_ref[...] + ffn, g_ref[...], b_ref[...])


def _combine(h, y_tok, meta, ln2_g, ln2_b):
    t = h.shape[0]
    tm = COMBINE_ROWS
    return pl.pallas_call(
        _combine_kernel,
        grid=(t // tm,),
        in_specs=[
            pl.BlockSpec((tm, D_MODEL), lambda i: (i, 0)),
            pl.BlockSpec((TOP_K, tm, D_MODEL // 2), lambda i: (0, i, 0)),
            pl.BlockSpec((META_ROWS, tm), lambda i: (0, i)),
            pl.BlockSpec((1, D_MODEL), lambda i: (0, 0)),
            pl.BlockSpec((1, D_MODEL), lambda i: (0, 0)),
        ],
        out_specs=pl.BlockSpec((tm, D_MODEL), lambda i: (i, 0)),
        out_shape=jax.ShapeDtypeStruct((t, D_MODEL), jnp.float32),
        compiler_params=pltpu.CompilerParams(
            dimension_semantics=("arbitrary",), vmem_limit_bytes=VMEM_LIMIT_BYTES),
        name="combine",
    )(h, y_tok, meta, ln2_g, ln2_b)


def _layer(x2, seq_len, w_in, b_in, sinks, ln_v_g, ln_v_b, w_spatial, b_spatial, w_out, b_out,
           ln1_g, ln1_b, w_router, b_router, w_gate, b_gate, w_up, b_up, w_down, b_down, ln2_g, ln2_b):
    t = x2.shape[0]
    tk = t * TOP_K
    bm = EXPERT_ROWS
    bf16 = jnp.bfloat16

    w_r_hi = w_router.astype(bf16)
    w_r_lo = (w_router - w_r_hi.astype(jnp.float32)).astype(bf16)
    lane_pad = jnp.zeros((D_MODEL, ROUTER_LANES - 2 * N_EXPERTS), bf16)
    w_r = jnp.concatenate([w_r_hi, w_r_lo, lane_pad], axis=1)
    w_r_hi_only = jnp.concatenate([w_r_hi, jnp.zeros_like(w_r_lo), lane_pad], axis=1)
    b_sp_full = jnp.repeat(b_spatial.T, GMLP_GROUP_DIM, axis=1)

    h, h_packed, meta, counts = _mixer(
        x2, sinks, w_in.astype(bf16), b_in[None], ln_v_g[None], ln_v_b[None], w_spatial, b_sp_full,
        w_out.astype(bf16), b_out[None], ln1_g[None], ln1_b[None], w_r, w_r_hi_only, b_router[:, None], seq_len)

    counts = counts[:, 0].astype(jnp.int32)
    experts = jnp.arange(N_EXPERTS, dtype=jnp.int32)
    padded = (counts + bm - 1) // bm * bm
    padded_end = jnp.sum(jnp.where(experts[None, :] <= experts[:, None], padded[None, :], 0), axis=1)
    padded_start = padded_end - padded
    n_blocks = tk // bm + N_EXPERTS
    n_rows = n_blocks * bm
    top_idx_t = meta[:TOP_K].astype(jnp.int32)
    rank_t = meta[TOP_K:2 * TOP_K].astype(jnp.int32)
    dest_t = rank_t + jnp.sum(
        jnp.where(top_idx_t[None] == experts[:, None, None], padded_start[:, None, None], 0), axis=0)
    block_start = jnp.arange(n_blocks, dtype=jnp.int32) * bm
    block_e = jnp.minimum(
        jnp.sum((padded_end[None, :] <= block_start[:, None]).astype(jnp.int32), axis=1), N_EXPERTS - 1)
    valid_end = jnp.sum(jnp.where(block_e[:, None] == experts[None, :], (padded_start + counts)[None, :], 0), axis=1)
    n_valid = jnp.clip(valid_end - block_start, 0, bm)

    x_rows = _dispatch_rows(dest_t, h_packed, n_rows)
    y_rows = _experts(block_e, n_valid, x_rows, w_gate, b_gate, w_up, b_up, w_down, b_down)
    y_tok = _gather_rows(dest_t.reshape(-1), y_rows).reshape(TOP_K, t, D_MODEL // 2)
    return _combine(h, y_tok, meta, ln2_g[None], ln2_b[None])


def kernel(x, w_in, b_in, sinks, ln_v_g, ln_v_b, w_spatial, b_spatial, w_out, b_out, ln1_g, ln1_b,
           w_router, b_router, w_gate, b_gate, w_up, b_up, w_down, b_down, ln2_g, ln2_b):
    batch, seq_len, d = x.shape
    x2 = x.reshape(batch * seq_len, d)
    for l in range(DEPTH):
        x2 = _layer(x2, seq_len, w_in[l], b_in[l], sinks[l], ln_v_g[l], ln_v_b[l], w_spatial[l],
                    b_spatial[l], w_out[l], b_out[l], ln1_g[l], ln1_b[l], w_router[l], b_router[l],
                    w_gate[l], b_gate[l], w_up[l], b_up[l], w_down[l], b_down[l], ln2_g[l], ln2_b[l])
    return x2.reshape(batch, seq_len, d)
```

```python
import functools

import jax
import jax.numpy as jnp
from jax import lax
from jax.experimental import pallas as pl
from jax.experimental.pallas import tpu as pltpu
from jax.experimental.pallas import tpu_sc as plsc

D_MODEL = 1024
HEAD_DIM = 64
N_Q_HEADS = 8
N_KV_HEADS = 2
Q_REP = N_Q_HEADS // N_KV_HEADS
ATTN_WIDTH = N_Q_HEADS * HEAD_DIM
KV_WIDTH = N_KV_HEADS * HEAD_DIM
ATTN_BLOCK = 128
N_GMLP_GROUPS = 8
GMLP_WIDTH = D_MODEL - ATTN_WIDTH
GMLP_GROUP_DIM = GMLP_WIDTH // N_GMLP_GROUPS
IN_WIDTH = ATTN_WIDTH + 2 * KV_WIDTH + 2 * GMLP_WIDTH
N_EXPERTS = 32
TOP_K = 4
SWIGLU_LIMIT = 7.0
SWIGLU_ALPHA = 1.702
LN_EPS = 1e-5
DEPTH = 1
DEEPNORM_ALPHA = (2.0 * DEPTH) ** 0.25
NEG_INF = -1e30
LOG2_E = 1.4426950408889634

LANES = 128
SUBLANES = 8

MIXER_ROWS = 512
PROJ_CHUNK = 256
EXPERT_ROWS = 1024
EXPERT_PARTIAL_ROWS = (128, 256, 384, 512, 768)
GATHER_WINDOW = 64
META_ROWS = 16
ROUTER_LANES = LANES
COMBINE_ROWS = 1024
VMEM_LIMIT_BYTES = 56 * 1024 * 1024

_O_K = ATTN_WIDTH
_O_V = _O_K + KV_WIDTH
_O_U = _O_V + KV_WIDTH
_O_G = _O_U + GMLP_WIDTH


def _pack_bf16_pairs(v):
    n = v.shape[1] // 2
    hi = lax.bitcast_convert_type(v[:, :n].astype(jnp.float32), jnp.int32)
    lo = lax.bitcast_convert_type(v[:, n:].astype(jnp.float32), jnp.int32)
    return hi | lax.shift_right_logical(lo, 16)


def _unpack_bf16_pairs(p):
    hi = lax.bitcast_convert_type(p & jnp.int32(-65536), jnp.float32)
    lo = lax.bitcast_convert_type(lax.shift_left(p, 16), jnp.float32)
    return hi, lo


def _layer_norm(v, g, b):
    mu = jnp.mean(v, axis=-1, keepdims=True)
    vc = v - mu
    var = jnp.mean(vc * vc, axis=-1, keepdims=True)
    return vc * lax.rsqrt(var + LN_EPS) * g + b


def _attention_block(q, kb, vb, bias_ref, bias_sel):
    outs = []
    for g in range(N_KV_HEADS):
        kg = kb[:, g * HEAD_DIM:(g + 1) * HEAD_DIM]
        vg = vb[:, g * HEAD_DIM:(g + 1) * HEAD_DIM]
        qg = jnp.concatenate(
            [q[:, (g * Q_REP + r) * HEAD_DIM:(g * Q_REP + r + 1) * HEAD_DIM] for r in range(Q_REP)],
            axis=0).astype(jnp.bfloat16)
        s = (lax.dot_general(qg, kg, (((1,), (1,)), ((), ())), preferred_element_type=jnp.float32)
             + bias_ref[g, bias_sel])
        p = jnp.exp2(s - jnp.max(s, axis=-1, keepdims=True))
        denom = jnp.sum(p, axis=-1, keepdims=True)
        o = jnp.dot(p.astype(jnp.bfloat16), vg, preferred_element_type=jnp.float32) / denom
        outs.extend(o[r * ATTN_BLOCK:(r + 1) * ATTN_BLOCK] for r in range(Q_REP))
    return jnp.concatenate(outs, axis=-1)


def _mixer_kernel(x_ref, w_in_ref, b_in_ref, bias_ref, lnv_g_ref, lnv_b_ref, grp_avg_ref,
                  w_sp_ref, b_sp_ref, w_out_ref, b_out_ref, ln1_g_ref, ln1_b_ref,
                  w_r_ref, w_r_hi_ref, b_r_ref, tri_ref,
                  h_ref, hp_ref, meta_ref, count_ref,
                  kv_prev_ref, hres_ref, *, steps_per_seq, n_steps):
    i = pl.program_id(0)
    first_step = (i % steps_per_seq) == 0
    tm = x_ref.shape[0]
    n_sub = tm // ATTN_BLOCK

    @pl.when(i == 0)
    def _():
        count_ref[...] = jnp.zeros_like(count_ref)

    @pl.when(first_step)
    def _():
        kv_prev_ref[...] = jnp.zeros_like(kv_prev_ref)

    def route_stages():
        h = _layer_norm(hres_ref[...], ln1_g_ref[...], ln1_b_ref[...])
        h_ref[...] = h
        yield
        h_hi = h.astype(jnp.bfloat16)
        hp_ref[...] = _pack_bf16_pairs(h_hi)
        h_lo = (h - h_hi.astype(jnp.float32)).astype(jnp.bfloat16)
        yield
        part = (jnp.dot(h_hi, w_r_ref[...], preferred_element_type=jnp.float32)
                + jnp.dot(h_lo, w_r_hi_ref[...], preferred_element_type=jnp.float32)).T
        logits = part[:N_EXPERTS] + part[N_EXPERTS:2 * N_EXPERTS] + b_r_ref[...]
        yield
        n_grp = N_EXPERTS // SUBLANES
        grp = [logits[SUBLANES * g:SUBLANES * (g + 1)] for g in range(n_grp)]
        sub = lax.broadcasted_iota(jnp.int32, (SUBLANES, tm), 0)
        beaten = [jnp.zeros((SUBLANES, tm), jnp.float32) for _ in range(n_grp)]
        for e2 in range(N_EXPERTS):
            g2, r2 = divmod(e2, SUBLANES)
            row = logits[e2:e2 + 1]
            for g in range(n_grp):
                if g > g2:
                    wins = jnp.where(row >= grp[g], 1.0, 0.0)
                elif g < g2:
                    wins = jnp.where(row > grp[g], 1.0, 0.0)
                else:
                    wins = jnp.where(sub > r2, jnp.where(row >= grp[g], 1.0, 0.0),
                                     jnp.where(row > grp[g], 1.0, 0.0))
                beaten[g] = beaten[g] + wins
            if e2 == N_EXPERTS // 2 - 1:
                yield
        place = jnp.concatenate(beaten, axis=0)
        expert_id = lax.broadcasted_iota(jnp.int32, (N_EXPERTS, tm), 0).astype(jnp.float32)
        onehot = jnp.where(place < TOP_K, 1.0, 0.0)
        yield
        before = (jnp.dot(onehot.astype(jnp.bfloat16), tri_ref[...], preferred_element_type=jnp.float32)
                  + count_ref[...])

        def pick(k, table):
            return jnp.sum(jnp.where(place == k, table, 0.0), axis=0, keepdims=True)

        vals = [pick(k, logits) for k in range(TOP_K)]
        exps = [jnp.exp(v - vals[0]) for v in vals]
        denom = exps[0] + exps[1] + exps[2] + exps[3]
        yield
        meta_ref[...] = jnp.zeros_like(meta_ref)
        for k in range(TOP_K):
            meta_ref[k:k + 1, :] = pick(k, expert_id)
            meta_ref[TOP_K + k:TOP_K + k + 1, :] = pick(k, before)
            meta_ref[2 * TOP_K + k:2 * TOP_K + k + 1, :] = exps[k] / denom
        count_ref[...] += jnp.sum(onehot, axis=1, keepdims=True)

    def mix_block(stages):
        x = x_ref[...]
        x_bf = x.astype(jnp.bfloat16)
        proj = []
        for c in range(IN_WIDTH // PROJ_CHUNK):
            cols = pl.ds(c * PROJ_CHUNK, PROJ_CHUNK)
            proj.append(jnp.dot(x_bf, w_in_ref[:, cols], preferred_element_type=jnp.float32) + b_in_ref[:, cols])
            next(stages, None)
        for _ in stages:
            pass

        def proj_cols(lo, hi):
            return jnp.concatenate(proj[lo // PROJ_CHUNK:hi // PROJ_CHUNK], axis=-1)

        q_all = proj_cols(0, _O_K) * (LOG2_E * HEAD_DIM ** -0.5)
        kv_all = proj_cols(_O_K, _O_U).astype(jnp.bfloat16)
        k_all, v_all = kv_all[:, :KV_WIDTH], kv_all[:, KV_WIDTH:]
        k_prev = kv_prev_ref[:, :KV_WIDTH]
        v_prev = kv_prev_ref[:, KV_WIDTH:]
        is_row0 = lax.broadcasted_iota(jnp.int32, (ATTN_BLOCK, KV_WIDTH), 0) == 0
        attn_blocks = []
        for sb in range(n_sub):
            rows = slice(sb * ATTN_BLOCK, (sb + 1) * ATTN_BLOCK)
            k_cur, v_cur = k_all[rows], v_all[rows]
            kb = jnp.concatenate([jnp.where(is_row0, 0, k_prev), k_cur], axis=0)
            vb = jnp.concatenate([jnp.where(is_row0, 0, v_prev), v_cur], axis=0)
            bias_sel = jnp.where(first_step, 1, 0) if sb == 0 else 0
            attn_blocks.append(_attention_block(q_all[rows], kb, vb, bias_ref, bias_sel))
            k_prev, v_prev = k_cur, v_cur
        kv_prev_ref[:, :KV_WIDTH] = k_prev
        kv_prev_ref[:, KV_WIDTH:] = v_prev
        attn = jnp.concatenate(attn_blocks, axis=0)

        u = jax.nn.gelu(proj_cols(_O_U, _O_G))
        gg = jax.nn.gelu(proj_cols(_O_G, IN_WIDTH))
        avg = grp_avg_ref[...]
        mu = jnp.dot(gg.astype(jnp.bfloat16), avg, preferred_element_type=jnp.float32)
        gc = gg - mu
        var = jnp.dot((gc * gc).astype(jnp.bfloat16), avg, preferred_element_type=jnp.float32)
        gn = (gc * lax.rsqrt(var + LN_EPS) * lnv_g_ref[...] + lnv_b_ref[...]).astype(jnp.bfloat16)
        causal = (lax.broadcasted_iota(jnp.int32, (ATTN_BLOCK, ATTN_BLOCK), 0)
                  >= lax.broadcasted_iota(jnp.int32, (ATTN_BLOCK, ATTN_BLOCK), 1))
        w_sp = [jnp.where(causal, w_sp_ref[g], 0.0).astype(jnp.bfloat16) for g in range(N_GMLP_GROUPS)]
        mixed_chunks = []
        for c in range(n_sub):
            rows = slice(c * ATTN_BLOCK, (c + 1) * ATTN_BLOCK)
            pieces = [
                jnp.dot(w_sp[g], gn[rows, g * GMLP_GROUP_DIM:(g + 1) * GMLP_GROUP_DIM],
                        preferred_element_type=jnp.float32)
                for g in range(N_GMLP_GROUPS)]
            mixed_chunks.append(jnp.concatenate(pieces, axis=-1) + b_sp_ref[...])
        sgu = u * jnp.concatenate(mixed_chunks, axis=0)

        mix = (jnp.dot(attn.astype(jnp.bfloat16), w_out_ref[:ATTN_WIDTH, :], preferred_element_type=jnp.float32)
               + jnp.dot(sgu.astype(jnp.bfloat16), w_out_ref[ATTN_WIDTH:, :], preferred_element_type=jnp.float32)
               + b_out_ref[...])
        hres_ref[...] = DEEPNORM_ALPHA * x + mix

    @pl.when(i == 0)
    def _():
        mix_block(iter(()))

    @pl.when(jnp.logical_and(i > 0, i < n_steps))
    def _():
        mix_block(route_stages())

    @pl.when(i == n_steps)
    def _():
        for _ in route_stages():
            pass


def _score_bias(sinks):
    t_idx = jnp.arange(Q_REP * ATTN_BLOCK)[:, None] % ATTN_BLOCK
    s_idx = jnp.arange(2 * ATTN_BLOCK)[None, :]
    diff = t_idx + ATTN_BLOCK - s_idx
    band = (diff >= 0) & (diff < ATTN_BLOCK)
    masks = jnp.stack([band, band & (s_idx >= ATTN_BLOCK)])
    bias = jnp.where(masks, 0.0, NEG_INF).astype(jnp.float32)
    sink_rows = jnp.repeat(sinks.reshape(N_KV_HEADS, Q_REP) * LOG2_E, ATTN_BLOCK, axis=1)
    return jnp.where(s_idx == 0, sink_rows[:, None, :, None], bias[None])


def _mixer(x2, sinks, w_in, b_in, lnv_g, lnv_b, w_sp, b_sp_full, w_out, b_out, ln1_g, ln1_b,
           w_r, w_r_hi, b_r, seq_len):
    t = x2.shape[0]
    tm = MIXER_ROWS
    n_steps = t // tm
    grp = jnp.arange(GMLP_WIDTH) // GMLP_GROUP_DIM
    grp_avg = jnp.where(grp[:, None] == grp[None, :], 1.0 / GMLP_GROUP_DIM, 0.0).astype(jnp.bfloat16)
    tri = (jnp.arange(tm)[:, None] < jnp.arange(tm)[None, :]).astype(jnp.bfloat16)

    def full(shape):
        return pl.BlockSpec(shape, lambda i: (0,) * len(shape))

    return pl.pallas_call(
        functools.partial(_mixer_kernel, steps_per_seq=seq_len // tm, n_steps=n_steps),
        grid=(n_steps + 1,),
        in_specs=[
            pl.BlockSpec((tm, D_MODEL), lambda i: (jnp.minimum(i, n_steps - 1), 0)),
            full((D_MODEL, IN_WIDTH)), full((1, IN_WIDTH)),
            full((N_KV_HEADS, 2, Q_REP * ATTN_BLOCK, 2 * ATTN_BLOCK)),
            full((1, GMLP_WIDTH)), full((1, GMLP_WIDTH)), full((GMLP_WIDTH, GMLP_WIDTH)),
            full((N_GMLP_GROUPS, ATTN_BLOCK, ATTN_BLOCK)), full((ATTN_BLOCK, GMLP_WIDTH)),
            full((D_MODEL, D_MODEL)), full((1, D_MODEL)), full((1, D_MODEL)), full((1, D_MODEL)),
            full((D_MODEL, ROUTER_LANES)), full((D_MODEL, ROUTER_LANES)), full((N_EXPERTS, 1)), full((tm, tm)),
        ],
        out_specs=[
            pl.BlockSpec((tm, D_MODEL), lambda i: (jnp.maximum(i - 1, 0), 0)),
            pl.BlockSpec((tm, D_MODEL // 2), lambda i: (jnp.maximum(i - 1, 0), 0)),
            pl.BlockSpec((META_ROWS, tm), lambda i: (0, jnp.maximum(i - 1, 0))),
            pl.BlockSpec((N_EXPERTS, 1), lambda i: (0, 0)),
        ],
        out_shape=[
            jax.ShapeDtypeStruct((t, D_MODEL), jnp.float32),
            jax.ShapeDtypeStruct((t, D_MODEL // 2), jnp.int32),
            jax.ShapeDtypeStruct((META_ROWS, t), jnp.float32),
            jax.ShapeDtypeStruct((N_EXPERTS, 1), jnp.float32),
        ],
        scratch_shapes=[pltpu.VMEM((ATTN_BLOCK, 2 * KV_WIDTH), jnp.bfloat16),
                        pltpu.VMEM((tm, D_MODEL), jnp.float32)],
        compiler_params=pltpu.CompilerParams(
            dimension_semantics=("arbitrary",), vmem_limit_bytes=VMEM_LIMIT_BYTES),
        name="mixer",
    )(x2, w_in, b_in, _score_bias(sinks), lnv_g, lnv_b, grp_avg, w_sp, b_sp_full, w_out, b_out, ln1_g, ln1_b,
      w_r, w_r_hi, b_r, tri)


def _gather_rows(idx, src):
    n = idx.shape[0]
    width = src.shape[1]
    win = GATHER_WINDOW
    sc = plsc.get_sparse_core_info()
    n_workers = sc.num_cores * sc.num_subcores
    per_worker = n // n_workers
    n_pairs = per_worker // (2 * win)
    assert n_pairs * 2 * win * n_workers == n
    mesh = plsc.VectorSubcoreMesh(core_axis_name="core", subcore_axis_name="subcore")

    @functools.partial(
        pl.kernel, out_type=jax.ShapeDtypeStruct((n, width), src.dtype), mesh=mesh,
        scratch_types=[pltpu.VMEM((per_worker,), jnp.int32), pltpu.VMEM((2, win, width), src.dtype),
                       pltpu.SemaphoreType.DMA((2,)), pltpu.SemaphoreType.DMA((2,))],
        name="gather_rows")
    def gather(src_hbm, idx_hbm, out_hbm, idx_v, rows_v, fetch_sem, store_sem):
        worker = lax.axis_index("subcore") * sc.num_cores + lax.axis_index("core")
        base = worker * per_worker
        pltpu.sync_copy(idx_hbm.at[pl.ds(base, per_worker)], idx_v)

        def fetch(chunk, buf):
            return pltpu.make_async_copy(src_hbm.at[idx_v.at[pl.ds(chunk * win, win)]], rows_v.at[buf],
                                         fetch_sem.at[buf])

        def store(chunk, buf):
            return pltpu.make_async_copy(rows_v.at[buf], out_hbm.at[pl.ds(base + chunk * win, win)],
                                         store_sem.at[buf])

        @pl.loop(0, n_pairs)
        def _(p):
            for buf in range(2):
                @pl.when(p > 0)
                def _():
                    store(2 * p - 2 + buf, buf).wait()
                fetch(2 * p + buf, buf).start()
            for buf in range(2):
                fetch(2 * p + buf, buf).wait()
                store(2 * p + buf, buf).start()

        for buf in range(2):
            store(2 * n_pairs - 2 + buf, buf).wait()

    return gather(src, idx)


def _dispatch_rows(dest_t, src, n_rows):
    t, width = src.shape
    win = GATHER_WINDOW
    sc = plsc.get_sparse_core_info()
    n_workers = sc.num_cores * sc.num_subcores
    per_worker = t // n_workers
    n_chunks = per_worker // win
    n_pairs = n_chunks // 2
    assert n_pairs * 2 * win * n_workers == t
    idx = dest_t.reshape(TOP_K, n_workers, n_chunks, win).transpose(1, 0, 2, 3)
    idx = idx.reshape(n_workers, TOP_K * n_chunks, win)
    mesh = plsc.VectorSubcoreMesh(core_axis_name="core", subcore_axis_name="subcore")

    @functools.partial(
        pl.kernel, out_type=jax.ShapeDtypeStruct((n_rows, width), src.dtype), mesh=mesh,
        scratch_types=[pltpu.VMEM((TOP_K * n_chunks, win), jnp.int32), pltpu.VMEM((2, win, width), src.dtype),
                       pltpu.SemaphoreType.DMA((2,)), pltpu.SemaphoreType.DMA((2,))],
        name="dispatch_rows")
    def dispatch(src_hbm, idx_hbm, out_hbm, idx_v, rows_v, fetch_sem, store_sem):
        worker = lax.axis_index("subcore") * sc.num_cores + lax.axis_index("core")
        base = worker * per_worker
        pltpu.sync_copy(idx_hbm.at[worker], idx_v)

        def fetch(chunk, buf):
            return pltpu.make_async_copy(src_hbm.at[pl.ds(base + chunk * win, win)], rows_v.at[buf],
                                         fetch_sem.at[buf])

        def store(chunk, k, buf):
            return pltpu.make_async_copy(rows_v.at[buf], out_hbm.at[idx_v.at[k * n_chunks + chunk]],
                                         store_sem.at[buf])

        @pl.loop(0, n_pairs)
        def _(p):
            for buf in range(2):
                @pl.when(p > 0)
                def _():
                    for k in range(TOP_K):
                        store(2 * p - 2 + buf, k, buf).wait()
                fetch(2 * p + buf, buf).start()
            for buf in range(2):
                fetch(2 * p + buf, buf).wait()
                for k in range(TOP_K):
                    store(2 * p + buf, k, buf).start()

        for buf in range(2):
            for k in range(TOP_K):
                store(2 * n_pairs - 2 + buf, k, buf).wait()

    return dispatch(src, idx)


def _expert_kernel(block_e_ref, n_valid_ref, first_ref, next_e_ref, row_block_ref,
                   x_ref, wg_hbm, wu_hbm, wd_hbm, bias_ref,
                   y_ref, wg_f32, wu_f32, wd_f32, wg_bf, wu_bf, wd_bf, sem):
    i = pl.program_id(0)
    n_valid = n_valid_ref[i]
    staged = ((wg_hbm, wg_f32, wg_bf), (wu_hbm, wu_f32, wu_bf), (wd_hbm, wd_f32, wd_bf))

    def weight_copy(m, expert):
        return pltpu.make_async_copy(staged[m][0].at[expert], staged[m][1], sem.at[m])

    @pl.when(i == 0)
    def _():
        for m in range(3):
            weight_copy(m, block_e_ref[0]).start()

    @pl.when(first_ref[i] == 1)
    def _():
        for m in range(3):
            weight_copy(m, block_e_ref[i]).wait()
            staged[m][2][...] = staged[m][1][...].astype(jnp.bfloat16)

            @pl.when(next_e_ref[i] >= 0)
            def _():
                weight_copy(m, next_e_ref[i]).start()

    def expert_mlp(rows):
        row = lax.broadcasted_iota(jnp.int32, (rows, 1), 0)
        x_hi, x_lo = _unpack_bf16_pairs(jnp.where(row < n_valid, x_ref[:rows, :], 0))
        xb = jnp.concatenate([x_hi, x_lo], axis=1).astype(jnp.bfloat16)
        gt = jnp.minimum(jnp.dot(xb, wg_bf[...], preferred_element_type=jnp.float32) + bias_ref[0, 0:1], SWIGLU_LIMIT)
        up = jnp.clip(jnp.dot(xb, wu_bf[...], preferred_element_type=jnp.float32) + bias_ref[0, 1:2],
                      -SWIGLU_LIMIT, SWIGLU_LIMIT)
        hid = gt * jax.nn.sigmoid(SWIGLU_ALPHA * gt) * (up + 1.0)
        y = jnp.dot(hid.astype(jnp.bfloat16), wd_bf[...], preferred_element_type=jnp.float32) + bias_ref[0, 2:3]
        y_ref[:rows, :] = _pack_bf16_pairs(y.astype(jnp.bfloat16))
        if rows < y_ref.shape[0]:
            y_ref[rows:, :] = jnp.zeros((y_ref.shape[0] - rows, y_ref.shape[1]), y_ref.dtype)

    row_options = (0,) + EXPERT_PARTIAL_ROWS + (x_ref.shape[0],)
    for lo, hi in zip(row_options[:-1], row_options[1:]):
        pl.when(jnp.logical_and(n_valid > lo, n_valid <= hi))(functools.partial(expert_mlp, hi))


def _experts(block_e, n_valid, x_rows, w_gate, b_gate, w_up, b_up, w_down, b_down):
    n_rows = x_rows.shape[0]
    bm = EXPERT_ROWS
    n_blocks = n_rows // bm
    d_ff = w_gate.shape[2]

    blocks = jnp.arange(n_blocks, dtype=jnp.int32)
    last_used = jnp.sum((n_valid > 0).astype(jnp.int32)) - 1
    row_block = jnp.minimum(blocks, last_used)
    block_e = jnp.sum(jnp.where(blocks[None, :] == row_block[:, None], block_e[None, :], 0), axis=1)

    is_first = jnp.concatenate([jnp.ones((1,), jnp.int32), (block_e[1:] != block_e[:-1]).astype(jnp.int32)])
    next_e = jnp.min(jnp.where(block_e[None, :] > block_e[:, None], block_e[None, :], N_EXPERTS), axis=1)
    next_e = jnp.where(next_e == N_EXPERTS, -1, next_e)

    def by_expert(shape):
        return pl.BlockSpec(shape, lambda i, be, *_: (be[i],) + (0,) * (len(shape) - 1))

    hbm = pl.BlockSpec(memory_space=pl.ANY)
    assert d_ff == D_MODEL
    biases = jnp.zeros((N_EXPERTS, SUBLANES, D_MODEL), jnp.float32)
    biases = biases.at[:, 0].set(b_gate).at[:, 1].set(b_up).at[:, 2].set(b_down)
    return pl.pallas_call(
        _expert_kernel,
        grid_spec=pltpu.PrefetchScalarGridSpec(
            num_scalar_prefetch=5,
            grid=(n_blocks,),
            in_specs=[
                pl.BlockSpec((bm, D_MODEL // 2), lambda i, be, nv, fi, ne, rb: (rb[i], 0)),
                hbm, hbm, hbm, by_expert((1, SUBLANES, D_MODEL)),
            ],
            out_specs=pl.BlockSpec((bm, D_MODEL // 2), lambda i, be, nv, fi, ne, rb: (rb[i], 0)),
            scratch_shapes=[
                pltpu.VMEM((D_MODEL, d_ff), jnp.float32),
                pltpu.VMEM((D_MODEL, d_ff), jnp.float32),
                pltpu.VMEM((d_ff, D_MODEL), jnp.float32),
                pltpu.VMEM((D_MODEL, d_ff), jnp.bfloat16),
                pltpu.VMEM((D_MODEL, d_ff), jnp.bfloat16),
                pltpu.VMEM((d_ff, D_MODEL), jnp.bfloat16),
                pltpu.SemaphoreType.DMA((3,)),
            ],
        ),
        out_shape=jax.ShapeDtypeStruct((n_rows, D_MODEL // 2), jnp.int32),
        input_output_aliases={5: 0},
        compiler_params=pltpu.CompilerParams(
            dimension_semantics=("arbitrary",), vmem_limit_bytes=VMEM_LIMIT_BYTES),
        name="experts",
    )(block_e, n_valid, is_first, next_e, row_block, x_rows, w_gate, w_up, w_down, biases)


def _combine_kernel(h_ref, y_ref, meta_ref, g_ref, b_ref, o_ref):
    tm = h_ref.shape[0]
    meta_cols = jnp.concatenate([meta_ref[...], jnp.zeros((LANES - META_ROWS, tm), jnp.float32)], axis=0).T
    ffn_hi, ffn_lo = 0.0, 0.0
    for k in range(TOP_K):
        gate = meta_cols[:, 2 * TOP_K + k:2 * TOP_K + k + 1]
        y_hi, y_lo = _unpack_bf16_pairs(y_ref[k])
        ffn_hi = ffn_hi + gate * y_hi
        ffn_lo = ffn_lo + gate * y_lo
    ffn = jnp.concatenate([ffn_hi, ffn_lo], axis=1)
    o_ref[...] = _layer_norm(DEEPNORM_ALPHA * h_ref[...] + ffn, g_ref[...], b_ref[...])


def _combine(h, y_tok, meta, ln2_g, ln2_b):
    t = h.shape[0]
    tm = COMBINE_ROWS
    return pl.pallas_call(
        _combine_kernel,
        grid=(t // tm,),
        in_specs=[
            pl.BlockSpec((tm, D_MODEL), lambda i: (i, 0)),
            pl.BlockSpec((TOP_K, tm, D_MODEL // 2), lambda i: (0, i, 0)),
            pl.BlockSpec((META_ROWS, tm), lambda i: (0, i)),
            pl.BlockSpec((1, D_MODEL), lambda i: (0, 0)),
            pl.BlockSpec((1, D_MODEL), lambda i: (0, 0)),
        ],
        out_specs=pl.BlockSpec((tm, D_MODEL), lambda i: (i, 0)),
        out_shape=jax.ShapeDtypeStruct((t, D_MODEL), jnp.float32),
        compiler_params=pltpu.CompilerParams(
            dimension_semantics=("arbitrary",), vmem_limit_bytes=VMEM_LIMIT_BYTES),
        name="combine",
    )(h, y_tok, meta, ln2_g, ln2_b)


def _layer(x2, seq_len, w_in, b_in, sinks, ln_v_g, ln_v_b, w_spatial, b_spatial, w_out, b_out,
           ln1_g, ln1_b, w_router, b_router, w_gate, b_gate, w_up, b_up, w_down, b_down, ln2_g, ln2_b):
    t = x2.shape[0]
    tk = t * TOP_K
    bm = EXPERT_ROWS
    bf16 = jnp.bfloat16

    w_r_hi = w_router.astype(bf16)
    w_r_lo = (w_router - w_r_hi.astype(jnp.float32)).astype(bf16)
    lane_pad = jnp.zeros((D_MODEL, ROUTER_LANES - 2 * N_EXPERTS), bf16)
    w_r = jnp.concatenate([w_r_hi, w_r_lo, lane_pad], axis=1)
    w_r_hi_only = jnp.concatenate([w_r_hi, jnp.zeros_like(w_r_lo), lane_pad], axis=1)
    b_sp_full = jnp.repeat(b_spatial.T, GMLP_GROUP_DIM, axis=1)

    h, h_packed, meta, counts = _mixer(
        x2, sinks, w_in.astype(bf16), b_in[None], ln_v_g[None], ln_v_b[None], w_spatial, b_sp_full,
        w_out.astype(bf16), b_out[None], ln1_g[None], ln1_b[None], w_r, w_r_hi_only, b_router[:, None], seq_len)

    counts = counts[:, 0].astype(jnp.int32)
    experts = jnp.arange(N_EXPERTS, dtype=jnp.int32)
    padded = (counts + bm - 1) // bm * bm
    padded_end = jnp.sum(jnp.where(experts[None, :] <= experts[:, None], padded[None, :], 0), axis=1)
    padded_start = padded_end - padded
    n_blocks = tk // bm + N_EXPERTS
    n_rows = n_blocks * bm
    top_idx_t = meta[:TOP_K].astype(jnp.int32)
    rank_t = meta[TOP_K:2 * TOP_K].astype(jnp.int32)
    dest_t = rank_t + jnp.sum(
        jnp.where(top_idx_t[None] == experts[:, None, None], padded_start[:, None, None], 0), axis=0)
    block_start = jnp.arange(n_blocks, dtype=jnp.int32) * bm
    block_e = jnp.minimum(
        jnp.sum((padded_end[None, :] <= block_start[:, None]).astype(jnp.int32), axis=1), N_EXPERTS - 1)
    valid_end = jnp.sum(jnp.where(block_e[:, None] == experts[None, :], (padded_start + counts)[None, :], 0), axis=1)
    n_valid = jnp.clip(valid_end - block_start, 0, bm)

    x_rows = _dispatch_rows(dest_t, h_packed, n_rows)
    y_rows = _experts(block_e, n_valid, x_rows, w_gate, b_gate, w_up, b_up, w_down, b_down)
    y_tok = _gather_rows(dest_t.reshape(-1), y_rows).reshape(TOP_K, t, D_MODEL // 2)
    return _combine(h, y_tok, meta, ln2_g[None], ln2_b[None])


def kernel(x, w_in, b_in, sinks, ln_v_g, ln_v_b, w_spatial, b_spatial, w_out, b_out, ln1_g, ln1_b,
           w_router, b_router, w_gate, b_gate, w_up, b_up, w_down, b_down, ln2_g, ln2_b):
    batch, seq_len, d = x.shape
    x2 = x.reshape(batch * seq_len, d)
    for l in range(DEPTH):
        x2 = _layer(x2, seq_len, w_in[l], b_in[l], sinks[l], ln_v_g[l], ln_v_b[l], w_spatial[l],
                    b_spatial[l], w_out[l], b_out[l], ln1_g[l], ln1_b[l], w_router[l], b_router[l],
                    w_gate[l], b_gate[l], w_up[l], b_up[l], w_down[l], b_down[l], ln2_g[l], ln2_b[l])
    return x2.reshape(batch, seq_len, d)
```

```python
import functools

import jax
import jax.numpy as jnp
from jax import lax
from jax.experimental import pallas as pl
from jax.experimental.pallas import tpu as pltpu
from jax.experimental.pallas import tpu_sc as plsc

D_MODEL = 1024
HEAD_DIM = 64
N_Q_HEADS = 8
N_KV_HEADS = 2
Q_REP = N_Q_HEADS // N_KV_HEADS
ATTN_WIDTH = N_Q_HEADS * HEAD_DIM
KV_WIDTH = N_KV_HEADS * HEAD_DIM
ATTN_BLOCK = 128
N_GMLP_GROUPS = 8
GMLP_WIDTH = D_MODEL - ATTN_WIDTH
GMLP_GROUP_DIM = GMLP_WIDTH // N_GMLP_GROUPS
IN_WIDTH = ATTN_WIDTH + 2 * KV_WIDTH + 2 * GMLP_WIDTH
N_EXPERTS = 32
TOP_K = 4
SWIGLU_LIMIT = 7.0
SWIGLU_ALPHA = 1.702
LN_EPS = 1e-5
DEPTH = 1
DEEPNORM_ALPHA = (2.0 * DEPTH) ** 0.25
NEG_INF = -1e30
LOG2_E = 1.4426950408889634

LANES = 128
SUBLANES = 8

MIXER_ROWS = 512
PROJ_CHUNK = 256
EXPERT_ROWS = 1024
EXPERT_PARTIAL_ROWS = (128, 256, 384, 512, 768)
GATHER_WINDOW = 64
GATHER_BUFFERS = 4
DISPATCH_BUFFERS = 4
META_ROWS = 16
ROUTER_LANES = LANES
COMBINE_ROWS = 1024
VMEM_LIMIT_BYTES = 56 * 1024 * 1024

_O_K = ATTN_WIDTH
_O_V = _O_K + KV_WIDTH
_O_U = _O_V + KV_WIDTH
_O_G = _O_U + GMLP_WIDTH


def _pack_bf16_pairs(v):
    n = v.shape[1] // 2
    hi = lax.bitcast_convert_type(v[:, :n].astype(jnp.float32), jnp.int32)
    lo = lax.bitcast_convert_type(v[:, n:].astype(jnp.float32), jnp.int32)
    return hi | lax.shift_right_logical(lo, 16)


def _unpack_bf16_pairs(p):
    hi = lax.bitcast_convert_type(p & jnp.int32(-65536), jnp.float32)
    lo = lax.bitcast_convert_type(lax.shift_left(p, 16), jnp.float32)
    return hi, lo


def _layer_norm(v, g, b):
    mu = jnp.mean(v, axis=-1, keepdims=True)
    vc = v - mu
    var = jnp.mean(vc * vc, axis=-1, keepdims=True)
    return vc * lax.rsqrt(var + LN_EPS) * g + b


def _attention_block(q, kb, vb, bias_ref, bias_sel):
    outs = []
    for g in range(N_KV_HEADS):
        kg = kb[:, g * HEAD_DIM:(g + 1) * HEAD_DIM]
        vg = vb[:, g * HEAD_DIM:(g + 1) * HEAD_DIM]
        qg = jnp.concatenate(
            [q[:, (g * Q_REP + r) * HEAD_DIM:(g * Q_REP + r + 1) * HEAD_DIM] for r in range(Q_REP)],
            axis=0).astype(jnp.bfloat16)
        s = (lax.dot_general(qg, kg, (((1,), (1,)), ((), ())), preferred_element_type=jnp.float32)
             + bias_ref[g, bias_sel])
        p = jnp.exp2(s - jnp.max(s, axis=-1, keepdims=True))
        denom = jnp.sum(p, axis=-1, keepdims=True)
        o = jnp.dot(p.astype(jnp.bfloat16), vg, preferred_element_type=jnp.float32) / denom
        outs.extend(o[r * ATTN_BLOCK:(r + 1) * ATTN_BLOCK] for r in range(Q_REP))
    return jnp.concatenate(outs, axis=-1)


def _mixer_kernel(x_ref, w_in_ref, b_in_ref, bias_ref, lnv_g_ref, lnv_b_ref, grp_avg_ref,
                  w_sp_ref, b_sp_ref, w_out_ref, b_out_ref, ln1_g_ref, ln1_b_ref,
                  w_r_ref, w_r_hi_ref, b_r_ref, tri_ref,
                  h_ref, hp_ref, meta_ref, count_ref,
                  kv_prev_ref, hres_ref, *, steps_per_seq, n_steps):
    i = pl.program_id(0)
    first_step = (i % steps_per_seq) == 0
    tm = x_ref.shape[0]
    n_sub = tm // ATTN_BLOCK

    @pl.when(i == 0)
    def _():
        count_ref[...] = jnp.zeros_like(count_ref)

    @pl.when(first_step)
    def _():
        kv_prev_ref[...] = jnp.zeros_like(kv_prev_ref)

    def route_stages():
        h = _layer_norm(hres_ref[...], ln1_g_ref[...], ln1_b_ref[...])
        h_ref[...] = h
        yield
        h_hi = h.astype(jnp.bfloat16)
        hp_ref[...] = _pack_bf16_pairs(h_hi)
        h_lo = (h - h_hi.astype(jnp.float32)).astype(jnp.bfloat16)
        yield
        part = (jnp.dot(h_hi, w_r_ref[...], preferred_element_type=jnp.float32)
                + jnp.dot(h_lo, w_r_hi_ref[...], preferred_element_type=jnp.float32)).T
        logits = part[:N_EXPERTS] + part[N_EXPERTS:2 * N_EXPERTS] + b_r_ref[...]
        yield
        n_grp = N_EXPERTS // SUBLANES
        grp = [logits[SUBLANES * g:SUBLANES * (g + 1)] for g in range(n_grp)]
        sub = lax.broadcasted_iota(jnp.int32, (SUBLANES, tm), 0)
        beaten = [jnp.zeros((SUBLANES, tm), jnp.float32) for _ in range(n_grp)]
        for e2 in range(N_EXPERTS):
            g2, r2 = divmod(e2, SUBLANES)
            row = logits[e2:e2 + 1]
            for g in range(n_grp):
                if g > g2:
                    wins = jnp.where(row >= grp[g], 1.0, 0.0)
                elif g < g2:
                    wins = jnp.where(row > grp[g], 1.0, 0.0)
                else:
                    wins = jnp.where(sub > r2, jnp.where(row >= grp[g], 1.0, 0.0),
                                     jnp.where(row > grp[g], 1.0, 0.0))
                beaten[g] = beaten[g] + wins
            if e2 == N_EXPERTS // 2 - 1:
                yield
        place = jnp.concatenate(beaten, axis=0)
        expert_id = lax.broadcasted_iota(jnp.int32, (N_EXPERTS, tm), 0).astype(jnp.float32)
        onehot = jnp.where(place < TOP_K, 1.0, 0.0)
        yield
        before = (jnp.dot(onehot.astype(jnp.bfloat16), tri_ref[...], preferred_element_type=jnp.float32)
                  + count_ref[...])

        def pick(k, table):
            return jnp.sum(jnp.where(place == k, table, 0.0), axis=0, keepdims=True)

        vals = [pick(k, logits) for k in range(TOP_K)]
        exps = [jnp.exp(v - vals[0]) for v in vals]
        denom = exps[0] + exps[1] + exps[2] + exps[3]
        yield
        meta_ref[...] = jnp.zeros_like(meta_ref)
        for k in range(TOP_K):
            meta_ref[k:k + 1, :] = pick(k, expert_id)
            meta_ref[TOP_K + k:TOP_K + k + 1, :] = pick(k, before)
            meta_ref[2 * TOP_K + k:2 * TOP_K + k + 1, :] = exps[k] / denom
        count_ref[...] += jnp.sum(onehot, axis=1, keepdims=True)

    def mix_block(stages):
        x = x_ref[...]
        x_bf = x.astype(jnp.bfloat16)
        proj = []
        for c in range(IN_WIDTH // PROJ_CHUNK):
            cols = pl.ds(c * PROJ_CHUNK, PROJ_CHUNK)
            proj.append(jnp.dot(x_bf, w_in_ref[:, cols], preferred_element_type=jnp.float32) + b_in_ref[:, cols])
            next(stages, None)
        for _ in stages:
            pass

        def proj_cols(lo, hi):
            return jnp.concatenate(proj[lo // PROJ_CHUNK:hi // PROJ_CHUNK], axis=-1)

        q_all = proj_cols(0, _O_K) * (LOG2_E * HEAD_DIM ** -0.5)
        kv_all = proj_cols(_O_K, _O_U).astype(jnp.bfloat16)
        k_all, v_all = kv_all[:, :KV_WIDTH], kv_all[:, KV_WIDTH:]
        k_prev = kv_prev_ref[:, :KV_WIDTH]
        v_prev = kv_prev_ref[:, KV_WIDTH:]
        is_row0 = lax.broadcasted_iota(jnp.int32, (ATTN_BLOCK, KV_WIDTH), 0) == 0
        attn_blocks = []
        for sb in range(n_sub):
            rows = slice(sb * ATTN_BLOCK, (sb + 1) * ATTN_BLOCK)
            k_cur, v_cur = k_all[rows], v_all[rows]
            kb = jnp.concatenate([jnp.where(is_row0, 0, k_prev), k_cur], axis=0)
            vb = jnp.concatenate([jnp.where(is_row0, 0, v_prev), v_cur], axis=0)
            bias_sel = jnp.where(first_step, 1, 0) if sb == 0 else 0
            attn_blocks.append(_attention_block(q_all[rows], kb, vb, bias_ref, bias_sel))
            k_prev, v_prev = k_cur, v_cur
        kv_prev_ref[:, :KV_WIDTH] = k_prev
        kv_prev_ref[:, KV_WIDTH:] = v_prev
        attn = jnp.concatenate(attn_blocks, axis=0)

        u = jax.nn.gelu(proj_cols(_O_U, _O_G))
        gg = jax.nn.gelu(proj_cols(_O_G, IN_WIDTH))
        avg = grp_avg_ref[...]
        mu = jnp.dot(gg.astype(jnp.bfloat16), avg, preferred_element_type=jnp.float32)
        gc = gg - mu
        var = jnp.dot((gc * gc).astype(jnp.bfloat16), avg, preferred_element_type=jnp.float32)
        gn = (gc * lax.rsqrt(var + LN_EPS) * lnv_g_ref[...] + lnv_b_ref[...]).astype(jnp.bfloat16)
        causal = (lax.broadcasted_iota(jnp.int32, (ATTN_BLOCK, ATTN_BLOCK), 0)
                  >= lax.broadcasted_iota(jnp.int32, (ATTN_BLOCK, ATTN_BLOCK), 1))
        w_sp = [jnp.where(causal, w_sp_ref[g], 0.0).astype(jnp.bfloat16) for g in range(N_GMLP_GROUPS)]
        mixed_chunks = []
        for c in range(n_sub):
            rows = slice(c * ATTN_BLOCK, (c + 1) * ATTN_BLOCK)
            pieces = [
                jnp.dot(w_sp[g], gn[rows, g * GMLP_GROUP_DIM:(g + 1) * GMLP_GROUP_DIM],
                        preferred_element_type=jnp.float32)
                for g in range(N_GMLP_GROUPS)]
            mixed_chunks.append(jnp.concatenate(pieces, axis=-1) + b_sp_ref[...])
        sgu = u * jnp.concatenate(mixed_chunks, axis=0)

        mix = (jnp.dot(attn.astype(jnp.bfloat16), w_out_ref[:ATTN_WIDTH, :], preferred_element_type=jnp.float32)
               + jnp.dot(sgu.astype(jnp.bfloat16), w_out_ref[ATTN_WIDTH:, :], preferred_element_type=jnp.float32)
               + b_out_ref[...])
        hres_ref[...] = DEEPNORM_ALPHA * x + mix

    @pl.when(i == 0)
    def _():
        mix_block(iter(()))

    @pl.when(jnp.logical_and(i > 0, i < n_steps))
    def _():
        mix_block(route_stages())

    @pl.when(i == n_steps)
    def _():
        for _ in route_stages():
            pass


def _score_bias(sinks):
    t_idx = jnp.arange(Q_REP * ATTN_BLOCK)[:, None] % ATTN_BLOCK
    s_idx = jnp.arange(2 * ATTN_BLOCK)[None, :]
    diff = t_idx + ATTN_BLOCK - s_idx
    band = (diff >= 0) & (diff < ATTN_BLOCK)
    masks = jnp.stack([band, band & (s_idx >= ATTN_BLOCK)])
    bias = jnp.where(masks, 0.0, NEG_INF).astype(jnp.float32)
    sink_rows = jnp.repeat(sinks.reshape(N_KV_HEADS, Q_REP) * LOG2_E, ATTN_BLOCK, axis=1)
    return jnp.where(s_idx == 0, sink_rows[:, None, :, None], bias[None])


def _mixer(x2, sinks, w_in, b_in, lnv_g, lnv_b, w_sp, b_sp_full, w_out, b_out, ln1_g, ln1_b,
           w_r, w_r_hi, b_r, seq_len):
    t = x2.shape[0]
    tm = MIXER_ROWS
    n_steps = t // tm
    grp = jnp.arange(GMLP_WIDTH) // GMLP_GROUP_DIM
    grp_avg = jnp.where(grp[:, None] == grp[None, :], 1.0 / GMLP_GROUP_DIM, 0.0).astype(jnp.bfloat16)
    tri = (jnp.arange(tm)[:, None] < jnp.arange(tm)[None, :]).astype(jnp.bfloat16)

    def full(shape):
        return pl.BlockSpec(shape, lambda i: (0,) * len(shape))

    return pl.pallas_call(
        functools.partial(_mixer_kernel, steps_per_seq=seq_len // tm, n_steps=n_steps),
        grid=(n_steps + 1,),
        in_specs=[
            pl.BlockSpec((tm, D_MODEL), lambda i: (jnp.minimum(i, n_steps - 1), 0)),
            full((D_MODEL, IN_WIDTH)), full((1, IN_WIDTH)),
            full((N_KV_HEADS, 2, Q_REP * ATTN_BLOCK, 2 * ATTN_BLOCK)),
            full((1, GMLP_WIDTH)), full((1, GMLP_WIDTH)), full((GMLP_WIDTH, GMLP_WIDTH)),
            full((N_GMLP_GROUPS, ATTN_BLOCK, ATTN_BLOCK)), full((ATTN_BLOCK, GMLP_WIDTH)),
            full((D_MODEL, D_MODEL)), full((1, D_MODEL)), full((1, D_MODEL)), full((1, D_MODEL)),
            full((D_MODEL, ROUTER_LANES)), full((D_MODEL, ROUTER_LANES)), full((N_EXPERTS, 1)), full((tm, tm)),
        ],
        out_specs=[
            pl.BlockSpec((tm, D_MODEL), lambda i: (jnp.maximum(i - 1, 0), 0)),
            pl.BlockSpec((tm, D_MODEL // 2), lambda i: (jnp.maximum(i - 1, 0), 0)),
            pl.BlockSpec((META_ROWS, tm), lambda i: (0, jnp.maximum(i - 1, 0))),
            pl.BlockSpec((N_EXPERTS, 1), lambda i: (0, 0)),
        ],
        out_shape=[
            jax.ShapeDtypeStruct((t, D_MODEL), jnp.float32),
            jax.ShapeDtypeStruct((t, D_MODEL // 2), jnp.int32),
            jax.ShapeDtypeStruct((META_ROWS, t), jnp.float32),
            jax.ShapeDtypeStruct((N_EXPERTS, 1), jnp.float32),
        ],
        scratch_shapes=[pltpu.VMEM((ATTN_BLOCK, 2 * KV_WIDTH), jnp.bfloat16),
                        pltpu.VMEM((tm, D_MODEL), jnp.float32)],
        compiler_params=pltpu.CompilerParams(
            dimension_semantics=("arbitrary",), vmem_limit_bytes=VMEM_LIMIT_BYTES,
            allow_input_fusion=[k in (1, 9) for k in range(17)]),
        name="mixer",
    )(x2, w_in, b_in, _score_bias(sinks), lnv_g, lnv_b, grp_avg, w_sp, b_sp_full, w_out, b_out, ln1_g, ln1_b,
      w_r, w_r_hi, b_r, tri)


def _gather_rows(idx, src):
    n = idx.shape[0]
    width = src.shape[1]
    nbuf = GATHER_BUFFERS
    win = 2 * GATHER_WINDOW // nbuf
    sc = plsc.get_sparse_core_info()
    n_workers = sc.num_cores * sc.num_subcores
    per_worker = n // n_workers
    n_rounds = per_worker // (nbuf * win)
    assert n_rounds * nbuf * win * n_workers == n
    mesh = plsc.VectorSubcoreMesh(core_axis_name="core", subcore_axis_name="subcore")

    @functools.partial(
        pl.kernel, out_type=jax.ShapeDtypeStruct((n, width), src.dtype), mesh=mesh,
        scratch_types=[pltpu.VMEM((per_worker,), jnp.int32), pltpu.VMEM((nbuf, win, width), src.dtype),
                       pltpu.SemaphoreType.DMA((nbuf,)), pltpu.SemaphoreType.DMA((nbuf,))],
        name="gather_rows")
    def gather(src_hbm, idx_hbm, out_hbm, idx_v, rows_v, fetch_sem, store_sem):
        worker = lax.axis_index("subcore") * sc.num_cores + lax.axis_index("core")
        base = worker * per_worker
        pltpu.sync_copy(idx_hbm.at[pl.ds(base, per_worker)], idx_v)

        def fetch(chunk, buf):
            return pltpu.make_async_copy(src_hbm.at[idx_v.at[pl.ds(chunk * win, win)]], rows_v.at[buf],
                                         fetch_sem.at[buf])

        def store(chunk, buf):
            return pltpu.make_async_copy(rows_v.at[buf], out_hbm.at[pl.ds(base + chunk * win, win)],
                                         store_sem.at[buf])

        @pl.loop(0, n_rounds)
        def _(p):
            for buf in range(nbuf):
                @pl.when(p > 0)
                def _():
                    store(nbuf * (p - 1) + buf, buf).wait()
                fetch(nbuf * p + buf, buf).start()
            for buf in range(nbuf):
                fetch(nbuf * p + buf, buf).wait()
                store(nbuf * p + buf, buf).start()

        for buf in range(nbuf):
            store(nbuf * (n_rounds - 1) + buf, buf).wait()

    return gather(src, idx)


def _dispatch_rows(dest_t, src, n_rows):
    t, width = src.shape
    nbuf = DISPATCH_BUFFERS
    win = 2 * GATHER_WINDOW // nbuf
    sc = plsc.get_sparse_core_info()
    n_workers = sc.num_cores * sc.num_subcores
    per_worker = t // n_workers
    n_chunks = per_worker // win
    n_rounds = n_chunks // nbuf
    assert n_rounds * nbuf * win * n_workers == t
    idx = dest_t.reshape(TOP_K, n_workers, n_chunks, win).transpose(1, 0, 2, 3)
    idx = idx.reshape(n_workers, TOP_K * n_chunks, win)
    mesh = plsc.VectorSubcoreMesh(core_axis_name="core", subcore_axis_name="subcore")

    @functools.partial(
        pl.kernel, out_type=jax.ShapeDtypeStruct((n_rows, width), src.dtype), mesh=mesh,
        scratch_types=[pltpu.VMEM((TOP_K * n_chunks, win), jnp.int32), pltpu.VMEM((nbuf, win, width), src.dtype),
                       pltpu.SemaphoreType.DMA((nbuf,)), pltpu.SemaphoreType.DMA((nbuf,))],
        name="dispatch_rows")
    def dispatch(src_hbm, idx_hbm, out_hbm, idx_v, rows_v, fetch_sem, store_sem):
        worker = lax.axis_index("subcore") * sc.num_cores + lax.axis_index("core")
        base = worker * per_worker
        pltpu.sync_copy(idx_hbm.at[worker], idx_v)

        def fetch(chunk, buf):
            return pltpu.make_async_copy(src_hbm.at[pl.ds(base + chunk * win, win)], rows_v.at[buf],
                                         fetch_sem.at[buf])

        def store(chunk, k, buf):
            return pltpu.make_async_copy(rows_v.at[buf], out_hbm.at[idx_v.at[k * n_chunks + chunk]],
                                         store_sem.at[buf])

        @pl.loop(0, n_rounds)
        def _(p):
            for buf in range(nbuf):
                @pl.when(p > 0)
                def _():
                    for k in range(TOP_K):
                        store(nbuf * (p - 1) + buf, k, buf).wait()
                fetch(nbuf * p + buf, buf).start()
            for buf in range(nbuf):
                fetch(nbuf * p + buf, buf).wait()
                for k in range(TOP_K):
                    store(nbuf * p + buf, k, buf).start()

        for buf in range(nbuf):
            for k in range(TOP_K):
                store(nbuf * (n_rounds - 1) + buf, k, buf).wait()

    return dispatch(src, idx)


def _expert_kernel(block_e_ref, n_valid_ref, first_ref, next_e_ref, row_block_ref,
                   x_ref, wg_hbm, wu_hbm, wd_hbm, bias_ref,
                   y_ref, wg_f32, wu_f32, wd_f32, wg_bf, wu_bf, wd_bf, sem):
    i = pl.program_id(0)
    n_valid = n_valid_ref[i]
    staged = ((wg_hbm, wg_f32, wg_bf), (wu_hbm, wu_f32, wu_bf), (wd_hbm, wd_f32, wd_bf))

    def weight_copy(m, expert):
        return pltpu.make_async_copy(staged[m][0].at[expert], staged[m][1], sem.at[m])

    @pl.when(i == 0)
    def _():
        for m in range(3):
            weight_copy(m, block_e_ref[0]).start()

    @pl.when(first_ref[i] == 1)
    def _():
        for m in range(3):
            weight_copy(m, block_e_ref[i]).wait()
            staged[m][2][...] = staged[m][1][...].astype(jnp.bfloat16)

            @pl.when(next_e_ref[i] >= 0)
            def _():
                weight_copy(m, next_e_ref[i]).start()

    def expert_mlp(rows):
        row = lax.broadcasted_iota(jnp.int32, (rows, 1), 0)
        x_hi, x_lo = _unpack_bf16_pairs(jnp.where(row < n_valid, x_ref[:rows, :], 0))
        xb = jnp.concatenate([x_hi, x_lo], axis=1).astype(jnp.bfloat16)
        gt = jnp.minimum(jnp.dot(xb, wg_bf[...], preferred_element_type=jnp.float32) + bias_ref[0, 0:1], SWIGLU_LIMIT)
        up = jnp.clip(jnp.dot(xb, wu_bf[...], preferred_element_type=jnp.float32) + bias_ref[0, 1:2],
                      -SWIGLU_LIMIT, SWIGLU_LIMIT)
        hid = gt * jax.nn.sigmoid(SWIGLU_ALPHA * gt) * (up + 1.0)
        y = jnp.dot(hid.astype(jnp.bfloat16), wd_bf[...], preferred_element_type=jnp.float32) + bias_ref[0, 2:3]
        y_ref[:rows, :] = _pack_bf16_pairs(y.astype(jnp.bfloat16))
        if rows < y_ref.shape[0]:
            y_ref[rows:, :] = jnp.zeros((y_ref.shape[0] - rows, y_ref.shape[1]), y_ref.dtype)

    row_options = (0,) + EXPERT_PARTIAL_ROWS + (x_ref.shape[0],)
    for lo, hi in zip(row_options[:-1], row_options[1:]):
        pl.when(jnp.logical_and(n_valid > lo, n_valid <= hi))(functools.partial(expert_mlp, hi))


def _experts(block_e, n_valid, x_rows, w_gate, b_gate, w_up, b_up, w_down, b_down):
    n_rows = x_rows.shape[0]
    bm = EXPERT_ROWS
    n_blocks = n_rows // bm
    d_ff = w_gate.shape[2]

    blocks = jnp.arange(n_blocks, dtype=jnp.int32)
    last_used = jnp.sum((n_valid > 0).astype(jnp.int32)) - 1
    row_block = jnp.minimum(blocks, last_used)
    block_e = jnp.sum(jnp.where(blocks[None, :] == row_block[:, None], block_e[None, :], 0), axis=1)

    is_first = jnp.concatenate([jnp.ones((1,), jnp.int32), (block_e[1:] != block_e[:-1]).astype(jnp.int32)])
    next_e = jnp.min(jnp.where(block_e[None, :] > block_e[:, None], block_e[None, :], N_EXPERTS), axis=1)
    next_e = jnp.where(next_e == N_EXPERTS, -1, next_e)

    def by_expert(shape):
        return pl.BlockSpec(shape, lambda i, be, *_: (be[i],) + (0,) * (len(shape) - 1))

    hbm = pl.BlockSpec(memory_space=pl.ANY)
    assert d_ff == D_MODEL
    biases = jnp.zeros((N_EXPERTS, SUBLANES, D_MODEL), jnp.float32)
    biases = biases.at[:, 0].set(b_gate).at[:, 1].set(b_up).at[:, 2].set(b_down)
    return pl.pallas_call(
        _expert_kernel,
        grid_spec=pltpu.PrefetchScalarGridSpec(
            num_scalar_prefetch=5,
            grid=(n_blocks,),
            in_specs=[
                pl.BlockSpec((bm, D_MODEL // 2), lambda i, be, nv, fi, ne, rb: (rb[i], 0)),
                hbm, hbm, hbm, by_expert((1, SUBLANES, D_MODEL)),
            ],
            out_specs=pl.BlockSpec((bm, D_MODEL // 2), lambda i, be, nv, fi, ne, rb: (rb[i], 0)),
            scratch_shapes=[
                pltpu.VMEM((D_MODEL, d_ff), jnp.float32),
                pltpu.VMEM((D_MODEL, d_ff), jnp.float32),
                pltpu.VMEM((d_ff, D_MODEL), jnp.float32),
                pltpu.VMEM((D_MODEL, d_ff), jnp.bfloat16),
                pltpu.VMEM((D_MODEL, d_ff), jnp.bfloat16),
                pltpu.VMEM((d_ff, D_MODEL), jnp.bfloat16),
                pltpu.SemaphoreType.DMA((3,)),
            ],
        ),
        out_shape=jax.ShapeDtypeStruct((n_rows, D_MODEL // 2), jnp.int32),
        input_output_aliases={5: 0},
        compiler_params=pltpu.CompilerParams(
            dimension_semantics=("arbitrary",), vmem_limit_bytes=VMEM_LIMIT_BYTES),
        name="experts",
    )(block_e, n_valid, is_first, next_e, row_block, x_rows, w_gate, w_up, w_down, biases)


def _combine_kernel(h_ref, y_ref, meta_ref, g_ref, b_ref, o_ref):
    tm = h_ref.shape[0]
    meta_cols = jnp.concatenate([meta_ref[...], jnp.zeros((LANES - META_ROWS, tm), jnp.float32)], axis=0).T
    ffn_hi, ffn_lo = 0.0, 0.0
    for k in range(TOP_K):
        gate = meta_cols[:, 2 * TOP_K + k:2 * TOP_K + k + 1]
        y_hi, y_lo = _unpack_bf16_pairs(y_ref[k])
        ffn_hi = ffn_hi + gate * y_hi
        ffn_lo = ffn_lo + gate * y_lo
    ffn = jnp.concatenate([ffn_hi, ffn_lo], axis=1)
    o_ref[...] = _layer_norm(DEEPNORM_ALPHA * h_ref[...] + ffn, g_ref[...], b_ref[...])


def _combine(h, y_tok, meta, ln2_g, ln2_b):
    t = h.shape[0]
    tm = COMBINE_ROWS
    return pl.pallas_call(
        _combine_kernel,
        grid=(t // tm,),
        in_specs=[
            pl.BlockSpec((tm, D_MODEL), lambda i: (i, 0)),
            pl.BlockSpec((TOP_K, tm, D_MODEL // 2), lambda i: (0, i, 0)),
            pl.BlockSpec((META_ROWS, tm), lambda i: (0, i)),
            pl.BlockSpec((1, D_MODEL), lambda i: (0, 0)),
            pl.BlockSpec((1, D_MODEL), lambda i: (0, 0)),
        ],
        out_specs=pl.BlockSpec((tm, D_MODEL), lambda i: (i, 0)),
        out_shape=jax.ShapeDtypeStruct((t, D_MODEL), jnp.float32),
        input_output_aliases={0: 0},
        compiler_params=pltpu.CompilerParams(
            dimension_semantics=("arbitrary",), vmem_limit_bytes=VMEM_LIMIT_BYTES),
        name="combine",
    )(h, y_tok, meta, ln2_g, ln2_b)


def _layer(x2, seq_len, w_in, b_in, sinks, ln_v_g, ln_v_b, w_spatial, b_spatial, w_out, b_out,
           ln1_g, ln1_b, w_router, b_router, w_gate, b_gate, w_up, b_up, w_down, b_down, ln2_g, ln2_b):
    t = x2.shape[0]
    tk = t * TOP_K
    bm = EXPERT_ROWS
    bf16 = jnp.bfloat16

    w_r_hi = w_router.astype(bf16)
    w_r_lo = (w_router - w_r_hi.astype(jnp.float32)).astype(bf16)
    lane_pad = jnp.zeros((D_MODEL, ROUTER_LANES - 2 * N_EXPERTS), bf16)
    w_r = jnp.concatenate([w_r_hi, w_r_lo, lane_pad], axis=1)
    w_r_hi_only = jnp.concatenate([w_r_hi, jnp.zeros_like(w_r_lo), lane_pad], axis=1)
    b_sp_full = jnp.repeat(b_spatial.T, GMLP_GROUP_DIM, axis=1)

    h, h_packed, meta, counts = _mixer(
        x2, sinks, w_in.astype(bf16), b_in[None], ln_v_g[None], ln_v_b[None], w_spatial, b_sp_full,
        w_out.astype(bf16), b_out[None], ln1_g[None], ln1_b[None], w_r, w_r_hi_only, b_router[:, None], seq_len)

    counts = counts[:, 0].astype(jnp.int32)
    experts = jnp.arange(N_EXPERTS, dtype=jnp.int32)
    padded = (counts + bm - 1) // bm * bm
    padded_end = jnp.sum(jnp.where(experts[None, :] <= experts[:, None], padded[None, :], 0), axis=1)
    padded_start = padded_end - padded
    n_blocks = tk // bm + N_EXPERTS
    n_rows = n_blocks * bm
    top_idx_t = meta[:TOP_K].astype(jnp.int32)
    rank_t = meta[TOP_K:2 * TOP_K].astype(jnp.int32)
    dest_t = rank_t + jnp.sum(
        jnp.where(top_idx_t[None] == experts[:, None, None], padded_start[:, None, None], 0), axis=0)
    block_start = jnp.arange(n_blocks, dtype=jnp.int32) * bm
    block_e = jnp.minimum(
        jnp.sum((padded_end[None, :] <= block_start[:, None]).astype(jnp.int32), axis=1), N_EXPERTS - 1)
    valid_end = jnp.sum(jnp.where(block_e[:, None] == experts[None, :], (padded_start + counts)[None, :], 0), axis=1)
    n_valid = jnp.clip(valid_end - block_start, 0, bm)

    x_rows = _dispatch_rows(dest_t, h_packed, n_rows)
    y_rows = _experts(block_e, n_valid, x_rows, w_gate, b_gate, w_up, b_up, w_down, b_down)
    y_tok = _gather_rows(dest_t.reshape(-1), y_rows).reshape(TOP_K, t, D_MODEL // 2)
    return _combine(h, y_tok, meta, ln2_g[None], ln2_b[None])


def kernel(x, w_in, b_in, sinks, ln_v_g, ln_v_b, w_spatial, b_spatial, w_out, b_out, ln1_g, ln1_b,
           w_router, b_router, w_gate, b_gate, w_up, b_up, w_down, b_down, ln2_g, ln2_b):
    batch, seq_len, d = x.shape
    x2 = x.reshape(batch * seq_len, d)
    for l in range(DEPTH):
        x2 = _layer(x2, seq_len, w_in[l], b_in[l], sinks[l], ln_v_g[l], ln_v_b[l], w_spatial[l],
                    b_spatial[l], w_out[l], b_out[l], ln1_g[l], ln1_b[l], w_router[l], b_router[l],
                    w_gate[l], b_gate[l], w_up[l], b_up[l], w_down[l], b_down[l], ln2_g[l], ln2_b[l])
    return x2.reshape(batch, seq_len, d)
```

```python
import functools

import jax
import jax.numpy as jnp
from jax import lax
from jax.experimental import pallas as pl
from jax.experimental.pallas import tpu as pltpu
from jax.experimental.pallas import tpu_sc as plsc

D_MODEL = 1024
HEAD_DIM = 64
N_Q_HEADS = 8
N_KV_HEADS = 2
Q_REP = N_Q_HEADS // N_KV_HEADS
ATTN_WIDTH = N_Q_HEADS * HEAD_DIM
KV_WIDTH = N_KV_HEADS * HEAD_DIM
ATTN_BLOCK = 128
N_GMLP_GROUPS = 8
GMLP_WIDTH = D_MODEL - ATTN_WIDTH
GMLP_GROUP_DIM = GMLP_WIDTH // N_GMLP_GROUPS
IN_WIDTH = ATTN_WIDTH + 2 * KV_WIDTH + 2 * GMLP_WIDTH
N_EXPERTS = 32
TOP_K = 4
SWIGLU_LIMIT = 7.0
SWIGLU_ALPHA = 1.702
LN_EPS = 1e-5
DEPTH = 1
DEEPNORM_ALPHA = (2.0 * DEPTH) ** 0.25
NEG_INF = -1e30
LOG2_E = 1.4426950408889634

LANES = 128
SUBLANES = 8

MIXER_ROWS = 512
PROJ_CHUNK = 256
EXPERT_ROWS = 1024
EXPERT_PARTIAL_ROWS = (128, 256, 384, 512, 768)
GATHER_WINDOW = 64
GATHER_BUFFERS = 4
META_ROWS = 16
ROUTER_LANES = LANES
COMBINE_ROWS = 1024
VMEM_LIMIT_BYTES = 56 * 1024 * 1024

_O_K = ATTN_WIDTH
_O_V = _O_K + KV_WIDTH
_O_U = _O_V + KV_WIDTH
_O_G = _O_U + GMLP_WIDTH


def _pack_bf16_pairs(v):
    n = v.shape[1] // 2
    hi = lax.bitcast_convert_type(v[:, :n].astype(jnp.float32), jnp.int32)
    lo = lax.bitcast_convert_type(v[:, n:].astype(jnp.float32), jnp.int32)
    return hi | lax.shift_right_logical(lo, 16)


def _unpack_bf16_pairs(p):
    hi = lax.bitcast_convert_type(p & jnp.int32(-65536), jnp.float32)
    lo = lax.bitcast_convert_type(lax.shift_left(p, 16), jnp.float32)
    return hi, lo


def _layer_norm(v, g, b):
    mu = jnp.mean(v, axis=-1, keepdims=True)
    vc = v - mu
    var = jnp.mean(vc * vc, axis=-1, keepdims=True)
    return vc * lax.rsqrt(var + LN_EPS) * g + b


def _attention_block(q, kb, vb, bias_ref, bias_sel):
    outs = []
    for g in range(N_KV_HEADS):
        kg = kb[:, g * HEAD_DIM:(g + 1) * HEAD_DIM]
        vg = vb[:, g * HEAD_DIM:(g + 1) * HEAD_DIM]
        qg = jnp.concatenate(
            [q[:, (g * Q_REP + r) * HEAD_DIM:(g * Q_REP + r + 1) * HEAD_DIM] for r in range(Q_REP)],
            axis=0).astype(jnp.bfloat16)
        s = (lax.dot_general(qg, kg, (((1,), (1,)), ((), ())), preferred_element_type=jnp.float32)
             + bias_ref[g, bias_sel])
        p = jnp.exp2(s - jnp.max(s, axis=-1, keepdims=True))
        denom = jnp.sum(p, axis=-1, keepdims=True)
        o = jnp.dot(p.astype(jnp.bfloat16), vg, preferred_element_type=jnp.float32) / denom
        outs.extend(o[r * ATTN_BLOCK:(r + 1) * ATTN_BLOCK] for r in range(Q_REP))
    return jnp.concatenate(outs, axis=-1)


def _mixer_kernel(x_ref, w_in_ref, b_in_ref, bias_ref, lnv_g_ref, lnv_b_ref, grp_avg_ref,
                  w_sp_ref, b_sp_ref, w_out_ref, b_out_ref, ln1_g_ref, ln1_b_ref,
                  w_r_ref, w_r_hi_ref, b_r_ref, tri_ref,
                  h_ref, hp_ref, meta_ref, count_ref,
                  kv_prev_ref, hres_ref, *, steps_per_seq, n_steps):
    i = pl.program_id(0)
    first_step = (i % steps_per_seq) == 0
    tm = x_ref.shape[0]
    n_sub = tm // ATTN_BLOCK

    @pl.when(i == 0)
    def _():
        count_ref[...] = jnp.zeros_like(count_ref)

    @pl.when(first_step)
    def _():
        kv_prev_ref[...] = jnp.zeros_like(kv_prev_ref)

    def route_stages():
        h = _layer_norm(hres_ref[...], ln1_g_ref[...], ln1_b_ref[...])
        h_ref[...] = h
        yield
        h_hi = h.astype(jnp.bfloat16)
        hp_ref[...] = _pack_bf16_pairs(h_hi)
        h_lo = (h - h_hi.astype(jnp.float32)).astype(jnp.bfloat16)
        yield
        part = (jnp.dot(h_hi, w_r_ref[...], preferred_element_type=jnp.float32)
                + jnp.dot(h_lo, w_r_hi_ref[...], preferred_element_type=jnp.float32)).T
        logits = part[:N_EXPERTS] + part[N_EXPERTS:2 * N_EXPERTS] + b_r_ref[...]
        yield
        n_grp = N_EXPERTS // SUBLANES
        grp = [logits[SUBLANES * g:SUBLANES * (g + 1)] for g in range(n_grp)]
        sub = lax.broadcasted_iota(jnp.int32, (SUBLANES, tm), 0)
        beaten = [jnp.zeros((SUBLANES, tm), jnp.float32) for _ in range(n_grp)]
        for e2 in range(N_EXPERTS):
            g2, r2 = divmod(e2, SUBLANES)
            row = logits[e2:e2 + 1]
            for g in range(n_grp):
                if g > g2:
                    wins = jnp.where(row >= grp[g], 1.0, 0.0)
                elif g < g2:
                    wins = jnp.where(row > grp[g], 1.0, 0.0)
                else:
                    wins = jnp.where(sub > r2, jnp.where(row >= grp[g], 1.0, 0.0),
                                     jnp.where(row > grp[g], 1.0, 0.0))
                beaten[g] = beaten[g] + wins
            if e2 == N_EXPERTS // 2 - 1:
                yield
        place = jnp.concatenate(beaten, axis=0)
        expert_id = lax.broadcasted_iota(jnp.int32, (N_EXPERTS, tm), 0).astype(jnp.float32)
        onehot = jnp.where(place < TOP_K, 1.0, 0.0)
        yield
        before = (jnp.dot(onehot.astype(jnp.bfloat16), tri_ref[...], preferred_element_type=jnp.float32)
                  + count_ref[...])

        def pick(k, table):
            return jnp.sum(jnp.where(place == k, table, 0.0), axis=0, keepdims=True)

        vals = [pick(k, logits) for k in range(TOP_K)]
        exps = [jnp.exp(v - vals[0]) for v in vals]
        denom = exps[0] + exps[1] + exps[2] + exps[3]
        yield
        meta_ref[...] = jnp.zeros_like(meta_ref)
        for k in range(TOP_K):
            meta_ref[k:k + 1, :] = pick(k, expert_id)
            meta_ref[TOP_K + k:TOP_K + k + 1, :] = pick(k, before)
            meta_ref[2 * TOP_K + k:2 * TOP_K + k + 1, :] = exps[k] / denom
        count_ref[...] += jnp.sum(onehot, axis=1, keepdims=True)

    def mix_block(stages):
        x = x_ref[...]
        x_bf = x.astype(jnp.bfloat16)
        proj = []
        for c in range(IN_WIDTH // PROJ_CHUNK):
            cols = pl.ds(c * PROJ_CHUNK, PROJ_CHUNK)
            proj.append(jnp.dot(x_bf, w_in_ref[:, cols], preferred_element_type=jnp.float32) + b_in_ref[:, cols])
            next(stages, None)
        for _ in stages:
            pass

        def proj_cols(lo, hi):
            return jnp.concatenate(proj[lo // PROJ_CHUNK:hi // PROJ_CHUNK], axis=-1)

        q_all = proj_cols(0, _O_K) * (LOG2_E * HEAD_DIM ** -0.5)
        kv_all = proj_cols(_O_K, _O_U).astype(jnp.bfloat16)
        k_all, v_all = kv_all[:, :KV_WIDTH], kv_all[:, KV_WIDTH:]
        k_prev = kv_prev_ref[:, :KV_WIDTH]
        v_prev = kv_prev_ref[:, KV_WIDTH:]
        is_row0 = lax.broadcasted_iota(jnp.int32, (ATTN_BLOCK, KV_WIDTH), 0) == 0
        attn_blocks = []
        for sb in range(n_sub):
            rows = slice(sb * ATTN_BLOCK, (sb + 1) * ATTN_BLOCK)
            k_cur, v_cur = k_all[rows], v_all[rows]
            kb = jnp.concatenate([jnp.where(is_row0, 0, k_prev), k_cur], axis=0)
            vb = jnp.concatenate([jnp.where(is_row0, 0, v_prev), v_cur], axis=0)
            bias_sel = jnp.where(first_step, 1, 0) if sb == 0 else 0
            attn_blocks.append(_attention_block(q_all[rows], kb, vb, bias_ref, bias_sel))
            k_prev, v_prev = k_cur, v_cur
        kv_prev_ref[:, :KV_WIDTH] = k_prev
        kv_prev_ref[:, KV_WIDTH:] = v_prev
        attn = jnp.concatenate(attn_blocks, axis=0)

        u = jax.nn.gelu(proj_cols(_O_U, _O_G))
        gg = jax.nn.gelu(proj_cols(_O_G, IN_WIDTH))
        avg = grp_avg_ref[...]
        mu = jnp.dot(gg.astype(jnp.bfloat16), avg, preferred_element_type=jnp.float32)
        gc = gg - mu
        var = jnp.dot((gc * gc).astype(jnp.bfloat16), avg, preferred_element_type=jnp.float32)
        gn = (gc * lax.rsqrt(var + LN_EPS) * lnv_g_ref[...] + lnv_b_ref[...]).astype(jnp.bfloat16)
        causal = (lax.broadcasted_iota(jnp.int32, (ATTN_BLOCK, ATTN_BLOCK), 0)
                  >= lax.broadcasted_iota(jnp.int32, (ATTN_BLOCK, ATTN_BLOCK), 1))
        w_sp = [jnp.where(causal, w_sp_ref[g], 0.0).astype(jnp.bfloat16) for g in range(N_GMLP_GROUPS)]
        mixed_chunks = []
        for c in range(n_sub):
            rows = slice(c * ATTN_BLOCK, (c + 1) * ATTN_BLOCK)
            pieces = [
                jnp.dot(w_sp[g], gn[rows, g * GMLP_GROUP_DIM:(g + 1) * GMLP_GROUP_DIM],
                        preferred_element_type=jnp.float32)
                for g in range(N_GMLP_GROUPS)]
            mixed_chunks.append(jnp.concatenate(pieces, axis=-1) + b_sp_ref[...])
        sgu = u * jnp.concatenate(mixed_chunks, axis=0)

        mix = (jnp.dot(attn.astype(jnp.bfloat16), w_out_ref[:ATTN_WIDTH, :], preferred_element_type=jnp.float32)
               + jnp.dot(sgu.astype(jnp.bfloat16), w_out_ref[ATTN_WIDTH:, :], preferred_element_type=jnp.float32)
               + b_out_ref[...])
        hres_ref[...] = DEEPNORM_ALPHA * x + mix

    @pl.when(i == 0)
    def _():
        mix_block(iter(()))

    @pl.when(jnp.logical_and(i > 0, i < n_steps))
    def _():
        mix_block(route_stages())

    @pl.when(i == n_steps)
    def _():
        for _ in route_stages():
            pass


def _score_bias(sinks):
    t_idx = jnp.arange(Q_REP * ATTN_BLOCK)[:, None] % ATTN_BLOCK
    s_idx = jnp.arange(2 * ATTN_BLOCK)[None, :]
    diff = t_idx + ATTN_BLOCK - s_idx
    band = (diff >= 0) & (diff < ATTN_BLOCK)
    masks = jnp.stack([band, band & (s_idx >= ATTN_BLOCK)])
    bias = jnp.where(masks, 0.0, NEG_INF).astype(jnp.float32)
    sink_rows = jnp.repeat(sinks.reshape(N_KV_HEADS, Q_REP) * LOG2_E, ATTN_BLOCK, axis=1)
    return jnp.where(s_idx == 0, sink_rows[:, None, :, None], bias[None])


def _mixer(x2, sinks, w_in, b_in, lnv_g, lnv_b, w_sp, b_sp_full, w_out, b_out, ln1_g, ln1_b,
           w_r, w_r_hi, b_r, seq_len):
    t = x2.shape[0]
    tm = MIXER_ROWS
    n_steps = t // tm
    grp = jnp.arange(GMLP_WIDTH) // GMLP_GROUP_DIM
    grp_avg = jnp.where(grp[:, None] == grp[None, :], 1.0 / GMLP_GROUP_DIM, 0.0).astype(jnp.bfloat16)
    tri = (jnp.arange(tm)[:, None] < jnp.arange(tm)[None, :]).astype(jnp.bfloat16)

    def full(shape):
        return pl.BlockSpec(shape, lambda i: (0,) * len(shape))

    return pl.pallas_call(
        functools.partial(_mixer_kernel, steps_per_seq=seq_len // tm, n_steps=n_steps),
        grid=(n_steps + 1,),
        in_specs=[
            pl.BlockSpec((tm, D_MODEL), lambda i: (jnp.minimum(i, n_steps - 1), 0)),
            full((D_MODEL, IN_WIDTH)), full((1, IN_WIDTH)),
            full((N_KV_HEADS, 2, Q_REP * ATTN_BLOCK, 2 * ATTN_BLOCK)),
            full((1, GMLP_WIDTH)), full((1, GMLP_WIDTH)), full((GMLP_WIDTH, GMLP_WIDTH)),
            full((N_GMLP_GROUPS, ATTN_BLOCK, ATTN_BLOCK)), full((ATTN_BLOCK, GMLP_WIDTH)),
            full((D_MODEL, D_MODEL)), full((1, D_MODEL)), full((1, D_MODEL)), full((1, D_MODEL)),
            full((D_MODEL, ROUTER_LANES)), full((D_MODEL, ROUTER_LANES)), full((N_EXPERTS, 1)), full((tm, tm)),
        ],
        out_specs=[
            pl.BlockSpec((tm, D_MODEL), lambda i: (jnp.maximum(i - 1, 0), 0)),
            pl.BlockSpec((tm, D_MODEL // 2), lambda i: (jnp.maximum(i - 1, 0), 0)),
            pl.BlockSpec((META_ROWS, tm), lambda i: (0, jnp.maximum(i - 1, 0))),
            pl.BlockSpec((N_EXPERTS, 1), lambda i: (0, 0)),
        ],
        out_shape=[
            jax.ShapeDtypeStruct((t, D_MODEL), jnp.float32),
            jax.ShapeDtypeStruct((t, D_MODEL // 2), jnp.int32),
            jax.ShapeDtypeStruct((META_ROWS, t), jnp.float32),
            jax.ShapeDtypeStruct((N_EXPERTS, 1), jnp.float32),
        ],
        scratch_shapes=[pltpu.VMEM((ATTN_BLOCK, 2 * KV_WIDTH), jnp.bfloat16),
                        pltpu.VMEM((tm, D_MODEL), jnp.float32)],
        compiler_params=pltpu.CompilerParams(
            dimension_semantics=("arbitrary",), vmem_limit_bytes=VMEM_LIMIT_BYTES,
            allow_input_fusion=[k in (1, 9) for k in range(17)]),
        name="mixer",
    )(x2, w_in, b_in, _score_bias(sinks), lnv_g, lnv_b, grp_avg, w_sp, b_sp_full, w_out, b_out, ln1_g, ln1_b,
      w_r, w_r_hi, b_r, tri)


def _gather_rows(idx, src):
    n = idx.shape[0]
    width = src.shape[1]
    nbuf = GATHER_BUFFERS
    win = 2 * GATHER_WINDOW // nbuf
    sc = plsc.get_sparse_core_info()
    n_workers = sc.num_cores * sc.num_subcores
    per_worker = n // n_workers
    n_rounds = per_worker // (nbuf * win)
    assert n_rounds * nbuf * win * n_workers == n
    mesh = plsc.VectorSubcoreMesh(core_axis_name="core", subcore_axis_name="subcore")

    @functools.partial(
        pl.kernel, out_type=jax.ShapeDtypeStruct((n, width), src.dtype), mesh=mesh,
        scratch_types=[pltpu.VMEM((per_worker,), jnp.int32), pltpu.VMEM((nbuf, win, width), src.dtype),
                       pltpu.SemaphoreType.DMA((nbuf,)), pltpu.SemaphoreType.DMA((nbuf,))],
        name="gather_rows")
    def gather(src_hbm, idx_hbm, out_hbm, idx_v, rows_v, fetch_sem, store_sem):
        worker = lax.axis_index("subcore") * sc.num_cores + lax.axis_index("core")
        base = worker * per_worker
        pltpu.sync_copy(idx_hbm.at[pl.ds(base, per_worker)], idx_v)

        def fetch(chunk, buf):
            return pltpu.make_async_copy(src_hbm.at[idx_v.at[pl.ds(chunk * win, win)]], rows_v.at[buf],
                                         fetch_sem.at[buf])

        def store(chunk, buf):
            return pltpu.make_async_copy(rows_v.at[buf], out_hbm.at[pl.ds(base + chunk * win, win)],
                                         store_sem.at[buf])

        @pl.loop(0, n_rounds)
        def _(p):
            for buf in range(nbuf):
                @pl.when(p > 0)
                def _():
                    store(nbuf * (p - 1) + buf, buf).wait()
                fetch(nbuf * p + buf, buf).start()
            for buf in range(nbuf):
                fetch(nbuf * p + buf, buf).wait()
                store(nbuf * p + buf, buf).start()

        for buf in range(nbuf):
            store(nbuf * (n_rounds - 1) + buf, buf).wait()

    return gather(src, idx)


def _dispatch_rows(dest_t, src, n_rows):
    t, width = src.shape
    win = GATHER_WINDOW
    sc = plsc.get_sparse_core_info()
    n_workers = sc.num_cores * sc.num_subcores
    per_worker = t // n_workers
    n_chunks = per_worker // win
    n_pairs = n_chunks // 2
    assert n_pairs * 2 * win * n_workers == t
    idx = dest_t.reshape(TOP_K, n_workers, n_chunks, win).transpose(1, 0, 2, 3)
    idx = idx.reshape(n_workers, TOP_K * n_chunks, win)
    mesh = plsc.VectorSubcoreMesh(core_axis_name="core", subcore_axis_name="subcore")

    @functools.partial(
        pl.kernel, out_type=jax.ShapeDtypeStruct((n_rows, width), src.dtype), mesh=mesh,
        scratch_types=[pltpu.VMEM((TOP_K * n_chunks, win), jnp.int32), pltpu.VMEM((2, win, width), src.dtype),
                       pltpu.SemaphoreType.DMA((2,)), pltpu.SemaphoreType.DMA((2,))],
        name="dispatch_rows")
    def dispatch(src_hbm, idx_hbm, out_hbm, idx_v, rows_v, fetch_sem, store_sem):
        worker = lax.axis_index("subcore") * sc.num_cores + lax.axis_index("core")
        base = worker * per_worker
        pltpu.sync_copy(idx_hbm.at[worker], idx_v)

        def fetch(chunk, buf):
            return pltpu.make_async_copy(src_hbm.at[pl.ds(base + chunk * win, win)], rows_v.at[buf],
                                         fetch_sem.at[buf])

        def store(chunk, k, buf):
            return pltpu.make_async_copy(rows_v.at[buf], out_hbm.at[idx_v.at[k * n_chunks + chunk]],
                                         store_sem.at[buf])

        @pl.loop(0, n_pairs)
        def _(p):
            for buf in range(2):
                @pl.when(p > 0)
                def _():
                    for k in range(TOP_K):
                        store(2 * p - 2 + buf, k, buf).wait()
                fetch(2 * p + buf, buf).start()
            for buf in range(2):
                fetch(2 * p + buf, buf).wait()
                for k in range(TOP_K):
                    store(2 * p + buf, k, buf).start()

        for buf in range(2):
            for k in range(TOP_K):
                store(2 * n_pairs - 2 + buf, k, buf).wait()

    return dispatch(src, idx)


def _expert_kernel(block_e_ref, n_valid_ref, first_ref, next_e_ref, row_block_ref,
                   x_ref, wg_hbm, wu_hbm, wd_hbm, bias_ref,
                   y_ref, wg_f32, wu_f32, wd_f32, wg_bf, wu_bf, wd_bf, sem):
    i = pl.program_id(0)
    n_valid = n_valid_ref[i]
    staged = ((wg_hbm, wg_f32, wg_bf), (wu_hbm, wu_f32, wu_bf), (wd_hbm, wd_f32, wd_bf))

    def weight_copy(m, expert):
        return pltpu.make_async_copy(staged[m][0].at[expert], staged[m][1], sem.at[m])

    @pl.when(i == 0)
    def _():
        for m in range(3):
            weight_copy(m, block_e_ref[0]).start()

    @pl.when(first_ref[i] == 1)
    def _():
        for m in range(3):
            weight_copy(m, block_e_ref[i]).wait()
            staged[m][2][...] = staged[m][1][...].astype(jnp.bfloat16)

            @pl.when(next_e_ref[i] >= 0)
            def _():
                weight_copy(m, next_e_ref[i]).start(priority=1)

    def expert_mlp(rows):
        row = lax.broadcasted_iota(jnp.int32, (rows, 1), 0)
        x_hi, x_lo = _unpack_bf16_pairs(jnp.where(row < n_valid, x_ref[:rows, :], 0))
        xb = jnp.concatenate([x_hi, x_lo], axis=1).astype(jnp.bfloat16)
        gt = jnp.minimum(jnp.dot(xb, wg_bf[...], preferred_element_type=jnp.float32) + bias_ref[0, 0:1], SWIGLU_LIMIT)
        up = jnp.clip(jnp.dot(xb, wu_bf[...], preferred_element_type=jnp.float32) + bias_ref[0, 1:2],
                      -SWIGLU_LIMIT, SWIGLU_LIMIT)
        hid = gt * jax.nn.sigmoid(SWIGLU_ALPHA * gt) * (up + 1.0)
        y = jnp.dot(hid.astype(jnp.bfloat16), wd_bf[...], preferred_element_type=jnp.float32) + bias_ref[0, 2:3]
        y_ref[:rows, :] = _pack_bf16_pairs(y.astype(jnp.bfloat16))
        if rows < y_ref.shape[0]:
            y_ref[rows:, :] = jnp.zeros((y_ref.shape[0] - rows, y_ref.shape[1]), y_ref.dtype)

    row_options = (0,) + EXPERT_PARTIAL_ROWS + (x_ref.shape[0],)
    for lo, hi in zip(row_options[:-1], row_options[1:]):
        pl.when(jnp.logical_and(n_valid > lo, n_valid <= hi))(functools.partial(expert_mlp, hi))


def _experts(block_e, n_valid, x_rows, w_gate, b_gate, w_up, b_up, w_down, b_down):
    n_rows = x_rows.shape[0]
    bm = EXPERT_ROWS
    n_blocks = n_rows // bm
    d_ff = w_gate.shape[2]

    blocks = jnp.arange(n_blocks, dtype=jnp.int32)
    last_used = jnp.sum((n_valid > 0).astype(jnp.int32)) - 1
    row_block = jnp.minimum(blocks, last_used)
    block_e = jnp.sum(jnp.where(blocks[None, :] == row_block[:, None], block_e[None, :], 0), axis=1)

    is_first = jnp.concatenate([jnp.ones((1,), jnp.int32), (block_e[1:] != block_e[:-1]).astype(jnp.int32)])
    next_e = jnp.min(jnp.where(block_e[None, :] > block_e[:, None], block_e[None, :], N_EXPERTS), axis=1)
    next_e = jnp.where(next_e == N_EXPERTS, -1, next_e)

    def by_expert(shape):
        return pl.BlockSpec(shape, lambda i, be, *_: (be[i],) + (0,) * (len(shape) - 1))

    hbm = pl.BlockSpec(memory_space=pl.ANY)
    assert d_ff == D_MODEL
    biases = jnp.zeros((N_EXPERTS, SUBLANES, D_MODEL), jnp.float32)
    biases = biases.at[:, 0].set(b_gate).at[:, 1].set(b_up).at[:, 2].set(b_down)
    return pl.pallas_call(
        _expert_kernel,
        grid_spec=pltpu.PrefetchScalarGridSpec(
            num_scalar_prefetch=5,
            grid=(n_blocks,),
            in_specs=[
                pl.BlockSpec((bm, D_MODEL // 2), lambda i, be, nv, fi, ne, rb: (rb[i], 0)),
                hbm, hbm, hbm, by_expert((1, SUBLANES, D_MODEL)),
            ],
            out_specs=pl.BlockSpec((bm, D_MODEL // 2), lambda i, be, nv, fi, ne, rb: (rb[i], 0)),
            scratch_shapes=[
                pltpu.VMEM((D_MODEL, d_ff), jnp.float32),
                pltpu.VMEM((D_MODEL, d_ff), jnp.float32),
                pltpu.VMEM((d_ff, D_MODEL), jnp.float32),
                pltpu.VMEM((D_MODEL, d_ff), jnp.bfloat16),
                pltpu.VMEM((D_MODEL, d_ff), jnp.bfloat16),
                pltpu.VMEM((d_ff, D_MODEL), jnp.bfloat16),
                pltpu.SemaphoreType.DMA((3,)),
            ],
        ),
        out_shape=jax.ShapeDtypeStruct((n_rows, D_MODEL // 2), jnp.int32),
        input_output_aliases={5: 0},
        compiler_params=pltpu.CompilerParams(
            dimension_semantics=("arbitrary",), vmem_limit_bytes=VMEM_LIMIT_BYTES),
        name="experts",
    )(block_e, n_valid, is_first, next_e, row_block, x_rows, w_gate, w_up, w_down, biases)


def _combine_kernel(h_ref, y_ref, meta_ref, g_ref, b_ref, o_ref):
    tm = h_ref.shape[0]
    meta_cols = jnp.concatenate([meta_ref[...], jnp.zeros((LANES - META_ROWS, tm), jnp.float32)], axis=0).T
    ffn_hi, ffn_lo = 0.0, 0.0
    for k in range(TOP_K):
        gate = meta_cols[:, 2 * TOP_K + k:2 * TOP_K + k + 1]
        y_hi, y_lo = _unpack_bf16_pairs(y_ref[k])
        ffn_hi = ffn_hi + gate * y_hi
        ffn_lo = ffn_lo + gate * y_lo
    ffn = jnp.concatenate([ffn_hi, ffn_lo], axis=1)
    o_ref[...] = _layer_norm(DEEPNORM_ALPHA * h_ref[...] + ffn, g_ref[...], b_ref[...])


def _combine(h, y_tok, meta, ln2_g, ln2_b):
    t = h.shape[0]
    tm = COMBINE_ROWS
    return pl.pallas_call(
        _combine_kernel,
        grid=(t // tm,),
        in_specs=[
            pl.BlockSpec((tm, D_MODEL), lambda i: (i, 0)),
            pl.BlockSpec((TOP_K, tm, D_MODEL // 2), lambda i: (0, i, 0)),
            pl.BlockSpec((META_ROWS, tm), lambda i: (0, i)),
            pl.BlockSpec((1, D_MODEL), lambda i: (0, 0)),
            pl.BlockSpec((1, D_MODEL), lambda i: (0, 0)),
        ],
        out_specs=pl.BlockSpec((tm, D_MODEL), lambda i: (i, 0)),
        out_shape=jax.ShapeDtypeStruct((t, D_MODEL), jnp.float32),
        input_output_aliases={0: 0},
        compiler_params=pltpu.CompilerParams(
            dimension_semantics=("arbitrary",), vmem_limit_bytes=VMEM_LIMIT_BYTES),
        name="combine",
    )(h, y_tok, meta, ln2_g, ln2_b)


def _layer(x2, seq_len, w_in, b_in, sinks, ln_v_g, ln_v_b, w_spatial, b_spatial, w_out, b_out,
           ln1_g, ln1_b, w_router, b_router, w_gate, b_gate, w_up, b_up, w_down, b_down, ln2_g, ln2_b):
    t = x2.shape[0]
    tk = t * TOP_K
    bm = EXPERT_ROWS
    bf16 = jnp.bfloat16

    w_r_hi = w_router.astype(bf16)
    w_r_lo = (w_router - w_r_hi.astype(jnp.float32)).astype(bf16)
    lane_pad = jnp.zeros((D_MODEL, ROUTER_LANES - 2 * N_EXPERTS), bf16)
    w_r = jnp.concatenate([w_r_hi, w_r_lo, lane_pad], axis=1)
    w_r_hi_only = jnp.concatenate([w_r_hi, jnp.zeros_like(w_r_lo), lane_pad], axis=1)
    b_sp_full = jnp.repeat(b_spatial.T, GMLP_GROUP_DIM, axis=1)

    h, h_packed, meta, counts = _mixer(
        x2, sinks, w_in.astype(bf16), b_in[None], ln_v_g[None], ln_v_b[None], w_spatial, b_sp_full,
        w_out.astype(bf16), b_out[None], ln1_g[None], ln1_b[None], w_r, w_r_hi_only, b_router[:, None], seq_len)

    counts = counts[:, 0].astype(jnp.int32)
    experts = jnp.arange(N_EXPERTS, dtype=jnp.int32)
    padded = (counts + bm - 1) // bm * bm
    padded_end = jnp.sum(jnp.where(experts[None, :] <= experts[:, None], padded[None, :], 0), axis=1)
    padded_start = padded_end - padded
    n_blocks = tk // bm + N_EXPERTS
    n_rows = n_blocks * bm
    top_idx_t = meta[:TOP_K].astype(jnp.int32)
    rank_t = meta[TOP_K:2 * TOP_K].astype(jnp.int32)
    dest_t = rank_t + jnp.sum(
        jnp.where(top_idx_t[None] == experts[:, None, None], padded_start[:, None, None], 0), axis=0)
    block_start = jnp.arange(n_blocks, dtype=jnp.int32) * bm
    block_e = jnp.minimum(
        jnp.sum((padded_end[None, :] <= block_start[:, None]).astype(jnp.int32), axis=1), N_EXPERTS - 1)
    valid_end = jnp.sum(jnp.where(block_e[:, None] == experts[None, :], (padded_start + counts)[None, :], 0), axis=1)
    n_valid = jnp.clip(valid_end - block_start, 0, bm)

    x_rows = _dispatch_rows(dest_t, h_packed, n_rows)
    y_rows = _experts(block_e, n_valid, x_rows, w_gate, b_gate, w_up, b_up, w_down, b_down)
    y_tok = _gather_rows(dest_t.reshape(-1), y_rows).reshape(TOP_K, t, D_MODEL // 2)
    return _combine(h, y_tok, meta, ln2_g[None], ln2_b[None])


def kernel(x, w_in, b_in, sinks, ln_v_g, ln_v_b, w_spatial, b_spatial, w_out, b_out, ln1_g, ln1_b,
           w_router, b_router, w_gate, b_gate, w_up, b_up, w_down, b_down, ln2_g, ln2_b):
    batch, seq_len, d = x.shape
    x2 = x.reshape(batch * seq_len, d)
    for l in range(DEPTH):
        x2 = _layer(x2, seq_len, w_in[l], b_in[l], sinks[l], ln_v_g[l], ln_v_b[l], w_spatial[l],
                    b_spatial[l], w_out[l], b_out[l], ln1_g[l], ln1_b[l], w_router[l], b_router[l],
                    w_gate[l], b_gate[l], w_up[l], b_up[l], w_down[l], b_down[l], ln2_g[l], ln2_b[l])
    return x2.reshape(batch, seq_len, d)
```
